```python
import jax, jax.numpy as jnp
from jax import lax
import numpy as np

D_MODEL = 1024
BATCH = 8
SEQ = 2048
DEPTH = 1

HEAD_DIM = 64
N_FOX_HEADS = 8
DIL_GROUPS = ((128, 1), (512, 4), (2048, 16))
N_DIL_HEADS_PER_GROUP = 4
N_DIL_HEADS = N_DIL_HEADS_PER_GROUP * len(DIL_GROUPS)
FOX_W = N_FOX_HEADS * HEAD_DIM
DIL_W = N_DIL_HEADS * HEAD_DIM
DIL_OUT_W = N_DIL_HEADS_PER_GROUP * HEAD_DIM
ROT_DIM = HEAD_DIM // 4
ROPE_THETA = 500000.0
D_FF = -(-8 * D_MODEL // (3 * 256)) * 256
Q_BLOCK = 128
EPS = 1e-6
NEG = -1e30
SPLIT_SIZES = (FOX_W, FOX_W, FOX_W, N_FOX_HEADS, DIL_W, DIL_W, DIL_W, D_MODEL, D_MODEL)
IN_COLS = sum(SPLIT_SIZES)

kernel_name = "hybrid_fox_dilated_adaln_block"


def rmsnorm(x, g):
    xf = x.astype(jnp.float32)
    y = xf * lax.rsqrt(jnp.mean(xf * xf, axis=-1, keepdims=True) + EPS)
    return (y * g.astype(jnp.float32)).astype(x.dtype)


def modulate(h, shift, scale):
    return h * (1 + scale[:, None, :]) + shift[:, None, :]


def partial_rope(t):
    S = t.shape[1]
    pos = jnp.arange(S, dtype=jnp.float32)
    inv_freq = ROPE_THETA ** (-jnp.arange(0, ROT_DIM, 2, dtype=jnp.float32) / ROT_DIM)
    ang = pos[:, None] * inv_freq[None, :]
    cos = jnp.cos(ang)[None, :, None, :]
    sin = jnp.sin(ang)[None, :, None, :]
    tf = t.astype(jnp.float32)
    x1 = tf[..., : ROT_DIM // 2]
    x2 = tf[..., ROT_DIM // 2: ROT_DIM]
    rot = jnp.concatenate([x1 * cos - x2 * sin, x2 * cos + x1 * sin], axis=-1)
    return jnp.concatenate([rot, tf[..., ROT_DIM:]], axis=-1).astype(t.dtype)


def forgetting_attention(q, k, v, f_logit):
    B, S, H, Dh = q.shape
    scale = Dh ** -0.5
    F = jnp.cumsum(jax.nn.log_sigmoid(f_logit.astype(jnp.float32)), axis=1)
    Ft = jnp.transpose(F, (0, 2, 1))
    outs = []
    for blk in range(S // Q_BLOCK):
        q0, q1 = blk * Q_BLOCK, (blk + 1) * Q_BLOCK
        logits = jnp.einsum('bqhd,bkhd->bhqk', q[:, q0:q1], k[:, :q1],
                            preferred_element_type=jnp.float32) * scale
        logits = logits + (Ft[:, :, q0:q1, None] - Ft[:, :, None, :q1])
        causal = jnp.arange(q0, q1)[:, None] >= jnp.arange(q1)[None, :]
        p = jax.nn.softmax(jnp.where(causal[None, None], logits, NEG), axis=-1)
        outs.append(jnp.einsum('bhqk,bkhd->bqhd', p.astype(v.dtype), v[:, :q1]))
    return jnp.concatenate(outs, axis=1)


def dilated_window_attention(q, k, v, dilation, span):
    B, S, H, Dh = q.shape
    L = S // dilation
    nb = -(-L // span)
    Lp = nb * span
    Z = B * dilation
    scale = Dh ** -0.5

    def to_sub(t):
        t = t.reshape(B, L, dilation, H, Dh).transpose(0, 2, 1, 3, 4).reshape(Z, L, H, Dh)
        t = jnp.pad(t, ((0, 0), (0, Lp - L), (0, 0), (0, 0)))
        return t.reshape(Z, nb, span, H, Dh)

    qb, kb, vb = to_sub(q), to_sub(k), to_sub(v)

    def band(t):
        prev = jnp.pad(t, ((0, 0), (1, 0), (0, 0), (0, 0), (0, 0)))[:, :-1]
        return jnp.concatenate([prev, t], axis=2)

    kband, vband = band(kb), band(vb)
    logits = jnp.einsum('znqhd,znkhd->znhqk', qb, kband,
                        preferred_element_type=jnp.float32) * scale
    qi = jnp.arange(span)[:, None] + span
    kj = jnp.arange(2 * span)[None, :]
    dist = qi - kj
    in_band = (dist >= 0) & (dist <= span)
    has_prev = (jnp.arange(nb)[:, None, None] > 0) | (kj >= span)[None]
    valid = in_band[None] & has_prev
    logits = jnp.where(valid[None, :, None], logits, NEG)
    m = jnp.max(logits, axis=-1, keepdims=True)
    p = jnp.exp(logits - m)
    s = jnp.sum(p, axis=-1)
    o = jnp.einsum('znhqk,znkhd->znqhd', p.astype(v.dtype), vband).astype(jnp.float32)
    o = o / jnp.transpose(s, (0, 1, 3, 2))[..., None]
    lse = jnp.transpose(m[..., 0] + jnp.log(s), (0, 1, 3, 2))

    def from_sub(t):
        rest = t.shape[3:]
        t = t.reshape((Z, Lp) + rest)[:, :L]
        t = t.reshape((B, dilation, L) + rest)
        t = jnp.swapaxes(t, 1, 2)
        return t.reshape((B, S) + rest)

    return from_sub(o), from_sub(lse)


def hybrid_mixer(h, w_in, b_fgate, w_br_a, w_br_b, w_out):
    B, S, _ = h.shape
    proj = jnp.einsum('bsd,de->bse', h, w_in)
    splits = [int(i) for i in np.cumsum(SPLIT_SIZES)[:-1]]
    qa, ka, va, fa, qb, kb, vb, ga, gb = jnp.split(proj, splits, axis=-1)

    qa = qa.reshape(B, S, N_FOX_HEADS, HEAD_DIM)
    ka = ka.reshape(B, S, N_FOX_HEADS, HEAD_DIM)
    va = va.reshape(B, S, N_FOX_HEADS, HEAD_DIM)
    ya = forgetting_attention(qa, ka, va, fa + b_fgate)
    ya = jnp.einsum('bse,ed->bsd', ya.reshape(B, S, FOX_W), w_br_a)

    qb = partial_rope(qb.reshape(B, S, N_DIL_HEADS, HEAD_DIM))
    kb = partial_rope(kb.reshape(B, S, N_DIL_HEADS, HEAD_DIM))
    vb = vb.reshape(B, S, N_DIL_HEADS, HEAD_DIM)
    outs, lses = [], []
    for g, (window, dilation) in enumerate(DIL_GROUPS):
        sl = slice(g * N_DIL_HEADS_PER_GROUP, (g + 1) * N_DIL_HEADS_PER_GROUP)
        o, lse = dilated_window_attention(qb[:, :, sl], kb[:, :, sl], vb[:, :, sl],
                                          dilation, window // dilation)
        outs.append(o)
        lses.append(lse)
    alpha = jax.nn.softmax(jnp.stack(lses, axis=0), axis=0)
    yb = jnp.sum(alpha[..., None] * jnp.stack(outs, axis=0), axis=0).astype(h.dtype)
    yb = jnp.einsum('bse,ed->bsd', yb.reshape(B, S, DIL_OUT_W), w_br_b)

    merged = jax.nn.sigmoid(ga) * ya + jax.nn.sigmoid(gb) * yb
    return jnp.einsum('bsd,de->bse', merged, w_out)


def swiglu(h, w_gate, w_up, w_down):
    a = jnp.einsum('bsd,df->bsf', h, w_gate)
    u = jnp.einsum('bsd,df->bsf', h, w_up)
    return jnp.einsum('bsf,fd->bsd', jax.nn.silu(a) * u, w_down)


def _fwd_setup_inputs(seed: int = 0) -> dict:
    key = jax.random.key(seed)
    ks = jax.random.split(key, 16)
    f32 = jnp.float32
    L, D = DEPTH, D_MODEL
    nrm = lambda k, shape, fan_in, s=1.0: (jax.random.normal(k, shape, f32) * (s * fan_in ** -0.5))
    return {
        "x": jax.random.normal(ks[0], (BATCH, SEQ, D), f32),
        "c": jax.random.normal(ks[1], (BATCH, D), f32),
        "w_ada": nrm(ks[2], (L, D, 6 * D), D, 0.5),
        "b_ada": 0.1 * jax.random.normal(ks[3], (L, 6 * D), f32),
        "g_mix": 1.0 + 0.02 * jax.random.normal(ks[4], (L, D), f32),
        "w_in": nrm(ks[5], (L, D, IN_COLS), D),
        "b_fgate": jax.random.uniform(ks[6], (L, N_FOX_HEADS), f32, 1.0, 4.0),
        "w_br_a": nrm(ks[7], (L, FOX_W, D), FOX_W),
        "w_br_b": nrm(ks[8], (L, DIL_OUT_W, D), DIL_OUT_W),
        "w_out": nrm(ks[9], (L, D, D), D),
        "g_ffn": 1.0 + 0.02 * jax.random.normal(ks[10], (L, D), f32),
        "w_ffn_gate": nrm(ks[11], (L, D, D_FF), D),
        "w_ffn_up": nrm(ks[12], (L, D, D_FF), D),
        "w_ffn_down": nrm(ks[13], (L, D_FF, D), D_FF),
        "g_final": 1.0 + 0.02 * jax.random.normal(ks[14], (D,), f32),
    }


def _fwd_reference(x, c, w_ada, b_ada, g_mix, w_in, b_fgate, w_br_a, w_br_b, w_out,
              g_ffn, w_ffn_gate, w_ffn_up, w_ffn_down, g_final):
    for l in range(DEPTH):
        mod = jnp.einsum('bd,de->be', jax.nn.silu(c), w_ada[l]) + b_ada[l]
        sh_m, sc_m, ga_m, sh_f, sc_f, ga_f = jnp.split(mod, 6, axis=-1)
        h = modulate(rmsnorm(x, g_mix[l]), sh_m, sc_m)
        x = x + ga_m[:, None, :] * hybrid_mixer(h, w_in[l], b_fgate[l], w_br_a[l], w_br_b[l], w_out[l])
        h = modulate(rmsnorm(x, g_ffn[l]), sh_f, sc_f)
        x = x + ga_f[:, None, :] * swiglu(h, w_ffn_gate[l], w_ffn_up[l], w_ffn_down[l])
    return rmsnorm(x, g_final)


import jax as _jax
import jax.numpy as _jnp

TWIN_FORMAT = 'train_step'
FWD_PARAMS = ['x', 'c', 'w_ada', 'b_ada', 'g_mix', 'w_in', 'b_fgate', 'w_br_a', 'w_br_b', 'w_out', 'g_ffn', 'w_ffn_gate', 'w_ffn_up', 'w_ffn_down', 'g_final']
TWIN_WEIGHTS = ['w_ada', 'b_ada', 'g_mix', 'w_in', 'b_fgate', 'w_br_a', 'w_br_b', 'w_out', 'g_ffn', 'w_ffn_gate', 'w_ffn_up', 'w_ffn_down', 'g_final']
TWIN_DIFF_INPUT = 'x'
TWIN_INPUTS = ['x', 'c', 'w_ada', 'b_ada', 'g_mix', 'w_in', 'b_fgate', 'w_br_a', 'w_br_b', 'w_out', 'g_ffn', 'w_ffn_gate', 'w_ffn_up', 'w_ffn_down', 'g_final', 'loss_target', 'm_w_ada', 'm_b_ada', 'm_g_mix', 'm_w_in', 'm_b_fgate', 'm_w_br_a', 'm_w_br_b', 'm_w_out', 'm_g_ffn', 'm_w_ffn_gate', 'm_w_ffn_up', 'm_w_ffn_down', 'm_g_final', 'v_w_ada', 'v_b_ada', 'v_g_mix', 'v_w_in', 'v_b_fgate', 'v_w_br_a', 'v_w_br_b', 'v_w_out', 'v_g_ffn', 'v_w_ffn_gate', 'v_w_ffn_up', 'v_w_ffn_down', 'v_g_final']
TWIN_OUTPUTS = ['loss', 'grad_x', 'grad_w_ada', 'grad_b_ada', 'grad_g_mix', 'grad_w_in', 'grad_b_fgate', 'grad_w_br_a', 'grad_w_br_b', 'grad_w_out', 'grad_g_ffn', 'grad_w_ffn_gate', 'grad_w_ffn_up', 'grad_w_ffn_down', 'grad_g_final', 'delta_w_ada', 'delta_b_ada', 'delta_g_mix', 'delta_w_in', 'delta_b_fgate', 'delta_w_br_a', 'delta_w_br_b', 'delta_w_out', 'delta_g_ffn', 'delta_w_ffn_gate', 'delta_w_ffn_up', 'delta_w_ffn_down', 'delta_g_final', 'new_m_w_ada', 'new_m_b_ada', 'new_m_g_mix', 'new_m_w_in', 'new_m_b_fgate', 'new_m_w_br_a', 'new_m_w_br_b', 'new_m_w_out', 'new_m_g_ffn', 'new_m_w_ffn_gate', 'new_m_w_ffn_up', 'new_m_w_ffn_down', 'new_m_g_final', 'new_v_w_ada', 'new_v_b_ada', 'new_v_g_mix', 'new_v_w_in', 'new_v_b_fgate', 'new_v_w_br_a', 'new_v_w_br_b', 'new_v_w_out', 'new_v_g_ffn', 'new_v_w_ffn_gate', 'new_v_w_ffn_up', 'new_v_w_ffn_down', 'new_v_g_final']
TWIN_LEAF_KINDS = {'loss': 'loss', 'grad_x': 'grad_x', 'grad_w_ada': 'grad_w', 'grad_b_ada': 'grad_w', 'grad_g_mix': 'grad_w', 'grad_w_in': 'grad_w', 'grad_b_fgate': 'grad_w', 'grad_w_br_a': 'grad_w', 'grad_w_br_b': 'grad_w', 'grad_w_out': 'grad_w', 'grad_g_ffn': 'grad_w', 'grad_w_ffn_gate': 'grad_w', 'grad_w_ffn_up': 'grad_w', 'grad_w_ffn_down': 'grad_w', 'grad_g_final': 'grad_w', 'delta_w_ada': 'delta_w', 'delta_b_ada': 'delta_w', 'delta_g_mix': 'delta_w', 'delta_w_in': 'delta_w', 'delta_b_fgate': 'delta_w', 'delta_w_br_a': 'delta_w', 'delta_w_br_b': 'delta_w', 'delta_w_out': 'delta_w', 'delta_g_ffn': 'delta_w', 'delta_w_ffn_gate': 'delta_w', 'delta_w_ffn_up': 'delta_w', 'delta_w_ffn_down': 'delta_w', 'delta_g_final': 'delta_w', 'new_m_w_ada': 'new_m', 'new_m_b_ada': 'new_m', 'new_m_g_mix': 'new_m', 'new_m_w_in': 'new_m', 'new_m_b_fgate': 'new_m', 'new_m_w_br_a': 'new_m', 'new_m_w_br_b': 'new_m', 'new_m_w_out': 'new_m', 'new_m_g_ffn': 'new_m', 'new_m_w_ffn_gate': 'new_m', 'new_m_w_ffn_up': 'new_m', 'new_m_w_ffn_down': 'new_m', 'new_m_g_final': 'new_m', 'new_v_w_ada': 'new_v', 'new_v_b_ada': 'new_v', 'new_v_g_mix': 'new_v', 'new_v_w_in': 'new_v', 'new_v_b_fgate': 'new_v', 'new_v_w_br_a': 'new_v', 'new_v_w_br_b': 'new_v', 'new_v_w_out': 'new_v', 'new_v_g_ffn': 'new_v', 'new_v_w_ffn_gate': 'new_v', 'new_v_w_ffn_up': 'new_v', 'new_v_w_ffn_down': 'new_v', 'new_v_g_final': 'new_v'}


def _forward(args):
    return _fwd_reference(*[args[k] for k in FWD_PARAMS])


def _output_shape():
    out = _jax.eval_shape(lambda: _forward(_fwd_setup_inputs(0)))
    return out.shape, out.dtype

N_MICROBATCH = 1
ADAM_LR = 0.001
ADAM_B1 = 0.9
ADAM_B2 = 0.999
ADAM_EPS = 1e-08
ADAM_WD = 0.01
ADAM_STEP = 10
PER_EXAMPLE_BATCH_AXIS = {'x': 0, 'c': 0, 'loss_target': 0}
SHARED_INPUTS = []
_WEIGHT_DTYPES = {'w_ada': _jnp.float32, 'b_ada': _jnp.float32, 'g_mix': _jnp.float32, 'w_in': _jnp.float32, 'b_fgate': _jnp.float32, 'w_br_a': _jnp.float32, 'w_br_b': _jnp.float32, 'w_out': _jnp.float32, 'g_ffn': _jnp.float32, 'w_ffn_gate': _jnp.float32, 'w_ffn_up': _jnp.float32, 'w_ffn_down': _jnp.float32, 'g_final': _jnp.float32}
MOMENT_SCALE = {'w_ada': 3.009510e-02, 'b_ada': 5.046313e-02, 'g_mix': 1.756636e-02, 'w_in': 8.355044e-03, 'b_fgate': 4.646622e-02, 'w_br_a': 1.144999e-02, 'w_br_b': 7.405753e-03, 'w_out': 1.364011e-02, 'g_ffn': 4.234383e-02, 'w_ffn_gate': 1.782633e-02, 'w_ffn_up': 1.724305e-02, 'w_ffn_down': 2.852840e-02, 'g_final': 1.602044e+01}


def _to_microbatches(a, axis):
    t = _jnp.moveaxis(a, axis, 0)
    t = t.reshape((N_MICROBATCH, t.shape[0] // N_MICROBATCH) + t.shape[1:])
    return _jnp.moveaxis(t, 1, axis + 1)


def setup_inputs(seed: int = 0) -> dict:
    inp = _fwd_setup_inputs(seed)
    key = _jax.random.fold_in(_jax.random.key(seed), 7919)
    shape, _ = _output_shape()
    out = dict(inp)
    out["loss_target"] = _jax.random.normal(_jax.random.fold_in(key, 0), shape, _jnp.float32)
    for i, name in enumerate(TWIN_WEIGHTS):
        w = inp[name].astype(_jnp.float32)
        if MOMENT_SCALE is None:
            s = _jnp.sqrt(_jnp.mean(_jnp.square(w)) + 1e-30)
        else:
            s = MOMENT_SCALE[name]
        km, kv = _jax.random.split(_jax.random.fold_in(key, i + 1))
        out[name] = w
        out["m_" + name] = s * _jax.random.normal(km, w.shape, _jnp.float32)
        out["v_" + name] = (s * s) * _jax.random.uniform(kv, w.shape, _jnp.float32, 0.5, 1.5)
    if N_MICROBATCH > 1:
        for name, axis in PER_EXAMPLE_BATCH_AXIS.items():
            out[name] = _to_microbatches(out[name], axis)
    return {'x': out['x'], 'c': out['c'], 'w_ada': out['w_ada'], 'b_ada': out['b_ada'], 'g_mix': out['g_mix'], 'w_in': out['w_in'], 'b_fgate': out['b_fgate'], 'w_br_a': out['w_br_a'], 'w_br_b': out['w_br_b'], 'w_out': out['w_out'], 'g_ffn': out['g_ffn'], 'w_ffn_gate': out['w_ffn_gate'], 'w_ffn_up': out['w_ffn_up'], 'w_ffn_down': out['w_ffn_down'], 'g_final': out['g_final'], 'loss_target': out['loss_target'], 'm_w_ada': out['m_w_ada'], 'm_b_ada': out['m_b_ada'], 'm_g_mix': out['m_g_mix'], 'm_w_in': out['m_w_in'], 'm_b_fgate': out['m_b_fgate'], 'm_w_br_a': out['m_w_br_a'], 'm_w_br_b': out['m_w_br_b'], 'm_w_out': out['m_w_out'], 'm_g_ffn': out['m_g_ffn'], 'm_w_ffn_gate': out['m_w_ffn_gate'], 'm_w_ffn_up': out['m_w_ffn_up'], 'm_w_ffn_down': out['m_w_ffn_down'], 'm_g_final': out['m_g_final'], 'v_w_ada': out['v_w_ada'], 'v_b_ada': out['v_b_ada'], 'v_g_mix': out['v_g_mix'], 'v_w_in': out['v_w_in'], 'v_b_fgate': out['v_b_fgate'], 'v_w_br_a': out['v_w_br_a'], 'v_w_br_b': out['v_w_br_b'], 'v_w_out': out['v_w_out'], 'v_g_ffn': out['v_g_ffn'], 'v_w_ffn_gate': out['v_w_ffn_gate'], 'v_w_ffn_up': out['v_w_ffn_up'], 'v_w_ffn_down': out['v_w_ffn_down'], 'v_g_final': out['v_g_final']}


def _loss(weights, diff, rest, loss_target):
    with _jax.named_scope("forward"):
        args = {**rest, TWIN_DIFF_INPUT: diff, **{k: w.astype(_WEIGHT_DTYPES[k]) for k, w in weights.items()}}
        y = _forward(args)
    with _jax.named_scope("loss_head"):
        err = _jnp.square(y.astype(_jnp.float32) - loss_target)
        return 0.5 * _jnp.sum(_jnp.mean(err, axis=-1)) if err.ndim else 0.5 * err


def _adamw(w, g, m, v):
    m = ADAM_B1 * m + (1.0 - ADAM_B1) * g
    v = ADAM_B2 * v + (1.0 - ADAM_B2) * _jnp.square(g)
    m_hat = m / (1.0 - ADAM_B1 ** ADAM_STEP)
    v_hat = v / (1.0 - ADAM_B2 ** ADAM_STEP)
    delta = -ADAM_LR * (m_hat / (_jnp.sqrt(v_hat) + ADAM_EPS) + ADAM_WD * w)
    return delta, m, v


def reference(x, c, w_ada, b_ada, g_mix, w_in, b_fgate, w_br_a, w_br_b, w_out, g_ffn, w_ffn_gate, w_ffn_up, w_ffn_down, g_final, loss_target, m_w_ada, m_b_ada, m_g_mix, m_w_in, m_b_fgate, m_w_br_a, m_w_br_b, m_w_out, m_g_ffn, m_w_ffn_gate, m_w_ffn_up, m_w_ffn_down, m_g_final, v_w_ada, v_b_ada, v_g_mix, v_w_in, v_b_fgate, v_w_br_a, v_w_br_b, v_w_out, v_g_ffn, v_w_ffn_gate, v_w_ffn_up, v_w_ffn_down, v_g_final):
    given = dict(x=x, c=c, w_ada=w_ada, b_ada=b_ada, g_mix=g_mix, w_in=w_in, b_fgate=b_fgate, w_br_a=w_br_a, w_br_b=w_br_b, w_out=w_out, g_ffn=g_ffn, w_ffn_gate=w_ffn_gate, w_ffn_up=w_ffn_up, w_ffn_down=w_ffn_down, g_final=g_final, loss_target=loss_target, m_w_ada=m_w_ada, m_b_ada=m_b_ada, m_g_mix=m_g_mix, m_w_in=m_w_in, m_b_fgate=m_b_fgate, m_w_br_a=m_w_br_a, m_w_br_b=m_w_br_b, m_w_out=m_w_out, m_g_ffn=m_g_ffn, m_w_ffn_gate=m_w_ffn_gate, m_w_ffn_up=m_w_ffn_up, m_w_ffn_down=m_w_ffn_down, m_g_final=m_g_final, v_w_ada=v_w_ada, v_b_ada=v_b_ada, v_g_mix=v_g_mix, v_w_in=v_w_in, v_b_fgate=v_b_fgate, v_w_br_a=v_w_br_a, v_w_br_b=v_w_br_b, v_w_out=v_w_out, v_g_ffn=v_g_ffn, v_w_ffn_gate=v_w_ffn_gate, v_w_ffn_up=v_w_ffn_up, v_w_ffn_down=v_w_ffn_down, v_g_final=v_g_final)
    weights = {n: given[n] for n in TWIN_WEIGHTS}
    shared = {n: given[n] for n in SHARED_INPUTS}
    per_example = {n: given[n] for n in ['x', 'c']}
    grad_fn = _jax.value_and_grad(_loss, argnums=(0, 1))

    def one_microbatch(ex, loss_target):
        ex = dict(ex)
        diff = ex.pop(TWIN_DIFF_INPUT)
        return grad_fn(weights, diff, {**shared, **ex}, loss_target)

    if N_MICROBATCH == 1:
        loss, (grad_w, grad_x) = one_microbatch(per_example, given["loss_target"])
    else:
        def body(carry, xs):
            loss_sum, grad_sum = carry
            l_k, (gw_k, gx_k) = one_microbatch(xs[0], xs[1])
            with _jax.named_scope("update"):
                return (loss_sum + l_k, _jax.tree.map(_jnp.add, grad_sum, gw_k)), gx_k

        init = (_jnp.zeros((), _jnp.float32), _jax.tree.map(_jnp.zeros_like, weights))
        (loss, grad_w), grad_x = _jax.lax.scan(body, init, (per_example, given["loss_target"]))
    with _jax.named_scope("update"):
        delta_w, new_m, new_v = {}, {}, {}
        for n in TWIN_WEIGHTS:
            delta_w[n], new_m[n], new_v[n] = _adamw(weights[n], grad_w[n], given["m_" + n], given["v_" + n])
    return (loss, grad_x, *[grad_w[n] for n in TWIN_WEIGHTS], *[delta_w[n] for n in TWIN_WEIGHTS],
            *[new_m[n] for n in TWIN_WEIGHTS], *[new_v[n] for n in TWIN_WEIGHTS])
```

```python
import functools

import numpy as np
import jax
import jax.numpy as jnp
from jax import lax
from jax.experimental import pallas as pl
from jax.experimental.pallas import tpu as pltpu

F32, BF16 = jnp.float32, jnp.bfloat16
S, D = 2048, 1024
HD = 64
LANES = 128
N_FOX_PAIRS, N_DIL_PAIRS = 4, 2
DIL_GROUPS = ((1, 16), (4, 4), (16, 1))
SPAN = 128
ROT_DIM, ROPE_THETA = 16, 500000.0
D_FF, FF_SHARD, FF_PAD = 2816, 704, 768
FFP = 4 * FF_PAD
IN_COLS, IN_SHARD, IN_SHARD_PAD = 5896, 1474, 1536
LAY_B, LAY_A, LAY_F, LAY_G, LAY_N = 0, 2304, 3840, 4096, 6144
EPS, NEG = 1e-6, -1e30
SCALE = HD ** -0.5
ADAM_LR, ADAM_B1, ADAM_B2, ADAM_EPS, ADAM_WD, ADAM_STEP = 0.001, 0.9, 0.999, 1e-08, 0.01, 10
VMEM_MB = 56
MESH = pl.DeviceIdType.MESH


def _params(vmem_mb=None, **kw):
    if vmem_mb is not None:
        kw["vmem_limit_bytes"] = vmem_mb * 1024 * 1024
    return pltpu.CompilerParams(**kw)


def _sds(shape, dtype):
    return jax.ShapeDtypeStruct(shape, dtype)


def _sigmoid(x):
    return 1.0 / (1.0 + jnp.exp(-x))


def _colsum8(x):
    tm, n = x.shape
    return jnp.sum(x.reshape(tm // 8, 8, n), axis=0)


def norm_mod_fwd(x, g, mod, shift_row, scale_row):
    tm = 256

    def body(x_ref, g_ref, mod_ref, h_ref):
        xv = x_ref[...]
        r = lax.rsqrt(jnp.mean(xv * xv, axis=1, keepdims=True) + EPS)
        n = xv * r * g_ref[...]
        h = n * (1.0 + mod_ref[scale_row:scale_row + 1, :]) + mod_ref[shift_row:shift_row + 1, :]
        h_ref[...] = h.astype(BF16)

    return pl.pallas_call(
        body, name="norm_mod_fwd", grid=(S // tm,),
        in_specs=[pl.BlockSpec((tm, D), lambda i: (i, 0)), pl.BlockSpec((1, D), lambda i: (0, 0)),
                  pl.BlockSpec((8, D), lambda i: (0, 0))],
        out_specs=pl.BlockSpec((tm, D), lambda i: (i, 0)),
        out_shape=_sds((S, D), BF16),
    )(x, g, mod)


def rope_tables():
    pos = jnp.arange(S, dtype=F32)
    inv_freq = ROPE_THETA ** (-jnp.arange(0, ROT_DIM, 2, dtype=F32) / ROT_DIM)
    ang = pos[:, None] * inv_freq[None, :]
    cos, sin = jnp.cos(ang), jnp.sin(ang)
    one, zero = jnp.ones((S, HD - ROT_DIM), F32), jnp.zeros((S, HD - ROT_DIM), F32)
    z8 = jnp.zeros((S, 8), F32)
    c = jnp.concatenate([cos, cos, one], axis=1)
    s1 = jnp.concatenate([-sin, z8, zero], axis=1)
    s2 = jnp.concatenate([z8, sin, zero], axis=1)
    return tuple(jnp.concatenate([t, t], axis=1) for t in (c, s1, s2))


def _rope(y, c, s1, s2):
    return y * c + pltpu.roll(y, LANES - 8, 1) * s1 + pltpu.roll(y, 8, 1) * s2


def _rope_bwd(dy, c, s1, s2):
    return dy * c + pltpu.roll(dy * s1, 8, 1) + pltpu.roll(dy * s2, LANES - 8, 1)


def in_proj_fwd(h, w_lay, tabs):
    tm, tn = 512, 384
    n_rope = 2 * N_DIL_PAIRS * 3 // 2

    def body(a_ref, w_ref, c_ref, s1_ref, s2_ref, o_ref):
        j = pl.program_id(0)
        y = jnp.dot(a_ref[...], w_ref[...], preferred_element_type=F32)

        @pl.when(j < n_rope)
        def _():
            c, s1, s2 = c_ref[...], s1_ref[...], s2_ref[...]
            for t in range(2):
                o_ref[:, LANES * t:LANES * (t + 1)] = _rope(y[:, LANES * t:LANES * (t + 1)], c, s1, s2)
            o_ref[:, 2 * LANES:] = y[:, 2 * LANES:]

        @pl.when(j >= n_rope)
        def _():
            o_ref[...] = y

    tab = pl.BlockSpec((tm, LANES), lambda j, i: (i, 0))
    return pl.pallas_call(
        body, name="in_proj_fwd", grid=(LAY_N // tn, S // tm),
        in_specs=[pl.BlockSpec((tm, D), lambda j, i: (i, 0)), pl.BlockSpec((D, tn), lambda j, i: (0, j)), tab, tab, tab],
        out_specs=pl.BlockSpec((tm, tn), lambda j, i: (i, j)),
        out_shape=_sds((S, LAY_N), F32),
    )(h, w_lay, *tabs)


def _log1p_small(t):
    return jnp.where(t < 1e-2, t * (1.0 - t * (0.5 - t * (1.0 / 3.0))), jnp.log(1.0 + t))


def fgate_fwd(p, b_pad):
    def body(fa_ref, b_ref, frow_ref, fraw_ref, fcol_ref):
        f = fa_ref[...] + b_ref[...]
        fr = f.T[0:8, :]
        ls = jnp.minimum(fr, 0.0) - _log1p_small(jnp.exp(-jnp.abs(fr)))
        lane = lax.broadcasted_iota(jnp.int32, (8, S), 1)
        acc, sh = ls, 1
        while sh < S:
            acc = acc + jnp.where(lane >= sh, pltpu.roll(acc, sh, 1), 0.0)
            sh *= 2
        frow_ref[...] = acc
        fraw_ref[...] = fr
        for hh in range(8):
            fcol_ref[hh] = jnp.broadcast_to(acc[hh:hh + 1, :], (LANES, S)).T

    return pl.pallas_call(
        body, name="fgate_fwd", grid=(1,),
        in_specs=[pl.BlockSpec((S, LANES), lambda i: (0, LAY_F // LANES)), pl.BlockSpec((1, LANES), lambda i: (0, 0))],
        out_specs=[pl.BlockSpec((8, S), lambda i: (0, 0)), pl.BlockSpec((8, S), lambda i: (0, 0)),
                   pl.BlockSpec((8, S, LANES), lambda i: (0, 0, 0))],
        out_shape=[_sds((8, S), F32), _sds((8, S), F32), _sds((8, S, LANES), F32)],
        compiler_params=_params(VMEM_MB),
    )(p, b_pad)


TQ = TK = 256


def _head_masks(rows):
    lane = lax.broadcasted_iota(jnp.int32, (rows, LANES), 1)
    return lane < HD, lane >= HD


def fox_fwd(p, fcol, frow):
    def body(qkv_ref, fc_ref, fr_ref, o_ref, g_ref):
        masks = _head_masks(TQ)
        rowi = lax.broadcasted_iota(jnp.int32, (TQ, TK), 0)
        coli = lax.broadcasted_iota(jnp.int32, (TQ, TK), 1)

        def qloop(qi, _):
            q0 = pl.multiple_of(qi * TQ, TQ)
            q = qkv_ref[pl.ds(q0, TQ), 0:LANES]
            outs = []
            for hh in range(2):
                qm = (jnp.where(masks[hh], q, 0.0) * SCALE).astype(BF16)
                fc = fc_ref[hh, pl.ds(q0, TQ), :]
                fc2 = jnp.concatenate([fc, fc], axis=1)

                def kloop(kb, carry):
                    m, l, acc = carry
                    k0 = pl.multiple_of(kb * TK, TK)
                    k = qkv_ref[pl.ds(k0, TK), LANES:2 * LANES].astype(BF16)
                    v = qkv_ref[pl.ds(k0, TK), 2 * LANES:3 * LANES].astype(BF16)
                    s = lax.dot_general(qm, k, (((1,), (1,)), ((), ())), preferred_element_type=F32)
                    s = s + (fc2 - fr_ref[hh:hh + 1, pl.ds(k0, TK)])
                    s = jnp.where(coli + k0 <= rowi + q0, s, NEG)
                    m_new = jnp.maximum(m, jnp.max(s, axis=1, keepdims=True))
                    pr = jnp.exp(s - m_new)
                    alpha = jnp.exp(m - m_new)
                    l = l * alpha + jnp.sum(pr, axis=1, keepdims=True)
                    acc = acc * alpha + jnp.dot(pr.astype(BF16), v, preferred_element_type=F32)
                    return m_new, l, acc

                init = (jnp.full((TQ, 1), NEG, F32), jnp.zeros((TQ, 1), F32), jnp.zeros((TQ, LANES), F32))
                m, l, acc = lax.fori_loop(0, qi + 1, kloop, init)
                outs.append(acc / l)
                g_ref[hh, pl.ds(q0, TQ), :] = fc - (m + jnp.log(l))
            o_ref[pl.ds(q0, TQ), :] = jnp.where(masks[0], outs[0], outs[1]).astype(BF16)
            return 0

        lax.fori_loop(0, S // TQ, qloop, 0)

    a_blk = LAY_A // 384
    return pl.pallas_call(
        body, name="fox_fwd", grid=(N_FOX_PAIRS,),
        in_specs=[pl.BlockSpec((S, 384), lambda p_: (0, a_blk + p_)),
                  pl.BlockSpec((2, S, LANES), lambda p_: (p_, 0, 0)),
                  pl.BlockSpec((None, 2, S), lambda p_: (p_, 0, 0))],
        out_specs=[pl.BlockSpec((S, LANES), lambda p_: (0, p_)), pl.BlockSpec((2, S, LANES), lambda p_: (p_, 0, 0))],
        out_shape=[_sds((S, 4 * LANES), BF16), _sds((8, S, LANES), F32)],
        compiler_params=_params(VMEM_MB),
    )(p, fcol, frow.reshape(4, 2, S))


def _dil_rows(ref, start, d):
    return ref[pl.ds(start, SPAN), :] if d == 1 else ref[pl.ds(start, SPAN, stride=d), :]


def _dil_store(ref, start, d, val):
    if d == 1:
        ref[pl.ds(start, SPAN), :] = val
    else:
        ref[pl.ds(start, SPAN, stride=d), :] = val


def _band_mask(has_prev):
    qi = lax.broadcasted_iota(jnp.int32, (SPAN, 2 * SPAN), 0) + SPAN
    kj = lax.broadcasted_iota(jnp.int32, (SPAN, 2 * SPAN), 1)
    dist = qi - kj
    return (dist >= 0) & (dist <= SPAN) & (has_prev | (kj >= SPAN))


def _dil_block(n, d, nb):
    r, j = n // nb, n % nb
    start = r + d * SPAN * j
    prev = jnp.maximum(start - d * SPAN, r)
    return start, prev, j > 0


def dil_fwd(p):
    def body(*refs):
        qkv = [refs[3 * g:3 * g + 3] for g in range(3)]
        y_ref, lse_ref = refs[9], refs[10]
        acc_s, m_s, l_s = refs[11], refs[12], refs[13]
        masks = _head_masks(SPAN)
        for g, (d, nb) in enumerate(DIL_GROUPS):
            q_ref, k_ref, v_ref = qkv[g]

            def blk(n, _):
                start, prev, has_prev = _dil_block(n, d, nb)
                q = _dil_rows(q_ref, start, d)
                kc = jnp.concatenate([_dil_rows(k_ref, prev, d), _dil_rows(k_ref, start, d)], axis=0).astype(BF16)
                vc = jnp.concatenate([_dil_rows(v_ref, prev, d), _dil_rows(v_ref, start, d)], axis=0).astype(BF16)
                valid = _band_mask(has_prev)
                accs, ms, ls = [], [], []
                for hh in range(2):
                    qm = (jnp.where(masks[hh], q, 0.0) * SCALE).astype(BF16)
                    s = lax.dot_general(qm, kc, (((1,), (1,)), ((), ())), preferred_element_type=F32)
                    s = jnp.where(valid, s, NEG)
                    m = jnp.max(s, axis=1, keepdims=True)
                    pr = jnp.exp(s - m)
                    ls.append(jnp.sum(pr, axis=1, keepdims=True))
                    ms.append(m)
                    accs.append(jnp.dot(pr.astype(BF16), vc, preferred_element_type=F32))
                _dil_store(acc_s.at[g], start, d, jnp.where(masks[0], accs[0], accs[1]))
                _dil_store(m_s.at[g], start, d, jnp.where(masks[0], ms[0], ms[1]))
                _dil_store(l_s.at[g], start, d, jnp.where(masks[0], ls[0], ls[1]))
                return 0

            lax.fori_loop(0, 16, blk, 0)

        def merge(i, _):
            rows = pl.ds(pl.multiple_of(i * 256, 256), 256)
            m = [m_s[g, rows, :] for g in range(3)]
            mx = jnp.maximum(jnp.maximum(m[0], m[1]), m[2])
            w = [jnp.exp(m[g] - mx) for g in range(3)]
            l = sum(l_s[g, rows, :] * w[g] for g in range(3))
            y_ref[rows, :] = sum(acc_s[g, rows, :] * w[g] for g in range(3)) / l
            lse_ref[rows, :] = mx + jnp.log(l)
            return 0

        lax.fori_loop(0, S // 256, merge, 0)

    def spec(g, t):
        return pl.BlockSpec((S, LANES), lambda p_: (0, (p_ * 3 + g) * 3 + t))

    return pl.pallas_call(
        body, name="dil_fwd", grid=(N_DIL_PAIRS,),
        in_specs=[spec(g, t) for g in range(3) for t in range(3)],
        out_specs=[pl.BlockSpec((S, LANES), lambda p_: (0, p_)), pl.BlockSpec((S, LANES), lambda p_: (0, p_))],
        out_shape=[_sds((S, 2 * LANES), F32), _sds((S, 2 * LANES), F32)],
        scratch_shapes=[pltpu.VMEM((3, S, LANES), F32)] * 3,
        compiler_params=_params(VMEM_MB),
    )(*([p] * 9))


def merge_fwd(ya_att, yb, p, w_bra, w_brb):
    tm = 256
    gblk = LAY_G // D

    def body(a_ref, b_ref, ga_ref, gb_ref, wa_ref, wb_ref, mg_ref, ya_ref, yb_ref):
        ya = jnp.dot(a_ref[...], wa_ref[...], preferred_element_type=F32)
        ybp = jnp.dot(b_ref[...].astype(BF16), wb_ref[...], preferred_element_type=F32)
        mg_ref[...] = (_sigmoid(ga_ref[...]) * ya + _sigmoid(gb_ref[...]) * ybp).astype(BF16)
        ya_ref[...] = ya
        yb_ref[...] = ybp

    row = lambda w: pl.BlockSpec((tm, w), lambda i: (i, 0))
    return pl.pallas_call(
        body, name="merge_fwd", grid=(S // tm,),
        in_specs=[row(512), row(256), pl.BlockSpec((tm, D), lambda i: (i, gblk)), pl.BlockSpec((tm, D), lambda i: (i, gblk + 1)),
                  pl.BlockSpec((512, D), lambda i: (0, 0)), pl.BlockSpec((256, D), lambda i: (0, 0))],
        out_specs=[row(D), row(D), row(D)],
        out_shape=[_sds((S, D), BF16), _sds((S, D), F32), _sds((S, D), F32)],
    )(ya_att, yb, p, p, w_bra, w_brb)


def out_proj_fwd(merged, w_out, x, mod, g_ffn):
    tm = 256

    def body(a_ref, w_ref, x_ref, mod_ref, g_ref, mix_ref, x1_ref, h2_ref):
        mix = jnp.dot(a_ref[...], w_ref[...], preferred_element_type=F32)
        x1 = x_ref[...] + mod_ref[2:3, :] * mix
        r = lax.rsqrt(jnp.mean(x1 * x1, axis=1, keepdims=True) + EPS)
        h2 = (x1 * r * g_ref[...]) * (1.0 + mod_ref[4:5, :]) + mod_ref[3:4, :]
        mix_ref[...] = mix
        x1_ref[...] = x1
        h2_ref[...] = h2.astype(BF16)

    row = pl.BlockSpec((tm, D), lambda i: (i, 0))
    return pl.pallas_call(
        body, name="out_proj_fwd", grid=(S // tm,),
        in_specs=[row, pl.BlockSpec((D, D), lambda i: (0, 0)), row, pl.BlockSpec((8, D), lambda i: (0, 0)),
                  pl.BlockSpec((1, D), lambda i: (0, 0))],
        out_specs=[row, row, row],
        out_shape=[_sds((S, D), F32), _sds((S, D), F32), _sds((S, D), BF16)],
    )(merged, w_out, x, mod, g_ffn)


def ffn_up_fwd(h2, w_gate, w_up):
    tm = 512

    def body(h_ref, wg_ref, wu_ref, a_ref, u_ref, z_ref):
        h = h_ref[...]
        a = jnp.dot(h, wg_ref[...], preferred_element_type=F32)
        u = jnp.dot(h, wu_ref[...], preferred_element_type=F32)
        a_ref[...] = a
        u_ref[...] = u
        z_ref[...] = (a * _sigmoid(a) * u).astype(BF16)

    out = pl.BlockSpec((tm, FF_PAD), lambda k, i: (i, k))
    return pl.pallas_call(
        body, name="ffn_up_fwd", grid=(4, S // tm),
        in_specs=[pl.BlockSpec((tm, D), lambda k, i: (i, 0)), pl.BlockSpec((None, D, FF_PAD), lambda k, i: (k, 0, 0)),
                  pl.BlockSpec((None, D, FF_PAD), lambda k, i: (k, 0, 0))],
        out_specs=[out, out, out],
        out_shape=[_sds((S, FFP), F32), _sds((S, FFP), F32), _sds((S, FFP), BF16)],
    )(h2, w_gate, w_up)


def ffn_down_loss(z, w_down, x1, mod, g_final, tgt):
    tm = 256

    def body(z_ref, w_ref, x1_ref, mod_ref, g_ref, t_ref, dx2_ref, dffn_ref, dg_ref, dga_ref, loss_ref, s_dg, s_dga, s_loss):
        i = pl.program_id(0)

        @pl.when(i == 0)
        def _():
            s_dg[...] = jnp.zeros_like(s_dg)
            s_dga[...] = jnp.zeros_like(s_dga)
            s_loss[...] = jnp.zeros_like(s_loss)

        ffn = jnp.dot(z_ref[...], w_ref[...], preferred_element_type=F32)
        gaf = mod_ref[5:6, :]
        x2 = x1_ref[...] + gaf * ffn
        r = lax.rsqrt(jnp.mean(x2 * x2, axis=1, keepdims=True) + EPS)
        xh = x2 * r
        g = g_ref[...]
        e = xh * g - t_ref[...]
        s_loss[...] += 0.5 * jnp.sum(jnp.mean(e * e, axis=1, keepdims=True), axis=0, keepdims=True)
        dy = e * (1.0 / D)
        gdy = dy * g
        dx2 = r * (gdy - xh * jnp.mean(gdy * xh, axis=1, keepdims=True))
        s_dg[...] += _colsum8(dy * xh)
        s_dga[...] += _colsum8(dx2 * ffn)
        dx2_ref[...] = dx2
        dffn_ref[...] = (dx2 * gaf).astype(BF16)

        @pl.when(i == pl.num_programs(0) - 1)
        def _():
            dg_ref[...] = jnp.sum(s_dg[...], axis=0, keepdims=True)
            dga_ref[...] = jnp.sum(s_dga[...], axis=0, keepdims=True)
            loss_ref[...] = jnp.broadcast_to(s_loss[...], (1, LANES))

    row = pl.BlockSpec((tm, D), lambda i: (i, 0))
    vec = pl.BlockSpec((1, D), lambda i: (0, 0))
    return pl.pallas_call(
        body, name="ffn_down_loss", grid=(S // tm,),
        in_specs=[pl.BlockSpec((tm, FFP), lambda i: (i, 0)), pl.BlockSpec((FFP, D), lambda i: (0, 0)), row,
                  pl.BlockSpec((8, D), lambda i: (0, 0)), vec, row],
        out_specs=[row, row, vec, vec, pl.BlockSpec((1, LANES), lambda i: (0, 0))],
        out_shape=[_sds((S, D), F32), _sds((S, D), BF16), _sds((1, D), F32), _sds((1, D), F32), _sds((1, LANES), F32)],
        scratch_shapes=[pltpu.VMEM((8, D), F32), pltpu.VMEM((8, D), F32), pltpu.VMEM((1, 1), F32)],
        compiler_params=_params(VMEM_MB),
    )(z, w_down, x1, mod, g_final, tgt)


def ffn_down_bwd(dffn, w_down, a, u, z):
    tm, tn = 512, 384

    def body(d_ref, w_ref, a_ref, u_ref, z_ref, da_ref, du_ref, dw_ref):
        i = pl.program_id(1)
        dff = d_ref[...]
        dz = lax.dot_general(dff, w_ref[...], (((1,), (1,)), ((), ())), preferred_element_type=F32)
        av, uv = a_ref[...], u_ref[...]
        sg = _sigmoid(av)
        du_ref[...] = (dz * (av * sg)).astype(BF16)
        da_ref[...] = (dz * uv * (sg * (1.0 + av * (1.0 - sg)))).astype(BF16)
        dw = lax.dot_general(z_ref[...], dff, (((0,), (0,)), ((), ())), preferred_element_type=F32)

        @pl.when(i == 0)
        def _():
            dw_ref[...] = dw

        @pl.when(i > 0)
        def _():
            dw_ref[...] += dw

    tile = pl.BlockSpec((tm, tn), lambda j, i: (i, j))
    return pl.pallas_call(
        body, name="ffn_down_bwd", grid=(FFP // tn, S // tm),
        in_specs=[pl.BlockSpec((tm, D), lambda j, i: (i, 0)), pl.BlockSpec((tn, D), lambda j, i: (j, 0)), tile, tile, tile],
        out_specs=[tile, tile, pl.BlockSpec((tn, D), lambda j, i: (j, 0))],
        out_shape=[_sds((S, FFP), BF16), _sds((S, FFP), BF16), _sds((FFP, D), F32)],
    )(dffn, w_down, a, u, z)


def mm_nt(dy, w, name):
    tm = 512
    n = dy.shape[1]
    if w.ndim == 2:
        k_in, tk = w.shape[0], 768
        w_spec = pl.BlockSpec((k_in, tk), lambda i, k: (0, k))
    else:
        k_in, tk = w.shape[1], FF_PAD
        w_spec = pl.BlockSpec((None, k_in, tk), lambda i, k: (k, 0, 0))
    nk = n // tk

    def body(d_ref, w_ref, o_ref, acc):
        k = pl.program_id(1)
        part = lax.dot_general(d_ref[...], w_ref[...], (((1,), (1,)), ((), ())), preferred_element_type=F32)

        @pl.when(k == 0)
        def _():
            acc[...] = part

        @pl.when(k > 0)
        def _():
            acc[...] += part

        @pl.when(k == nk - 1)
        def _():
            o_ref[...] = acc[...]

    return pl.pallas_call(
        body, name=name, grid=(S // tm, nk),
        in_specs=[pl.BlockSpec((tm, tk), lambda i, k: (i, k)), w_spec],
        out_specs=pl.BlockSpec((tm, k_in), lambda i, k: (i, 0)),
        out_shape=_sds((S, k_in), F32),
        scratch_shapes=[pltpu.VMEM((tm, k_in), F32)],
    )(dy, w)


def mm_tn(h, dy, name, shard_major=False):
    tm, tn = 512, 768
    k_in, n = h.shape[1], dy.shape[1]

    def body(h_ref, d_ref, o_ref):
        i = pl.program_id(1)
        dw = lax.dot_general(h_ref[...], d_ref[...], (((0,), (0,)), ((), ())), preferred_element_type=F32)

        @pl.when(i == 0)
        def _():
            o_ref[...] = dw

        @pl.when(i > 0)
        def _():
            o_ref[...] += dw

    if shard_major:
        out_spec, out_shape = pl.BlockSpec((None, k_in, tn), lambda j, i: (j, 0, 0)), _sds((n // tn, k_in, tn), F32)
    else:
        out_spec, out_shape = pl.BlockSpec((k_in, tn), lambda j, i: (0, j)), _sds((k_in, n), F32)
    return pl.pallas_call(
        body, name=name, grid=(n // tn, S // tm),
        in_specs=[pl.BlockSpec((tm, k_in), lambda j, i: (i, 0)), pl.BlockSpec((tm, tn), lambda j, i: (i, j))],
        out_specs=out_spec, out_shape=out_shape,
    )(h, dy)


def mid_bwd(dh2a, dh2b, x1, dx2, mix, mod, g_ffn, p, ya, ybp, merged, ya_att, yb, w_out, w_bra, w_brb):
    tm = 128
    gblk = LAY_G // D
    nsteps = S // tm

    def body(dha_ref, dhb_ref, x1_ref, dx2_ref, mix_ref, mod_ref, g_ref, ga_ref, gb_ref, ya_ref, yb_ref, mg_ref,
             att_ref, ybb_ref, wo_ref, wa_ref, wb_ref,
             dx1_ref, dpg_ref, datt_ref, dyb_ref, cs_ref, dwo_ref, dwa_ref, dwb_ref, s_cs):
        i = pl.program_id(0)

        @pl.when(i == 0)
        def _():
            s_cs[...] = jnp.zeros_like(s_cs)
            dwo_ref[...] = jnp.zeros_like(dwo_ref)
            dwa_ref[...] = jnp.zeros_like(dwa_ref)
            dwb_ref[...] = jnp.zeros_like(dwb_ref)

        x1 = x1_ref[...]
        g = g_ref[...]
        r = lax.rsqrt(jnp.mean(x1 * x1, axis=1, keepdims=True) + EPS)
        xh = x1 * r
        dh2 = dha_ref[...] + dhb_ref[...]
        s_cs[0] += _colsum8(dh2)
        s_cs[1] += _colsum8(dh2 * (xh * g))
        dn2 = dh2 * (1.0 + mod_ref[4:5, :])
        s_cs[2] += _colsum8(dn2 * xh)
        gd = dn2 * g
        dx1 = dx2_ref[...] + r * (gd - xh * jnp.mean(gd * xh, axis=1, keepdims=True))
        s_cs[3] += _colsum8(dx1 * mix_ref[...])
        dx1_ref[...] = dx1
        dmix = (dx1 * mod_ref[2:3, :]).astype(BF16)
        dmg = lax.dot_general(dmix, wo_ref[...], (((1,), (1,)), ((), ())), preferred_element_type=F32)
        sga, sgb = _sigmoid(ga_ref[...]), _sigmoid(gb_ref[...])
        dya = (dmg * sga).astype(BF16)
        dybp = (dmg * sgb).astype(BF16)
        dpg_ref[:, 0:D] = (dmg * ya_ref[...] * (sga * (1.0 - sga))).astype(BF16)
        dpg_ref[:, D:2 * D] = (dmg * yb_ref[...] * (sgb * (1.0 - sgb))).astype(BF16)
        datt_ref[...] = lax.dot_general(dya, wa_ref[...], (((1,), (1,)), ((), ())), preferred_element_type=F32).astype(BF16)
        dyb_ref[...] = lax.dot_general(dybp, wb_ref[...], (((1,), (1,)), ((), ())), preferred_element_type=F32)
        tn_dims = (((0,), (0,)), ((), ()))
        dwo_ref[...] += lax.dot_general(mg_ref[...], dmix, tn_dims, preferred_element_type=F32)
        dwa_ref[...] += lax.dot_general(att_ref[...], dya, tn_dims, preferred_element_type=F32)
        dwb_ref[...] += lax.dot_general(ybb_ref[...].astype(BF16), dybp, tn_dims, preferred_element_type=F32)

        @pl.when(i == nsteps - 1)
        def _():
            for t in range(4):
                cs_ref[t:t + 1, :] = jnp.sum(s_cs[t], axis=0, keepdims=True)
            cs_ref[4:8, :] = jnp.zeros((4, D), F32)

    row = lambda w: pl.BlockSpec((tm, w), lambda i: (i, 0))
    full = lambda a, b: pl.BlockSpec((a, b), lambda i: (0, 0))
    return pl.pallas_call(
        body, name="mid_bwd", grid=(nsteps,),
        in_specs=[row(D), row(D), row(D), row(D), row(D), full(8, D), full(1, D),
                  pl.BlockSpec((tm, D), lambda i: (i, gblk)), pl.BlockSpec((tm, D), lambda i: (i, gblk + 1)),
                  row(D), row(D), row(D), row(512), row(256), full(D, D), full(512, D), full(256, D)],
        out_specs=[row(D), pl.BlockSpec((tm, 2 * D), lambda i: (i, LAY_G // (2 * D))), row(512), row(256), full(8, D),
                   full(D, D), full(512, D), full(256, D)],
        out_shape=[_sds((S, D), F32), _sds((S, LAY_N), BF16), _sds((S, 512), BF16), _sds((S, 256), F32), _sds((8, D), F32),
                   _sds((D, D), F32), _sds((512, D), F32), _sds((256, D), F32)],
        scratch_shapes=[pltpu.VMEM((4, 8, D), F32)],
        compiler_params=_params(VMEM_MB),
    )(dh2a, dh2b, x1, dx2, mix, mod, g_ffn, p, p, ya, ybp, merged, ya_att, yb, w_out, w_bra, w_brb)


def fox_bwd(p, do, o, gcol, frow, dp):
    nq = S // TQ

    def body(qkv_ref, do_ref, o_ref, g_ref, fr_ref, dp_in, dp_ref, df_ref, rs_ref, dq_s):
        del dp_in
        masks = _head_masks(TQ)
        lane = lax.broadcasted_iota(jnp.int32, (TQ, LANES), 1)
        head0 = 2 * pl.program_id(0)
        rowi = lax.broadcasted_iota(jnp.int32, (TQ, TK), 0)
        coli = lax.broadcasted_iota(jnp.int32, (TQ, TK), 1)
        dq_s[...] = jnp.zeros_like(dq_s)
        rs_ref[...] = jnp.zeros_like(rs_ref)
        nt = (((1,), (1,)), ((), ()))
        tn = (((0,), (0,)), ((), ()))

        def kloop(kb, _):
            k0 = pl.multiple_of(kb * TK, TK)
            k = qkv_ref[pl.ds(k0, TK), LANES:2 * LANES].astype(BF16)
            v = qkv_ref[pl.ds(k0, TK), 2 * LANES:3 * LANES].astype(BF16)
            fr = [fr_ref[hh:hh + 1, pl.ds(k0, TK)] for hh in range(2)]

            def qloop(qi, carry):
                dk, dv, df0, df1 = carry
                q0 = pl.multiple_of(qi * TQ, TQ)
                q = qkv_ref[pl.ds(q0, TQ), 0:LANES]
                dov = do_ref[pl.ds(q0, TQ), :].astype(F32)
                ov = o_ref[pl.ds(q0, TQ), :].astype(F32)
                causal = coli + k0 <= rowi + q0
                dqs, dfs = [], []
                rs = jnp.zeros((TQ, LANES), F32)
                for hh in range(2):
                    qm = (jnp.where(masks[hh], q, 0.0) * SCALE).astype(BF16)
                    dom = jnp.where(masks[hh], dov, 0.0)
                    delta = jnp.sum(dom * ov, axis=1, keepdims=True)
                    dob = dom.astype(BF16)
                    gc = g_ref[hh, pl.ds(q0, TQ), :]
                    gc2 = jnp.concatenate([gc, gc], axis=1)
                    s = lax.dot_general(qm, k, nt, preferred_element_type=F32) + (gc2 - fr[hh])
                    pr = jnp.where(causal, jnp.exp(jnp.where(causal, s, NEG)), 0.0)
                    dpr = lax.dot_general(dob, v, nt, preferred_element_type=F32)
                    ds = pr * (dpr - delta)
                    dsb = ds.astype(BF16)
                    dqs.append(jnp.dot(dsb, k, preferred_element_type=F32) * SCALE)
                    dk = dk + lax.dot_general(dsb, qm, tn, preferred_element_type=F32)
                    dv = dv + lax.dot_general(pr.astype(BF16), dob, tn, preferred_element_type=F32)
                    dfs.append(jnp.sum(ds, axis=0, keepdims=True))
                    rs = rs + jnp.where(lane == head0 + hh, jnp.sum(ds, axis=1, keepdims=True), 0.0)
                dq_s[pl.ds(q0, TQ), :] += jnp.where(masks[0], dqs[0], dqs[1])
                rs_ref[pl.ds(q0, TQ), :] += rs
                return dk, dv, df0 - dfs[0], df1 - dfs[1]

            z = jnp.zeros((TK, LANES), F32)
            z1 = jnp.zeros((1, TK), F32)
            dk, dv, df0, df1 = lax.fori_loop(kb, nq, qloop, (z, z, z1, z1))
            dp_ref[pl.ds(k0, TK), LANES:2 * LANES] = dk.astype(BF16)
            dp_ref[pl.ds(k0, TK), 2 * LANES:3 * LANES] = dv.astype(BF16)
            df_ref[0:1, pl.ds(k0, TK)] = df0
            df_ref[1:2, pl.ds(k0, TK)] = df1
            return 0

        lax.fori_loop(0, S // TK, kloop, 0)
        dp_ref[:, 0:LANES] = dq_s[...].astype(BF16)

    a_blk = LAY_A // 384
    pair = pl.BlockSpec((S, LANES), lambda p_: (0, p_))
    return pl.pallas_call(
        body, name="fox_bwd", grid=(N_FOX_PAIRS,),
        in_specs=[pl.BlockSpec((S, 384), lambda p_: (0, a_blk + p_)), pair, pair,
                  pl.BlockSpec((2, S, LANES), lambda p_: (p_, 0, 0)), pl.BlockSpec((None, 2, S), lambda p_: (p_, 0, 0)),
                  pl.BlockSpec(memory_space=pl.ANY)],
        out_specs=[pl.BlockSpec((S, 384), lambda p_: (0, a_blk + p_)), pl.BlockSpec((None, 2, S), lambda p_: (p_, 0, 0)),
                   pl.BlockSpec((None, S, LANES), lambda p_: (p_, 0, 0))],
        out_shape=[_sds((S, LAY_N), BF16), _sds((4, 2, S), F32), _sds((4, S, LANES), F32)],
        scratch_shapes=[pltpu.VMEM((S, LANES), F32)],
        input_output_aliases={5: 0},
        compiler_params=_params(VMEM_MB),
    )(p, do, o, gcol, frow.reshape(4, 2, S), dp)


def fgate_bwd(dfrow, dfcol, fraw, dp):
    def body(df_ref, dc_ref, f_ref, dp_in, dpf_ref, db_ref):
        del dp_in
        lane = lax.broadcasted_iota(jnp.int32, (8, S), 1)
        rsum = (dc_ref[0] + dc_ref[1]) + (dc_ref[2] + dc_ref[3])
        acc, sh = df_ref[...] + rsum.T[0:8, :], 1
        while sh < S:
            acc = acc + jnp.where(lane < S - sh, pltpu.roll(acc, S - sh, 1), 0.0)
            sh *= 2
        df = acc * _sigmoid(-f_ref[...])
        db_ref[...] = jnp.broadcast_to(jnp.sum(df, axis=1, keepdims=True), (8, LANES))
        dfc = jnp.concatenate([df, jnp.zeros((LANES - 8, S), F32)], axis=0).T
        dpf_ref[:, 0:LANES] = dfc.astype(BF16)
        dpf_ref[:, LANES:2 * LANES] = jnp.zeros((S, LANES), BF16)

    return pl.pallas_call(
        body, name="fgate_bwd", grid=(1,),
        in_specs=[pl.BlockSpec((8, S), lambda i: (0, 0)), pl.BlockSpec((4, S, LANES), lambda i: (0, 0, 0)),
                  pl.BlockSpec((8, S), lambda i: (0, 0)), pl.BlockSpec(memory_space=pl.ANY)],
        out_specs=[pl.BlockSpec((S, 2 * LANES), lambda i: (0, LAY_F // (2 * LANES))), pl.BlockSpec((8, LANES), lambda i: (0, 0))],
        out_shape=[_sds((S, LAY_N), BF16), _sds((8, LANES), F32)],
        input_output_aliases={3: 0},
        compiler_params=_params(VMEM_MB),
    )(dfrow, dfcol, fraw, dp)


def dil_bwd(p, dyb, yb, lse, tabs, dp):
    def body(*refs):
        qkv = [refs[3 * g:3 * g + 3] for g in range(3)]
        dy_ref, y_ref, lse_ref, c_ref, s1_ref, s2_ref = refs[9:15]
        dp_ref = refs[16]
        dq_s, dk_s, dv_s, dl_s = refs[17:21]
        masks = _head_masks(SPAN)
        m256 = _head_masks(256)
        nt = (((1,), (1,)), ((), ()))
        tn = (((0,), (0,)), ((), ()))
        dk_s[...] = jnp.zeros_like(dk_s)
        dv_s[...] = jnp.zeros_like(dv_s)

        def prep(i, _):
            rows = pl.ds(pl.multiple_of(i * 256, 256), 256)
            pr = dy_ref[rows, :] * y_ref[rows, :]
            d0 = jnp.sum(jnp.where(m256[0], pr, 0.0), axis=1, keepdims=True)
            d1 = jnp.sum(jnp.where(m256[1], pr, 0.0), axis=1, keepdims=True)
            dl_s[rows, :] = jnp.where(m256[0], d0, d1)
            return 0

        lax.fori_loop(0, S // 256, prep, 0)

        for g, (d, nb) in enumerate(DIL_GROUPS):
            q_ref, k_ref, v_ref = qkv[g]

            def blk(n, _):
                start, prev, has_prev = _dil_block(n, d, nb)
                q = _dil_rows(q_ref, start, d)
                kc = jnp.concatenate([_dil_rows(k_ref, prev, d), _dil_rows(k_ref, start, d)], axis=0).astype(BF16)
                vc = jnp.concatenate([_dil_rows(v_ref, prev, d), _dil_rows(v_ref, start, d)], axis=0).astype(BF16)
                dov = _dil_rows(dy_ref, start, d)
                lsev = _dil_rows(lse_ref, start, d)
                dlv = _dil_rows(dl_s, start, d)
                valid = _band_mask(has_prev)
                dqs = []
                dkc = jnp.zeros((2 * SPAN, LANES), F32)
                dvc = jnp.zeros((2 * SPAN, LANES), F32)
                for hh in range(2):
                    qm = (jnp.where(masks[hh], q, 0.0) * SCALE).astype(BF16)
                    dob = jnp.where(masks[hh], dov, 0.0).astype(BF16)
                    lse_h = jnp.max(jnp.where(masks[hh], lsev, NEG), axis=1, keepdims=True)
                    dl_h = jnp.max(jnp.where(masks[hh], dlv, NEG), axis=1, keepdims=True)
                    s = lax.dot_general(qm, kc, nt, preferred_element_type=F32)
                    pr = jnp.where(valid, jnp.exp(jnp.where(valid, s, NEG) - lse_h), 0.0)
                    dpr = lax.dot_general(dob, vc, nt, preferred_element_type=F32)
                    dsb = (pr * (dpr - dl_h)).astype(BF16)
                    dqs.append(jnp.dot(dsb, kc, preferred_element_type=F32) * SCALE)
                    dkc = dkc + lax.dot_general(dsb, qm, tn, preferred_element_type=F32)
                    dvc = dvc + lax.dot_general(pr.astype(BF16), dob, tn, preferred_element_type=F32)
                _dil_store(dq_s.at[g], start, d, jnp.where(masks[0], dqs[0], dqs[1]))
                for ref, val in ((dk_s.at[g], dkc), (dv_s.at[g], dvc)):
                    _dil_store(ref, prev, d, _dil_rows(ref, prev, d) + jnp.where(has_prev, val[0:SPAN], 0.0))
                    _dil_store(ref, start, d, _dil_rows(ref, start, d) + val[SPAN:])
                return 0

            lax.fori_loop(0, 16, blk, 0)

        def fin(i, _):
            rows = pl.ds(pl.multiple_of(i * 256, 256), 256)
            c, s1, s2 = c_ref[rows, :], s1_ref[rows, :], s2_ref[rows, :]
            for g in range(3):
                base = g * 384
                dp_ref[rows, base:base + LANES] = _rope_bwd(dq_s[g, rows, :], c, s1, s2).astype(BF16)
                dp_ref[rows, base + LANES:base + 2 * LANES] = _rope_bwd(dk_s[g, rows, :], c, s1, s2).astype(BF16)
                dp_ref[rows, base + 2 * LANES:base + 3 * LANES] = dv_s[g, rows, :].astype(BF16)
            return 0

        lax.fori_loop(0, S // 256, fin, 0)

    def spec(g, t):
        return pl.BlockSpec((S, LANES), lambda p_: (0, (p_ * 3 + g) * 3 + t))

    pair = pl.BlockSpec((S, LANES), lambda p_: (0, p_))
    tab = pl.BlockSpec((S, LANES), lambda p_: (0, 0))
    return pl.pallas_call(
        body, name="dil_bwd", grid=(N_DIL_PAIRS,),
        in_specs=[spec(g, t) for g in range(3) for t in range(3)] + [pair, pair, pair, tab, tab, tab, pl.BlockSpec(memory_space=pl.ANY)],
        out_specs=pl.BlockSpec((S, 1152), lambda p_: (0, p_)),
        out_shape=_sds((S, LAY_N), BF16),
        scratch_shapes=[pltpu.VMEM((3, S, LANES), F32)] * 3 + [pltpu.VMEM((S, LANES), F32)],
        input_output_aliases={15: 0},
        compiler_params=_params(VMEM_MB),
    )(*([p] * 9), dyb, yb, lse, *tabs, dp)


def in_bwd_tail(dh1, x, dx1, mod, g_mix):
    tm = 256
    nsteps = S // tm

    def body(dh_ref, x_ref, dx1_ref, mod_ref, g_ref, dx_ref, cs_ref, s_cs):
        i = pl.program_id(0)

        @pl.when(i == 0)
        def _():
            s_cs[...] = jnp.zeros_like(s_cs)

        xv, g, dh = x_ref[...], g_ref[...], dh_ref[...]
        r = lax.rsqrt(jnp.mean(xv * xv, axis=1, keepdims=True) + EPS)
        xh = xv * r
        s_cs[0] += _colsum8(dh)
        s_cs[1] += _colsum8(dh * (xh * g))
        dn = dh * (1.0 + mod_ref[1:2, :])
        s_cs[2] += _colsum8(dn * xh)
        gd = dn * g
        dx_ref[...] = dx1_ref[...] + r * (gd - xh * jnp.mean(gd * xh, axis=1, keepdims=True))

        @pl.when(i == nsteps - 1)
        def _():
            for t in range(3):
                cs_ref[t:t + 1, :] = jnp.sum(s_cs[t], axis=0, keepdims=True)
            cs_ref[3:8, :] = jnp.zeros((5, D), F32)

    row = pl.BlockSpec((tm, D), lambda i: (i, 0))
    return pl.pallas_call(
        body, name="in_bwd_tail", grid=(nsteps,),
        in_specs=[row, row, row, pl.BlockSpec((8, D), lambda i: (0, 0)), pl.BlockSpec((1, D), lambda i: (0, 0))],
        out_specs=[row, pl.BlockSpec((8, D), lambda i: (0, 0))],
        out_shape=[_sds((S, D), F32), _sds((8, D), F32)],
        scratch_shapes=[pltpu.VMEM((3, 8, D), F32)],
    )(dh1, x, dx1, mod, g_mix)


def local_fwd_bwd(x, tgt, mod, g_mix, g_ffn, g_final, b_fgate, w_lay, w_bra, w_brb, w_out, w_gate, w_up, w_down):
    tabs = rope_tables()
    b_pad = jnp.pad(b_fgate, ((0, 0), (0, LANES - 8)))
    h1 = norm_mod_fwd(x, g_mix, mod, 0, 1)
    p = in_proj_fwd(h1, w_lay, tabs)
    frow, fraw, fcol = fgate_fwd(p, b_pad)
    ya_att, gcol = fox_fwd(p, fcol, frow)
    yb, lse_b = dil_fwd(p)
    merged, ya, ybp = merge_fwd(ya_att, yb, p, w_bra, w_brb)
    mix, x1, h2 = out_proj_fwd(merged, w_out, x, mod, g_ffn)
    a, u, z = ffn_up_fwd(h2, w_gate, w_up)
    dx2, dffn, dg_final, dga_f, loss = ffn_down_loss(z, w_down, x1, mod, g_final, tgt)

    da, du, dw_down = ffn_down_bwd(dffn, w_down, a, u, z)
    dh2a = mm_nt(da, w_gate, "ffn_gate_dx")
    dh2b = mm_nt(du, w_up, "ffn_up_dx")
    dw_gate = mm_tn(h2, da, "ffn_gate_dw", shard_major=True)
    dw_up = mm_tn(h2, du, "ffn_up_dw", shard_major=True)
    dx1, dp1, dya_att, dyb, cs_mid, dw_out, dw_bra, dw_brb = mid_bwd(
        dh2a, dh2b, x1, dx2, mix, mod, g_ffn, p, ya, ybp, merged, ya_att, yb, w_out, w_bra, w_brb)
    dp2, dfrow, dfcol = fox_bwd(p, dya_att, ya_att, gcol, frow, dp1)
    dp3, db_fg = fgate_bwd(dfrow.reshape(8, S), dfcol, fraw, dp2)
    dp4 = dil_bwd(p, dyb, yb, lse_b, tabs, dp3)
    dh1 = mm_nt(dp4, w_lay, "in_proj_dx")
    dw_lay = mm_tn(h1, dp4, "in_proj_dw")
    dx, cs_in = in_bwd_tail(dh1, x, dx1, mod, g_mix)
    dmod = jnp.concatenate([cs_in[0:2], cs_mid[3:4], cs_mid[0:2], dga_f], axis=0)
    small = dict(dmod=dmod, dg_mix=cs_in[2:3], dg_ffn=cs_mid[2:3], dg_final=dg_final, db_fgate=db_fg[:, 0], loss=loss[0, 0])
    grads = dict(w_lay=dw_lay, w_bra=dw_bra, w_brb=dw_brb, w_out=dw_out, w_gate=dw_gate, w_up=dw_up, w_down=dw_down)
    return dx, grads, small


def _lay_pieces():
    out = []
    qa, ka, va, fa, qb, kb, vb, ga = 0, 512, 1024, 1536, 1544, 2312, 3080, 3848
    for p in range(N_DIL_PAIRS):
        for g in range(3):
            base = LAY_B + (p * 3 + g) * 384
            hd0 = (4 * g + 2 * p) * HD
            out += [(base, qb + hd0, LANES), (base + LANES, kb + hd0, LANES), (base + 2 * LANES, vb + hd0, LANES)]
    for p in range(N_FOX_PAIRS):
        base = LAY_A + p * 384
        out += [(base, qa + p * LANES, LANES), (base + LANES, ka + p * LANES, LANES), (base + 2 * LANES, va + p * LANES, LANES)]
    out.append((LAY_F, fa, 8))
    out.append((LAY_G, ga, 2 * D))
    return out


def lay_from_nat(w_nat):
    parts, pos = [], 0
    for lay, nat, width in sorted(_lay_pieces()):
        if lay > pos:
            parts.append(jnp.zeros((w_nat.shape[0], lay - pos), w_nat.dtype))
        parts.append(w_nat[:, nat:nat + width])
        pos = lay + width
    if pos < LAY_N:
        parts.append(jnp.zeros((w_nat.shape[0], LAY_N - pos), w_nat.dtype))
    return jnp.concatenate(parts, axis=1)


def nat_from_lay(w_lay):
    parts = [w_lay[:, lay:lay + width] for lay, nat, width in sorted(_lay_pieces(), key=lambda t: t[1])]
    return jnp.concatenate(parts, axis=1)


def _pos():
    return lax.axis_index("x"), lax.axis_index("y"), lax.axis_index("c")


def _other_chips(x, y):
    return [(1 - x, y), (x, 1 - y), (1 - x, 1 - y)]


def _remote(src, dst, send_sem, recv_sem, dev):
    return pltpu.make_async_remote_copy(src_ref=src, dst_ref=dst, send_sem=send_sem, recv_sem=recv_sem,
                                        device_id=dev, device_id_type=MESH)


VMEM_SPEC = pl.BlockSpec(memory_space=pltpu.VMEM)
ANY_SPEC = pl.BlockSpec(memory_space=pl.ANY)


def gather_all(v, name, with_sum):
    r = v.shape[0]

    def body(v_ref, out_ref, *rest):
        send_s, recv_s = rest[-2:]
        x, y, c = _pos()
        me = 4 * x + 2 * y + c
        out_ref[me] = v_ref[...]
        peers = []
        for m in range(1, 8):
            px = 1 - x if m & 4 else x
            py = 1 - y if m & 2 else y
            pc = 1 - c if m & 1 else c
            peers.append((px, py, pc))
        copies = [_remote(v_ref, out_ref.at[me], send_s.at[i], recv_s.at[i], dev) for i, dev in enumerate(peers)]
        for cp in copies:
            cp.start()
        for i, (px, py, pc) in enumerate(peers):
            _remote(v_ref, out_ref.at[4 * px + 2 * py + pc], send_s.at[i], recv_s.at[i], (px, py, pc)).wait_recv()
        for cp in copies:
            cp.wait_send()
        if with_sum:
            acc = out_ref[0]
            for b in range(1, 8):
                acc = acc + out_ref[b]
            rest[0][...] = acc

    out_shape = [_sds((8, r, LANES), F32)] + ([_sds((r, LANES), F32)] if with_sum else [])
    return pl.pallas_call(
        body, name=name, in_specs=[VMEM_SPEC], out_specs=[VMEM_SPEC] * len(out_shape), out_shape=out_shape,
        scratch_shapes=[pltpu.SemaphoreType.DMA((7,)), pltpu.SemaphoreType.DMA((7,))],
    )(v)


def mod_exchange(c_all, w_ada_sh, b_sh):
    def body(c_ref, w_ref, b_ref, out_ref, sc_ref, modp, send_s, recv_s):
        cv = c_ref[...]
        sc = cv * _sigmoid(cv)
        sc_ref[...] = sc
        modp[...] = jnp.dot(sc, w_ref[...], precision=lax.Precision.HIGHEST, preferred_element_type=F32) + b_ref[...]
        x, y, c = _pos()
        k = 2 * x + y
        out_ref[k] = modp[...]
        chips = _other_chips(x, y)
        copies = [_remote(modp, out_ref.at[k], send_s.at[j], recv_s.at[j], (cx, cy, c)) for j, (cx, cy) in enumerate(chips)]
        for cp in copies:
            cp.start()
        for j, (cx, cy) in enumerate(chips):
            _remote(modp, out_ref.at[2 * cx + cy], send_s.at[j], recv_s.at[j], (cx, cy, c)).wait_recv()
        for cp in copies:
            cp.wait_send()

    n = w_ada_sh.shape[1]
    return pl.pallas_call(
        body, name="mod_exchange", in_specs=[VMEM_SPEC] * 3, out_specs=[VMEM_SPEC] * 2,
        out_shape=[_sds((4, 8, n), F32), _sds((8, D), F32)],
        scratch_shapes=[pltpu.VMEM((8, n), F32), pltpu.SemaphoreType.DMA((3,)), pltpu.SemaphoreType.DMA((3,))],
        compiler_params=_params(VMEM_MB),
    )(c_all, w_ada_sh, b_sh)


def gather_weights(shards):
    n = len(shards)

    def body(*refs):
        ins, outs = refs[:n], refs[n:2 * n]
        send_s, recv_s, fsend_s, frecv_s, loc_s = refs[2 * n:]
        x, y, c = _pos()
        k = 2 * x + y
        chips = _other_chips(x, y)
        local, sends, fwds = [], [], []
        for a in range(n):
            half = ins[a].shape[0] // 2
            rows = pl.ds(c * half, half)
            lc = pltpu.make_async_copy(ins[a], outs[a].at[k], loc_s.at[a])
            lc.start()
            local.append(lc)
            for j, (cx, cy) in enumerate(chips):
                cp = _remote(ins[a].at[rows], outs[a].at[k, rows], send_s.at[3 * a + j], recv_s.at[3 * a + j], (cx, cy, c))
                cp.start()
                sends.append(cp)
        for a in range(n):
            half = ins[a].shape[0] // 2
            rows = pl.ds(c * half, half)
            for j, (cx, cy) in enumerate(chips):
                kj = 2 * cx + cy
                _remote(ins[a].at[rows], outs[a].at[kj, rows], send_s.at[3 * a + j], recv_s.at[3 * a + j], (cx, cy, c)).wait_recv()
                fw = _remote(outs[a].at[kj, rows], outs[a].at[kj, rows], fsend_s.at[3 * a + j], frecv_s.at[3 * a + j], (x, y, 1 - c))
                fw.start()
                fwds.append(fw)
        for a in range(n):
            half = ins[a].shape[0] // 2
            orows = pl.ds((1 - c) * half, half)
            for j, (cx, cy) in enumerate(chips):
                kj = 2 * cx + cy
                _remote(outs[a].at[kj, orows], outs[a].at[kj, orows], fsend_s.at[3 * a + j], frecv_s.at[3 * a + j], (x, y, 1 - c)).wait_recv()
        for cp in sends + fwds:
            cp.wait_send()
        for lc in local:
            lc.wait()

    return pl.pallas_call(
        body, name="gather_weights", in_specs=[ANY_SPEC] * n, out_specs=[ANY_SPEC] * n,
        out_shape=[_sds((4,) + s.shape, s.dtype) for s in shards],
        scratch_shapes=[pltpu.SemaphoreType.DMA((3 * n,))] * 4 + [pltpu.SemaphoreType.DMA((n,))],
    )(*shards)


def _row_tile(rows, cap=256):
    t = cap
    while rows % t or t % 8:
        t -= 8
    return t


def rs_sibling_exchange(grads):
    n = len(grads)

    def body(*refs):
        ins, outs = refs[:n], refs[n:2 * n]
        send_s, recv_s = refs[2 * n:]
        x, y, c = _pos()
        copies = []
        for a in range(n):
            half = ins[a].shape[1] // 2
            cp = _remote(ins[a].at[:, pl.ds((1 - c) * half, half), :], outs[a], send_s.at[a], recv_s.at[a], (x, y, 1 - c))
            cp.start()
            copies.append(cp)
        for cp in copies:
            cp.wait_recv()
        for cp in copies:
            cp.wait_send()

    return pl.pallas_call(
        body, name="rs_sibling_exchange", in_specs=[ANY_SPEC] * n, out_specs=[ANY_SPEC] * n,
        out_shape=[_sds((4, g.shape[1] // 2, g.shape[2]), g.dtype) for g in grads],
        scratch_shapes=[pltpu.SemaphoreType.DMA((n,)), pltpu.SemaphoreType.DMA((n,))],
    )(*grads)


def rs_add_halves(g, other, core, name):
    _, r, cdim = g.shape
    half = r // 2
    tr = _row_tile(half, 128)
    nb = half // tr

    def body(core_ref, g_ref, o_ref, out_ref):
        del core_ref
        out_ref[...] = g_ref[...] + o_ref[...]

    grid_spec = pltpu.PrefetchScalarGridSpec(
        num_scalar_prefetch=1, grid=(4, nb),
        in_specs=[pl.BlockSpec((None, tr, cdim), lambda k, i, cr: (k, cr[0] * nb + i, 0)),
                  pl.BlockSpec((None, tr, cdim), lambda k, i, cr: (k, i, 0))],
        out_specs=pl.BlockSpec((None, tr, cdim), lambda k, i, cr: (k, i, 0)))
    return pl.pallas_call(body, name=name, grid_spec=grid_spec, out_shape=_sds((4, half, cdim), F32))(core, g, other)


def rs_chip_exchange(pres):
    n = len(pres)

    def body(*refs):
        ins, outs = refs[:n], refs[n:2 * n]
        send_s, recv_s, loc_s = refs[2 * n:]
        x, y, c = _pos()
        k = 2 * x + y
        chips = _other_chips(x, y)
        copies, local = [], []
        for a in range(n):
            lc = pltpu.make_async_copy(ins[a].at[k], outs[a].at[3], loc_s.at[a])
            lc.start()
            local.append(lc)
            for j, (cx, cy) in enumerate(chips):
                cp = _remote(ins[a].at[2 * cx + cy], outs[a].at[j], send_s.at[3 * a + j], recv_s.at[3 * a + j], (cx, cy, c))
                cp.start()
                copies.append(cp)
        for cp in copies:
            cp.wait_recv()
        for cp in copies:
            cp.wait_send()
        for lc in local:
            lc.wait()

    return pl.pallas_call(
        body, name="rs_chip_exchange", in_specs=[ANY_SPEC] * n, out_specs=[ANY_SPEC] * n,
        out_shape=[_sds(p_.shape, p_.dtype) for p_ in pres],
        scratch_shapes=[pltpu.SemaphoreType.DMA((3 * n,)), pltpu.SemaphoreType.DMA((3 * n,)), pltpu.SemaphoreType.DMA((n,))],
    )(*pres)


def rs_add_slabs(t, name):
    _, half, cdim = t.shape
    tr = _row_tile(half, 128)

    def body(t_ref, out_ref):
        out_ref[...] = ((t_ref[3] + t_ref[0]) + t_ref[1]) + t_ref[2]

    return pl.pallas_call(
        body, name=name, grid=(half // tr,),
        in_specs=[pl.BlockSpec((4, tr, cdim), lambda i: (0, i, 0))],
        out_specs=pl.BlockSpec((tr, cdim), lambda i: (i, 0)),
        out_shape=_sds((half, cdim), F32),
    )(t)


def rs_sibling_share(reds):
    n = len(reds)

    def body(*refs):
        ins, outs = refs[:n], refs[n:2 * n]
        send_s, recv_s, loc_s = refs[2 * n:]
        x, y, c = _pos()
        copies, local = [], []
        for a in range(n):
            half = ins[a].shape[0]
            rows = pl.ds(c * half, half)
            lc = pltpu.make_async_copy(ins[a], outs[a].at[rows], loc_s.at[a])
            lc.start()
            local.append(lc)
            cp = _remote(ins[a], outs[a].at[rows], send_s.at[a], recv_s.at[a], (x, y, 1 - c))
            cp.start()
            copies.append(cp)
        for a, cp in enumerate(copies):
            half = ins[a].shape[0]
            _remote(ins[a], outs[a].at[pl.ds((1 - c) * half, half)], send_s.at[a], recv_s.at[a], (x, y, 1 - c)).wait_recv()
        for cp in copies:
            cp.wait_send()
        for lc in local:
            lc.wait()

    return pl.pallas_call(
        body, name="rs_sibling_share", in_specs=[ANY_SPEC] * n, out_specs=[ANY_SPEC] * n,
        out_shape=[_sds((2 * r_.shape[0], r_.shape[1]), r_.dtype) for r_ in reds],
        scratch_shapes=[pltpu.SemaphoreType.DMA((n,)), pltpu.SemaphoreType.DMA((n,)), pltpu.SemaphoreType.DMA((n,))],
    )(*reds)


def reduce_scatter(grads, core):
    others = rs_sibling_exchange(grads)
    pres = [rs_add_halves(g, o, core, f"rs_add_halves_{a}") for a, (g, o) in enumerate(zip(grads, others))]
    slabs = rs_chip_exchange(pres)
    reds = [rs_add_slabs(t, f"rs_add_slabs_{a}") for a, t in enumerate(slabs)]
    return rs_sibling_share(reds)


def _adam_math(w, g, m, v):
    m = ADAM_B1 * m + (1.0 - ADAM_B1) * g
    v = ADAM_B2 * v + (1.0 - ADAM_B2) * (g * g)
    m_hat = m / (1.0 - ADAM_B1 ** ADAM_STEP)
    v_hat = v / (1.0 - ADAM_B2 ** ADAM_STEP)
    delta = -ADAM_LR * (m_hat / (jnp.sqrt(v_hat) + ADAM_EPS) + ADAM_WD * w)
    return delta, m, v


def adam(w, g, m, v, name):
    r, cdim = w.shape
    tr = _row_tile(r) if r >= 8 else r

    def body(w_ref, g_ref, m_ref, v_ref, d_ref, nm_ref, nv_ref):
        d_ref[...], nm_ref[...], nv_ref[...] = _adam_math(w_ref[...], g_ref[...], m_ref[...], v_ref[...])

    blk = pl.BlockSpec((tr, cdim), lambda i: (i, 0))
    return pl.pallas_call(
        body, name=name, grid=(r // tr,), in_specs=[blk] * 4, out_specs=[blk] * 3,
        out_shape=[_sds((r, cdim), F32)] * 3,
    )(w, g, m, v)


def adam_w_ada(sc_t, dmod_sh, w, m, v):
    r, cdim = w.shape
    tr = 256

    def body(s_ref, d_ref, w_ref, m_ref, v_ref, g_ref, dl_ref, nm_ref, nv_ref):
        g = jnp.dot(s_ref[...], d_ref[...], precision=lax.Precision.HIGHEST, preferred_element_type=F32)
        g_ref[...] = g
        dl_ref[...], nm_ref[...], nv_ref[...] = _adam_math(w_ref[...], g, m_ref[...], v_ref[...])

    blk = pl.BlockSpec((tr, cdim), lambda i: (i, 0))
    return pl.pallas_call(
        body, name="adam_w_ada", grid=(r // tr,),
        in_specs=[pl.BlockSpec((tr, LANES), lambda i: (i, 0)), pl.BlockSpec((LANES, cdim), lambda i: (0, 0)), blk, blk, blk],
        out_specs=[blk] * 4, out_shape=[_sds((r, cdim), F32)] * 4,
    )(sc_t, dmod_sh, w, m, v)


SMALL_ROWS = 80


def kernel(x, c, w_ada, b_ada, g_mix, w_in, b_fgate, w_br_a, w_br_b, w_out, g_ffn, w_ffn_gate, w_ffn_up, w_ffn_down, g_final, loss_target, m_w_ada, m_b_ada, m_g_mix, m_w_in, m_b_fgate, m_w_br_a, m_w_br_b, m_w_out, m_g_ffn, m_w_ffn_gate, m_w_ffn_up, m_w_ffn_down, m_g_final, v_w_ada, v_b_ada, v_g_mix, v_w_in, v_b_fgate, v_w_br_a, v_w_br_b, v_w_out, v_g_ffn, v_w_ffn_gate, v_w_ffn_up, v_w_ffn_down, v_g_final):
    xi, yi, ci = _pos()
    chip = 2 * xi + yi
    seq = 4 * xi + 2 * yi + ci
    n_ada = w_ada.shape[2]

    c_all = gather_all(c.reshape(8, LANES), "gather_c", False)[0].reshape(8, D)
    b_sh = lax.dynamic_slice(b_ada, (0, chip * n_ada), (1, n_ada))
    mod_all, sc = mod_exchange(c_all, w_ada[0], b_sh)
    mod = lax.dynamic_index_in_dim(mod_all, seq, axis=1, keepdims=False).reshape(6, D)
    mod8 = jnp.pad(mod, ((0, 2), (0, 0)))

    shards = [
        jnp.pad(w_in[0], ((0, 0), (0, IN_SHARD_PAD - IN_SHARD))), w_br_a[0], w_br_b[0], w_out[0],
        jnp.pad(w_ffn_gate[0], ((0, 0), (0, FF_PAD - FF_SHARD))), jnp.pad(w_ffn_up[0], ((0, 0), (0, FF_PAD - FF_SHARD))),
        jnp.pad(w_ffn_down[0], ((0, FF_PAD - FF_SHARD), (0, 0))),
    ]
    g_in, g_bra, g_brb, g_out, w_gate, w_up, g_down = gather_weights([s.astype(BF16) for s in shards])
    w_lay = lay_from_nat(jnp.concatenate([g_in[k][:, :IN_SHARD] for k in range(4)], axis=1))
    w_bra = g_bra.transpose(1, 0, 2).reshape(512, D)
    w_brb = g_brb.transpose(1, 0, 2).reshape(256, D)

    dx, grads, small = local_fwd_bwd(x[0], loss_target[0], mod8, g_mix, g_ffn, g_final.reshape(1, D), b_fgate,
                                     w_lay, w_bra, w_brb, g_out.reshape(D, D), w_gate, w_up, g_down.reshape(FFP, D))

    dw_in = nat_from_lay(grads["w_lay"]).reshape(D, 4, IN_SHARD).transpose(1, 0, 2)
    per_chip = [
        jnp.pad(dw_in, ((0, 0), (0, 0), (0, IN_SHARD_PAD - IN_SHARD))),
        grads["w_bra"].reshape(512, 4, 256).transpose(1, 0, 2),
        grads["w_brb"].reshape(256, 4, 256).transpose(1, 0, 2),
        grads["w_out"].reshape(4, 256, D),
        grads["w_gate"], grads["w_up"], grads["w_down"].reshape(4, FF_PAD, D),
    ]
    core = ci.astype(jnp.int32).reshape(1)
    r_in, r_bra, r_brb, r_out, r_gate, r_up, r_down = reduce_scatter(per_chip, core)
    gw = dict(w_in=r_in[:, :IN_SHARD], w_br_a=r_bra, w_br_b=r_brb, w_out=r_out, w_ffn_gate=r_gate[:, :FF_SHARD],
              w_ffn_up=r_up[:, :FF_SHARD], w_ffn_down=r_down[:FF_SHARD])

    sv = jnp.concatenate([
        small["dmod"].reshape(48, LANES), small["dg_mix"].reshape(8, LANES), small["dg_ffn"].reshape(8, LANES),
        small["dg_final"].reshape(8, LANES), jnp.pad(small["db_fgate"], (0, LANES - 8)).reshape(1, LANES),
        jnp.broadcast_to(small["loss"], (1, LANES)), jnp.zeros((SMALL_ROWS - 74, LANES), F32)], axis=0)
    sv_all, sv_sum = gather_all(sv, "gather_small", True)
    loss = sv_sum[73, 0]
    g_small = dict(b_ada=sv_sum[0:48].reshape(1, 6 * D), g_mix=sv_sum[48:56].reshape(1, D), g_ffn=sv_sum[56:64].reshape(1, D),
                   g_final=sv_sum[64:72].reshape(D), b_fgate=sv_sum[72, 0:8].reshape(1, 8))

    dmod_all = lax.dynamic_slice(sv_all[:, 0:48, :].reshape(8, 6 * D), (0, chip * n_ada), (8, n_ada))
    g_ada, d_ada, nm_ada, nv_ada = adam_w_ada(jnp.pad(sc.T, ((0, 0), (0, LANES - 8))), jnp.pad(dmod_all, ((0, LANES - 8), (0, 0))),
                                              w_ada[0], m_w_ada[0], v_w_ada[0])

    big = dict(w_in=(w_in, m_w_in, v_w_in), w_br_a=(w_br_a, m_w_br_a, v_w_br_a), w_br_b=(w_br_b, m_w_br_b, v_w_br_b),
               w_out=(w_out, m_w_out, v_w_out), w_ffn_gate=(w_ffn_gate, m_w_ffn_gate, v_w_ffn_gate),
               w_ffn_up=(w_ffn_up, m_w_ffn_up, v_w_ffn_up), w_ffn_down=(w_ffn_down, m_w_ffn_down, v_w_ffn_down))
    upd = {nm: adam(w[0], gw[nm], m[0], v[0], "adam_" + nm) for nm, (w, m, v) in big.items()}

    def pack(gm, gf, gl, ba, bf):
        rows = [gm.reshape(1, D), gf.reshape(1, D), gl.reshape(1, D), ba.reshape(6, D), jnp.pad(bf.reshape(1, 8), ((0, 0), (0, D - 8)))]
        return jnp.concatenate(rows + [jnp.zeros((6, D), F32)], axis=0)

    packed = adam(pack(g_mix, g_ffn, g_final, b_ada, b_fgate),
                  pack(g_small["g_mix"], g_small["g_ffn"], g_small["g_final"], g_small["b_ada"], g_small["b_fgate"]),
                  pack(m_g_mix, m_g_ffn, m_g_final, m_b_ada, m_b_fgate), pack(v_g_mix, v_g_ffn, v_g_final, v_b_ada, v_b_fgate),
                  "adam_small")

    def unpack(t):
        return dict(g_mix=t[0:1], g_ffn=t[1:2], g_final=t[2], b_ada=t[3:9].reshape(1, 6 * D), b_fgate=t[9:10, 0:8])

    small_upd = [unpack(t) for t in packed]
    order = ["w_ada", "b_ada", "g_mix", "w_in", "b_fgate", "w_br_a", "w_br_b", "w_out", "g_ffn", "w_ffn_gate", "w_ffn_up", "w_ffn_down", "g_final"]

    def leaf(nm, which):
        if nm == "w_ada":
            return (g_ada, d_ada, nm_ada, nv_ada)[which][None]
        if nm in big:
            return (gw[nm] if which == 0 else upd[nm][which - 1])[None]
        return g_small[nm] if which == 0 else small_upd[which - 1][nm]

    outs = [loss, dx[None]]
    for which in range(4):
        outs += [leaf(nm, which) for nm in order]
    return tuple(outs)
```

```python
import functools

import numpy as np
import jax
import jax.numpy as jnp
from jax import lax
from jax.experimental import pallas as pl
from jax.experimental.pallas import tpu as pltpu

F32, BF16 = jnp.float32, jnp.bfloat16
S, D = 2048, 1024
HD = 64
LANES = 128
N_FOX_PAIRS, N_DIL_PAIRS = 4, 2
DIL_GROUPS = ((1, 16), (4, 4), (16, 1))
SPAN = 128
ROT_DIM, ROPE_THETA = 16, 500000.0
D_FF, FF_SHARD, FF_PAD = 2816, 704, 768
FFP = 4 * FF_PAD
IN_COLS, IN_SHARD, IN_SHARD_PAD = 5896, 1474, 1536
LAY_B, LAY_A, LAY_F, LAY_G, LAY_N = 0, 2304, 3840, 4096, 6144
EPS, NEG = 1e-6, -1e30
SCALE = HD ** -0.5
ADAM_LR, ADAM_B1, ADAM_B2, ADAM_EPS, ADAM_WD, ADAM_STEP = 0.001, 0.9, 0.999, 1e-08, 0.01, 10
VMEM_MB = 56
MESH = pl.DeviceIdType.MESH


def _params(vmem_mb=None, **kw):
    if vmem_mb is not None:
        kw["vmem_limit_bytes"] = vmem_mb * 1024 * 1024
    return pltpu.CompilerParams(**kw)


def _sds(shape, dtype):
    return jax.ShapeDtypeStruct(shape, dtype)


def _sigmoid(x):
    return 1.0 / (1.0 + jnp.exp(-x))


def _colsum8(x):
    tm, n = x.shape
    return jnp.sum(x.reshape(tm // 8, 8, n), axis=0)


class Comm:
    def __init__(self, ins, out_shapes, sems, start, wait, aliases=None):
        self.ins, self.out_shapes, self.sems = list(ins), list(out_shapes), list(sems)
        self.start, self.wait, self.aliases = start, wait, dict(aliases or {})


def _hosted_call(body, comm, args, *, name, grid, in_specs, out_specs, out_shape, scratch_shapes=(), aliases=None, vmem_mb=None):
    single = not isinstance(out_shape, (list, tuple))
    out_specs_l = [out_specs] if single else list(out_specs)
    out_shape_l = [out_shape] if single else list(out_shape)
    n_in, n_out, n_scr = len(in_specs), len(out_shape_l), len(scratch_shapes)
    aliases = dict(aliases or {})
    if comm is None:
        res = pl.pallas_call(body, name=name, grid=grid, in_specs=list(in_specs), out_specs=out_specs, out_shape=out_shape,
                             scratch_shapes=list(scratch_shapes), input_output_aliases=aliases,
                             compiler_params=_params(vmem_mb))(*args)
        return res, []
    nci, nco = len(comm.ins), len(comm.out_shapes)

    def wrapped(*refs):
        main_in, cin = refs[:n_in], refs[n_in:n_in + nci]
        o0 = n_in + nci
        main_out, cout = refs[o0:o0 + n_out], refs[o0 + n_out:o0 + n_out + nco]
        s0 = o0 + n_out + nco
        scr, sems = refs[s0:s0 + n_scr], refs[s0 + n_scr:]
        ids = [pl.program_id(i) for i in range(len(grid))]
        first = functools.reduce(jnp.logical_and, [i == 0 for i in ids])
        last = functools.reduce(jnp.logical_and, [i == g - 1 for i, g in zip(ids, grid)])

        @pl.when(first)
        def _():
            comm.start(cin, cout, sems)

        body(*main_in, *main_out, *scr)

        @pl.when(last)
        def _():
            comm.wait(cin, cout, sems)

    for ci, co in comm.aliases.items():
        aliases[n_in + ci] = n_out + co
    any_spec = pl.BlockSpec(memory_space=pl.ANY)
    res = pl.pallas_call(
        wrapped, name=name, grid=grid, in_specs=list(in_specs) + [any_spec] * nci, out_specs=out_specs_l + [any_spec] * nco,
        out_shape=out_shape_l + comm.out_shapes,
        scratch_shapes=list(scratch_shapes) + [pltpu.SemaphoreType.DMA((s,)) for s in comm.sems],
        input_output_aliases=aliases, compiler_params=_params(vmem_mb))(*args, *comm.ins)
    main = list(res[:n_out])
    return (main[0] if single else main), list(res[n_out:])


def norm_mod_fwd(x, g, mod, shift_row, scale_row):
    tm = 256

    def body(x_ref, g_ref, mod_ref, h_ref):
        xv = x_ref[...]
        r = lax.rsqrt(jnp.mean(xv * xv, axis=1, keepdims=True) + EPS)
        n = xv * r * g_ref[...]
        h = n * (1.0 + mod_ref[scale_row:scale_row + 1, :]) + mod_ref[shift_row:shift_row + 1, :]
        h_ref[...] = h.astype(BF16)

    return pl.pallas_call(
        body, name="norm_mod_fwd", grid=(S // tm,),
        in_specs=[pl.BlockSpec((tm, D), lambda i: (i, 0)), pl.BlockSpec((1, D), lambda i: (0, 0)),
                  pl.BlockSpec((8, D), lambda i: (0, 0))],
        out_specs=pl.BlockSpec((tm, D), lambda i: (i, 0)),
        out_shape=_sds((S, D), BF16),
    )(x, g, mod)


def rope_tables():
    pos = jnp.arange(S, dtype=F32)
    inv_freq = ROPE_THETA ** (-jnp.arange(0, ROT_DIM, 2, dtype=F32) / ROT_DIM)
    ang = pos[:, None] * inv_freq[None, :]
    cos, sin = jnp.cos(ang), jnp.sin(ang)
    one, zero = jnp.ones((S, HD - ROT_DIM), F32), jnp.zeros((S, HD - ROT_DIM), F32)
    z8 = jnp.zeros((S, 8), F32)
    c = jnp.concatenate([cos, cos, one], axis=1)
    s1 = jnp.concatenate([-sin, z8, zero], axis=1)
    s2 = jnp.concatenate([z8, sin, zero], axis=1)
    return tuple(jnp.concatenate([t, t], axis=1) for t in (c, s1, s2))


def _rope(y, c, s1, s2):
    return y * c + pltpu.roll(y, LANES - 8, 1) * s1 + pltpu.roll(y, 8, 1) * s2


def _rope_bwd(dy, c, s1, s2):
    return dy * c + pltpu.roll(dy * s1, 8, 1) + pltpu.roll(dy * s2, LANES - 8, 1)


def in_proj_fwd(h, w_lay, tabs):
    tm, tn = 512, 384
    n_rope = 2 * N_DIL_PAIRS * 3 // 2

    def body(a_ref, w_ref, c_ref, s1_ref, s2_ref, o_ref):
        j = pl.program_id(0)
        y = jnp.dot(a_ref[...], w_ref[...], preferred_element_type=F32)

        @pl.when(j < n_rope)
        def _():
            c, s1, s2 = c_ref[...], s1_ref[...], s2_ref[...]
            for t in range(2):
                o_ref[:, LANES * t:LANES * (t + 1)] = _rope(y[:, LANES * t:LANES * (t + 1)], c, s1, s2)
            o_ref[:, 2 * LANES:] = y[:, 2 * LANES:]

        @pl.when(j >= n_rope)
        def _():
            o_ref[...] = y

    tab = pl.BlockSpec((tm, LANES), lambda j, i: (i, 0))
    return pl.pallas_call(
        body, name="in_proj_fwd", grid=(LAY_N // tn, S // tm),
        in_specs=[pl.BlockSpec((tm, D), lambda j, i: (i, 0)), pl.BlockSpec((D, tn), lambda j, i: (0, j)), tab, tab, tab],
        out_specs=pl.BlockSpec((tm, tn), lambda j, i: (i, j)),
        out_shape=_sds((S, LAY_N), F32),
    )(h, w_lay, *tabs)


def _log1p_small(t):
    return jnp.where(t < 1e-2, t * (1.0 - t * (0.5 - t * (1.0 / 3.0))), jnp.log(1.0 + t))


def fgate_fwd(p, b_pad):
    def body(fa_ref, b_ref, frow_ref, fraw_ref, fcol_ref):
        f = fa_ref[...] + b_ref[...]
        fr = f.T[0:8, :]
        ls = jnp.minimum(fr, 0.0) - _log1p_small(jnp.exp(-jnp.abs(fr)))
        lane = lax.broadcasted_iota(jnp.int32, (8, S), 1)
        acc, sh = ls, 1
        while sh < S:
            acc = acc + jnp.where(lane >= sh, pltpu.roll(acc, sh, 1), 0.0)
            sh *= 2
        frow_ref[...] = acc
        fraw_ref[...] = fr
        for hh in range(8):
            fcol_ref[hh] = jnp.broadcast_to(acc[hh:hh + 1, :], (LANES, S)).T

    return pl.pallas_call(
        body, name="fgate_fwd", grid=(1,),
        in_specs=[pl.BlockSpec((S, LANES), lambda i: (0, LAY_F // LANES)), pl.BlockSpec((1, LANES), lambda i: (0, 0))],
        out_specs=[pl.BlockSpec((8, S), lambda i: (0, 0)), pl.BlockSpec((8, S), lambda i: (0, 0)),
                   pl.BlockSpec((8, S, LANES), lambda i: (0, 0, 0))],
        out_shape=[_sds((8, S), F32), _sds((8, S), F32), _sds((8, S, LANES), F32)],
        compiler_params=_params(VMEM_MB),
    )(p, b_pad)


TQ = TK = 256


def _head_masks(rows):
    lane = lax.broadcasted_iota(jnp.int32, (rows, LANES), 1)
    return lane < HD, lane >= HD


def fox_fwd(p, fcol, frow, comm=None):
    def body(qkv_ref, fc_ref, fr_ref, o_ref, g_ref):
        masks = _head_masks(TQ)
        rowi = lax.broadcasted_iota(jnp.int32, (TQ, TK), 0)
        coli = lax.broadcasted_iota(jnp.int32, (TQ, TK), 1)

        def qloop(qi, _):
            q0 = pl.multiple_of(qi * TQ, TQ)
            q = qkv_ref[pl.ds(q0, TQ), 0:LANES]
            outs = []
            for hh in range(2):
                qm = (jnp.where(masks[hh], q, 0.0) * SCALE).astype(BF16)
                fc = fc_ref[hh, pl.ds(q0, TQ), :]
                fc2 = jnp.concatenate([fc, fc], axis=1)

                def kloop(kb, carry):
                    m, l, acc = carry
                    k0 = pl.multiple_of(kb * TK, TK)
                    k = qkv_ref[pl.ds(k0, TK), LANES:2 * LANES].astype(BF16)
                    v = qkv_ref[pl.ds(k0, TK), 2 * LANES:3 * LANES].astype(BF16)
                    s = lax.dot_general(qm, k, (((1,), (1,)), ((), ())), preferred_element_type=F32)
                    s = s + (fc2 - fr_ref[hh:hh + 1, pl.ds(k0, TK)])
                    s = jnp.where(coli + k0 <= rowi + q0, s, NEG)
                    m_new = jnp.maximum(m, jnp.max(s, axis=1, keepdims=True))
                    pr = jnp.exp(s - m_new)
                    alpha = jnp.exp(m - m_new)
                    l = l * alpha + jnp.sum(pr, axis=1, keepdims=True)
                    acc = acc * alpha + jnp.dot(pr.astype(BF16), v, preferred_element_type=F32)
                    return m_new, l, acc

                init = (jnp.full((TQ, 1), NEG, F32), jnp.zeros((TQ, 1), F32), jnp.zeros((TQ, LANES), F32))
                m, l, acc = lax.fori_loop(0, qi + 1, kloop, init)
                outs.append(acc / l)
                g_ref[hh, pl.ds(q0, TQ), :] = fc - (m + jnp.log(l))
            o_ref[pl.ds(q0, TQ), :] = jnp.where(masks[0], outs[0], outs[1]).astype(BF16)
            return 0

        lax.fori_loop(0, S // TQ, qloop, 0)

    a_blk = LAY_A // 384
    return _hosted_call(
        body, comm, (p, fcol, frow.reshape(4, 2, S)), name="fox_fwd", grid=(N_FOX_PAIRS,),
        in_specs=[pl.BlockSpec((S, 384), lambda p_: (0, a_blk + p_)),
                  pl.BlockSpec((2, S, LANES), lambda p_: (p_, 0, 0)),
                  pl.BlockSpec((None, 2, S), lambda p_: (p_, 0, 0))],
        out_specs=[pl.BlockSpec((S, LANES), lambda p_: (0, p_)), pl.BlockSpec((2, S, LANES), lambda p_: (p_, 0, 0))],
        out_shape=[_sds((S, 4 * LANES), BF16), _sds((8, S, LANES), F32)],
        vmem_mb=VMEM_MB)


def _dil_rows(ref, start, d):
    return ref[pl.ds(start, SPAN), :] if d == 1 else ref[pl.ds(start, SPAN, stride=d), :]


def _dil_store(ref, start, d, val):
    if d == 1:
        ref[pl.ds(start, SPAN), :] = val
    else:
        ref[pl.ds(start, SPAN, stride=d), :] = val


def _band_mask(has_prev):
    qi = lax.broadcasted_iota(jnp.int32, (SPAN, 2 * SPAN), 0) + SPAN
    kj = lax.broadcasted_iota(jnp.int32, (SPAN, 2 * SPAN), 1)
    dist = qi - kj
    return (dist >= 0) & (dist <= SPAN) & (has_prev | (kj >= SPAN))


def _dil_block(n, d, nb):
    r, j = n // nb, n % nb
    start = r + d * SPAN * j
    prev = jnp.maximum(start - d * SPAN, r)
    return start, prev, j > 0


def dil_fwd(p, comm=None):
    def body(*refs):
        qkv = [refs[3 * g:3 * g + 3] for g in range(3)]
        y_ref, lse_ref = refs[9], refs[10]
        acc_s, m_s, l_s = refs[11], refs[12], refs[13]
        masks = _head_masks(SPAN)
        for g, (d, nb) in enumerate(DIL_GROUPS):
            q_ref, k_ref, v_ref = qkv[g]

            def blk(n, _):
                start, prev, has_prev = _dil_block(n, d, nb)
                q = _dil_rows(q_ref, start, d)
                kc = jnp.concatenate([_dil_rows(k_ref, prev, d), _dil_rows(k_ref, start, d)], axis=0).astype(BF16)
                vc = jnp.concatenate([_dil_rows(v_ref, prev, d), _dil_rows(v_ref, start, d)], axis=0).astype(BF16)
                valid = _band_mask(has_prev)
                accs, ms, ls = [], [], []
                for hh in range(2):
                    qm = (jnp.where(masks[hh], q, 0.0) * SCALE).astype(BF16)
                    s = lax.dot_general(qm, kc, (((1,), (1,)), ((), ())), preferred_element_type=F32)
                    s = jnp.where(valid, s, NEG)
                    m = jnp.max(s, axis=1, keepdims=True)
                    pr = jnp.exp(s - m)
                    ls.append(jnp.sum(pr, axis=1, keepdims=True))
                    ms.append(m)
                    accs.append(jnp.dot(pr.astype(BF16), vc, preferred_element_type=F32))
                _dil_store(acc_s.at[g], start, d, jnp.where(masks[0], accs[0], accs[1]))
                _dil_store(m_s.at[g], start, d, jnp.where(masks[0], ms[0], ms[1]))
                _dil_store(l_s.at[g], start, d, jnp.where(masks[0], ls[0], ls[1]))
                return 0

            lax.fori_loop(0, 16, blk, 0)

        def merge(i, _):
            rows = pl.ds(pl.multiple_of(i * 256, 256), 256)
            m = [m_s[g, rows, :] for g in range(3)]
            mx = jnp.maximum(jnp.maximum(m[0], m[1]), m[2])
            w = [jnp.exp(m[g] - mx) for g in range(3)]
            l = sum(l_s[g, rows, :] * w[g] for g in range(3))
            y_ref[rows, :] = sum(acc_s[g, rows, :] * w[g] for g in range(3)) / l
            lse_ref[rows, :] = mx + jnp.log(l)
            return 0

        lax.fori_loop(0, S // 256, merge, 0)

    def spec(g, t):
        return pl.BlockSpec((S, LANES), lambda p_: (0, (p_ * 3 + g) * 3 + t))

    return _hosted_call(
        body, comm, [p] * 9, name="dil_fwd", grid=(N_DIL_PAIRS,),
        in_specs=[spec(g, t) for g in range(3) for t in range(3)],
        out_specs=[pl.BlockSpec((S, LANES), lambda p_: (0, p_)), pl.BlockSpec((S, LANES), lambda p_: (0, p_))],
        out_shape=[_sds((S, 2 * LANES), F32), _sds((S, 2 * LANES), F32)],
        scratch_shapes=[pltpu.VMEM((3, S, LANES), F32)] * 3,
        vmem_mb=VMEM_MB)


def merge_fwd(ya_att, yb, p, w_bra, w_brb):
    tm = 256
    gblk = LAY_G // D

    def body(a_ref, b_ref, ga_ref, gb_ref, wa_ref, wb_ref, mg_ref, ya_ref, yb_ref):
        ya = jnp.dot(a_ref[...], wa_ref[...], preferred_element_type=F32)
        ybp = jnp.dot(b_ref[...].astype(BF16), wb_ref[...], preferred_element_type=F32)
        mg_ref[...] = (_sigmoid(ga_ref[...]) * ya + _sigmoid(gb_ref[...]) * ybp).astype(BF16)
        ya_ref[...] = ya
        yb_ref[...] = ybp

    row = lambda w: pl.BlockSpec((tm, w), lambda i: (i, 0))
    return pl.pallas_call(
        body, name="merge_fwd", grid=(S // tm,),
        in_specs=[row(512), row(256), pl.BlockSpec((tm, D), lambda i: (i, gblk)), pl.BlockSpec((tm, D), lambda i: (i, gblk + 1)),
                  pl.BlockSpec((512, D), lambda i: (0, 0)), pl.BlockSpec((256, D), lambda i: (0, 0))],
        out_specs=[row(D), row(D), row(D)],
        out_shape=[_sds((S, D), BF16), _sds((S, D), F32), _sds((S, D), F32)],
    )(ya_att, yb, p, p, w_bra, w_brb)


def out_proj_fwd(merged, w_out, x, mod, g_ffn):
    tm = 256

    def body(a_ref, w_ref, x_ref, mod_ref, g_ref, mix_ref, x1_ref, h2_ref):
        mix = jnp.dot(a_ref[...], w_ref[...], preferred_element_type=F32)
        x1 = x_ref[...] + mod_ref[2:3, :] * mix
        r = lax.rsqrt(jnp.mean(x1 * x1, axis=1, keepdims=True) + EPS)
        h2 = (x1 * r * g_ref[...]) * (1.0 + mod_ref[4:5, :]) + mod_ref[3:4, :]
        mix_ref[...] = mix
        x1_ref[...] = x1
        h2_ref[...] = h2.astype(BF16)

    row = pl.BlockSpec((tm, D), lambda i: (i, 0))
    return pl.pallas_call(
        body, name="out_proj_fwd", grid=(S // tm,),
        in_specs=[row, pl.BlockSpec((D, D), lambda i: (0, 0)), row, pl.BlockSpec((8, D), lambda i: (0, 0)),
                  pl.BlockSpec((1, D), lambda i: (0, 0))],
        out_specs=[row, row, row],
        out_shape=[_sds((S, D), F32), _sds((S, D), F32), _sds((S, D), BF16)],
    )(merged, w_out, x, mod, g_ffn)


def ffn_up_fwd(h2, w_gate, w_up):
    tm = 512

    def body(h_ref, wg_ref, wu_ref, a_ref, u_ref, z_ref):
        h = h_ref[...]
        a = jnp.dot(h, wg_ref[...], preferred_element_type=F32)
        u = jnp.dot(h, wu_ref[...], preferred_element_type=F32)
        a_ref[...] = a
        u_ref[...] = u
        z_ref[...] = (a * _sigmoid(a) * u).astype(BF16)

    out = pl.BlockSpec((tm, FF_PAD), lambda k, i: (i, k))
    return pl.pallas_call(
        body, name="ffn_up_fwd", grid=(4, S // tm),
        in_specs=[pl.BlockSpec((tm, D), lambda k, i: (i, 0)), pl.BlockSpec((None, D, FF_PAD), lambda k, i: (k, 0, 0)),
                  pl.BlockSpec((None, D, FF_PAD), lambda k, i: (k, 0, 0))],
        out_specs=[out, out, out],
        out_shape=[_sds((S, FFP), F32), _sds((S, FFP), F32), _sds((S, FFP), BF16)],
    )(h2, w_gate, w_up)


def ffn_down_loss(z, w_down, x1, mod, g_final, tgt):
    tm = 256

    def body(z_ref, w_ref, x1_ref, mod_ref, g_ref, t_ref, dx2_ref, dffn_ref, dg_ref, dga_ref, loss_ref, s_dg, s_dga, s_loss):
        i = pl.program_id(0)

        @pl.when(i == 0)
        def _():
            s_dg[...] = jnp.zeros_like(s_dg)
            s_dga[...] = jnp.zeros_like(s_dga)
            s_loss[...] = jnp.zeros_like(s_loss)

        ffn = jnp.dot(z_ref[...], w_ref[...], preferred_element_type=F32)
        gaf = mod_ref[5:6, :]
        x2 = x1_ref[...] + gaf * ffn
        r = lax.rsqrt(jnp.mean(x2 * x2, axis=1, keepdims=True) + EPS)
        xh = x2 * r
        g = g_ref[...]
        e = xh * g - t_ref[...]
        s_loss[...] += 0.5 * jnp.sum(jnp.mean(e * e, axis=1, keepdims=True), axis=0, keepdims=True)
        dy = e * (1.0 / D)
        gdy = dy * g
        dx2 = r * (gdy - xh * jnp.mean(gdy * xh, axis=1, keepdims=True))
        s_dg[...] += _colsum8(dy * xh)
        s_dga[...] += _colsum8(dx2 * ffn)
        dx2_ref[...] = dx2
        dffn_ref[...] = (dx2 * gaf).astype(BF16)

        @pl.when(i == pl.num_programs(0) - 1)
        def _():
            dg_ref[...] = jnp.sum(s_dg[...], axis=0, keepdims=True)
            dga_ref[...] = jnp.sum(s_dga[...], axis=0, keepdims=True)
            loss_ref[...] = jnp.broadcast_to(s_loss[...], (1, LANES))

    row = pl.BlockSpec((tm, D), lambda i: (i, 0))
    vec = pl.BlockSpec((1, D), lambda i: (0, 0))
    return pl.pallas_call(
        body, name="ffn_down_loss", grid=(S // tm,),
        in_specs=[pl.BlockSpec((tm, FFP), lambda i: (i, 0)), pl.BlockSpec((FFP, D), lambda i: (0, 0)), row,
                  pl.BlockSpec((8, D), lambda i: (0, 0)), vec, row],
        out_specs=[row, row, vec, vec, pl.BlockSpec((1, LANES), lambda i: (0, 0))],
        out_shape=[_sds((S, D), F32), _sds((S, D), BF16), _sds((1, D), F32), _sds((1, D), F32), _sds((1, LANES), F32)],
        scratch_shapes=[pltpu.VMEM((8, D), F32), pltpu.VMEM((8, D), F32), pltpu.VMEM((1, 1), F32)],
        compiler_params=_params(VMEM_MB),
    )(z, w_down, x1, mod, g_final, tgt)


def ffn_down_bwd(dffn, w_down, a, u, z):
    tm, tn = 512, 384

    def body(d_ref, w_ref, a_ref, u_ref, z_ref, da_ref, du_ref, dw_ref):
        i = pl.program_id(1)
        dff = d_ref[...]
        dz = lax.dot_general(dff, w_ref[...], (((1,), (1,)), ((), ())), preferred_element_type=F32)
        av, uv = a_ref[...], u_ref[...]
        sg = _sigmoid(av)
        du_ref[...] = (dz * (av * sg)).astype(BF16)
        da_ref[...] = (dz * uv * (sg * (1.0 + av * (1.0 - sg)))).astype(BF16)
        dw = lax.dot_general(z_ref[...], dff, (((0,), (0,)), ((), ())), preferred_element_type=F32)

        @pl.when(i == 0)
        def _():
            dw_ref[...] = dw

        @pl.when(i > 0)
        def _():
            dw_ref[...] += dw

    tile = pl.BlockSpec((tm, tn), lambda j, i: (i, j))
    return pl.pallas_call(
        body, name="ffn_down_bwd", grid=(FFP // tn, S // tm),
        in_specs=[pl.BlockSpec((tm, D), lambda j, i: (i, 0)), pl.BlockSpec((tn, D), lambda j, i: (j, 0)), tile, tile, tile],
        out_specs=[tile, tile, pl.BlockSpec((tn, D), lambda j, i: (j, 0))],
        out_shape=[_sds((S, FFP), BF16), _sds((S, FFP), BF16), _sds((FFP, D), F32)],
    )(dffn, w_down, a, u, z)


def mm_nt(dy, w, name, comm=None):
    tm = 512
    n = dy.shape[1]
    if w.ndim == 2:
        k_in, tk = w.shape[0], 768
        w_spec = pl.BlockSpec((k_in, tk), lambda i, k: (0, k))
    else:
        k_in, tk = w.shape[1], FF_PAD
        w_spec = pl.BlockSpec((None, k_in, tk), lambda i, k: (k, 0, 0))
    nk = n // tk

    def body(d_ref, w_ref, o_ref, acc):
        k = pl.program_id(1)
        part = lax.dot_general(d_ref[...], w_ref[...], (((1,), (1,)), ((), ())), preferred_element_type=F32)

        @pl.when(k == 0)
        def _():
            acc[...] = part

        @pl.when(k > 0)
        def _():
            acc[...] += part

        @pl.when(k == nk - 1)
        def _():
            o_ref[...] = acc[...]

    return _hosted_call(
        body, comm, (dy, w), name=name, grid=(S // tm, nk),
        in_specs=[pl.BlockSpec((tm, tk), lambda i, k: (i, k)), w_spec],
        out_specs=pl.BlockSpec((tm, k_in), lambda i, k: (i, 0)),
        out_shape=_sds((S, k_in), F32),
        scratch_shapes=[pltpu.VMEM((tm, k_in), F32)])


def mm_tn(h, dy, name, shard_major=False):
    tm, tn = 512, 768
    k_in, n = h.shape[1], dy.shape[1]

    def body(h_ref, d_ref, o_ref):
        i = pl.program_id(1)
        dw = lax.dot_general(h_ref[...], d_ref[...], (((0,), (0,)), ((), ())), preferred_element_type=F32)

        @pl.when(i == 0)
        def _():
            o_ref[...] = dw

        @pl.when(i > 0)
        def _():
            o_ref[...] += dw

    if shard_major:
        out_spec, out_shape = pl.BlockSpec((None, k_in, tn), lambda j, i: (j, 0, 0)), _sds((n // tn, k_in, tn), F32)
    else:
        out_spec, out_shape = pl.BlockSpec((k_in, tn), lambda j, i: (0, j)), _sds((k_in, n), F32)
    return pl.pallas_call(
        body, name=name, grid=(n // tn, S // tm),
        in_specs=[pl.BlockSpec((tm, k_in), lambda j, i: (i, 0)), pl.BlockSpec((tm, tn), lambda j, i: (i, j))],
        out_specs=out_spec, out_shape=out_shape,
    )(h, dy)


def mid_bwd(dh2a, dh2b, x1, dx2, mix, mod, g_ffn, p, ya, ybp, merged, ya_att, yb, w_out, w_bra, w_brb, comm=None):
    tm = 128
    gblk = LAY_G // D
    nsteps = S // tm

    def body(dha_ref, dhb_ref, x1_ref, dx2_ref, mix_ref, mod_ref, g_ref, ga_ref, gb_ref, ya_ref, yb_ref, mg_ref,
             att_ref, ybb_ref, wo_ref, wa_ref, wb_ref,
             dx1_ref, dpg_ref, datt_ref, dyb_ref, cs_ref, dwo_ref, dwa_ref, dwb_ref, s_cs):
        i = pl.program_id(0)

        @pl.when(i == 0)
        def _():
            s_cs[...] = jnp.zeros_like(s_cs)
            dwo_ref[...] = jnp.zeros_like(dwo_ref)
            dwa_ref[...] = jnp.zeros_like(dwa_ref)
            dwb_ref[...] = jnp.zeros_like(dwb_ref)

        x1 = x1_ref[...]
        g = g_ref[...]
        r = lax.rsqrt(jnp.mean(x1 * x1, axis=1, keepdims=True) + EPS)
        xh = x1 * r
        dh2 = dha_ref[...] + dhb_ref[...]
        s_cs[0] += _colsum8(dh2)
        s_cs[1] += _colsum8(dh2 * (xh * g))
        dn2 = dh2 * (1.0 + mod_ref[4:5, :])
        s_cs[2] += _colsum8(dn2 * xh)
        gd = dn2 * g
        dx1 = dx2_ref[...] + r * (gd - xh * jnp.mean(gd * xh, axis=1, keepdims=True))
        s_cs[3] += _colsum8(dx1 * mix_ref[...])
        dx1_ref[...] = dx1
        dmix = (dx1 * mod_ref[2:3, :]).astype(BF16)
        dmg = lax.dot_general(dmix, wo_ref[...], (((1,), (1,)), ((), ())), preferred_element_type=F32)
        sga, sgb = _sigmoid(ga_ref[...]), _sigmoid(gb_ref[...])
        dya = (dmg * sga).astype(BF16)
        dybp = (dmg * sgb).astype(BF16)
        dpg_ref[:, 0:D] = (dmg * ya_ref[...] * (sga * (1.0 - sga))).astype(BF16)
        dpg_ref[:, D:2 * D] = (dmg * yb_ref[...] * (sgb * (1.0 - sgb))).astype(BF16)
        datt_ref[...] = lax.dot_general(dya, wa_ref[...], (((1,), (1,)), ((), ())), preferred_element_type=F32).astype(BF16)
        dyb_ref[...] = lax.dot_general(dybp, wb_ref[...], (((1,), (1,)), ((), ())), preferred_element_type=F32)
        tn_dims = (((0,), (0,)), ((), ()))
        dwo_ref[...] += lax.dot_general(mg_ref[...], dmix, tn_dims, preferred_element_type=F32)
        dwa_ref[...] += lax.dot_general(att_ref[...], dya, tn_dims, preferred_element_type=F32)
        dwb_ref[...] += lax.dot_general(ybb_ref[...].astype(BF16), dybp, tn_dims, preferred_element_type=F32)

        @pl.when(i == nsteps - 1)
        def _():
            for t in range(4):
                cs_ref[t:t + 1, :] = jnp.sum(s_cs[t], axis=0, keepdims=True)
            cs_ref[4:8, :] = jnp.zeros((4, D), F32)

    row = lambda w: pl.BlockSpec((tm, w), lambda i: (i, 0))
    full = lambda a, b: pl.BlockSpec((a, b), lambda i: (0, 0))
    return _hosted_call(
        body, comm, (dh2a, dh2b, x1, dx2, mix, mod, g_ffn, p, p, ya, ybp, merged, ya_att, yb, w_out, w_bra, w_brb),
        name="mid_bwd", grid=(nsteps,),
        in_specs=[row(D), row(D), row(D), row(D), row(D), full(8, D), full(1, D),
                  pl.BlockSpec((tm, D), lambda i: (i, gblk)), pl.BlockSpec((tm, D), lambda i: (i, gblk + 1)),
                  row(D), row(D), row(D), row(512), row(256), full(D, D), full(512, D), full(256, D)],
        out_specs=[row(D), pl.BlockSpec((tm, 2 * D), lambda i: (i, LAY_G // (2 * D))), row(512), row(256), full(8, D),
                   full(D, D), full(512, D), full(256, D)],
        out_shape=[_sds((S, D), F32), _sds((S, LAY_N), BF16), _sds((S, 512), BF16), _sds((S, 256), F32), _sds((8, D), F32),
                   _sds((D, D), F32), _sds((512, D), F32), _sds((256, D), F32)],
        scratch_shapes=[pltpu.VMEM((4, 8, D), F32)],
        vmem_mb=VMEM_MB)


def fox_bwd(p, do, o, gcol, frow, dp, comm=None):
    nq = S // TQ

    def body(qkv_ref, do_ref, o_ref, g_ref, fr_ref, dp_in, dp_ref, df_ref, rs_ref, dq_s):
        del dp_in
        masks = _head_masks(TQ)
        lane = lax.broadcasted_iota(jnp.int32, (TQ, LANES), 1)
        head0 = 2 * pl.program_id(0)
        rowi = lax.broadcasted_iota(jnp.int32, (TQ, TK), 0)
        coli = lax.broadcasted_iota(jnp.int32, (TQ, TK), 1)
        dq_s[...] = jnp.zeros_like(dq_s)
        rs_ref[...] = jnp.zeros_like(rs_ref)
        nt = (((1,), (1,)), ((), ()))
        tn = (((0,), (0,)), ((), ()))

        def kloop(kb, _):
            k0 = pl.multiple_of(kb * TK, TK)
            k = qkv_ref[pl.ds(k0, TK), LANES:2 * LANES].astype(BF16)
            v = qkv_ref[pl.ds(k0, TK), 2 * LANES:3 * LANES].astype(BF16)
            fr = [fr_ref[hh:hh + 1, pl.ds(k0, TK)] for hh in range(2)]

            def qloop(qi, carry):
                dk, dv, df0, df1 = carry
                q0 = pl.multiple_of(qi * TQ, TQ)
                q = qkv_ref[pl.ds(q0, TQ), 0:LANES]
                dov = do_ref[pl.ds(q0, TQ), :].astype(F32)
                ov = o_ref[pl.ds(q0, TQ), :].astype(F32)
                causal = coli + k0 <= rowi + q0
                dqs, dfs = [], []
                rs = jnp.zeros((TQ, LANES), F32)
                for hh in range(2):
                    qm = (jnp.where(masks[hh], q, 0.0) * SCALE).astype(BF16)
                    dom = jnp.where(masks[hh], dov, 0.0)
                    delta = jnp.sum(dom * ov, axis=1, keepdims=True)
                    dob = dom.astype(BF16)
                    gc = g_ref[hh, pl.ds(q0, TQ), :]
                    gc2 = jnp.concatenate([gc, gc], axis=1)
                    s = lax.dot_general(qm, k, nt, preferred_element_type=F32) + (gc2 - fr[hh])
                    pr = jnp.where(causal, jnp.exp(jnp.where(causal, s, NEG)), 0.0)
                    dpr = lax.dot_general(dob, v, nt, preferred_element_type=F32)
                    ds = pr * (dpr - delta)
                    dsb = ds.astype(BF16)
                    dqs.append(jnp.dot(dsb, k, preferred_element_type=F32) * SCALE)
                    dk = dk + lax.dot_general(dsb, qm, tn, preferred_element_type=F32)
                    dv = dv + lax.dot_general(pr.astype(BF16), dob, tn, preferred_element_type=F32)
                    dfs.append(jnp.sum(ds, axis=0, keepdims=True))
                    rs = rs + jnp.where(lane == head0 + hh, jnp.sum(ds, axis=1, keepdims=True), 0.0)
                dq_s[pl.ds(q0, TQ), :] += jnp.where(masks[0], dqs[0], dqs[1])
                rs_ref[pl.ds(q0, TQ), :] += rs
                return dk, dv, df0 - dfs[0], df1 - dfs[1]

            z = jnp.zeros((TK, LANES), F32)
            z1 = jnp.zeros((1, TK), F32)
            dk, dv, df0, df1 = lax.fori_loop(kb, nq, qloop, (z, z, z1, z1))
            dp_ref[pl.ds(k0, TK), LANES:2 * LANES] = dk.astype(BF16)
            dp_ref[pl.ds(k0, TK), 2 * LANES:3 * LANES] = dv.astype(BF16)
            df_ref[0:1, pl.ds(k0, TK)] = df0
            df_ref[1:2, pl.ds(k0, TK)] = df1
            return 0

        lax.fori_loop(0, S // TK, kloop, 0)
        dp_ref[:, 0:LANES] = dq_s[...].astype(BF16)

    a_blk = LAY_A // 384
    pair = pl.BlockSpec((S, LANES), lambda p_: (0, p_))
    return _hosted_call(
        body, comm, (p, do, o, gcol, frow.reshape(4, 2, S), dp), name="fox_bwd", grid=(N_FOX_PAIRS,),
        in_specs=[pl.BlockSpec((S, 384), lambda p_: (0, a_blk + p_)), pair, pair,
                  pl.BlockSpec((2, S, LANES), lambda p_: (p_, 0, 0)), pl.BlockSpec((None, 2, S), lambda p_: (p_, 0, 0)),
                  pl.BlockSpec(memory_space=pl.ANY)],
        out_specs=[pl.BlockSpec((S, 384), lambda p_: (0, a_blk + p_)), pl.BlockSpec((None, 2, S), lambda p_: (p_, 0, 0)),
                   pl.BlockSpec((None, S, LANES), lambda p_: (p_, 0, 0))],
        out_shape=[_sds((S, LAY_N), BF16), _sds((4, 2, S), F32), _sds((4, S, LANES), F32)],
        scratch_shapes=[pltpu.VMEM((S, LANES), F32)],
        aliases={5: 0}, vmem_mb=VMEM_MB)


def fgate_bwd(dfrow, dfcol, fraw, dp):
    def body(df_ref, dc_ref, f_ref, dp_in, dpf_ref, db_ref):
        del dp_in
        lane = lax.broadcasted_iota(jnp.int32, (8, S), 1)
        rsum = (dc_ref[0] + dc_ref[1]) + (dc_ref[2] + dc_ref[3])
        acc, sh = df_ref[...] + rsum.T[0:8, :], 1
        while sh < S:
            acc = acc + jnp.where(lane < S - sh, pltpu.roll(acc, S - sh, 1), 0.0)
            sh *= 2
        df = acc * _sigmoid(-f_ref[...])
        db_ref[...] = jnp.broadcast_to(jnp.sum(df, axis=1, keepdims=True), (8, LANES))
        dfc = jnp.concatenate([df, jnp.zeros((LANES - 8, S), F32)], axis=0).T
        dpf_ref[:, 0:LANES] = dfc.astype(BF16)
        dpf_ref[:, LANES:2 * LANES] = jnp.zeros((S, LANES), BF16)

    return pl.pallas_call(
        body, name="fgate_bwd", grid=(1,),
        in_specs=[pl.BlockSpec((8, S), lambda i: (0, 0)), pl.BlockSpec((4, S, LANES), lambda i: (0, 0, 0)),
                  pl.BlockSpec((8, S), lambda i: (0, 0)), pl.BlockSpec(memory_space=pl.ANY)],
        out_specs=[pl.BlockSpec((S, 2 * LANES), lambda i: (0, LAY_F // (2 * LANES))), pl.BlockSpec((8, LANES), lambda i: (0, 0))],
        out_shape=[_sds((S, LAY_N), BF16), _sds((8, LANES), F32)],
        input_output_aliases={3: 0},
        compiler_params=_params(VMEM_MB),
    )(dfrow, dfcol, fraw, dp)


def dil_bwd(p, dyb, yb, lse, tabs, dp, comm=None):
    def body(*refs):
        qkv = [refs[3 * g:3 * g + 3] for g in range(3)]
        dy_ref, y_ref, lse_ref, c_ref, s1_ref, s2_ref = refs[9:15]
        dp_ref = refs[16]
        dq_s, dk_s, dv_s, dl_s = refs[17:21]
        masks = _head_masks(SPAN)
        m256 = _head_masks(256)
        nt = (((1,), (1,)), ((), ()))
        tn = (((0,), (0,)), ((), ()))
        dk_s[...] = jnp.zeros_like(dk_s)
        dv_s[...] = jnp.zeros_like(dv_s)

        def prep(i, _):
            rows = pl.ds(pl.multiple_of(i * 256, 256), 256)
            pr = dy_ref[rows, :] * y_ref[rows, :]
            d0 = jnp.sum(jnp.where(m256[0], pr, 0.0), axis=1, keepdims=True)
            d1 = jnp.sum(jnp.where(m256[1], pr, 0.0), axis=1, keepdims=True)
            dl_s[rows, :] = jnp.where(m256[0], d0, d1)
            return 0

        lax.fori_loop(0, S // 256, prep, 0)

        for g, (d, nb) in enumerate(DIL_GROUPS):
            q_ref, k_ref, v_ref = qkv[g]

            def blk(n, _):
                start, prev, has_prev = _dil_block(n, d, nb)
                q = _dil_rows(q_ref, start, d)
                kc = jnp.concatenate([_dil_rows(k_ref, prev, d), _dil_rows(k_ref, start, d)], axis=0).astype(BF16)
                vc = jnp.concatenate([_dil_rows(v_ref, prev, d), _dil_rows(v_ref, start, d)], axis=0).astype(BF16)
                dov = _dil_rows(dy_ref, start, d)
                lsev = _dil_rows(lse_ref, start, d)
                dlv = _dil_rows(dl_s, start, d)
                valid = _band_mask(has_prev)
                dqs = []
                dkc = jnp.zeros((2 * SPAN, LANES), F32)
                dvc = jnp.zeros((2 * SPAN, LANES), F32)
                for hh in range(2):
                    qm = (jnp.where(masks[hh], q, 0.0) * SCALE).astype(BF16)
                    dob = jnp.where(masks[hh], dov, 0.0).astype(BF16)
                    lse_h = jnp.max(jnp.where(masks[hh], lsev, NEG), axis=1, keepdims=True)
                    dl_h = jnp.max(jnp.where(masks[hh], dlv, NEG), axis=1, keepdims=True)
                    s = lax.dot_general(qm, kc, nt, preferred_element_type=F32)
                    pr = jnp.where(valid, jnp.exp(jnp.where(valid, s, NEG) - lse_h), 0.0)
                    dpr = lax.dot_general(dob, vc, nt, preferred_element_type=F32)
                    dsb = (pr * (dpr - dl_h)).astype(BF16)
                    dqs.append(jnp.dot(dsb, kc, preferred_element_type=F32) * SCALE)
                    dkc = dkc + lax.dot_general(dsb, qm, tn, preferred_element_type=F32)
                    dvc = dvc + lax.dot_general(pr.astype(BF16), dob, tn, preferred_element_type=F32)
                _dil_store(dq_s.at[g], start, d, jnp.where(masks[0], dqs[0], dqs[1]))
                for ref, val in ((dk_s.at[g], dkc), (dv_s.at[g], dvc)):
                    _dil_store(ref, prev, d, _dil_rows(ref, prev, d) + jnp.where(has_prev, val[0:SPAN], 0.0))
                    _dil_store(ref, start, d, _dil_rows(ref, start, d) + val[SPAN:])
                return 0

            lax.fori_loop(0, 16, blk, 0)

        def fin(i, _):
            rows = pl.ds(pl.multiple_of(i * 256, 256), 256)
            c, s1, s2 = c_ref[rows, :], s1_ref[rows, :], s2_ref[rows, :]
            for g in range(3):
                base = g * 384
                dp_ref[rows, base:base + LANES] = _rope_bwd(dq_s[g, rows, :], c, s1, s2).astype(BF16)
                dp_ref[rows, base + LANES:base + 2 * LANES] = _rope_bwd(dk_s[g, rows, :], c, s1, s2).astype(BF16)
                dp_ref[rows, base + 2 * LANES:base + 3 * LANES] = dv_s[g, rows, :].astype(BF16)
            return 0

        lax.fori_loop(0, S // 256, fin, 0)

    def spec(g, t):
        return pl.BlockSpec((S, LANES), lambda p_: (0, (p_ * 3 + g) * 3 + t))

    pair = pl.BlockSpec((S, LANES), lambda p_: (0, p_))
    tab = pl.BlockSpec((S, LANES), lambda p_: (0, 0))
    return _hosted_call(
        body, comm, [p] * 9 + [dyb, yb, lse, *tabs, dp], name="dil_bwd", grid=(N_DIL_PAIRS,),
        in_specs=[spec(g, t) for g in range(3) for t in range(3)] + [pair, pair, pair, tab, tab, tab, pl.BlockSpec(memory_space=pl.ANY)],
        out_specs=pl.BlockSpec((S, 1152), lambda p_: (0, p_)),
        out_shape=_sds((S, LAY_N), BF16),
        scratch_shapes=[pltpu.VMEM((3, S, LANES), F32)] * 3 + [pltpu.VMEM((S, LANES), F32)],
        aliases={15: 0}, vmem_mb=VMEM_MB)


def in_bwd_tail(dh1, x, dx1, mod, g_mix, comm=None):
    tm = 256
    nsteps = S // tm

    def body(dh_ref, x_ref, dx1_ref, mod_ref, g_ref, dx_ref, cs_ref, s_cs):
        i = pl.program_id(0)

        @pl.when(i == 0)
        def _():
            s_cs[...] = jnp.zeros_like(s_cs)

        xv, g, dh = x_ref[...], g_ref[...], dh_ref[...]
        r = lax.rsqrt(jnp.mean(xv * xv, axis=1, keepdims=True) + EPS)
        xh = xv * r
        s_cs[0] += _colsum8(dh)
        s_cs[1] += _colsum8(dh * (xh * g))
        dn = dh * (1.0 + mod_ref[1:2, :])
        s_cs[2] += _colsum8(dn * xh)
        gd = dn * g
        dx_ref[...] = dx1_ref[...] + r * (gd - xh * jnp.mean(gd * xh, axis=1, keepdims=True))

        @pl.when(i == nsteps - 1)
        def _():
            for t in range(3):
                cs_ref[t:t + 1, :] = jnp.sum(s_cs[t], axis=0, keepdims=True)
            cs_ref[3:8, :] = jnp.zeros((5, D), F32)

    row = pl.BlockSpec((tm, D), lambda i: (i, 0))
    return _hosted_call(
        body, comm, (dh1, x, dx1, mod, g_mix), name="in_bwd_tail", grid=(nsteps,),
        in_specs=[row, row, row, pl.BlockSpec((8, D), lambda i: (0, 0)), pl.BlockSpec((1, D), lambda i: (0, 0))],
        out_specs=[row, pl.BlockSpec((8, D), lambda i: (0, 0))],
        out_shape=[_sds((S, D), F32), _sds((8, D), F32)],
        scratch_shapes=[pltpu.VMEM((3, 8, D), F32)])


def _lay_pieces():
    out = []
    qa, ka, va, fa, qb, kb, vb, ga = 0, 512, 1024, 1536, 1544, 2312, 3080, 3848
    for p in range(N_DIL_PAIRS):
        for g in range(3):
            base = LAY_B + (p * 3 + g) * 384
            hd0 = (4 * g + 2 * p) * HD
            out += [(base, qb + hd0, LANES), (base + LANES, kb + hd0, LANES), (base + 2 * LANES, vb + hd0, LANES)]
    for p in range(N_FOX_PAIRS):
        base = LAY_A + p * 384
        out += [(base, qa + p * LANES, LANES), (base + LANES, ka + p * LANES, LANES), (base + 2 * LANES, va + p * LANES, LANES)]
    out.append((LAY_F, fa, 8))
    out.append((LAY_G, ga, 2 * D))
    return out


def lay_from_nat(w_nat):
    parts, pos = [], 0
    for lay, nat, width in sorted(_lay_pieces()):
        if lay > pos:
            parts.append(jnp.zeros((w_nat.shape[0], lay - pos), w_nat.dtype))
        parts.append(w_nat[:, nat:nat + width])
        pos = lay + width
    if pos < LAY_N:
        parts.append(jnp.zeros((w_nat.shape[0], LAY_N - pos), w_nat.dtype))
    return jnp.concatenate(parts, axis=1)


def nat_from_lay(w_lay):
    parts = [w_lay[:, lay:lay + width] for lay, nat, width in sorted(_lay_pieces(), key=lambda t: t[1])]
    return jnp.concatenate(parts, axis=1)


def _pos():
    return lax.axis_index("x"), lax.axis_index("y"), lax.axis_index("c")


def _other_chips(x, y):
    return [(1 - x, y), (x, 1 - y), (1 - x, 1 - y)]


def _remote(src, dst, send_sem, recv_sem, dev):
    return pltpu.make_async_remote_copy(src_ref=src, dst_ref=dst, send_sem=send_sem, recv_sem=recv_sem,
                                        device_id=dev, device_id_type=MESH)


VMEM_SPEC = pl.BlockSpec(memory_space=pltpu.VMEM)
ANY_SPEC = pl.BlockSpec(memory_space=pl.ANY)


def gather_all(v, name, with_sum):
    r = v.shape[0]

    def body(v_ref, out_ref, *rest):
        send_s, recv_s = rest[-2:]
        x, y, c = _pos()
        me = 4 * x + 2 * y + c
        out_ref[me] = v_ref[...]
        peers = []
        for m in range(1, 8):
            px = 1 - x if m & 4 else x
            py = 1 - y if m & 2 else y
            pc = 1 - c if m & 1 else c
            peers.append((px, py, pc))
        copies = [_remote(v_ref, out_ref.at[me], send_s.at[i], recv_s.at[i], dev) for i, dev in enumerate(peers)]
        for cp in copies:
            cp.start()
        for i, (px, py, pc) in enumerate(peers):
            _remote(v_ref, out_ref.at[4 * px + 2 * py + pc], send_s.at[i], recv_s.at[i], (px, py, pc)).wait_recv()
        for cp in copies:
            cp.wait_send()
        if with_sum:
            acc = out_ref[0]
            for b in range(1, 8):
                acc = acc + out_ref[b]
            rest[0][...] = acc

    out_shape = [_sds((8, r, LANES), F32)] + ([_sds((r, LANES), F32)] if with_sum else [])
    return pl.pallas_call(
        body, name=name, in_specs=[VMEM_SPEC], out_specs=[VMEM_SPEC] * len(out_shape), out_shape=out_shape,
        scratch_shapes=[pltpu.SemaphoreType.DMA((7,)), pltpu.SemaphoreType.DMA((7,))],
    )(v)


def mod_exchange(c_all, w_ada_sh, b_sh):
    def body(c_ref, w_ref, b_ref, out_ref, sc_ref, modp, send_s, recv_s):
        cv = c_ref[...]
        sc = cv * _sigmoid(cv)
        sc_ref[...] = sc
        modp[...] = jnp.dot(sc, w_ref[...], precision=lax.Precision.HIGHEST, preferred_element_type=F32) + b_ref[...]
        x, y, c = _pos()
        k = 2 * x + y
        out_ref[k] = modp[...]
        chips = _other_chips(x, y)
        copies = [_remote(modp, out_ref.at[k], send_s.at[j], recv_s.at[j], (cx, cy, c)) for j, (cx, cy) in enumerate(chips)]
        for cp in copies:
            cp.start()
        for j, (cx, cy) in enumerate(chips):
            _remote(modp, out_ref.at[2 * cx + cy], send_s.at[j], recv_s.at[j], (cx, cy, c)).wait_recv()
        for cp in copies:
            cp.wait_send()

    n = w_ada_sh.shape[1]
    return pl.pallas_call(
        body, name="mod_exchange", in_specs=[VMEM_SPEC] * 3, out_specs=[VMEM_SPEC] * 2,
        out_shape=[_sds((4, 8, n), F32), _sds((8, D), F32)],
        scratch_shapes=[pltpu.VMEM((8, n), F32), pltpu.SemaphoreType.DMA((3,)), pltpu.SemaphoreType.DMA((3,))],
        compiler_params=_params(VMEM_MB),
    )(c_all, w_ada_sh, b_sh)


def gather_weights(shards):
    n = len(shards)

    def body(*refs):
        ins, outs = refs[:n], refs[n:2 * n]
        send_s, recv_s, fsend_s, frecv_s, loc_s = refs[2 * n:]
        x, y, c = _pos()
        k = 2 * x + y
        chips = _other_chips(x, y)
        local, sends, fwds = [], [], []
        for a in range(n):
            half = ins[a].shape[0] // 2
            rows = pl.ds(c * half, half)
            lc = pltpu.make_async_copy(ins[a], outs[a].at[k], loc_s.at[a])
            lc.start()
            local.append(lc)
            for j, (cx, cy) in enumerate(chips):
                cp = _remote(ins[a].at[rows], outs[a].at[k, rows], send_s.at[3 * a + j], recv_s.at[3 * a + j], (cx, cy, c))
                cp.start()
                sends.append(cp)
        for a in range(n):
            half = ins[a].shape[0] // 2
            rows = pl.ds(c * half, half)
            for j, (cx, cy) in enumerate(chips):
                kj = 2 * cx + cy
                _remote(ins[a].at[rows], outs[a].at[kj, rows], send_s.at[3 * a + j], recv_s.at[3 * a + j], (cx, cy, c)).wait_recv()
                fw = _remote(outs[a].at[kj, rows], outs[a].at[kj, rows], fsend_s.at[3 * a + j], frecv_s.at[3 * a + j], (x, y, 1 - c))
                fw.start()
                fwds.append(fw)
        for a in range(n):
            half = ins[a].shape[0] // 2
            orows = pl.ds((1 - c) * half, half)
            for j, (cx, cy) in enumerate(chips):
                kj = 2 * cx + cy
                _remote(outs[a].at[kj, orows], outs[a].at[kj, orows], fsend_s.at[3 * a + j], frecv_s.at[3 * a + j], (x, y, 1 - c)).wait_recv()
        for cp in sends + fwds:
            cp.wait_send()
        for lc in local:
            lc.wait()

    return pl.pallas_call(
        body, name="gather_weights", in_specs=[ANY_SPEC] * n, out_specs=[ANY_SPEC] * n,
        out_shape=[_sds((4,) + s.shape, s.dtype) for s in shards],
        scratch_shapes=[pltpu.SemaphoreType.DMA((3 * n,))] * 4 + [pltpu.SemaphoreType.DMA((n,))],
    )(*shards)


def _row_tile(rows, cap=256):
    t = cap
    while rows % t or t % 8:
        t -= 8
    return t


def _comm_wait(sends, recvs, local=()):
    for cp in recvs:
        cp.wait_recv()
    for cp in sends:
        cp.wait_send()
    for cp in local:
        cp.wait()


def ag_ici(shards):
    n = len(shards)

    def copies(ins, outs, sems):
        send_s, recv_s, loc_s = sems
        x, y, c = _pos()
        k = 2 * x + y
        sends, recvs, local = [], [], []
        for a in range(n):
            half = ins[a].shape[0] // 2
            rows = pl.ds(c * half, half)
            local.append(pltpu.make_async_copy(ins[a], outs[a].at[k], loc_s.at[a]))
            for j, (cx, cy) in enumerate(_other_chips(x, y)):
                sem = (send_s.at[3 * a + j], recv_s.at[3 * a + j], (cx, cy, c))
                sends.append(_remote(ins[a].at[rows], outs[a].at[k, rows], *sem))
                recvs.append(_remote(ins[a].at[rows], outs[a].at[2 * cx + cy, rows], *sem))
        return sends, recvs, local

    def start(ins, outs, sems):
        sends, _, local = copies(ins, outs, sems)
        for cp in local + sends:
            cp.start()

    def wait(ins, outs, sems):
        _comm_wait(*copies(ins, outs, sems))

    return Comm(shards, [_sds((4,) + s.shape, s.dtype) for s in shards], [3 * n, 3 * n, n], start, wait)


def ag_d2d(bufs):
    n = len(bufs)

    def copies(ins, outs, sems):
        send_s, recv_s = sems
        x, y, c = _pos()
        sends, recvs = [], []
        for a in range(n):
            half = outs[a].shape[1] // 2
            rows, orows = pl.ds(c * half, half), pl.ds((1 - c) * half, half)
            for j, (cx, cy) in enumerate(_other_chips(x, y)):
                kj = 2 * cx + cy
                sem = (send_s.at[3 * a + j], recv_s.at[3 * a + j], (x, y, 1 - c))
                sends.append(_remote(outs[a].at[kj, rows], outs[a].at[kj, rows], *sem))
                recvs.append(_remote(outs[a].at[kj, orows], outs[a].at[kj, orows], *sem))
        return sends, recvs

    def start(ins, outs, sems):
        for cp in copies(ins, outs, sems)[0]:
            cp.start()

    def wait(ins, outs, sems):
        _comm_wait(*copies(ins, outs, sems))

    return Comm(bufs, [_sds(b.shape, b.dtype) for b in bufs], [3 * n, 3 * n], start, wait, aliases={a: a for a in range(n)})


def rs_a(grads):
    n = len(grads)

    def copies(ins, outs, sems):
        send_s, recv_s = sems
        x, y, c = _pos()
        cps = []
        for a in range(n):
            half = ins[a].shape[1] // 2
            cps.append(_remote(ins[a].at[:, pl.ds((1 - c) * half, half), :], outs[a], send_s.at[a], recv_s.at[a], (x, y, 1 - c)))
        return cps

    def start(ins, outs, sems):
        for cp in copies(ins, outs, sems):
            cp.start()

    def wait(ins, outs, sems):
        cps = copies(ins, outs, sems)
        _comm_wait(cps, cps)

    return Comm(grads, [_sds((4, g.shape[1] // 2, g.shape[2]), g.dtype) for g in grads], [n, n], start, wait)


def rs_b(pres):
    n = len(pres)

    def copies(ins, outs, sems):
        send_s, recv_s, loc_s = sems
        x, y, c = _pos()
        k = 2 * x + y
        cps, local = [], []
        for a in range(n):
            local.append(pltpu.make_async_copy(ins[a].at[k], outs[a].at[3], loc_s.at[a]))
            for j, (cx, cy) in enumerate(_other_chips(x, y)):
                cps.append(_remote(ins[a].at[2 * cx + cy], outs[a].at[j], send_s.at[3 * a + j], recv_s.at[3 * a + j], (cx, cy, c)))
        return cps, local

    def start(ins, outs, sems):
        cps, local = copies(ins, outs, sems)
        for cp in local + cps:
            cp.start()

    def wait(ins, outs, sems):
        cps, local = copies(ins, outs, sems)
        _comm_wait(cps, cps, local)

    return Comm(pres, [_sds(p_.shape, p_.dtype) for p_ in pres], [3 * n, 3 * n, n], start, wait)


def rs_c(reds):
    n = len(reds)

    def copies(ins, outs, sems):
        send_s, recv_s, loc_s = sems
        x, y, c = _pos()
        sends, recvs, local = [], [], []
        for a in range(n):
            half = ins[a].shape[0]
            rows, orows = pl.ds(c * half, half), pl.ds((1 - c) * half, half)
            local.append(pltpu.make_async_copy(ins[a], outs[a].at[rows], loc_s.at[a]))
            sem = (send_s.at[a], recv_s.at[a], (x, y, 1 - c))
            sends.append(_remote(ins[a], outs[a].at[rows], *sem))
            recvs.append(_remote(ins[a], outs[a].at[orows], *sem))
        return sends, recvs, local

    def start(ins, outs, sems):
        sends, _, local = copies(ins, outs, sems)
        for cp in local + sends:
            cp.start()

    def wait(ins, outs, sems):
        _comm_wait(*copies(ins, outs, sems))

    return Comm(reds, [_sds((2 * r_.shape[0], r_.shape[1]), r_.dtype) for r_ in reds], [n, n, n], start, wait)


def comm_only(comm, name):
    nci, nco = len(comm.ins), len(comm.out_shapes)

    def body(*refs):
        ins, outs, sems = refs[:nci], refs[nci:nci + nco], refs[nci + nco:]
        comm.start(ins, outs, sems)
        comm.wait(ins, outs, sems)

    return pl.pallas_call(
        body, name=name, in_specs=[ANY_SPEC] * nci, out_specs=[ANY_SPEC] * nco, out_shape=comm.out_shapes,
        scratch_shapes=[pltpu.SemaphoreType.DMA((s,)) for s in comm.sems],
        input_output_aliases=comm.aliases,
    )(*comm.ins)


def rs_add_halves(g, other, core, name):
    _, r, cdim = g.shape
    half = r // 2
    tr = _row_tile(half, 128)
    nb = half // tr

    def body(core_ref, g_ref, o_ref, out_ref):
        del core_ref
        out_ref[...] = (g_ref[...] + o_ref[...]).astype(BF16)

    grid_spec = pltpu.PrefetchScalarGridSpec(
        num_scalar_prefetch=1, grid=(4, nb),
        in_specs=[pl.BlockSpec((None, tr, cdim), lambda k, i, cr: (k, cr[0] * nb + i, 0)),
                  pl.BlockSpec((None, tr, cdim), lambda k, i, cr: (k, i, 0))],
        out_specs=pl.BlockSpec((None, tr, cdim), lambda k, i, cr: (k, i, 0)))
    return pl.pallas_call(body, name=name, grid_spec=grid_spec, out_shape=_sds((4, half, cdim), BF16))(core, g, other)


def rs_add_slabs(t, name):
    _, half, cdim = t.shape
    tr = _row_tile(half, 128)

    def body(t_ref, out_ref):
        s = [t_ref[i].astype(F32) for i in range(4)]
        out_ref[...] = ((s[3] + s[0]) + s[1]) + s[2]

    return pl.pallas_call(
        body, name=name, grid=(half // tr,),
        in_specs=[pl.BlockSpec((4, tr, cdim), lambda i: (0, i, 0))],
        out_specs=pl.BlockSpec((tr, cdim), lambda i: (i, 0)),
        out_shape=_sds((half, cdim), F32),
    )(t)


def _adam_math(w, g, m, v):
    m = ADAM_B1 * m + (1.0 - ADAM_B1) * g
    v = ADAM_B2 * v + (1.0 - ADAM_B2) * (g * g)
    m_hat = m / (1.0 - ADAM_B1 ** ADAM_STEP)
    v_hat = v / (1.0 - ADAM_B2 ** ADAM_STEP)
    delta = -ADAM_LR * (m_hat / (jnp.sqrt(v_hat) + ADAM_EPS) + ADAM_WD * w)
    return delta, m, v


def adam(w, g, m, v, name):
    r, cdim = w.shape
    tr = _row_tile(r) if r >= 8 else r

    def body(w_ref, g_ref, m_ref, v_ref, g_out, d_ref, nm_ref, nv_ref):
        gv = g_ref[:, :cdim]
        g_out[...] = gv
        d_ref[...], nm_ref[...], nv_ref[...] = _adam_math(w_ref[...], gv, m_ref[...], v_ref[...])

    blk = pl.BlockSpec((tr, cdim), lambda i: (i, 0))
    return pl.pallas_call(
        body, name=name, grid=(r // tr,), in_specs=[blk, pl.BlockSpec((tr, g.shape[1]), lambda i: (i, 0)), blk, blk],
        out_specs=[blk] * 4, out_shape=[_sds((r, cdim), F32)] * 4,
    )(w, g, m, v)


def adam_w_ada(sc_t, dmod_sh, w, m, v):
    r, cdim = w.shape
    tr = 256

    def body(s_ref, d_ref, w_ref, m_ref, v_ref, g_ref, dl_ref, nm_ref, nv_ref):
        g = jnp.dot(s_ref[...], d_ref[...], precision=lax.Precision.HIGHEST, preferred_element_type=F32)
        g_ref[...] = g
        dl_ref[...], nm_ref[...], nv_ref[...] = _adam_math(w_ref[...], g, m_ref[...], v_ref[...])

    blk = pl.BlockSpec((tr, cdim), lambda i: (i, 0))
    return pl.pallas_call(
        body, name="adam_w_ada", grid=(r // tr,),
        in_specs=[pl.BlockSpec((tr, LANES), lambda i: (i, 0)), pl.BlockSpec((LANES, cdim), lambda i: (0, 0)), blk, blk, blk],
        out_specs=[blk] * 4, out_shape=[_sds((r, cdim), F32)] * 4,
    )(sc_t, dmod_sh, w, m, v)


SMALL_ROWS = 80


def kernel(x, c, w_ada, b_ada, g_mix, w_in, b_fgate, w_br_a, w_br_b, w_out, g_ffn, w_ffn_gate, w_ffn_up, w_ffn_down, g_final, loss_target, m_w_ada, m_b_ada, m_g_mix, m_w_in, m_b_fgate, m_w_br_a, m_w_br_b, m_w_out, m_g_ffn, m_w_ffn_gate, m_w_ffn_up, m_w_ffn_down, m_g_final, v_w_ada, v_b_ada, v_g_mix, v_w_in, v_b_fgate, v_w_br_a, v_w_br_b, v_w_out, v_g_ffn, v_w_ffn_gate, v_w_ffn_up, v_w_ffn_down, v_g_final):
    xi, yi, ci = _pos()
    chip = 2 * xi + yi
    seq = 4 * xi + 2 * yi + ci
    n_ada = w_ada.shape[2]

    c_all = gather_all(c.reshape(8, LANES), "gather_c", False)[0].reshape(8, D)
    b_sh = lax.dynamic_slice(b_ada, (0, chip * n_ada), (1, n_ada))
    mod_all, sc = mod_exchange(c_all, w_ada[0], b_sh)
    mod = lax.dynamic_index_in_dim(mod_all, seq, axis=1, keepdims=False).reshape(6, D)
    mod8 = jnp.pad(mod, ((0, 2), (0, 0)))

    shards = [
        jnp.pad(w_in[0], ((0, 0), (0, IN_SHARD_PAD - IN_SHARD))), w_br_a[0], w_br_b[0], w_out[0],
        jnp.pad(w_ffn_gate[0], ((0, 0), (0, FF_PAD - FF_SHARD))), jnp.pad(w_ffn_up[0], ((0, 0), (0, FF_PAD - FF_SHARD))),
        jnp.pad(w_ffn_down[0], ((0, FF_PAD - FF_SHARD), (0, 0))),
    ]
    shards = [s.astype(BF16) for s in shards]
    core = ci.astype(jnp.int32).reshape(1)
    xs, tgt, g_fin = x[0], loss_target[0], g_final.reshape(1, D)

    g_in = gather_weights(shards[:1])[0]
    w_lay = lay_from_nat(jnp.concatenate([g_in[k][:, :IN_SHARD] for k in range(4)], axis=1))
    tabs = rope_tables()
    h1 = norm_mod_fwd(xs, g_mix, mod8, 0, 1)
    p = in_proj_fwd(h1, w_lay, tabs)
    frow, fraw, fcol = fgate_fwd(p, jnp.pad(b_fgate, ((0, 0), (0, LANES - 8))))
    (ya_att, gcol), bufs = fox_fwd(p, fcol, frow, comm=ag_ici(shards[1:]))
    (yb, lse_b), bufs = dil_fwd(p, comm=ag_d2d(bufs))
    g_bra, g_brb, g_out, w_gate, w_up, g_down = bufs
    w_bra = g_bra.transpose(1, 0, 2).reshape(512, D)
    w_brb = g_brb.transpose(1, 0, 2).reshape(256, D)
    w_o, w_down = g_out.reshape(D, D), g_down.reshape(FFP, D)
    merged, ya, ybp = merge_fwd(ya_att, yb, p, w_bra, w_brb)
    mix, x1, h2 = out_proj_fwd(merged, w_o, xs, mod8, g_ffn)
    a, u, z = ffn_up_fwd(h2, w_gate, w_up)
    dx2, dffn, dg_final, dga_f, loss_part = ffn_down_loss(z, w_down, x1, mod8, g_fin, tgt)

    da, du, dw_down = ffn_down_bwd(dffn, w_down, a, u, z)
    dh2a, _ = mm_nt(da, w_gate, "ffn_gate_dx")
    dh2b, _ = mm_nt(du, w_up, "ffn_up_dx")
    dw_gate = mm_tn(h2, da, "ffn_gate_dw", shard_major=True)
    dw_up = mm_tn(h2, du, "ffn_up_dw", shard_major=True)
    ffn_grads = [dw_gate, dw_up, dw_down.reshape(4, FF_PAD, D)]
    (dx1, dp1, dya_att, dyb, cs_mid, dw_out, dw_bra, dw_brb), oth = mid_bwd(
        dh2a, dh2b, x1, dx2, mix, mod8, g_ffn, p, ya, ybp, merged, ya_att, yb, w_o, w_bra, w_brb, comm=rs_a(ffn_grads))
    pres = [rs_add_halves(g, o, core, f"rs_ffn_halves_{i}") for i, (g, o) in enumerate(zip(ffn_grads, oth))]
    (dp2, dfrow, dfcol), slabs = fox_bwd(p, dya_att, ya_att, gcol, frow, dp1, comm=rs_b(pres))
    reds = [rs_add_slabs(t, f"rs_ffn_slabs_{i}") for i, t in enumerate(slabs)]
    dp3, db_fg = fgate_bwd(dfrow.reshape(8, S), dfcol, fraw, dp2)
    dp4, (r_gate, r_up, r_down) = dil_bwd(p, dyb, yb, lse_b, tabs, dp3, comm=rs_c(reds))

    dw_lay = mm_tn(h1, dp4, "in_proj_dw")
    dw_in = nat_from_lay(dw_lay).reshape(D, 4, IN_SHARD).transpose(1, 0, 2)
    mix_grads = [
        jnp.pad(dw_in, ((0, 0), (0, 0), (0, IN_SHARD_PAD - IN_SHARD))),
        dw_bra.reshape(512, 4, 256).transpose(1, 0, 2), dw_brb.reshape(256, 4, 256).transpose(1, 0, 2), dw_out.reshape(4, 256, D),
    ]
    dh1, oth = mm_nt(dp4, w_lay, "in_proj_dx", comm=rs_a(mix_grads))
    pres = [rs_add_halves(g, o, core, f"rs_mix_halves_{i}") for i, (g, o) in enumerate(zip(mix_grads, oth))]
    (dx, cs_in), slabs = in_bwd_tail(dh1, xs, dx1, mod8, g_mix, comm=rs_b(pres))
    reds = [rs_add_slabs(t, f"rs_mix_slabs_{i}") for i, t in enumerate(slabs)]
    r_in, r_bra, r_brb, r_out = comm_only(rs_c(reds), "rs_mix_share")
    gpad = dict(w_in=r_in, w_br_a=r_bra, w_br_b=r_brb, w_out=r_out, w_ffn_gate=r_gate, w_ffn_up=r_up, w_ffn_down=r_down)

    dmod = jnp.concatenate([cs_in[0:2], cs_mid[3:4], cs_mid[0:2], dga_f], axis=0)
    small = dict(dmod=dmod, dg_mix=cs_in[2:3], dg_ffn=cs_mid[2:3], dg_final=dg_final, db_fgate=db_fg[:, 0], loss=loss_part[0, 0])
    sv = jnp.concatenate([
        small["dmod"].reshape(48, LANES), small["dg_mix"].reshape(8, LANES), small["dg_ffn"].reshape(8, LANES),
        small["dg_final"].reshape(8, LANES), jnp.pad(small["db_fgate"], (0, LANES - 8)).reshape(1, LANES),
        jnp.broadcast_to(small["loss"], (1, LANES)), jnp.zeros((SMALL_ROWS - 74, LANES), F32)], axis=0)
    sv_all, sv_sum = gather_all(sv, "gather_small", True)
    loss = sv_sum[73, 0]
    g_small = dict(b_ada=sv_sum[0:48].reshape(1, 6 * D), g_mix=sv_sum[48:56].reshape(1, D), g_ffn=sv_sum[56:64].reshape(1, D),
                   g_final=sv_sum[64:72].reshape(D), b_fgate=sv_sum[72, 0:8].reshape(1, 8))

    dmod_all = lax.dynamic_slice(sv_all[:, 0:48, :].reshape(8, 6 * D), (0, chip * n_ada), (8, n_ada))
    g_ada, d_ada, nm_ada, nv_ada = adam_w_ada(jnp.pad(sc.T, ((0, 0), (0, LANES - 8))), jnp.pad(dmod_all, ((0, LANES - 8), (0, 0))),
                                              w_ada[0], m_w_ada[0], v_w_ada[0])

    big = dict(w_in=(w_in, m_w_in, v_w_in), w_br_a=(w_br_a, m_w_br_a, v_w_br_a), w_br_b=(w_br_b, m_w_br_b, v_w_br_b),
               w_out=(w_out, m_w_out, v_w_out), w_ffn_gate=(w_ffn_gate, m_w_ffn_gate, v_w_ffn_gate),
               w_ffn_up=(w_ffn_up, m_w_ffn_up, v_w_ffn_up), w_ffn_down=(w_ffn_down, m_w_ffn_down, v_w_ffn_down))
    upd = {nm: adam(w[0], gpad[nm], m[0], v[0], "adam_" + nm) for nm, (w, m, v) in big.items()}

    def pack(gm, gf, gl, ba, bf):
        rows = [gm.reshape(1, D), gf.reshape(1, D), gl.reshape(1, D), ba.reshape(6, D), jnp.pad(bf.reshape(1, 8), ((0, 0), (0, D - 8)))]
        return jnp.concatenate(rows + [jnp.zeros((6, D), F32)], axis=0)

    packed = adam(pack(g_mix, g_ffn, g_final, b_ada, b_fgate),
                  pack(g_small["g_mix"], g_small["g_ffn"], g_small["g_final"], g_small["b_ada"], g_small["b_fgate"]),
                  pack(m_g_mix, m_g_ffn, m_g_final, m_b_ada, m_b_fgate), pack(v_g_mix, v_g_ffn, v_g_final, v_b_ada, v_b_fgate),
                  "adam_small")

    def unpack(t):
        return dict(g_mix=t[0:1], g_ffn=t[1:2], g_final=t[2], b_ada=t[3:9].reshape(1, 6 * D), b_fgate=t[9:10, 0:8])

    small_upd = [unpack(t) for t in packed[1:]]
    order =["w_ada", "b_ada", "g_mix", "w_in", "b_fgate", "w_br_a", "w_br_b", "w_out", "g_ffn", "w_ffn_gate", "w_ffn_up", "w_ffn_down", "g_final"]

    def leaf(nm, which):
        if nm == "w_ada":
            return (g_ada, d_ada, nm_ada, nv_ada)[which][None]
        if nm in big:
            return upd[nm][which][None]
        return g_small[nm] if which == 0 else small_upd[which - 1][nm]

    outs = [loss, dx[None]]
    for which in range(4):
        outs += [leaf(nm, which) for nm in order]
    return tuple(outs)
```

```python
import functools

import numpy as np
import jax
import jax.numpy as jnp
from jax import lax
from jax.experimental import pallas as pl
from jax.experimental.pallas import tpu as pltpu

F32, BF16 = jnp.float32, jnp.bfloat16
S, D = 2048, 1024
HD = 64
LANES = 128
N_FOX_PAIRS, N_DIL_PAIRS = 4, 2
DIL_GROUPS = ((1, 16), (4, 4), (16, 1))
SPAN = 128
ROT_DIM, ROPE_THETA = 16, 500000.0
D_FF, FF_SHARD, FF_PAD = 2816, 704, 768
FFP = 4 * FF_PAD
IN_COLS, IN_SHARD, IN_SHARD_PAD = 5896, 1474, 1536
LAY_B, LAY_A, LAY_F, LAY_G, LAY_N = 0, 2304, 3840, 4096, 6144
EPS, NEG = 1e-6, -1e30
SCALE = HD ** -0.5
ADAM_LR, ADAM_B1, ADAM_B2, ADAM_EPS, ADAM_WD, ADAM_STEP = 0.001, 0.9, 0.999, 1e-08, 0.01, 10
VMEM_MB = 56
MESH = pl.DeviceIdType.MESH


def _params(vmem_mb=None, **kw):
    if vmem_mb is not None:
        kw["vmem_limit_bytes"] = vmem_mb * 1024 * 1024
    return pltpu.CompilerParams(**kw)


def _sds(shape, dtype):
    return jax.ShapeDtypeStruct(shape, dtype)


def _sigmoid(x):
    return 1.0 / (1.0 + jnp.exp(-x))


def _colsum8(x):
    tm, n = x.shape
    return jnp.sum(x.reshape(tm // 8, 8, n), axis=0)


class Comm:
    def __init__(self, ins, out_shapes, sems, start, wait, aliases=None):
        self.ins, self.out_shapes, self.sems = list(ins), list(out_shapes), list(sems)
        self.start, self.wait, self.aliases = start, wait, dict(aliases or {})


def _hosted_call(body, comm, args, *, name, grid, in_specs, out_specs, out_shape, scratch_shapes=(), aliases=None, vmem_mb=None):
    single = not isinstance(out_shape, (list, tuple))
    out_specs_l = [out_specs] if single else list(out_specs)
    out_shape_l = [out_shape] if single else list(out_shape)
    n_in, n_out, n_scr = len(in_specs), len(out_shape_l), len(scratch_shapes)
    aliases = dict(aliases or {})
    if comm is None:
        res = pl.pallas_call(body, name=name, grid=grid, in_specs=list(in_specs), out_specs=out_specs, out_shape=out_shape,
                             scratch_shapes=list(scratch_shapes), input_output_aliases=aliases,
                             compiler_params=_params(vmem_mb))(*args)
        return res, []
    nci, nco = len(comm.ins), len(comm.out_shapes)

    def wrapped(*refs):
        main_in, cin = refs[:n_in], refs[n_in:n_in + nci]
        o0 = n_in + nci
        main_out, cout = refs[o0:o0 + n_out], refs[o0 + n_out:o0 + n_out + nco]
        s0 = o0 + n_out + nco
        scr, sems = refs[s0:s0 + n_scr], refs[s0 + n_scr:]
        ids = [pl.program_id(i) for i in range(len(grid))]
        first = functools.reduce(jnp.logical_and, [i == 0 for i in ids])
        last = functools.reduce(jnp.logical_and, [i == g - 1 for i, g in zip(ids, grid)])

        @pl.when(first)
        def _():
            comm.start(cin, cout, sems)

        body(*main_in, *main_out, *scr)

        @pl.when(last)
        def _():
            comm.wait(cin, cout, sems)

    for ci, co in comm.aliases.items():
        aliases[n_in + ci] = n_out + co
    any_spec = pl.BlockSpec(memory_space=pl.ANY)
    res = pl.pallas_call(
        wrapped, name=name, grid=grid, in_specs=list(in_specs) + [any_spec] * nci, out_specs=out_specs_l + [any_spec] * nco,
        out_shape=out_shape_l + comm.out_shapes,
        scratch_shapes=list(scratch_shapes) + [pltpu.SemaphoreType.DMA((s,)) for s in comm.sems],
        input_output_aliases=aliases, compiler_params=_params(vmem_mb))(*args, *comm.ins)
    main = list(res[:n_out])
    return (main[0] if single else main), list(res[n_out:])


def norm_mod_fwd(x, g, mod, shift_row, scale_row):
    tm = 256

    def body(x_ref, g_ref, mod_ref, h_ref):
        xv = x_ref[...]
        r = lax.rsqrt(jnp.mean(xv * xv, axis=1, keepdims=True) + EPS)
        n = xv * r * g_ref[...]
        h = n * (1.0 + mod_ref[scale_row:scale_row + 1, :]) + mod_ref[shift_row:shift_row + 1, :]
        h_ref[...] = h.astype(BF16)

    return pl.pallas_call(
        body, name="norm_mod_fwd", grid=(S // tm,),
        in_specs=[pl.BlockSpec((tm, D), lambda i: (i, 0)), pl.BlockSpec((1, D), lambda i: (0, 0)),
                  pl.BlockSpec((8, D), lambda i: (0, 0))],
        out_specs=pl.BlockSpec((tm, D), lambda i: (i, 0)),
        out_shape=_sds((S, D), BF16),
    )(x, g, mod)


def rope_tables():
    pos = jnp.arange(S, dtype=F32)
    inv_freq = ROPE_THETA ** (-jnp.arange(0, ROT_DIM, 2, dtype=F32) / ROT_DIM)
    ang = pos[:, None] * inv_freq[None, :]
    cos, sin = jnp.cos(ang), jnp.sin(ang)
    one, zero = jnp.ones((S, HD - ROT_DIM), F32), jnp.zeros((S, HD - ROT_DIM), F32)
    z8 = jnp.zeros((S, 8), F32)
    c = jnp.concatenate([cos, cos, one], axis=1)
    s1 = jnp.concatenate([-sin, z8, zero], axis=1)
    s2 = jnp.concatenate([z8, sin, zero], axis=1)
    return tuple(jnp.concatenate([t, t], axis=1) for t in (c, s1, s2))


def _rope(y, c, s1, s2):
    return y * c + pltpu.roll(y, LANES - 8, 1) * s1 + pltpu.roll(y, 8, 1) * s2


def _rope_bwd(dy, c, s1, s2):
    return dy * c + pltpu.roll(dy * s1, 8, 1) + pltpu.roll(dy * s2, LANES - 8, 1)


def in_proj_fwd(h, w_lay, tabs):
    tm, tn = 2048, 384
    n_rope = 2 * N_DIL_PAIRS * 3 // 2

    def body(a_ref, w_ref, c_ref, s1_ref, s2_ref, o_ref):
        j = pl.program_id(0)
        y = jnp.dot(a_ref[...], w_ref[...], preferred_element_type=F32)

        @pl.when(j < n_rope)
        def _():
            c, s1, s2 = c_ref[...], s1_ref[...], s2_ref[...]
            for t in range(2):
                o_ref[:, LANES * t:LANES * (t + 1)] = _rope(y[:, LANES * t:LANES * (t + 1)], c, s1, s2)
            o_ref[:, 2 * LANES:] = y[:, 2 * LANES:]

        @pl.when(j >= n_rope)
        def _():
            o_ref[...] = y

    tab = pl.BlockSpec((tm, LANES), lambda j, i: (i, 0))
    return pl.pallas_call(
        body, name="in_proj_fwd", grid=(LAY_N // tn, S // tm),
        in_specs=[pl.BlockSpec((tm, D), lambda j, i: (i, 0)), pl.BlockSpec((D, tn), lambda j, i: (0, j)), tab, tab, tab],
        out_specs=pl.BlockSpec((tm, tn), lambda j, i: (i, j)),
        out_shape=_sds((S, LAY_N), F32), compiler_params=_params(VMEM_MB),
    )(h, w_lay, *tabs)


def _log1p_small(t):
    return jnp.where(t < 1e-2, t * (1.0 - t * (0.5 - t * (1.0 / 3.0))), jnp.log(1.0 + t))


def fgate_fwd(p, b_pad):
    def body(fa_ref, b_ref, frow_ref, fraw_ref, fcol_ref):
        f = fa_ref[...] + b_ref[...]
        fr = f.T[0:8, :]
        ls = jnp.minimum(fr, 0.0) - _log1p_small(jnp.exp(-jnp.abs(fr)))
        lane = lax.broadcasted_iota(jnp.int32, (8, S), 1)
        acc, sh = ls, 1
        while sh < S:
            acc = acc + jnp.where(lane >= sh, pltpu.roll(acc, sh, 1), 0.0)
            sh *= 2
        frow_ref[...] = acc
        fraw_ref[...] = fr
        for hh in range(8):
            fcol_ref[hh] = jnp.broadcast_to(acc[hh:hh + 1, :], (LANES, S)).T

    return pl.pallas_call(
        body, name="fgate_fwd", grid=(1,),
        in_specs=[pl.BlockSpec((S, LANES), lambda i: (0, LAY_F // LANES)), pl.BlockSpec((1, LANES), lambda i: (0, 0))],
        out_specs=[pl.BlockSpec((8, S), lambda i: (0, 0)), pl.BlockSpec((8, S), lambda i: (0, 0)),
                   pl.BlockSpec((8, S, LANES), lambda i: (0, 0, 0))],
        out_shape=[_sds((8, S), F32), _sds((8, S), F32), _sds((8, S, LANES), F32)],
        compiler_params=_params(VMEM_MB),
    )(p, b_pad)


def _head_masks(rows):
    lane = lax.broadcasted_iota(jnp.int32, (rows, LANES), 1)
    return lane < HD, lane >= HD


FT = 256
FS = 256


def _split3(f):
    hi = f.astype(BF16).astype(F32)
    r = f - hi
    mid = r.astype(BF16).astype(F32)
    return hi, mid, r - mid


def _fox_operands(qkv_ref, tcol_ref, scol_ref, qa_s, ka_s):
    rows = 256
    lane = lax.broadcasted_iota(jnp.int32, (rows, LANES), 1)

    def chunk(i, _):
        r = pl.ds(pl.multiple_of(i * rows, rows), rows)
        q, k = qkv_ref[r, 0:LANES], qkv_ref[r, LANES:2 * LANES]
        for hh in range(2):
            own = (lane < HD) if hh == 0 else (lane >= HD)
            b = HD if hh == 0 else 0
            t3, s3 = _split3(tcol_ref[hh, r, :]), _split3(scol_ref[hh, r, :])
            qa = jnp.where(lane < b + 3, 1.0, jnp.where(lane == b + 3, t3[0], jnp.where(lane == b + 4, t3[1], jnp.where(lane == b + 5, t3[2], 0.0))))
            ka = jnp.where(lane == b, -s3[0], jnp.where(lane == b + 1, -s3[1], jnp.where(lane == b + 2, -s3[2], jnp.where(lane < b + 6, 1.0, 0.0))))
            qa_s[hh, r, :] = jnp.where(own, q * SCALE, qa).astype(BF16)
            ka_s[hh, r, :] = jnp.where(own, k, ka).astype(BF16)
        return 0

    lax.fori_loop(0, S // rows, chunk, 0)


def fox_fwd(p, fcol, comm=None):
    nt = (((1,), (1,)), ((), ()))

    def body(qkv_ref, fc_ref, o_ref, g_ref, qa_s, ka_s):
        _fox_operands(qkv_ref, fc_ref, fc_ref, qa_s, ka_s)
        masks = _head_masks(FT)
        causal = lax.broadcasted_iota(jnp.int32, (FT, FT), 1) <= lax.broadcasted_iota(jnp.int32, (FT, FT), 0)

        def qloop(qi, _):
            q0 = pl.multiple_of(qi * FT, FT)
            qa = [qa_s[hh, pl.ds(q0, FT), :] for hh in range(2)]

            def step(kb, carry, diagonal):
                k0 = pl.multiple_of(kb * FT, FT)
                v = qkv_ref[pl.ds(k0, FT), 2 * LANES:3 * LANES].astype(BF16)
                out = []
                for hh in range(2):
                    ka = ka_s[hh, pl.ds(k0, FT), :]
                    ms, ls, accs = [], [], []
                    for r in range(FT // FS):
                        rows = slice(r * FS, (r + 1) * FS)
                        m, l, acc = [t[rows] for t in carry[3 * hh:3 * hh + 3]]
                        s = lax.dot_general(qa[hh][rows], ka, nt, preferred_element_type=F32)
                        if diagonal:
                            s = jnp.where(causal[rows], s, NEG)
                        m_new = jnp.maximum(m, jnp.max(s, axis=1, keepdims=True))
                        pr = jnp.exp(s - m_new)
                        alpha = jnp.exp(m - m_new)
                        ms.append(m_new)
                        ls.append(l * alpha + jnp.sum(pr, axis=1, keepdims=True))
                        accs.append(acc * alpha + jnp.dot(pr.astype(BF16), v, preferred_element_type=F32))
                    out += [jnp.concatenate(t, axis=0) for t in (ms, ls, accs)]
                return tuple(out)

            init = (jnp.full((FT, 1), NEG, F32), jnp.zeros((FT, 1), F32), jnp.zeros((FT, LANES), F32)) * 2
            carry = lax.fori_loop(0, qi, lambda kb, cr: step(kb, cr, False), init)
            m0, l0, a0, m1, l1, a1 = step(qi, carry, True)
            o_ref[pl.ds(q0, FT), :] = jnp.where(masks[0], a0 / l0, a1 / l1).astype(BF16)
            g_ref[0, pl.ds(q0, FT), :] = fc_ref[0, pl.ds(q0, FT), :] - (m0 + jnp.log(l0))
            g_ref[1, pl.ds(q0, FT), :] = fc_ref[1, pl.ds(q0, FT), :] - (m1 + jnp.log(l1))
            return 0

        lax.fori_loop(0, S // FT, qloop, 0)

    a_blk = LAY_A // 384
    return _hosted_call(
        body, comm, (p, fcol), name="fox_fwd", grid=(N_FOX_PAIRS,),
        in_specs=[pl.BlockSpec((S, 384), lambda p_: (0, a_blk + p_)), pl.BlockSpec((2, S, LANES), lambda p_: (p_, 0, 0))],
        out_specs=[pl.BlockSpec((S, LANES), lambda p_: (0, p_)), pl.BlockSpec((2, S, LANES), lambda p_: (p_, 0, 0))],
        out_shape=[_sds((S, 4 * LANES), BF16), _sds((8, S, LANES), F32)],
        scratch_shapes=[pltpu.VMEM((2, S, LANES), BF16)] * 2,
        vmem_mb=VMEM_MB)


def _dil_rows(ref, start, d):
    return ref[pl.ds(start, SPAN), :] if d == 1 else ref[pl.ds(start, SPAN, stride=d), :]


def _dil_store(ref, start, d, val):
    if d == 1:
        ref[pl.ds(start, SPAN), :] = val
    else:
        ref[pl.ds(start, SPAN, stride=d), :] = val


def _band_mask(has_prev):
    qi = lax.broadcasted_iota(jnp.int32, (SPAN, 2 * SPAN), 0) + SPAN
    kj = lax.broadcasted_iota(jnp.int32, (SPAN, 2 * SPAN), 1)
    dist = qi - kj
    return (dist >= 0) & (dist <= SPAN) & (has_prev | (kj >= SPAN))


def _dil_block(n, d, nb):
    r, j = n // nb, n % nb
    start = r + d * SPAN * j
    prev = jnp.maximum(start - d * SPAN, r)
    return start, prev, j > 0


def dil_fwd(p, comm=None):
    def body(*refs):
        qkv = [refs[3 * g:3 * g + 3] for g in range(3)]
        y_ref, lse_ref = refs[9], refs[10]
        acc_s, m_s, l_s = refs[11], refs[12], refs[13]
        masks = _head_masks(SPAN)
        for g, (d, nb) in enumerate(DIL_GROUPS):
            q_ref, k_ref, v_ref = qkv[g]

            def blk(n, _):
                start, prev, has_prev = _dil_block(n, d, nb)
                q = _dil_rows(q_ref, start, d)
                kc = jnp.concatenate([_dil_rows(k_ref, prev, d), _dil_rows(k_ref, start, d)], axis=0).astype(BF16)
                vc = jnp.concatenate([_dil_rows(v_ref, prev, d), _dil_rows(v_ref, start, d)], axis=0).astype(BF16)
                valid = _band_mask(has_prev)
                accs, ms, ls = [], [], []
                for hh in range(2):
                    qm = (jnp.where(masks[hh], q, 0.0) * SCALE).astype(BF16)
                    s = lax.dot_general(qm, kc, (((1,), (1,)), ((), ())), preferred_element_type=F32)
                    s = jnp.where(valid, s, NEG)
                    m = jnp.max(s, axis=1, keepdims=True)
                    pr = jnp.exp(s - m)
                    ls.append(jnp.sum(pr, axis=1, keepdims=True))
                    ms.append(m)
                    accs.append(jnp.dot(pr.astype(BF16), vc, preferred_element_type=F32))
                _dil_store(acc_s.at[g], start, d, jnp.where(masks[0], accs[0], accs[1]))
                _dil_store(m_s.at[g], start, d, jnp.where(masks[0], ms[0], ms[1]))
                _dil_store(l_s.at[g], start, d, jnp.where(masks[0], ls[0], ls[1]))
                return 0

            lax.fori_loop(0, 16, blk, 0)

        def merge(i, _):
            rows = pl.ds(pl.multiple_of(i * 256, 256), 256)
            m = [m_s[g, rows, :] for g in range(3)]
            mx = jnp.maximum(jnp.maximum(m[0], m[1]), m[2])
            w = [jnp.exp(m[g] - mx) for g in range(3)]
            l = sum(l_s[g, rows, :] * w[g] for g in range(3))
            y_ref[rows, :] = sum(acc_s[g, rows, :] * w[g] for g in range(3)) / l
            lse_ref[rows, :] = mx + jnp.log(l)
            return 0

        lax.fori_loop(0, S // 256, merge, 0)

    def spec(g, t):
        return pl.BlockSpec((S, LANES), lambda p_: (0, (p_ * 3 + g) * 3 + t))

    return _hosted_call(
        body, comm, [p] * 9, name="dil_fwd", grid=(N_DIL_PAIRS,),
        in_specs=[spec(g, t) for g in range(3) for t in range(3)],
        out_specs=[pl.BlockSpec((S, LANES), lambda p_: (0, p_)), pl.BlockSpec((S, LANES), lambda p_: (0, p_))],
        out_shape=[_sds((S, 2 * LANES), F32), _sds((S, 2 * LANES), F32)],
        scratch_shapes=[pltpu.VMEM((3, S, LANES), F32)] * 3,
        vmem_mb=VMEM_MB)


def merge_fwd(ya_att, yb, p, w_bra, w_brb):
    tm = 256
    gblk = LAY_G // D

    def body(a_ref, b_ref, ga_ref, gb_ref, wa_ref, wb_ref, mg_ref, ya_ref, yb_ref):
        ya = jnp.dot(a_ref[...], wa_ref[...], preferred_element_type=F32)
        ybp = jnp.dot(b_ref[...].astype(BF16), wb_ref[...], preferred_element_type=F32)
        mg_ref[...] = (_sigmoid(ga_ref[...]) * ya + _sigmoid(gb_ref[...]) * ybp).astype(BF16)
        ya_ref[...] = ya
        yb_ref[...] = ybp

    row = lambda w: pl.BlockSpec((tm, w), lambda i: (i, 0))
    return pl.pallas_call(
        body, name="merge_fwd", grid=(S // tm,),
        in_specs=[row(512), row(256), pl.BlockSpec((tm, D), lambda i: (i, gblk)), pl.BlockSpec((tm, D), lambda i: (i, gblk + 1)),
                  pl.BlockSpec((512, D), lambda i: (0, 0)), pl.BlockSpec((256, D), lambda i: (0, 0))],
        out_specs=[row(D), row(D), row(D)],
        out_shape=[_sds((S, D), BF16), _sds((S, D), F32), _sds((S, D), F32)],
    )(ya_att, yb, p, p, w_bra, w_brb)


def out_proj_fwd(merged, w_out, x, mod, g_ffn):
    tm = 256

    def body(a_ref, w_ref, x_ref, mod_ref, g_ref, mix_ref, x1_ref, h2_ref):
        mix = jnp.dot(a_ref[...], w_ref[...], preferred_element_type=F32)
        x1 = x_ref[...] + mod_ref[2:3, :] * mix
        r = lax.rsqrt(jnp.mean(x1 * x1, axis=1, keepdims=True) + EPS)
        h2 = (x1 * r * g_ref[...]) * (1.0 + mod_ref[4:5, :]) + mod_ref[3:4, :]
        mix_ref[...] = mix
        x1_ref[...] = x1
        h2_ref[...] = h2.astype(BF16)

    row = pl.BlockSpec((tm, D), lambda i: (i, 0))
    return pl.pallas_call(
        body, name="out_proj_fwd", grid=(S // tm,),
        in_specs=[row, pl.BlockSpec((D, D), lambda i: (0, 0)), row, pl.BlockSpec((8, D), lambda i: (0, 0)),
                  pl.BlockSpec((1, D), lambda i: (0, 0))],
        out_specs=[row, row, row],
        out_shape=[_sds((S, D), F32), _sds((S, D), F32), _sds((S, D), BF16)],
    )(merged, w_out, x, mod, g_ffn)


def ffn_up_fwd(h2, w_gate, w_up):
    tm = 1024

    def body(h_ref, wg_ref, wu_ref, a_ref, u_ref, z_ref):
        h = h_ref[...]
        a = jnp.dot(h, wg_ref[...], preferred_element_type=F32)
        u = jnp.dot(h, wu_ref[...], preferred_element_type=F32)
        a_ref[...] = a
        u_ref[...] = u
        z_ref[...] = (a * _sigmoid(a) * u).astype(BF16)

    out = pl.BlockSpec((tm, FF_PAD), lambda k, i: (i, k))
    return pl.pallas_call(
        body, name="ffn_up_fwd", grid=(4, S // tm),
        in_specs=[pl.BlockSpec((tm, D), lambda k, i: (i, 0)), pl.BlockSpec((None, D, FF_PAD), lambda k, i: (k, 0, 0)),
                  pl.BlockSpec((None, D, FF_PAD), lambda k, i: (k, 0, 0))],
        out_specs=[out, out, out],
        out_shape=[_sds((S, FFP), F32), _sds((S, FFP), F32), _sds((S, FFP), BF16)], compiler_params=_params(VMEM_MB),
    )(h2, w_gate, w_up)


def ffn_down_loss(z, w_down, x1, mod, g_final, tgt):
    tm = 256

    def body(z_ref, w_ref, x1_ref, mod_ref, g_ref, t_ref, dx2_ref, dffn_ref, dg_ref, dga_ref, loss_ref, s_dg, s_dga, s_loss):
        i = pl.program_id(0)

        @pl.when(i == 0)
        def _():
            s_dg[...] = jnp.zeros_like(s_dg)
            s_dga[...] = jnp.zeros_like(s_dga)
            s_loss[...] = jnp.zeros_like(s_loss)

        ffn = jnp.dot(z_ref[...], w_ref[...], preferred_element_type=F32)
        gaf = mod_ref[5:6, :]
        x2 = x1_ref[...] + gaf * ffn
        r = lax.rsqrt(jnp.mean(x2 * x2, axis=1, keepdims=True) + EPS)
        xh = x2 * r
        g = g_ref[...]
        e = xh * g - t_ref[...]
        s_loss[...] += 0.5 * jnp.sum(jnp.mean(e * e, axis=1, keepdims=True), axis=0, keepdims=True)
        dy = e * (1.0 / D)
        gdy = dy * g
        dx2 = r * (gdy - xh * jnp.mean(gdy * xh, axis=1, keepdims=True))
        s_dg[...] += _colsum8(dy * xh)
        s_dga[...] += _colsum8(dx2 * ffn)
        dx2_ref[...] = dx2
        dffn_ref[...] = (dx2 * gaf).astype(BF16)

        @pl.when(i == pl.num_programs(0) - 1)
        def _():
            dg_ref[...] = jnp.sum(s_dg[...], axis=0, keepdims=True)
            dga_ref[...] = jnp.sum(s_dga[...], axis=0, keepdims=True)
            loss_ref[...] = jnp.broadcast_to(s_loss[...], (1, LANES))

    row = pl.BlockSpec((tm, D), lambda i: (i, 0))
    vec = pl.BlockSpec((1, D), lambda i: (0, 0))
    return pl.pallas_call(
        body, name="ffn_down_loss", grid=(S // tm,),
        in_specs=[pl.BlockSpec((tm, FFP), lambda i: (i, 0)), pl.BlockSpec((FFP, D), lambda i: (0, 0)), row,
                  pl.BlockSpec((8, D), lambda i: (0, 0)), vec, row],
        out_specs=[row, row, vec, vec, pl.BlockSpec((1, LANES), lambda i: (0, 0))],
        out_shape=[_sds((S, D), F32), _sds((S, D), BF16), _sds((1, D), F32), _sds((1, D), F32), _sds((1, LANES), F32)],
        scratch_shapes=[pltpu.VMEM((8, D), F32), pltpu.VMEM((8, D), F32), pltpu.VMEM((1, 1), F32)],
        compiler_params=_params(VMEM_MB),
    )(z, w_down, x1, mod, g_final, tgt)


def ffn_down_bwd(dffn, w_down, a, u, z):
    tm, tn = 1024, 768

    def body(d_ref, w_ref, a_ref, u_ref, z_ref, da_ref, du_ref, dw_ref):
        i = pl.program_id(1)
        dff = d_ref[...]
        dz = lax.dot_general(dff, w_ref[...], (((1,), (1,)), ((), ())), preferred_element_type=F32)
        av, uv = a_ref[...], u_ref[...]
        sg = _sigmoid(av)
        du_ref[...] = (dz * (av * sg)).astype(BF16)
        da_ref[...] = (dz * uv * (sg * (1.0 + av * (1.0 - sg)))).astype(BF16)
        dw = lax.dot_general(z_ref[...], dff, (((0,), (0,)), ((), ())), preferred_element_type=F32)

        @pl.when(i == 0)
        def _():
            dw_ref[...] = dw

        @pl.when(i > 0)
        def _():
            dw_ref[...] += dw

    tile = pl.BlockSpec((tm, tn), lambda j, i: (i, j))
    return pl.pallas_call(
        body, name="ffn_down_bwd", grid=(FFP // tn, S // tm),
        in_specs=[pl.BlockSpec((tm, D), lambda j, i: (i, 0)), pl.BlockSpec((tn, D), lambda j, i: (j, 0)), tile, tile, tile],
        out_specs=[tile, tile, pl.BlockSpec((tn, D), lambda j, i: (j, 0))],
        out_shape=[_sds((S, FFP), BF16), _sds((S, FFP), BF16), _sds((FFP, D), F32)], compiler_params=_params(VMEM_MB),
    )(dffn, w_down, a, u, z)


def mm_nt(dy, w, name, comm=None):
    tm = 1024
    n = dy.shape[1]
    if w.ndim == 2:
        k_in, tk = w.shape[0], 768
        w_spec = pl.BlockSpec((k_in, tk), lambda i, k: (0, k))
    else:
        k_in, tk = w.shape[1], FF_PAD
        w_spec = pl.BlockSpec((None, k_in, tk), lambda i, k: (k, 0, 0))
    nk = n // tk

    def body(d_ref, w_ref, o_ref, acc):
        k = pl.program_id(1)
        part = lax.dot_general(d_ref[...], w_ref[...], (((1,), (1,)), ((), ())), preferred_element_type=F32)

        @pl.when(k == 0)
        def _():
            acc[...] = part

        @pl.when(k > 0)
        def _():
            acc[...] += part

        @pl.when(k == nk - 1)
        def _():
            o_ref[...] = acc[...]

    return _hosted_call(
        body, comm, (dy, w), name=name, grid=(S // tm, nk),
        in_specs=[pl.BlockSpec((tm, tk), lambda i, k: (i, k)), w_spec],
        out_specs=pl.BlockSpec((tm, k_in), lambda i, k: (i, 0)),
        out_shape=_sds((S, k_in), F32),
        scratch_shapes=[pltpu.VMEM((tm, k_in), F32)], vmem_mb=VMEM_MB)


def mm_tn(h, dy, name, shard_major=False):
    tm, tn = 2048, 768
    k_in, n = h.shape[1], dy.shape[1]

    def body(h_ref, d_ref, o_ref):
        i = pl.program_id(1)
        dw = lax.dot_general(h_ref[...], d_ref[...], (((0,), (0,)), ((), ())), preferred_element_type=F32)

        @pl.when(i == 0)
        def _():
            o_ref[...] = dw

        @pl.when(i > 0)
        def _():
            o_ref[...] += dw

    if shard_major:
        out_spec, out_shape = pl.BlockSpec((None, k_in, tn), lambda j, i: (j, 0, 0)), _sds((n // tn, k_in, tn), F32)
    else:
        out_spec, out_shape = pl.BlockSpec((k_in, tn), lambda j, i: (0, j)), _sds((k_in, n), F32)
    return pl.pallas_call(
        body, name=name, grid=(n // tn, S // tm),
        in_specs=[pl.BlockSpec((tm, k_in), lambda j, i: (i, 0)), pl.BlockSpec((tm, tn), lambda j, i: (i, j))],
        out_specs=out_spec, out_shape=out_shape, compiler_params=_params(VMEM_MB),
    )(h, dy)


def mid_bwd(dh2a, dh2b, x1, dx2, mix, mod, g_ffn, p, ya, ybp, merged, ya_att, yb, w_out, w_bra, w_brb, comm=None):
    tm = 256
    gblk = LAY_G // D
    nsteps = S // tm

    def body(dha_ref, dhb_ref, x1_ref, dx2_ref, mix_ref, mod_ref, g_ref, ga_ref, gb_ref, ya_ref, yb_ref, mg_ref,
             att_ref, ybb_ref, wo_ref, wa_ref, wb_ref,
             dx1_ref, dpg_ref, datt_ref, dyb_ref, cs_ref, dwo_ref, dwa_ref, dwb_ref, s_cs):
        i = pl.program_id(0)

        @pl.when(i == 0)
        def _():
            s_cs[...] = jnp.zeros_like(s_cs)
            dwo_ref[...] = jnp.zeros_like(dwo_ref)
            dwa_ref[...] = jnp.zeros_like(dwa_ref)
            dwb_ref[...] = jnp.zeros_like(dwb_ref)

        x1 = x1_ref[...]
        g = g_ref[...]
        r = lax.rsqrt(jnp.mean(x1 * x1, axis=1, keepdims=True) + EPS)
        xh = x1 * r
        dh2 = dha_ref[...] + dhb_ref[...]
        s_cs[0] += _colsum8(dh2)
        s_cs[1] += _colsum8(dh2 * (xh * g))
        dn2 = dh2 * (1.0 + mod_ref[4:5, :])
        s_cs[2] += _colsum8(dn2 * xh)
        gd = dn2 * g
        dx1 = dx2_ref[...] + r * (gd - xh * jnp.mean(gd * xh, axis=1, keepdims=True))
        s_cs[3] += _colsum8(dx1 * mix_ref[...])
        dx1_ref[...] = dx1
        dmix = (dx1 * mod_ref[2:3, :]).astype(BF16)
        dmg = lax.dot_general(dmix, wo_ref[...], (((1,), (1,)), ((), ())), preferred_element_type=F32)
        sga, sgb = _sigmoid(ga_ref[...]), _sigmoid(gb_ref[...])
        dya = (dmg * sga).astype(BF16)
        dybp = (dmg * sgb).astype(BF16)
        dpg_ref[:, 0:D] = (dmg * ya_ref[...] * (sga * (1.0 - sga))).astype(BF16)
        dpg_ref[:, D:2 * D] = (dmg * yb_ref[...] * (sgb * (1.0 - sgb))).astype(BF16)
        datt_ref[...] = lax.dot_general(dya, wa_ref[...], (((1,), (1,)), ((), ())), preferred_element_type=F32).astype(BF16)
        dyb_ref[...] = lax.dot_general(dybp, wb_ref[...], (((1,), (1,)), ((), ())), preferred_element_type=F32)
        tn_dims = (((0,), (0,)), ((), ()))
        dwo_ref[...] += lax.dot_general(mg_ref[...], dmix, tn_dims, preferred_element_type=F32)
        dwa_ref[...] += lax.dot_general(att_ref[...], dya, tn_dims, preferred_element_type=F32)
        dwb_ref[...] += lax.dot_general(ybb_ref[...].astype(BF16), dybp, tn_dims, preferred_element_type=F32)

        @pl.when(i == nsteps - 1)
        def _():
            for t in range(4):
                cs_ref[t:t + 1, :] = jnp.sum(s_cs[t], axis=0, keepdims=True)
            cs_ref[4:8, :] = jnp.zeros((4, D), F32)

    row = lambda w: pl.BlockSpec((tm, w), lambda i: (i, 0))
    full = lambda a, b: pl.BlockSpec((a, b), lambda i: (0, 0))
    return _hosted_call(
        body, comm, (dh2a, dh2b, x1, dx2, mix, mod, g_ffn, p, p, ya, ybp, merged, ya_att, yb, w_out, w_bra, w_brb),
        name="mid_bwd", grid=(nsteps,),
        in_specs=[row(D), row(D), row(D), row(D), row(D), full(8, D), full(1, D),
                  pl.BlockSpec((tm, D), lambda i: (i, gblk)), pl.BlockSpec((tm, D), lambda i: (i, gblk + 1)),
                  row(D), row(D), row(D), row(512), row(256), full(D, D), full(512, D), full(256, D)],
        out_specs=[row(D), pl.BlockSpec((tm, 2 * D), lambda i: (i, LAY_G // (2 * D))), row(512), row(256), full(8, D),
                   full(D, D), full(512, D), full(256, D)],
        out_shape=[_sds((S, D), F32), _sds((S, LAY_N), BF16), _sds((S, 512), BF16), _sds((S, 256), F32), _sds((8, D), F32),
                   _sds((D, D), F32), _sds((512, D), F32), _sds((256, D), F32)],
        scratch_shapes=[pltpu.VMEM((4, 8, D), F32)],
        vmem_mb=VMEM_MB)


def fox_bwd(p, do, o, gcol, fcol, dp, comm=None):
    nq = S // FT
    nt = (((1,), (1,)), ((), ()))
    tn = (((0,), (0,)), ((), ()))

    def body(qkv_ref, do_ref, o_ref, g_ref, fc_ref, dp_in, dp_ref, df_ref, rs_ref, dq_s, qa_s, ka_s, qm_s, dob_s, dl_s):
        del dp_in
        _fox_operands(qkv_ref, g_ref, fc_ref, qa_s, ka_s)
        masks = _head_masks(FT)
        lane = lax.broadcasted_iota(jnp.int32, (FT, LANES), 1)
        head0 = 2 * pl.program_id(0)
        causal = lax.broadcasted_iota(jnp.int32, (FT, FT), 1) <= lax.broadcasted_iota(jnp.int32, (FT, FT), 0)
        dq_s[...] = jnp.zeros_like(dq_s)
        rs_ref[...] = jnp.zeros_like(rs_ref)

        def prep(i, _):
            r = pl.ds(pl.multiple_of(i * 256, 256), 256)
            m256 = _head_masks(256)
            dov, ov = do_ref[r, :].astype(F32), o_ref[r, :].astype(F32)
            for hh in range(2):
                dom = jnp.where(m256[hh], dov, 0.0)
                dob_s[hh, r, :] = dom.astype(BF16)
                dl_s[hh, r, :] = jnp.broadcast_to(jnp.sum(dom * ov, axis=1, keepdims=True), (256, LANES))
                qm_s[hh, r, :] = jnp.where(m256[hh], qa_s[hh, r, :], 0.0).astype(BF16)
            return 0

        lax.fori_loop(0, S // 256, prep, 0)

        def kloop(kb, _):
            k0 = pl.multiple_of(kb * FT, FT)
            k = qkv_ref[pl.ds(k0, FT), LANES:2 * LANES].astype(BF16)
            v = qkv_ref[pl.ds(k0, FT), 2 * LANES:3 * LANES].astype(BF16)
            ka = [ka_s[hh, pl.ds(k0, FT), :] for hh in range(2)]

            def step(qi, carry, diagonal):
                dk, dv, df0, df1 = carry
                q0 = pl.multiple_of(qi * FT, FT)
                dqs, dfs = [], []
                rs = jnp.zeros((FT, LANES), F32)
                for hh in range(2):
                    dob = dob_s[hh, pl.ds(q0, FT), :]
                    s = lax.dot_general(qa_s[hh, pl.ds(q0, FT), :], ka[hh], nt, preferred_element_type=F32)
                    pr = jnp.exp(jnp.where(causal, s, NEG)) if diagonal else jnp.exp(s)
                    dpr = lax.dot_general(dob, v, nt, preferred_element_type=F32)
                    ds = pr * (dpr - jnp.tile(dl_s[hh, pl.ds(q0, FT), :], (1, FT // LANES)))
                    dsb = ds.astype(BF16)
                    dqs.append(jnp.dot(dsb, k, preferred_element_type=F32) * SCALE)
                    dk = dk + lax.dot_general(dsb, qm_s[hh, pl.ds(q0, FT), :], tn, preferred_element_type=F32)
                    dv = dv + lax.dot_general(pr.astype(BF16), dob, tn, preferred_element_type=F32)
                    dfs.append(jnp.sum(ds, axis=0, keepdims=True))
                    rs = rs + jnp.where(lane == head0 + hh, jnp.sum(ds, axis=1, keepdims=True), 0.0)
                dq_s[pl.ds(q0, FT), :] += jnp.where(masks[0], dqs[0], dqs[1])
                rs_ref[pl.ds(q0, FT), :] += rs
                return dk, dv, df0 - dfs[0], df1 - dfs[1]

            z = jnp.zeros((FT, LANES), F32)
            z1 = jnp.zeros((1, FT), F32)
            carry = step(kb, (z, z, z1, z1), True)
            dk, dv, df0, df1 = lax.fori_loop(kb + 1, nq, lambda qi, cr: step(qi, cr, False), carry)
            dp_ref[pl.ds(k0, FT), LANES:2 * LANES] = dk.astype(BF16)
            dp_ref[pl.ds(k0, FT), 2 * LANES:3 * LANES] = dv.astype(BF16)
            df_ref[0:1, pl.ds(k0, FT)] = df0
            df_ref[1:2, pl.ds(k0, FT)] = df1
            return 0

        lax.fori_loop(0, S // FT, kloop, 0)
        dp_ref[:, 0:LANES] = dq_s[...].astype(BF16)

    a_blk = LAY_A // 384
    pair = pl.BlockSpec((S, LANES), lambda p_: (0, p_))
    heads = pl.BlockSpec((2, S, LANES), lambda p_: (p_, 0, 0))
    return _hosted_call(
        body, comm, (p, do, o, gcol, fcol, dp), name="fox_bwd", grid=(N_FOX_PAIRS,),
        in_specs=[pl.BlockSpec((S, 384), lambda p_: (0, a_blk + p_)), pair, pair, heads, heads, pl.BlockSpec(memory_space=pl.ANY)],
        out_specs=[pl.BlockSpec((S, 384), lambda p_: (0, a_blk + p_)), pl.BlockSpec((None, 2, S), lambda p_: (p_, 0, 0)),
                   pl.BlockSpec((None, S, LANES), lambda p_: (p_, 0, 0))],
        out_shape=[_sds((S, LAY_N), BF16), _sds((4, 2, S), F32), _sds((4, S, LANES), F32)],
        scratch_shapes=[pltpu.VMEM((S, LANES), F32)] + [pltpu.VMEM((2, S, LANES), BF16)] * 4 + [pltpu.VMEM((2, S, LANES), F32)],
        aliases={5: 0}, vmem_mb=VMEM_MB)


def fgate_bwd(dfrow, dfcol, fraw, dp):
    def body(df_ref, dc_ref, f_ref, dp_in, dpf_ref, db_ref):
        del dp_in
        lane = lax.broadcasted_iota(jnp.int32, (8, S), 1)
        rsum = (dc_ref[0] + dc_ref[1]) + (dc_ref[2] + dc_ref[3])
        acc, sh = df_ref[...] + rsum.T[0:8, :], 1
        while sh < S:
            acc = acc + jnp.where(lane < S - sh, pltpu.roll(acc, S - sh, 1), 0.0)
            sh *= 2
        df = acc * _sigmoid(-f_ref[...])
        db_ref[...] = jnp.broadcast_to(jnp.sum(df, axis=1, keepdims=True), (8, LANES))
        dfc = jnp.concatenate([df, jnp.zeros((LANES - 8, S), F32)], axis=0).T
        dpf_ref[:, 0:LANES] = dfc.astype(BF16)
        dpf_ref[:, LANES:2 * LANES] = jnp.zeros((S, LANES), BF16)

    return pl.pallas_call(
        body, name="fgate_bwd", grid=(1,),
        in_specs=[pl.BlockSpec((8, S), lambda i: (0, 0)), pl.BlockSpec((4, S, LANES), lambda i: (0, 0, 0)),
                  pl.BlockSpec((8, S), lambda i: (0, 0)), pl.BlockSpec(memory_space=pl.ANY)],
        out_specs=[pl.BlockSpec((S, 2 * LANES), lambda i: (0, LAY_F // (2 * LANES))), pl.BlockSpec((8, LANES), lambda i: (0, 0))],
        out_shape=[_sds((S, LAY_N), BF16), _sds((8, LANES), F32)],
        input_output_aliases={3: 0},
        compiler_params=_params(VMEM_MB),
    )(dfrow, dfcol, fraw, dp)


def dil_bwd(p, dyb, yb, lse, tabs, dp, comm=None):
    def body(*refs):
        qkv = [refs[3 * g:3 * g + 3] for g in range(3)]
        dy_ref, y_ref, lse_ref, c_ref, s1_ref, s2_ref = refs[9:15]
        dp_ref = refs[16]
        dq_s, dk_s, dv_s, dl_s = refs[17:21]
        masks = _head_masks(SPAN)
        m256 = _head_masks(256)
        nt = (((1,), (1,)), ((), ()))
        tn = (((0,), (0,)), ((), ()))
        dk_s[...] = jnp.zeros_like(dk_s)
        dv_s[...] = jnp.zeros_like(dv_s)

        def prep(i, _):
            rows = pl.ds(pl.multiple_of(i * 256, 256), 256)
            pr = dy_ref[rows, :] * y_ref[rows, :]
            d0 = jnp.sum(jnp.where(m256[0], pr, 0.0), axis=1, keepdims=True)
            d1 = jnp.sum(jnp.where(m256[1], pr, 0.0), axis=1, keepdims=True)
            dl_s[rows, :] = jnp.where(m256[0], d0, d1)
            return 0

        lax.fori_loop(0, S // 256, prep, 0)

        for g, (d, nb) in enumerate(DIL_GROUPS):
            q_ref, k_ref, v_ref = qkv[g]

            def blk(n, _):
                start, prev, has_prev = _dil_block(n, d, nb)
                q = _dil_rows(q_ref, start, d)
                kc = jnp.concatenate([_dil_rows(k_ref, prev, d), _dil_rows(k_ref, start, d)], axis=0).astype(BF16)
                vc = jnp.concatenate([_dil_rows(v_ref, prev, d), _dil_rows(v_ref, start, d)], axis=0).astype(BF16)
                dov = _dil_rows(dy_ref, start, d)
                lsev = _dil_rows(lse_ref, start, d)
                dlv = _dil_rows(dl_s, start, d)
                valid = _band_mask(has_prev)
                dqs = []
                dkc = jnp.zeros((2 * SPAN, LANES), F32)
                dvc = jnp.zeros((2 * SPAN, LANES), F32)
                for hh in range(2):
                    qm = (jnp.where(masks[hh], q, 0.0) * SCALE).astype(BF16)
                    dob = jnp.where(masks[hh], dov, 0.0).astype(BF16)
                    lse_h = jnp.max(jnp.where(masks[hh], lsev, NEG), axis=1, keepdims=True)
                    dl_h = jnp.max(jnp.where(masks[hh], dlv, NEG), axis=1, keepdims=True)
                    s = lax.dot_general(qm, kc, nt, preferred_element_type=F32)
                    pr = jnp.where(valid, jnp.exp(jnp.where(valid, s, NEG) - lse_h), 0.0)
                    dpr = lax.dot_general(dob, vc, nt, preferred_element_type=F32)
                    dsb = (pr * (dpr - dl_h)).astype(BF16)
                    dqs.append(jnp.dot(dsb, kc, preferred_element_type=F32) * SCALE)
                    dkc = dkc + lax.dot_general(dsb, qm, tn, preferred_element_type=F32)
                    dvc = dvc + lax.dot_general(pr.astype(BF16), dob, tn, preferred_element_type=F32)
                _dil_store(dq_s.at[g], start, d, jnp.where(masks[0], dqs[0], dqs[1]))
                for ref, val in ((dk_s.at[g], dkc), (dv_s.at[g], dvc)):
                    _dil_store(ref, prev, d, _dil_rows(ref, prev, d) + jnp.where(has_prev, val[0:SPAN], 0.0))
                    _dil_store(ref, start, d, _dil_rows(ref, start, d) + val[SPAN:])
                return 0

            lax.fori_loop(0, 16, blk, 0)

        def fin(i, _):
            rows = pl.ds(pl.multiple_of(i * 256, 256), 256)
            c, s1, s2 = c_ref[rows, :], s1_ref[rows, :], s2_ref[rows, :]
            for g in range(3):
                base = g * 384
                dp_ref[rows, base:base + LANES] = _rope_bwd(dq_s[g, rows, :], c, s1, s2).astype(BF16)
                dp_ref[rows, base + LANES:base + 2 * LANES] = _rope_bwd(dk_s[g, rows, :], c, s1, s2).astype(BF16)
                dp_ref[rows, base + 2 * LANES:base + 3 * LANES] = dv_s[g, rows, :].astype(BF16)
            return 0

        lax.fori_loop(0, S // 256, fin, 0)

    def spec(g, t):
        return pl.BlockSpec((S, LANES), lambda p_: (0, (p_ * 3 + g) * 3 + t))

    pair = pl.BlockSpec((S, LANES), lambda p_: (0, p_))
    tab = pl.BlockSpec((S, LANES), lambda p_: (0, 0))
    return _hosted_call(
        body, comm, [p] * 9 + [dyb, yb, lse, *tabs, dp], name="dil_bwd", grid=(N_DIL_PAIRS,),
        in_specs=[spec(g, t) for g in range(3) for t in range(3)] + [pair, pair, pair, tab, tab, tab, pl.BlockSpec(memory_space=pl.ANY)],
        out_specs=pl.BlockSpec((S, 1152), lambda p_: (0, p_)),
        out_shape=_sds((S, LAY_N), BF16),
        scratch_shapes=[pltpu.VMEM((3, S, LANES), F32)] * 3 + [pltpu.VMEM((S, LANES), F32)],
        aliases={15: 0}, vmem_mb=VMEM_MB)


def in_bwd_tail(dh1, x, dx1, mod, g_mix, comm=None):
    tm = 256
    nsteps = S // tm

    def body(dh_ref, x_ref, dx1_ref, mod_ref, g_ref, dx_ref, cs_ref, s_cs):
        i = pl.program_id(0)

        @pl.when(i == 0)
        def _():
            s_cs[...] = jnp.zeros_like(s_cs)

        xv, g, dh = x_ref[...], g_ref[...], dh_ref[...]
        r = lax.rsqrt(jnp.mean(xv * xv, axis=1, keepdims=True) + EPS)
        xh = xv * r
        s_cs[0] += _colsum8(dh)
        s_cs[1] += _colsum8(dh * (xh * g))
        dn = dh * (1.0 + mod_ref[1:2, :])
        s_cs[2] += _colsum8(dn * xh)
        gd = dn * g
        dx_ref[...] = dx1_ref[...] + r * (gd - xh * jnp.mean(gd * xh, axis=1, keepdims=True))

        @pl.when(i == nsteps - 1)
        def _():
            for t in range(3):
                cs_ref[t:t + 1, :] = jnp.sum(s_cs[t], axis=0, keepdims=True)
            cs_ref[3:8, :] = jnp.zeros((5, D), F32)

    row = pl.BlockSpec((tm, D), lambda i: (i, 0))
    return _hosted_call(
        body, comm, (dh1, x, dx1, mod, g_mix), name="in_bwd_tail", grid=(nsteps,),
        in_specs=[row, row, row, pl.BlockSpec((8, D), lambda i: (0, 0)), pl.BlockSpec((1, D), lambda i: (0, 0))],
        out_specs=[row, pl.BlockSpec((8, D), lambda i: (0, 0))],
        out_shape=[_sds((S, D), F32), _sds((8, D), F32)],
        scratch_shapes=[pltpu.VMEM((3, 8, D), F32)])


def _lay_pieces():
    out = []
    qa, ka, va, fa, qb, kb, vb, ga = 0, 512, 1024, 1536, 1544, 2312, 3080, 3848
    for p in range(N_DIL_PAIRS):
        for g in range(3):
            base = LAY_B + (p * 3 + g) * 384
            hd0 = (4 * g + 2 * p) * HD
            out += [(base, qb + hd0, LANES), (base + LANES, kb + hd0, LANES), (base + 2 * LANES, vb + hd0, LANES)]
    for p in range(N_FOX_PAIRS):
        base = LAY_A + p * 384
        out += [(base, qa + p * LANES, LANES), (base + LANES, ka + p * LANES, LANES), (base + 2 * LANES, va + p * LANES, LANES)]
    out.append((LAY_F, fa, 8))
    out.append((LAY_G, ga, 2 * D))
    return out


def lay_from_nat(w_nat):
    parts, pos = [], 0
    for lay, nat, width in sorted(_lay_pieces()):
        if lay > pos:
            parts.append(jnp.zeros((w_nat.shape[0], lay - pos), w_nat.dtype))
        parts.append(w_nat[:, nat:nat + width])
        pos = lay + width
    if pos < LAY_N:
        parts.append(jnp.zeros((w_nat.shape[0], LAY_N - pos), w_nat.dtype))
    return jnp.concatenate(parts, axis=1)


def nat_from_lay(w_lay):
    parts = [w_lay[:, lay:lay + width] for lay, nat, width in sorted(_lay_pieces(), key=lambda t: t[1])]
    return jnp.concatenate(parts, axis=1)


def _pos():
    return lax.axis_index("x"), lax.axis_index("y"), lax.axis_index("c")


def _other_chips(x, y):
    return [(1 - x, y), (x, 1 - y), (1 - x, 1 - y)]


def _remote(src, dst, send_sem, recv_sem, dev):
    return pltpu.make_async_remote_copy(src_ref=src, dst_ref=dst, send_sem=send_sem, recv_sem=recv_sem,
                                        device_id=dev, device_id_type=MESH)


VMEM_SPEC = pl.BlockSpec(memory_space=pltpu.VMEM)
ANY_SPEC = pl.BlockSpec(memory_space=pl.ANY)


def gather_all(v, name, with_sum):
    r = v.shape[0]

    def body(v_ref, out_ref, *rest):
        send_s, recv_s = rest[-2:]
        x, y, c = _pos()
        me = 4 * x + 2 * y + c
        out_ref[me] = v_ref[...]
        peers = []
        for m in range(1, 8):
            px = 1 - x if m & 4 else x
            py = 1 - y if m & 2 else y
            pc = 1 - c if m & 1 else c
            peers.append((px, py, pc))
        copies = [_remote(v_ref, out_ref.at[me], send_s.at[i], recv_s.at[i], dev) for i, dev in enumerate(peers)]
        for cp in copies:
            cp.start()
        for i, (px, py, pc) in enumerate(peers):
            _remote(v_ref, out_ref.at[4 * px + 2 * py + pc], send_s.at[i], recv_s.at[i], (px, py, pc)).wait_recv()
        for cp in copies:
            cp.wait_send()
        if with_sum:
            acc = out_ref[0]
            for b in range(1, 8):
                acc = acc + out_ref[b]
            rest[0][...] = acc

    out_shape = [_sds((8, r, LANES), F32)] + ([_sds((r, LANES), F32)] if with_sum else [])
    return pl.pallas_call(
        body, name=name, in_specs=[VMEM_SPEC], out_specs=[VMEM_SPEC] * len(out_shape), out_shape=out_shape,
        scratch_shapes=[pltpu.SemaphoreType.DMA((7,)), pltpu.SemaphoreType.DMA((7,))],
    )(v)


def mod_exchange(c_all, w_ada_sh, b_sh):
    def body(c_ref, w_ref, b_ref, out_ref, sc_ref, modp, send_s, recv_s):
        cv = c_ref[...]
        sc = cv * _sigmoid(cv)
        sc_ref[...] = sc
        modp[...] = jnp.dot(sc, w_ref[...], precision=lax.Precision.HIGHEST, preferred_element_type=F32) + b_ref[...]
        x, y, c = _pos()
        k = 2 * x + y
        out_ref[k] = modp[...]
        chips = _other_chips(x, y)
        copies = [_remote(modp, out_ref.at[k], send_s.at[j], recv_s.at[j], (cx, cy, c)) for j, (cx, cy) in enumerate(chips)]
        for cp in copies:
            cp.start()
        for j, (cx, cy) in enumerate(chips):
            _remote(modp, out_ref.at[2 * cx + cy], send_s.at[j], recv_s.at[j], (cx, cy, c)).wait_recv()
        for cp in copies:
            cp.wait_send()

    n = w_ada_sh.shape[1]
    return pl.pallas_call(
        body, name="mod_exchange", in_specs=[VMEM_SPEC] * 3, out_specs=[VMEM_SPEC] * 2,
        out_shape=[_sds((4, 8, n), F32), _sds((8, D), F32)],
        scratch_shapes=[pltpu.VMEM((8, n), F32), pltpu.SemaphoreType.DMA((3,)), pltpu.SemaphoreType.DMA((3,))],
        compiler_params=_params(VMEM_MB),
    )(c_all, w_ada_sh, b_sh)


def gather_weights(shards):
    n = len(shards)

    def body(*refs):
        ins, outs = refs[:n], refs[n:2 * n]
        send_s, recv_s, fsend_s, frecv_s, loc_s = refs[2 * n:]
        x, y, c = _pos()
        k = 2 * x + y
        chips = _other_chips(x, y)
        local, sends, fwds = [], [], []
        for a in range(n):
            half = ins[a].shape[0] // 2
            rows = pl.ds(c * half, half)
            lc = pltpu.make_async_copy(ins[a], outs[a].at[k], loc_s.at[a])
            lc.start()
            local.append(lc)
            for j, (cx, cy) in enumerate(chips):
                cp = _remote(ins[a].at[rows], outs[a].at[k, rows], send_s.at[3 * a + j], recv_s.at[3 * a + j], (cx, cy, c))
                cp.start()
                sends.append(cp)
        for a in range(n):
            half = ins[a].shape[0] // 2
            rows = pl.ds(c * half, half)
            for j, (cx, cy) in enumerate(chips):
                kj = 2 * cx + cy
                _remote(ins[a].at[rows], outs[a].at[kj, rows], send_s.at[3 * a + j], recv_s.at[3 * a + j], (cx, cy, c)).wait_recv()
                fw = _remote(outs[a].at[kj, rows], outs[a].at[kj, rows], fsend_s.at[3 * a + j], frecv_s.at[3 * a + j], (x, y, 1 - c))
                fw.start()
                fwds.append(fw)
        for a in range(n):
            half = ins[a].shape[0] // 2
            orows = pl.ds((1 - c) * half, half)
            for j, (cx, cy) in enumerate(chips):
                kj = 2 * cx + cy
                _remote(outs[a].at[kj, orows], outs[a].at[kj, orows], fsend_s.at[3 * a + j], frecv_s.at[3 * a + j], (x, y, 1 - c)).wait_recv()
        for cp in sends + fwds:
            cp.wait_send()
        for lc in local:
            lc.wait()

    return pl.pallas_call(
        body, name="gather_weights", in_specs=[ANY_SPEC] * n, out_specs=[ANY_SPEC] * n,
        out_shape=[_sds((4,) + s.shape, s.dtype) for s in shards],
        scratch_shapes=[pltpu.SemaphoreType.DMA((3 * n,))] * 4 + [pltpu.SemaphoreType.DMA((n,))],
    )(*shards)


def _row_tile(rows, cap=256):
    t = cap
    while rows % t or t % 8:
        t -= 8
    return t


def _comm_wait(sends, recvs, local=()):
    for cp in recvs:
        cp.wait_recv()
    for cp in sends:
        cp.wait_send()
    for cp in local:
        cp.wait()


def ag_ici(shards):
    n = len(shards)

    def copies(ins, outs, sems):
        send_s, recv_s, loc_s = sems
        x, y, c = _pos()
        k = 2 * x + y
        sends, recvs, local = [], [], []
        for a in range(n):
            half = ins[a].shape[0] // 2
            rows = pl.ds(c * half, half)
            local.append(pltpu.make_async_copy(ins[a], outs[a].at[k], loc_s.at[a]))
            for j, (cx, cy) in enumerate(_other_chips(x, y)):
                sem = (send_s.at[3 * a + j], recv_s.at[3 * a + j], (cx, cy, c))
                sends.append(_remote(ins[a].at[rows], outs[a].at[k, rows], *sem))
                recvs.append(_remote(ins[a].at[rows], outs[a].at[2 * cx + cy, rows], *sem))
        return sends, recvs, local

    def start(ins, outs, sems):
        sends, _, local = copies(ins, outs, sems)
        for cp in local + sends:
            cp.start()

    def wait(ins, outs, sems):
        _comm_wait(*copies(ins, outs, sems))

    return Comm(shards, [_sds((4,) + s.shape, s.dtype) for s in shards], [3 * n, 3 * n, n], start, wait)


def ag_d2d(bufs):
    n = len(bufs)

    def copies(ins, outs, sems):
        send_s, recv_s = sems
        x, y, c = _pos()
        sends, recvs = [], []
        for a in range(n):
            half = outs[a].shape[1] // 2
            rows, orows = pl.ds(c * half, half), pl.ds((1 - c) * half, half)
            for j, (cx, cy) in enumerate(_other_chips(x, y)):
                kj = 2 * cx + cy
                sem = (send_s.at[3 * a + j], recv_s.at[3 * a + j], (x, y, 1 - c))
                sends.append(_remote(outs[a].at[kj, rows], outs[a].at[kj, rows], *sem))
                recvs.append(_remote(outs[a].at[kj, orows], outs[a].at[kj, orows], *sem))
        return sends, recvs

    def start(ins, outs, sems):
        for cp in copies(ins, outs, sems)[0]:
            cp.start()

    def wait(ins, outs, sems):
        _comm_wait(*copies(ins, outs, sems))

    return Comm(bufs, [_sds(b.shape, b.dtype) for b in bufs], [3 * n, 3 * n], start, wait, aliases={a: a for a in range(n)})


def rs_a(grads):
    n = len(grads)

    def copies(ins, outs, sems):
        send_s, recv_s = sems
        x, y, c = _pos()
        cps = []
        for a in range(n):
            half = ins[a].shape[1] // 2
            cps.append(_remote(ins[a].at[:, pl.ds((1 - c) * half, half), :], outs[a], send_s.at[a], recv_s.at[a], (x, y, 1 - c)))
        return cps

    def start(ins, outs, sems):
        for cp in copies(ins, outs, sems):
            cp.start()

    def wait(ins, outs, sems):
        cps = copies(ins, outs, sems)
        _comm_wait(cps, cps)

    return Comm(grads, [_sds((4, g.shape[1] // 2, g.shape[2]), g.dtype) for g in grads], [n, n], start, wait)


def rs_b(pres):
    n = len(pres)

    def copies(ins, outs, sems):
        send_s, recv_s, loc_s = sems
        x, y, c = _pos()
        k = 2 * x + y
        cps, local = [], []
        for a in range(n):
            local.append(pltpu.make_async_copy(ins[a].at[k], outs[a].at[3], loc_s.at[a]))
            for j, (cx, cy) in enumerate(_other_chips(x, y)):
                cps.append(_remote(ins[a].at[2 * cx + cy], outs[a].at[j], send_s.at[3 * a + j], recv_s.at[3 * a + j], (cx, cy, c)))
        return cps, local

    def start(ins, outs, sems):
        cps, local = copies(ins, outs, sems)
        for cp in local + cps:
            cp.start()

    def wait(ins, outs, sems):
        cps, local = copies(ins, outs, sems)
        _comm_wait(cps, cps, local)

    return Comm(pres, [_sds(p_.shape, p_.dtype) for p_ in pres], [3 * n, 3 * n, n], start, wait)


def rs_c(reds):
    n = len(reds)

    def copies(ins, outs, sems):
        send_s, recv_s, loc_s = sems
        x, y, c = _pos()
        sends, recvs, local = [], [], []
        for a in range(n):
            half = ins[a].shape[0]
            rows, orows = pl.ds(c * half, half), pl.ds((1 - c) * half, half)
            local.append(pltpu.make_async_copy(ins[a], outs[a].at[rows], loc_s.at[a]))
            sem = (send_s.at[a], recv_s.at[a], (x, y, 1 - c))
            sends.append(_remote(ins[a], outs[a].at[rows], *sem))
            recvs.append(_remote(ins[a], outs[a].at[orows], *sem))
        return sends, recvs, local

    def start(ins, outs, sems):
        sends, _, local = copies(ins, outs, sems)
        for cp in local + sends:
            cp.start()

    def wait(ins, outs, sems):
        _comm_wait(*copies(ins, outs, sems))

    return Comm(reds, [_sds((2 * r_.shape[0], r_.shape[1]), r_.dtype) for r_ in reds], [n, n, n], start, wait)


def comm_only(comm, name):
    nci, nco = len(comm.ins), len(comm.out_shapes)

    def body(*refs):
        ins, outs, sems = refs[:nci], refs[nci:nci + nco], refs[nci + nco:]
        comm.start(ins, outs, sems)
        comm.wait(ins, outs, sems)

    return pl.pallas_call(
        body, name=name, in_specs=[ANY_SPEC] * nci, out_specs=[ANY_SPEC] * nco, out_shape=comm.out_shapes,
        scratch_shapes=[pltpu.SemaphoreType.DMA((s,)) for s in comm.sems],
        input_output_aliases=comm.aliases,
    )(*comm.ins)


def rs_add_halves(g, other, core, name):
    _, r, cdim = g.shape
    half = r // 2
    tr = _row_tile(half, 128)
    nb = half // tr

    def body(core_ref, g_ref, o_ref, out_ref):
        del core_ref
        out_ref[...] = (g_ref[...] + o_ref[...]).astype(BF16)

    grid_spec = pltpu.PrefetchScalarGridSpec(
        num_scalar_prefetch=1, grid=(4, nb),
        in_specs=[pl.BlockSpec((None, tr, cdim), lambda k, i, cr: (k, cr[0] * nb + i, 0)),
                  pl.BlockSpec((None, tr, cdim), lambda k, i, cr: (k, i, 0))],
        out_specs=pl.BlockSpec((None, tr, cdim), lambda k, i, cr: (k, i, 0)))
    return pl.pallas_call(body, name=name, grid_spec=grid_spec, out_shape=_sds((4, half, cdim), BF16))(core, g, other)


def rs_add_slabs(t, name):
    _, half, cdim = t.shape
    tr = _row_tile(half, 128)

    def body(t_ref, out_ref):
        s = [t_ref[i].astype(F32) for i in range(4)]
        out_ref[...] = ((s[3] + s[0]) + s[1]) + s[2]

    return pl.pallas_call(
        body, name=name, grid=(half // tr,),
        in_specs=[pl.BlockSpec((4, tr, cdim), lambda i: (0, i, 0))],
        out_specs=pl.BlockSpec((tr, cdim), lambda i: (i, 0)),
        out_shape=_sds((half, cdim), F32),
    )(t)


def _adam_math(w, g, m, v):
    m = ADAM_B1 * m + (1.0 - ADAM_B1) * g
    v = ADAM_B2 * v + (1.0 - ADAM_B2) * (g * g)
    m_hat = m / (1.0 - ADAM_B1 ** ADAM_STEP)
    v_hat = v / (1.0 - ADAM_B2 ** ADAM_STEP)
    delta = -ADAM_LR * (m_hat / (jnp.sqrt(v_hat) + ADAM_EPS) + ADAM_WD * w)
    return delta, m, v


def adam(w, g, m, v, name):
    r, cdim = w.shape
    tr = _row_tile(r) if r >= 8 else r

    def body(w_ref, g_ref, m_ref, v_ref, g_out, d_ref, nm_ref, nv_ref):
        gv = g_ref[:, :cdim]
        g_out[...] = gv
        d_ref[...], nm_ref[...], nv_ref[...] = _adam_math(w_ref[...], gv, m_ref[...], v_ref[...])

    blk = pl.BlockSpec((tr, cdim), lambda i: (i, 0))
    return pl.pallas_call(
        body, name=name, grid=(r // tr,), in_specs=[blk, pl.BlockSpec((tr, g.shape[1]), lambda i: (i, 0)), blk, blk],
        out_specs=[blk] * 4, out_shape=[_sds((r, cdim), F32)] * 4,
    )(w, g, m, v)


def adam_w_ada(sc_t, dmod_sh, w, m, v):
    r, cdim = w.shape
    tr = 256

    def body(s_ref, d_ref, w_ref, m_ref, v_ref, g_ref, dl_ref, nm_ref, nv_ref):
        g = jnp.dot(s_ref[...], d_ref[...], precision=lax.Precision.HIGHEST, preferred_element_type=F32)
        g_ref[...] = g
        dl_ref[...], nm_ref[...], nv_ref[...] = _adam_math(w_ref[...], g, m_ref[...], v_ref[...])

    blk = pl.BlockSpec((tr, cdim), lambda i: (i, 0))
    return pl.pallas_call(
        body, name="adam_w_ada", grid=(r // tr,),
        in_specs=[pl.BlockSpec((tr, LANES), lambda i: (i, 0)), pl.BlockSpec((LANES, cdim), lambda i: (0, 0)), blk, blk, blk],
        out_specs=[blk] * 4, out_shape=[_sds((r, cdim), F32)] * 4,
    )(sc_t, dmod_sh, w, m, v)


SMALL_ROWS = 80


def kernel(x, c, w_ada, b_ada, g_mix, w_in, b_fgate, w_br_a, w_br_b, w_out, g_ffn, w_ffn_gate, w_ffn_up, w_ffn_down, g_final, loss_target, m_w_ada, m_b_ada, m_g_mix, m_w_in, m_b_fgate, m_w_br_a, m_w_br_b, m_w_out, m_g_ffn, m_w_ffn_gate, m_w_ffn_up, m_w_ffn_down, m_g_final, v_w_ada, v_b_ada, v_g_mix, v_w_in, v_b_fgate, v_w_br_a, v_w_br_b, v_w_out, v_g_ffn, v_w_ffn_gate, v_w_ffn_up, v_w_ffn_down, v_g_final):
    xi, yi, ci = _pos()
    chip = 2 * xi + yi
    seq = 4 * xi + 2 * yi + ci
    n_ada = w_ada.shape[2]

    c_all = gather_all(c.reshape(8, LANES), "gather_c", False)[0].reshape(8, D)
    b_sh = lax.dynamic_slice(b_ada, (0, chip * n_ada), (1, n_ada))
    mod_all, sc = mod_exchange(c_all, w_ada[0], b_sh)
    mod = lax.dynamic_index_in_dim(mod_all, seq, axis=1, keepdims=False).reshape(6, D)
    mod8 = jnp.pad(mod, ((0, 2), (0, 0)))

    shards = [
        jnp.pad(w_in[0], ((0, 0), (0, IN_SHARD_PAD - IN_SHARD))), w_br_a[0], w_br_b[0], w_out[0],
        jnp.pad(w_ffn_gate[0], ((0, 0), (0, FF_PAD - FF_SHARD))), jnp.pad(w_ffn_up[0], ((0, 0), (0, FF_PAD - FF_SHARD))),
        jnp.pad(w_ffn_down[0], ((0, FF_PAD - FF_SHARD), (0, 0))),
    ]
    shards = [s.astype(BF16) for s in shards]
    core = ci.astype(jnp.int32).reshape(1)
    xs, tgt, g_fin = x[0], loss_target[0], g_final.reshape(1, D)

    g_in = gather_weights(shards[:1])[0]
    w_lay = lay_from_nat(jnp.concatenate([g_in[k][:, :IN_SHARD] for k in range(4)], axis=1))
    tabs = rope_tables()
    h1 = norm_mod_fwd(xs, g_mix, mod8, 0, 1)
    p = in_proj_fwd(h1, w_lay, tabs)
    frow, fraw, fcol = fgate_fwd(p, jnp.pad(b_fgate, ((0, 0), (0, LANES - 8))))
    (ya_att, gcol), bufs = fox_fwd(p, fcol, comm=ag_ici(shards[1:]))
    (yb, lse_b), bufs = dil_fwd(p, comm=ag_d2d(bufs))
    g_bra, g_brb, g_out, w_gate, w_up, g_down = bufs
    w_bra = g_bra.transpose(1, 0, 2).reshape(512, D)
    w_brb = g_brb.transpose(1, 0, 2).reshape(256, D)
    w_o, w_down = g_out.reshape(D, D), g_down.reshape(FFP, D)
    merged, ya, ybp = merge_fwd(ya_att, yb, p, w_bra, w_brb)
    mix, x1, h2 = out_proj_fwd(merged, w_o, xs, mod8, g_ffn)
    a, u, z = ffn_up_fwd(h2, w_gate, w_up)
    dx2, dffn, dg_final, dga_f, loss_part = ffn_down_loss(z, w_down, x1, mod8, g_fin, tgt)

    da, du, dw_down = ffn_down_bwd(dffn, w_down, a, u, z)
    dh2a, _ = mm_nt(da, w_gate, "ffn_gate_dx")
    dh2b, _ = mm_nt(du, w_up, "ffn_up_dx")
    dw_gate = mm_tn(h2, da, "ffn_gate_dw", shard_major=True)
    dw_up = mm_tn(h2, du, "ffn_up_dw", shard_major=True)
    ffn_grads = [dw_gate, dw_up, dw_down.reshape(4, FF_PAD, D)]
    (dx1, dp1, dya_att, dyb, cs_mid, dw_out, dw_bra, dw_brb), oth = mid_bwd(
        dh2a, dh2b, x1, dx2, mix, mod8, g_ffn, p, ya, ybp, merged, ya_att, yb, w_o, w_bra, w_brb, comm=rs_a(ffn_grads))
    pres = [rs_add_halves(g, o, core, f"rs_ffn_halves_{i}") for i, (g, o) in enumerate(zip(ffn_grads, oth))]
    (dp2, dfrow, dfcol), slabs = fox_bwd(p, dya_att, ya_att, gcol, fcol, dp1, comm=rs_b(pres))
    reds = [rs_add_slabs(t, f"rs_ffn_slabs_{i}") for i, t in enumerate(slabs)]
    dp3, db_fg = fgate_bwd(dfrow.reshape(8, S), dfcol, fraw, dp2)
    dp4, (r_gate, r_up, r_down) = dil_bwd(p, dyb, yb, lse_b, tabs, dp3, comm=rs_c(reds))

    dw_lay = mm_tn(h1, dp4, "in_proj_dw")
    dw_in = nat_from_lay(dw_lay).reshape(D, 4, IN_SHARD).transpose(1, 0, 2)
    mix_grads = [
        jnp.pad(dw_in, ((0, 0), (0, 0), (0, IN_SHARD_PAD - IN_SHARD))),
        dw_bra.reshape(512, 4, 256).transpose(1, 0, 2), dw_brb.reshape(256, 4, 256).transpose(1, 0, 2), dw_out.reshape(4, 256, D),
    ]
    dh1, oth = mm_nt(dp4, w_lay, "in_proj_dx", comm=rs_a(mix_grads))
    pres = [rs_add_halves(g, o, core, f"rs_mix_halves_{i}") for i, (g, o) in enumerate(zip(mix_grads, oth))]
    (dx, cs_in), slabs = in_bwd_tail(dh1, xs, dx1, mod8, g_mix, comm=rs_b(pres))
    reds = [rs_add_slabs(t, f"rs_mix_slabs_{i}") for i, t in enumerate(slabs)]
    r_in, r_bra, r_brb, r_out = comm_only(rs_c(reds), "rs_mix_share")
    gpad = dict(w_in=r_in, w_br_a=r_bra, w_br_b=r_brb, w_out=r_out, w_ffn_gate=r_gate, w_ffn_up=r_up, w_ffn_down=r_down)

    dmod = jnp.concatenate([cs_in[0:2], cs_mid[3:4], cs_mid[0:2], dga_f], axis=0)
    small = dict(dmod=dmod, dg_mix=cs_in[2:3], dg_ffn=cs_mid[2:3], dg_final=dg_final, db_fgate=db_fg[:, 0], loss=loss_part[0, 0])
    sv = jnp.concatenate([
        small["dmod"].reshape(48, LANES), small["dg_mix"].reshape(8, LANES), small["dg_ffn"].reshape(8, LANES),
        small["dg_final"].reshape(8, LANES), jnp.pad(small["db_fgate"], (0, LANES - 8)).reshape(1, LANES),
        jnp.broadcast_to(small["loss"], (1, LANES)), jnp.zeros((SMALL_ROWS - 74, LANES), F32)], axis=0)
    sv_all, sv_sum = gather_all(sv, "gather_small", True)
    loss = sv_sum[73, 0]
    g_small = dict(b_ada=sv_sum[0:48].reshape(1, 6 * D), g_mix=sv_sum[48:56].reshape(1, D), g_ffn=sv_sum[56:64].reshape(1, D),
                   g_final=sv_sum[64:72].reshape(D), b_fgate=sv_sum[72, 0:8].reshape(1, 8))

    dmod_all = lax.dynamic_slice(sv_all[:, 0:48, :].reshape(8, 6 * D), (0, chip * n_ada), (8, n_ada))
    g_ada, d_ada, nm_ada, nv_ada = adam_w_ada(jnp.pad(sc.T, ((0, 0), (0, LANES - 8))), jnp.pad(dmod_all, ((0, LANES - 8), (0, 0))),
                                              w_ada[0], m_w_ada[0], v_w_ada[0])

    big = dict(w_in=(w_in, m_w_in, v_w_in), w_br_a=(w_br_a, m_w_br_a, v_w_br_a), w_br_b=(w_br_b, m_w_br_b, v_w_br_b),
               w_out=(w_out, m_w_out, v_w_out), w_ffn_gate=(w_ffn_gate, m_w_ffn_gate, v_w_ffn_gate),
               w_ffn_up=(w_ffn_up, m_w_ffn_up, v_w_ffn_up), w_ffn_down=(w_ffn_down, m_w_ffn_down, v_w_ffn_down))
    upd = {nm: adam(w[0], gpad[nm], m[0], v[0], "adam_" + nm) for nm, (w, m, v) in big.items()}

    def pack(gm, gf, gl, ba, bf):
        rows = [gm.reshape(1, D), gf.reshape(1, D), gl.reshape(1, D), ba.reshape(6, D), jnp.pad(bf.reshape(1, 8), ((0, 0), (0, D - 8)))]
        return jnp.concatenate(rows + [jnp.zeros((6, D), F32)], axis=0)

    packed = adam(pack(g_mix, g_ffn, g_final, b_ada, b_fgate),
                  pack(g_small["g_mix"], g_small["g_ffn"], g_small["g_final"], g_small["b_ada"], g_small["b_fgate"]),
                  pack(m_g_mix, m_g_ffn, m_g_final, m_b_ada, m_b_fgate), pack(v_g_mix, v_g_ffn, v_g_final, v_b_ada, v_b_fgate),
                  "adam_small")

    def unpack(t):
        return dict(g_mix=t[0:1], g_ffn=t[1:2], g_final=t[2], b_ada=t[3:9].reshape(1, 6 * D), b_fgate=t[9:10, 0:8])

    small_upd = [unpack(t) for t in packed[1:]]
    order =["w_ada", "b_ada", "g_mix", "w_in", "b_fgate", "w_br_a", "w_br_b", "w_out", "g_ffn", "w_ffn_gate", "w_ffn_up", "w_ffn_down", "g_final"]

    def leaf(nm, which):
        if nm == "w_ada":
            return (g_ada, d_ada, nm_ada, nv_ada)[which][None]
        if nm in big:
            return upd[nm][which][None]
        return g_small[nm] if which == 0 else small_upd[which - 1][nm]

    outs = [loss, dx[None]]
    for which in range(4):
        outs += [leaf(nm, which) for nm in order]
    return tuple(outs)
```

```python
import functools

import numpy as np
import jax
import jax.numpy as jnp
from jax import lax
from jax.experimental import pallas as pl
from jax.experimental.pallas import tpu as pltpu

F32, BF16 = jnp.float32, jnp.bfloat16
S, D = 2048, 1024
HD = 64
LANES = 128
N_FOX_PAIRS, N_DIL_PAIRS = 4, 2
DIL_GROUPS = ((1, 16), (4, 4), (16, 1))
SPAN = 128
ROT_DIM, ROPE_THETA = 16, 500000.0
D_FF, FF_SHARD, FF_PAD = 2816, 704, 768
FFP = 4 * FF_PAD
IN_COLS, IN_SHARD, IN_SHARD_PAD = 5896, 1474, 1536
LAY_B, LAY_A, LAY_F, LAY_G, LAY_N = 0, 2304, 3840, 4096, 6144
EPS, NEG = 1e-6, -1e30
SCALE = HD ** -0.5
ADAM_LR, ADAM_B1, ADAM_B2, ADAM_EPS, ADAM_WD, ADAM_STEP = 0.001, 0.9, 0.999, 1e-08, 0.01, 10
VMEM_MB = 56
MESH = pl.DeviceIdType.MESH


def _params(vmem_mb=None, **kw):
    if vmem_mb is not None:
        kw["vmem_limit_bytes"] = vmem_mb * 1024 * 1024
    return pltpu.CompilerParams(**kw)


def _sds(shape, dtype):
    return jax.ShapeDtypeStruct(shape, dtype)


def _sigmoid(x):
    return 1.0 / (1.0 + jnp.exp(-x))


def _colsum8(x):
    tm, n = x.shape
    return jnp.sum(x.reshape(tm // 8, 8, n), axis=0)


class Comm:
    def __init__(self, ins, out_shapes, sems, start, wait, aliases=None):
        self.ins, self.out_shapes, self.sems = list(ins), list(out_shapes), list(sems)
        self.start, self.wait, self.aliases = start, wait, dict(aliases or {})


def _hosted_call(body, comm, args, *, name, grid, in_specs, out_specs, out_shape, scratch_shapes=(), aliases=None, vmem_mb=None):
    single = not isinstance(out_shape, (list, tuple))
    out_specs_l = [out_specs] if single else list(out_specs)
    out_shape_l = [out_shape] if single else list(out_shape)
    n_in, n_out, n_scr = len(in_specs), len(out_shape_l), len(scratch_shapes)
    aliases = dict(aliases or {})
    if comm is None:
        res = pl.pallas_call(body, name=name, grid=grid, in_specs=list(in_specs), out_specs=out_specs, out_shape=out_shape,
                             scratch_shapes=list(scratch_shapes), input_output_aliases=aliases,
                             compiler_params=_params(vmem_mb))(*args)
        return res, []
    nci, nco = len(comm.ins), len(comm.out_shapes)

    def wrapped(*refs):
        main_in, cin = refs[:n_in], refs[n_in:n_in + nci]
        o0 = n_in + nci
        main_out, cout = refs[o0:o0 + n_out], refs[o0 + n_out:o0 + n_out + nco]
        s0 = o0 + n_out + nco
        scr, sems = refs[s0:s0 + n_scr], refs[s0 + n_scr:]
        ids = [pl.program_id(i) for i in range(len(grid))]
        first = functools.reduce(jnp.logical_and, [i == 0 for i in ids])
        last = functools.reduce(jnp.logical_and, [i == g - 1 for i, g in zip(ids, grid)])

        @pl.when(first)
        def _():
            comm.start(cin, cout, sems)

        body(*main_in, *main_out, *scr)

        @pl.when(last)
        def _():
            comm.wait(cin, cout, sems)

    for ci, co in comm.aliases.items():
        aliases[n_in + ci] = n_out + co
    any_spec = pl.BlockSpec(memory_space=pl.ANY)
    res = pl.pallas_call(
        wrapped, name=name, grid=grid, in_specs=list(in_specs) + [any_spec] * nci, out_specs=out_specs_l + [any_spec] * nco,
        out_shape=out_shape_l + comm.out_shapes,
        scratch_shapes=list(scratch_shapes) + [pltpu.SemaphoreType.DMA((s,)) for s in comm.sems],
        input_output_aliases=aliases, compiler_params=_params(vmem_mb))(*args, *comm.ins)
    main = list(res[:n_out])
    return (main[0] if single else main), list(res[n_out:])


def norm_mod_fwd(x, g, mod, shift_row, scale_row):
    tm = 256

    def body(x_ref, g_ref, mod_ref, h_ref):
        xv = x_ref[...]
        r = lax.rsqrt(jnp.mean(xv * xv, axis=1, keepdims=True) + EPS)
        n = xv * r * g_ref[...]
        h = n * (1.0 + mod_ref[scale_row:scale_row + 1, :]) + mod_ref[shift_row:shift_row + 1, :]
        h_ref[...] = h.astype(BF16)

    return pl.pallas_call(
        body, name="norm_mod_fwd", grid=(S // tm,),
        in_specs=[pl.BlockSpec((tm, D), lambda i: (i, 0)), pl.BlockSpec((1, D), lambda i: (0, 0)),
                  pl.BlockSpec((8, D), lambda i: (0, 0))],
        out_specs=pl.BlockSpec((tm, D), lambda i: (i, 0)),
        out_shape=_sds((S, D), BF16),
    )(x, g, mod)


def rope_tables():
    pos = jnp.arange(S, dtype=F32)
    inv_freq = ROPE_THETA ** (-jnp.arange(0, ROT_DIM, 2, dtype=F32) / ROT_DIM)
    ang = pos[:, None] * inv_freq[None, :]
    cos, sin = jnp.cos(ang), jnp.sin(ang)
    one, zero = jnp.ones((S, HD - ROT_DIM), F32), jnp.zeros((S, HD - ROT_DIM), F32)
    z8 = jnp.zeros((S, 8), F32)
    c = jnp.concatenate([cos, cos, one], axis=1)
    s1 = jnp.concatenate([-sin, z8, zero], axis=1)
    s2 = jnp.concatenate([z8, sin, zero], axis=1)
    return tuple(jnp.concatenate([t, t], axis=1) for t in (c, s1, s2))


def _rope(y, c, s1, s2):
    return y * c + pltpu.roll(y, LANES - 8, 1) * s1 + pltpu.roll(y, 8, 1) * s2


def _rope_bwd(dy, c, s1, s2):
    return dy * c + pltpu.roll(dy * s1, 8, 1) + pltpu.roll(dy * s2, LANES - 8, 1)


def in_proj_fwd(h, w_lay, tabs, comm=None):
    tm, tn = 2048, 384
    n_rope = 2 * N_DIL_PAIRS * 3 // 2

    def body(a_ref, w_ref, c_ref, s1_ref, s2_ref, o_ref):
        j = pl.program_id(0)
        y = jnp.dot(a_ref[...], w_ref[...], preferred_element_type=F32)

        @pl.when(j < n_rope)
        def _():
            c, s1, s2 = c_ref[...], s1_ref[...], s2_ref[...]
            for t in range(2):
                o_ref[:, LANES * t:LANES * (t + 1)] = _rope(y[:, LANES * t:LANES * (t + 1)], c, s1, s2)
            o_ref[:, 2 * LANES:] = y[:, 2 * LANES:]

        @pl.when(j >= n_rope)
        def _():
            o_ref[...] = y

    tab = pl.BlockSpec((tm, LANES), lambda j, i: (i, 0))
    return _hosted_call(
        body, comm, (h, w_lay, *tabs), name="in_proj_fwd", grid=(LAY_N // tn, S // tm),
        in_specs=[pl.BlockSpec((tm, D), lambda j, i: (i, 0)), pl.BlockSpec((D, tn), lambda j, i: (0, j)), tab, tab, tab],
        out_specs=pl.BlockSpec((tm, tn), lambda j, i: (i, j)),
        out_shape=_sds((S, LAY_N), F32), vmem_mb=VMEM_MB)


def _log1p_small(t):
    return jnp.where(t < 1e-2, t * (1.0 - t * (0.5 - t * (1.0 / 3.0))), jnp.log(1.0 + t))


def fgate_fwd(p, b_pad):
    def body(fa_ref, b_ref, frow_ref, fraw_ref, fcol_ref):
        f = fa_ref[...] + b_ref[...]
        fr = f.T[0:8, :]
        ls = jnp.minimum(fr, 0.0) - _log1p_small(jnp.exp(-jnp.abs(fr)))
        lane = lax.broadcasted_iota(jnp.int32, (8, S), 1)
        acc, sh = ls, 1
        while sh < S:
            acc = acc + jnp.where(lane >= sh, pltpu.roll(acc, sh, 1), 0.0)
            sh *= 2
        frow_ref[...] = acc
        fraw_ref[...] = fr
        for hh in range(8):
            fcol_ref[hh] = jnp.broadcast_to(acc[hh:hh + 1, :], (LANES, S)).T

    return pl.pallas_call(
        body, name="fgate_fwd", grid=(1,),
        in_specs=[pl.BlockSpec((S, LANES), lambda i: (0, LAY_F // LANES)), pl.BlockSpec((1, LANES), lambda i: (0, 0))],
        out_specs=[pl.BlockSpec((8, S), lambda i: (0, 0)), pl.BlockSpec((8, S), lambda i: (0, 0)),
                   pl.BlockSpec((8, S, LANES), lambda i: (0, 0, 0))],
        out_shape=[_sds((8, S), F32), _sds((8, S), F32), _sds((8, S, LANES), F32)],
        compiler_params=_params(VMEM_MB),
    )(p, b_pad)


def _head_masks(rows):
    lane = lax.broadcasted_iota(jnp.int32, (rows, LANES), 1)
    return lane < HD, lane >= HD


FT = 256
FS = 256


def _split3(f):
    hi = f.astype(BF16).astype(F32)
    r = f - hi
    mid = r.astype(BF16).astype(F32)
    return hi, mid, r - mid


def _fox_operands(qkv_ref, tcol_ref, scol_ref, qa_s, ka_s):
    rows = 256
    lane = lax.broadcasted_iota(jnp.int32, (rows, LANES), 1)

    def chunk(i, _):
        r = pl.ds(pl.multiple_of(i * rows, rows), rows)
        q, k = qkv_ref[r, 0:LANES], qkv_ref[r, LANES:2 * LANES]
        for hh in range(2):
            own = (lane < HD) if hh == 0 else (lane >= HD)
            b = HD if hh == 0 else 0
            t3, s3 = _split3(tcol_ref[hh, r, :]), _split3(scol_ref[hh, r, :])
            qa = jnp.where(lane < b + 3, 1.0, jnp.where(lane == b + 3, t3[0], jnp.where(lane == b + 4, t3[1], jnp.where(lane == b + 5, t3[2], 0.0))))
            ka = jnp.where(lane == b, -s3[0], jnp.where(lane == b + 1, -s3[1], jnp.where(lane == b + 2, -s3[2], jnp.where(lane < b + 6, 1.0, 0.0))))
            qa_s[hh, r, :] = jnp.where(own, q * SCALE, qa).astype(BF16)
            ka_s[hh, r, :] = jnp.where(own, k, ka).astype(BF16)
        return 0

    lax.fori_loop(0, S // rows, chunk, 0)


def fox_fwd(p, fcol, comm=None):
    nt = (((1,), (1,)), ((), ()))

    def body(qkv_ref, fc_ref, o_ref, g_ref, qa_s, ka_s):
        _fox_operands(qkv_ref, fc_ref, fc_ref, qa_s, ka_s)
        masks = _head_masks(FT)
        causal = lax.broadcasted_iota(jnp.int32, (FT, FT), 1) <= lax.broadcasted_iota(jnp.int32, (FT, FT), 0)

        def qloop(qi, _):
            q0 = pl.multiple_of(qi * FT, FT)
            qa = [qa_s[hh, pl.ds(q0, FT), :] for hh in range(2)]

            def step(kb, carry, diagonal):
                k0 = pl.multiple_of(kb * FT, FT)
                v = qkv_ref[pl.ds(k0, FT), 2 * LANES:3 * LANES].astype(BF16)
                out = []
                for hh in range(2):
                    ka = ka_s[hh, pl.ds(k0, FT), :]
                    ms, ls, accs = [], [], []
                    for r in range(FT // FS):
                        rows = slice(r * FS, (r + 1) * FS)
                        m, l, acc = [t[rows] for t in carry[3 * hh:3 * hh + 3]]
                        s = lax.dot_general(qa[hh][rows], ka, nt, preferred_element_type=F32)
                        if diagonal:
                            s = jnp.where(causal[rows], s, NEG)
                        m_new = jnp.maximum(m, jnp.max(s, axis=1, keepdims=True))
                        pr = jnp.exp(s - m_new)
                        alpha = jnp.exp(m - m_new)
                        ms.append(m_new)
                        ls.append(l * alpha + jnp.sum(pr, axis=1, keepdims=True))
                        accs.append(acc * alpha + jnp.dot(pr.astype(BF16), v, preferred_element_type=F32))
                    out += [jnp.concatenate(t, axis=0) for t in (ms, ls, accs)]
                return tuple(out)

            init = (jnp.full((FT, 1), NEG, F32), jnp.zeros((FT, 1), F32), jnp.zeros((FT, LANES), F32)) * 2
            carry = lax.fori_loop(0, qi, lambda kb, cr: step(kb, cr, False), init)
            m0, l0, a0, m1, l1, a1 = step(qi, carry, True)
            o_ref[pl.ds(q0, FT), :] = jnp.where(masks[0], a0 / l0, a1 / l1).astype(BF16)
            g_ref[0, pl.ds(q0, FT), :] = fc_ref[0, pl.ds(q0, FT), :] - (m0 + jnp.log(l0))
            g_ref[1, pl.ds(q0, FT), :] = fc_ref[1, pl.ds(q0, FT), :] - (m1 + jnp.log(l1))
            return 0

        lax.fori_loop(0, S // FT, qloop, 0)

    a_blk = LAY_A // 384
    return _hosted_call(
        body, comm, (p, fcol), name="fox_fwd", grid=(N_FOX_PAIRS,),
        in_specs=[pl.BlockSpec((S, 384), lambda p_: (0, a_blk + p_)), pl.BlockSpec((2, S, LANES), lambda p_: (p_, 0, 0))],
        out_specs=[pl.BlockSpec((S, LANES), lambda p_: (0, p_)), pl.BlockSpec((2, S, LANES), lambda p_: (p_, 0, 0))],
        out_shape=[_sds((S, 4 * LANES), BF16), _sds((8, S, LANES), F32)],
        scratch_shapes=[pltpu.VMEM((2, S, LANES), BF16)] * 2,
        vmem_mb=VMEM_MB)


def _dil_rows(ref, start, d):
    return ref[pl.ds(start, SPAN), :] if d == 1 else ref[pl.ds(start, SPAN, stride=d), :]


def _dil_store(ref, start, d, val):
    if d == 1:
        ref[pl.ds(start, SPAN), :] = val
    else:
        ref[pl.ds(start, SPAN, stride=d), :] = val


def _band_mask(has_prev):
    qi = lax.broadcasted_iota(jnp.int32, (SPAN, 2 * SPAN), 0) + SPAN
    kj = lax.broadcasted_iota(jnp.int32, (SPAN, 2 * SPAN), 1)
    dist = qi - kj
    return (dist >= 0) & (dist <= SPAN) & (has_prev | (kj >= SPAN))


def _dil_block(n, d, nb):
    r, j = n // nb, n % nb
    start = r + d * SPAN * j
    prev = jnp.maximum(start - d * SPAN, r)
    return start, prev, j > 0


def dil_fwd(p, comm=None):
    def body(*refs):
        qkv = [refs[3 * g:3 * g + 3] for g in range(3)]
        y_ref, lse_ref = refs[9], refs[10]
        acc_s, m_s, l_s = refs[11], refs[12], refs[13]
        masks = _head_masks(SPAN)
        for g, (d, nb) in enumerate(DIL_GROUPS):
            q_ref, k_ref, v_ref = qkv[g]

            def blk(n, _):
                start, prev, has_prev = _dil_block(n, d, nb)
                q = _dil_rows(q_ref, start, d)
                kc = jnp.concatenate([_dil_rows(k_ref, prev, d), _dil_rows(k_ref, start, d)], axis=0).astype(BF16)
                vc = jnp.concatenate([_dil_rows(v_ref, prev, d), _dil_rows(v_ref, start, d)], axis=0).astype(BF16)
                valid = _band_mask(has_prev)
                accs, ms, ls = [], [], []
                for hh in range(2):
                    qm = (jnp.where(masks[hh], q, 0.0) * SCALE).astype(BF16)
                    s = lax.dot_general(qm, kc, (((1,), (1,)), ((), ())), preferred_element_type=F32)
                    s = jnp.where(valid, s, NEG)
                    m = jnp.max(s, axis=1, keepdims=True)
                    pr = jnp.exp(s - m)
                    ls.append(jnp.sum(pr, axis=1, keepdims=True))
                    ms.append(m)
                    accs.append(jnp.dot(pr.astype(BF16), vc, preferred_element_type=F32))
                _dil_store(acc_s.at[g], start, d, jnp.where(masks[0], accs[0], accs[1]))
                _dil_store(m_s.at[g], start, d, jnp.where(masks[0], ms[0], ms[1]))
                _dil_store(l_s.at[g], start, d, jnp.where(masks[0], ls[0], ls[1]))
                return 0

            lax.fori_loop(0, 16, blk, 0)

        def merge(i, _):
            rows = pl.ds(pl.multiple_of(i * 256, 256), 256)
            m = [m_s[g, rows, :] for g in range(3)]
            mx = jnp.maximum(jnp.maximum(m[0], m[1]), m[2])
            w = [jnp.exp(m[g] - mx) for g in range(3)]
            l = sum(l_s[g, rows, :] * w[g] for g in range(3))
            y_ref[rows, :] = sum(acc_s[g, rows, :] * w[g] for g in range(3)) / l
            lse_ref[rows, :] = mx + jnp.log(l)
            return 0

        lax.fori_loop(0, S // 256, merge, 0)

    def spec(g, t):
        return pl.BlockSpec((S, LANES), lambda p_: (0, (p_ * 3 + g) * 3 + t))

    return _hosted_call(
        body, comm, [p] * 9, name="dil_fwd", grid=(N_DIL_PAIRS,),
        in_specs=[spec(g, t) for g in range(3) for t in range(3)],
        out_specs=[pl.BlockSpec((S, LANES), lambda p_: (0, p_)), pl.BlockSpec((S, LANES), lambda p_: (0, p_))],
        out_shape=[_sds((S, 2 * LANES), F32), _sds((S, 2 * LANES), F32)],
        scratch_shapes=[pltpu.VMEM((3, S, LANES), F32)] * 3,
        vmem_mb=VMEM_MB)


def merge_fwd(ya_att, yb, p, w_bra, w_brb, comm=None):
    tm = 256
    gblk = LAY_G // D

    def body(a_ref, b_ref, ga_ref, gb_ref, wa_ref, wb_ref, mg_ref, ya_ref, yb_ref):
        ya = jnp.dot(a_ref[...], wa_ref[...], preferred_element_type=F32)
        ybp = jnp.dot(b_ref[...].astype(BF16), wb_ref[...], preferred_element_type=F32)
        mg_ref[...] = (_sigmoid(ga_ref[...]) * ya + _sigmoid(gb_ref[...]) * ybp).astype(BF16)
        ya_ref[...] = ya
        yb_ref[...] = ybp

    row = lambda w: pl.BlockSpec((tm, w), lambda i: (i, 0))
    return _hosted_call(
        body, comm, (ya_att, yb, p, p, w_bra, w_brb), name="merge_fwd", grid=(S // tm,),
        in_specs=[row(512), row(256), pl.BlockSpec((tm, D), lambda i: (i, gblk)), pl.BlockSpec((tm, D), lambda i: (i, gblk + 1)),
                  pl.BlockSpec((512, D), lambda i: (0, 0)), pl.BlockSpec((256, D), lambda i: (0, 0))],
        out_specs=[row(D), row(D), row(D)],
        out_shape=[_sds((S, D), BF16), _sds((S, D), F32), _sds((S, D), F32)])


def out_proj_fwd(merged, w_out, x, mod, g_ffn):
    tm = 256

    def body(a_ref, w_ref, x_ref, mod_ref, g_ref, mix_ref, x1_ref, h2_ref):
        mix = jnp.dot(a_ref[...], w_ref[...], preferred_element_type=F32)
        x1 = x_ref[...] + mod_ref[2:3, :] * mix
        r = lax.rsqrt(jnp.mean(x1 * x1, axis=1, keepdims=True) + EPS)
        h2 = (x1 * r * g_ref[...]) * (1.0 + mod_ref[4:5, :]) + mod_ref[3:4, :]
        mix_ref[...] = mix
        x1_ref[...] = x1
        h2_ref[...] = h2.astype(BF16)

    row = pl.BlockSpec((tm, D), lambda i: (i, 0))
    return pl.pallas_call(
        body, name="out_proj_fwd", grid=(S // tm,),
        in_specs=[row, pl.BlockSpec((D, D), lambda i: (0, 0)), row, pl.BlockSpec((8, D), lambda i: (0, 0)),
                  pl.BlockSpec((1, D), lambda i: (0, 0))],
        out_specs=[row, row, row],
        out_shape=[_sds((S, D), F32), _sds((S, D), F32), _sds((S, D), BF16)],
    )(merged, w_out, x, mod, g_ffn)


def ffn_up_fwd(h2, w_gate, w_up):
    tm = 1024

    def body(h_ref, wg_ref, wu_ref, a_ref, u_ref, z_ref):
        h = h_ref[...]
        a = jnp.dot(h, wg_ref[...], preferred_element_type=F32)
        u = jnp.dot(h, wu_ref[...], preferred_element_type=F32)
        a_ref[...] = a
        u_ref[...] = u
        z_ref[...] = (a * _sigmoid(a) * u).astype(BF16)

    out = pl.BlockSpec((tm, FF_PAD), lambda k, i: (i, k))
    return pl.pallas_call(
        body, name="ffn_up_fwd", grid=(4, S // tm),
        in_specs=[pl.BlockSpec((tm, D), lambda k, i: (i, 0)), pl.BlockSpec((None, D, FF_PAD), lambda k, i: (k, 0, 0)),
                  pl.BlockSpec((None, D, FF_PAD), lambda k, i: (k, 0, 0))],
        out_specs=[out, out, out],
        out_shape=[_sds((S, FFP), F32), _sds((S, FFP), F32), _sds((S, FFP), BF16)], compiler_params=_params(VMEM_MB),
    )(h2, w_gate, w_up)


def ffn_down_loss(z, w_down, x1, mod, g_final, tgt):
    tm = 256

    def body(z_ref, w_ref, x1_ref, mod_ref, g_ref, t_ref, dx2_ref, dffn_ref, dg_ref, dga_ref, loss_ref, s_dg, s_dga, s_loss):
        i = pl.program_id(0)

        @pl.when(i == 0)
        def _():
            s_dg[...] = jnp.zeros_like(s_dg)
            s_dga[...] = jnp.zeros_like(s_dga)
            s_loss[...] = jnp.zeros_like(s_loss)

        ffn = jnp.dot(z_ref[...], w_ref[...], preferred_element_type=F32)
        gaf = mod_ref[5:6, :]
        x2 = x1_ref[...] + gaf * ffn
        r = lax.rsqrt(jnp.mean(x2 * x2, axis=1, keepdims=True) + EPS)
        xh = x2 * r
        g = g_ref[...]
        e = xh * g - t_ref[...]
        s_loss[...] += 0.5 * jnp.sum(jnp.mean(e * e, axis=1, keepdims=True), axis=0, keepdims=True)
        dy = e * (1.0 / D)
        gdy = dy * g
        dx2 = r * (gdy - xh * jnp.mean(gdy * xh, axis=1, keepdims=True))
        s_dg[...] += _colsum8(dy * xh)
        s_dga[...] += _colsum8(dx2 * ffn)
        dx2_ref[...] = dx2
        dffn_ref[...] = (dx2 * gaf).astype(BF16)

        @pl.when(i == pl.num_programs(0) - 1)
        def _():
            dg_ref[...] = jnp.sum(s_dg[...], axis=0, keepdims=True)
            dga_ref[...] = jnp.sum(s_dga[...], axis=0, keepdims=True)
            loss_ref[...] = jnp.broadcast_to(s_loss[...], (1, LANES))

    row = pl.BlockSpec((tm, D), lambda i: (i, 0))
    vec = pl.BlockSpec((1, D), lambda i: (0, 0))
    return pl.pallas_call(
        body, name="ffn_down_loss", grid=(S // tm,),
        in_specs=[pl.BlockSpec((tm, FFP), lambda i: (i, 0)), pl.BlockSpec((FFP, D), lambda i: (0, 0)), row,
                  pl.BlockSpec((8, D), lambda i: (0, 0)), vec, row],
        out_specs=[row, row, vec, vec, pl.BlockSpec((1, LANES), lambda i: (0, 0))],
        out_shape=[_sds((S, D), F32), _sds((S, D), BF16), _sds((1, D), F32), _sds((1, D), F32), _sds((1, LANES), F32)],
        scratch_shapes=[pltpu.VMEM((8, D), F32), pltpu.VMEM((8, D), F32), pltpu.VMEM((1, 1), F32)],
        compiler_params=_params(VMEM_MB),
    )(z, w_down, x1, mod, g_final, tgt)


def ffn_down_bwd(dffn, w_down, a, u, z):
    tm, tn = 1024, 768

    def body(d_ref, w_ref, a_ref, u_ref, z_ref, da_ref, du_ref, dw_ref):
        i = pl.program_id(1)
        dff = d_ref[...]
        dz = lax.dot_general(dff, w_ref[...], (((1,), (1,)), ((), ())), preferred_element_type=F32)
        av, uv = a_ref[...], u_ref[...]
        sg = _sigmoid(av)
        du_ref[...] = (dz * (av * sg)).astype(BF16)
        da_ref[...] = (dz * uv * (sg * (1.0 + av * (1.0 - sg)))).astype(BF16)
        dw = lax.dot_general(z_ref[...], dff, (((0,), (0,)), ((), ())), preferred_element_type=F32)

        @pl.when(i == 0)
        def _():
            dw_ref[...] = dw

        @pl.when(i > 0)
        def _():
            dw_ref[...] += dw

    tile = pl.BlockSpec((tm, tn), lambda j, i: (i, j))
    return pl.pallas_call(
        body, name="ffn_down_bwd", grid=(FFP // tn, S // tm),
        in_specs=[pl.BlockSpec((tm, D), lambda j, i: (i, 0)), pl.BlockSpec((tn, D), lambda j, i: (j, 0)), tile, tile, tile],
        out_specs=[tile, tile, pl.BlockSpec((tn, D), lambda j, i: (j, 0))],
        out_shape=[_sds((S, FFP), BF16), _sds((S, FFP), BF16), _sds((FFP, D), F32)], compiler_params=_params(VMEM_MB),
    )(dffn, w_down, a, u, z)


def mm_nt(dy, w, name, comm=None):
    tm = 1024
    n = dy.shape[1]
    if w.ndim == 2:
        k_in, tk = w.shape[0], 768
        w_spec = pl.BlockSpec((k_in, tk), lambda i, k: (0, k))
    else:
        k_in, tk = w.shape[1], FF_PAD
        w_spec = pl.BlockSpec((None, k_in, tk), lambda i, k: (k, 0, 0))
    nk = n // tk

    def body(d_ref, w_ref, o_ref, acc):
        k = pl.program_id(1)
        part = lax.dot_general(d_ref[...], w_ref[...], (((1,), (1,)), ((), ())), preferred_element_type=F32)

        @pl.when(k == 0)
        def _():
            acc[...] = part

        @pl.when(k > 0)
        def _():
            acc[...] += part

        @pl.when(k == nk - 1)
        def _():
            o_ref[...] = acc[...]

    return _hosted_call(
        body, comm, (dy, w), name=name, grid=(S // tm, nk),
        in_specs=[pl.BlockSpec((tm, tk), lambda i, k: (i, k)), w_spec],
        out_specs=pl.BlockSpec((tm, k_in), lambda i, k: (i, 0)),
        out_shape=_sds((S, k_in), F32),
        scratch_shapes=[pltpu.VMEM((tm, k_in), F32)], vmem_mb=VMEM_MB)


def mm_tn(h, dy, name, shard_major=False, comm=None):
    tm, tn = 2048, 768
    k_in, n = h.shape[1], dy.shape[1]

    def body(h_ref, d_ref, o_ref):
        i = pl.program_id(1)
        dw = lax.dot_general(h_ref[...], d_ref[...], (((0,), (0,)), ((), ())), preferred_element_type=F32)

        @pl.when(i == 0)
        def _():
            o_ref[...] = dw

        @pl.when(i > 0)
        def _():
            o_ref[...] += dw

    if shard_major:
        out_spec, out_shape = pl.BlockSpec((None, k_in, tn), lambda j, i: (j, 0, 0)), _sds((n // tn, k_in, tn), F32)
    else:
        out_spec, out_shape = pl.BlockSpec((k_in, tn), lambda j, i: (0, j)), _sds((k_in, n), F32)
    return _hosted_call(
        body, comm, (h, dy), name=name, grid=(n // tn, S // tm),
        in_specs=[pl.BlockSpec((tm, k_in), lambda j, i: (i, 0)), pl.BlockSpec((tm, tn), lambda j, i: (i, j))],
        out_specs=out_spec, out_shape=out_shape, vmem_mb=VMEM_MB)


def mid_bwd(dh2a, dh2b, x1, dx2, mix, mod, g_ffn, p, ya, ybp, merged, ya_att, yb, w_out, w_bra, w_brb, comm=None):
    tm = 256
    gblk = LAY_G // D
    nsteps = S // tm

    def body(dha_ref, dhb_ref, x1_ref, dx2_ref, mix_ref, mod_ref, g_ref, ga_ref, gb_ref, ya_ref, yb_ref, mg_ref,
             att_ref, ybb_ref, wo_ref, wa_ref, wb_ref,
             dx1_ref, dpg_ref, datt_ref, dyb_ref, cs_ref, dwo_ref, dwa_ref, dwb_ref, s_cs):
        i = pl.program_id(0)

        @pl.when(i == 0)
        def _():
            s_cs[...] = jnp.zeros_like(s_cs)
            dwo_ref[...] = jnp.zeros_like(dwo_ref)
            dwa_ref[...] = jnp.zeros_like(dwa_ref)
            dwb_ref[...] = jnp.zeros_like(dwb_ref)

        x1 = x1_ref[...]
        g = g_ref[...]
        r = lax.rsqrt(jnp.mean(x1 * x1, axis=1, keepdims=True) + EPS)
        xh = x1 * r
        dh2 = dha_ref[...] + dhb_ref[...]
        s_cs[0] += _colsum8(dh2)
        s_cs[1] += _colsum8(dh2 * (xh * g))
        dn2 = dh2 * (1.0 + mod_ref[4:5, :])
        s_cs[2] += _colsum8(dn2 * xh)
        gd = dn2 * g
        dx1 = dx2_ref[...] + r * (gd - xh * jnp.mean(gd * xh, axis=1, keepdims=True))
        s_cs[3] += _colsum8(dx1 * mix_ref[...])
        dx1_ref[...] = dx1
        dmix = (dx1 * mod_ref[2:3, :]).astype(BF16)
        dmg = lax.dot_general(dmix, wo_ref[...], (((1,), (1,)), ((), ())), preferred_element_type=F32)
        sga, sgb = _sigmoid(ga_ref[...]), _sigmoid(gb_ref[...])
        dya = (dmg * sga).astype(BF16)
        dybp = (dmg * sgb).astype(BF16)
        dpg_ref[:, 0:D] = (dmg * ya_ref[...] * (sga * (1.0 - sga))).astype(BF16)
        dpg_ref[:, D:2 * D] = (dmg * yb_ref[...] * (sgb * (1.0 - sgb))).astype(BF16)
        datt_ref[...] = lax.dot_general(dya, wa_ref[...], (((1,), (1,)), ((), ())), preferred_element_type=F32).astype(BF16)
        dyb_ref[...] = lax.dot_general(dybp, wb_ref[...], (((1,), (1,)), ((), ())), preferred_element_type=F32)
        tn_dims = (((0,), (0,)), ((), ()))
        dwo_ref[...] += lax.dot_general(mg_ref[...], dmix, tn_dims, preferred_element_type=F32)
        dwa_ref[...] += lax.dot_general(att_ref[...], dya, tn_dims, preferred_element_type=F32)
        dwb_ref[...] += lax.dot_general(ybb_ref[...].astype(BF16), dybp, tn_dims, preferred_element_type=F32)

        @pl.when(i == nsteps - 1)
        def _():
            for t in range(4):
                cs_ref[t:t + 1, :] = jnp.sum(s_cs[t], axis=0, keepdims=True)
            cs_ref[4:8, :] = jnp.zeros((4, D), F32)

    row = lambda w: pl.BlockSpec((tm, w), lambda i: (i, 0))
    full = lambda a, b: pl.BlockSpec((a, b), lambda i: (0, 0))
    return _hosted_call(
        body, comm, (dh2a, dh2b, x1, dx2, mix, mod, g_ffn, p, p, ya, ybp, merged, ya_att, yb, w_out, w_bra, w_brb),
        name="mid_bwd", grid=(nsteps,),
        in_specs=[row(D), row(D), row(D), row(D), row(D), full(8, D), full(1, D),
                  pl.BlockSpec((tm, D), lambda i: (i, gblk)), pl.BlockSpec((tm, D), lambda i: (i, gblk + 1)),
                  row(D), row(D), row(D), row(512), row(256), full(D, D), full(512, D), full(256, D)],
        out_specs=[row(D), pl.BlockSpec((tm, 2 * D), lambda i: (i, LAY_G // (2 * D))), row(512), row(256), full(8, D),
                   full(D, D), full(512, D), full(256, D)],
        out_shape=[_sds((S, D), F32), _sds((S, LAY_N), BF16), _sds((S, 512), BF16), _sds((S, 256), F32), _sds((8, D), F32),
                   _sds((D, D), F32), _sds((512, D), F32), _sds((256, D), F32)],
        scratch_shapes=[pltpu.VMEM((4, 8, D), F32)],
        vmem_mb=VMEM_MB)


def fox_bwd(p, do, o, gcol, fcol, dp, comm=None):
    nq = S // FT
    nt = (((1,), (1,)), ((), ()))
    tn = (((0,), (0,)), ((), ()))

    def body(qkv_ref, do_ref, o_ref, g_ref, fc_ref, dp_in, dp_ref, df_ref, rs_ref, dq_s, qa_s, ka_s, qm_s, dob_s, dl_s):
        del dp_in
        _fox_operands(qkv_ref, g_ref, fc_ref, qa_s, ka_s)
        masks = _head_masks(FT)
        lane = lax.broadcasted_iota(jnp.int32, (FT, LANES), 1)
        head0 = 2 * pl.program_id(0)
        causal = lax.broadcasted_iota(jnp.int32, (FT, FT), 1) <= lax.broadcasted_iota(jnp.int32, (FT, FT), 0)
        dq_s[...] = jnp.zeros_like(dq_s)
        rs_ref[...] = jnp.zeros_like(rs_ref)

        def prep(i, _):
            r = pl.ds(pl.multiple_of(i * 256, 256), 256)
            m256 = _head_masks(256)
            dov, ov = do_ref[r, :].astype(F32), o_ref[r, :].astype(F32)
            for hh in range(2):
                dom = jnp.where(m256[hh], dov, 0.0)
                dob_s[hh, r, :] = dom.astype(BF16)
                dl_s[hh, r, :] = jnp.broadcast_to(jnp.sum(dom * ov, axis=1, keepdims=True), (256, LANES))
                qm_s[hh, r, :] = jnp.where(m256[hh], qa_s[hh, r, :], 0.0).astype(BF16)
            return 0

        lax.fori_loop(0, S // 256, prep, 0)

        def kloop(kb, _):
            k0 = pl.multiple_of(kb * FT, FT)
            k = qkv_ref[pl.ds(k0, FT), LANES:2 * LANES].astype(BF16)
            v = qkv_ref[pl.ds(k0, FT), 2 * LANES:3 * LANES].astype(BF16)
            ka = [ka_s[hh, pl.ds(k0, FT), :] for hh in range(2)]

            def step(qi, carry, diagonal):
                dk, dv, df0, df1 = carry
                q0 = pl.multiple_of(qi * FT, FT)
                dqs, dfs = [], []
                rs = jnp.zeros((FT, LANES), F32)
                for hh in range(2):
                    dob = dob_s[hh, pl.ds(q0, FT), :]
                    s = lax.dot_general(qa_s[hh, pl.ds(q0, FT), :], ka[hh], nt, preferred_element_type=F32)
                    pr = jnp.exp(jnp.where(causal, s, NEG)) if diagonal else jnp.exp(s)
                    dpr = lax.dot_general(dob, v, nt, preferred_element_type=F32)
                    ds = pr * (dpr - jnp.tile(dl_s[hh, pl.ds(q0, FT), :], (1, FT // LANES)))
                    dsb = ds.astype(BF16)
                    dqs.append(jnp.dot(dsb, k, preferred_element_type=F32) * SCALE)
                    dk = dk + lax.dot_general(dsb, qm_s[hh, pl.ds(q0, FT), :], tn, preferred_element_type=F32)
                    dv = dv + lax.dot_general(pr.astype(BF16), dob, tn, preferred_element_type=F32)
                    dfs.append(jnp.sum(ds, axis=0, keepdims=True))
                    rs = rs + jnp.where(lane == head0 + hh, jnp.sum(ds, axis=1, keepdims=True), 0.0)
                dq_s[pl.ds(q0, FT), :] += jnp.where(masks[0], dqs[0], dqs[1])
                rs_ref[pl.ds(q0, FT), :] += rs
                return dk, dv, df0 - dfs[0], df1 - dfs[1]

            z = jnp.zeros((FT, LANES), F32)
            z1 = jnp.zeros((1, FT), F32)
            carry = step(kb, (z, z, z1, z1), True)
            dk, dv, df0, df1 = lax.fori_loop(kb + 1, nq, lambda qi, cr: step(qi, cr, False), carry)
            dp_ref[pl.ds(k0, FT), LANES:2 * LANES] = dk.astype(BF16)
            dp_ref[pl.ds(k0, FT), 2 * LANES:3 * LANES] = dv.astype(BF16)
            df_ref[0:1, pl.ds(k0, FT)] = df0
            df_ref[1:2, pl.ds(k0, FT)] = df1
            return 0

        lax.fori_loop(0, S // FT, kloop, 0)
        dp_ref[:, 0:LANES] = dq_s[...].astype(BF16)

    a_blk = LAY_A // 384
    pair = pl.BlockSpec((S, LANES), lambda p_: (0, p_))
    heads = pl.BlockSpec((2, S, LANES), lambda p_: (p_, 0, 0))
    return _hosted_call(
        body, comm, (p, do, o, gcol, fcol, dp), name="fox_bwd", grid=(N_FOX_PAIRS,),
        in_specs=[pl.BlockSpec((S, 384), lambda p_: (0, a_blk + p_)), pair, pair, heads, heads, pl.BlockSpec(memory_space=pl.ANY)],
        out_specs=[pl.BlockSpec((S, 384), lambda p_: (0, a_blk + p_)), pl.BlockSpec((None, 2, S), lambda p_: (p_, 0, 0)),
                   pl.BlockSpec((None, S, LANES), lambda p_: (p_, 0, 0))],
        out_shape=[_sds((S, LAY_N), BF16), _sds((4, 2, S), F32), _sds((4, S, LANES), F32)],
        scratch_shapes=[pltpu.VMEM((S, LANES), F32)] + [pltpu.VMEM((2, S, LANES), BF16)] * 4 + [pltpu.VMEM((2, S, LANES), F32)],
        aliases={5: 0}, vmem_mb=VMEM_MB)


def fgate_bwd(dfrow, dfcol, fraw, dp):
    def body(df_ref, dc_ref, f_ref, dp_in, dpf_ref, db_ref):
        del dp_in
        lane = lax.broadcasted_iota(jnp.int32, (8, S), 1)
        rsum = (dc_ref[0] + dc_ref[1]) + (dc_ref[2] + dc_ref[3])
        acc, sh = df_ref[...] + rsum.T[0:8, :], 1
        while sh < S:
            acc = acc + jnp.where(lane < S - sh, pltpu.roll(acc, S - sh, 1), 0.0)
            sh *= 2
        df = acc * _sigmoid(-f_ref[...])
        db_ref[...] = jnp.broadcast_to(jnp.sum(df, axis=1, keepdims=True), (8, LANES))
        dfc = jnp.concatenate([df, jnp.zeros((LANES - 8, S), F32)], axis=0).T
        dpf_ref[:, 0:LANES] = dfc.astype(BF16)
        dpf_ref[:, LANES:2 * LANES] = jnp.zeros((S, LANES), BF16)

    return pl.pallas_call(
        body, name="fgate_bwd", grid=(1,),
        in_specs=[pl.BlockSpec((8, S), lambda i: (0, 0)), pl.BlockSpec((4, S, LANES), lambda i: (0, 0, 0)),
                  pl.BlockSpec((8, S), lambda i: (0, 0)), pl.BlockSpec(memory_space=pl.ANY)],
        out_specs=[pl.BlockSpec((S, 2 * LANES), lambda i: (0, LAY_F // (2 * LANES))), pl.BlockSpec((8, LANES), lambda i: (0, 0))],
        out_shape=[_sds((S, LAY_N), BF16), _sds((8, LANES), F32)],
        input_output_aliases={3: 0},
        compiler_params=_params(VMEM_MB),
    )(dfrow, dfcol, fraw, dp)


def dil_bwd(p, dyb, yb, lse, tabs, dp, comm=None):
    def body(*refs):
        qkv = [refs[3 * g:3 * g + 3] for g in range(3)]
        dy_ref, y_ref, lse_ref, c_ref, s1_ref, s2_ref = refs[9:15]
        dp_ref = refs[16]
        dq_s, dk_s, dv_s, dl_s = refs[17:21]
        masks = _head_masks(SPAN)
        m256 = _head_masks(256)
        nt = (((1,), (1,)), ((), ()))
        tn = (((0,), (0,)), ((), ()))
        dk_s[...] = jnp.zeros_like(dk_s)
        dv_s[...] = jnp.zeros_like(dv_s)

        def prep(i, _):
            rows = pl.ds(pl.multiple_of(i * 256, 256), 256)
            pr = dy_ref[rows, :] * y_ref[rows, :]
            d0 = jnp.sum(jnp.where(m256[0], pr, 0.0), axis=1, keepdims=True)
            d1 = jnp.sum(jnp.where(m256[1], pr, 0.0), axis=1, keepdims=True)
            dl_s[rows, :] = jnp.where(m256[0], d0, d1)
            return 0

        lax.fori_loop(0, S // 256, prep, 0)

        for g, (d, nb) in enumerate(DIL_GROUPS):
            q_ref, k_ref, v_ref = qkv[g]

            def blk(n, _):
                start, prev, has_prev = _dil_block(n, d, nb)
                q = _dil_rows(q_ref, start, d)
                kc = jnp.concatenate([_dil_rows(k_ref, prev, d), _dil_rows(k_ref, start, d)], axis=0).astype(BF16)
                vc = jnp.concatenate([_dil_rows(v_ref, prev, d), _dil_rows(v_ref, start, d)], axis=0).astype(BF16)
                dov = _dil_rows(dy_ref, start, d)
                lsev = _dil_rows(lse_ref, start, d)
                dlv = _dil_rows(dl_s, start, d)
                valid = _band_mask(has_prev)
                dqs = []
                dkc = jnp.zeros((2 * SPAN, LANES), F32)
                dvc = jnp.zeros((2 * SPAN, LANES), F32)
                for hh in range(2):
                    qm = (jnp.where(masks[hh], q, 0.0) * SCALE).astype(BF16)
                    dob = jnp.where(masks[hh], dov, 0.0).astype(BF16)
                    lse_h = jnp.max(jnp.where(masks[hh], lsev, NEG), axis=1, keepdims=True)
                    dl_h = jnp.max(jnp.where(masks[hh], dlv, NEG), axis=1, keepdims=True)
                    s = lax.dot_general(qm, kc, nt, preferred_element_type=F32)
                    pr = jnp.where(valid, jnp.exp(jnp.where(valid, s, NEG) - lse_h), 0.0)
                    dpr = lax.dot_general(dob, vc, nt, preferred_element_type=F32)
                    dsb = (pr * (dpr - dl_h)).astype(BF16)
                    dqs.append(jnp.dot(dsb, kc, preferred_element_type=F32) * SCALE)
                    dkc = dkc + lax.dot_general(dsb, qm, tn, preferred_element_type=F32)
                    dvc = dvc + lax.dot_general(pr.astype(BF16), dob, tn, preferred_element_type=F32)
                _dil_store(dq_s.at[g], start, d, jnp.where(masks[0], dqs[0], dqs[1]))
                for ref, val in ((dk_s.at[g], dkc), (dv_s.at[g], dvc)):
                    _dil_store(ref, prev, d, _dil_rows(ref, prev, d) + jnp.where(has_prev, val[0:SPAN], 0.0))
                    _dil_store(ref, start, d, _dil_rows(ref, start, d) + val[SPAN:])
                return 0

            lax.fori_loop(0, 16, blk, 0)

        def fin(i, _):
            rows = pl.ds(pl.multiple_of(i * 256, 256), 256)
            c, s1, s2 = c_ref[rows, :], s1_ref[rows, :], s2_ref[rows, :]
            for g in range(3):
                base = g * 384
                dp_ref[rows, base:base + LANES] = _rope_bwd(dq_s[g, rows, :], c, s1, s2).astype(BF16)
                dp_ref[rows, base + LANES:base + 2 * LANES] = _rope_bwd(dk_s[g, rows, :], c, s1, s2).astype(BF16)
                dp_ref[rows, base + 2 * LANES:base + 3 * LANES] = dv_s[g, rows, :].astype(BF16)
            return 0

        lax.fori_loop(0, S // 256, fin, 0)

    def spec(g, t):
        return pl.BlockSpec((S, LANES), lambda p_: (0, (p_ * 3 + g) * 3 + t))

    pair = pl.BlockSpec((S, LANES), lambda p_: (0, p_))
    tab = pl.BlockSpec((S, LANES), lambda p_: (0, 0))
    return _hosted_call(
        body, comm, [p] * 9 + [dyb, yb, lse, *tabs, dp], name="dil_bwd", grid=(N_DIL_PAIRS,),
        in_specs=[spec(g, t) for g in range(3) for t in range(3)] + [pair, pair, pair, tab, tab, tab, pl.BlockSpec(memory_space=pl.ANY)],
        out_specs=pl.BlockSpec((S, 1152), lambda p_: (0, p_)),
        out_shape=_sds((S, LAY_N), BF16),
        scratch_shapes=[pltpu.VMEM((3, S, LANES), F32)] * 3 + [pltpu.VMEM((S, LANES), F32)],
        aliases={15: 0}, vmem_mb=VMEM_MB)


def in_bwd_tail(dh1, x, dx1, mod, g_mix, comm=None):
    tm = 256
    nsteps = S // tm

    def body(dh_ref, x_ref, dx1_ref, mod_ref, g_ref, dx_ref, cs_ref, s_cs):
        i = pl.program_id(0)

        @pl.when(i == 0)
        def _():
            s_cs[...] = jnp.zeros_like(s_cs)

        xv, g, dh = x_ref[...], g_ref[...], dh_ref[...]
        r = lax.rsqrt(jnp.mean(xv * xv, axis=1, keepdims=True) + EPS)
        xh = xv * r
        s_cs[0] += _colsum8(dh)
        s_cs[1] += _colsum8(dh * (xh * g))
        dn = dh * (1.0 + mod_ref[1:2, :])
        s_cs[2] += _colsum8(dn * xh)
        gd = dn * g
        dx_ref[...] = dx1_ref[...] + r * (gd - xh * jnp.mean(gd * xh, axis=1, keepdims=True))

        @pl.when(i == nsteps - 1)
        def _():
            for t in range(3):
                cs_ref[t:t + 1, :] = jnp.sum(s_cs[t], axis=0, keepdims=True)
            cs_ref[3:8, :] = jnp.zeros((5, D), F32)

    row = pl.BlockSpec((tm, D), lambda i: (i, 0))
    return _hosted_call(
        body, comm, (dh1, x, dx1, mod, g_mix), name="in_bwd_tail", grid=(nsteps,),
        in_specs=[row, row, row, pl.BlockSpec((8, D), lambda i: (0, 0)), pl.BlockSpec((1, D), lambda i: (0, 0))],
        out_specs=[row, pl.BlockSpec((8, D), lambda i: (0, 0))],
        out_shape=[_sds((S, D), F32), _sds((8, D), F32)],
        scratch_shapes=[pltpu.VMEM((3, 8, D), F32)])


def _lay_pieces():
    out = []
    qa, ka, va, fa, qb, kb, vb, ga = 0, 512, 1024, 1536, 1544, 2312, 3080, 3848
    for p in range(N_DIL_PAIRS):
        for g in range(3):
            base = LAY_B + (p * 3 + g) * 384
            hd0 = (4 * g + 2 * p) * HD
            out += [(base, qb + hd0, LANES), (base + LANES, kb + hd0, LANES), (base + 2 * LANES, vb + hd0, LANES)]
    for p in range(N_FOX_PAIRS):
        base = LAY_A + p * 384
        out += [(base, qa + p * LANES, LANES), (base + LANES, ka + p * LANES, LANES), (base + 2 * LANES, va + p * LANES, LANES)]
    out.append((LAY_F, fa, 8))
    out.append((LAY_G, ga, 2 * D))
    return out


def lay_from_nat(w_nat):
    parts, pos = [], 0
    for lay, nat, width in sorted(_lay_pieces()):
        if lay > pos:
            parts.append(jnp.zeros((w_nat.shape[0], lay - pos), w_nat.dtype))
        parts.append(w_nat[:, nat:nat + width])
        pos = lay + width
    if pos < LAY_N:
        parts.append(jnp.zeros((w_nat.shape[0], LAY_N - pos), w_nat.dtype))
    return jnp.concatenate(parts, axis=1)


def nat_from_lay(w_lay):
    parts = [w_lay[:, lay:lay + width] for lay, nat, width in sorted(_lay_pieces(), key=lambda t: t[1])]
    return jnp.concatenate(parts, axis=1)


def _shard_runs():
    runs = []
    for lay, nat, width in _lay_pieces():
        while width:
            k, loc = nat // IN_SHARD, nat % IN_SHARD
            w = min(width, IN_SHARD - loc)
            runs.append((lay, k, loc, w))
            lay, nat, width = lay + w, nat + w, width - w
    return runs


def lay_from_shards(g):
    tm = 256

    def body(g_ref, o_ref):
        o_ref[:, LAY_F:LAY_G] = jnp.zeros((tm, LAY_G - LAY_F), g.dtype)
        for lay, k, loc, w in _shard_runs():
            o_ref[:, lay:lay + w] = g_ref[k, :, loc:loc + w]

    return pl.pallas_call(
        body, name="lay_from_shards", grid=(D // tm,),
        in_specs=[pl.BlockSpec((4, tm, IN_SHARD_PAD), lambda i: (0, i, 0))],
        out_specs=pl.BlockSpec((tm, LAY_N), lambda i: (i, 0)),
        out_shape=_sds((D, LAY_N), g.dtype), compiler_params=_params(VMEM_MB),
    )(g)


def shards_from_lay(dw_lay):
    tm = 256

    def body(x_ref, o_ref):
        o_ref[:, :, IN_SHARD:] = jnp.zeros((4, tm, IN_SHARD_PAD - IN_SHARD), F32)
        for lay, k, loc, w in _shard_runs():
            o_ref[k, :, loc:loc + w] = x_ref[:, lay:lay + w]

    return pl.pallas_call(
        body, name="shards_from_lay", grid=(D // tm,),
        in_specs=[pl.BlockSpec((tm, LAY_N), lambda i: (i, 0))],
        out_specs=pl.BlockSpec((4, tm, IN_SHARD_PAD), lambda i: (0, i, 0)),
        out_shape=_sds((4, D, IN_SHARD_PAD), F32), compiler_params=_params(VMEM_MB),
    )(dw_lay)


def _pos():
    return lax.axis_index("x"), lax.axis_index("y"), lax.axis_index("c")


def _other_chips(x, y):
    return [(1 - x, y), (x, 1 - y), (1 - x, 1 - y)]


def _remote(src, dst, send_sem, recv_sem, dev):
    return pltpu.make_async_remote_copy(src_ref=src, dst_ref=dst, send_sem=send_sem, recv_sem=recv_sem,
                                        device_id=dev, device_id_type=MESH)


VMEM_SPEC = pl.BlockSpec(memory_space=pltpu.VMEM)
ANY_SPEC = pl.BlockSpec(memory_space=pl.ANY)


def gather_all(v, name, with_sum):
    r = v.shape[0]

    def body(v_ref, out_ref, *rest):
        send_s, recv_s = rest[-2:]
        x, y, c = _pos()
        me = 4 * x + 2 * y + c
        out_ref[me] = v_ref[...]
        peers = []
        for m in range(1, 8):
            px = 1 - x if m & 4 else x
            py = 1 - y if m & 2 else y
            pc = 1 - c if m & 1 else c
            peers.append((px, py, pc))
        copies = [_remote(v_ref, out_ref.at[me], send_s.at[i], recv_s.at[i], dev) for i, dev in enumerate(peers)]
        for cp in copies:
            cp.start()
        for i, (px, py, pc) in enumerate(peers):
            _remote(v_ref, out_ref.at[4 * px + 2 * py + pc], send_s.at[i], recv_s.at[i], (px, py, pc)).wait_recv()
        for cp in copies:
            cp.wait_send()
        if with_sum:
            acc = out_ref[0]
            for b in range(1, 8):
                acc = acc + out_ref[b]
            rest[0][...] = acc

    out_shape = [_sds((8, r, LANES), F32)] + ([_sds((r, LANES), F32)] if with_sum else [])
    return pl.pallas_call(
        body, name=name, in_specs=[VMEM_SPEC], out_specs=[VMEM_SPEC] * len(out_shape), out_shape=out_shape,
        scratch_shapes=[pltpu.SemaphoreType.DMA((7,)), pltpu.SemaphoreType.DMA((7,))],
    )(v)


def mod_exchange(c_all, w_ada_sh, b_sh):
    def body(c_ref, w_ref, b_ref, out_ref, sc_ref, modp, send_s, recv_s):
        cv = c_ref[...]
        sc = cv * _sigmoid(cv)
        sc_ref[...] = sc
        modp[...] = jnp.dot(sc, w_ref[...], precision=lax.Precision.HIGHEST, preferred_element_type=F32) + b_ref[...]
        x, y, c = _pos()
        k = 2 * x + y
        out_ref[k] = modp[...]
        chips = _other_chips(x, y)
        copies = [_remote(modp, out_ref.at[k], send_s.at[j], recv_s.at[j], (cx, cy, c)) for j, (cx, cy) in enumerate(chips)]
        for cp in copies:
            cp.start()
        for j, (cx, cy) in enumerate(chips):
            _remote(modp, out_ref.at[2 * cx + cy], send_s.at[j], recv_s.at[j], (cx, cy, c)).wait_recv()
        for cp in copies:
            cp.wait_send()

    n = w_ada_sh.shape[1]
    return pl.pallas_call(
        body, name="mod_exchange", in_specs=[VMEM_SPEC] * 3, out_specs=[VMEM_SPEC] * 2,
        out_shape=[_sds((4, 8, n), F32), _sds((8, D), F32)],
        scratch_shapes=[pltpu.VMEM((8, n), F32), pltpu.SemaphoreType.DMA((3,)), pltpu.SemaphoreType.DMA((3,))],
        compiler_params=_params(VMEM_MB),
    )(c_all, w_ada_sh, b_sh)


def gather_weights(shards):
    n = len(shards)

    def body(*refs):
        ins, outs = refs[:n], refs[n:2 * n]
        send_s, recv_s, fsend_s, frecv_s, loc_s = refs[2 * n:]
        x, y, c = _pos()
        k = 2 * x + y
        chips = _other_chips(x, y)
        local, sends, fwds = [], [], []
        for a in range(n):
            half = ins[a].shape[0] // 2
            rows = pl.ds(c * half, half)
            lc = pltpu.make_async_copy(ins[a], outs[a].at[k], loc_s.at[a])
            lc.start()
            local.append(lc)
            for j, (cx, cy) in enumerate(chips):
                cp = _remote(ins[a].at[rows], outs[a].at[k, rows], send_s.at[3 * a + j], recv_s.at[3 * a + j], (cx, cy, c))
                cp.start()
                sends.append(cp)
        for a in range(n):
            half = ins[a].shape[0] // 2
            rows = pl.ds(c * half, half)
            for j, (cx, cy) in enumerate(chips):
                kj = 2 * cx + cy
                _remote(ins[a].at[rows], outs[a].at[kj, rows], send_s.at[3 * a + j], recv_s.at[3 * a + j], (cx, cy, c)).wait_recv()
                fw = _remote(outs[a].at[kj, rows], outs[a].at[kj, rows], fsend_s.at[3 * a + j], frecv_s.at[3 * a + j], (x, y, 1 - c))
                fw.start()
                fwds.append(fw)
        for a in range(n):
            half = ins[a].shape[0] // 2
            orows = pl.ds((1 - c) * half, half)
            for j, (cx, cy) in enumerate(chips):
                kj = 2 * cx + cy
                _remote(outs[a].at[kj, orows], outs[a].at[kj, orows], fsend_s.at[3 * a + j], frecv_s.at[3 * a + j], (x, y, 1 - c)).wait_recv()
        for cp in sends + fwds:
            cp.wait_send()
        for lc in local:
            lc.wait()

    return pl.pallas_call(
        body, name="gather_weights", in_specs=[ANY_SPEC] * n, out_specs=[ANY_SPEC] * n,
        out_shape=[_sds((4,) + s.shape, s.dtype) for s in shards],
        scratch_shapes=[pltpu.SemaphoreType.DMA((3 * n,))] * 4 + [pltpu.SemaphoreType.DMA((n,))],
    )(*shards)


def _row_tile(rows, cap=256):
    t = cap
    while rows % t or t % 8:
        t -= 8
    return t


def _comm_wait(sends, recvs, local=()):
    for cp in recvs:
        cp.wait_recv()
    for cp in sends:
        cp.wait_send()
    for cp in local:
        cp.wait()


def ag_ici(shards):
    n = len(shards)

    def copies(ins, outs, sems):
        send_s, recv_s, loc_s = sems
        x, y, c = _pos()
        k = 2 * x + y
        sends, recvs, local = [], [], []
        for a in range(n):
            half = ins[a].shape[0] // 2
            rows = pl.ds(c * half, half)
            local.append(pltpu.make_async_copy(ins[a], outs[a].at[k], loc_s.at[a]))
            for j, (cx, cy) in enumerate(_other_chips(x, y)):
                sem = (send_s.at[3 * a + j], recv_s.at[3 * a + j], (cx, cy, c))
                sends.append(_remote(ins[a].at[rows], outs[a].at[k, rows], *sem))
                recvs.append(_remote(ins[a].at[rows], outs[a].at[2 * cx + cy, rows], *sem))
        return sends, recvs, local

    def start(ins, outs, sems):
        sends, _, local = copies(ins, outs, sems)
        for cp in local + sends:
            cp.start()

    def wait(ins, outs, sems):
        _comm_wait(*copies(ins, outs, sems))

    return Comm(shards, [_sds((4,) + s.shape, s.dtype) for s in shards], [3 * n, 3 * n, n], start, wait)


def ag_d2d(bufs):
    n = len(bufs)

    def copies(ins, outs, sems):
        send_s, recv_s = sems
        x, y, c = _pos()
        sends, recvs = [], []
        for a in range(n):
            half = outs[a].shape[1] // 2
            rows, orows = pl.ds(c * half, half), pl.ds((1 - c) * half, half)
            for j, (cx, cy) in enumerate(_other_chips(x, y)):
                kj = 2 * cx + cy
                sem = (send_s.at[3 * a + j], recv_s.at[3 * a + j], (x, y, 1 - c))
                sends.append(_remote(outs[a].at[kj, rows], outs[a].at[kj, rows], *sem))
                recvs.append(_remote(outs[a].at[kj, orows], outs[a].at[kj, orows], *sem))
        return sends, recvs

    def start(ins, outs, sems):
        for cp in copies(ins, outs, sems)[0]:
            cp.start()

    def wait(ins, outs, sems):
        _comm_wait(*copies(ins, outs, sems))

    return Comm(bufs, [_sds(b.shape, b.dtype) for b in bufs], [3 * n, 3 * n], start, wait, aliases={a: a for a in range(n)})


def rs_a(grads):
    n = len(grads)

    def copies(ins, outs, sems):
        send_s, recv_s = sems
        x, y, c = _pos()
        cps = []
        for a in range(n):
            half = ins[a].shape[1] // 2
            cps.append(_remote(ins[a].at[:, pl.ds((1 - c) * half, half), :], outs[a], send_s.at[a], recv_s.at[a], (x, y, 1 - c)))
        return cps

    def start(ins, outs, sems):
        for cp in copies(ins, outs, sems):
            cp.start()

    def wait(ins, outs, sems):
        cps = copies(ins, outs, sems)
        _comm_wait(cps, cps)

    return Comm(grads, [_sds((4, g.shape[1] // 2, g.shape[2]), g.dtype) for g in grads], [n, n], start, wait)


def rs_b(pres):
    n = len(pres)

    def copies(ins, outs, sems):
        send_s, recv_s, loc_s = sems
        x, y, c = _pos()
        k = 2 * x + y
        cps, local = [], []
        for a in range(n):
            local.append(pltpu.make_async_copy(ins[a].at[k], outs[a].at[3], loc_s.at[a]))
            for j, (cx, cy) in enumerate(_other_chips(x, y)):
                cps.append(_remote(ins[a].at[2 * cx + cy], outs[a].at[j], send_s.at[3 * a + j], recv_s.at[3 * a + j], (cx, cy, c)))
        return cps, local

    def start(ins, outs, sems):
        cps, local = copies(ins, outs, sems)
        for cp in local + cps:
            cp.start()

    def wait(ins, outs, sems):
        cps, local = copies(ins, outs, sems)
        _comm_wait(cps, cps, local)

    return Comm(pres, [_sds(p_.shape, p_.dtype) for p_ in pres], [3 * n, 3 * n, n], start, wait)


def rs_c(reds):
    n = len(reds)

    def copies(ins, outs, sems):
        send_s, recv_s, loc_s = sems
        x, y, c = _pos()
        sends, recvs, local = [], [], []
        for a in range(n):
            half = ins[a].shape[0]
            rows, orows = pl.ds(c * half, half), pl.ds((1 - c) * half, half)
            local.append(pltpu.make_async_copy(ins[a], outs[a].at[rows], loc_s.at[a]))
            sem = (send_s.at[a], recv_s.at[a], (x, y, 1 - c))
            sends.append(_remote(ins[a], outs[a].at[rows], *sem))
            recvs.append(_remote(ins[a], outs[a].at[orows], *sem))
        return sends, recvs, local

    def start(ins, outs, sems):
        sends, _, local = copies(ins, outs, sems)
        for cp in local + sends:
            cp.start()

    def wait(ins, outs, sems):
        _comm_wait(*copies(ins, outs, sems))

    return Comm(reds, [_sds((2 * r_.shape[0], r_.shape[1]), r_.dtype) for r_ in reds], [n, n, n], start, wait)


def comm_join(*comms):
    ni = np.cumsum([0] + [len(c.ins) for c in comms])
    no = np.cumsum([0] + [len(c.out_shapes) for c in comms])
    ns = np.cumsum([0] + [len(c.sems) for c in comms])

    def parts(ins, outs, sems):
        return [(c, ins[ni[i]:ni[i + 1]], outs[no[i]:no[i + 1]], sems[ns[i]:ns[i + 1]]) for i, c in enumerate(comms)]

    def start(ins, outs, sems):
        for c, a, b, s in parts(ins, outs, sems):
            c.start(a, b, s)

    def wait(ins, outs, sems):
        for c, a, b, s in parts(ins, outs, sems):
            c.wait(a, b, s)

    aliases = {int(ni[i]) + k: int(no[i]) + v for i, c in enumerate(comms) for k, v in c.aliases.items()}
    return Comm(sum((c.ins for c in comms), []), sum((c.out_shapes for c in comms), []), sum((c.sems for c in comms), []),
                start, wait, aliases)


def comm_only(comm, name):
    nci, nco = len(comm.ins), len(comm.out_shapes)

    def body(*refs):
        ins, outs, sems = refs[:nci], refs[nci:nci + nco], refs[nci + nco:]
        comm.start(ins, outs, sems)
        comm.wait(ins, outs, sems)

    return pl.pallas_call(
        body, name=name, in_specs=[ANY_SPEC] * nci, out_specs=[ANY_SPEC] * nco, out_shape=comm.out_shapes,
        scratch_shapes=[pltpu.SemaphoreType.DMA((s,)) for s in comm.sems],
        input_output_aliases=comm.aliases,
    )(*comm.ins)


def rs_add_halves(g, other, core, name):
    _, r, cdim = g.shape
    half = r // 2
    tr = _row_tile(half, 128)
    nb = half // tr

    def body(core_ref, g_ref, o_ref, out_ref):
        del core_ref
        out_ref[...] = (g_ref[...] + o_ref[...]).astype(BF16)

    grid_spec = pltpu.PrefetchScalarGridSpec(
        num_scalar_prefetch=1, grid=(4, nb),
        in_specs=[pl.BlockSpec((None, tr, cdim), lambda k, i, cr: (k, cr[0] * nb + i, 0)),
                  pl.BlockSpec((None, tr, cdim), lambda k, i, cr: (k, i, 0))],
        out_specs=pl.BlockSpec((None, tr, cdim), lambda k, i, cr: (k, i, 0)))
    return pl.pallas_call(body, name=name, grid_spec=grid_spec, out_shape=_sds((4, half, cdim), BF16))(core, g, other)


def rs_add_slabs(t, name):
    _, half, cdim = t.shape
    tr = _row_tile(half, 128)

    def body(t_ref, out_ref):
        s = [t_ref[i].astype(F32) for i in range(4)]
        out_ref[...] = ((s[3] + s[0]) + s[1]) + s[2]

    return pl.pallas_call(
        body, name=name, grid=(half // tr,),
        in_specs=[pl.BlockSpec((4, tr, cdim), lambda i: (0, i, 0))],
        out_specs=pl.BlockSpec((tr, cdim), lambda i: (i, 0)),
        out_shape=_sds((half, cdim), F32),
    )(t)


def _adam_math(w, g, m, v):
    m = ADAM_B1 * m + (1.0 - ADAM_B1) * g
    v = ADAM_B2 * v + (1.0 - ADAM_B2) * (g * g)
    m_hat = m / (1.0 - ADAM_B1 ** ADAM_STEP)
    v_hat = v / (1.0 - ADAM_B2 ** ADAM_STEP)
    delta = -ADAM_LR * (m_hat / (jnp.sqrt(v_hat) + ADAM_EPS) + ADAM_WD * w)
    return delta, m, v


def adam(w, g, m, v, name):
    r, cdim = w.shape
    tr = _row_tile(r) if r >= 8 else r

    def body(w_ref, g_ref, m_ref, v_ref, g_out, d_ref, nm_ref, nv_ref):
        gv = g_ref[:, :cdim]
        g_out[...] = gv
        d_ref[...], nm_ref[...], nv_ref[...] = _adam_math(w_ref[...], gv, m_ref[...], v_ref[...])

    blk = pl.BlockSpec((tr, cdim), lambda i: (i, 0))
    return pl.pallas_call(
        body, name=name, grid=(r // tr,), in_specs=[blk, pl.BlockSpec((tr, g.shape[1]), lambda i: (i, 0)), blk, blk],
        out_specs=[blk] * 4, out_shape=[_sds((r, cdim), F32)] * 4,
    )(w, g, m, v)


def adam_w_ada(sc_t, dmod_sh, w, m, v):
    r, cdim = w.shape
    tr = 256

    def body(s_ref, d_ref, w_ref, m_ref, v_ref, g_ref, dl_ref, nm_ref, nv_ref):
        g = jnp.dot(s_ref[...], d_ref[...], precision=lax.Precision.HIGHEST, preferred_element_type=F32)
        g_ref[...] = g
        dl_ref[...], nm_ref[...], nv_ref[...] = _adam_math(w_ref[...], g, m_ref[...], v_ref[...])

    blk = pl.BlockSpec((tr, cdim), lambda i: (i, 0))
    return pl.pallas_call(
        body, name="adam_w_ada", grid=(r // tr,),
        in_specs=[pl.BlockSpec((tr, LANES), lambda i: (i, 0)), pl.BlockSpec((LANES, cdim), lambda i: (0, 0)), blk, blk, blk],
        out_specs=[blk] * 4, out_shape=[_sds((r, cdim), F32)] * 4,
    )(sc_t, dmod_sh, w, m, v)


SMALL_ROWS = 80


def kernel(x, c, w_ada, b_ada, g_mix, w_in, b_fgate, w_br_a, w_br_b, w_out, g_ffn, w_ffn_gate, w_ffn_up, w_ffn_down, g_final, loss_target, m_w_ada, m_b_ada, m_g_mix, m_w_in, m_b_fgate, m_w_br_a, m_w_br_b, m_w_out, m_g_ffn, m_w_ffn_gate, m_w_ffn_up, m_w_ffn_down, m_g_final, v_w_ada, v_b_ada, v_g_mix, v_w_in, v_b_fgate, v_w_br_a, v_w_br_b, v_w_out, v_g_ffn, v_w_ffn_gate, v_w_ffn_up, v_w_ffn_down, v_g_final):
    xi, yi, ci = _pos()
    chip = 2 * xi + yi
    seq = 4 * xi + 2 * yi + ci
    n_ada = w_ada.shape[2]

    c_all = gather_all(c.reshape(8, LANES), "gather_c", False)[0].reshape(8, D)
    b_sh = lax.dynamic_slice(b_ada, (0, chip * n_ada), (1, n_ada))
    mod_all, sc = mod_exchange(c_all, w_ada[0], b_sh)
    mod = lax.dynamic_index_in_dim(mod_all, seq, axis=1, keepdims=False).reshape(6, D)
    mod8 = jnp.pad(mod, ((0, 2), (0, 0)))

    shards = [
        jnp.pad(w_in[0], ((0, 0), (0, IN_SHARD_PAD - IN_SHARD))), w_br_a[0], w_br_b[0], w_out[0],
        jnp.pad(w_ffn_gate[0], ((0, 0), (0, FF_PAD - FF_SHARD))), jnp.pad(w_ffn_up[0], ((0, 0), (0, FF_PAD - FF_SHARD))),
        jnp.pad(w_ffn_down[0], ((0, FF_PAD - FF_SHARD), (0, 0))),
    ]
    shards = [s.astype(BF16) for s in shards]
    core = ci.astype(jnp.int32).reshape(1)
    xs, tgt, g_fin = x[0], loss_target[0], g_final.reshape(1, D)

    s_in, s_bra, s_brb, s_out, s_gate, s_up, s_down = shards

    def halves(gs, others, tag):
        return [rs_add_halves(g, o, core, f"rs_{tag}_halves_{i}") for i, (g, o) in enumerate(zip(gs, others))]

    def slab_sums(ts, tag):
        return [rs_add_slabs(t, f"rs_{tag}_slabs_{i}") for i, t in enumerate(ts)]

    g_in = gather_weights([s_in])[0]
    w_lay = lay_from_shards(g_in)
    tabs = rope_tables()
    h1 = norm_mod_fwd(xs, g_mix, mod8, 0, 1)
    p, mix_w = in_proj_fwd(h1, w_lay, tabs, comm=ag_ici([s_bra, s_brb, s_out]))
    frow, fraw, fcol = fgate_fwd(p, jnp.pad(b_fgate, ((0, 0), (0, LANES - 8))))
    (ya_att, gcol), res = fox_fwd(p, fcol, comm=comm_join(ag_d2d(mix_w), ag_ici([s_gate, s_up])))
    g_bra, g_brb, g_out = res[:3]
    (yb, lse_b), res = dil_fwd(p, comm=comm_join(ag_d2d(res[3:]), ag_ici([s_down])))
    w_gate, w_up = res[:2]
    w_bra = g_bra.transpose(1, 0, 2).reshape(512, D)
    w_brb = g_brb.transpose(1, 0, 2).reshape(256, D)
    w_o = g_out.reshape(D, D)
    (merged, ya, ybp), (g_down,) = merge_fwd(ya_att, yb, p, w_bra, w_brb, comm=ag_d2d(res[2:]))
    w_down = g_down.reshape(FFP, D)
    mix, x1, h2 = out_proj_fwd(merged, w_o, xs, mod8, g_ffn)
    a, u, z = ffn_up_fwd(h2, w_gate, w_up)
    dx2, dffn, dg_final, dga_f, loss_part = ffn_down_loss(z, w_down, x1, mod8, g_fin, tgt)

    da, du, dw_down = ffn_down_bwd(dffn, w_down, a, u, z)
    g_down = [dw_down.reshape(4, FF_PAD, D)]
    dh2a, oth = mm_nt(da, w_gate, "ffn_gate_dx", comm=rs_a(g_down))
    pre_down = halves(g_down, oth, "down")
    dh2b, _ = mm_nt(du, w_up, "ffn_up_dx")
    dw_gate, _ = mm_tn(h2, da, "ffn_gate_dw", shard_major=True)
    dw_up, _ = mm_tn(h2, du, "ffn_up_dw", shard_major=True)
    g_gu = [dw_gate, dw_up]
    (dx1, dp1, dya_att, dyb, cs_mid, dw_out, dw_bra, dw_brb), res = mid_bwd(
        dh2a, dh2b, x1, dx2, mix, mod8, g_ffn, p, ya, ybp, merged, ya_att, yb, w_o, w_bra, w_brb,
        comm=comm_join(rs_b(pre_down), rs_a(g_gu)))
    red_down = slab_sums(res[:1], "down")
    pre_gu = halves(g_gu, res[1:], "gu")
    g_mix3 = [dw_bra.reshape(512, 4, 256).transpose(1, 0, 2), dw_brb.reshape(256, 4, 256).transpose(1, 0, 2), dw_out.reshape(4, 256, D)]
    (dp2, dfrow, dfcol), res = fox_bwd(p, dya_att, ya_att, gcol, fcol, dp1,
                                       comm=comm_join(rs_b(pre_gu), rs_c(red_down), rs_a(g_mix3)))
    red_gu = slab_sums(res[:2], "gu")
    r_down = res[2]
    pre_mix3 = halves(g_mix3, res[3:], "mix")
    dp3, db_fg = fgate_bwd(dfrow.reshape(8, S), dfcol, fraw, dp2)
    dp4, res = dil_bwd(p, dyb, yb, lse_b, tabs, dp3, comm=comm_join(rs_c(red_gu), rs_b(pre_mix3)))
    r_gate, r_up = res[:2]
    red_mix3 = slab_sums(res[2:], "mix")
    dw_lay, (r_bra, r_brb, r_out) = mm_tn(h1, dp4, "in_proj_dw", comm=rs_c(red_mix3))
    g_in4 = [shards_from_lay(dw_lay)]
    dh1, oth = mm_nt(dp4, w_lay, "in_proj_dx", comm=rs_a(g_in4))
    pre_in = halves(g_in4, oth, "in")
    (dx, cs_in), res = in_bwd_tail(dh1, xs, dx1, mod8, g_mix, comm=rs_b(pre_in))
    (r_in,) = comm_only(rs_c(slab_sums(res, "in")), "rs_in_share")
    gpad = dict(w_in=r_in, w_br_a=r_bra, w_br_b=r_brb, w_out=r_out, w_ffn_gate=r_gate, w_ffn_up=r_up, w_ffn_down=r_down)

    dmod = jnp.concatenate([cs_in[0:2], cs_mid[3:4], cs_mid[0:2], dga_f], axis=0)
    small = dict(dmod=dmod, dg_mix=cs_in[2:3], dg_ffn=cs_mid[2:3], dg_final=dg_final, db_fgate=db_fg[:, 0], loss=loss_part[0, 0])
    sv = jnp.concatenate([
        small["dmod"].reshape(48, LANES), small["dg_mix"].reshape(8, LANES), small["dg_ffn"].reshape(8, LANES),
        small["dg_final"].reshape(8, LANES), jnp.pad(small["db_fgate"], (0, LANES - 8)).reshape(1, LANES),
        jnp.broadcast_to(small["loss"], (1, LANES)), jnp.zeros((SMALL_ROWS - 74, LANES), F32)], axis=0)
    sv_all, sv_sum = gather_all(sv, "gather_small", True)
    loss = sv_sum[73, 0]
    g_small = dict(b_ada=sv_sum[0:48].reshape(1, 6 * D), g_mix=sv_sum[48:56].reshape(1, D), g_ffn=sv_sum[56:64].reshape(1, D),
                   g_final=sv_sum[64:72].reshape(D), b_fgate=sv_sum[72, 0:8].reshape(1, 8))

    dmod_all = lax.dynamic_slice(sv_all[:, 0:48, :].reshape(8, 6 * D), (0, chip * n_ada), (8, n_ada))
    g_ada, d_ada, nm_ada, nv_ada = adam_w_ada(jnp.pad(sc.T, ((0, 0), (0, LANES - 8))), jnp.pad(dmod_all, ((0, LANES - 8), (0, 0))),
                                              w_ada[0], m_w_ada[0], v_w_ada[0])

    big = dict(w_in=(w_in, m_w_in, v_w_in), w_br_a=(w_br_a, m_w_br_a, v_w_br_a), w_br_b=(w_br_b, m_w_br_b, v_w_br_b),
               w_out=(w_out, m_w_out, v_w_out), w_ffn_gate=(w_ffn_gate, m_w_ffn_gate, v_w_ffn_gate),
               w_ffn_up=(w_ffn_up, m_w_ffn_up, v_w_ffn_up), w_ffn_down=(w_ffn_down, m_w_ffn_down, v_w_ffn_down))
    upd = {nm: adam(w[0], gpad[nm], m[0], v[0], "adam_" + nm) for nm, (w, m, v) in big.items()}

    def pack(gm, gf, gl, ba, bf):
        rows = [gm.reshape(1, D), gf.reshape(1, D), gl.reshape(1, D), ba.reshape(6, D), jnp.pad(bf.reshape(1, 8), ((0, 0), (0, D - 8)))]
        return jnp.concatenate(rows + [jnp.zeros((6, D), F32)], axis=0)

    packed = adam(pack(g_mix, g_ffn, g_final, b_ada, b_fgate),
                  pack(g_small["g_mix"], g_small["g_ffn"], g_small["g_final"], g_small["b_ada"], g_small["b_fgate"]),
                  pack(m_g_mix, m_g_ffn, m_g_final, m_b_ada, m_b_fgate), pack(v_g_mix, v_g_ffn, v_g_final, v_b_ada, v_b_fgate),
                  "adam_small")

    def unpack(t):
        return dict(g_mix=t[0:1], g_ffn=t[1:2], g_final=t[2], b_ada=t[3:9].reshape(1, 6 * D), b_fgate=t[9:10, 0:8])

    small_upd = [unpack(t) for t in packed[1:]]
    order =["w_ada", "b_ada", "g_mix", "w_in", "b_fgate", "w_br_a", "w_br_b", "w_out", "g_ffn", "w_ffn_gate", "w_ffn_up", "w_ffn_down", "g_final"]

    def leaf(nm, which):
        if nm == "w_ada":
            return (g_ada, d_ada, nm_ada, nv_ada)[which][None]
        if nm in big:
            return upd[nm][which][None]
        return g_small[nm] if which == 0 else small_upd[which - 1][nm]

    outs = [loss, dx[None]]
    for which in range(4):
        outs += [leaf(nm, which) for nm in order]
    return tuple(outs)
```

```python
import functools

import numpy as np
import jax
import jax.numpy as jnp
from jax import lax
from jax.experimental import pallas as pl
from jax.experimental.pallas import tpu as pltpu

F32, BF16 = jnp.float32, jnp.bfloat16
S, D = 2048, 1024
HD = 64
LANES = 128
N_FOX_PAIRS, N_DIL_PAIRS = 4, 2
DIL_GROUPS = ((1, 16), (4, 4), (16, 1))
SPAN = 128
ROT_DIM, ROPE_THETA = 16, 500000.0
D_FF, FF_SHARD, FF_PAD = 2816, 704, 768
FFP = 4 * FF_PAD
IN_COLS, IN_SHARD, IN_SHARD_PAD = 5896, 1474, 1536
LAY_B, LAY_A, LAY_F, LAY_G, LAY_N = 0, 2304, 3840, 4096, 6144
EPS, NEG = 1e-6, -1e30
SCALE = HD ** -0.5
ADAM_LR, ADAM_B1, ADAM_B2, ADAM_EPS, ADAM_WD, ADAM_STEP = 0.001, 0.9, 0.999, 1e-08, 0.01, 10
VMEM_MB = 56
MESH = pl.DeviceIdType.MESH


def _params(vmem_mb=None, **kw):
    if vmem_mb is not None:
        kw["vmem_limit_bytes"] = vmem_mb * 1024 * 1024
    return pltpu.CompilerParams(**kw)


def _sds(shape, dtype):
    return jax.ShapeDtypeStruct(shape, dtype)


def _sigmoid(x):
    return 1.0 / (1.0 + jnp.exp(-x))


def _colsum8(x):
    tm, n = x.shape
    return jnp.sum(x.reshape(tm // 8, 8, n), axis=0)


class Comm:
    def __init__(self, ins, out_shapes, sems, start, wait, aliases=None):
        self.ins, self.out_shapes, self.sems = list(ins), list(out_shapes), list(sems)
        self.start, self.wait, self.aliases = start, wait, dict(aliases or {})


def _hosted_call(body, comm, args, *, name, grid, in_specs, out_specs, out_shape, scratch_shapes=(), aliases=None, vmem_mb=None):
    single = not isinstance(out_shape, (list, tuple))
    out_specs_l = [out_specs] if single else list(out_specs)
    out_shape_l = [out_shape] if single else list(out_shape)
    n_in, n_out, n_scr = len(in_specs), len(out_shape_l), len(scratch_shapes)
    aliases = dict(aliases or {})
    if comm is None:
        res = pl.pallas_call(body, name=name, grid=grid, in_specs=list(in_specs), out_specs=out_specs, out_shape=out_shape,
                             scratch_shapes=list(scratch_shapes), input_output_aliases=aliases,
                             compiler_params=_params(vmem_mb))(*args)
        return res, []
    nci, nco = len(comm.ins), len(comm.out_shapes)

    def wrapped(*refs):
        main_in, cin = refs[:n_in], refs[n_in:n_in + nci]
        o0 = n_in + nci
        main_out, cout = refs[o0:o0 + n_out], refs[o0 + n_out:o0 + n_out + nco]
        s0 = o0 + n_out + nco
        scr, sems = refs[s0:s0 + n_scr], refs[s0 + n_scr:]
        ids = [pl.program_id(i) for i in range(len(grid))]
        first = functools.reduce(jnp.logical_and, [i == 0 for i in ids])
        last = functools.reduce(jnp.logical_and, [i == g - 1 for i, g in zip(ids, grid)])

        @pl.when(first)
        def _():
            comm.start(cin, cout, sems)

        body(*main_in, *main_out, *scr)

        @pl.when(last)
        def _():
            comm.wait(cin, cout, sems)

    for ci, co in comm.aliases.items():
        aliases[n_in + ci] = n_out + co
    any_spec = pl.BlockSpec(memory_space=pl.ANY)
    res = pl.pallas_call(
        wrapped, name=name, grid=grid, in_specs=list(in_specs) + [any_spec] * nci, out_specs=out_specs_l + [any_spec] * nco,
        out_shape=out_shape_l + comm.out_shapes,
        scratch_shapes=list(scratch_shapes) + [pltpu.SemaphoreType.DMA((s,)) for s in comm.sems],
        input_output_aliases=aliases, compiler_params=_params(vmem_mb))(*args, *comm.ins)
    main = list(res[:n_out])
    return (main[0] if single else main), list(res[n_out:])


def norm_mod_fwd(x, g, mod, shift_row, scale_row):
    tm = 256

    def body(x_ref, g_ref, mod_ref, h_ref):
        xv = x_ref[...]
        r = lax.rsqrt(jnp.mean(xv * xv, axis=1, keepdims=True) + EPS)
        n = xv * r * g_ref[...]
        h = n * (1.0 + mod_ref[scale_row:scale_row + 1, :]) + mod_ref[shift_row:shift_row + 1, :]
        h_ref[...] = h.astype(BF16)

    return pl.pallas_call(
        body, name="norm_mod_fwd", grid=(S // tm,),
        in_specs=[pl.BlockSpec((tm, D), lambda i: (i, 0)), pl.BlockSpec((1, D), lambda i: (0, 0)),
                  pl.BlockSpec((8, D), lambda i: (0, 0))],
        out_specs=pl.BlockSpec((tm, D), lambda i: (i, 0)),
        out_shape=_sds((S, D), BF16),
    )(x, g, mod)


def rope_tables():
    pos = jnp.arange(S, dtype=F32)
    inv_freq = ROPE_THETA ** (-jnp.arange(0, ROT_DIM, 2, dtype=F32) / ROT_DIM)
    ang = pos[:, None] * inv_freq[None, :]
    cos, sin = jnp.cos(ang), jnp.sin(ang)
    one, zero = jnp.ones((S, HD - ROT_DIM), F32), jnp.zeros((S, HD - ROT_DIM), F32)
    z8 = jnp.zeros((S, 8), F32)
    c = jnp.concatenate([cos, cos, one], axis=1)
    s1 = jnp.concatenate([-sin, z8, zero], axis=1)
    s2 = jnp.concatenate([z8, sin, zero], axis=1)
    return tuple(jnp.concatenate([t, t], axis=1) for t in (c, s1, s2))


def _rope(y, c, s1, s2):
    return y * c + pltpu.roll(y, LANES - 8, 1) * s1 + pltpu.roll(y, 8, 1) * s2


def _rope_bwd(dy, c, s1, s2):
    return dy * c + pltpu.roll(dy * s1, 8, 1) + pltpu.roll(dy * s2, LANES - 8, 1)


def in_proj_fwd(h, w_lay, tabs, comm=None):
    tm, tn = 2048, 384
    n_rope = 2 * N_DIL_PAIRS * 3 // 2

    def body(a_ref, w_ref, c_ref, s1_ref, s2_ref, o_ref):
        j = pl.program_id(0)
        y = jnp.dot(a_ref[...], w_ref[...], preferred_element_type=F32)

        @pl.when(j < n_rope)
        def _():
            c, s1, s2 = c_ref[...], s1_ref[...], s2_ref[...]
            for t in range(2):
                o_ref[:, LANES * t:LANES * (t + 1)] = _rope(y[:, LANES * t:LANES * (t + 1)], c, s1, s2)
            o_ref[:, 2 * LANES:] = y[:, 2 * LANES:]

        @pl.when(j >= n_rope)
        def _():
            o_ref[...] = y

    tab = pl.BlockSpec((tm, LANES), lambda j, i: (i, 0))
    return _hosted_call(
        body, comm, (h, w_lay, *tabs), name="in_proj_fwd", grid=(LAY_N // tn, S // tm),
        in_specs=[pl.BlockSpec((tm, D), lambda j, i: (i, 0)), pl.BlockSpec((D, tn), lambda j, i: (0, j)), tab, tab, tab],
        out_specs=pl.BlockSpec((tm, tn), lambda j, i: (i, j)),
        out_shape=_sds((S, LAY_N), F32), vmem_mb=VMEM_MB)


def _log1p_small(t):
    return jnp.where(t < 1e-2, t * (1.0 - t * (0.5 - t * (1.0 / 3.0))), jnp.log(1.0 + t))


def fgate_fwd(p, b_pad):
    def body(fa_ref, b_ref, frow_ref, fraw_ref, fcol_ref):
        f = fa_ref[...] + b_ref[...]
        fr = f.T[0:8, :]
        ls = jnp.minimum(fr, 0.0) - _log1p_small(jnp.exp(-jnp.abs(fr)))
        lane = lax.broadcasted_iota(jnp.int32, (8, S), 1)
        acc, sh = ls, 1
        while sh < S:
            acc = acc + jnp.where(lane >= sh, pltpu.roll(acc, sh, 1), 0.0)
            sh *= 2
        frow_ref[...] = acc
        fraw_ref[...] = fr
        for hh in range(8):
            fcol_ref[hh] = jnp.broadcast_to(acc[hh:hh + 1, :], (LANES, S)).T

    return pl.pallas_call(
        body, name="fgate_fwd", grid=(1,),
        in_specs=[pl.BlockSpec((S, LANES), lambda i: (0, LAY_F // LANES)), pl.BlockSpec((1, LANES), lambda i: (0, 0))],
        out_specs=[pl.BlockSpec((8, S), lambda i: (0, 0)), pl.BlockSpec((8, S), lambda i: (0, 0)),
                   pl.BlockSpec((8, S, LANES), lambda i: (0, 0, 0))],
        out_shape=[_sds((8, S), F32), _sds((8, S), F32), _sds((8, S, LANES), F32)],
        compiler_params=_params(VMEM_MB),
    )(p, b_pad)


def _head_masks(rows):
    lane = lax.broadcasted_iota(jnp.int32, (rows, LANES), 1)
    return lane < HD, lane >= HD


FT = 256


def _split3(f):
    hi = f.astype(BF16).astype(F32)
    r = f - hi
    mid = r.astype(BF16).astype(F32)
    return hi, mid, r - mid


def _fox_operands(qkv_ref, tcol_ref, scol_ref, qa_s, ka_s):
    rows = 256
    lane = lax.broadcasted_iota(jnp.int32, (rows, LANES), 1)

    def chunk(i, _):
        r = pl.ds(pl.multiple_of(i * rows, rows), rows)
        q, k = qkv_ref[r, 0:LANES], qkv_ref[r, LANES:2 * LANES]
        s0, s1 = _split3(scol_ref[0, r, :]), _split3(scol_ref[1, r, :])
        ka = jnp.where(lane == 0, -s0[0], jnp.where(lane == 1, -s0[1], jnp.where(lane == 2, -s0[2], jnp.where(
            lane == 3, -s1[0], jnp.where(lane == 4, -s1[1], jnp.where(lane == 5, -s1[2], jnp.where(lane < 9, 1.0, 0.0)))))))
        ka_s[r, 0:LANES] = k.astype(BF16)
        ka_s[r, LANES:2 * LANES] = ka.astype(BF16)
        for hh in range(2):
            own = (lane < HD) if hh == 0 else (lane >= HD)
            t3 = _split3(tcol_ref[hh, r, :])
            ones = (lane >= 3 * hh) & (lane < 3 * hh + 3)
            qa = jnp.where(ones, 1.0, jnp.where(lane == 6, t3[0], jnp.where(lane == 7, t3[1], jnp.where(lane == 8, t3[2], 0.0))))
            qa_s[hh, r, 0:LANES] = jnp.where(own, q * SCALE, 0.0).astype(BF16)
            qa_s[hh, r, LANES:2 * LANES] = qa.astype(BF16)
        return 0

    lax.fori_loop(0, S // rows, chunk, 0)


def fox_fwd(p, fcol, comm=None):
    nt = (((1,), (1,)), ((), ()))

    def body(qkv_ref, fc_ref, o_ref, g_ref, qa_s, ka_s):
        _fox_operands(qkv_ref, fc_ref, fc_ref, qa_s, ka_s)
        masks = _head_masks(FT)
        causal = lax.broadcasted_iota(jnp.int32, (FT, FT), 1) <= lax.broadcasted_iota(jnp.int32, (FT, FT), 0)
        causal2 = jnp.concatenate([causal, causal], axis=0)

        def qloop(qi, _):
            q0 = pl.multiple_of(qi * FT, FT)
            qa = jnp.concatenate([qa_s[0, pl.ds(q0, FT), :], qa_s[1, pl.ds(q0, FT), :]], axis=0)

            def step(kb, carry, diagonal):
                m, l, acc = carry
                k0 = pl.multiple_of(kb * FT, FT)
                v = qkv_ref[pl.ds(k0, FT), 2 * LANES:3 * LANES].astype(BF16)
                s = lax.dot_general(qa, ka_s[pl.ds(k0, FT), :], nt, preferred_element_type=F32)
                if diagonal:
                    s = jnp.where(causal2, s, NEG)
                m_new = jnp.maximum(m, jnp.max(s, axis=1, keepdims=True))
                pr = jnp.exp(s - m_new)
                alpha = jnp.exp(m - m_new)
                return (m_new, l * alpha + jnp.sum(pr, axis=1, keepdims=True),
                        acc * alpha + jnp.dot(pr.astype(BF16), v, preferred_element_type=F32))

            init = (jnp.full((2 * FT, 1), NEG, F32), jnp.zeros((2 * FT, 1), F32), jnp.zeros((2 * FT, LANES), F32))
            carry = lax.fori_loop(0, qi, lambda kb, cr: step(kb, cr, False), init)
            m, l, acc = step(qi, carry, True)
            out = acc / l
            lse = m + jnp.log(l)
            o_ref[pl.ds(q0, FT), :] = jnp.where(masks[0], out[:FT], out[FT:]).astype(BF16)
            g_ref[0, pl.ds(q0, FT), :] = fc_ref[0, pl.ds(q0, FT), :] - lse[:FT]
            g_ref[1, pl.ds(q0, FT), :] = fc_ref[1, pl.ds(q0, FT), :] - lse[FT:]
            return 0

        lax.fori_loop(0, S // FT, qloop, 0)

    a_blk = LAY_A // 384
    return _hosted_call(
        body, comm, (p, fcol), name="fox_fwd", grid=(N_FOX_PAIRS,),
        in_specs=[pl.BlockSpec((S, 384), lambda p_: (0, a_blk + p_)), pl.BlockSpec((2, S, LANES), lambda p_: (p_, 0, 0))],
        out_specs=[pl.BlockSpec((S, LANES), lambda p_: (0, p_)), pl.BlockSpec((2, S, LANES), lambda p_: (p_, 0, 0))],
        out_shape=[_sds((S, 4 * LANES), BF16), _sds((8, S, LANES), F32)],
        scratch_shapes=[pltpu.VMEM((2, S, 2 * LANES), BF16), pltpu.VMEM((S, 2 * LANES), BF16)],
        vmem_mb=VMEM_MB)


def _dil_rows(ref, start, d):
    return ref[pl.ds(start, SPAN), :] if d == 1 else ref[pl.ds(start, SPAN, stride=d), :]


def _dil_store(ref, start, d, val):
    if d == 1:
        ref[pl.ds(start, SPAN), :] = val
    else:
        ref[pl.ds(start, SPAN, stride=d), :] = val


def _band_mask(has_prev):
    qi = lax.broadcasted_iota(jnp.int32, (SPAN, 2 * SPAN), 0) + SPAN
    kj = lax.broadcasted_iota(jnp.int32, (SPAN, 2 * SPAN), 1)
    dist = qi - kj
    return (dist >= 0) & (dist <= SPAN) & (has_prev | (kj >= SPAN))


def _dil_block(n, d, nb):
    r, j = n // nb, n % nb
    start = r + d * SPAN * j
    prev = jnp.maximum(start - d * SPAN, r)
    return start, prev, j > 0


def dil_fwd(p, comm=None):
    def body(*refs):
        qkv = [refs[3 * g:3 * g + 3] for g in range(3)]
        y_ref, lse_ref = refs[9], refs[10]
        acc_s, m_s, l_s = refs[11], refs[12], refs[13]
        masks = _head_masks(SPAN)
        for g, (d, nb) in enumerate(DIL_GROUPS):
            q_ref, k_ref, v_ref = qkv[g]

            def blk(n, _):
                start, prev, has_prev = _dil_block(n, d, nb)
                q = _dil_rows(q_ref, start, d)
                kc = jnp.concatenate([_dil_rows(k_ref, prev, d), _dil_rows(k_ref, start, d)], axis=0).astype(BF16)
                vc = jnp.concatenate([_dil_rows(v_ref, prev, d), _dil_rows(v_ref, start, d)], axis=0).astype(BF16)
                valid = _band_mask(has_prev)
                valid2 = jnp.concatenate([valid, valid], axis=0)
                q2 = (jnp.concatenate([jnp.where(masks[0], q, 0.0), jnp.where(masks[1], q, 0.0)], axis=0) * SCALE).astype(BF16)
                s = jnp.where(valid2, lax.dot_general(q2, kc, (((1,), (1,)), ((), ())), preferred_element_type=F32), NEG)
                m = jnp.max(s, axis=1, keepdims=True)
                pr = jnp.exp(s - m)
                l = jnp.sum(pr, axis=1, keepdims=True)
                acc = jnp.dot(pr.astype(BF16), vc, preferred_element_type=F32)
                _dil_store(acc_s.at[g], start, d, jnp.where(masks[0], acc[:SPAN], acc[SPAN:]))
                _dil_store(m_s.at[g], start, d, jnp.where(masks[0], m[:SPAN], m[SPAN:]))
                _dil_store(l_s.at[g], start, d, jnp.where(masks[0], l[:SPAN], l[SPAN:]))
                return 0

            lax.fori_loop(0, 16, blk, 0)

        def merge(i, _):
            rows = pl.ds(pl.multiple_of(i * 256, 256), 256)
            m = [m_s[g, rows, :] for g in range(3)]
            mx = jnp.maximum(jnp.maximum(m[0], m[1]), m[2])
            w = [jnp.exp(m[g] - mx) for g in range(3)]
            l = sum(l_s[g, rows, :] * w[g] for g in range(3))
            y_ref[rows, :] = sum(acc_s[g, rows, :] * w[g] for g in range(3)) / l
            lse_ref[rows, :] = mx + jnp.log(l)
            return 0

        lax.fori_loop(0, S // 256, merge, 0)

    def spec(g, t):
        return pl.BlockSpec((S, LANES), lambda p_: (0, (p_ * 3 + g) * 3 + t))

    return _hosted_call(
        body, comm, [p] * 9, name="dil_fwd", grid=(N_DIL_PAIRS,),
        in_specs=[spec(g, t) for g in range(3) for t in range(3)],
        out_specs=[pl.BlockSpec((S, LANES), lambda p_: (0, p_)), pl.BlockSpec((S, LANES), lambda p_: (0, p_))],
        out_shape=[_sds((S, 2 * LANES), F32), _sds((S, 2 * LANES), F32)],
        scratch_shapes=[pltpu.VMEM((3, S, LANES), F32)] * 3,
        vmem_mb=VMEM_MB)


def merge_fwd(ya_att, yb, p, w_bra, w_brb, comm=None):
    tm = 256
    gblk = LAY_G // D

    def body(a_ref, b_ref, ga_ref, gb_ref, wa_ref, wb_ref, mg_ref, ya_ref, yb_ref):
        ya = jnp.dot(a_ref[...], wa_ref[...], preferred_element_type=F32)
        ybp = jnp.dot(b_ref[...].astype(BF16), wb_ref[...], preferred_element_type=F32)
        mg_ref[...] = (_sigmoid(ga_ref[...]) * ya + _sigmoid(gb_ref[...]) * ybp).astype(BF16)
        ya_ref[...] = ya
        yb_ref[...] = ybp

    row = lambda w: pl.BlockSpec((tm, w), lambda i: (i, 0))
    return _hosted_call(
        body, comm, (ya_att, yb, p, p, w_bra, w_brb), name="merge_fwd", grid=(S // tm,),
        in_specs=[row(512), row(256), pl.BlockSpec((tm, D), lambda i: (i, gblk)), pl.BlockSpec((tm, D), lambda i: (i, gblk + 1)),
                  pl.BlockSpec((512, D), lambda i: (0, 0)), pl.BlockSpec((256, D), lambda i: (0, 0))],
        out_specs=[row(D), row(D), row(D)],
        out_shape=[_sds((S, D), BF16), _sds((S, D), F32), _sds((S, D), F32)])


def out_proj_fwd(merged, w_out, x, mod, g_ffn):
    tm = 256

    def body(a_ref, w_ref, x_ref, mod_ref, g_ref, mix_ref, x1_ref, h2_ref):
        mix = jnp.dot(a_ref[...], w_ref[...], preferred_element_type=F32)
        x1 = x_ref[...] + mod_ref[2:3, :] * mix
        r = lax.rsqrt(jnp.mean(x1 * x1, axis=1, keepdims=True) + EPS)
        h2 = (x1 * r * g_ref[...]) * (1.0 + mod_ref[4:5, :]) + mod_ref[3:4, :]
        mix_ref[...] = mix
        x1_ref[...] = x1
        h2_ref[...] = h2.astype(BF16)

    row = pl.BlockSpec((tm, D), lambda i: (i, 0))
    return pl.pallas_call(
        body, name="out_proj_fwd", grid=(S // tm,),
        in_specs=[row, pl.BlockSpec((D, D), lambda i: (0, 0)), row, pl.BlockSpec((8, D), lambda i: (0, 0)),
                  pl.BlockSpec((1, D), lambda i: (0, 0))],
        out_specs=[row, row, row],
        out_shape=[_sds((S, D), F32), _sds((S, D), F32), _sds((S, D), BF16)],
    )(merged, w_out, x, mod, g_ffn)


def ffn_up_fwd(h2, w_gate, w_up):
    tm = 1024

    def body(h_ref, wg_ref, wu_ref, a_ref, u_ref, z_ref):
        h = h_ref[...]
        a = jnp.dot(h, wg_ref[...], preferred_element_type=F32)
        u = jnp.dot(h, wu_ref[...], preferred_element_type=F32)
        a_ref[...] = a
        u_ref[...] = u
        z_ref[...] = (a * _sigmoid(a) * u).astype(BF16)

    out = pl.BlockSpec((tm, FF_PAD), lambda k, i: (i, k))
    return pl.pallas_call(
        body, name="ffn_up_fwd", grid=(4, S // tm),
        in_specs=[pl.BlockSpec((tm, D), lambda k, i: (i, 0)), pl.BlockSpec((None, D, FF_PAD), lambda k, i: (k, 0, 0)),
                  pl.BlockSpec((None, D, FF_PAD), lambda k, i: (k, 0, 0))],
        out_specs=[out, out, out],
        out_shape=[_sds((S, FFP), F32), _sds((S, FFP), F32), _sds((S, FFP), BF16)], compiler_params=_params(VMEM_MB),
    )(h2, w_gate, w_up)


def ffn_down_loss(z, w_down, x1, mod, g_final, tgt):
    tm = 256

    def body(z_ref, w_ref, x1_ref, mod_ref, g_ref, t_ref, dx2_ref, dffn_ref, dg_ref, dga_ref, loss_ref, s_dg, s_dga, s_loss):
        i = pl.program_id(0)

        @pl.when(i == 0)
        def _():
            s_dg[...] = jnp.zeros_like(s_dg)
            s_dga[...] = jnp.zeros_like(s_dga)
            s_loss[...] = jnp.zeros_like(s_loss)

        ffn = jnp.dot(z_ref[...], w_ref[...], preferred_element_type=F32)
        gaf = mod_ref[5:6, :]
        x2 = x1_ref[...] + gaf * ffn
        r = lax.rsqrt(jnp.mean(x2 * x2, axis=1, keepdims=True) + EPS)
        xh = x2 * r
        g = g_ref[...]
        e = xh * g - t_ref[...]
        s_loss[...] += 0.5 * jnp.sum(jnp.mean(e * e, axis=1, keepdims=True), axis=0, keepdims=True)
        dy = e * (1.0 / D)
        gdy = dy * g
        dx2 = r * (gdy - xh * jnp.mean(gdy * xh, axis=1, keepdims=True))
        s_dg[...] += _colsum8(dy * xh)
        s_dga[...] += _colsum8(dx2 * ffn)
        dx2_ref[...] = dx2
        dffn_ref[...] = (dx2 * gaf).astype(BF16)

        @pl.when(i == pl.num_programs(0) - 1)
        def _():
            dg_ref[...] = jnp.sum(s_dg[...], axis=0, keepdims=True)
            dga_ref[...] = jnp.sum(s_dga[...], axis=0, keepdims=True)
            loss_ref[...] = jnp.broadcast_to(s_loss[...], (1, LANES))

    row = pl.BlockSpec((tm, D), lambda i: (i, 0))
    vec = pl.BlockSpec((1, D), lambda i: (0, 0))
    return pl.pallas_call(
        body, name="ffn_down_loss", grid=(S // tm,),
        in_specs=[pl.BlockSpec((tm, FFP), lambda i: (i, 0)), pl.BlockSpec((FFP, D), lambda i: (0, 0)), row,
                  pl.BlockSpec((8, D), lambda i: (0, 0)), vec, row],
        out_specs=[row, row, vec, vec, pl.BlockSpec((1, LANES), lambda i: (0, 0))],
        out_shape=[_sds((S, D), F32), _sds((S, D), BF16), _sds((1, D), F32), _sds((1, D), F32), _sds((1, LANES), F32)],
        scratch_shapes=[pltpu.VMEM((8, D), F32), pltpu.VMEM((8, D), F32), pltpu.VMEM((1, 1), F32)],
        compiler_params=_params(VMEM_MB),
    )(z, w_down, x1, mod, g_final, tgt)


def ffn_down_bwd(dffn, w_down, a, u, z):
    tm, tn = 1024, 768

    def body(d_ref, w_ref, a_ref, u_ref, z_ref, da_ref, du_ref, dw_ref):
        i = pl.program_id(1)
        dff = d_ref[...]
        dz = lax.dot_general(dff, w_ref[...], (((1,), (1,)), ((), ())), preferred_element_type=F32)
        av, uv = a_ref[...], u_ref[...]
        sg = _sigmoid(av)
        du_ref[...] = (dz * (av * sg)).astype(BF16)
        da_ref[...] = (dz * uv * (sg * (1.0 + av * (1.0 - sg)))).astype(BF16)
        dw = lax.dot_general(z_ref[...], dff, (((0,), (0,)), ((), ())), preferred_element_type=F32)

        @pl.when(i == 0)
        def _():
            dw_ref[...] = dw

        @pl.when(i > 0)
        def _():
            dw_ref[...] += dw

    tile = pl.BlockSpec((tm, tn), lambda j, i: (i, j))
    return pl.pallas_call(
        body, name="ffn_down_bwd", grid=(FFP // tn, S // tm),
        in_specs=[pl.BlockSpec((tm, D), lambda j, i: (i, 0)), pl.BlockSpec((tn, D), lambda j, i: (j, 0)), tile, tile, tile],
        out_specs=[tile, tile, pl.BlockSpec((tn, D), lambda j, i: (j, 0))],
        out_shape=[_sds((S, FFP), BF16), _sds((S, FFP), BF16), _sds((FFP, D), F32)], compiler_params=_params(VMEM_MB),
    )(dffn, w_down, a, u, z)


def mm_nt(dy, w, name, comm=None):
    tm = 1024
    n = dy.shape[1]
    if w.ndim == 2:
        k_in, tk = w.shape[0], 768
        w_spec = pl.BlockSpec((k_in, tk), lambda i, k: (0, k))
    else:
        k_in, tk = w.shape[1], FF_PAD
        w_spec = pl.BlockSpec((None, k_in, tk), lambda i, k: (k, 0, 0))
    nk = n // tk

    def body(d_ref, w_ref, o_ref, acc):
        k = pl.program_id(1)
        part = lax.dot_general(d_ref[...], w_ref[...], (((1,), (1,)), ((), ())), preferred_element_type=F32)

        @pl.when(k == 0)
        def _():
            acc[...] = part

        @pl.when(k > 0)
        def _():
            acc[...] += part

        @pl.when(k == nk - 1)
        def _():
            o_ref[...] = acc[...]

    return _hosted_call(
        body, comm, (dy, w), name=name, grid=(S // tm, nk),
        in_specs=[pl.BlockSpec((tm, tk), lambda i, k: (i, k)), w_spec],
        out_specs=pl.BlockSpec((tm, k_in), lambda i, k: (i, 0)),
        out_shape=_sds((S, k_in), F32),
        scratch_shapes=[pltpu.VMEM((tm, k_in), F32)], vmem_mb=VMEM_MB)


def mm_tn(h, dy, name, shard_major=False, comm=None):
    tm, tn = 2048, 768
    k_in, n = h.shape[1], dy.shape[1]

    def body(h_ref, d_ref, o_ref):
        i = pl.program_id(1)
        dw = lax.dot_general(h_ref[...], d_ref[...], (((0,), (0,)), ((), ())), preferred_element_type=F32)

        @pl.when(i == 0)
        def _():
            o_ref[...] = dw

        @pl.when(i > 0)
        def _():
            o_ref[...] += dw

    if shard_major:
        out_spec, out_shape = pl.BlockSpec((None, k_in, tn), lambda j, i: (j, 0, 0)), _sds((n // tn, k_in, tn), F32)
    else:
        out_spec, out_shape = pl.BlockSpec((k_in, tn), lambda j, i: (0, j)), _sds((k_in, n), F32)
    return _hosted_call(
        body, comm, (h, dy), name=name, grid=(n // tn, S // tm),
        in_specs=[pl.BlockSpec((tm, k_in), lambda j, i: (i, 0)), pl.BlockSpec((tm, tn), lambda j, i: (i, j))],
        out_specs=out_spec, out_shape=out_shape, vmem_mb=VMEM_MB)


def mid_bwd(dh2a, dh2b, x1, dx2, mix, mod, g_ffn, p, ya, ybp, merged, ya_att, yb, w_out, w_bra, w_brb, comm=None):
    tm = 256
    gblk = LAY_G // D
    nsteps = S // tm

    def body(dha_ref, dhb_ref, x1_ref, dx2_ref, mix_ref, mod_ref, g_ref, ga_ref, gb_ref, ya_ref, yb_ref, mg_ref,
             att_ref, ybb_ref, wo_ref, wa_ref, wb_ref,
             dx1_ref, dpg_ref, datt_ref, dyb_ref, cs_ref, dwo_ref, dwa_ref, dwb_ref, s_cs):
        i = pl.program_id(0)

        @pl.when(i == 0)
        def _():
            s_cs[...] = jnp.zeros_like(s_cs)
            dwo_ref[...] = jnp.zeros_like(dwo_ref)
            dwa_ref[...] = jnp.zeros_like(dwa_ref)
            dwb_ref[...] = jnp.zeros_like(dwb_ref)

        x1 = x1_ref[...]
        g = g_ref[...]
        r = lax.rsqrt(jnp.mean(x1 * x1, axis=1, keepdims=True) + EPS)
        xh = x1 * r
        dh2 = dha_ref[...] + dhb_ref[...]
        s_cs[0] += _colsum8(dh2)
        s_cs[1] += _colsum8(dh2 * (xh * g))
        dn2 = dh2 * (1.0 + mod_ref[4:5, :])
        s_cs[2] += _colsum8(dn2 * xh)
        gd = dn2 * g
        dx1 = dx2_ref[...] + r * (gd - xh * jnp.mean(gd * xh, axis=1, keepdims=True))
        s_cs[3] += _colsum8(dx1 * mix_ref[...])
        dx1_ref[...] = dx1
        dmix = (dx1 * mod_ref[2:3, :]).astype(BF16)
        dmg = lax.dot_general(dmix, wo_ref[...], (((1,), (1,)), ((), ())), preferred_element_type=F32)
        sga, sgb = _sigmoid(ga_ref[...]), _sigmoid(gb_ref[...])
        dya = (dmg * sga).astype(BF16)
        dybp = (dmg * sgb).astype(BF16)
        dpg_ref[:, 0:D] = (dmg * ya_ref[...] * (sga * (1.0 - sga))).astype(BF16)
        dpg_ref[:, D:2 * D] = (dmg * yb_ref[...] * (sgb * (1.0 - sgb))).astype(BF16)
        datt_ref[...] = lax.dot_general(dya, wa_ref[...], (((1,), (1,)), ((), ())), preferred_element_type=F32).astype(BF16)
        dyb_ref[...] = lax.dot_general(dybp, wb_ref[...], (((1,), (1,)), ((), ())), preferred_element_type=F32)
        tn_dims = (((0,), (0,)), ((), ()))
        dwo_ref[...] += lax.dot_general(mg_ref[...], dmix, tn_dims, preferred_element_type=F32)
        dwa_ref[...] += lax.dot_general(att_ref[...], dya, tn_dims, preferred_element_type=F32)
        dwb_ref[...] += lax.dot_general(ybb_ref[...].astype(BF16), dybp, tn_dims, preferred_element_type=F32)

        @pl.when(i == nsteps - 1)
        def _():
            for t in range(4):
                cs_ref[t:t + 1, :] = jnp.sum(s_cs[t], axis=0, keepdims=True)
            cs_ref[4:8, :] = jnp.zeros((4, D), F32)

    row = lambda w: pl.BlockSpec((tm, w), lambda i: (i, 0))
    full = lambda a, b: pl.BlockSpec((a, b), lambda i: (0, 0))
    return _hosted_call(
        body, comm, (dh2a, dh2b, x1, dx2, mix, mod, g_ffn, p, p, ya, ybp, merged, ya_att, yb, w_out, w_bra, w_brb),
        name="mid_bwd", grid=(nsteps,),
        in_specs=[row(D), row(D), row(D), row(D), row(D), full(8, D), full(1, D),
                  pl.BlockSpec((tm, D), lambda i: (i, gblk)), pl.BlockSpec((tm, D), lambda i: (i, gblk + 1)),
                  row(D), row(D), row(D), row(512), row(256), full(D, D), full(512, D), full(256, D)],
        out_specs=[row(D), pl.BlockSpec((tm, 2 * D), lambda i: (i, LAY_G // (2 * D))), row(512), row(256), full(8, D),
                   full(D, D), full(512, D), full(256, D)],
        out_shape=[_sds((S, D), F32), _sds((S, LAY_N), BF16), _sds((S, 512), BF16), _sds((S, 256), F32), _sds((8, D), F32),
                   _sds((D, D), F32), _sds((512, D), F32), _sds((256, D), F32)],
        scratch_shapes=[pltpu.VMEM((4, 8, D), F32)],
        vmem_mb=VMEM_MB)


def fox_bwd(p, do, o, gcol, fcol, dp, comm=None):
    nq = S // FT
    nt = (((1,), (1,)), ((), ()))
    tn = (((0,), (0,)), ((), ()))

    def body(qkv_ref, do_ref, o_ref, g_ref, fc_ref, dp_in, dp_ref, df_ref, rs_ref, dq_s, qa_s, ka_s, dob_s, dl_s):
        del dp_in
        _fox_operands(qkv_ref, g_ref, fc_ref, qa_s, ka_s)
        masks = _head_masks(FT)
        lane = lax.broadcasted_iota(jnp.int32, (FT, LANES), 1)
        head0 = 2 * pl.program_id(0)
        causal = lax.broadcasted_iota(jnp.int32, (FT, FT), 1) <= lax.broadcasted_iota(jnp.int32, (FT, FT), 0)
        dq_s[...] = jnp.zeros_like(dq_s)
        rs_ref[...] = jnp.zeros_like(rs_ref)

        causal2 = jnp.concatenate([causal, causal], axis=0)

        def prep(i, _):
            r = pl.ds(pl.multiple_of(i * 256, 256), 256)
            m256 = _head_masks(256)
            dov, ov = do_ref[r, :].astype(F32), o_ref[r, :].astype(F32)
            for hh in range(2):
                dom = jnp.where(m256[hh], dov, 0.0)
                dob_s[hh, r, :] = dom.astype(BF16)
                dl_s[hh, r, :] = jnp.broadcast_to(jnp.sum(dom * ov, axis=1, keepdims=True), (256, LANES))
            return 0

        lax.fori_loop(0, S // 256, prep, 0)

        def stack(ref, q0, cols=slice(None)):
            return jnp.concatenate([ref[0, pl.ds(q0, FT), cols], ref[1, pl.ds(q0, FT), cols]], axis=0)

        def kloop(kb, _):
            k0 = pl.multiple_of(kb * FT, FT)
            k = qkv_ref[pl.ds(k0, FT), LANES:2 * LANES].astype(BF16)
            v = qkv_ref[pl.ds(k0, FT), 2 * LANES:3 * LANES].astype(BF16)
            ka = ka_s[pl.ds(k0, FT), :]

            def step(qi, carry, diagonal):
                dk, dv, df0, df1 = carry
                q0 = pl.multiple_of(qi * FT, FT)
                qa, dob = stack(qa_s, q0), stack(dob_s, q0)
                s = lax.dot_general(qa, ka, nt, preferred_element_type=F32)
                pr = jnp.exp(jnp.where(causal2, s, NEG)) if diagonal else jnp.exp(s)
                dpr = lax.dot_general(dob, v, nt, preferred_element_type=F32)
                ds = pr * (dpr - jnp.tile(stack(dl_s, q0), (1, FT // LANES)))
                dsb = ds.astype(BF16)
                dq = jnp.dot(dsb, k, preferred_element_type=F32) * SCALE
                dk = dk + lax.dot_general(dsb, qa[:, 0:LANES], tn, preferred_element_type=F32)
                dv = dv + lax.dot_general(pr.astype(BF16), dob, tn, preferred_element_type=F32)
                rsum = jnp.sum(ds, axis=1, keepdims=True)
                dq_s[pl.ds(q0, FT), :] += jnp.where(masks[0], dq[:FT], dq[FT:])
                rs_ref[pl.ds(q0, FT), :] += jnp.where(lane == head0, rsum[:FT], 0.0) + jnp.where(lane == head0 + 1, rsum[FT:], 0.0)
                return (dk, dv, df0 - jnp.sum(ds[:FT], axis=0, keepdims=True), df1 - jnp.sum(ds[FT:], axis=0, keepdims=True))

            z = jnp.zeros((FT, LANES), F32)
            z1 = jnp.zeros((1, FT), F32)
            carry = step(kb, (z, z, z1, z1), True)
            dk, dv, df0, df1 = lax.fori_loop(kb + 1, nq, lambda qi, cr: step(qi, cr, False), carry)
            dp_ref[pl.ds(k0, FT), LANES:2 * LANES] = dk.astype(BF16)
            dp_ref[pl.ds(k0, FT), 2 * LANES:3 * LANES] = dv.astype(BF16)
            df_ref[0:1, pl.ds(k0, FT)] = df0
            df_ref[1:2, pl.ds(k0, FT)] = df1
            return 0

        lax.fori_loop(0, S // FT, kloop, 0)
        dp_ref[:, 0:LANES] = dq_s[...].astype(BF16)

    a_blk = LAY_A // 384
    pair = pl.BlockSpec((S, LANES), lambda p_: (0, p_))
    heads = pl.BlockSpec((2, S, LANES), lambda p_: (p_, 0, 0))
    return _hosted_call(
        body, comm, (p, do, o, gcol, fcol, dp), name="fox_bwd", grid=(N_FOX_PAIRS,),
        in_specs=[pl.BlockSpec((S, 384), lambda p_: (0, a_blk + p_)), pair, pair, heads, heads, pl.BlockSpec(memory_space=pl.ANY)],
        out_specs=[pl.BlockSpec((S, 384), lambda p_: (0, a_blk + p_)), pl.BlockSpec((None, 2, S), lambda p_: (p_, 0, 0)),
                   pl.BlockSpec((None, S, LANES), lambda p_: (p_, 0, 0))],
        out_shape=[_sds((S, LAY_N), BF16), _sds((4, 2, S), F32), _sds((4, S, LANES), F32)],
        scratch_shapes=[pltpu.VMEM((S, LANES), F32), pltpu.VMEM((2, S, 2 * LANES), BF16), pltpu.VMEM((S, 2 * LANES), BF16),
                        pltpu.VMEM((2, S, LANES), BF16), pltpu.VMEM((2, S, LANES), F32)],
        aliases={5: 0}, vmem_mb=VMEM_MB)


def fgate_bwd(dfrow, dfcol, fraw, dp):
    def body(df_ref, dc_ref, f_ref, dp_in, dpf_ref, db_ref):
        del dp_in
        lane = lax.broadcasted_iota(jnp.int32, (8, S), 1)
        rsum = (dc_ref[0] + dc_ref[1]) + (dc_ref[2] + dc_ref[3])
        acc, sh = df_ref[...] + rsum.T[0:8, :], 1
        while sh < S:
            acc = acc + jnp.where(lane < S - sh, pltpu.roll(acc, S - sh, 1), 0.0)
            sh *= 2
        df = acc * _sigmoid(-f_ref[...])
        db_ref[...] = jnp.broadcast_to(jnp.sum(df, axis=1, keepdims=True), (8, LANES))
        dfc = jnp.concatenate([df, jnp.zeros((LANES - 8, S), F32)], axis=0).T
        dpf_ref[:, 0:LANES] = dfc.astype(BF16)
        dpf_ref[:, LANES:2 * LANES] = jnp.zeros((S, LANES), BF16)

    return pl.pallas_call(
        body, name="fgate_bwd", grid=(1,),
        in_specs=[pl.BlockSpec((8, S), lambda i: (0, 0)), pl.BlockSpec((4, S, LANES), lambda i: (0, 0, 0)),
                  pl.BlockSpec((8, S), lambda i: (0, 0)), pl.BlockSpec(memory_space=pl.ANY)],
        out_specs=[pl.BlockSpec((S, 2 * LANES), lambda i: (0, LAY_F // (2 * LANES))), pl.BlockSpec((8, LANES), lambda i: (0, 0))],
        out_shape=[_sds((S, LAY_N), BF16), _sds((8, LANES), F32)],
        input_output_aliases={3: 0},
        compiler_params=_params(VMEM_MB),
    )(dfrow, dfcol, fraw, dp)


def dil_bwd(p, dyb, yb, lse, tabs, dp, comm=None):
    def body(*refs):
        qkv = [refs[3 * g:3 * g + 3] for g in range(3)]
        dy_ref, y_ref, lse_ref, c_ref, s1_ref, s2_ref = refs[9:15]
        dp_ref = refs[16]
        dq_s, dk_s, dv_s, dl_s = refs[17:21]
        masks = _head_masks(SPAN)
        m256 = _head_masks(256)
        nt = (((1,), (1,)), ((), ()))
        tn = (((0,), (0,)), ((), ()))
        dk_s[...] = jnp.zeros_like(dk_s)
        dv_s[...] = jnp.zeros_like(dv_s)

        def prep(i, _):
            rows = pl.ds(pl.multiple_of(i * 256, 256), 256)
            pr = dy_ref[rows, :] * y_ref[rows, :]
            d0 = jnp.sum(jnp.where(m256[0], pr, 0.0), axis=1, keepdims=True)
            d1 = jnp.sum(jnp.where(m256[1], pr, 0.0), axis=1, keepdims=True)
            dl_s[rows, :] = jnp.where(m256[0], d0, d1)
            return 0

        lax.fori_loop(0, S // 256, prep, 0)

        for g, (d, nb) in enumerate(DIL_GROUPS):
            q_ref, k_ref, v_ref = qkv[g]

            def blk(n, _):
                start, prev, has_prev = _dil_block(n, d, nb)
                q = _dil_rows(q_ref, start, d)
                kc = jnp.concatenate([_dil_rows(k_ref, prev, d), _dil_rows(k_ref, start, d)], axis=0).astype(BF16)
                vc = jnp.concatenate([_dil_rows(v_ref, prev, d), _dil_rows(v_ref, start, d)], axis=0).astype(BF16)
                dov = _dil_rows(dy_ref, start, d)
                lsev = _dil_rows(lse_ref, start, d)
                dlv = _dil_rows(dl_s, start, d)
                valid = _band_mask(has_prev)
                valid2 = jnp.concatenate([valid, valid], axis=0)

                def stack(t):
                    return jnp.concatenate([jnp.where(masks[0], t, 0.0), jnp.where(masks[1], t, 0.0)], axis=0)

                def column(t):
                    return jnp.concatenate([jnp.max(jnp.where(masks[hh], t, NEG), axis=1, keepdims=True) for hh in range(2)], axis=0)

                q2 = (stack(q) * SCALE).astype(BF16)
                dob = stack(dov).astype(BF16)
                s = jnp.where(valid2, lax.dot_general(q2, kc, nt, preferred_element_type=F32), NEG)
                pr = jnp.exp(s - column(lsev))
                dpr = lax.dot_general(dob, vc, nt, preferred_element_type=F32)
                dsb = (pr * (dpr - column(dlv))).astype(BF16)
                dq = jnp.dot(dsb, kc, preferred_element_type=F32) * SCALE
                dkc = lax.dot_general(dsb, q2, tn, preferred_element_type=F32)
                dvc = lax.dot_general(pr.astype(BF16), dob, tn, preferred_element_type=F32)
                _dil_store(dq_s.at[g], start, d, jnp.where(masks[0], dq[:SPAN], dq[SPAN:]))
                for ref, val in ((dk_s.at[g], dkc), (dv_s.at[g], dvc)):
                    _dil_store(ref, prev, d, _dil_rows(ref, prev, d) + jnp.where(has_prev, val[0:SPAN], 0.0))
                    _dil_store(ref, start, d, _dil_rows(ref, start, d) + val[SPAN:])
                return 0

            lax.fori_loop(0, 16, blk, 0)

        def fin(i, _):
            rows = pl.ds(pl.multiple_of(i * 256, 256), 256)
            c, s1, s2 = c_ref[rows, :], s1_ref[rows, :], s2_ref[rows, :]
            for g in range(3):
                base = g * 384
                dp_ref[rows, base:base + LANES] = _rope_bwd(dq_s[g, rows, :], c, s1, s2).astype(BF16)
                dp_ref[rows, base + LANES:base + 2 * LANES] = _rope_bwd(dk_s[g, rows, :], c, s1, s2).astype(BF16)
                dp_ref[rows, base + 2 * LANES:base + 3 * LANES] = dv_s[g, rows, :].astype(BF16)
            return 0

        lax.fori_loop(0, S // 256, fin, 0)

    def spec(g, t):
        return pl.BlockSpec((S, LANES), lambda p_: (0, (p_ * 3 + g) * 3 + t))

    pair = pl.BlockSpec((S, LANES), lambda p_: (0, p_))
    tab = pl.BlockSpec((S, LANES), lambda p_: (0, 0))
    return _hosted_call(
        body, comm, [p] * 9 + [dyb, yb, lse, *tabs, dp], name="dil_bwd", grid=(N_DIL_PAIRS,),
        in_specs=[spec(g, t) for g in range(3) for t in range(3)] + [pair, pair, pair, tab, tab, tab, pl.BlockSpec(memory_space=pl.ANY)],
        out_specs=pl.BlockSpec((S, 1152), lambda p_: (0, p_)),
        out_shape=_sds((S, LAY_N), BF16),
        scratch_shapes=[pltpu.VMEM((3, S, LANES), F32)] * 3 + [pltpu.VMEM((S, LANES), F32)],
        aliases={15: 0}, vmem_mb=VMEM_MB)


def in_bwd_tail(dh1, x, dx1, mod, g_mix, comm=None):
    tm = 256
    nsteps = S // tm

    def body(dh_ref, x_ref, dx1_ref, mod_ref, g_ref, dx_ref, cs_ref, s_cs):
        i = pl.program_id(0)

        @pl.when(i == 0)
        def _():
            s_cs[...] = jnp.zeros_like(s_cs)

        xv, g, dh = x_ref[...], g_ref[...], dh_ref[...]
        r = lax.rsqrt(jnp.mean(xv * xv, axis=1, keepdims=True) + EPS)
        xh = xv * r
        s_cs[0] += _colsum8(dh)
        s_cs[1] += _colsum8(dh * (xh * g))
        dn = dh * (1.0 + mod_ref[1:2, :])
        s_cs[2] += _colsum8(dn * xh)
        gd = dn * g
        dx_ref[...] = dx1_ref[...] + r * (gd - xh * jnp.mean(gd * xh, axis=1, keepdims=True))

        @pl.when(i == nsteps - 1)
        def _():
            for t in range(3):
                cs_ref[t:t + 1, :] = jnp.sum(s_cs[t], axis=0, keepdims=True)
            cs_ref[3:8, :] = jnp.zeros((5, D), F32)

    row = pl.BlockSpec((tm, D), lambda i: (i, 0))
    return _hosted_call(
        body, comm, (dh1, x, dx1, mod, g_mix), name="in_bwd_tail", grid=(nsteps,),
        in_specs=[row, row, row, pl.BlockSpec((8, D), lambda i: (0, 0)), pl.BlockSpec((1, D), lambda i: (0, 0))],
        out_specs=[row, pl.BlockSpec((8, D), lambda i: (0, 0))],
        out_shape=[_sds((S, D), F32), _sds((8, D), F32)],
        scratch_shapes=[pltpu.VMEM((3, 8, D), F32)])


def _lay_pieces():
    out = []
    qa, ka, va, fa, qb, kb, vb, ga = 0, 512, 1024, 1536, 1544, 2312, 3080, 3848
    for p in range(N_DIL_PAIRS):
        for g in range(3):
            base = LAY_B + (p * 3 + g) * 384
            hd0 = (4 * g + 2 * p) * HD
            out += [(base, qb + hd0, LANES), (base + LANES, kb + hd0, LANES), (base + 2 * LANES, vb + hd0, LANES)]
    for p in range(N_FOX_PAIRS):
        base = LAY_A + p * 384
        out += [(base, qa + p * LANES, LANES), (base + LANES, ka + p * LANES, LANES), (base + 2 * LANES, va + p * LANES, LANES)]
    out.append((LAY_F, fa, 8))
    out.append((LAY_G, ga, 2 * D))
    return out


def lay_from_nat(w_nat):
    parts, pos = [], 0
    for lay, nat, width in sorted(_lay_pieces()):
        if lay > pos:
            parts.append(jnp.zeros((w_nat.shape[0], lay - pos), w_nat.dtype))
        parts.append(w_nat[:, nat:nat + width])
        pos = lay + width
    if pos < LAY_N:
        parts.append(jnp.zeros((w_nat.shape[0], LAY_N - pos), w_nat.dtype))
    return jnp.concatenate(parts, axis=1)


def nat_from_lay(w_lay):
    parts = [w_lay[:, lay:lay + width] for lay, nat, width in sorted(_lay_pieces(), key=lambda t: t[1])]
    return jnp.concatenate(parts, axis=1)


def _shard_runs():
    runs = []
    for lay, nat, width in _lay_pieces():
        while width:
            k, loc = nat // IN_SHARD, nat % IN_SHARD
            w = min(width, IN_SHARD - loc)
            runs.append((lay, k, loc, w))
            lay, nat, width = lay + w, nat + w, width - w
    return runs


def lay_from_shards(g):
    tm = 256

    def body(g_ref, o_ref):
        o_ref[:, LAY_F:LAY_G] = jnp.zeros((tm, LAY_G - LAY_F), g.dtype)
        for lay, k, loc, w in _shard_runs():
            o_ref[:, lay:lay + w] = g_ref[k, :, loc:loc + w]

    return pl.pallas_call(
        body, name="lay_from_shards", grid=(D // tm,),
        in_specs=[pl.BlockSpec((4, tm, IN_SHARD_PAD), lambda i: (0, i, 0))],
        out_specs=pl.BlockSpec((tm, LAY_N), lambda i: (i, 0)),
        out_shape=_sds((D, LAY_N), g.dtype), compiler_params=_params(VMEM_MB),
    )(g)


def shards_from_lay(dw_lay):
    tm = 256

    def body(x_ref, o_ref):
        o_ref[:, :, IN_SHARD:] = jnp.zeros((4, tm, IN_SHARD_PAD - IN_SHARD), F32)
        for lay, k, loc, w in _shard_runs():
            o_ref[k, :, loc:loc + w] = x_ref[:, lay:lay + w]

    return pl.pallas_call(
        body, name="shards_from_lay", grid=(D // tm,),
        in_specs=[pl.BlockSpec((tm, LAY_N), lambda i: (i, 0))],
        out_specs=pl.BlockSpec((4, tm, IN_SHARD_PAD), lambda i: (0, i, 0)),
        out_shape=_sds((4, D, IN_SHARD_PAD), F32), compiler_params=_params(VMEM_MB),
    )(dw_lay)


def _pos():
    return lax.axis_index("x"), lax.axis_index("y"), lax.axis_index("c")


def _other_chips(x, y):
    return [(1 - x, y), (x, 1 - y), (1 - x, 1 - y)]


def _remote(src, dst, send_sem, recv_sem, dev):
    return pltpu.make_async_remote_copy(src_ref=src, dst_ref=dst, send_sem=send_sem, recv_sem=recv_sem,
                                        device_id=dev, device_id_type=MESH)


VMEM_SPEC = pl.BlockSpec(memory_space=pltpu.VMEM)
ANY_SPEC = pl.BlockSpec(memory_space=pl.ANY)


def gather_all(v, name, with_sum):
    r = v.shape[0]

    def body(v_ref, out_ref, *rest):
        send_s, recv_s = rest[-2:]
        x, y, c = _pos()
        me = 4 * x + 2 * y + c
        out_ref[me] = v_ref[...]
        peers = []
        for m in range(1, 8):
            px = 1 - x if m & 4 else x
            py = 1 - y if m & 2 else y
            pc = 1 - c if m & 1 else c
            peers.append((px, py, pc))
        copies = [_remote(v_ref, out_ref.at[me], send_s.at[i], recv_s.at[i], dev) for i, dev in enumerate(peers)]
        for cp in copies:
            cp.start()
        for i, (px, py, pc) in enumerate(peers):
            _remote(v_ref, out_ref.at[4 * px + 2 * py + pc], send_s.at[i], recv_s.at[i], (px, py, pc)).wait_recv()
        for cp in copies:
            cp.wait_send()
        if with_sum:
            acc = out_ref[0]
            for b in range(1, 8):
                acc = acc + out_ref[b]
            rest[0][...] = acc

    out_shape = [_sds((8, r, LANES), F32)] + ([_sds((r, LANES), F32)] if with_sum else [])
    return pl.pallas_call(
        body, name=name, in_specs=[VMEM_SPEC], out_specs=[VMEM_SPEC] * len(out_shape), out_shape=out_shape,
        scratch_shapes=[pltpu.SemaphoreType.DMA((7,)), pltpu.SemaphoreType.DMA((7,))],
    )(v)


def mod_exchange(c_all, w_ada_sh, b_sh):
    def body(c_ref, w_ref, b_ref, out_ref, sc_ref, modp, send_s, recv_s):
        cv = c_ref[...]
        sc = cv * _sigmoid(cv)
        sc_ref[...] = sc
        modp[...] = jnp.dot(sc, w_ref[...], precision=lax.Precision.HIGHEST, preferred_element_type=F32) + b_ref[...]
        x, y, c = _pos()
        k = 2 * x + y
        out_ref[k] = modp[...]
        chips = _other_chips(x, y)
        copies = [_remote(modp, out_ref.at[k], send_s.at[j], recv_s.at[j], (cx, cy, c)) for j, (cx, cy) in enumerate(chips)]
        for cp in copies:
            cp.start()
        for j, (cx, cy) in enumerate(chips):
            _remote(modp, out_ref.at[2 * cx + cy], send_s.at[j], recv_s.at[j], (cx, cy, c)).wait_recv()
        for cp in copies:
            cp.wait_send()

    n = w_ada_sh.shape[1]
    return pl.pallas_call(
        body, name="mod_exchange", in_specs=[VMEM_SPEC] * 3, out_specs=[VMEM_SPEC] * 2,
        out_shape=[_sds((4, 8, n), F32), _sds((8, D), F32)],
        scratch_shapes=[pltpu.VMEM((8, n), F32), pltpu.SemaphoreType.DMA((3,)), pltpu.SemaphoreType.DMA((3,))],
        compiler_params=_params(VMEM_MB),
    )(c_all, w_ada_sh, b_sh)


def gather_weights(shards):
    n = len(shards)

    def body(*refs):
        ins, outs = refs[:n], refs[n:2 * n]
        send_s, recv_s, fsend_s, frecv_s, loc_s = refs[2 * n:]
        x, y, c = _pos()
        k = 2 * x + y
        chips = _other_chips(x, y)
        local, sends, fwds = [], [], []
        for a in range(n):
            half = ins[a].shape[0] // 2
            rows = pl.ds(c * half, half)
            lc = pltpu.make_async_copy(ins[a], outs[a].at[k], loc_s.at[a])
            lc.start()
            local.append(lc)
            for j, (cx, cy) in enumerate(chips):
                cp = _remote(ins[a].at[rows], outs[a].at[k, rows], send_s.at[3 * a + j], recv_s.at[3 * a + j], (cx, cy, c))
                cp.start()
                sends.append(cp)
        for a in range(n):
            half = ins[a].shape[0] // 2
            rows = pl.ds(c * half, half)
            for j, (cx, cy) in enumerate(chips):
                kj = 2 * cx + cy
                _remote(ins[a].at[rows], outs[a].at[kj, rows], send_s.at[3 * a + j], recv_s.at[3 * a + j], (cx, cy, c)).wait_recv()
                fw = _remote(outs[a].at[kj, rows], outs[a].at[kj, rows], fsend_s.at[3 * a + j], frecv_s.at[3 * a + j], (x, y, 1 - c))
                fw.start()
                fwds.append(fw)
        for a in range(n):
            half = ins[a].shape[0] // 2
            orows = pl.ds((1 - c) * half, half)
            for j, (cx, cy) in enumerate(chips):
                kj = 2 * cx + cy
                _remote(outs[a].at[kj, orows], outs[a].at[kj, orows], fsend_s.at[3 * a + j], frecv_s.at[3 * a + j], (x, y, 1 - c)).wait_recv()
        for cp in sends + fwds:
            cp.wait_send()
        for lc in local:
            lc.wait()

    return pl.pallas_call(
        body, name="gather_weights", in_specs=[ANY_SPEC] * n, out_specs=[ANY_SPEC] * n,
        out_shape=[_sds((4,) + s.shape, s.dtype) for s in shards],
        scratch_shapes=[pltpu.SemaphoreType.DMA((3 * n,))] * 4 + [pltpu.SemaphoreType.DMA((n,))],
    )(*shards)


def _row_tile(rows, cap=256):
    t = cap
    while rows % t or t % 8:
        t -= 8
    return t


def _comm_wait(sends, recvs, local=()):
    for cp in recvs:
        cp.wait_recv()
    for cp in sends:
        cp.wait_send()
    for cp in local:
        cp.wait()


def ag_ici(shards):
    n = len(shards)

    def copies(ins, outs, sems):
        send_s, recv_s, loc_s = sems
        x, y, c = _pos()
        k = 2 * x + y
        sends, recvs, local = [], [], []
        for a in range(n):
            half = ins[a].shape[0] // 2
            rows = pl.ds(c * half, half)
            local.append(pltpu.make_async_copy(ins[a], outs[a].at[k], loc_s.at[a]))
            for j, (cx, cy) in enumerate(_other_chips(x, y)):
                sem = (send_s.at[3 * a + j], recv_s.at[3 * a + j], (cx, cy, c))
                sends.append(_remote(ins[a].at[rows], outs[a].at[k, rows], *sem))
                recvs.append(_remote(ins[a].at[rows], outs[a].at[2 * cx + cy, rows], *sem))
        return sends, recvs, local

    def start(ins, outs, sems):
        sends, _, local = copies(ins, outs, sems)
        for cp in local + sends:
            cp.start()

    def wait(ins, outs, sems):
        _comm_wait(*copies(ins, outs, sems))

    return Comm(shards, [_sds((4,) + s.shape, s.dtype) for s in shards], [3 * n, 3 * n, n], start, wait)


def ag_d2d(bufs):
    n = len(bufs)

    def copies(ins, outs, sems):
        send_s, recv_s = sems
        x, y, c = _pos()
        sends, recvs = [], []
        for a in range(n):
            half = outs[a].shape[1] // 2
            rows, orows = pl.ds(c * half, half), pl.ds((1 - c) * half, half)
            for j, (cx, cy) in enumerate(_other_chips(x, y)):
                kj = 2 * cx + cy
                sem = (send_s.at[3 * a + j], recv_s.at[3 * a + j], (x, y, 1 - c))
                sends.append(_remote(outs[a].at[kj, rows], outs[a].at[kj, rows], *sem))
                recvs.append(_remote(outs[a].at[kj, orows], outs[a].at[kj, orows], *sem))
        return sends, recvs

    def start(ins, outs, sems):
        for cp in copies(ins, outs, sems)[0]:
            cp.start()

    def wait(ins, outs, sems):
        _comm_wait(*copies(ins, outs, sems))

    return Comm(bufs, [_sds(b.shape, b.dtype) for b in bufs], [3 * n, 3 * n], start, wait, aliases={a: a for a in range(n)})


def rs_a(grads):
    n = len(grads)

    def copies(ins, outs, sems):
        send_s, recv_s = sems
        x, y, c = _pos()
        cps = []
        for a in range(n):
            half = ins[a].shape[1] // 2
            cps.append(_remote(ins[a].at[:, pl.ds((1 - c) * half, half), :], outs[a], send_s.at[a], recv_s.at[a], (x, y, 1 - c)))
        return cps

    def start(ins, outs, sems):
        for cp in copies(ins, outs, sems):
            cp.start()

    def wait(ins, outs, sems):
        cps = copies(ins, outs, sems)
        _comm_wait(cps, cps)

    return Comm(grads, [_sds((4, g.shape[1] // 2, g.shape[2]), g.dtype) for g in grads], [n, n], start, wait)


def rs_b(pres):
    n = len(pres)

    def copies(ins, outs, sems):
        send_s, recv_s, loc_s = sems
        x, y, c = _pos()
        k = 2 * x + y
        cps, local = [], []
        for a in range(n):
            local.append(pltpu.make_async_copy(ins[a].at[k], outs[a].at[3], loc_s.at[a]))
            for j, (cx, cy) in enumerate(_other_chips(x, y)):
                cps.append(_remote(ins[a].at[2 * cx + cy], outs[a].at[j], send_s.at[3 * a + j], recv_s.at[3 * a + j], (cx, cy, c)))
        return cps, local

    def start(ins, outs, sems):
        cps, local = copies(ins, outs, sems)
        for cp in local + cps:
            cp.start()

    def wait(ins, outs, sems):
        cps, local = copies(ins, outs, sems)
        _comm_wait(cps, cps, local)

    return Comm(pres, [_sds(p_.shape, p_.dtype) for p_ in pres], [3 * n, 3 * n, n], start, wait)


def rs_c(reds):
    n = len(reds)

    def copies(ins, outs, sems):
        send_s, recv_s, loc_s = sems
        x, y, c = _pos()
        sends, recvs, local = [], [], []
        for a in range(n):
            half = ins[a].shape[0]
            rows, orows = pl.ds(c * half, half), pl.ds((1 - c) * half, half)
            local.append(pltpu.make_async_copy(ins[a], outs[a].at[rows], loc_s.at[a]))
            sem = (send_s.at[a], recv_s.at[a], (x, y, 1 - c))
            sends.append(_remote(ins[a], outs[a].at[rows], *sem))
            recvs.append(_remote(ins[a], outs[a].at[orows], *sem))
        return sends, recvs, local

    def start(ins, outs, sems):
        sends, _, local = copies(ins, outs, sems)
        for cp in local + sends:
            cp.start()

    def wait(ins, outs, sems):
        _comm_wait(*copies(ins, outs, sems))

    return Comm(reds, [_sds((2 * r_.shape[0], r_.shape[1]), r_.dtype) for r_ in reds], [n, n, n], start, wait)


def comm_join(*comms):
    ni = np.cumsum([0] + [len(c.ins) for c in comms])
    no = np.cumsum([0] + [len(c.out_shapes) for c in comms])
    ns = np.cumsum([0] + [len(c.sems) for c in comms])

    def parts(ins, outs, sems):
        return [(c, ins[ni[i]:ni[i + 1]], outs[no[i]:no[i + 1]], sems[ns[i]:ns[i + 1]]) for i, c in enumerate(comms)]

    def start(ins, outs, sems):
        for c, a, b, s in parts(ins, outs, sems):
            c.start(a, b, s)

    def wait(ins, outs, sems):
        for c, a, b, s in parts(ins, outs, sems):
            c.wait(a, b, s)

    aliases = {int(ni[i]) + k: int(no[i]) + v for i, c in enumerate(comms) for k, v in c.aliases.items()}
    return Comm(sum((c.ins for c in comms), []), sum((c.out_shapes for c in comms), []), sum((c.sems for c in comms), []),
                start, wait, aliases)


def comm_only(comm, name):
    nci, nco = len(comm.ins), len(comm.out_shapes)

    def body(*refs):
        ins, outs, sems = refs[:nci], refs[nci:nci + nco], refs[nci + nco:]
        comm.start(ins, outs, sems)
        comm.wait(ins, outs, sems)

    return pl.pallas_call(
        body, name=name, in_specs=[ANY_SPEC] * nci, out_specs=[ANY_SPEC] * nco, out_shape=comm.out_shapes,
        scratch_shapes=[pltpu.SemaphoreType.DMA((s,)) for s in comm.sems],
        input_output_aliases=comm.aliases,
    )(*comm.ins)


def rs_add_halves(g, other, core, name):
    _, r, cdim = g.shape
    half = r // 2
    tr = _row_tile(half, 128)
    nb = half // tr

    def body(core_ref, g_ref, o_ref, out_ref):
        del core_ref
        out_ref[...] = (g_ref[...] + o_ref[...]).astype(BF16)

    grid_spec = pltpu.PrefetchScalarGridSpec(
        num_scalar_prefetch=1, grid=(4, nb),
        in_specs=[pl.BlockSpec((None, tr, cdim), lambda k, i, cr: (k, cr[0] * nb + i, 0)),
                  pl.BlockSpec((None, tr, cdim), lambda k, i, cr: (k, i, 0))],
        out_specs=pl.BlockSpec((None, tr, cdim), lambda k, i, cr: (k, i, 0)))
    return pl.pallas_call(body, name=name, grid_spec=grid_spec, out_shape=_sds((4, half, cdim), BF16))(core, g, other)


def rs_add_slabs(t, name):
    _, half, cdim = t.shape
    tr = _row_tile(half, 128)

    def body(t_ref, out_ref):
        s = [t_ref[i].astype(F32) for i in range(4)]
        out_ref[...] = ((s[3] + s[0]) + s[1]) + s[2]

    return pl.pallas_call(
        body, name=name, grid=(half // tr,),
        in_specs=[pl.BlockSpec((4, tr, cdim), lambda i: (0, i, 0))],
        out_specs=pl.BlockSpec((tr, cdim), lambda i: (i, 0)),
        out_shape=_sds((half, cdim), F32),
    )(t)


def _adam_math(w, g, m, v):
    m = ADAM_B1 * m + (1.0 - ADAM_B1) * g
    v = ADAM_B2 * v + (1.0 - ADAM_B2) * (g * g)
    m_hat = m / (1.0 - ADAM_B1 ** ADAM_STEP)
    v_hat = v / (1.0 - ADAM_B2 ** ADAM_STEP)
    delta = -ADAM_LR * (m_hat / (jnp.sqrt(v_hat) + ADAM_EPS) + ADAM_WD * w)
    return delta, m, v


def adam(w, g, m, v, name):
    r, cdim = w.shape
    tr = _row_tile(r) if r >= 8 else r

    def body(w_ref, g_ref, m_ref, v_ref, g_out, d_ref, nm_ref, nv_ref):
        gv = g_ref[:, :cdim]
        g_out[...] = gv
        d_ref[...], nm_ref[...], nv_ref[...] = _adam_math(w_ref[...], gv, m_ref[...], v_ref[...])

    blk = pl.BlockSpec((tr, cdim), lambda i: (i, 0))
    return pl.pallas_call(
        body, name=name, grid=(r // tr,), in_specs=[blk, pl.BlockSpec((tr, g.shape[1]), lambda i: (i, 0)), blk, blk],
        out_specs=[blk] * 4, out_shape=[_sds((r, cdim), F32)] * 4,
    )(w, g, m, v)


def adam_w_ada(sc_t, dmod_sh, w, m, v):
    r, cdim = w.shape
    tr = 256

    def body(s_ref, d_ref, w_ref, m_ref, v_ref, g_ref, dl_ref, nm_ref, nv_ref):
        g = jnp.dot(s_ref[...], d_ref[...], precision=lax.Precision.HIGHEST, preferred_element_type=F32)
        g_ref[...] = g
        dl_ref[...], nm_ref[...], nv_ref[...] = _adam_math(w_ref[...], g, m_ref[...], v_ref[...])

    blk = pl.BlockSpec((tr, cdim), lambda i: (i, 0))
    return pl.pallas_call(
        body, name="adam_w_ada", grid=(r // tr,),
        in_specs=[pl.BlockSpec((tr, LANES), lambda i: (i, 0)), pl.BlockSpec((LANES, cdim), lambda i: (0, 0)), blk, blk, blk],
        out_specs=[blk] * 4, out_shape=[_sds((r, cdim), F32)] * 4,
    )(sc_t, dmod_sh, w, m, v)


SMALL_ROWS = 80


def kernel(x, c, w_ada, b_ada, g_mix, w_in, b_fgate, w_br_a, w_br_b, w_out, g_ffn, w_ffn_gate, w_ffn_up, w_ffn_down, g_final, loss_target, m_w_ada, m_b_ada, m_g_mix, m_w_in, m_b_fgate, m_w_br_a, m_w_br_b, m_w_out, m_g_ffn, m_w_ffn_gate, m_w_ffn_up, m_w_ffn_down, m_g_final, v_w_ada, v_b_ada, v_g_mix, v_w_in, v_b_fgate, v_w_br_a, v_w_br_b, v_w_out, v_g_ffn, v_w_ffn_gate, v_w_ffn_up, v_w_ffn_down, v_g_final):
    xi, yi, ci = _pos()
    chip = 2 * xi + yi
    seq = 4 * xi + 2 * yi + ci
    n_ada = w_ada.shape[2]

    c_all = gather_all(c.reshape(8, LANES), "gather_c", False)[0].reshape(8, D)
    b_sh = lax.dynamic_slice(b_ada, (0, chip * n_ada), (1, n_ada))
    mod_all, sc = mod_exchange(c_all, w_ada[0], b_sh)
    mod = lax.dynamic_index_in_dim(mod_all, seq, axis=1, keepdims=False).reshape(6, D)
    mod8 = jnp.pad(mod, ((0, 2), (0, 0)))

    shards = [
        jnp.pad(w_in[0], ((0, 0), (0, IN_SHARD_PAD - IN_SHARD))), w_br_a[0], w_br_b[0], w_out[0],
        jnp.pad(w_ffn_gate[0], ((0, 0), (0, FF_PAD - FF_SHARD))), jnp.pad(w_ffn_up[0], ((0, 0), (0, FF_PAD - FF_SHARD))),
        jnp.pad(w_ffn_down[0], ((0, FF_PAD - FF_SHARD), (0, 0))),
    ]
    shards = [s.astype(BF16) for s in shards]
    core = ci.astype(jnp.int32).reshape(1)
    xs, tgt, g_fin = x[0], loss_target[0], g_final.reshape(1, D)

    s_in, s_bra, s_brb, s_out, s_gate, s_up, s_down = shards

    def halves(gs, others, tag):
        return [rs_add_halves(g, o, core, f"rs_{tag}_halves_{i}") for i, (g, o) in enumerate(zip(gs, others))]

    def slab_sums(ts, tag):
        return [rs_add_slabs(t, f"rs_{tag}_slabs_{i}") for i, t in enumerate(ts)]

    g_in = gather_weights([s_in])[0]
    w_lay = lay_from_shards(g_in)
    tabs = rope_tables()
    h1 = norm_mod_fwd(xs, g_mix, mod8, 0, 1)
    p, mix_w = in_proj_fwd(h1, w_lay, tabs, comm=ag_ici([s_bra, s_brb, s_out]))
    frow, fraw, fcol = fgate_fwd(p, jnp.pad(b_fgate, ((0, 0), (0, LANES - 8))))
    (ya_att, gcol), res = fox_fwd(p, fcol, comm=comm_join(ag_d2d(mix_w), ag_ici([s_gate, s_up])))
    g_bra, g_brb, g_out = res[:3]
    (yb, lse_b), res = dil_fwd(p, comm=comm_join(ag_d2d(res[3:]), ag_ici([s_down])))
    w_gate, w_up = res[:2]
    w_bra = g_bra.transpose(1, 0, 2).reshape(512, D)
    w_brb = g_brb.transpose(1, 0, 2).reshape(256, D)
    w_o = g_out.reshape(D, D)
    (merged, ya, ybp), (g_down,) = merge_fwd(ya_att, yb, p, w_bra, w_brb, comm=ag_d2d(res[2:]))
    w_down = g_down.reshape(FFP, D)
    mix, x1, h2 = out_proj_fwd(merged, w_o, xs, mod8, g_ffn)
    a, u, z = ffn_up_fwd(h2, w_gate, w_up)
    dx2, dffn, dg_final, dga_f, loss_part = ffn_down_loss(z, w_down, x1, mod8, g_fin, tgt)

    da, du, dw_down = ffn_down_bwd(dffn, w_down, a, u, z)
    g_down = [dw_down.reshape(4, FF_PAD, D)]
    dh2a, oth = mm_nt(da, w_gate, "ffn_gate_dx", comm=rs_a(g_down))
    pre_down = halves(g_down, oth, "down")
    dh2b, _ = mm_nt(du, w_up, "ffn_up_dx")
    dw_gate, _ = mm_tn(h2, da, "ffn_gate_dw", shard_major=True)
    dw_up, _ = mm_tn(h2, du, "ffn_up_dw", shard_major=True)
    g_gu = [dw_gate, dw_up]
    (dx1, dp1, dya_att, dyb, cs_mid, dw_out, dw_bra, dw_brb), res = mid_bwd(
        dh2a, dh2b, x1, dx2, mix, mod8, g_ffn, p, ya, ybp, merged, ya_att, yb, w_o, w_bra, w_brb,
        comm=comm_join(rs_b(pre_down), rs_a(g_gu)))
    red_down = slab_sums(res[:1], "down")
    pre_gu = halves(g_gu, res[1:], "gu")
    g_mix3 = [dw_bra.reshape(512, 4, 256).transpose(1, 0, 2), dw_brb.reshape(256, 4, 256).transpose(1, 0, 2), dw_out.reshape(4, 256, D)]
    (dp2, dfrow, dfcol), res = fox_bwd(p, dya_att, ya_att, gcol, fcol, dp1,
                                       comm=comm_join(rs_b(pre_gu), rs_c(red_down), rs_a(g_mix3)))
    red_gu = slab_sums(res[:2], "gu")
    r_down = res[2]
    pre_mix3 = halves(g_mix3, res[3:], "mix")
    dp3, db_fg = fgate_bwd(dfrow.reshape(8, S), dfcol, fraw, dp2)
    dp4, res = dil_bwd(p, dyb, yb, lse_b, tabs, dp3, comm=comm_join(rs_c(red_gu), rs_b(pre_mix3)))
    r_gate, r_up = res[:2]
    red_mix3 = slab_sums(res[2:], "mix")
    dw_lay, (r_bra, r_brb, r_out) = mm_tn(h1, dp4, "in_proj_dw", comm=rs_c(red_mix3))
    g_in4 = [shards_from_lay(dw_lay)]
    dh1, oth = mm_nt(dp4, w_lay, "in_proj_dx", comm=rs_a(g_in4))
    pre_in = halves(g_in4, oth, "in")
    (dx, cs_in), res = in_bwd_tail(dh1, xs, dx1, mod8, g_mix, comm=rs_b(pre_in))
    (r_in,) = comm_only(rs_c(slab_sums(res, "in")), "rs_in_share")
    gpad = dict(w_in=r_in, w_br_a=r_bra, w_br_b=r_brb, w_out=r_out, w_ffn_gate=r_gate, w_ffn_up=r_up, w_ffn_down=r_down)

    dmod = jnp.concatenate([cs_in[0:2], cs_mid[3:4], cs_mid[0:2], dga_f], axis=0)
    small = dict(dmod=dmod, dg_mix=cs_in[2:3], dg_ffn=cs_mid[2:3], dg_final=dg_final, db_fgate=db_fg[:, 0], loss=loss_part[0, 0])
    sv = jnp.concatenate([
        small["dmod"].reshape(48, LANES), small["dg_mix"].reshape(8, LANES), small["dg_ffn"].reshape(8, LANES),
        small["dg_final"].reshape(8, LANES), jnp.pad(small["db_fgate"], (0, LANES - 8)).reshape(1, LANES),
        jnp.broadcast_to(small["loss"], (1, LANES)), jnp.zeros((SMALL_ROWS - 74, LANES), F32)], axis=0)
    sv_all, sv_sum = gather_all(sv, "gather_small", True)
    loss = sv_sum[73, 0]
    g_small = dict(b_ada=sv_sum[0:48].reshape(1, 6 * D), g_mix=sv_sum[48:56].reshape(1, D), g_ffn=sv_sum[56:64].reshape(1, D),
                   g_final=sv_sum[64:72].reshape(D), b_fgate=sv_sum[72, 0:8].reshape(1, 8))

    dmod_all = lax.dynamic_slice(sv_all[:, 0:48, :].reshape(8, 6 * D), (0, chip * n_ada), (8, n_ada))
    g_ada, d_ada, nm_ada, nv_ada = adam_w_ada(jnp.pad(sc.T, ((0, 0), (0, LANES - 8))), jnp.pad(dmod_all, ((0, LANES - 8), (0, 0))),
                                              w_ada[0], m_w_ada[0], v_w_ada[0])

    big = dict(w_in=(w_in, m_w_in, v_w_in), w_br_a=(w_br_a, m_w_br_a, v_w_br_a), w_br_b=(w_br_b, m_w_br_b, v_w_br_b),
               w_out=(w_out, m_w_out, v_w_out), w_ffn_gate=(w_ffn_gate, m_w_ffn_gate, v_w_ffn_gate),
               w_ffn_up=(w_ffn_up, m_w_ffn_up, v_w_ffn_up), w_ffn_down=(w_ffn_down, m_w_ffn_down, v_w_ffn_down))
    upd = {nm: adam(w[0], gpad[nm], m[0], v[0], "adam_" + nm) for nm, (w, m, v) in big.items()}

    def pack(gm, gf, gl, ba, bf):
        rows = [gm.reshape(1, D), gf.reshape(1, D), gl.reshape(1, D), ba.reshape(6, D), jnp.pad(bf.reshape(1, 8), ((0, 0), (0, D - 8)))]
        return jnp.concatenate(rows + [jnp.zeros((6, D), F32)], axis=0)

    packed = adam(pack(g_mix, g_ffn, g_final, b_ada, b_fgate),
                  pack(g_small["g_mix"], g_small["g_ffn"], g_small["g_final"], g_small["b_ada"], g_small["b_fgate"]),
                  pack(m_g_mix, m_g_ffn, m_g_final, m_b_ada, m_b_fgate), pack(v_g_mix, v_g_ffn, v_g_final, v_b_ada, v_b_fgate),
                  "adam_small")

    def unpack(t):
        return dict(g_mix=t[0:1], g_ffn=t[1:2], g_final=t[2], b_ada=t[3:9].reshape(1, 6 * D), b_fgate=t[9:10, 0:8])

    small_upd = [unpack(t) for t in packed[1:]]
    order =["w_ada", "b_ada", "g_mix", "w_in", "b_fgate", "w_br_a", "w_br_b", "w_out", "g_ffn", "w_ffn_gate", "w_ffn_up", "w_ffn_down", "g_final"]

    def leaf(nm, which):
        if nm == "w_ada":
            return (g_ada, d_ada, nm_ada, nv_ada)[which][None]
        if nm in big:
            return upd[nm][which][None]
        return g_small[nm] if which == 0 else small_upd[which - 1][nm]

    outs = [loss, dx[None]]
    for which in range(4):
        outs += [leaf(nm, which) for nm in order]
    return tuple(outs)
```

```python
import functools

import numpy as np
import jax
import jax.numpy as jnp
from jax import lax
from jax.experimental import pallas as pl
from jax.experimental.pallas import tpu as pltpu

F32, BF16 = jnp.float32, jnp.bfloat16
S, D = 2048, 1024
HD = 64
LANES = 128
N_FOX_PAIRS, N_DIL_PAIRS = 4, 2
DIL_GROUPS = ((1, 16), (4, 4), (16, 1))
SPAN = 128
ROT_DIM, ROPE_THETA = 16, 500000.0
D_FF, FF_SHARD, FF_PAD = 2816, 704, 768
FFP = 4 * FF_PAD
IN_COLS, IN_SHARD, IN_SHARD_PAD = 5896, 1474, 1536
LAY_B, LAY_A, LAY_F, LAY_G, LAY_N = 0, 2304, 3840, 4096, 6144
EPS, NEG = 1e-6, -1e30
SCALE = HD ** -0.5
ADAM_LR, ADAM_B1, ADAM_B2, ADAM_EPS, ADAM_WD, ADAM_STEP = 0.001, 0.9, 0.999, 1e-08, 0.01, 10
VMEM_MB = 56
MESH = pl.DeviceIdType.MESH


def _params(vmem_mb=None, **kw):
    if vmem_mb is not None:
        kw["vmem_limit_bytes"] = vmem_mb * 1024 * 1024
    return pltpu.CompilerParams(**kw)


def _sds(shape, dtype):
    return jax.ShapeDtypeStruct(shape, dtype)


def _sigmoid(x):
    return 1.0 / (1.0 + jnp.exp(-x))


def _colsum8(x):
    tm, n = x.shape
    return jnp.sum(x.reshape(tm // 8, 8, n), axis=0)


class Comm:
    def __init__(self, ins, out_shapes, sems, start, wait, aliases=None):
        self.ins, self.out_shapes, self.sems = list(ins), list(out_shapes), list(sems)
        self.start, self.wait, self.aliases = start, wait, dict(aliases or {})


def _hosted_call(body, comm, args, *, name, grid, in_specs, out_specs, out_shape, scratch_shapes=(), aliases=None, vmem_mb=None):
    single = not isinstance(out_shape, (list, tuple))
    out_specs_l = [out_specs] if single else list(out_specs)
    out_shape_l = [out_shape] if single else list(out_shape)
    n_in, n_out, n_scr = len(in_specs), len(out_shape_l), len(scratch_shapes)
    aliases = dict(aliases or {})
    if comm is None:
        res = pl.pallas_call(body, name=name, grid=grid, in_specs=list(in_specs), out_specs=out_specs, out_shape=out_shape,
                             scratch_shapes=list(scratch_shapes), input_output_aliases=aliases,
                             compiler_params=_params(vmem_mb))(*args)
        return res, []
    nci, nco = len(comm.ins), len(comm.out_shapes)

    def wrapped(*refs):
        main_in, cin = refs[:n_in], refs[n_in:n_in + nci]
        o0 = n_in + nci
        main_out, cout = refs[o0:o0 + n_out], refs[o0 + n_out:o0 + n_out + nco]
        s0 = o0 + n_out + nco
        scr, sems = refs[s0:s0 + n_scr], refs[s0 + n_scr:]
        ids = [pl.program_id(i) for i in range(len(grid))]
        first = functools.reduce(jnp.logical_and, [i == 0 for i in ids])
        last = functools.reduce(jnp.logical_and, [i == g - 1 for i, g in zip(ids, grid)])

        @pl.when(first)
        def _():
            comm.start(cin, cout, sems)

        body(*main_in, *main_out, *scr)

        @pl.when(last)
        def _():
            comm.wait(cin, cout, sems)

    for ci, co in comm.aliases.items():
        aliases[n_in + ci] = n_out + co
    any_spec = pl.BlockSpec(memory_space=pl.ANY)
    res = pl.pallas_call(
        wrapped, name=name, grid=grid, in_specs=list(in_specs) + [any_spec] * nci, out_specs=out_specs_l + [any_spec] * nco,
        out_shape=out_shape_l + comm.out_shapes,
        scratch_shapes=list(scratch_shapes) + [pltpu.SemaphoreType.DMA((s,)) for s in comm.sems],
        input_output_aliases=aliases, compiler_params=_params(vmem_mb))(*args, *comm.ins)
    main = list(res[:n_out])
    return (main[0] if single else main), list(res[n_out:])


def norm_mod_fwd(x, g, mod, shift_row, scale_row):
    tm = 256

    def body(x_ref, g_ref, mod_ref, h_ref):
        xv = x_ref[...]
        r = lax.rsqrt(jnp.mean(xv * xv, axis=1, keepdims=True) + EPS)
        n = xv * r * g_ref[...]
        h = n * (1.0 + mod_ref[scale_row:scale_row + 1, :]) + mod_ref[shift_row:shift_row + 1, :]
        h_ref[...] = h.astype(BF16)

    return pl.pallas_call(
        body, name="norm_mod_fwd", grid=(S // tm,),
        in_specs=[pl.BlockSpec((tm, D), lambda i: (i, 0)), pl.BlockSpec((1, D), lambda i: (0, 0)),
                  pl.BlockSpec((8, D), lambda i: (0, 0))],
        out_specs=pl.BlockSpec((tm, D), lambda i: (i, 0)),
        out_shape=_sds((S, D), BF16),
    )(x, g, mod)


def rope_tables():
    pos = jnp.arange(S, dtype=F32)
    inv_freq = ROPE_THETA ** (-jnp.arange(0, ROT_DIM, 2, dtype=F32) / ROT_DIM)
    ang = pos[:, None] * inv_freq[None, :]
    cos, sin = jnp.cos(ang), jnp.sin(ang)
    one, zero = jnp.ones((S, HD - ROT_DIM), F32), jnp.zeros((S, HD - ROT_DIM), F32)
    z8 = jnp.zeros((S, 8), F32)
    c = jnp.concatenate([cos, cos, one], axis=1)
    s1 = jnp.concatenate([-sin, z8, zero], axis=1)
    s2 = jnp.concatenate([z8, sin, zero], axis=1)
    return tuple(jnp.concatenate([t, t], axis=1) for t in (c, s1, s2))


def _rope(y, c, s1, s2):
    return y * c + pltpu.roll(y, LANES - 8, 1) * s1 + pltpu.roll(y, 8, 1) * s2


def _rope_bwd(dy, c, s1, s2):
    return dy * c + pltpu.roll(dy * s1, 8, 1) + pltpu.roll(dy * s2, LANES - 8, 1)


def in_proj_fwd(h, w_lay, tabs, comm=None):
    tm, tn = 2048, 768
    n_rope = N_DIL_PAIRS * 3 // 2

    def body(a_ref, w_ref, c_ref, s1_ref, s2_ref, o_ref):
        j = pl.program_id(0)
        y = jnp.dot(a_ref[...], w_ref[...], preferred_element_type=F32)

        @pl.when(j < n_rope)
        def _():
            c, s1, s2 = c_ref[...], s1_ref[...], s2_ref[...]
            for t in range(tn // LANES):
                chunk = y[:, LANES * t:LANES * (t + 1)]
                o_ref[:, LANES * t:LANES * (t + 1)] = chunk if t % 3 == 2 else _rope(chunk, c, s1, s2)

        @pl.when(j >= n_rope)
        def _():
            o_ref[...] = y

    tab = pl.BlockSpec((tm, LANES), lambda j, i: (i, 0))
    return _hosted_call(
        body, comm, (h, w_lay, *tabs), name="in_proj_fwd", grid=(LAY_N // tn, S // tm),
        in_specs=[pl.BlockSpec((tm, D), lambda j, i: (i, 0)), pl.BlockSpec((D, tn), lambda j, i: (0, j)), tab, tab, tab],
        out_specs=pl.BlockSpec((tm, tn), lambda j, i: (i, j)),
        out_shape=_sds((S, LAY_N), F32), vmem_mb=VMEM_MB)


def _log1p_small(t):
    return jnp.where(t < 1e-2, t * (1.0 - t * (0.5 - t * (1.0 / 3.0))), jnp.log(1.0 + t))


def fgate_fwd(p, b_pad):
    def body(fa_ref, b_ref, frow_ref, fraw_ref, fcol_ref):
        f = fa_ref[...] + b_ref[...]
        fr = f.T[0:8, :]
        ls = jnp.minimum(fr, 0.0) - _log1p_small(jnp.exp(-jnp.abs(fr)))
        lane = lax.broadcasted_iota(jnp.int32, (8, S), 1)
        acc, sh = ls, 1
        while sh < S:
            acc = acc + jnp.where(lane >= sh, pltpu.roll(acc, sh, 1), 0.0)
            sh *= 2
        frow_ref[...] = acc
        fraw_ref[...] = fr
        for hh in range(8):
            fcol_ref[hh] = jnp.broadcast_to(acc[hh:hh + 1, :], (LANES, S)).T

    return pl.pallas_call(
        body, name="fgate_fwd", grid=(1,),
        in_specs=[pl.BlockSpec((S, LANES), lambda i: (0, LAY_F // LANES)), pl.BlockSpec((1, LANES), lambda i: (0, 0))],
        out_specs=[pl.BlockSpec((8, S), lambda i: (0, 0)), pl.BlockSpec((8, S), lambda i: (0, 0)),
                   pl.BlockSpec((8, S, LANES), lambda i: (0, 0, 0))],
        out_shape=[_sds((8, S), F32), _sds((8, S), F32), _sds((8, S, LANES), F32)],
        compiler_params=_params(VMEM_MB),
    )(p, b_pad)


def _head_masks(rows):
    lane = lax.broadcasted_iota(jnp.int32, (rows, LANES), 1)
    return lane < HD, lane >= HD


FT = 256


def _split3(f):
    hi = f.astype(BF16).astype(F32)
    r = f - hi
    mid = r.astype(BF16).astype(F32)
    return hi, mid, r - mid


def _fox_operands(qkv_ref, tcol_ref, scol_ref, qa_s, ka_s):
    rows = 256
    lane = lax.broadcasted_iota(jnp.int32, (rows, LANES), 1)

    def chunk(i, _):
        r = pl.ds(pl.multiple_of(i * rows, rows), rows)
        q, k = qkv_ref[r, 0:LANES], qkv_ref[r, LANES:2 * LANES]
        s0, s1 = _split3(scol_ref[0, r, :]), _split3(scol_ref[1, r, :])
        ka = jnp.where(lane == 0, -s0[0], jnp.where(lane == 1, -s0[1], jnp.where(lane == 2, -s0[2], jnp.where(
            lane == 3, -s1[0], jnp.where(lane == 4, -s1[1], jnp.where(lane == 5, -s1[2], jnp.where(lane < 9, 1.0, 0.0)))))))
        ka_s[r, 0:LANES] = k.astype(BF16)
        ka_s[r, LANES:2 * LANES] = ka.astype(BF16)
        for hh in range(2):
            own = (lane < HD) if hh == 0 else (lane >= HD)
            t3 = _split3(tcol_ref[hh, r, :])
            ones = (lane >= 3 * hh) & (lane < 3 * hh + 3)
            qa = jnp.where(ones, 1.0, jnp.where(lane == 6, t3[0], jnp.where(lane == 7, t3[1], jnp.where(lane == 8, t3[2], 0.0))))
            qa_s[hh, r, 0:LANES] = jnp.where(own, q * SCALE, 0.0).astype(BF16)
            qa_s[hh, r, LANES:2 * LANES] = qa.astype(BF16)
        return 0

    lax.fori_loop(0, S // rows, chunk, 0)


def fox_fwd(p, fcol, comm=None):
    nt = (((1,), (1,)), ((), ()))

    def body(qkv_ref, fc_ref, o_ref, g_ref, qa_s, ka_s):
        _fox_operands(qkv_ref, fc_ref, fc_ref, qa_s, ka_s)
        masks = _head_masks(FT)
        causal = lax.broadcasted_iota(jnp.int32, (FT, FT), 1) <= lax.broadcasted_iota(jnp.int32, (FT, FT), 0)
        causal2 = jnp.concatenate([causal, causal], axis=0)

        def qloop(qi, _):
            q0 = pl.multiple_of(qi * FT, FT)
            qa = jnp.concatenate([qa_s[0, pl.ds(q0, FT), :], qa_s[1, pl.ds(q0, FT), :]], axis=0)

            def step(kb, carry, diagonal):
                m, l, acc = carry
                k0 = pl.multiple_of(kb * FT, FT)
                v = qkv_ref[pl.ds(k0, FT), 2 * LANES:3 * LANES].astype(BF16)
                s = lax.dot_general(qa, ka_s[pl.ds(k0, FT), :], nt, preferred_element_type=F32)
                if diagonal:
                    s = jnp.where(causal2, s, NEG)
                m_new = jnp.maximum(m, jnp.max(s, axis=1, keepdims=True))
                pr = jnp.exp(s - m_new)
                alpha = jnp.exp(m - m_new)
                return (m_new, l * alpha + jnp.sum(pr, axis=1, keepdims=True),
                        acc * alpha + jnp.dot(pr.astype(BF16), v, preferred_element_type=F32))

            init = (jnp.full((2 * FT, 1), NEG, F32), jnp.zeros((2 * FT, 1), F32), jnp.zeros((2 * FT, LANES), F32))
            carry = lax.fori_loop(0, qi, lambda kb, cr: step(kb, cr, False), init)
            m, l, acc = step(qi, carry, True)
            out = acc / l
            lse = m + jnp.log(l)
            o_ref[pl.ds(q0, FT), :] = jnp.where(masks[0], out[:FT], out[FT:]).astype(BF16)
            g_ref[0, pl.ds(q0, FT), :] = fc_ref[0, pl.ds(q0, FT), :] - lse[:FT]
            g_ref[1, pl.ds(q0, FT), :] = fc_ref[1, pl.ds(q0, FT), :] - lse[FT:]
            return 0

        lax.fori_loop(0, S // FT, qloop, 0)

    a_blk = LAY_A // 384
    return _hosted_call(
        body, comm, (p, fcol), name="fox_fwd", grid=(N_FOX_PAIRS,),
        in_specs=[pl.BlockSpec((S, 384), lambda p_: (0, a_blk + p_)), pl.BlockSpec((2, S, LANES), lambda p_: (p_, 0, 0))],
        out_specs=[pl.BlockSpec((S, LANES), lambda p_: (0, p_)), pl.BlockSpec((2, S, LANES), lambda p_: (p_, 0, 0))],
        out_shape=[_sds((S, 4 * LANES), BF16), _sds((8, S, LANES), F32)],
        scratch_shapes=[pltpu.VMEM((2, S, 2 * LANES), BF16), pltpu.VMEM((S, 2 * LANES), BF16)],
        vmem_mb=VMEM_MB)


def _dil_rows(ref, start, d):
    return ref[pl.ds(start, SPAN), :] if d == 1 else ref[pl.ds(start, SPAN, stride=d), :]


def _dil_store(ref, start, d, val):
    if d == 1:
        ref[pl.ds(start, SPAN), :] = val
    else:
        ref[pl.ds(start, SPAN, stride=d), :] = val


def _band_mask(has_prev):
    qi = lax.broadcasted_iota(jnp.int32, (SPAN, 2 * SPAN), 0) + SPAN
    kj = lax.broadcasted_iota(jnp.int32, (SPAN, 2 * SPAN), 1)
    dist = qi - kj
    return (dist >= 0) & (dist <= SPAN) & (has_prev | (kj >= SPAN))


def _dil_block(n, d, nb):
    r, j = n // nb, n % nb
    start = r + d * SPAN * j
    prev = jnp.maximum(start - d * SPAN, r)
    return start, prev, j > 0


def dil_fwd(p, comm=None):
    def body(*refs):
        qkv = [refs[3 * g:3 * g + 3] for g in range(3)]
        y_ref, lse_ref = refs[9], refs[10]
        acc_s, m_s, l_s = refs[11], refs[12], refs[13]
        masks = _head_masks(SPAN)
        for g, (d, nb) in enumerate(DIL_GROUPS):
            q_ref, k_ref, v_ref = qkv[g]

            def blk(n, _):
                start, prev, has_prev = _dil_block(n, d, nb)
                q = _dil_rows(q_ref, start, d)
                kc = jnp.concatenate([_dil_rows(k_ref, prev, d), _dil_rows(k_ref, start, d)], axis=0).astype(BF16)
                vc = jnp.concatenate([_dil_rows(v_ref, prev, d), _dil_rows(v_ref, start, d)], axis=0).astype(BF16)
                valid = _band_mask(has_prev)
                valid2 = jnp.concatenate([valid, valid], axis=0)
                q2 = (jnp.concatenate([jnp.where(masks[0], q, 0.0), jnp.where(masks[1], q, 0.0)], axis=0) * SCALE).astype(BF16)
                s = jnp.where(valid2, lax.dot_general(q2, kc, (((1,), (1,)), ((), ())), preferred_element_type=F32), NEG)
                m = jnp.max(s, axis=1, keepdims=True)
                pr = jnp.exp(s - m)
                l = jnp.sum(pr, axis=1, keepdims=True)
                acc = jnp.dot(pr.astype(BF16), vc, preferred_element_type=F32)
                _dil_store(acc_s.at[g], start, d, jnp.where(masks[0], acc[:SPAN], acc[SPAN:]))
                _dil_store(m_s.at[g], start, d, jnp.where(masks[0], m[:SPAN], m[SPAN:]))
                _dil_store(l_s.at[g], start, d, jnp.where(masks[0], l[:SPAN], l[SPAN:]))
                return 0

            lax.fori_loop(0, 16, blk, 0)

        def merge(i, _):
            rows = pl.ds(pl.multiple_of(i * 256, 256), 256)
            m = [m_s[g, rows, :] for g in range(3)]
            mx = jnp.maximum(jnp.maximum(m[0], m[1]), m[2])
            w = [jnp.exp(m[g] - mx) for g in range(3)]
            l = sum(l_s[g, rows, :] * w[g] for g in range(3))
            y_ref[rows, :] = sum(acc_s[g, rows, :] * w[g] for g in range(3)) / l
            lse_ref[rows, :] = mx + jnp.log(l)
            return 0

        lax.fori_loop(0, S // 256, merge, 0)

    def spec(g, t):
        return pl.BlockSpec((S, LANES), lambda p_: (0, (p_ * 3 + g) * 3 + t))

    return _hosted_call(
        body, comm, [p] * 9, name="dil_fwd", grid=(N_DIL_PAIRS,),
        in_specs=[spec(g, t) for g in range(3) for t in range(3)],
        out_specs=[pl.BlockSpec((S, LANES), lambda p_: (0, p_)), pl.BlockSpec((S, LANES), lambda p_: (0, p_))],
        out_shape=[_sds((S, 2 * LANES), F32), _sds((S, 2 * LANES), F32)],
        scratch_shapes=[pltpu.VMEM((3, S, LANES), F32)] * 3,
        vmem_mb=VMEM_MB)


def merge_fwd(ya_att, yb, p, w_bra, w_brb, comm=None):
    tm = 256
    gblk = LAY_G // D

    def body(a_ref, b_ref, ga_ref, gb_ref, wa_ref, wb_ref, mg_ref, ya_ref, yb_ref):
        ya = jnp.dot(a_ref[...], wa_ref[...], preferred_element_type=F32)
        ybp = jnp.dot(b_ref[...].astype(BF16), wb_ref[...], preferred_element_type=F32)
        mg_ref[...] = (_sigmoid(ga_ref[...]) * ya + _sigmoid(gb_ref[...]) * ybp).astype(BF16)
        ya_ref[...] = ya
        yb_ref[...] = ybp

    row = lambda w: pl.BlockSpec((tm, w), lambda i: (i, 0))
    return _hosted_call(
        body, comm, (ya_att, yb, p, p, w_bra, w_brb), name="merge_fwd", grid=(S // tm,),
        in_specs=[row(512), row(256), pl.BlockSpec((tm, D), lambda i: (i, gblk)), pl.BlockSpec((tm, D), lambda i: (i, gblk + 1)),
                  pl.BlockSpec((512, D), lambda i: (0, 0)), pl.BlockSpec((256, D), lambda i: (0, 0))],
        out_specs=[row(D), row(D), row(D)],
        out_shape=[_sds((S, D), BF16), _sds((S, D), F32), _sds((S, D), F32)])


def out_proj_fwd(merged, w_out, x, mod, g_ffn):
    tm = 256

    def body(a_ref, w_ref, x_ref, mod_ref, g_ref, mix_ref, x1_ref, h2_ref):
        mix = jnp.dot(a_ref[...], w_ref[...], preferred_element_type=F32)
        x1 = x_ref[...] + mod_ref[2:3, :] * mix
        r = lax.rsqrt(jnp.mean(x1 * x1, axis=1, keepdims=True) + EPS)
        h2 = (x1 * r * g_ref[...]) * (1.0 + mod_ref[4:5, :]) + mod_ref[3:4, :]
        mix_ref[...] = mix
        x1_ref[...] = x1
        h2_ref[...] = h2.astype(BF16)

    row = pl.BlockSpec((tm, D), lambda i: (i, 0))
    return pl.pallas_call(
        body, name="out_proj_fwd", grid=(S // tm,),
        in_specs=[row, pl.BlockSpec((D, D), lambda i: (0, 0)), row, pl.BlockSpec((8, D), lambda i: (0, 0)),
                  pl.BlockSpec((1, D), lambda i: (0, 0))],
        out_specs=[row, row, row],
        out_shape=[_sds((S, D), F32), _sds((S, D), F32), _sds((S, D), BF16)],
    )(merged, w_out, x, mod, g_ffn)


def ffn_up_fwd(h2, w_gate, w_up):
    tm = 1024

    def body(h_ref, wg_ref, wu_ref, a_ref, u_ref, z_ref):
        h = h_ref[...]
        a = jnp.dot(h, wg_ref[...], preferred_element_type=F32)
        u = jnp.dot(h, wu_ref[...], preferred_element_type=F32)
        a_ref[...] = a
        u_ref[...] = u
        z_ref[...] = (a * _sigmoid(a) * u).astype(BF16)

    out = pl.BlockSpec((tm, FF_PAD), lambda k, i: (i, k))
    return pl.pallas_call(
        body, name="ffn_up_fwd", grid=(4, S // tm),
        in_specs=[pl.BlockSpec((tm, D), lambda k, i: (i, 0)), pl.BlockSpec((None, D, FF_PAD), lambda k, i: (k, 0, 0)),
                  pl.BlockSpec((None, D, FF_PAD), lambda k, i: (k, 0, 0))],
        out_specs=[out, out, out],
        out_shape=[_sds((S, FFP), F32), _sds((S, FFP), F32), _sds((S, FFP), BF16)], compiler_params=_params(VMEM_MB),
    )(h2, w_gate, w_up)


def ffn_down_loss(z, w_down, x1, mod, g_final, tgt):
    tm = 256

    def body(z_ref, w_ref, x1_ref, mod_ref, g_ref, t_ref, dx2_ref, dffn_ref, dg_ref, dga_ref, loss_ref, s_dg, s_dga, s_loss):
        i = pl.program_id(0)

        @pl.when(i == 0)
        def _():
            s_dg[...] = jnp.zeros_like(s_dg)
            s_dga[...] = jnp.zeros_like(s_dga)
            s_loss[...] = jnp.zeros_like(s_loss)

        ffn = jnp.dot(z_ref[...], w_ref[...], preferred_element_type=F32)
        gaf = mod_ref[5:6, :]
        x2 = x1_ref[...] + gaf * ffn
        r = lax.rsqrt(jnp.mean(x2 * x2, axis=1, keepdims=True) + EPS)
        xh = x2 * r
        g = g_ref[...]
        e = xh * g - t_ref[...]
        s_loss[...] += 0.5 * jnp.sum(jnp.mean(e * e, axis=1, keepdims=True), axis=0, keepdims=True)
        dy = e * (1.0 / D)
        gdy = dy * g
        dx2 = r * (gdy - xh * jnp.mean(gdy * xh, axis=1, keepdims=True))
        s_dg[...] += _colsum8(dy * xh)
        s_dga[...] += _colsum8(dx2 * ffn)
        dx2_ref[...] = dx2
        dffn_ref[...] = (dx2 * gaf).astype(BF16)

        @pl.when(i == pl.num_programs(0) - 1)
        def _():
            dg_ref[...] = jnp.sum(s_dg[...], axis=0, keepdims=True)
            dga_ref[...] = jnp.sum(s_dga[...], axis=0, keepdims=True)
            loss_ref[...] = jnp.broadcast_to(s_loss[...], (1, LANES))

    row = pl.BlockSpec((tm, D), lambda i: (i, 0))
    vec = pl.BlockSpec((1, D), lambda i: (0, 0))
    return pl.pallas_call(
        body, name="ffn_down_loss", grid=(S // tm,),
        in_specs=[pl.BlockSpec((tm, FFP), lambda i: (i, 0)), pl.BlockSpec((FFP, D), lambda i: (0, 0)), row,
                  pl.BlockSpec((8, D), lambda i: (0, 0)), vec, row],
        out_specs=[row, row, vec, vec, pl.BlockSpec((1, LANES), lambda i: (0, 0))],
        out_shape=[_sds((S, D), F32), _sds((S, D), BF16), _sds((1, D), F32), _sds((1, D), F32), _sds((1, LANES), F32)],
        scratch_shapes=[pltpu.VMEM((8, D), F32), pltpu.VMEM((8, D), F32), pltpu.VMEM((1, 1), F32)],
        compiler_params=_params(VMEM_MB),
    )(z, w_down, x1, mod, g_final, tgt)


def ffn_down_bwd(dffn, w_down, a, u, z):
    tm, tn = 1024, 768

    def body(d_ref, w_ref, a_ref, u_ref, z_ref, da_ref, du_ref, dw_ref):
        i = pl.program_id(1)
        dff = d_ref[...]
        dz = lax.dot_general(dff, w_ref[...], (((1,), (1,)), ((), ())), preferred_element_type=F32)
        av, uv = a_ref[...], u_ref[...]
        sg = _sigmoid(av)
        du_ref[...] = (dz * (av * sg)).astype(BF16)
        da_ref[...] = (dz * uv * (sg * (1.0 + av * (1.0 - sg)))).astype(BF16)
        dw = lax.dot_general(z_ref[...], dff, (((0,), (0,)), ((), ())), preferred_element_type=F32)

        @pl.when(i == 0)
        def _():
            dw_ref[...] = dw

        @pl.when(i > 0)
        def _():
            dw_ref[...] += dw

    tile = pl.BlockSpec((tm, tn), lambda j, i: (i, j))
    return pl.pallas_call(
        body, name="ffn_down_bwd", grid=(FFP // tn, S // tm),
        in_specs=[pl.BlockSpec((tm, D), lambda j, i: (i, 0)), pl.BlockSpec((tn, D), lambda j, i: (j, 0)), tile, tile, tile],
        out_specs=[tile, tile, pl.BlockSpec((tn, D), lambda j, i: (j, 0))],
        out_shape=[_sds((S, FFP), BF16), _sds((S, FFP), BF16), _sds((FFP, D), F32)], compiler_params=_params(VMEM_MB),
    )(dffn, w_down, a, u, z)


def mm_nt(dy, w, name, comm=None):
    tm = 1024
    n = dy.shape[1]
    if w.ndim == 2:
        k_in, tk = w.shape[0], 768
        w_spec = pl.BlockSpec((k_in, tk), lambda i, k: (0, k))
    else:
        k_in, tk = w.shape[1], FF_PAD
        w_spec = pl.BlockSpec((None, k_in, tk), lambda i, k: (k, 0, 0))
    nk = n // tk

    def body(d_ref, w_ref, o_ref, acc):
        k = pl.program_id(1)
        part = lax.dot_general(d_ref[...], w_ref[...], (((1,), (1,)), ((), ())), preferred_element_type=F32)

        @pl.when(k == 0)
        def _():
            acc[...] = part

        @pl.when(k > 0)
        def _():
            acc[...] += part

        @pl.when(k == nk - 1)
        def _():
            o_ref[...] = acc[...]

    return _hosted_call(
        body, comm, (dy, w), name=name, grid=(S // tm, nk),
        in_specs=[pl.BlockSpec((tm, tk), lambda i, k: (i, k)), w_spec],
        out_specs=pl.BlockSpec((tm, k_in), lambda i, k: (i, 0)),
        out_shape=_sds((S, k_in), F32),
        scratch_shapes=[pltpu.VMEM((tm, k_in), F32)], vmem_mb=VMEM_MB)


def mm_tn(h, dy, name, shard_major=False, comm=None):
    tm, tn = 2048, 768
    k_in, n = h.shape[1], dy.shape[1]

    def body(h_ref, d_ref, o_ref):
        i = pl.program_id(1)
        dw = lax.dot_general(h_ref[...], d_ref[...], (((0,), (0,)), ((), ())), preferred_element_type=F32)

        @pl.when(i == 0)
        def _():
            o_ref[...] = dw

        @pl.when(i > 0)
        def _():
            o_ref[...] += dw

    if shard_major:
        out_spec, out_shape = pl.BlockSpec((None, k_in, tn), lambda j, i: (j, 0, 0)), _sds((n // tn, k_in, tn), F32)
    else:
        out_spec, out_shape = pl.BlockSpec((k_in, tn), lambda j, i: (0, j)), _sds((k_in, n), F32)
    return _hosted_call(
        body, comm, (h, dy), name=name, grid=(n // tn, S // tm),
        in_specs=[pl.BlockSpec((tm, k_in), lambda j, i: (i, 0)), pl.BlockSpec((tm, tn), lambda j, i: (i, j))],
        out_specs=out_spec, out_shape=out_shape, vmem_mb=VMEM_MB)


def mid_bwd(dh2a, dh2b, x1, dx2, mix, mod, g_ffn, p, ya, ybp, merged, ya_att, yb, w_out, w_bra, w_brb, comm=None):
    tm = 256
    gblk = LAY_G // D
    nsteps = S // tm

    def body(dha_ref, dhb_ref, x1_ref, dx2_ref, mix_ref, mod_ref, g_ref, ga_ref, gb_ref, ya_ref, yb_ref, mg_ref,
             att_ref, ybb_ref, wo_ref, wa_ref, wb_ref,
             dx1_ref, dpg_ref, datt_ref, dyb_ref, cs_ref, dwo_ref, dwa_ref, dwb_ref, s_cs):
        i = pl.program_id(0)

        @pl.when(i == 0)
        def _():
            s_cs[...] = jnp.zeros_like(s_cs)
            dwo_ref[...] = jnp.zeros_like(dwo_ref)
            dwa_ref[...] = jnp.zeros_like(dwa_ref)
            dwb_ref[...] = jnp.zeros_like(dwb_ref)

        x1 = x1_ref[...]
        g = g_ref[...]
        r = lax.rsqrt(jnp.mean(x1 * x1, axis=1, keepdims=True) + EPS)
        xh = x1 * r
        dh2 = dha_ref[...] + dhb_ref[...]
        s_cs[0] += _colsum8(dh2)
        s_cs[1] += _colsum8(dh2 * (xh * g))
        dn2 = dh2 * (1.0 + mod_ref[4:5, :])
        s_cs[2] += _colsum8(dn2 * xh)
        gd = dn2 * g
        dx1 = dx2_ref[...] + r * (gd - xh * jnp.mean(gd * xh, axis=1, keepdims=True))
        s_cs[3] += _colsum8(dx1 * mix_ref[...])
        dx1_ref[...] = dx1
        dmix = (dx1 * mod_ref[2:3, :]).astype(BF16)
        dmg = lax.dot_general(dmix, wo_ref[...], (((1,), (1,)), ((), ())), preferred_element_type=F32)
        sga, sgb = _sigmoid(ga_ref[...]), _sigmoid(gb_ref[...])
        dya = (dmg * sga).astype(BF16)
        dybp = (dmg * sgb).astype(BF16)
        dpg_ref[:, 0:D] = (dmg * ya_ref[...] * (sga * (1.0 - sga))).astype(BF16)
        dpg_ref[:, D:2 * D] = (dmg * yb_ref[...] * (sgb * (1.0 - sgb))).astype(BF16)
        datt_ref[...] = lax.dot_general(dya, wa_ref[...], (((1,), (1,)), ((), ())), preferred_element_type=F32).astype(BF16)
        dyb_ref[...] = lax.dot_general(dybp, wb_ref[...], (((1,), (1,)), ((), ())), preferred_element_type=F32)
        tn_dims = (((0,), (0,)), ((), ()))
        dwo_ref[...] += lax.dot_general(mg_ref[...], dmix, tn_dims, preferred_element_type=F32)
        dwa_ref[...] += lax.dot_general(att_ref[...], dya, tn_dims, preferred_element_type=F32)
        dwb_ref[...] += lax.dot_general(ybb_ref[...].astype(BF16), dybp, tn_dims, preferred_element_type=F32)

        @pl.when(i == nsteps - 1)
        def _():
            for t in range(4):
                cs_ref[t:t + 1, :] = jnp.sum(s_cs[t], axis=0, keepdims=True)
            cs_ref[4:8, :] = jnp.zeros((4, D), F32)

    row = lambda w: pl.BlockSpec((tm, w), lambda i: (i, 0))
    full = lambda a, b: pl.BlockSpec((a, b), lambda i: (0, 0))
    return _hosted_call(
        body, comm, (dh2a, dh2b, x1, dx2, mix, mod, g_ffn, p, p, ya, ybp, merged, ya_att, yb, w_out, w_bra, w_brb),
        name="mid_bwd", grid=(nsteps,),
        in_specs=[row(D), row(D), row(D), row(D), row(D), full(8, D), full(1, D),
                  pl.BlockSpec((tm, D), lambda i: (i, gblk)), pl.BlockSpec((tm, D), lambda i: (i, gblk + 1)),
                  row(D), row(D), row(D), row(512), row(256), full(D, D), full(512, D), full(256, D)],
        out_specs=[row(D), pl.BlockSpec((tm, 2 * D), lambda i: (i, LAY_G // (2 * D))), row(512), row(256), full(8, D),
                   full(D, D), full(512, D), full(256, D)],
        out_shape=[_sds((S, D), F32), _sds((S, LAY_N), BF16), _sds((S, 512), BF16), _sds((S, 256), F32), _sds((8, D), F32),
                   _sds((D, D), F32), _sds((512, D), F32), _sds((256, D), F32)],
        scratch_shapes=[pltpu.VMEM((4, 8, D), F32)],
        vmem_mb=VMEM_MB)


def fox_bwd(p, do, o, gcol, fcol, dp, comm=None):
    nq = S // FT
    nt = (((1,), (1,)), ((), ()))
    tn = (((0,), (0,)), ((), ()))

    def body(qkv_ref, do_ref, o_ref, g_ref, fc_ref, dp_in, dp_ref, df_ref, rs_ref, dq_s, qa_s, ka_s, dob_s, dl_s):
        del dp_in
        _fox_operands(qkv_ref, g_ref, fc_ref, qa_s, ka_s)
        masks = _head_masks(FT)
        lane = lax.broadcasted_iota(jnp.int32, (FT, LANES), 1)
        head0 = 2 * pl.program_id(0)
        causal = lax.broadcasted_iota(jnp.int32, (FT, FT), 1) <= lax.broadcasted_iota(jnp.int32, (FT, FT), 0)
        dq_s[...] = jnp.zeros_like(dq_s)
        rs_ref[...] = jnp.zeros_like(rs_ref)

        causal2 = jnp.concatenate([causal, causal], axis=0)

        def prep(i, _):
            r = pl.ds(pl.multiple_of(i * 256, 256), 256)
            m256 = _head_masks(256)
            dov, ov = do_ref[r, :].astype(F32), o_ref[r, :].astype(F32)
            for hh in range(2):
                dom = jnp.where(m256[hh], dov, 0.0)
                dob_s[hh, r, :] = dom.astype(BF16)
                dl_s[hh, r, :] = jnp.broadcast_to(jnp.sum(dom * ov, axis=1, keepdims=True), (256, LANES))
            return 0

        lax.fori_loop(0, S // 256, prep, 0)

        def stack(ref, q0, cols=slice(None)):
            return jnp.concatenate([ref[0, pl.ds(q0, FT), cols], ref[1, pl.ds(q0, FT), cols]], axis=0)

        def kloop(kb, _):
            k0 = pl.multiple_of(kb * FT, FT)
            k = qkv_ref[pl.ds(k0, FT), LANES:2 * LANES].astype(BF16)
            v = qkv_ref[pl.ds(k0, FT), 2 * LANES:3 * LANES].astype(BF16)
            ka = ka_s[pl.ds(k0, FT), :]

            def step(qi, carry, diagonal):
                dk, dv, df0, df1 = carry
                q0 = pl.multiple_of(qi * FT, FT)
                qa, dob = stack(qa_s, q0), stack(dob_s, q0)
                s = lax.dot_general(qa, ka, nt, preferred_element_type=F32)
                pr = jnp.exp(jnp.where(causal2, s, NEG)) if diagonal else jnp.exp(s)
                dpr = lax.dot_general(dob, v, nt, preferred_element_type=F32)
                ds = pr * (dpr - jnp.tile(stack(dl_s, q0), (1, FT // LANES)))
                dsb = ds.astype(BF16)
                dq = jnp.dot(dsb, k, preferred_element_type=F32) * SCALE
                dk = dk + lax.dot_general(dsb, qa[:, 0:LANES], tn, preferred_element_type=F32)
                dv = dv + lax.dot_general(pr.astype(BF16), dob, tn, preferred_element_type=F32)
                rsum = jnp.sum(ds, axis=1, keepdims=True)
                dq_s[pl.ds(q0, FT), :] += jnp.where(masks[0], dq[:FT], dq[FT:])
                rs_ref[pl.ds(q0, FT), :] += jnp.where(lane == head0, rsum[:FT], 0.0) + jnp.where(lane == head0 + 1, rsum[FT:], 0.0)
                return (dk, dv, df0 - jnp.sum(ds[:FT], axis=0, keepdims=True), df1 - jnp.sum(ds[FT:], axis=0, keepdims=True))

            z = jnp.zeros((FT, LANES), F32)
            z1 = jnp.zeros((1, FT), F32)
            carry = step(kb, (z, z, z1, z1), True)
            dk, dv, df0, df1 = lax.fori_loop(kb + 1, nq, lambda qi, cr: step(qi, cr, False), carry)
            dp_ref[pl.ds(k0, FT), LANES:2 * LANES] = dk.astype(BF16)
            dp_ref[pl.ds(k0, FT), 2 * LANES:3 * LANES] = dv.astype(BF16)
            df_ref[0:1, pl.ds(k0, FT)] = df0
            df_ref[1:2, pl.ds(k0, FT)] = df1
            return 0

        lax.fori_loop(0, S // FT, kloop, 0)
        dp_ref[:, 0:LANES] = dq_s[...].astype(BF16)

    a_blk = LAY_A // 384
    pair = pl.BlockSpec((S, LANES), lambda p_: (0, p_))
    heads = pl.BlockSpec((2, S, LANES), lambda p_: (p_, 0, 0))
    return _hosted_call(
        body, comm, (p, do, o, gcol, fcol, dp), name="fox_bwd", grid=(N_FOX_PAIRS,),
        in_specs=[pl.BlockSpec((S, 384), lambda p_: (0, a_blk + p_)), pair, pair, heads, heads, pl.BlockSpec(memory_space=pl.ANY)],
        out_specs=[pl.BlockSpec((S, 384), lambda p_: (0, a_blk + p_)), pl.BlockSpec((None, 2, S), lambda p_: (p_, 0, 0)),
                   pl.BlockSpec((None, S, LANES), lambda p_: (p_, 0, 0))],
        out_shape=[_sds((S, LAY_N), BF16), _sds((4, 2, S), F32), _sds((4, S, LANES), F32)],
        scratch_shapes=[pltpu.VMEM((S, LANES), F32), pltpu.VMEM((2, S, 2 * LANES), BF16), pltpu.VMEM((S, 2 * LANES), BF16),
                        pltpu.VMEM((2, S, LANES), BF16), pltpu.VMEM((2, S, LANES), F32)],
        aliases={5: 0}, vmem_mb=VMEM_MB)


def fgate_bwd(dfrow, dfcol, fraw, dp):
    def body(df_ref, dc_ref, f_ref, dp_in, dpf_ref, db_ref):
        del dp_in
        lane = lax.broadcasted_iota(jnp.int32, (8, S), 1)
        rsum = (dc_ref[0] + dc_ref[1]) + (dc_ref[2] + dc_ref[3])
        acc, sh = df_ref[...] + rsum.T[0:8, :], 1
        while sh < S:
            acc = acc + jnp.where(lane < S - sh, pltpu.roll(acc, S - sh, 1), 0.0)
            sh *= 2
        df = acc * _sigmoid(-f_ref[...])
        db_ref[...] = jnp.broadcast_to(jnp.sum(df, axis=1, keepdims=True), (8, LANES))
        dfc = jnp.concatenate([df, jnp.zeros((LANES - 8, S), F32)], axis=0).T
        dpf_ref[:, 0:LANES] = dfc.astype(BF16)
        dpf_ref[:, LANES:2 * LANES] = jnp.zeros((S, LANES), BF16)

    return pl.pallas_call(
        body, name="fgate_bwd", grid=(1,),
        in_specs=[pl.BlockSpec((8, S), lambda i: (0, 0)), pl.BlockSpec((4, S, LANES), lambda i: (0, 0, 0)),
                  pl.BlockSpec((8, S), lambda i: (0, 0)), pl.BlockSpec(memory_space=pl.ANY)],
        out_specs=[pl.BlockSpec((S, 2 * LANES), lambda i: (0, LAY_F // (2 * LANES))), pl.BlockSpec((8, LANES), lambda i: (0, 0))],
        out_shape=[_sds((S, LAY_N), BF16), _sds((8, LANES), F32)],
        input_output_aliases={3: 0},
        compiler_params=_params(VMEM_MB),
    )(dfrow, dfcol, fraw, dp)


def dil_bwd(p, dyb, yb, lse, tabs, dp, comm=None):
    def body(*refs):
        qkv = [refs[3 * g:3 * g + 3] for g in range(3)]
        dy_ref, y_ref, lse_ref, c_ref, s1_ref, s2_ref = refs[9:15]
        dp_ref = refs[16]
        dq_s, dk_s, dv_s, dl_s = refs[17:21]
        masks = _head_masks(SPAN)
        m256 = _head_masks(256)
        nt = (((1,), (1,)), ((), ()))
        tn = (((0,), (0,)), ((), ()))
        dk_s[...] = jnp.zeros_like(dk_s)
        dv_s[...] = jnp.zeros_like(dv_s)

        def prep(i, _):
            rows = pl.ds(pl.multiple_of(i * 256, 256), 256)
            pr = dy_ref[rows, :] * y_ref[rows, :]
            d0 = jnp.sum(jnp.where(m256[0], pr, 0.0), axis=1, keepdims=True)
            d1 = jnp.sum(jnp.where(m256[1], pr, 0.0), axis=1, keepdims=True)
            dl_s[rows, :] = jnp.where(m256[0], d0, d1)
            return 0

        lax.fori_loop(0, S // 256, prep, 0)

        for g, (d, nb) in enumerate(DIL_GROUPS):
            q_ref, k_ref, v_ref = qkv[g]

            def blk(n, _):
                start, prev, has_prev = _dil_block(n, d, nb)
                q = _dil_rows(q_ref, start, d)
                kc = jnp.concatenate([_dil_rows(k_ref, prev, d), _dil_rows(k_ref, start, d)], axis=0).astype(BF16)
                vc = jnp.concatenate([_dil_rows(v_ref, prev, d), _dil_rows(v_ref, start, d)], axis=0).astype(BF16)
                dov = _dil_rows(dy_ref, start, d)
                lsev = _dil_rows(lse_ref, start, d)
                dlv = _dil_rows(dl_s, start, d)
                valid = _band_mask(has_prev)
                valid2 = jnp.concatenate([valid, valid], axis=0)

                def stack(t):
                    return jnp.concatenate([jnp.where(masks[0], t, 0.0), jnp.where(masks[1], t, 0.0)], axis=0)

                def column(t):
                    return jnp.concatenate([jnp.max(jnp.where(masks[hh], t, NEG), axis=1, keepdims=True) for hh in range(2)], axis=0)

                q2 = (stack(q) * SCALE).astype(BF16)
                dob = stack(dov).astype(BF16)
                s = jnp.where(valid2, lax.dot_general(q2, kc, nt, preferred_element_type=F32), NEG)
                pr = jnp.exp(s - column(lsev))
                dpr = lax.dot_general(dob, vc, nt, preferred_element_type=F32)
                dsb = (pr * (dpr - column(dlv))).astype(BF16)
                dq = jnp.dot(dsb, kc, preferred_element_type=F32) * SCALE
                dkc = lax.dot_general(dsb, q2, tn, preferred_element_type=F32)
                dvc = lax.dot_general(pr.astype(BF16), dob, tn, preferred_element_type=F32)
                _dil_store(dq_s.at[g], start, d, jnp.where(masks[0], dq[:SPAN], dq[SPAN:]))
                for ref, val in ((dk_s.at[g], dkc), (dv_s.at[g], dvc)):
                    _dil_store(ref, prev, d, _dil_rows(ref, prev, d) + jnp.where(has_prev, val[0:SPAN], 0.0))
                    _dil_store(ref, start, d, _dil_rows(ref, start, d) + val[SPAN:])
                return 0

            lax.fori_loop(0, 16, blk, 0)

        def fin(i, _):
            rows = pl.ds(pl.multiple_of(i * 256, 256), 256)
            c, s1, s2 = c_ref[rows, :], s1_ref[rows, :], s2_ref[rows, :]
            for g in range(3):
                base = g * 384
                dp_ref[rows, base:base + LANES] = _rope_bwd(dq_s[g, rows, :], c, s1, s2).astype(BF16)
                dp_ref[rows, base + LANES:base + 2 * LANES] = _rope_bwd(dk_s[g, rows, :], c, s1, s2).astype(BF16)
                dp_ref[rows, base + 2 * LANES:base + 3 * LANES] = dv_s[g, rows, :].astype(BF16)
            return 0

        lax.fori_loop(0, S // 256, fin, 0)

    def spec(g, t):
        return pl.BlockSpec((S, LANES), lambda p_: (0, (p_ * 3 + g) * 3 + t))

    pair = pl.BlockSpec((S, LANES), lambda p_: (0, p_))
    tab = pl.BlockSpec((S, LANES), lambda p_: (0, 0))
    return _hosted_call(
        body, comm, [p] * 9 + [dyb, yb, lse, *tabs, dp], name="dil_bwd", grid=(N_DIL_PAIRS,),
        in_specs=[spec(g, t) for g in range(3) for t in range(3)] + [pair, pair, pair, tab, tab, tab, pl.BlockSpec(memory_space=pl.ANY)],
        out_specs=pl.BlockSpec((S, 1152), lambda p_: (0, p_)),
        out_shape=_sds((S, LAY_N), BF16),
        scratch_shapes=[pltpu.VMEM((3, S, LANES), F32)] * 3 + [pltpu.VMEM((S, LANES), F32)],
        aliases={15: 0}, vmem_mb=VMEM_MB)


def in_bwd_tail(dh1, x, dx1, mod, g_mix, comm=None):
    tm = 256
    nsteps = S // tm

    def body(dh_ref, x_ref, dx1_ref, mod_ref, g_ref, dx_ref, cs_ref, s_cs):
        i = pl.program_id(0)

        @pl.when(i == 0)
        def _():
            s_cs[...] = jnp.zeros_like(s_cs)

        xv, g, dh = x_ref[...], g_ref[...], dh_ref[...]
        r = lax.rsqrt(jnp.mean(xv * xv, axis=1, keepdims=True) + EPS)
        xh = xv * r
        s_cs[0] += _colsum8(dh)
        s_cs[1] += _colsum8(dh * (xh * g))
        dn = dh * (1.0 + mod_ref[1:2, :])
        s_cs[2] += _colsum8(dn * xh)
        gd = dn * g
        dx_ref[...] = dx1_ref[...] + r * (gd - xh * jnp.mean(gd * xh, axis=1, keepdims=True))

        @pl.when(i == nsteps - 1)
        def _():
            for t in range(3):
                cs_ref[t:t + 1, :] = jnp.sum(s_cs[t], axis=0, keepdims=True)
            cs_ref[3:8, :] = jnp.zeros((5, D), F32)

    row = pl.BlockSpec((tm, D), lambda i: (i, 0))
    return _hosted_call(
        body, comm, (dh1, x, dx1, mod, g_mix), name="in_bwd_tail", grid=(nsteps,),
        in_specs=[row, row, row, pl.BlockSpec((8, D), lambda i: (0, 0)), pl.BlockSpec((1, D), lambda i: (0, 0))],
        out_specs=[row, pl.BlockSpec((8, D), lambda i: (0, 0))],
        out_shape=[_sds((S, D), F32), _sds((8, D), F32)],
        scratch_shapes=[pltpu.VMEM((3, 8, D), F32)])


def _lay_pieces():
    out = []
    qa, ka, va, fa, qb, kb, vb, ga = 0, 512, 1024, 1536, 1544, 2312, 3080, 3848
    for p in range(N_DIL_PAIRS):
        for g in range(3):
            base = LAY_B + (p * 3 + g) * 384
            hd0 = (4 * g + 2 * p) * HD
            out += [(base, qb + hd0, LANES), (base + LANES, kb + hd0, LANES), (base + 2 * LANES, vb + hd0, LANES)]
    for p in range(N_FOX_PAIRS):
        base = LAY_A + p * 384
        out += [(base, qa + p * LANES, LANES), (base + LANES, ka + p * LANES, LANES), (base + 2 * LANES, va + p * LANES, LANES)]
    out.append((LAY_F, fa, 8))
    out.append((LAY_G, ga, 2 * D))
    return out


def lay_from_nat(w_nat):
    parts, pos = [], 0
    for lay, nat, width in sorted(_lay_pieces()):
        if lay > pos:
            parts.append(jnp.zeros((w_nat.shape[0], lay - pos), w_nat.dtype))
        parts.append(w_nat[:, nat:nat + width])
        pos = lay + width
    if pos < LAY_N:
        parts.append(jnp.zeros((w_nat.shape[0], LAY_N - pos), w_nat.dtype))
    return jnp.concatenate(parts, axis=1)


def nat_from_lay(w_lay):
    parts = [w_lay[:, lay:lay + width] for lay, nat, width in sorted(_lay_pieces(), key=lambda t: t[1])]
    return jnp.concatenate(parts, axis=1)


def _shard_runs():
    runs = []
    for lay, nat, width in _lay_pieces():
        while width:
            k, loc = nat // IN_SHARD, nat % IN_SHARD
            w = min(width, IN_SHARD - loc)
            runs.append((lay, k, loc, w))
            lay, nat, width = lay + w, nat + w, width - w
    return runs


def lay_from_shards(g):
    tm = 256

    def body(g_ref, o_ref):
        o_ref[:, LAY_F:LAY_G] = jnp.zeros((tm, LAY_G - LAY_F), g.dtype)
        for lay, k, loc, w in _shard_runs():
            o_ref[:, lay:lay + w] = g_ref[k, :, loc:loc + w]

    return pl.pallas_call(
        body, name="lay_from_shards", grid=(D // tm,),
        in_specs=[pl.BlockSpec((4, tm, IN_SHARD_PAD), lambda i: (0, i, 0))],
        out_specs=pl.BlockSpec((tm, LAY_N), lambda i: (i, 0)),
        out_shape=_sds((D, LAY_N), g.dtype), compiler_params=_params(VMEM_MB),
    )(g)


def shards_from_lay(dw_lay):
    tm = 256

    def body(x_ref, o_ref):
        o_ref[:, :, IN_SHARD:] = jnp.zeros((4, tm, IN_SHARD_PAD - IN_SHARD), F32)
        for lay, k, loc, w in _shard_runs():
            o_ref[k, :, loc:loc + w] = x_ref[:, lay:lay + w]

    return pl.pallas_call(
        body, name="shards_from_lay", grid=(D // tm,),
        in_specs=[pl.BlockSpec((tm, LAY_N), lambda i: (i, 0))],
        out_specs=pl.BlockSpec((4, tm, IN_SHARD_PAD), lambda i: (0, i, 0)),
        out_shape=_sds((4, D, IN_SHARD_PAD), F32), compiler_params=_params(VMEM_MB),
    )(dw_lay)


def _pos():
    return lax.axis_index("x"), lax.axis_index("y"), lax.axis_index("c")


def _other_chips(x, y):
    return [(1 - x, y), (x, 1 - y), (1 - x, 1 - y)]


def _remote(src, dst, send_sem, recv_sem, dev):
    return pltpu.make_async_remote_copy(src_ref=src, dst_ref=dst, send_sem=send_sem, recv_sem=recv_sem,
                                        device_id=dev, device_id_type=MESH)


VMEM_SPEC = pl.BlockSpec(memory_space=pltpu.VMEM)
ANY_SPEC = pl.BlockSpec(memory_space=pl.ANY)


def gather_all(v, name, with_sum):
    r = v.shape[0]

    def body(v_ref, out_ref, *rest):
        send_s, recv_s = rest[-2:]
        x, y, c = _pos()
        me = 4 * x + 2 * y + c
        out_ref[me] = v_ref[...]
        peers = []
        for m in range(1, 8):
            px = 1 - x if m & 4 else x
            py = 1 - y if m & 2 else y
            pc = 1 - c if m & 1 else c
            peers.append((px, py, pc))
        copies = [_remote(v_ref, out_ref.at[me], send_s.at[i], recv_s.at[i], dev) for i, dev in enumerate(peers)]
        for cp in copies:
            cp.start()
        for i, (px, py, pc) in enumerate(peers):
            _remote(v_ref, out_ref.at[4 * px + 2 * py + pc], send_s.at[i], recv_s.at[i], (px, py, pc)).wait_recv()
        for cp in copies:
            cp.wait_send()
        if with_sum:
            acc = out_ref[0]
            for b in range(1, 8):
                acc = acc + out_ref[b]
            rest[0][...] = acc

    out_shape = [_sds((8, r, LANES), F32)] + ([_sds((r, LANES), F32)] if with_sum else [])
    return pl.pallas_call(
        body, name=name, in_specs=[VMEM_SPEC], out_specs=[VMEM_SPEC] * len(out_shape), out_shape=out_shape,
        scratch_shapes=[pltpu.SemaphoreType.DMA((7,)), pltpu.SemaphoreType.DMA((7,))],
    )(v)


def mod_exchange(c_all, w_ada_sh, b_sh):
    def body(c_ref, w_ref, b_ref, out_ref, sc_ref, modp, send_s, recv_s):
        cv = c_ref[...]
        sc = cv * _sigmoid(cv)
        sc_ref[...] = sc
        modp[...] = jnp.dot(sc, w_ref[...], precision=lax.Precision.HIGHEST, preferred_element_type=F32) + b_ref[...]
        x, y, c = _pos()
        k = 2 * x + y
        out_ref[k] = modp[...]
        chips = _other_chips(x, y)
        copies = [_remote(modp, out_ref.at[k], send_s.at[j], recv_s.at[j], (cx, cy, c)) for j, (cx, cy) in enumerate(chips)]
        for cp in copies:
            cp.start()
        for j, (cx, cy) in enumerate(chips):
            _remote(modp, out_ref.at[2 * cx + cy], send_s.at[j], recv_s.at[j], (cx, cy, c)).wait_recv()
        for cp in copies:
            cp.wait_send()

    n = w_ada_sh.shape[1]
    return pl.pallas_call(
        body, name="mod_exchange", in_specs=[VMEM_SPEC] * 3, out_specs=[VMEM_SPEC] * 2,
        out_shape=[_sds((4, 8, n), F32), _sds((8, D), F32)],
        scratch_shapes=[pltpu.VMEM((8, n), F32), pltpu.SemaphoreType.DMA((3,)), pltpu.SemaphoreType.DMA((3,))],
        compiler_params=_params(VMEM_MB),
    )(c_all, w_ada_sh, b_sh)


def gather_weights(shards):
    n = len(shards)

    def body(*refs):
        ins, outs = refs[:n], refs[n:2 * n]
        send_s, recv_s, fsend_s, frecv_s, loc_s = refs[2 * n:]
        x, y, c = _pos()
        k = 2 * x + y
        chips = _other_chips(x, y)
        local, sends, fwds = [], [], []
        for a in range(n):
            half = ins[a].shape[0] // 2
            rows = pl.ds(c * half, half)
            lc = pltpu.make_async_copy(ins[a], outs[a].at[k], loc_s.at[a])
            lc.start()
            local.append(lc)
            for j, (cx, cy) in enumerate(chips):
                cp = _remote(ins[a].at[rows], outs[a].at[k, rows], send_s.at[3 * a + j], recv_s.at[3 * a + j], (cx, cy, c))
                cp.start()
                sends.append(cp)
        for a in range(n):
            half = ins[a].shape[0] // 2
            rows = pl.ds(c * half, half)
            for j, (cx, cy) in enumerate(chips):
                kj = 2 * cx + cy
                _remote(ins[a].at[rows], outs[a].at[kj, rows], send_s.at[3 * a + j], recv_s.at[3 * a + j], (cx, cy, c)).wait_recv()
                fw = _remote(outs[a].at[kj, rows], outs[a].at[kj, rows], fsend_s.at[3 * a + j], frecv_s.at[3 * a + j], (x, y, 1 - c))
                fw.start()
                fwds.append(fw)
        for a in range(n):
            half = ins[a].shape[0] // 2
            orows = pl.ds((1 - c) * half, half)
            for j, (cx, cy) in enumerate(chips):
                kj = 2 * cx + cy
                _remote(outs[a].at[kj, orows], outs[a].at[kj, orows], fsend_s.at[3 * a + j], frecv_s.at[3 * a + j], (x, y, 1 - c)).wait_recv()
        for cp in sends + fwds:
            cp.wait_send()
        for lc in local:
            lc.wait()

    return pl.pallas_call(
        body, name="gather_weights", in_specs=[ANY_SPEC] * n, out_specs=[ANY_SPEC] * n,
        out_shape=[_sds((4,) + s.shape, s.dtype) for s in shards],
        scratch_shapes=[pltpu.SemaphoreType.DMA((3 * n,))] * 4 + [pltpu.SemaphoreType.DMA((n,))],
    )(*shards)


def _row_tile(rows, cap=256):
    t = cap
    while rows % t or t % 8:
        t -= 8
    return t


def _comm_wait(sends, recvs, local=()):
    for cp in recvs:
        cp.wait_recv()
    for cp in sends:
        cp.wait_send()
    for cp in local:
        cp.wait()


def ag_ici(shards):
    n = len(shards)

    def copies(ins, outs, sems):
        send_s, recv_s, loc_s = sems
        x, y, c = _pos()
        k = 2 * x + y
        sends, recvs, local = [], [], []
        for a in range(n):
            half = ins[a].shape[0] // 2
            rows = pl.ds(c * half, half)
            local.append(pltpu.make_async_copy(ins[a], outs[a].at[k], loc_s.at[a]))
            for j, (cx, cy) in enumerate(_other_chips(x, y)):
                sem = (send_s.at[3 * a + j], recv_s.at[3 * a + j], (cx, cy, c))
                sends.append(_remote(ins[a].at[rows], outs[a].at[k, rows], *sem))
                recvs.append(_remote(ins[a].at[rows], outs[a].at[2 * cx + cy, rows], *sem))
        return sends, recvs, local

    def start(ins, outs, sems):
        sends, _, local = copies(ins, outs, sems)
        for cp in local + sends:
            cp.start()

    def wait(ins, outs, sems):
        _comm_wait(*copies(ins, outs, sems))

    return Comm(shards, [_sds((4,) + s.shape, s.dtype) for s in shards], [3 * n, 3 * n, n], start, wait)


def ag_d2d(bufs):
    n = len(bufs)

    def copies(ins, outs, sems):
        send_s, recv_s = sems
        x, y, c = _pos()
        sends, recvs = [], []
        for a in range(n):
            half = outs[a].shape[1] // 2
            rows, orows = pl.ds(c * half, half), pl.ds((1 - c) * half, half)
            for j, (cx, cy) in enumerate(_other_chips(x, y)):
                kj = 2 * cx + cy
                sem = (send_s.at[3 * a + j], recv_s.at[3 * a + j], (x, y, 1 - c))
                sends.append(_remote(outs[a].at[kj, rows], outs[a].at[kj, rows], *sem))
                recvs.append(_remote(outs[a].at[kj, orows], outs[a].at[kj, orows], *sem))
        return sends, recvs

    def start(ins, outs, sems):
        for cp in copies(ins, outs, sems)[0]:
            cp.start()

    def wait(ins, outs, sems):
        _comm_wait(*copies(ins, outs, sems))

    return Comm(bufs, [_sds(b.shape, b.dtype) for b in bufs], [3 * n, 3 * n], start, wait, aliases={a: a for a in range(n)})


def rs_a(grads):
    n = len(grads)

    def copies(ins, outs, sems):
        send_s, recv_s = sems
        x, y, c = _pos()
        cps = []
        for a in range(n):
            half = ins[a].shape[1] // 2
            cps.append(_remote(ins[a].at[:, pl.ds((1 - c) * half, half), :], outs[a], send_s.at[a], recv_s.at[a], (x, y, 1 - c)))
        return cps

    def start(ins, outs, sems):
        for cp in copies(ins, outs, sems):
            cp.start()

    def wait(ins, outs, sems):
        cps = copies(ins, outs, sems)
        _comm_wait(cps, cps)

    return Comm(grads, [_sds((4, g.shape[1] // 2, g.shape[2]), g.dtype) for g in grads], [n, n], start, wait)


def rs_b(pres):
    n = len(pres)

    def copies(ins, outs, sems):
        send_s, recv_s, loc_s = sems
        x, y, c = _pos()
        k = 2 * x + y
        cps, local = [], []
        for a in range(n):
            local.append(pltpu.make_async_copy(ins[a].at[k], outs[a].at[3], loc_s.at[a]))
            for j, (cx, cy) in enumerate(_other_chips(x, y)):
                cps.append(_remote(ins[a].at[2 * cx + cy], outs[a].at[j], send_s.at[3 * a + j], recv_s.at[3 * a + j], (cx, cy, c)))
        return cps, local

    def start(ins, outs, sems):
        cps, local = copies(ins, outs, sems)
        for cp in local + cps:
            cp.start()

    def wait(ins, outs, sems):
        cps, local = copies(ins, outs, sems)
        _comm_wait(cps, cps, local)

    return Comm(pres, [_sds(p_.shape, p_.dtype) for p_ in pres], [3 * n, 3 * n, n], start, wait)


def rs_b_rows(pre, buf, lo, n):
    def copies(ins, outs, sems):
        send_s, recv_s, loc_s = sems
        x, y, c = _pos()
        rows = pl.ds(lo, n)
        local = [pltpu.make_async_copy(ins[0].at[2 * x + y, rows], outs[0].at[3, rows], loc_s.at[0])]
        cps = [_remote(ins[0].at[2 * cx + cy, rows], outs[0].at[j, rows], send_s.at[j], recv_s.at[j], (cx, cy, c))
               for j, (cx, cy) in enumerate(_other_chips(x, y))]
        return cps, local

    def start(ins, outs, sems):
        cps, local = copies(ins, outs, sems)
        for cp in local + cps:
            cp.start()

    def wait(ins, outs, sems):
        cps, local = copies(ins, outs, sems)
        _comm_wait(cps, cps, local)

    ins = [pre] if buf is None else [pre, buf]
    return Comm(ins, [_sds(pre.shape, pre.dtype)], [3, 3, 1], start, wait, aliases={} if buf is None else {1: 0})


def rs_c(reds):
    n = len(reds)

    def copies(ins, outs, sems):
        send_s, recv_s, loc_s = sems
        x, y, c = _pos()
        sends, recvs, local = [], [], []
        for a in range(n):
            half = ins[a].shape[0]
            rows, orows = pl.ds(c * half, half), pl.ds((1 - c) * half, half)
            local.append(pltpu.make_async_copy(ins[a], outs[a].at[rows], loc_s.at[a]))
            sem = (send_s.at[a], recv_s.at[a], (x, y, 1 - c))
            sends.append(_remote(ins[a], outs[a].at[rows], *sem))
            recvs.append(_remote(ins[a], outs[a].at[orows], *sem))
        return sends, recvs, local

    def start(ins, outs, sems):
        sends, _, local = copies(ins, outs, sems)
        for cp in local + sends:
            cp.start()

    def wait(ins, outs, sems):
        _comm_wait(*copies(ins, outs, sems))

    return Comm(reds, [_sds((2 * r_.shape[0], r_.shape[1]), r_.dtype) for r_ in reds], [n, n, n], start, wait)


def comm_join(*comms):
    ni = np.cumsum([0] + [len(c.ins) for c in comms])
    no = np.cumsum([0] + [len(c.out_shapes) for c in comms])
    ns = np.cumsum([0] + [len(c.sems) for c in comms])

    def parts(ins, outs, sems):
        return [(c, ins[ni[i]:ni[i + 1]], outs[no[i]:no[i + 1]], sems[ns[i]:ns[i + 1]]) for i, c in enumerate(comms)]

    def start(ins, outs, sems):
        for c, a, b, s in parts(ins, outs, sems):
            c.start(a, b, s)

    def wait(ins, outs, sems):
        for c, a, b, s in parts(ins, outs, sems):
            c.wait(a, b, s)

    aliases = {int(ni[i]) + k: int(no[i]) + v for i, c in enumerate(comms) for k, v in c.aliases.items()}
    return Comm(sum((c.ins for c in comms), []), sum((c.out_shapes for c in comms), []), sum((c.sems for c in comms), []),
                start, wait, aliases)


def comm_only(comm, name):
    nci, nco = len(comm.ins), len(comm.out_shapes)

    def body(*refs):
        ins, outs, sems = refs[:nci], refs[nci:nci + nco], refs[nci + nco:]
        comm.start(ins, outs, sems)
        comm.wait(ins, outs, sems)

    return pl.pallas_call(
        body, name=name, in_specs=[ANY_SPEC] * nci, out_specs=[ANY_SPEC] * nco, out_shape=comm.out_shapes,
        scratch_shapes=[pltpu.SemaphoreType.DMA((s,)) for s in comm.sems],
        input_output_aliases=comm.aliases,
    )(*comm.ins)


def rs_add_halves(g, other, core, name):
    _, r, cdim = g.shape
    half = r // 2
    tr = _row_tile(half, 256)
    nb = half // tr

    def body(core_ref, g_ref, o_ref, out_ref):
        del core_ref
        out_ref[...] = (g_ref[...] + o_ref[...]).astype(BF16)

    grid_spec = pltpu.PrefetchScalarGridSpec(
        num_scalar_prefetch=1, grid=(4, nb),
        in_specs=[pl.BlockSpec((None, tr, cdim), lambda k, i, cr: (k, cr[0] * nb + i, 0)),
                  pl.BlockSpec((None, tr, cdim), lambda k, i, cr: (k, i, 0))],
        out_specs=pl.BlockSpec((None, tr, cdim), lambda k, i, cr: (k, i, 0)))
    return pl.pallas_call(body, name=name, grid_spec=grid_spec, out_shape=_sds((4, half, cdim), BF16))(core, g, other)


def rs_add_slabs(t, name):
    _, half, cdim = t.shape
    tr = _row_tile(half, 256)

    def body(t_ref, out_ref):
        s = [t_ref[i].astype(F32) for i in range(4)]
        out_ref[...] = ((s[3] + s[0]) + s[1]) + s[2]

    return pl.pallas_call(
        body, name=name, grid=(half // tr,),
        in_specs=[pl.BlockSpec((4, tr, cdim), lambda i: (0, i, 0))],
        out_specs=pl.BlockSpec((tr, cdim), lambda i: (i, 0)),
        out_shape=_sds((half, cdim), F32),
    )(t)


def _adam_math(w, g, m, v):
    m = ADAM_B1 * m + (1.0 - ADAM_B1) * g
    v = ADAM_B2 * v + (1.0 - ADAM_B2) * (g * g)
    m_hat = m / (1.0 - ADAM_B1 ** ADAM_STEP)
    v_hat = v / (1.0 - ADAM_B2 ** ADAM_STEP)
    delta = -ADAM_LR * (m_hat / (jnp.sqrt(v_hat) + ADAM_EPS) + ADAM_WD * w)
    return delta, m, v


def adam(w, g, m, v, name, comm=None):
    r, cdim = w.shape
    tr = _row_tile(r) if r >= 8 else r

    def body(w_ref, g_ref, m_ref, v_ref, g_out, d_ref, nm_ref, nv_ref):
        gv = g_ref[:, :cdim]
        g_out[...] = gv
        d_ref[...], nm_ref[...], nv_ref[...] = _adam_math(w_ref[...], gv, m_ref[...], v_ref[...])

    blk = pl.BlockSpec((tr, cdim), lambda i: (i, 0))
    return _hosted_call(
        body, comm, (w, g, m, v), name=name, grid=(r // tr,),
        in_specs=[blk, pl.BlockSpec((tr, g.shape[1]), lambda i: (i, 0)), blk, blk],
        out_specs=[blk] * 4, out_shape=[_sds((r, cdim), F32)] * 4)


def adam_w_ada(sc_t, dmod_sh, w, m, v, comm=None):
    r, cdim = w.shape
    tr = 256

    def body(s_ref, d_ref, w_ref, m_ref, v_ref, g_ref, dl_ref, nm_ref, nv_ref):
        g = jnp.dot(s_ref[...], d_ref[...], precision=lax.Precision.HIGHEST, preferred_element_type=F32)
        g_ref[...] = g
        dl_ref[...], nm_ref[...], nv_ref[...] = _adam_math(w_ref[...], g, m_ref[...], v_ref[...])

    blk = pl.BlockSpec((tr, cdim), lambda i: (i, 0))
    return _hosted_call(
        body, comm, (sc_t, dmod_sh, w, m, v), name="adam_w_ada", grid=(r // tr,),
        in_specs=[pl.BlockSpec((tr, LANES), lambda i: (i, 0)), pl.BlockSpec((LANES, cdim), lambda i: (0, 0)), blk, blk, blk],
        out_specs=[blk] * 4, out_shape=[_sds((r, cdim), F32)] * 4)


SMALL_ROWS = 80


def kernel(x, c, w_ada, b_ada, g_mix, w_in, b_fgate, w_br_a, w_br_b, w_out, g_ffn, w_ffn_gate, w_ffn_up, w_ffn_down, g_final, loss_target, m_w_ada, m_b_ada, m_g_mix, m_w_in, m_b_fgate, m_w_br_a, m_w_br_b, m_w_out, m_g_ffn, m_w_ffn_gate, m_w_ffn_up, m_w_ffn_down, m_g_final, v_w_ada, v_b_ada, v_g_mix, v_w_in, v_b_fgate, v_w_br_a, v_w_br_b, v_w_out, v_g_ffn, v_w_ffn_gate, v_w_ffn_up, v_w_ffn_down, v_g_final):
    xi, yi, ci = _pos()
    chip = 2 * xi + yi
    seq = 4 * xi + 2 * yi + ci
    n_ada = w_ada.shape[2]

    c_all = gather_all(c.reshape(8, LANES), "gather_c", False)[0].reshape(8, D)
    b_sh = lax.dynamic_slice(b_ada, (0, chip * n_ada), (1, n_ada))
    mod_all, sc = mod_exchange(c_all, w_ada[0], b_sh)
    mod = lax.dynamic_index_in_dim(mod_all, seq, axis=1, keepdims=False).reshape(6, D)
    mod8 = jnp.pad(mod, ((0, 2), (0, 0)))

    shards = [
        jnp.pad(w_in[0], ((0, 0), (0, IN_SHARD_PAD - IN_SHARD))), w_br_a[0], w_br_b[0], w_out[0],
        jnp.pad(w_ffn_gate[0], ((0, 0), (0, FF_PAD - FF_SHARD))), jnp.pad(w_ffn_up[0], ((0, 0), (0, FF_PAD - FF_SHARD))),
        jnp.pad(w_ffn_down[0], ((0, FF_PAD - FF_SHARD), (0, 0))),
    ]
    shards = [s.astype(BF16) for s in shards]
    core = ci.astype(jnp.int32).reshape(1)
    xs, tgt, g_fin = x[0], loss_target[0], g_final.reshape(1, D)

    s_in, s_bra, s_brb, s_out, s_gate, s_up, s_down = shards

    def halves(gs, others, tag):
        return [rs_add_halves(g, o, core, f"rs_{tag}_halves_{i}") for i, (g, o) in enumerate(zip(gs, others))]

    def slab_sums(ts, tag):
        return [rs_add_slabs(t, f"rs_{tag}_slabs_{i}") for i, t in enumerate(ts)]

    g_in = gather_weights([s_in])[0]
    w_lay = lay_from_shards(g_in)
    tabs = rope_tables()
    h1 = norm_mod_fwd(xs, g_mix, mod8, 0, 1)
    p, mix_w = in_proj_fwd(h1, w_lay, tabs, comm=ag_ici([s_bra, s_brb, s_out]))
    frow, fraw, fcol = fgate_fwd(p, jnp.pad(b_fgate, ((0, 0), (0, LANES - 8))))
    (ya_att, gcol), res = fox_fwd(p, fcol, comm=comm_join(ag_d2d(mix_w), ag_ici([s_gate, s_up])))
    g_bra, g_brb, g_out = res[:3]
    (yb, lse_b), res = dil_fwd(p, comm=comm_join(ag_d2d(res[3:]), ag_ici([s_down])))
    w_gate, w_up = res[:2]
    w_bra = g_bra.transpose(1, 0, 2).reshape(512, D)
    w_brb = g_brb.transpose(1, 0, 2).reshape(256, D)
    w_o = g_out.reshape(D, D)
    (merged, ya, ybp), (g_down,) = merge_fwd(ya_att, yb, p, w_bra, w_brb, comm=ag_d2d(res[2:]))
    w_down = g_down.reshape(FFP, D)
    mix, x1, h2 = out_proj_fwd(merged, w_o, xs, mod8, g_ffn)
    a, u, z = ffn_up_fwd(h2, w_gate, w_up)
    dx2, dffn, dg_final, dga_f, loss_part = ffn_down_loss(z, w_down, x1, mod8, g_fin, tgt)

    da, du, dw_down = ffn_down_bwd(dffn, w_down, a, u, z)
    g_down = [dw_down.reshape(4, FF_PAD, D)]
    dh2a, oth = mm_nt(da, w_gate, "ffn_gate_dx", comm=rs_a(g_down))
    pre_down = halves(g_down, oth, "down")
    dh2b, _ = mm_nt(du, w_up, "ffn_up_dx")
    dw_gate, _ = mm_tn(h2, da, "ffn_gate_dw", shard_major=True)
    dw_up, _ = mm_tn(h2, du, "ffn_up_dw", shard_major=True)
    g_gu = [dw_gate, dw_up]
    (dx1, dp1, dya_att, dyb, cs_mid, dw_out, dw_bra, dw_brb), res = mid_bwd(
        dh2a, dh2b, x1, dx2, mix, mod8, g_ffn, p, ya, ybp, merged, ya_att, yb, w_o, w_bra, w_brb,
        comm=comm_join(rs_b(pre_down), rs_a(g_gu)))
    red_down = slab_sums(res[:1], "down")
    pre_gu = halves(g_gu, res[1:], "gu")
    g_mix3 = [dw_bra.reshape(512, 4, 256).transpose(1, 0, 2), dw_brb.reshape(256, 4, 256).transpose(1, 0, 2), dw_out.reshape(4, 256, D)]
    (dp2, dfrow, dfcol), res = fox_bwd(p, dya_att, ya_att, gcol, fcol, dp1,
                                       comm=comm_join(rs_b(pre_gu), rs_c(red_down), rs_a(g_mix3)))
    red_gu = slab_sums(res[:2], "gu")
    r_down = res[2]
    pre_mix3 = halves(g_mix3, res[3:], "mix")
    dp3, db_fg = fgate_bwd(dfrow.reshape(8, S), dfcol, fraw, dp2)
    dp4, res = dil_bwd(p, dyb, yb, lse_b, tabs, dp3, comm=comm_join(rs_c(red_gu), rs_b(pre_mix3)))
    r_gate, r_up = res[:2]
    red_mix3 = slab_sums(res[2:], "mix")
    dw_lay, (r_bra, r_brb, r_out) = mm_tn(h1, dp4, "in_proj_dw", comm=rs_c(red_mix3))
    g_in4 = [shards_from_lay(dw_lay)]
    dh1, oth = mm_nt(dp4, w_lay, "in_proj_dx", comm=rs_a(g_in4))
    (pre_in,) = halves(g_in4, oth, "in")
    qrows = pre_in.shape[1] // 4
    (dx, cs_in), (t_in,) = in_bwd_tail(dh1, xs, dx1, mod8, g_mix, comm=rs_b_rows(pre_in, None, 0, qrows))

    dmod = jnp.concatenate([cs_in[0:2], cs_mid[3:4], cs_mid[0:2], dga_f], axis=0)
    small = dict(dmod=dmod, dg_mix=cs_in[2:3], dg_ffn=cs_mid[2:3], dg_final=dg_final, db_fgate=db_fg[:, 0], loss=loss_part[0, 0])
    sv = jnp.concatenate([
        small["dmod"].reshape(48, LANES), small["dg_mix"].reshape(8, LANES), small["dg_ffn"].reshape(8, LANES),
        small["dg_final"].reshape(8, LANES), jnp.pad(small["db_fgate"], (0, LANES - 8)).reshape(1, LANES),
        jnp.broadcast_to(small["loss"], (1, LANES)), jnp.zeros((SMALL_ROWS - 74, LANES), F32)], axis=0)
    sv_all, sv_sum = gather_all(sv, "gather_small", True)
    loss = sv_sum[73, 0]
    g_small = dict(b_ada=sv_sum[0:48].reshape(1, 6 * D), g_mix=sv_sum[48:56].reshape(1, D), g_ffn=sv_sum[56:64].reshape(1, D),
                   g_final=sv_sum[64:72].reshape(D), b_fgate=sv_sum[72, 0:8].reshape(1, 8))

    dmod_all = lax.dynamic_slice(sv_all[:, 0:48, :].reshape(8, 6 * D), (0, chip * n_ada), (8, n_ada))
    (g_ada, d_ada, nm_ada, nv_ada), (t_in,) = adam_w_ada(
        jnp.pad(sc.T, ((0, 0), (0, LANES - 8))), jnp.pad(dmod_all, ((0, LANES - 8), (0, 0))), w_ada[0], m_w_ada[0], v_w_ada[0],
        comm=rs_b_rows(pre_in, t_in, qrows, qrows))

    big = dict(w_in=(w_in, m_w_in, v_w_in), w_br_a=(w_br_a, m_w_br_a, v_w_br_a), w_br_b=(w_br_b, m_w_br_b, v_w_br_b),
               w_out=(w_out, m_w_out, v_w_out), w_ffn_gate=(w_ffn_gate, m_w_ffn_gate, v_w_ffn_gate),
               w_ffn_up=(w_ffn_up, m_w_ffn_up, v_w_ffn_up), w_ffn_down=(w_ffn_down, m_w_ffn_down, v_w_ffn_down))
    gpad = dict(w_br_a=r_bra, w_br_b=r_brb, w_out=r_out, w_ffn_gate=r_gate, w_ffn_up=r_up, w_ffn_down=r_down)
    upd = {}
    for part, nm in ((2, "w_ffn_gate"), (3, "w_ffn_up")):
        w, m, v = big[nm]
        upd[nm], (t_in,) = adam(w[0], gpad[nm], m[0], v[0], "adam_" + nm, comm=rs_b_rows(pre_in, t_in, part * qrows, qrows))
    (gpad["w_in"],) = comm_only(rs_c(slab_sums([t_in], "in")), "rs_in_share")
    for nm, (w, m, v) in big.items():
        if nm not in upd:
            upd[nm] = adam(w[0], gpad[nm], m[0], v[0], "adam_" + nm)[0]

    def pack(gm, gf, gl, ba, bf):
        rows = [gm.reshape(1, D), gf.reshape(1, D), gl.reshape(1, D), ba.reshape(6, D), jnp.pad(bf.reshape(1, 8), ((0, 0), (0, D - 8)))]
        return jnp.concatenate(rows + [jnp.zeros((6, D), F32)], axis=0)

    packed = adam(pack(g_mix, g_ffn, g_final, b_ada, b_fgate),
                  pack(g_small["g_mix"], g_small["g_ffn"], g_small["g_final"], g_small["b_ada"], g_small["b_fgate"]),
                  pack(m_g_mix, m_g_ffn, m_g_final, m_b_ada, m_b_fgate), pack(v_g_mix, v_g_ffn, v_g_final, v_b_ada, v_b_fgate),
                  "adam_small")[0]

    def unpack(t):
        return dict(g_mix=t[0:1], g_ffn=t[1:2], g_final=t[2], b_ada=t[3:9].reshape(1, 6 * D), b_fgate=t[9:10, 0:8])

    small_upd = [unpack(t) for t in packed[1:]]
    order =["w_ada", "b_ada", "g_mix", "w_in", "b_fgate", "w_br_a", "w_br_b", "w_out", "g_ffn", "w_ffn_gate", "w_ffn_up", "w_ffn_down", "g_final"]

    def leaf(nm, which):
        if nm == "w_ada":
            return (g_ada, d_ada, nm_ada, nv_ada)[which][None]
        if nm in big:
            return upd[nm][which][None]
        return g_small[nm] if which == 0 else small_upd[which - 1][nm]

    outs = [loss, dx[None]]
    for which in range(4):
        outs += [leaf(nm, which) for nm in order]
    return tuple(outs)
```

```python
import functools

import numpy as np
import jax
import jax.numpy as jnp
from jax import lax
from jax.experimental import pallas as pl
from jax.experimental.pallas import tpu as pltpu

F32, BF16 = jnp.float32, jnp.bfloat16
S, D = 2048, 1024
HD = 64
LANES = 128
N_FOX_PAIRS, N_DIL_PAIRS = 4, 2
DIL_GROUPS = ((1, 16), (4, 4), (16, 1))
SPAN = 128
ROT_DIM, ROPE_THETA = 16, 500000.0
D_FF, FF_SHARD, FF_PAD = 2816, 704, 768
FFP = 4 * FF_PAD
IN_COLS, IN_SHARD, IN_SHARD_PAD = 5896, 1474, 1536
LAY_B, LAY_A, LAY_F, LAY_G, LAY_N = 0, 2304, 3840, 4096, 6144
EPS, NEG = 1e-6, -1e30
SCALE = HD ** -0.5
ADAM_LR, ADAM_B1, ADAM_B2, ADAM_EPS, ADAM_WD, ADAM_STEP = 0.001, 0.9, 0.999, 1e-08, 0.01, 10
VMEM_MB = 56
MESH = pl.DeviceIdType.MESH


def _params(vmem_mb=None, **kw):
    if vmem_mb is not None:
        kw["vmem_limit_bytes"] = vmem_mb * 1024 * 1024
    return pltpu.CompilerParams(**kw)


def _sds(shape, dtype):
    return jax.ShapeDtypeStruct(shape, dtype)


def _sigmoid(x):
    return 1.0 / (1.0 + jnp.exp(-x))


def _colsum8(x):
    tm, n = x.shape
    return jnp.sum(x.reshape(tm // 8, 8, n), axis=0)


class Comm:
    def __init__(self, ins, out_shapes, sems, start, wait, aliases=None):
        self.ins, self.out_shapes, self.sems = list(ins), list(out_shapes), list(sems)
        self.start, self.wait, self.aliases = start, wait, dict(aliases or {})


def _hosted_call(body, comm, args, *, name, grid, in_specs, out_specs, out_shape, scratch_shapes=(), aliases=None, vmem_mb=None):
    single = not isinstance(out_shape, (list, tuple))
    out_specs_l = [out_specs] if single else list(out_specs)
    out_shape_l = [out_shape] if single else list(out_shape)
    n_in, n_out, n_scr = len(in_specs), len(out_shape_l), len(scratch_shapes)
    aliases = dict(aliases or {})
    if comm is None:
        res = pl.pallas_call(body, name=name, grid=grid, in_specs=list(in_specs), out_specs=out_specs, out_shape=out_shape,
                             scratch_shapes=list(scratch_shapes), input_output_aliases=aliases,
                             compiler_params=_params(vmem_mb))(*args)
        return res, []
    nci, nco = len(comm.ins), len(comm.out_shapes)

    def wrapped(*refs):
        main_in, cin = refs[:n_in], refs[n_in:n_in + nci]
        o0 = n_in + nci
        main_out, cout = refs[o0:o0 + n_out], refs[o0 + n_out:o0 + n_out + nco]
        s0 = o0 + n_out + nco
        scr, sems = refs[s0:s0 + n_scr], refs[s0 + n_scr:]
        ids = [pl.program_id(i) for i in range(len(grid))]
        first = functools.reduce(jnp.logical_and, [i == 0 for i in ids])
        last = functools.reduce(jnp.logical_and, [i == g - 1 for i, g in zip(ids, grid)])

        @pl.when(first)
        def _():
            comm.start(cin, cout, sems)

        body(*main_in, *main_out, *scr)

        @pl.when(last)
        def _():
            comm.wait(cin, cout, sems)

    for ci, co in comm.aliases.items():
        aliases[n_in + ci] = n_out + co
    any_spec = pl.BlockSpec(memory_space=pl.ANY)
    res = pl.pallas_call(
        wrapped, name=name, grid=grid, in_specs=list(in_specs) + [any_spec] * nci, out_specs=out_specs_l + [any_spec] * nco,
        out_shape=out_shape_l + comm.out_shapes,
        scratch_shapes=list(scratch_shapes) + [pltpu.SemaphoreType.DMA((s,)) for s in comm.sems],
        input_output_aliases=aliases, compiler_params=_params(vmem_mb))(*args, *comm.ins)
    main = list(res[:n_out])
    return (main[0] if single else main), list(res[n_out:])


def norm_mod_fwd(x, g, mod, shift_row, scale_row):
    tm = 256

    def body(x_ref, g_ref, mod_ref, h_ref):
        xv = x_ref[...]
        r = lax.rsqrt(jnp.mean(xv * xv, axis=1, keepdims=True) + EPS)
        n = xv * r * g_ref[...]
        h = n * (1.0 + mod_ref[scale_row:scale_row + 1, :]) + mod_ref[shift_row:shift_row + 1, :]
        h_ref[...] = h.astype(BF16)

    return pl.pallas_call(
        body, name="norm_mod_fwd", grid=(S // tm,),
        in_specs=[pl.BlockSpec((tm, D), lambda i: (i, 0)), pl.BlockSpec((1, D), lambda i: (0, 0)),
                  pl.BlockSpec((8, D), lambda i: (0, 0))],
        out_specs=pl.BlockSpec((tm, D), lambda i: (i, 0)),
        out_shape=_sds((S, D), BF16),
    )(x, g, mod)


def rope_tables():
    pos = jnp.arange(S, dtype=F32)
    inv_freq = ROPE_THETA ** (-jnp.arange(0, ROT_DIM, 2, dtype=F32) / ROT_DIM)
    ang = pos[:, None] * inv_freq[None, :]
    cos, sin = jnp.cos(ang), jnp.sin(ang)
    one, zero = jnp.ones((S, HD - ROT_DIM), F32), jnp.zeros((S, HD - ROT_DIM), F32)
    z8 = jnp.zeros((S, 8), F32)
    c = jnp.concatenate([cos, cos, one], axis=1)
    s1 = jnp.concatenate([-sin, z8, zero], axis=1)
    s2 = jnp.concatenate([z8, sin, zero], axis=1)
    return tuple(jnp.concatenate([t, t], axis=1) for t in (c, s1, s2))


def _rope(y, c, s1, s2):
    return y * c + pltpu.roll(y, LANES - 8, 1) * s1 + pltpu.roll(y, 8, 1) * s2


def _rope_bwd(dy, c, s1, s2):
    return dy * c + pltpu.roll(dy * s1, 8, 1) + pltpu.roll(dy * s2, LANES - 8, 1)


def in_proj_fwd(h, w_lay, tabs, comm=None):
    tm, tn = 2048, 768
    n_rope = N_DIL_PAIRS * 3 // 2

    def body(a_ref, w_ref, c_ref, s1_ref, s2_ref, o_ref):
        j = pl.program_id(0)
        y = jnp.dot(a_ref[...], w_ref[...], preferred_element_type=F32)

        @pl.when(j < n_rope)
        def _():
            c, s1, s2 = c_ref[...], s1_ref[...], s2_ref[...]
            for t in range(tn // LANES):
                chunk = y[:, LANES * t:LANES * (t + 1)]
                o_ref[:, LANES * t:LANES * (t + 1)] = chunk if t % 3 == 2 else _rope(chunk, c, s1, s2)

        @pl.when(j >= n_rope)
        def _():
            o_ref[...] = y

    tab = pl.BlockSpec((tm, LANES), lambda j, i: (i, 0))
    return _hosted_call(
        body, comm, (h, w_lay, *tabs), name="in_proj_fwd", grid=(LAY_N // tn, S // tm),
        in_specs=[pl.BlockSpec((tm, D), lambda j, i: (i, 0)), pl.BlockSpec((D, tn), lambda j, i: (0, j)), tab, tab, tab],
        out_specs=pl.BlockSpec((tm, tn), lambda j, i: (i, j)),
        out_shape=_sds((S, LAY_N), F32), vmem_mb=VMEM_MB)


def _log1p_small(t):
    return jnp.where(t < 1e-2, t * (1.0 - t * (0.5 - t * (1.0 / 3.0))), jnp.log(1.0 + t))


def fgate_fwd(p, b_pad):
    def body(fa_ref, b_ref, frow_ref, fraw_ref, fcol_ref):
        f = fa_ref[...] + b_ref[...]
        fr = f.T[0:8, :]
        ls = jnp.minimum(fr, 0.0) - _log1p_small(jnp.exp(-jnp.abs(fr)))
        lane = lax.broadcasted_iota(jnp.int32, (8, S), 1)
        acc, sh = ls, 1
        while sh < S:
            acc = acc + jnp.where(lane >= sh, pltpu.roll(acc, sh, 1), 0.0)
            sh *= 2
        frow_ref[...] = acc
        fraw_ref[...] = fr
        for hh in range(8):
            fcol_ref[hh] = jnp.broadcast_to(acc[hh:hh + 1, :], (LANES, S)).T

    return pl.pallas_call(
        body, name="fgate_fwd", grid=(1,),
        in_specs=[pl.BlockSpec((S, LANES), lambda i: (0, LAY_F // LANES)), pl.BlockSpec((1, LANES), lambda i: (0, 0))],
        out_specs=[pl.BlockSpec((8, S), lambda i: (0, 0)), pl.BlockSpec((8, S), lambda i: (0, 0)),
                   pl.BlockSpec((8, S, LANES), lambda i: (0, 0, 0))],
        out_shape=[_sds((8, S), F32), _sds((8, S), F32), _sds((8, S, LANES), F32)],
        compiler_params=_params(VMEM_MB),
    )(p, b_pad)


def _head_masks(rows):
    lane = lax.broadcasted_iota(jnp.int32, (rows, LANES), 1)
    return lane < HD, lane >= HD


FT = 256


def _split3(f):
    hi = f.astype(BF16).astype(F32)
    r = f - hi
    mid = r.astype(BF16).astype(F32)
    return hi, mid, r - mid


def _fox_operands(qkv_ref, tcol_ref, scol_ref, qa_s, ka_s):
    rows = 256
    lane = lax.broadcasted_iota(jnp.int32, (rows, LANES), 1)

    def chunk(i, _):
        r = pl.ds(pl.multiple_of(i * rows, rows), rows)
        q, k = qkv_ref[r, 0:LANES], qkv_ref[r, LANES:2 * LANES]
        s0, s1 = _split3(scol_ref[0, r, :]), _split3(scol_ref[1, r, :])
        ka = jnp.where(lane == 0, -s0[0], jnp.where(lane == 1, -s0[1], jnp.where(lane == 2, -s0[2], jnp.where(
            lane == 3, -s1[0], jnp.where(lane == 4, -s1[1], jnp.where(lane == 5, -s1[2], jnp.where(lane < 9, 1.0, 0.0)))))))
        ka_s[r, 0:LANES] = k.astype(BF16)
        ka_s[r, LANES:2 * LANES] = ka.astype(BF16)
        for hh in range(2):
            own = (lane < HD) if hh == 0 else (lane >= HD)
            t3 = _split3(tcol_ref[hh, r, :])
            ones = (lane >= 3 * hh) & (lane < 3 * hh + 3)
            qa = jnp.where(ones, 1.0, jnp.where(lane == 6, t3[0], jnp.where(lane == 7, t3[1], jnp.where(lane == 8, t3[2], 0.0))))
            qa_s[hh, r, 0:LANES] = jnp.where(own, q * SCALE, 0.0).astype(BF16)
            qa_s[hh, r, LANES:2 * LANES] = qa.astype(BF16)
        return 0

    lax.fori_loop(0, S // rows, chunk, 0)


def fox_fwd(p, fcol, comm=None):
    nt = (((1,), (1,)), ((), ()))

    def body(qkv_ref, fc_ref, o_ref, g_ref, qa_s, ka_s):
        _fox_operands(qkv_ref, fc_ref, fc_ref, qa_s, ka_s)
        masks = _head_masks(FT)
        causal = lax.broadcasted_iota(jnp.int32, (FT, FT), 1) <= lax.broadcasted_iota(jnp.int32, (FT, FT), 0)
        causal2 = jnp.concatenate([causal, causal], axis=0)

        def qloop(qi, _):
            q0 = pl.multiple_of(qi * FT, FT)
            qa = jnp.concatenate([qa_s[0, pl.ds(q0, FT), :], qa_s[1, pl.ds(q0, FT), :]], axis=0)

            def step(kb, carry, diagonal):
                m, l, acc = carry
                k0 = pl.multiple_of(kb * FT, FT)
                v = qkv_ref[pl.ds(k0, FT), 2 * LANES:3 * LANES].astype(BF16)
                s = lax.dot_general(qa, ka_s[pl.ds(k0, FT), :], nt, preferred_element_type=F32)
                if diagonal:
                    s = jnp.where(causal2, s, NEG)
                m_new = jnp.maximum(m, jnp.max(s, axis=1, keepdims=True))
                pr = jnp.exp(s - m_new)
                alpha = jnp.exp(m - m_new)
                return (m_new, l * alpha + jnp.sum(pr, axis=1, keepdims=True),
                        acc * alpha + jnp.dot(pr.astype(BF16), v, preferred_element_type=F32))

            init = (jnp.full((2 * FT, 1), NEG, F32), jnp.zeros((2 * FT, 1), F32), jnp.zeros((2 * FT, LANES), F32))
            carry = lax.fori_loop(0, qi, lambda kb, cr: step(kb, cr, False), init)
            m, l, acc = step(qi, carry, True)
            out = acc / l
            lse = m + jnp.log(l)
            o_ref[pl.ds(q0, FT), :] = jnp.where(masks[0], out[:FT], out[FT:]).astype(BF16)
            g_ref[0, pl.ds(q0, FT), :] = fc_ref[0, pl.ds(q0, FT), :] - lse[:FT]
            g_ref[1, pl.ds(q0, FT), :] = fc_ref[1, pl.ds(q0, FT), :] - lse[FT:]
            return 0

        lax.fori_loop(0, S // FT, qloop, 0)

    a_blk = LAY_A // 384
    return _hosted_call(
        body, comm, (p, fcol), name="fox_fwd", grid=(N_FOX_PAIRS,),
        in_specs=[pl.BlockSpec((S, 384), lambda p_: (0, a_blk + p_)), pl.BlockSpec((2, S, LANES), lambda p_: (p_, 0, 0))],
        out_specs=[pl.BlockSpec((S, LANES), lambda p_: (0, p_)), pl.BlockSpec((2, S, LANES), lambda p_: (p_, 0, 0))],
        out_shape=[_sds((S, 4 * LANES), BF16), _sds((8, S, LANES), F32)],
        scratch_shapes=[pltpu.VMEM((2, S, 2 * LANES), BF16), pltpu.VMEM((S, 2 * LANES), BF16)],
        vmem_mb=VMEM_MB)


def _dil_rows(ref, start, d):
    return ref[pl.ds(start, SPAN), :] if d == 1 else ref[pl.ds(start, SPAN, stride=d), :]


def _dil_store(ref, start, d, val):
    if d == 1:
        ref[pl.ds(start, SPAN), :] = val
    else:
        ref[pl.ds(start, SPAN, stride=d), :] = val


def _band_mask(has_prev):
    qi = lax.broadcasted_iota(jnp.int32, (SPAN, 2 * SPAN), 0) + SPAN
    kj = lax.broadcasted_iota(jnp.int32, (SPAN, 2 * SPAN), 1)
    dist = qi - kj
    return (dist >= 0) & (dist <= SPAN) & (has_prev | (kj >= SPAN))


def _dil_block(n, d, nb):
    r, j = n // nb, n % nb
    start = r + d * SPAN * j
    prev = jnp.maximum(start - d * SPAN, r)
    return start, prev, j > 0


def dil_fwd(p, comm=None):
    def body(*refs):
        qkv = [refs[3 * g:3 * g + 3] for g in range(3)]
        y_ref, lse_ref = refs[9], refs[10]
        acc_s, m_s, l_s = refs[11], refs[12], refs[13]
        masks = _head_masks(SPAN)
        for g, (d, nb) in enumerate(DIL_GROUPS):
            q_ref, k_ref, v_ref = qkv[g]

            def blk(n, _):
                start, prev, has_prev = _dil_block(n, d, nb)
                q = _dil_rows(q_ref, start, d)
                kc = jnp.concatenate([_dil_rows(k_ref, prev, d), _dil_rows(k_ref, start, d)], axis=0).astype(BF16)
                vc = jnp.concatenate([_dil_rows(v_ref, prev, d), _dil_rows(v_ref, start, d)], axis=0).astype(BF16)
                valid = _band_mask(has_prev)
                valid2 = jnp.concatenate([valid, valid], axis=0)
                q2 = (jnp.concatenate([jnp.where(masks[0], q, 0.0), jnp.where(masks[1], q, 0.0)], axis=0) * SCALE).astype(BF16)
                s = jnp.where(valid2, lax.dot_general(q2, kc, (((1,), (1,)), ((), ())), preferred_element_type=F32), NEG)
                m = jnp.max(s, axis=1, keepdims=True)
                pr = jnp.exp(s - m)
                l = jnp.sum(pr, axis=1, keepdims=True)
                acc = jnp.dot(pr.astype(BF16), vc, preferred_element_type=F32)
                _dil_store(acc_s.at[g], start, d, jnp.where(masks[0], acc[:SPAN], acc[SPAN:]))
                _dil_store(m_s.at[g], start, d, jnp.where(masks[0], m[:SPAN], m[SPAN:]))
                _dil_store(l_s.at[g], start, d, jnp.where(masks[0], l[:SPAN], l[SPAN:]))
                return 0

            lax.fori_loop(0, 16, blk, 0)

        def merge(i, _):
            rows = pl.ds(pl.multiple_of(i * 256, 256), 256)
            m = [m_s[g, rows, :] for g in range(3)]
            mx = jnp.maximum(jnp.maximum(m[0], m[1]), m[2])
            w = [jnp.exp(m[g] - mx) for g in range(3)]
            l = sum(l_s[g, rows, :] * w[g] for g in range(3))
            y_ref[rows, :] = sum(acc_s[g, rows, :] * w[g] for g in range(3)) / l
            lse_ref[rows, :] = mx + jnp.log(l)
            return 0

        lax.fori_loop(0, S // 256, merge, 0)

    def spec(g, t):
        return pl.BlockSpec((S, LANES), lambda p_: (0, (p_ * 3 + g) * 3 + t))

    return _hosted_call(
        body, comm, [p] * 9, name="dil_fwd", grid=(N_DIL_PAIRS,),
        in_specs=[spec(g, t) for g in range(3) for t in range(3)],
        out_specs=[pl.BlockSpec((S, LANES), lambda p_: (0, p_)), pl.BlockSpec((S, LANES), lambda p_: (0, p_))],
        out_shape=[_sds((S, 2 * LANES), F32), _sds((S, 2 * LANES), F32)],
        scratch_shapes=[pltpu.VMEM((3, S, LANES), F32)] * 3,
        vmem_mb=VMEM_MB)


def merge_fwd(ya_att, yb, p, w_bra, w_brb, comm=None):
    tm = 256
    gblk = LAY_G // D

    def body(a_ref, b_ref, ga_ref, gb_ref, wa_ref, wb_ref, mg_ref, ya_ref, yb_ref):
        ya = jnp.dot(a_ref[...], wa_ref[...], preferred_element_type=F32)
        ybp = jnp.dot(b_ref[...].astype(BF16), wb_ref[...], preferred_element_type=F32)
        mg_ref[...] = (_sigmoid(ga_ref[...]) * ya + _sigmoid(gb_ref[...]) * ybp).astype(BF16)
        ya_ref[...] = ya
        yb_ref[...] = ybp

    row = lambda w: pl.BlockSpec((tm, w), lambda i: (i, 0))
    return _hosted_call(
        body, comm, (ya_att, yb, p, p, w_bra, w_brb), name="merge_fwd", grid=(S // tm,),
        in_specs=[row(512), row(256), pl.BlockSpec((tm, D), lambda i: (i, gblk)), pl.BlockSpec((tm, D), lambda i: (i, gblk + 1)),
                  pl.BlockSpec((512, D), lambda i: (0, 0)), pl.BlockSpec((256, D), lambda i: (0, 0))],
        out_specs=[row(D), row(D), row(D)],
        out_shape=[_sds((S, D), BF16), _sds((S, D), F32), _sds((S, D), F32)])


def out_proj_fwd(merged, w_out, x, mod, g_ffn):
    tm = 256

    def body(a_ref, w_ref, x_ref, mod_ref, g_ref, mix_ref, x1_ref, h2_ref):
        mix = jnp.dot(a_ref[...], w_ref[...], preferred_element_type=F32)
        x1 = x_ref[...] + mod_ref[2:3, :] * mix
        r = lax.rsqrt(jnp.mean(x1 * x1, axis=1, keepdims=True) + EPS)
        h2 = (x1 * r * g_ref[...]) * (1.0 + mod_ref[4:5, :]) + mod_ref[3:4, :]
        mix_ref[...] = mix
        x1_ref[...] = x1
        h2_ref[...] = h2.astype(BF16)

    row = pl.BlockSpec((tm, D), lambda i: (i, 0))
    return pl.pallas_call(
        body, name="out_proj_fwd", grid=(S // tm,),
        in_specs=[row, pl.BlockSpec((D, D), lambda i: (0, 0)), row, pl.BlockSpec((8, D), lambda i: (0, 0)),
                  pl.BlockSpec((1, D), lambda i: (0, 0))],
        out_specs=[row, row, row],
        out_shape=[_sds((S, D), F32), _sds((S, D), F32), _sds((S, D), BF16)],
    )(merged, w_out, x, mod, g_ffn)


def ffn_up_fwd(h2, w_gate, w_up):
    tm = 1024

    def body(h_ref, wg_ref, wu_ref, a_ref, u_ref, z_ref):
        h = h_ref[...]
        a = jnp.dot(h, wg_ref[...], preferred_element_type=F32)
        u = jnp.dot(h, wu_ref[...], preferred_element_type=F32)
        a_ref[...] = a
        u_ref[...] = u
        z_ref[...] = (a * _sigmoid(a) * u).astype(BF16)

    out = pl.BlockSpec((tm, FF_PAD), lambda k, i: (i, k))
    return pl.pallas_call(
        body, name="ffn_up_fwd", grid=(4, S // tm),
        in_specs=[pl.BlockSpec((tm, D), lambda k, i: (i, 0)), pl.BlockSpec((None, D, FF_PAD), lambda k, i: (k, 0, 0)),
                  pl.BlockSpec((None, D, FF_PAD), lambda k, i: (k, 0, 0))],
        out_specs=[out, out, out],
        out_shape=[_sds((S, FFP), F32), _sds((S, FFP), F32), _sds((S, FFP), BF16)], compiler_params=_params(VMEM_MB),
    )(h2, w_gate, w_up)


def ffn_down_loss(z, w_down, x1, mod, g_final, tgt):
    tm = 256

    def body(z_ref, w_ref, x1_ref, mod_ref, g_ref, t_ref, dx2_ref, dffn_ref, dg_ref, dga_ref, loss_ref, s_dg, s_dga, s_loss):
        i = pl.program_id(0)

        @pl.when(i == 0)
        def _():
            s_dg[...] = jnp.zeros_like(s_dg)
            s_dga[...] = jnp.zeros_like(s_dga)
            s_loss[...] = jnp.zeros_like(s_loss)

        ffn = jnp.dot(z_ref[...], w_ref[...], preferred_element_type=F32)
        gaf = mod_ref[5:6, :]
        x2 = x1_ref[...] + gaf * ffn
        r = lax.rsqrt(jnp.mean(x2 * x2, axis=1, keepdims=True) + EPS)
        xh = x2 * r
        g = g_ref[...]
        e = xh * g - t_ref[...]
        s_loss[...] += 0.5 * jnp.sum(jnp.mean(e * e, axis=1, keepdims=True), axis=0, keepdims=True)
        dy = e * (1.0 / D)
        gdy = dy * g
        dx2 = r * (gdy - xh * jnp.mean(gdy * xh, axis=1, keepdims=True))
        s_dg[...] += _colsum8(dy * xh)
        s_dga[...] += _colsum8(dx2 * ffn)
        dx2_ref[...] = dx2
        dffn_ref[...] = (dx2 * gaf).astype(BF16)

        @pl.when(i == pl.num_programs(0) - 1)
        def _():
            dg_ref[...] = jnp.sum(s_dg[...], axis=0, keepdims=True)
            dga_ref[...] = jnp.sum(s_dga[...], axis=0, keepdims=True)
            loss_ref[...] = jnp.broadcast_to(s_loss[...], (1, LANES))

    row = pl.BlockSpec((tm, D), lambda i: (i, 0))
    vec = pl.BlockSpec((1, D), lambda i: (0, 0))
    return pl.pallas_call(
        body, name="ffn_down_loss", grid=(S // tm,),
        in_specs=[pl.BlockSpec((tm, FFP), lambda i: (i, 0)), pl.BlockSpec((FFP, D), lambda i: (0, 0)), row,
                  pl.BlockSpec((8, D), lambda i: (0, 0)), vec, row],
        out_specs=[row, row, vec, vec, pl.BlockSpec((1, LANES), lambda i: (0, 0))],
        out_shape=[_sds((S, D), F32), _sds((S, D), BF16), _sds((1, D), F32), _sds((1, D), F32), _sds((1, LANES), F32)],
        scratch_shapes=[pltpu.VMEM((8, D), F32), pltpu.VMEM((8, D), F32), pltpu.VMEM((1, 1), F32)],
        compiler_params=_params(VMEM_MB),
    )(z, w_down, x1, mod, g_final, tgt)


def ffn_down_bwd(dffn, w_down, a, u, z):
    tm, tn = 1024, 768

    def body(d_ref, w_ref, a_ref, u_ref, z_ref, da_ref, du_ref, dw_ref):
        i = pl.program_id(1)
        dff = d_ref[...]
        dz = lax.dot_general(dff, w_ref[...], (((1,), (1,)), ((), ())), preferred_element_type=F32)
        av, uv = a_ref[...], u_ref[...]
        sg = _sigmoid(av)
        du_ref[...] = (dz * (av * sg)).astype(BF16)
        da_ref[...] = (dz * uv * (sg * (1.0 + av * (1.0 - sg)))).astype(BF16)
        dw = lax.dot_general(z_ref[...], dff, (((0,), (0,)), ((), ())), preferred_element_type=F32)

        @pl.when(i == 0)
        def _():
            dw_ref[...] = dw

        @pl.when(i > 0)
        def _():
            dw_ref[...] += dw

    tile = pl.BlockSpec((tm, tn), lambda j, i: (i, j))
    return pl.pallas_call(
        body, name="ffn_down_bwd", grid=(FFP // tn, S // tm),
        in_specs=[pl.BlockSpec((tm, D), lambda j, i: (i, 0)), pl.BlockSpec((tn, D), lambda j, i: (j, 0)), tile, tile, tile],
        out_specs=[tile, tile, pl.BlockSpec((tn, D), lambda j, i: (j, 0))],
        out_shape=[_sds((S, FFP), BF16), _sds((S, FFP), BF16), _sds((FFP, D), F32)], compiler_params=_params(VMEM_MB),
    )(dffn, w_down, a, u, z)


def mm_nt(dy, w, name, comm=None):
    tm = 1024
    n = dy.shape[1]
    if w.ndim == 2:
        k_in, tk = w.shape[0], 768
        w_spec = pl.BlockSpec((k_in, tk), lambda i, k: (0, k))
    else:
        k_in, tk = w.shape[1], FF_PAD
        w_spec = pl.BlockSpec((None, k_in, tk), lambda i, k: (k, 0, 0))
    nk = n // tk

    def body(d_ref, w_ref, o_ref, acc):
        k = pl.program_id(1)
        part = lax.dot_general(d_ref[...], w_ref[...], (((1,), (1,)), ((), ())), preferred_element_type=F32)

        @pl.when(k == 0)
        def _():
            acc[...] = part

        @pl.when(k > 0)
        def _():
            acc[...] += part

        @pl.when(k == nk - 1)
        def _():
            o_ref[...] = acc[...]

    return _hosted_call(
        body, comm, (dy, w), name=name, grid=(S // tm, nk),
        in_specs=[pl.BlockSpec((tm, tk), lambda i, k: (i, k)), w_spec],
        out_specs=pl.BlockSpec((tm, k_in), lambda i, k: (i, 0)),
        out_shape=_sds((S, k_in), F32),
        scratch_shapes=[pltpu.VMEM((tm, k_in), F32)], vmem_mb=VMEM_MB)


def mm_tn(h, dy, name, shard_major=False, comm=None):
    tm, tn = 2048, 768
    k_in, n = h.shape[1], dy.shape[1]

    def body(h_ref, d_ref, o_ref):
        i = pl.program_id(1)
        dw = lax.dot_general(h_ref[...], d_ref[...], (((0,), (0,)), ((), ())), preferred_element_type=F32)

        @pl.when(i == 0)
        def _():
            o_ref[...] = dw

        @pl.when(i > 0)
        def _():
            o_ref[...] += dw

    if shard_major:
        out_spec, out_shape = pl.BlockSpec((None, k_in, tn), lambda j, i: (j, 0, 0)), _sds((n // tn, k_in, tn), F32)
    else:
        out_spec, out_shape = pl.BlockSpec((k_in, tn), lambda j, i: (0, j)), _sds((k_in, n), F32)
    return _hosted_call(
        body, comm, (h, dy), name=name, grid=(n // tn, S // tm),
        in_specs=[pl.BlockSpec((tm, k_in), lambda j, i: (i, 0)), pl.BlockSpec((tm, tn), lambda j, i: (i, j))],
        out_specs=out_spec, out_shape=out_shape, vmem_mb=VMEM_MB)


def mid_bwd(dh2a, dh2b, x1, dx2, mix, mod, g_ffn, p, ya, ybp, merged, ya_att, yb, w_out, w_bra, w_brb, comm=None):
    tm = 256
    gblk = LAY_G // D
    nsteps = S // tm

    def body(dha_ref, dhb_ref, x1_ref, dx2_ref, mix_ref, mod_ref, g_ref, ga_ref, gb_ref, ya_ref, yb_ref, mg_ref,
             att_ref, ybb_ref, wo_ref, wa_ref, wb_ref,
             dx1_ref, dpg_ref, datt_ref, dyb_ref, cs_ref, dwo_ref, dwa_ref, dwb_ref, s_cs):
        i = pl.program_id(0)

        @pl.when(i == 0)
        def _():
            s_cs[...] = jnp.zeros_like(s_cs)
            dwo_ref[...] = jnp.zeros_like(dwo_ref)
            dwa_ref[...] = jnp.zeros_like(dwa_ref)
            dwb_ref[...] = jnp.zeros_like(dwb_ref)

        x1 = x1_ref[...]
        g = g_ref[...]
        r = lax.rsqrt(jnp.mean(x1 * x1, axis=1, keepdims=True) + EPS)
        xh = x1 * r
        dh2 = dha_ref[...] + dhb_ref[...]
        s_cs[0] += _colsum8(dh2)
        s_cs[1] += _colsum8(dh2 * (xh * g))
        dn2 = dh2 * (1.0 + mod_ref[4:5, :])
        s_cs[2] += _colsum8(dn2 * xh)
        gd = dn2 * g
        dx1 = dx2_ref[...] + r * (gd - xh * jnp.mean(gd * xh, axis=1, keepdims=True))
        s_cs[3] += _colsum8(dx1 * mix_ref[...])
        dx1_ref[...] = dx1
        dmix = (dx1 * mod_ref[2:3, :]).astype(BF16)
        dmg = lax.dot_general(dmix, wo_ref[...], (((1,), (1,)), ((), ())), preferred_element_type=F32)
        sga, sgb = _sigmoid(ga_ref[...]), _sigmoid(gb_ref[...])
        dya = (dmg * sga).astype(BF16)
        dybp = (dmg * sgb).astype(BF16)
        dpg_ref[:, 0:D] = (dmg * ya_ref[...] * (sga * (1.0 - sga))).astype(BF16)
        dpg_ref[:, D:2 * D] = (dmg * yb_ref[...] * (sgb * (1.0 - sgb))).astype(BF16)
        datt_ref[...] = lax.dot_general(dya, wa_ref[...], (((1,), (1,)), ((), ())), preferred_element_type=F32).astype(BF16)
        dyb_ref[...] = lax.dot_general(dybp, wb_ref[...], (((1,), (1,)), ((), ())), preferred_element_type=F32)
        tn_dims = (((0,), (0,)), ((), ()))
        dwo_ref[...] += lax.dot_general(mg_ref[...], dmix, tn_dims, preferred_element_type=F32)
        dwa_ref[...] += lax.dot_general(att_ref[...], dya, tn_dims, preferred_element_type=F32)
        dwb_ref[...] += lax.dot_general(ybb_ref[...].astype(BF16), dybp, tn_dims, preferred_element_type=F32)

        @pl.when(i == nsteps - 1)
        def _():
            for t in range(4):
                cs_ref[t:t + 1, :] = jnp.sum(s_cs[t], axis=0, keepdims=True)
            cs_ref[4:8, :] = jnp.zeros((4, D), F32)

    row = lambda w: pl.BlockSpec((tm, w), lambda i: (i, 0))
    full = lambda a, b: pl.BlockSpec((a, b), lambda i: (0, 0))
    return _hosted_call(
        body, comm, (dh2a, dh2b, x1, dx2, mix, mod, g_ffn, p, p, ya, ybp, merged, ya_att, yb, w_out, w_bra, w_brb),
        name="mid_bwd", grid=(nsteps,),
        in_specs=[row(D), row(D), row(D), row(D), row(D), full(8, D), full(1, D),
                  pl.BlockSpec((tm, D), lambda i: (i, gblk)), pl.BlockSpec((tm, D), lambda i: (i, gblk + 1)),
                  row(D), row(D), row(D), row(512), row(256), full(D, D), full(512, D), full(256, D)],
        out_specs=[row(D), pl.BlockSpec((tm, 2 * D), lambda i: (i, LAY_G // (2 * D))), row(512), row(256), full(8, D),
                   full(D, D), full(512, D), full(256, D)],
        out_shape=[_sds((S, D), F32), _sds((S, LAY_N), BF16), _sds((S, 512), BF16), _sds((S, 256), F32), _sds((8, D), F32),
                   _sds((D, D), F32), _sds((512, D), F32), _sds((256, D), F32)],
        scratch_shapes=[pltpu.VMEM((4, 8, D), F32)],
        vmem_mb=VMEM_MB)


def fox_bwd(p, do, o, gcol, fcol, dp, comm=None):
    nq = S // FT
    nt = (((1,), (1,)), ((), ()))
    tn = (((0,), (0,)), ((), ()))

    def body(qkv_ref, do_ref, o_ref, g_ref, fc_ref, dp_in, dp_ref, df_ref, rs_ref, dq_s, qa_s, ka_s, dob_s, dl_s):
        del dp_in
        _fox_operands(qkv_ref, g_ref, fc_ref, qa_s, ka_s)
        masks = _head_masks(FT)
        lane = lax.broadcasted_iota(jnp.int32, (FT, LANES), 1)
        head0 = 2 * pl.program_id(0)
        causal = lax.broadcasted_iota(jnp.int32, (FT, FT), 1) <= lax.broadcasted_iota(jnp.int32, (FT, FT), 0)
        dq_s[...] = jnp.zeros_like(dq_s)
        rs_ref[...] = jnp.zeros_like(rs_ref)

        causal2 = jnp.concatenate([causal, causal], axis=0)

        def prep(i, _):
            r = pl.ds(pl.multiple_of(i * 256, 256), 256)
            m256 = _head_masks(256)
            dov, ov = do_ref[r, :].astype(F32), o_ref[r, :].astype(F32)
            for hh in range(2):
                dom = jnp.where(m256[hh], dov, 0.0)
                dob_s[hh, r, :] = dom.astype(BF16)
                dl_s[hh, r, :] = jnp.broadcast_to(jnp.sum(dom * ov, axis=1, keepdims=True), (256, LANES))
            return 0

        lax.fori_loop(0, S // 256, prep, 0)

        def stack(ref, q0, cols=slice(None)):
            return jnp.concatenate([ref[0, pl.ds(q0, FT), cols], ref[1, pl.ds(q0, FT), cols]], axis=0)

        def kloop(kb, _):
            k0 = pl.multiple_of(kb * FT, FT)
            k = qkv_ref[pl.ds(k0, FT), LANES:2 * LANES].astype(BF16)
            v = qkv_ref[pl.ds(k0, FT), 2 * LANES:3 * LANES].astype(BF16)
            ka = ka_s[pl.ds(k0, FT), :]

            def step(qi, carry, diagonal):
                dk, dv, df0, df1 = carry
                q0 = pl.multiple_of(qi * FT, FT)
                qa, dob = stack(qa_s, q0), stack(dob_s, q0)
                s = lax.dot_general(qa, ka, nt, preferred_element_type=F32)
                pr = jnp.exp(jnp.where(causal2, s, NEG)) if diagonal else jnp.exp(s)
                dpr = lax.dot_general(dob, v, nt, preferred_element_type=F32)
                ds = pr * (dpr - jnp.tile(stack(dl_s, q0), (1, FT // LANES)))
                dsb = ds.astype(BF16)
                dq = jnp.dot(dsb, k, preferred_element_type=F32) * SCALE
                dk = dk + lax.dot_general(dsb, qa[:, 0:LANES], tn, preferred_element_type=F32)
                dv = dv + lax.dot_general(pr.astype(BF16), dob, tn, preferred_element_type=F32)
                rsum = jnp.sum(ds, axis=1, keepdims=True)
                dq_s[pl.ds(q0, FT), :] += jnp.where(masks[0], dq[:FT], dq[FT:])
                rs_ref[pl.ds(q0, FT), :] += jnp.where(lane == head0, rsum[:FT], 0.0) + jnp.where(lane == head0 + 1, rsum[FT:], 0.0)
                return (dk, dv, df0 - jnp.sum(ds[:FT], axis=0, keepdims=True), df1 - jnp.sum(ds[FT:], axis=0, keepdims=True))

            z = jnp.zeros((FT, LANES), F32)
            z1 = jnp.zeros((1, FT), F32)
            carry = step(kb, (z, z, z1, z1), True)
            dk, dv, df0, df1 = lax.fori_loop(kb + 1, nq, lambda qi, cr: step(qi, cr, False), carry)
            dp_ref[pl.ds(k0, FT), LANES:2 * LANES] = dk.astype(BF16)
            dp_ref[pl.ds(k0, FT), 2 * LANES:3 * LANES] = dv.astype(BF16)
            df_ref[0:1, pl.ds(k0, FT)] = df0
            df_ref[1:2, pl.ds(k0, FT)] = df1
            return 0

        lax.fori_loop(0, S // FT, kloop, 0)
        dp_ref[:, 0:LANES] = dq_s[...].astype(BF16)

    a_blk = LAY_A // 384
    pair = pl.BlockSpec((S, LANES), lambda p_: (0, p_))
    heads = pl.BlockSpec((2, S, LANES), lambda p_: (p_, 0, 0))
    return _hosted_call(
        body, comm, (p, do, o, gcol, fcol, dp), name="fox_bwd", grid=(N_FOX_PAIRS,),
        in_specs=[pl.BlockSpec((S, 384), lambda p_: (0, a_blk + p_)), pair, pair, heads, heads, pl.BlockSpec(memory_space=pl.ANY)],
        out_specs=[pl.BlockSpec((S, 384), lambda p_: (0, a_blk + p_)), pl.BlockSpec((None, 2, S), lambda p_: (p_, 0, 0)),
                   pl.BlockSpec((None, S, LANES), lambda p_: (p_, 0, 0))],
        out_shape=[_sds((S, LAY_N), BF16), _sds((4, 2, S), F32), _sds((4, S, LANES), F32)],
        scratch_shapes=[pltpu.VMEM((S, LANES), F32), pltpu.VMEM((2, S, 2 * LANES), BF16), pltpu.VMEM((S, 2 * LANES), BF16),
                        pltpu.VMEM((2, S, LANES), BF16), pltpu.VMEM((2, S, LANES), F32)],
        aliases={5: 0}, vmem_mb=VMEM_MB)


def fgate_bwd(dfrow, dfcol, fraw, dp):
    def body(df_ref, dc_ref, f_ref, dp_in, dpf_ref, db_ref):
        del dp_in
        lane = lax.broadcasted_iota(jnp.int32, (8, S), 1)
        rsum = (dc_ref[0] + dc_ref[1]) + (dc_ref[2] + dc_ref[3])
        acc, sh = df_ref[...] + rsum.T[0:8, :], 1
        while sh < S:
            acc = acc + jnp.where(lane < S - sh, pltpu.roll(acc, S - sh, 1), 0.0)
            sh *= 2
        df = acc * _sigmoid(-f_ref[...])
        db_ref[...] = jnp.broadcast_to(jnp.sum(df, axis=1, keepdims=True), (8, LANES))
        dfc = jnp.concatenate([df, jnp.zeros((LANES - 8, S), F32)], axis=0).T
        dpf_ref[:, 0:LANES] = dfc.astype(BF16)
        dpf_ref[:, LANES:2 * LANES] = jnp.zeros((S, LANES), BF16)

    return pl.pallas_call(
        body, name="fgate_bwd", grid=(1,),
        in_specs=[pl.BlockSpec((8, S), lambda i: (0, 0)), pl.BlockSpec((4, S, LANES), lambda i: (0, 0, 0)),
                  pl.BlockSpec((8, S), lambda i: (0, 0)), pl.BlockSpec(memory_space=pl.ANY)],
        out_specs=[pl.BlockSpec((S, 2 * LANES), lambda i: (0, LAY_F // (2 * LANES))), pl.BlockSpec((8, LANES), lambda i: (0, 0))],
        out_shape=[_sds((S, LAY_N), BF16), _sds((8, LANES), F32)],
        input_output_aliases={3: 0},
        compiler_params=_params(VMEM_MB),
    )(dfrow, dfcol, fraw, dp)


def dil_bwd(p, dyb, yb, lse, tabs, dp, comm=None):
    def body(*refs):
        qkv = [refs[3 * g:3 * g + 3] for g in range(3)]
        dy_ref, y_ref, lse_ref, c_ref, s1_ref, s2_ref = refs[9:15]
        dp_ref = refs[16]
        dq_s, dk_s, dv_s, dl_s = refs[17:21]
        masks = _head_masks(SPAN)
        m256 = _head_masks(256)
        nt = (((1,), (1,)), ((), ()))
        tn = (((0,), (0,)), ((), ()))
        dk_s[...] = jnp.zeros_like(dk_s)
        dv_s[...] = jnp.zeros_like(dv_s)

        def prep(i, _):
            rows = pl.ds(pl.multiple_of(i * 256, 256), 256)
            pr = dy_ref[rows, :] * y_ref[rows, :]
            d0 = jnp.sum(jnp.where(m256[0], pr, 0.0), axis=1, keepdims=True)
            d1 = jnp.sum(jnp.where(m256[1], pr, 0.0), axis=1, keepdims=True)
            dl_s[rows, :] = jnp.where(m256[0], d0, d1)
            return 0

        lax.fori_loop(0, S // 256, prep, 0)

        for g, (d, nb) in enumerate(DIL_GROUPS):
            q_ref, k_ref, v_ref = qkv[g]

            def blk(n, _):
                start, prev, has_prev = _dil_block(n, d, nb)
                q = _dil_rows(q_ref, start, d)
                kc = jnp.concatenate([_dil_rows(k_ref, prev, d), _dil_rows(k_ref, start, d)], axis=0).astype(BF16)
                vc = jnp.concatenate([_dil_rows(v_ref, prev, d), _dil_rows(v_ref, start, d)], axis=0).astype(BF16)
                dov = _dil_rows(dy_ref, start, d)
                lsev = _dil_rows(lse_ref, start, d)
                dlv = _dil_rows(dl_s, start, d)
                valid = _band_mask(has_prev)
                valid2 = jnp.concatenate([valid, valid], axis=0)

                def stack(t):
                    return jnp.concatenate([jnp.where(masks[0], t, 0.0), jnp.where(masks[1], t, 0.0)], axis=0)

                def column(t):
                    return jnp.concatenate([jnp.max(jnp.where(masks[hh], t, NEG), axis=1, keepdims=True) for hh in range(2)], axis=0)

                q2 = (stack(q) * SCALE).astype(BF16)
                dob = stack(dov).astype(BF16)
                s = jnp.where(valid2, lax.dot_general(q2, kc, nt, preferred_element_type=F32), NEG)
                pr = jnp.exp(s - column(lsev))
                dpr = lax.dot_general(dob, vc, nt, preferred_element_type=F32)
                dsb = (pr * (dpr - column(dlv))).astype(BF16)
                dq = jnp.dot(dsb, kc, preferred_element_type=F32) * SCALE
                dkc = lax.dot_general(dsb, q2, tn, preferred_element_type=F32)
                dvc = lax.dot_general(pr.astype(BF16), dob, tn, preferred_element_type=F32)
                _dil_store(dq_s.at[g], start, d, jnp.where(masks[0], dq[:SPAN], dq[SPAN:]))
                for ref, val in ((dk_s.at[g], dkc), (dv_s.at[g], dvc)):
                    _dil_store(ref, prev, d, _dil_rows(ref, prev, d) + jnp.where(has_prev, val[0:SPAN], 0.0))
                    _dil_store(ref, start, d, _dil_rows(ref, start, d) + val[SPAN:])
                return 0

            lax.fori_loop(0, 16, blk, 0)

        def fin(i, _):
            rows = pl.ds(pl.multiple_of(i * 256, 256), 256)
            c, s1, s2 = c_ref[rows, :], s1_ref[rows, :], s2_ref[rows, :]
            for g in range(3):
                base = g * 384
                dp_ref[rows, base:base + LANES] = _rope_bwd(dq_s[g, rows, :], c, s1, s2).astype(BF16)
                dp_ref[rows, base + LANES:base + 2 * LANES] = _rope_bwd(dk_s[g, rows, :], c, s1, s2).astype(BF16)
                dp_ref[rows, base + 2 * LANES:base + 3 * LANES] = dv_s[g, rows, :].astype(BF16)
            return 0

        lax.fori_loop(0, S // 256, fin, 0)

    def spec(g, t):
        return pl.BlockSpec((S, LANES), lambda p_: (0, (p_ * 3 + g) * 3 + t))

    pair = pl.BlockSpec((S, LANES), lambda p_: (0, p_))
    tab = pl.BlockSpec((S, LANES), lambda p_: (0, 0))
    return _hosted_call(
        body, comm, [p] * 9 + [dyb, yb, lse, *tabs, dp], name="dil_bwd", grid=(N_DIL_PAIRS,),
        in_specs=[spec(g, t) for g in range(3) for t in range(3)] + [pair, pair, pair, tab, tab, tab, pl.BlockSpec(memory_space=pl.ANY)],
        out_specs=pl.BlockSpec((S, 1152), lambda p_: (0, p_)),
        out_shape=_sds((S, LAY_N), BF16),
        scratch_shapes=[pltpu.VMEM((3, S, LANES), F32)] * 3 + [pltpu.VMEM((S, LANES), F32)],
        aliases={15: 0}, vmem_mb=VMEM_MB)


def in_bwd_tail(dh1, x, dx1, mod, g_mix, comm=None):
    tm = 256
    nsteps = S // tm

    def body(dh_ref, x_ref, dx1_ref, mod_ref, g_ref, dx_ref, cs_ref, s_cs):
        i = pl.program_id(0)

        @pl.when(i == 0)
        def _():
            s_cs[...] = jnp.zeros_like(s_cs)

        xv, g, dh = x_ref[...], g_ref[...], dh_ref[...]
        r = lax.rsqrt(jnp.mean(xv * xv, axis=1, keepdims=True) + EPS)
        xh = xv * r
        s_cs[0] += _colsum8(dh)
        s_cs[1] += _colsum8(dh * (xh * g))
        dn = dh * (1.0 + mod_ref[1:2, :])
        s_cs[2] += _colsum8(dn * xh)
        gd = dn * g
        dx_ref[...] = dx1_ref[...] + r * (gd - xh * jnp.mean(gd * xh, axis=1, keepdims=True))

        @pl.when(i == nsteps - 1)
        def _():
            for t in range(3):
                cs_ref[t:t + 1, :] = jnp.sum(s_cs[t], axis=0, keepdims=True)
            cs_ref[3:8, :] = jnp.zeros((5, D), F32)

    row = pl.BlockSpec((tm, D), lambda i: (i, 0))
    return _hosted_call(
        body, comm, (dh1, x, dx1, mod, g_mix), name="in_bwd_tail", grid=(nsteps,),
        in_specs=[row, row, row, pl.BlockSpec((8, D), lambda i: (0, 0)), pl.BlockSpec((1, D), lambda i: (0, 0))],
        out_specs=[row, pl.BlockSpec((8, D), lambda i: (0, 0))],
        out_shape=[_sds((S, D), F32), _sds((8, D), F32)],
        scratch_shapes=[pltpu.VMEM((3, 8, D), F32)])


def _lay_pieces():
    out = []
    qa, ka, va, fa, qb, kb, vb, ga = 0, 512, 1024, 1536, 1544, 2312, 3080, 3848
    for p in range(N_DIL_PAIRS):
        for g in range(3):
            base = LAY_B + (p * 3 + g) * 384
            hd0 = (4 * g + 2 * p) * HD
            out += [(base, qb + hd0, LANES), (base + LANES, kb + hd0, LANES), (base + 2 * LANES, vb + hd0, LANES)]
    for p in range(N_FOX_PAIRS):
        base = LAY_A + p * 384
        out += [(base, qa + p * LANES, LANES), (base + LANES, ka + p * LANES, LANES), (base + 2 * LANES, va + p * LANES, LANES)]
    out.append((LAY_F, fa, 8))
    out.append((LAY_G, ga, 2 * D))
    return out


def lay_from_nat(w_nat):
    parts, pos = [], 0
    for lay, nat, width in sorted(_lay_pieces()):
        if lay > pos:
            parts.append(jnp.zeros((w_nat.shape[0], lay - pos), w_nat.dtype))
        parts.append(w_nat[:, nat:nat + width])
        pos = lay + width
    if pos < LAY_N:
        parts.append(jnp.zeros((w_nat.shape[0], LAY_N - pos), w_nat.dtype))
    return jnp.concatenate(parts, axis=1)


def nat_from_lay(w_lay):
    parts = [w_lay[:, lay:lay + width] for lay, nat, width in sorted(_lay_pieces(), key=lambda t: t[1])]
    return jnp.concatenate(parts, axis=1)


def _shard_runs():
    runs = []
    for lay, nat, width in _lay_pieces():
        while width:
            k, loc = nat // IN_SHARD, nat % IN_SHARD
            w = min(width, IN_SHARD - loc)
            runs.append((lay, k, loc, w))
            lay, nat, width = lay + w, nat + w, width - w
    return runs


def lay_from_shards(g):
    tm = 256

    def body(g_ref, o_ref):
        o_ref[:, LAY_F:LAY_G] = jnp.zeros((tm, LAY_G - LAY_F), g.dtype)
        for lay, k, loc, w in _shard_runs():
            o_ref[:, lay:lay + w] = g_ref[k, :, loc:loc + w]

    return pl.pallas_call(
        body, name="lay_from_shards", grid=(D // tm,),
        in_specs=[pl.BlockSpec((4, tm, IN_SHARD_PAD), lambda i: (0, i, 0))],
        out_specs=pl.BlockSpec((tm, LAY_N), lambda i: (i, 0)),
        out_shape=_sds((D, LAY_N), g.dtype), compiler_params=_params(VMEM_MB),
    )(g)


def shards_from_lay(dw_lay):
    tm = 256

    def body(x_ref, o_ref):
        o_ref[:, :, IN_SHARD:] = jnp.zeros((4, tm, IN_SHARD_PAD - IN_SHARD), F32)
        for lay, k, loc, w in _shard_runs():
            o_ref[k, :, loc:loc + w] = x_ref[:, lay:lay + w]

    return pl.pallas_call(
        body, name="shards_from_lay", grid=(D // tm,),
        in_specs=[pl.BlockSpec((tm, LAY_N), lambda i: (i, 0))],
        out_specs=pl.BlockSpec((4, tm, IN_SHARD_PAD), lambda i: (0, i, 0)),
        out_shape=_sds((4, D, IN_SHARD_PAD), F32), compiler_params=_params(VMEM_MB),
    )(dw_lay)


def _pos():
    return lax.axis_index("x"), lax.axis_index("y"), lax.axis_index("c")


def _other_chips(x, y):
    return [(1 - x, y), (x, 1 - y), (1 - x, 1 - y)]


def _remote(src, dst, send_sem, recv_sem, dev):
    return pltpu.make_async_remote_copy(src_ref=src, dst_ref=dst, send_sem=send_sem, recv_sem=recv_sem,
                                        device_id=dev, device_id_type=MESH)


VMEM_SPEC = pl.BlockSpec(memory_space=pltpu.VMEM)
ANY_SPEC = pl.BlockSpec(memory_space=pl.ANY)


def gather_all(v, name, with_sum):
    r = v.shape[0]

    def body(v_ref, out_ref, *rest):
        send_s, recv_s = rest[-2:]
        x, y, c = _pos()
        me = 4 * x + 2 * y + c
        out_ref[me] = v_ref[...]
        peers = []
        for m in range(1, 8):
            px = 1 - x if m & 4 else x
            py = 1 - y if m & 2 else y
            pc = 1 - c if m & 1 else c
            peers.append((px, py, pc))
        copies = [_remote(v_ref, out_ref.at[me], send_s.at[i], recv_s.at[i], dev) for i, dev in enumerate(peers)]
        for cp in copies:
            cp.start()
        for i, (px, py, pc) in enumerate(peers):
            _remote(v_ref, out_ref.at[4 * px + 2 * py + pc], send_s.at[i], recv_s.at[i], (px, py, pc)).wait_recv()
        for cp in copies:
            cp.wait_send()
        if with_sum:
            acc = out_ref[0]
            for b in range(1, 8):
                acc = acc + out_ref[b]
            rest[0][...] = acc

    out_shape = [_sds((8, r, LANES), F32)] + ([_sds((r, LANES), F32)] if with_sum else [])
    return pl.pallas_call(
        body, name=name, in_specs=[VMEM_SPEC], out_specs=[VMEM_SPEC] * len(out_shape), out_shape=out_shape,
        scratch_shapes=[pltpu.SemaphoreType.DMA((7,)), pltpu.SemaphoreType.DMA((7,))],
    )(v)


def mod_exchange(c_all, w_ada_sh, b_sh):
    def body(c_ref, w_ref, b_ref, out_ref, sc_ref, modp, send_s, recv_s):
        cv = c_ref[...]
        sc = cv * _sigmoid(cv)
        sc_ref[...] = sc
        modp[...] = jnp.dot(sc, w_ref[...], precision=lax.Precision.HIGHEST, preferred_element_type=F32) + b_ref[...]
        x, y, c = _pos()
        k = 2 * x + y
        out_ref[k] = modp[...]
        chips = _other_chips(x, y)
        copies = [_remote(modp, out_ref.at[k], send_s.at[j], recv_s.at[j], (cx, cy, c)) for j, (cx, cy) in enumerate(chips)]
        for cp in copies:
            cp.start()
        for j, (cx, cy) in enumerate(chips):
            _remote(modp, out_ref.at[2 * cx + cy], send_s.at[j], recv_s.at[j], (cx, cy, c)).wait_recv()
        for cp in copies:
            cp.wait_send()

    n = w_ada_sh.shape[1]
    return pl.pallas_call(
        body, name="mod_exchange", in_specs=[VMEM_SPEC] * 3, out_specs=[VMEM_SPEC] * 2,
        out_shape=[_sds((4, 8, n), F32), _sds((8, D), F32)],
        scratch_shapes=[pltpu.VMEM((8, n), F32), pltpu.SemaphoreType.DMA((3,)), pltpu.SemaphoreType.DMA((3,))],
        compiler_params=_params(VMEM_MB),
    )(c_all, w_ada_sh, b_sh)


def gather_weights(bufs):
    n = len(bufs)

    def body(*refs):
        outs = refs[n:2 * n]
        send_s, recv_s, fsend_s, frecv_s = refs[2 * n:]
        x, y, c = _pos()
        k = 2 * x + y
        chips = _other_chips(x, y)
        sends, fwds = [], []
        for a in range(n):
            half = outs[a].shape[1] // 2
            rows = pl.ds(c * half, half)
            for j, (cx, cy) in enumerate(chips):
                cp = _remote(outs[a].at[k, rows], outs[a].at[k, rows], send_s.at[3 * a + j], recv_s.at[3 * a + j], (cx, cy, c))
                cp.start()
                sends.append(cp)
        for a in range(n):
            half = outs[a].shape[1] // 2
            rows = pl.ds(c * half, half)
            for j, (cx, cy) in enumerate(chips):
                kj = 2 * cx + cy
                _remote(outs[a].at[kj, rows], outs[a].at[kj, rows], send_s.at[3 * a + j], recv_s.at[3 * a + j], (cx, cy, c)).wait_recv()
                fw = _remote(outs[a].at[kj, rows], outs[a].at[kj, rows], fsend_s.at[3 * a + j], frecv_s.at[3 * a + j], (x, y, 1 - c))
                fw.start()
                fwds.append(fw)
        for a in range(n):
            half = outs[a].shape[1] // 2
            orows = pl.ds((1 - c) * half, half)
            for j, (cx, cy) in enumerate(chips):
                kj = 2 * cx + cy
                _remote(outs[a].at[kj, orows], outs[a].at[kj, orows], fsend_s.at[3 * a + j], frecv_s.at[3 * a + j], (x, y, 1 - c)).wait_recv()
        for cp in sends + fwds:
            cp.wait_send()

    return pl.pallas_call(
        body, name="gather_weights", in_specs=[ANY_SPEC] * n, out_specs=[ANY_SPEC] * n,
        out_shape=[_sds(b.shape, b.dtype) for b in bufs],
        scratch_shapes=[pltpu.SemaphoreType.DMA((3 * n,))] * 4,
        input_output_aliases={a: a for a in range(n)},
    )(*bufs)


def cast_into_slab(w, rows, cols, chip, tag):
    r0, c0 = w.shape
    tr = 64
    n_in, n_out = r0 // tr, rows // tr

    def body(chip_ref, w_ref, o_ref):
        del chip_ref
        i = pl.program_id(0)

        @pl.when(i < n_in)
        def _():
            o_ref[:, 0:c0] = w_ref[...].astype(BF16)
            if cols > c0:
                o_ref[:, c0:] = jnp.zeros((tr, cols - c0), BF16)

        @pl.when(i >= n_in)
        def _():
            o_ref[...] = jnp.zeros((tr, cols), BF16)

    grid_spec = pltpu.PrefetchScalarGridSpec(
        num_scalar_prefetch=1, grid=(n_out,),
        in_specs=[pl.BlockSpec((tr, c0), lambda i, k: (jnp.minimum(i, n_in - 1), 0))],
        out_specs=pl.BlockSpec((None, tr, cols), lambda i, k: (k[0], i, 0)))
    return pl.pallas_call(body, name="cast_" + tag, grid_spec=grid_spec, out_shape=_sds((4, rows, cols), BF16))(chip, w)


def _row_tile(rows, cap=256):
    t = cap
    while rows % t or t % 8:
        t -= 8
    return t


def _comm_wait(sends, recvs, local=()):
    for cp in recvs:
        cp.wait_recv()
    for cp in sends:
        cp.wait_send()
    for cp in local:
        cp.wait()


def ag_ici(bufs):
    n = len(bufs)

    def copies(ins, outs, sems):
        send_s, recv_s = sems
        x, y, c = _pos()
        k = 2 * x + y
        sends, recvs = [], []
        for a in range(n):
            half = outs[a].shape[1] // 2
            rows = pl.ds(c * half, half)
            for j, (cx, cy) in enumerate(_other_chips(x, y)):
                sem = (send_s.at[3 * a + j], recv_s.at[3 * a + j], (cx, cy, c))
                sends.append(_remote(outs[a].at[k, rows], outs[a].at[k, rows], *sem))
                recvs.append(_remote(outs[a].at[k, rows], outs[a].at[2 * cx + cy, rows], *sem))
        return sends, recvs

    def start(ins, outs, sems):
        for cp in copies(ins, outs, sems)[0]:
            cp.start()

    def wait(ins, outs, sems):
        _comm_wait(*copies(ins, outs, sems))

    return Comm(bufs, [_sds(b.shape, b.dtype) for b in bufs], [3 * n, 3 * n], start, wait, aliases={a: a for a in range(n)})


def ag_d2d(bufs):
    n = len(bufs)

    def copies(ins, outs, sems):
        send_s, recv_s = sems
        x, y, c = _pos()
        sends, recvs = [], []
        for a in range(n):
            half = outs[a].shape[1] // 2
            rows, orows = pl.ds(c * half, half), pl.ds((1 - c) * half, half)
            for j, (cx, cy) in enumerate(_other_chips(x, y)):
                kj = 2 * cx + cy
                sem = (send_s.at[3 * a + j], recv_s.at[3 * a + j], (x, y, 1 - c))
                sends.append(_remote(outs[a].at[kj, rows], outs[a].at[kj, rows], *sem))
                recvs.append(_remote(outs[a].at[kj, orows], outs[a].at[kj, orows], *sem))
        return sends, recvs

    def start(ins, outs, sems):
        for cp in copies(ins, outs, sems)[0]:
            cp.start()

    def wait(ins, outs, sems):
        _comm_wait(*copies(ins, outs, sems))

    return Comm(bufs, [_sds(b.shape, b.dtype) for b in bufs], [3 * n, 3 * n], start, wait, aliases={a: a for a in range(n)})


def rs_a(grads):
    n = len(grads)

    def copies(ins, outs, sems):
        send_s, recv_s = sems
        x, y, c = _pos()
        cps = []
        for a in range(n):
            half = ins[a].shape[1] // 2
            cps.append(_remote(ins[a].at[:, pl.ds((1 - c) * half, half), :], outs[a], send_s.at[a], recv_s.at[a], (x, y, 1 - c)))
        return cps

    def start(ins, outs, sems):
        for cp in copies(ins, outs, sems):
            cp.start()

    def wait(ins, outs, sems):
        cps = copies(ins, outs, sems)
        _comm_wait(cps, cps)

    return Comm(grads, [_sds((4, g.shape[1] // 2, g.shape[2]), g.dtype) for g in grads], [n, n], start, wait)


def rs_b(pres):
    n = len(pres)

    def copies(ins, outs, sems):
        send_s, recv_s = sems
        x, y, c = _pos()
        cps = []
        for a in range(n):
            for j, (cx, cy) in enumerate(_other_chips(x, y)):
                cps.append(_remote(ins[a].at[2 * cx + cy], outs[a].at[j], send_s.at[3 * a + j], recv_s.at[3 * a + j], (cx, cy, c)))
        return cps

    def start(ins, outs, sems):
        for cp in copies(ins, outs, sems):
            cp.start()

    def wait(ins, outs, sems):
        cps = copies(ins, outs, sems)
        _comm_wait(cps, cps)

    return Comm(pres, [_sds((3,) + p_.shape[1:], p_.dtype) for p_ in pres], [3 * n, 3 * n], start, wait)


def rs_b_rows(pre, buf, lo, n):
    def copies(ins, outs, sems):
        send_s, recv_s = sems
        x, y, c = _pos()
        rows = pl.ds(lo, n)
        return [_remote(ins[0].at[2 * cx + cy, rows], outs[0].at[j, rows], send_s.at[j], recv_s.at[j], (cx, cy, c))
                for j, (cx, cy) in enumerate(_other_chips(x, y))]

    def start(ins, outs, sems):
        for cp in copies(ins, outs, sems):
            cp.start()

    def wait(ins, outs, sems):
        cps = copies(ins, outs, sems)
        _comm_wait(cps, cps)

    ins = [pre] if buf is None else [pre, buf]
    return Comm(ins, [_sds((3,) + pre.shape[1:], pre.dtype)], [3, 3], start, wait, aliases={} if buf is None else {1: 0})


def rs_c(reds):
    n = len(reds)

    def copies(ins, outs, sems):
        send_s, recv_s = sems
        x, y, c = _pos()
        sends, recvs = [], []
        for a in range(n):
            half = outs[a].shape[0] // 2
            rows, orows = pl.ds(c * half, half), pl.ds((1 - c) * half, half)
            sem = (send_s.at[a], recv_s.at[a], (x, y, 1 - c))
            sends.append(_remote(outs[a].at[rows], outs[a].at[rows], *sem))
            recvs.append(_remote(outs[a].at[orows], outs[a].at[orows], *sem))
        return sends, recvs

    def start(ins, outs, sems):
        for cp in copies(ins, outs, sems)[0]:
            cp.start()

    def wait(ins, outs, sems):
        _comm_wait(*copies(ins, outs, sems))

    return Comm(reds, [_sds(r_.shape, r_.dtype) for r_ in reds], [n, n], start, wait, aliases={a: a for a in range(n)})


def comm_join(*comms):
    ni = np.cumsum([0] + [len(c.ins) for c in comms])
    no = np.cumsum([0] + [len(c.out_shapes) for c in comms])
    ns = np.cumsum([0] + [len(c.sems) for c in comms])

    def parts(ins, outs, sems):
        return [(c, ins[ni[i]:ni[i + 1]], outs[no[i]:no[i + 1]], sems[ns[i]:ns[i + 1]]) for i, c in enumerate(comms)]

    def start(ins, outs, sems):
        for c, a, b, s in parts(ins, outs, sems):
            c.start(a, b, s)

    def wait(ins, outs, sems):
        for c, a, b, s in parts(ins, outs, sems):
            c.wait(a, b, s)

    aliases = {int(ni[i]) + k: int(no[i]) + v for i, c in enumerate(comms) for k, v in c.aliases.items()}
    return Comm(sum((c.ins for c in comms), []), sum((c.out_shapes for c in comms), []), sum((c.sems for c in comms), []),
                start, wait, aliases)


def comm_only(comm, name):
    nci, nco = len(comm.ins), len(comm.out_shapes)

    def body(*refs):
        ins, outs, sems = refs[:nci], refs[nci:nci + nco], refs[nci + nco:]
        comm.start(ins, outs, sems)
        comm.wait(ins, outs, sems)

    return pl.pallas_call(
        body, name=name, in_specs=[ANY_SPEC] * nci, out_specs=[ANY_SPEC] * nco, out_shape=comm.out_shapes,
        scratch_shapes=[pltpu.SemaphoreType.DMA((s,)) for s in comm.sems],
        input_output_aliases=comm.aliases,
    )(*comm.ins)


def rs_add_halves(g, other, core, name):
    _, r, cdim = g.shape
    half = r // 2
    tr = _row_tile(half, 256)
    nb = half // tr

    def body(core_ref, g_ref, o_ref, out_ref):
        del core_ref
        out_ref[...] = (g_ref[...] + o_ref[...]).astype(BF16)

    grid_spec = pltpu.PrefetchScalarGridSpec(
        num_scalar_prefetch=1, grid=(4, nb),
        in_specs=[pl.BlockSpec((None, tr, cdim), lambda k, i, cr: (k, cr[0] * nb + i, 0)),
                  pl.BlockSpec((None, tr, cdim), lambda k, i, cr: (k, i, 0))],
        out_specs=pl.BlockSpec((None, tr, cdim), lambda k, i, cr: (k, i, 0)))
    return pl.pallas_call(body, name=name, grid_spec=grid_spec, out_shape=_sds((4, half, cdim), BF16))(core, g, other)


def rs_add_slabs(t, pre, place, name):
    _, half, cdim = t.shape
    tr = _row_tile(half, 256)
    nb = half // tr

    def body(place_ref, own_ref, t_ref, out_ref):
        del place_ref
        out_ref[...] = ((own_ref[...].astype(F32) + t_ref[0].astype(F32)) + t_ref[1].astype(F32)) + t_ref[2].astype(F32)

    grid_spec = pltpu.PrefetchScalarGridSpec(
        num_scalar_prefetch=1, grid=(nb,),
        in_specs=[pl.BlockSpec((None, tr, cdim), lambda i, pr: (pr[0], i, 0)), pl.BlockSpec((3, tr, cdim), lambda i, pr: (0, i, 0))],
        out_specs=pl.BlockSpec((tr, cdim), lambda i, pr: (pr[1] * nb + i, 0)))
    return pl.pallas_call(body, name=name, grid_spec=grid_spec, out_shape=_sds((2 * half, cdim), F32))(place, pre, t)


def _adam_math(w, g, m, v):
    m = ADAM_B1 * m + (1.0 - ADAM_B1) * g
    v = ADAM_B2 * v + (1.0 - ADAM_B2) * (g * g)
    m_hat = m / (1.0 - ADAM_B1 ** ADAM_STEP)
    v_hat = v / (1.0 - ADAM_B2 ** ADAM_STEP)
    delta = -ADAM_LR * (m_hat / (jnp.sqrt(v_hat) + ADAM_EPS) + ADAM_WD * w)
    return delta, m, v


def adam(w, g, m, v, name, comm=None):
    r, cdim = w.shape
    tr = _row_tile(r) if r >= 8 else r

    def body(w_ref, g_ref, m_ref, v_ref, g_out, d_ref, nm_ref, nv_ref):
        gv = g_ref[:, :cdim]
        g_out[...] = gv
        d_ref[...], nm_ref[...], nv_ref[...] = _adam_math(w_ref[...], gv, m_ref[...], v_ref[...])

    blk = pl.BlockSpec((tr, cdim), lambda i: (i, 0))
    return _hosted_call(
        body, comm, (w, g, m, v), name=name, grid=(r // tr,),
        in_specs=[blk, pl.BlockSpec((tr, g.shape[1]), lambda i: (i, 0)), blk, blk],
        out_specs=[blk] * 4, out_shape=[_sds((r, cdim), F32)] * 4)


def adam_w_ada(sc_t, dmod_sh, w, m, v, comm=None):
    r, cdim = w.shape
    tr = 256

    def body(s_ref, d_ref, w_ref, m_ref, v_ref, g_ref, dl_ref, nm_ref, nv_ref):
        g = jnp.dot(s_ref[...], d_ref[...], precision=lax.Precision.HIGHEST, preferred_element_type=F32)
        g_ref[...] = g
        dl_ref[...], nm_ref[...], nv_ref[...] = _adam_math(w_ref[...], g, m_ref[...], v_ref[...])

    blk = pl.BlockSpec((tr, cdim), lambda i: (i, 0))
    return _hosted_call(
        body, comm, (sc_t, dmod_sh, w, m, v), name="adam_w_ada", grid=(r // tr,),
        in_specs=[pl.BlockSpec((tr, LANES), lambda i: (i, 0)), pl.BlockSpec((LANES, cdim), lambda i: (0, 0)), blk, blk, blk],
        out_specs=[blk] * 4, out_shape=[_sds((r, cdim), F32)] * 4)


SMALL_ROWS = 80


def kernel(x, c, w_ada, b_ada, g_mix, w_in, b_fgate, w_br_a, w_br_b, w_out, g_ffn, w_ffn_gate, w_ffn_up, w_ffn_down, g_final, loss_target, m_w_ada, m_b_ada, m_g_mix, m_w_in, m_b_fgate, m_w_br_a, m_w_br_b, m_w_out, m_g_ffn, m_w_ffn_gate, m_w_ffn_up, m_w_ffn_down, m_g_final, v_w_ada, v_b_ada, v_g_mix, v_w_in, v_b_fgate, v_w_br_a, v_w_br_b, v_w_out, v_g_ffn, v_w_ffn_gate, v_w_ffn_up, v_w_ffn_down, v_g_final):
    xi, yi, ci = _pos()
    chip = 2 * xi + yi
    seq = 4 * xi + 2 * yi + ci
    n_ada = w_ada.shape[2]

    c_all = gather_all(c.reshape(8, LANES), "gather_c", False)[0].reshape(8, D)
    b_sh = lax.dynamic_slice(b_ada, (0, chip * n_ada), (1, n_ada))
    mod_all, sc = mod_exchange(c_all, w_ada[0], b_sh)
    mod = lax.dynamic_index_in_dim(mod_all, seq, axis=1, keepdims=False).reshape(6, D)
    mod8 = jnp.pad(mod, ((0, 2), (0, 0)))

    core = ci.astype(jnp.int32).reshape(1)
    chip1 = chip.astype(jnp.int32).reshape(1)
    place = jnp.stack([chip, ci]).astype(jnp.int32)
    s_in = cast_into_slab(w_in[0], D, IN_SHARD_PAD, chip1, "w_in")
    s_bra, s_brb, s_out = (cast_into_slab(w[0], w.shape[1], w.shape[2], chip1, t)
                           for w, t in ((w_br_a, "w_br_a"), (w_br_b, "w_br_b"), (w_out, "w_out")))
    s_gate = cast_into_slab(w_ffn_gate[0], D, FF_PAD, chip1, "w_gate")
    s_up = cast_into_slab(w_ffn_up[0], D, FF_PAD, chip1, "w_up")
    s_down = cast_into_slab(w_ffn_down[0], FF_PAD, D, chip1, "w_down")
    xs, tgt, g_fin = x[0], loss_target[0], g_final.reshape(1, D)

    def halves(gs, others, tag):
        return [rs_add_halves(g, o, core, f"rs_{tag}_halves_{i}") for i, (g, o) in enumerate(zip(gs, others))]

    def slab_sums(ts, pres, tag):
        return [rs_add_slabs(t, pre, place, f"rs_{tag}_slabs_{i}") for i, (t, pre) in enumerate(zip(ts, pres))]

    g_in = gather_weights([s_in])[0]
    w_lay = lay_from_shards(g_in)
    tabs = rope_tables()
    h1 = norm_mod_fwd(xs, g_mix, mod8, 0, 1)
    p, mix_w = in_proj_fwd(h1, w_lay, tabs, comm=ag_ici([s_bra, s_brb, s_out]))
    frow, fraw, fcol = fgate_fwd(p, jnp.pad(b_fgate, ((0, 0), (0, LANES - 8))))
    (ya_att, gcol), res = fox_fwd(p, fcol, comm=comm_join(ag_d2d(mix_w), ag_ici([s_gate, s_up])))
    g_bra, g_brb, g_out = res[:3]
    (yb, lse_b), res = dil_fwd(p, comm=comm_join(ag_d2d(res[3:]), ag_ici([s_down])))
    w_gate, w_up = res[:2]
    w_bra = g_bra.transpose(1, 0, 2).reshape(512, D)
    w_brb = g_brb.transpose(1, 0, 2).reshape(256, D)
    w_o = g_out.reshape(D, D)
    (merged, ya, ybp), (g_down,) = merge_fwd(ya_att, yb, p, w_bra, w_brb, comm=ag_d2d(res[2:]))
    w_down = g_down.reshape(FFP, D)
    mix, x1, h2 = out_proj_fwd(merged, w_o, xs, mod8, g_ffn)
    a, u, z = ffn_up_fwd(h2, w_gate, w_up)
    dx2, dffn, dg_final, dga_f, loss_part = ffn_down_loss(z, w_down, x1, mod8, g_fin, tgt)

    da, du, dw_down = ffn_down_bwd(dffn, w_down, a, u, z)
    g_down = [dw_down.reshape(4, FF_PAD, D)]
    dh2a, oth = mm_nt(da, w_gate, "ffn_gate_dx", comm=rs_a(g_down))
    pre_down = halves(g_down, oth, "down")
    dh2b, _ = mm_nt(du, w_up, "ffn_up_dx")
    dw_gate, _ = mm_tn(h2, da, "ffn_gate_dw", shard_major=True)
    dw_up, _ = mm_tn(h2, du, "ffn_up_dw", shard_major=True)
    g_gu = [dw_gate, dw_up]
    (dx1, dp1, dya_att, dyb, cs_mid, dw_out, dw_bra, dw_brb), res = mid_bwd(
        dh2a, dh2b, x1, dx2, mix, mod8, g_ffn, p, ya, ybp, merged, ya_att, yb, w_o, w_bra, w_brb,
        comm=comm_join(rs_b(pre_down), rs_a(g_gu)))
    red_down = slab_sums(res[:1], pre_down, "down")
    pre_gu = halves(g_gu, res[1:], "gu")
    g_mix3 = [dw_bra.reshape(512, 4, 256).transpose(1, 0, 2), dw_brb.reshape(256, 4, 256).transpose(1, 0, 2), dw_out.reshape(4, 256, D)]
    (dp2, dfrow, dfcol), res = fox_bwd(p, dya_att, ya_att, gcol, fcol, dp1,
                                       comm=comm_join(rs_b(pre_gu), rs_c(red_down), rs_a(g_mix3)))
    red_gu = slab_sums(res[:2], pre_gu, "gu")
    r_down = res[2]
    pre_mix3 = halves(g_mix3, res[3:], "mix")
    dp3, db_fg = fgate_bwd(dfrow.reshape(8, S), dfcol, fraw, dp2)
    dp4, res = dil_bwd(p, dyb, yb, lse_b, tabs, dp3, comm=comm_join(rs_c(red_gu), rs_b(pre_mix3)))
    r_gate, r_up = res[:2]
    red_mix3 = slab_sums(res[2:], pre_mix3, "mix")
    dw_lay, (r_bra, r_brb, r_out) = mm_tn(h1, dp4, "in_proj_dw", comm=rs_c(red_mix3))
    g_in4 = [shards_from_lay(dw_lay)]
    dh1, oth = mm_nt(dp4, w_lay, "in_proj_dx", comm=rs_a(g_in4))
    (pre_in,) = halves(g_in4, oth, "in")
    qrows = pre_in.shape[1] // 4
    (dx, cs_in), (t_in,) = in_bwd_tail(dh1, xs, dx1, mod8, g_mix, comm=rs_b_rows(pre_in, None, 0, qrows))

    dmod = jnp.concatenate([cs_in[0:2], cs_mid[3:4], cs_mid[0:2], dga_f], axis=0)
    small = dict(dmod=dmod, dg_mix=cs_in[2:3], dg_ffn=cs_mid[2:3], dg_final=dg_final, db_fgate=db_fg[:, 0], loss=loss_part[0, 0])
    sv = jnp.concatenate([
        small["dmod"].reshape(48, LANES), small["dg_mix"].reshape(8, LANES), small["dg_ffn"].reshape(8, LANES),
        small["dg_final"].reshape(8, LANES), jnp.pad(small["db_fgate"], (0, LANES - 8)).reshape(1, LANES),
        jnp.broadcast_to(small["loss"], (1, LANES)), jnp.zeros((SMALL_ROWS - 74, LANES), F32)], axis=0)
    sv_all, sv_sum = gather_all(sv, "gather_small", True)
    loss = sv_sum[73, 0]
    g_small = dict(b_ada=sv_sum[0:48].reshape(1, 6 * D), g_mix=sv_sum[48:56].reshape(1, D), g_ffn=sv_sum[56:64].reshape(1, D),
                   g_final=sv_sum[64:72].reshape(D), b_fgate=sv_sum[72, 0:8].reshape(1, 8))

    dmod_all = lax.dynamic_slice(sv_all[:, 0:48, :].reshape(8, 6 * D), (0, chip * n_ada), (8, n_ada))
    (g_ada, d_ada, nm_ada, nv_ada), (t_in,) = adam_w_ada(
        jnp.pad(sc.T, ((0, 0), (0, LANES - 8))), jnp.pad(dmod_all, ((0, LANES - 8), (0, 0))), w_ada[0], m_w_ada[0], v_w_ada[0],
        comm=rs_b_rows(pre_in, t_in, qrows, qrows))

    big = dict(w_in=(w_in, m_w_in, v_w_in), w_br_a=(w_br_a, m_w_br_a, v_w_br_a), w_br_b=(w_br_b, m_w_br_b, v_w_br_b),
               w_out=(w_out, m_w_out, v_w_out), w_ffn_gate=(w_ffn_gate, m_w_ffn_gate, v_w_ffn_gate),
               w_ffn_up=(w_ffn_up, m_w_ffn_up, v_w_ffn_up), w_ffn_down=(w_ffn_down, m_w_ffn_down, v_w_ffn_down))
    gpad = dict(w_br_a=r_bra, w_br_b=r_brb, w_out=r_out, w_ffn_gate=r_gate, w_ffn_up=r_up, w_ffn_down=r_down)
    upd = {}
    for part, nm in ((2, "w_ffn_gate"), (3, "w_ffn_up")):
        w, m, v = big[nm]
        upd[nm], (t_in,) = adam(w[0], gpad[nm], m[0], v[0], "adam_" + nm, comm=rs_b_rows(pre_in, t_in, part * qrows, qrows))
    (gpad["w_in"],) = comm_only(rs_c(slab_sums([t_in], [pre_in], "in")), "rs_in_share")
    for nm, (w, m, v) in big.items():
        if nm not in upd:
            upd[nm] = adam(w[0], gpad[nm], m[0], v[0], "adam_" + nm)[0]

    def pack(gm, gf, gl, ba, bf):
        rows = [gm.reshape(1, D), gf.reshape(1, D), gl.reshape(1, D), ba.reshape(6, D), jnp.pad(bf.reshape(1, 8), ((0, 0), (0, D - 8)))]
        return jnp.concatenate(rows + [jnp.zeros((6, D), F32)], axis=0)

    packed = adam(pack(g_mix, g_ffn, g_final, b_ada, b_fgate),
                  pack(g_small["g_mix"], g_small["g_ffn"], g_small["g_final"], g_small["b_ada"], g_small["b_fgate"]),
                  pack(m_g_mix, m_g_ffn, m_g_final, m_b_ada, m_b_fgate), pack(v_g_mix, v_g_ffn, v_g_final, v_b_ada, v_b_fgate),
                  "adam_small")[0]

    def unpack(t):
        return dict(g_mix=t[0:1], g_ffn=t[1:2], g_final=t[2], b_ada=t[3:9].reshape(1, 6 * D), b_fgate=t[9:10, 0:8])

    small_upd = [unpack(t) for t in packed[1:]]
    order =["w_ada", "b_ada", "g_mix", "w_in", "b_fgate", "w_br_a", "w_br_b", "w_out", "g_ffn", "w_ffn_gate", "w_ffn_up", "w_ffn_down", "g_final"]

    def leaf(nm, which):
        if nm == "w_ada":
            return (g_ada, d_ada, nm_ada, nv_ada)[which][None]
        if nm in big:
            return upd[nm][which][None]
        return g_small[nm] if which == 0 else small_upd[which - 1][nm]

    outs = [loss, dx[None]]
    for which in range(4):
        outs += [leaf(nm, which) for nm in order]
    return tuple(outs)
```

```python
import functools

import numpy as np
import jax
import jax.numpy as jnp
from jax import lax
from jax.experimental import pallas as pl
from jax.experimental.pallas import tpu as pltpu

F32, BF16 = jnp.float32, jnp.bfloat16
S, D = 2048, 1024
HD = 64
LANES = 128
N_FOX_PAIRS, N_DIL_PAIRS = 4, 2
DIL_GROUPS = ((1, 16), (4, 4), (16, 1))
SPAN = 128
ROT_DIM, ROPE_THETA = 16, 500000.0
D_FF, FF_SHARD, FF_PAD = 2816, 704, 768
FFP = 4 * FF_PAD
IN_COLS, IN_SHARD, IN_SHARD_PAD = 5896, 1474, 1536
LAY_B, LAY_A, LAY_F, LAY_G, LAY_N = 0, 2304, 3840, 4096, 6144
EPS, NEG = 1e-6, -1e30
SCALE = HD ** -0.5
ADAM_LR, ADAM_B1, ADAM_B2, ADAM_EPS, ADAM_WD, ADAM_STEP = 0.001, 0.9, 0.999, 1e-08, 0.01, 10
VMEM_MB = 56
MESH = pl.DeviceIdType.MESH


def _params(vmem_mb=None, **kw):
    if vmem_mb is not None:
        kw["vmem_limit_bytes"] = vmem_mb * 1024 * 1024
    return pltpu.CompilerParams(**kw)


def _sds(shape, dtype):
    return jax.ShapeDtypeStruct(shape, dtype)


def _sigmoid(x):
    return 1.0 / (1.0 + jnp.exp(-x))


def _colsum8(x):
    tm, n = x.shape
    return jnp.sum(x.reshape(tm // 8, 8, n), axis=0)


class Comm:
    def __init__(self, ins, out_shapes, sems, start, wait, aliases=None):
        self.ins, self.out_shapes, self.sems = list(ins), list(out_shapes), list(sems)
        self.start, self.wait, self.aliases = start, wait, dict(aliases or {})


def _hosted_call(body, comm, args, *, name, grid, in_specs, out_specs, out_shape, scratch_shapes=(), aliases=None, vmem_mb=None):
    single = not isinstance(out_shape, (list, tuple))
    out_specs_l = [out_specs] if single else list(out_specs)
    out_shape_l = [out_shape] if single else list(out_shape)
    n_in, n_out, n_scr = len(in_specs), len(out_shape_l), len(scratch_shapes)
    aliases = dict(aliases or {})
    if comm is None:
        res = pl.pallas_call(body, name=name, grid=grid, in_specs=list(in_specs), out_specs=out_specs, out_shape=out_shape,
                             scratch_shapes=list(scratch_shapes), input_output_aliases=aliases,
                             compiler_params=_params(vmem_mb))(*args)
        return res, []
    nci, nco = len(comm.ins), len(comm.out_shapes)

    def wrapped(*refs):
        main_in, cin = refs[:n_in], refs[n_in:n_in + nci]
        o0 = n_in + nci
        main_out, cout = refs[o0:o0 + n_out], refs[o0 + n_out:o0 + n_out + nco]
        s0 = o0 + n_out + nco
        scr, sems = refs[s0:s0 + n_scr], refs[s0 + n_scr:]
        ids = [pl.program_id(i) for i in range(len(grid))]
        first = functools.reduce(jnp.logical_and, [i == 0 for i in ids])
        last = functools.reduce(jnp.logical_and, [i == g - 1 for i, g in zip(ids, grid)])

        @pl.when(first)
        def _():
            comm.start(cin, cout, sems)

        body(*main_in, *main_out, *scr)

        @pl.when(last)
        def _():
            comm.wait(cin, cout, sems)

    for ci, co in comm.aliases.items():
        aliases[n_in + ci] = n_out + co
    any_spec = pl.BlockSpec(memory_space=pl.ANY)
    res = pl.pallas_call(
        wrapped, name=name, grid=grid, in_specs=list(in_specs) + [any_spec] * nci, out_specs=out_specs_l + [any_spec] * nco,
        out_shape=out_shape_l + comm.out_shapes,
        scratch_shapes=list(scratch_shapes) + [pltpu.SemaphoreType.DMA((s,)) for s in comm.sems],
        input_output_aliases=aliases, compiler_params=_params(vmem_mb))(*args, *comm.ins)
    main = list(res[:n_out])
    return (main[0] if single else main), list(res[n_out:])


def norm_mod_fwd(x, g, mod, shift_row, scale_row, comm=None):
    tm = 256

    def body(x_ref, g_ref, mod_ref, h_ref):
        xv = x_ref[...]
        r = lax.rsqrt(jnp.mean(xv * xv, axis=1, keepdims=True) + EPS)
        n = xv * r * g_ref[...]
        h = n * (1.0 + mod_ref[scale_row:scale_row + 1, :]) + mod_ref[shift_row:shift_row + 1, :]
        h_ref[...] = h.astype(BF16)

    return _hosted_call(
        body, comm, (x, g, mod), name="norm_mod_fwd", grid=(S // tm,),
        in_specs=[pl.BlockSpec((tm, D), lambda i: (i, 0)), pl.BlockSpec((1, D), lambda i: (0, 0)),
                  pl.BlockSpec((8, D), lambda i: (0, 0))],
        out_specs=pl.BlockSpec((tm, D), lambda i: (i, 0)),
        out_shape=_sds((S, D), BF16))


def rope_tables():
    pos = jnp.arange(S, dtype=F32)
    inv_freq = ROPE_THETA ** (-jnp.arange(0, ROT_DIM, 2, dtype=F32) / ROT_DIM)
    ang = pos[:, None] * inv_freq[None, :]
    cos, sin = jnp.cos(ang), jnp.sin(ang)
    one, zero = jnp.ones((S, HD - ROT_DIM), F32), jnp.zeros((S, HD - ROT_DIM), F32)
    z8 = jnp.zeros((S, 8), F32)
    c = jnp.concatenate([cos, cos, one], axis=1)
    s1 = jnp.concatenate([-sin, z8, zero], axis=1)
    s2 = jnp.concatenate([z8, sin, zero], axis=1)
    return tuple(jnp.concatenate([t, t], axis=1) for t in (c, s1, s2))


def _rope(y, c, s1, s2):
    return y * c + pltpu.roll(y, LANES - 8, 1) * s1 + pltpu.roll(y, 8, 1) * s2


def _rope_bwd(dy, c, s1, s2):
    return dy * c + pltpu.roll(dy * s1, 8, 1) + pltpu.roll(dy * s2, LANES - 8, 1)


def in_proj_fwd(h, w_lay, tabs, comm=None):
    tm, tn = 2048, 768
    n_rope = N_DIL_PAIRS * 3 // 2

    def body(a_ref, w_ref, c_ref, s1_ref, s2_ref, o_ref):
        j = pl.program_id(0)
        y = jnp.dot(a_ref[...], w_ref[...], preferred_element_type=F32)

        @pl.when(j < n_rope)
        def _():
            c, s1, s2 = c_ref[...], s1_ref[...], s2_ref[...]
            for t in range(tn // LANES):
                chunk = y[:, LANES * t:LANES * (t + 1)]
                o_ref[:, LANES * t:LANES * (t + 1)] = chunk if t % 3 == 2 else _rope(chunk, c, s1, s2)

        @pl.when(j >= n_rope)
        def _():
            o_ref[...] = y

    tab = pl.BlockSpec((tm, LANES), lambda j, i: (i, 0))
    return _hosted_call(
        body, comm, (h, w_lay, *tabs), name="in_proj_fwd", grid=(LAY_N // tn, S // tm),
        in_specs=[pl.BlockSpec((tm, D), lambda j, i: (i, 0)), pl.BlockSpec((D, tn), lambda j, i: (0, j)), tab, tab, tab],
        out_specs=pl.BlockSpec((tm, tn), lambda j, i: (i, j)),
        out_shape=_sds((S, LAY_N), F32), vmem_mb=VMEM_MB)


def _log1p_small(t):
    return jnp.where(t < 1e-2, t * (1.0 - t * (0.5 - t * (1.0 / 3.0))), jnp.log(1.0 + t))


def fgate_fwd(p, b_pad):
    def body(fa_ref, b_ref, frow_ref, fraw_ref, fcol_ref):
        f = fa_ref[...] + b_ref[...]
        fr = f.T[0:8, :]
        ls = jnp.minimum(fr, 0.0) - _log1p_small(jnp.exp(-jnp.abs(fr)))
        lane = lax.broadcasted_iota(jnp.int32, (8, S), 1)
        acc, sh = ls, 1
        while sh < S:
            acc = acc + jnp.where(lane >= sh, pltpu.roll(acc, sh, 1), 0.0)
            sh *= 2
        frow_ref[...] = acc
        fraw_ref[...] = fr
        for hh in range(8):
            fcol_ref[hh] = jnp.broadcast_to(acc[hh:hh + 1, :], (LANES, S)).T

    return pl.pallas_call(
        body, name="fgate_fwd", grid=(1,),
        in_specs=[pl.BlockSpec((S, LANES), lambda i: (0, LAY_F // LANES)), pl.BlockSpec((1, LANES), lambda i: (0, 0))],
        out_specs=[pl.BlockSpec((8, S), lambda i: (0, 0)), pl.BlockSpec((8, S), lambda i: (0, 0)),
                   pl.BlockSpec((8, S, LANES), lambda i: (0, 0, 0))],
        out_shape=[_sds((8, S), F32), _sds((8, S), F32), _sds((8, S, LANES), F32)],
        compiler_params=_params(VMEM_MB),
    )(p, b_pad)


def _head_masks(rows):
    lane = lax.broadcasted_iota(jnp.int32, (rows, LANES), 1)
    return lane < HD, lane >= HD


FT = 256


def _split3(f):
    hi = f.astype(BF16).astype(F32)
    r = f - hi
    mid = r.astype(BF16).astype(F32)
    return hi, mid, r - mid


def _fox_operands(qkv_ref, tcol_ref, scol_ref, qa_s, ka_s):
    rows = 256
    lane = lax.broadcasted_iota(jnp.int32, (rows, LANES), 1)

    def chunk(i, _):
        r = pl.ds(pl.multiple_of(i * rows, rows), rows)
        q, k = qkv_ref[r, 0:LANES], qkv_ref[r, LANES:2 * LANES]
        s0, s1 = _split3(scol_ref[0, r, :]), _split3(scol_ref[1, r, :])
        ka = jnp.where(lane == 0, -s0[0], jnp.where(lane == 1, -s0[1], jnp.where(lane == 2, -s0[2], jnp.where(
            lane == 3, -s1[0], jnp.where(lane == 4, -s1[1], jnp.where(lane == 5, -s1[2], jnp.where(lane < 9, 1.0, 0.0)))))))
        ka_s[r, 0:LANES] = k.astype(BF16)
        ka_s[r, LANES:2 * LANES] = ka.astype(BF16)
        for hh in range(2):
            own = (lane < HD) if hh == 0 else (lane >= HD)
            t3 = _split3(tcol_ref[hh, r, :])
            ones = (lane >= 3 * hh) & (lane < 3 * hh + 3)
            qa = jnp.where(ones, 1.0, jnp.where(lane == 6, t3[0], jnp.where(lane == 7, t3[1], jnp.where(lane == 8, t3[2], 0.0))))
            qa_s[hh, r, 0:LANES] = jnp.where(own, q * SCALE, 0.0).astype(BF16)
            qa_s[hh, r, LANES:2 * LANES] = qa.astype(BF16)
        return 0

    lax.fori_loop(0, S // rows, chunk, 0)


def fox_fwd(p, fcol, comm=None):
    nt = (((1,), (1,)), ((), ()))

    def body(qkv_ref, fc_ref, o_ref, g_ref, qa_s, ka_s):
        _fox_operands(qkv_ref, fc_ref, fc_ref, qa_s, ka_s)
        masks = _head_masks(FT)
        causal = lax.broadcasted_iota(jnp.int32, (FT, FT), 1) <= lax.broadcasted_iota(jnp.int32, (FT, FT), 0)
        causal2 = jnp.concatenate([causal, causal], axis=0)

        def qloop(qi, _):
            q0 = pl.multiple_of(qi * FT, FT)
            qa = jnp.concatenate([qa_s[0, pl.ds(q0, FT), :], qa_s[1, pl.ds(q0, FT), :]], axis=0)

            def step(kb, carry, diagonal):
                m, l, acc = carry
                k0 = pl.multiple_of(kb * FT, FT)
                v = qkv_ref[pl.ds(k0, FT), 2 * LANES:3 * LANES].astype(BF16)
                s = lax.dot_general(qa, ka_s[pl.ds(k0, FT), :], nt, preferred_element_type=F32)
                if diagonal:
                    s = jnp.where(causal2, s, NEG)
                m_new = jnp.maximum(m, jnp.max(s, axis=1, keepdims=True))
                pr = jnp.exp(s - m_new)
                alpha = jnp.exp(m - m_new)
                return (m_new, l * alpha + jnp.sum(pr, axis=1, keepdims=True),
                        acc * alpha + jnp.dot(pr.astype(BF16), v, preferred_element_type=F32))

            init = (jnp.full((2 * FT, 1), NEG, F32), jnp.zeros((2 * FT, 1), F32), jnp.zeros((2 * FT, LANES), F32))
            carry = lax.fori_loop(0, qi, lambda kb, cr: step(kb, cr, False), init)
            m, l, acc = step(qi, carry, True)
            out = acc / l
            lse = m + jnp.log(l)
            o_ref[pl.ds(q0, FT), :] = jnp.where(masks[0], out[:FT], out[FT:]).astype(BF16)
            g_ref[0, pl.ds(q0, FT), :] = fc_ref[0, pl.ds(q0, FT), :] - lse[:FT]
            g_ref[1, pl.ds(q0, FT), :] = fc_ref[1, pl.ds(q0, FT), :] - lse[FT:]
            return 0

        lax.fori_loop(0, S // FT, qloop, 0)

    a_blk = LAY_A // 384
    return _hosted_call(
        body, comm, (p, fcol), name="fox_fwd", grid=(N_FOX_PAIRS,),
        in_specs=[pl.BlockSpec((S, 384), lambda p_: (0, a_blk + p_)), pl.BlockSpec((2, S, LANES), lambda p_: (p_, 0, 0))],
        out_specs=[pl.BlockSpec((S, LANES), lambda p_: (0, p_)), pl.BlockSpec((2, S, LANES), lambda p_: (p_, 0, 0))],
        out_shape=[_sds((S, 4 * LANES), BF16), _sds((8, S, LANES), F32)],
        scratch_shapes=[pltpu.VMEM((2, S, 2 * LANES), BF16), pltpu.VMEM((S, 2 * LANES), BF16)],
        vmem_mb=VMEM_MB)


def _dil_rows(ref, start, d):
    return ref[pl.ds(start, SPAN), :] if d == 1 else ref[pl.ds(start, SPAN, stride=d), :]


def _dil_store(ref, start, d, val):
    if d == 1:
        ref[pl.ds(start, SPAN), :] = val
    else:
        ref[pl.ds(start, SPAN, stride=d), :] = val


def _band_mask(has_prev):
    qi = lax.broadcasted_iota(jnp.int32, (SPAN, 2 * SPAN), 0) + SPAN
    kj = lax.broadcasted_iota(jnp.int32, (SPAN, 2 * SPAN), 1)
    dist = qi - kj
    return (dist >= 0) & (dist <= SPAN) & (has_prev | (kj >= SPAN))


def _dil_block(n, d, nb):
    r, j = n // nb, n % nb
    start = r + d * SPAN * j
    prev = jnp.maximum(start - d * SPAN, r)
    return start, prev, j > 0


def dil_fwd(p, comm=None):
    def body(*refs):
        qkv = [refs[3 * g:3 * g + 3] for g in range(3)]
        y_ref, lse_ref = refs[9], refs[10]
        acc_s, m_s, l_s = refs[11], refs[12], refs[13]
        masks = _head_masks(SPAN)
        for g, (d, nb) in enumerate(DIL_GROUPS):
            q_ref, k_ref, v_ref = qkv[g]

            def blk(n, _):
                start, prev, has_prev = _dil_block(n, d, nb)
                q = _dil_rows(q_ref, start, d)
                kc = jnp.concatenate([_dil_rows(k_ref, prev, d), _dil_rows(k_ref, start, d)], axis=0).astype(BF16)
                vc = jnp.concatenate([_dil_rows(v_ref, prev, d), _dil_rows(v_ref, start, d)], axis=0).astype(BF16)
                valid = _band_mask(has_prev)
                valid2 = jnp.concatenate([valid, valid], axis=0)
                q2 = (jnp.concatenate([jnp.where(masks[0], q, 0.0), jnp.where(masks[1], q, 0.0)], axis=0) * SCALE).astype(BF16)
                s = jnp.where(valid2, lax.dot_general(q2, kc, (((1,), (1,)), ((), ())), preferred_element_type=F32), NEG)
                m = jnp.max(s, axis=1, keepdims=True)
                pr = jnp.exp(s - m)
                l = jnp.sum(pr, axis=1, keepdims=True)
                acc = jnp.dot(pr.astype(BF16), vc, preferred_element_type=F32)
                _dil_store(acc_s.at[g], start, d, jnp.where(masks[0], acc[:SPAN], acc[SPAN:]))
                _dil_store(m_s.at[g], start, d, jnp.where(masks[0], m[:SPAN], m[SPAN:]))
                _dil_store(l_s.at[g], start, d, jnp.where(masks[0], l[:SPAN], l[SPAN:]))
                return 0

            lax.fori_loop(0, 16, blk, 0)

        def merge(i, _):
            rows = pl.ds(pl.multiple_of(i * 256, 256), 256)
            m = [m_s[g, rows, :] for g in range(3)]
            mx = jnp.maximum(jnp.maximum(m[0], m[1]), m[2])
            w = [jnp.exp(m[g] - mx) for g in range(3)]
            l = sum(l_s[g, rows, :] * w[g] for g in range(3))
            y_ref[rows, :] = sum(acc_s[g, rows, :] * w[g] for g in range(3)) / l
            lse_ref[rows, :] = mx + jnp.log(l)
            return 0

        lax.fori_loop(0, S // 256, merge, 0)

    def spec(g, t):
        return pl.BlockSpec((S, LANES), lambda p_: (0, (p_ * 3 + g) * 3 + t))

    return _hosted_call(
        body, comm, [p] * 9, name="dil_fwd", grid=(N_DIL_PAIRS,),
        in_specs=[spec(g, t) for g in range(3) for t in range(3)],
        out_specs=[pl.BlockSpec((S, LANES), lambda p_: (0, p_)), pl.BlockSpec((S, LANES), lambda p_: (0, p_))],
        out_shape=[_sds((S, 2 * LANES), F32), _sds((S, 2 * LANES), F32)],
        scratch_shapes=[pltpu.VMEM((3, S, LANES), F32)] * 3,
        vmem_mb=VMEM_MB)


def merge_fwd(ya_att, yb, p, w_bra, w_brb, comm=None):
    tm = 256
    gblk = LAY_G // D

    def body(a_ref, b_ref, ga_ref, gb_ref, wa_ref, wb_ref, mg_ref, ya_ref, yb_ref):
        ya = jnp.dot(a_ref[...], wa_ref[...], preferred_element_type=F32)
        ybp = jnp.dot(b_ref[...].astype(BF16), wb_ref[...], preferred_element_type=F32)
        mg_ref[...] = (_sigmoid(ga_ref[...]) * ya + _sigmoid(gb_ref[...]) * ybp).astype(BF16)
        ya_ref[...] = ya
        yb_ref[...] = ybp

    row = lambda w: pl.BlockSpec((tm, w), lambda i: (i, 0))
    return _hosted_call(
        body, comm, (ya_att, yb, p, p, w_bra, w_brb), name="merge_fwd", grid=(S // tm,),
        in_specs=[row(512), row(256), pl.BlockSpec((tm, D), lambda i: (i, gblk)), pl.BlockSpec((tm, D), lambda i: (i, gblk + 1)),
                  pl.BlockSpec((512, D), lambda i: (0, 0)), pl.BlockSpec((256, D), lambda i: (0, 0))],
        out_specs=[row(D), row(D), row(D)],
        out_shape=[_sds((S, D), BF16), _sds((S, D), F32), _sds((S, D), F32)])


def out_proj_fwd(merged, w_out, x, mod, g_ffn):
    tm = 256

    def body(a_ref, w_ref, x_ref, mod_ref, g_ref, mix_ref, x1_ref, h2_ref):
        mix = jnp.dot(a_ref[...], w_ref[...], preferred_element_type=F32)
        x1 = x_ref[...] + mod_ref[2:3, :] * mix
        r = lax.rsqrt(jnp.mean(x1 * x1, axis=1, keepdims=True) + EPS)
        h2 = (x1 * r * g_ref[...]) * (1.0 + mod_ref[4:5, :]) + mod_ref[3:4, :]
        mix_ref[...] = mix
        x1_ref[...] = x1
        h2_ref[...] = h2.astype(BF16)

    row = pl.BlockSpec((tm, D), lambda i: (i, 0))
    return pl.pallas_call(
        body, name="out_proj_fwd", grid=(S // tm,),
        in_specs=[row, pl.BlockSpec((D, D), lambda i: (0, 0)), row, pl.BlockSpec((8, D), lambda i: (0, 0)),
                  pl.BlockSpec((1, D), lambda i: (0, 0))],
        out_specs=[row, row, row],
        out_shape=[_sds((S, D), F32), _sds((S, D), F32), _sds((S, D), BF16)],
    )(merged, w_out, x, mod, g_ffn)


def ffn_up_fwd(h2, w_gate, w_up):
    tm = 1024

    def body(h_ref, wg_ref, wu_ref, a_ref, u_ref, z_ref):
        h = h_ref[...]
        a = jnp.dot(h, wg_ref[...], preferred_element_type=F32)
        u = jnp.dot(h, wu_ref[...], preferred_element_type=F32)
        a_ref[...] = a
        u_ref[...] = u
        z_ref[...] = (a * _sigmoid(a) * u).astype(BF16)

    out = pl.BlockSpec((tm, FF_PAD), lambda k, i: (i, k))
    return pl.pallas_call(
        body, name="ffn_up_fwd", grid=(4, S // tm),
        in_specs=[pl.BlockSpec((tm, D), lambda k, i: (i, 0)), pl.BlockSpec((None, D, FF_PAD), lambda k, i: (k, 0, 0)),
                  pl.BlockSpec((None, D, FF_PAD), lambda k, i: (k, 0, 0))],
        out_specs=[out, out, out],
        out_shape=[_sds((S, FFP), F32), _sds((S, FFP), F32), _sds((S, FFP), BF16)], compiler_params=_params(VMEM_MB),
    )(h2, w_gate, w_up)


def ffn_down_loss(z, w_down, x1, mod, g_final, tgt):
    tm = 256

    def body(z_ref, w_ref, x1_ref, mod_ref, g_ref, t_ref, dx2_ref, dffn_ref, dg_ref, dga_ref, loss_ref, s_dg, s_dga, s_loss):
        i = pl.program_id(0)

        @pl.when(i == 0)
        def _():
            s_dg[...] = jnp.zeros_like(s_dg)
            s_dga[...] = jnp.zeros_like(s_dga)
            s_loss[...] = jnp.zeros_like(s_loss)

        ffn = jnp.dot(z_ref[...], w_ref[...], preferred_element_type=F32)
        gaf = mod_ref[5:6, :]
        x2 = x1_ref[...] + gaf * ffn
        r = lax.rsqrt(jnp.mean(x2 * x2, axis=1, keepdims=True) + EPS)
        xh = x2 * r
        g = g_ref[...]
        e = xh * g - t_ref[...]
        s_loss[...] += 0.5 * jnp.sum(jnp.mean(e * e, axis=1, keepdims=True), axis=0, keepdims=True)
        dy = e * (1.0 / D)
        gdy = dy * g
        dx2 = r * (gdy - xh * jnp.mean(gdy * xh, axis=1, keepdims=True))
        s_dg[...] += _colsum8(dy * xh)
        s_dga[...] += _colsum8(dx2 * ffn)
        dx2_ref[...] = dx2
        dffn_ref[...] = (dx2 * gaf).astype(BF16)

        @pl.when(i == pl.num_programs(0) - 1)
        def _():
            dg_ref[...] = jnp.sum(s_dg[...], axis=0, keepdims=True)
            dga_ref[...] = jnp.sum(s_dga[...], axis=0, keepdims=True)
            loss_ref[...] = jnp.broadcast_to(s_loss[...], (1, LANES))

    row = pl.BlockSpec((tm, D), lambda i: (i, 0))
    vec = pl.BlockSpec((1, D), lambda i: (0, 0))
    return pl.pallas_call(
        body, name="ffn_down_loss", grid=(S // tm,),
        in_specs=[pl.BlockSpec((tm, FFP), lambda i: (i, 0)), pl.BlockSpec((FFP, D), lambda i: (0, 0)), row,
                  pl.BlockSpec((8, D), lambda i: (0, 0)), vec, row],
        out_specs=[row, row, vec, vec, pl.BlockSpec((1, LANES), lambda i: (0, 0))],
        out_shape=[_sds((S, D), F32), _sds((S, D), BF16), _sds((1, D), F32), _sds((1, D), F32), _sds((1, LANES), F32)],
        scratch_shapes=[pltpu.VMEM((8, D), F32), pltpu.VMEM((8, D), F32), pltpu.VMEM((1, 1), F32)],
        compiler_params=_params(VMEM_MB),
    )(z, w_down, x1, mod, g_final, tgt)


def ffn_down_bwd(dffn, w_down, a, u, z):
    tm, tn = 1024, 768

    def body(d_ref, w_ref, a_ref, u_ref, z_ref, da_ref, du_ref, dw_ref):
        i = pl.program_id(1)
        dff = d_ref[...]
        dz = lax.dot_general(dff, w_ref[...], (((1,), (1,)), ((), ())), preferred_element_type=F32)
        av, uv = a_ref[...], u_ref[...]
        sg = _sigmoid(av)
        du_ref[...] = (dz * (av * sg)).astype(BF16)
        da_ref[...] = (dz * uv * (sg * (1.0 + av * (1.0 - sg)))).astype(BF16)
        dw = lax.dot_general(z_ref[...], dff, (((0,), (0,)), ((), ())), preferred_element_type=F32)

        @pl.when(i == 0)
        def _():
            dw_ref[...] = dw

        @pl.when(i > 0)
        def _():
            dw_ref[...] += dw

    tile = pl.BlockSpec((tm, tn), lambda j, i: (i, j))
    return pl.pallas_call(
        body, name="ffn_down_bwd", grid=(FFP // tn, S // tm),
        in_specs=[pl.BlockSpec((tm, D), lambda j, i: (i, 0)), pl.BlockSpec((tn, D), lambda j, i: (j, 0)), tile, tile, tile],
        out_specs=[tile, tile, pl.BlockSpec((tn, D), lambda j, i: (j, 0))],
        out_shape=[_sds((S, FFP), BF16), _sds((S, FFP), BF16), _sds((FFP, D), F32)], compiler_params=_params(VMEM_MB),
    )(dffn, w_down, a, u, z)


def mm_nt(dy, w, name, comm=None):
    tm = 1024
    n = dy.shape[1]
    if w.ndim == 2:
        k_in, tk = w.shape[0], 768
        w_spec = pl.BlockSpec((k_in, tk), lambda i, k: (0, k))
    else:
        k_in, tk = w.shape[1], FF_PAD
        w_spec = pl.BlockSpec((None, k_in, tk), lambda i, k: (k, 0, 0))
    nk = n // tk

    def body(d_ref, w_ref, o_ref, acc):
        k = pl.program_id(1)
        part = lax.dot_general(d_ref[...], w_ref[...], (((1,), (1,)), ((), ())), preferred_element_type=F32)

        @pl.when(k == 0)
        def _():
            acc[...] = part

        @pl.when(k > 0)
        def _():
            acc[...] += part

        @pl.when(k == nk - 1)
        def _():
            o_ref[...] = acc[...]

    return _hosted_call(
        body, comm, (dy, w), name=name, grid=(S // tm, nk),
        in_specs=[pl.BlockSpec((tm, tk), lambda i, k: (i, k)), w_spec],
        out_specs=pl.BlockSpec((tm, k_in), lambda i, k: (i, 0)),
        out_shape=_sds((S, k_in), F32),
        scratch_shapes=[pltpu.VMEM((tm, k_in), F32)], vmem_mb=VMEM_MB)


def mm_tn(h, dy, name, shard_major=False, comm=None):
    tm, tn = 2048, 768
    k_in, n = h.shape[1], dy.shape[1]

    def body(h_ref, d_ref, o_ref):
        i = pl.program_id(1)
        dw = lax.dot_general(h_ref[...], d_ref[...], (((0,), (0,)), ((), ())), preferred_element_type=F32)

        @pl.when(i == 0)
        def _():
            o_ref[...] = dw

        @pl.when(i > 0)
        def _():
            o_ref[...] += dw

    if shard_major:
        out_spec, out_shape = pl.BlockSpec((None, k_in, tn), lambda j, i: (j, 0, 0)), _sds((n // tn, k_in, tn), F32)
    else:
        out_spec, out_shape = pl.BlockSpec((k_in, tn), lambda j, i: (0, j)), _sds((k_in, n), F32)
    return _hosted_call(
        body, comm, (h, dy), name=name, grid=(n // tn, S // tm),
        in_specs=[pl.BlockSpec((tm, k_in), lambda j, i: (i, 0)), pl.BlockSpec((tm, tn), lambda j, i: (i, j))],
        out_specs=out_spec, out_shape=out_shape, vmem_mb=VMEM_MB)


def mid_bwd(dh2a, dh2b, x1, dx2, mix, mod, g_ffn, p, ya, ybp, merged, ya_att, yb, w_out, w_bra, w_brb, comm=None):
    tm = 256
    gblk = LAY_G // D
    nsteps = S // tm

    def body(dha_ref, dhb_ref, x1_ref, dx2_ref, mix_ref, mod_ref, g_ref, ga_ref, gb_ref, ya_ref, yb_ref, mg_ref,
             att_ref, ybb_ref, wo_ref, wa_ref, wb_ref,
             dx1_ref, dpg_ref, datt_ref, dyb_ref, cs_ref, dwo_ref, dwa_ref, dwb_ref, s_cs):
        i = pl.program_id(0)

        @pl.when(i == 0)
        def _():
            s_cs[...] = jnp.zeros_like(s_cs)
            dwo_ref[...] = jnp.zeros_like(dwo_ref)
            dwa_ref[...] = jnp.zeros_like(dwa_ref)
            dwb_ref[...] = jnp.zeros_like(dwb_ref)

        x1 = x1_ref[...]
        g = g_ref[...]
        r = lax.rsqrt(jnp.mean(x1 * x1, axis=1, keepdims=True) + EPS)
        xh = x1 * r
        dh2 = dha_ref[...] + dhb_ref[...]
        s_cs[0] += _colsum8(dh2)
        s_cs[1] += _colsum8(dh2 * (xh * g))
        dn2 = dh2 * (1.0 + mod_ref[4:5, :])
        s_cs[2] += _colsum8(dn2 * xh)
        gd = dn2 * g
        dx1 = dx2_ref[...] + r * (gd - xh * jnp.mean(gd * xh, axis=1, keepdims=True))
        s_cs[3] += _colsum8(dx1 * mix_ref[...])
        dx1_ref[...] = dx1
        dmix = (dx1 * mod_ref[2:3, :]).astype(BF16)
        dmg = lax.dot_general(dmix, wo_ref[...], (((1,), (1,)), ((), ())), preferred_element_type=F32)
        sga, sgb = _sigmoid(ga_ref[...]), _sigmoid(gb_ref[...])
        dya = (dmg * sga).astype(BF16)
        dybp = (dmg * sgb).astype(BF16)
        dpg_ref[:, 0:D] = (dmg * ya_ref[...] * (sga * (1.0 - sga))).astype(BF16)
        dpg_ref[:, D:2 * D] = (dmg * yb_ref[...] * (sgb * (1.0 - sgb))).astype(BF16)
        datt_ref[...] = lax.dot_general(dya, wa_ref[...], (((1,), (1,)), ((), ())), preferred_element_type=F32).astype(BF16)
        dyb_ref[...] = lax.dot_general(dybp, wb_ref[...], (((1,), (1,)), ((), ())), preferred_element_type=F32)
        tn_dims = (((0,), (0,)), ((), ()))
        dwo_ref[...] += lax.dot_general(mg_ref[...], dmix, tn_dims, preferred_element_type=F32)
        dwa_ref[...] += lax.dot_general(att_ref[...], dya, tn_dims, preferred_element_type=F32)
        dwb_ref[...] += lax.dot_general(ybb_ref[...].astype(BF16), dybp, tn_dims, preferred_element_type=F32)

        @pl.when(i == nsteps - 1)
        def _():
            for t in range(4):
                cs_ref[t:t + 1, :] = jnp.sum(s_cs[t], axis=0, keepdims=True)
            cs_ref[4:8, :] = jnp.zeros((4, D), F32)

    row = lambda w: pl.BlockSpec((tm, w), lambda i: (i, 0))
    full = lambda a, b: pl.BlockSpec((a, b), lambda i: (0, 0))
    return _hosted_call(
        body, comm, (dh2a, dh2b, x1, dx2, mix, mod, g_ffn, p, p, ya, ybp, merged, ya_att, yb, w_out, w_bra, w_brb),
        name="mid_bwd", grid=(nsteps,),
        in_specs=[row(D), row(D), row(D), row(D), row(D), full(8, D), full(1, D),
                  pl.BlockSpec((tm, D), lambda i: (i, gblk)), pl.BlockSpec((tm, D), lambda i: (i, gblk + 1)),
                  row(D), row(D), row(D), row(512), row(256), full(D, D), full(512, D), full(256, D)],
        out_specs=[row(D), pl.BlockSpec((tm, 2 * D), lambda i: (i, LAY_G // (2 * D))), row(512), row(256), full(8, D),
                   full(D, D), full(512, D), full(256, D)],
        out_shape=[_sds((S, D), F32), _sds((S, LAY_N), BF16), _sds((S, 512), BF16), _sds((S, 256), F32), _sds((8, D), F32),
                   _sds((D, D), F32), _sds((512, D), F32), _sds((256, D), F32)],
        scratch_shapes=[pltpu.VMEM((4, 8, D), F32)],
        vmem_mb=VMEM_MB)


def fox_bwd(p, do, o, gcol, fcol, dp, comm=None):
    nq = S // FT
    nt = (((1,), (1,)), ((), ()))
    tn = (((0,), (0,)), ((), ()))

    def body(qkv_ref, do_ref, o_ref, g_ref, fc_ref, dp_in, dp_ref, df_ref, rs_ref, dq_s, qa_s, ka_s, dob_s, dl_s):
        del dp_in
        _fox_operands(qkv_ref, g_ref, fc_ref, qa_s, ka_s)
        masks = _head_masks(FT)
        lane = lax.broadcasted_iota(jnp.int32, (FT, LANES), 1)
        head0 = 2 * pl.program_id(0)
        causal = lax.broadcasted_iota(jnp.int32, (FT, FT), 1) <= lax.broadcasted_iota(jnp.int32, (FT, FT), 0)
        dq_s[...] = jnp.zeros_like(dq_s)
        rs_ref[...] = jnp.zeros_like(rs_ref)

        causal2 = jnp.concatenate([causal, causal], axis=0)

        def prep(i, _):
            r = pl.ds(pl.multiple_of(i * 256, 256), 256)
            m256 = _head_masks(256)
            dov, ov = do_ref[r, :].astype(F32), o_ref[r, :].astype(F32)
            for hh in range(2):
                dom = jnp.where(m256[hh], dov, 0.0)
                dob_s[hh, r, :] = dom.astype(BF16)
                dl_s[hh, r, :] = jnp.broadcast_to(jnp.sum(dom * ov, axis=1, keepdims=True), (256, LANES))
            return 0

        lax.fori_loop(0, S // 256, prep, 0)

        def stack(ref, q0, cols=slice(None)):
            return jnp.concatenate([ref[0, pl.ds(q0, FT), cols], ref[1, pl.ds(q0, FT), cols]], axis=0)

        def kloop(kb, _):
            k0 = pl.multiple_of(kb * FT, FT)
            k = qkv_ref[pl.ds(k0, FT), LANES:2 * LANES].astype(BF16)
            v = qkv_ref[pl.ds(k0, FT), 2 * LANES:3 * LANES].astype(BF16)
            ka = ka_s[pl.ds(k0, FT), :]

            def step(qi, carry, diagonal):
                dk, dv, df0, df1 = carry
                q0 = pl.multiple_of(qi * FT, FT)
                qa, dob = stack(qa_s, q0), stack(dob_s, q0)
                s = lax.dot_general(qa, ka, nt, preferred_element_type=F32)
                pr = jnp.exp(jnp.where(causal2, s, NEG)) if diagonal else jnp.exp(s)
                dpr = lax.dot_general(dob, v, nt, preferred_element_type=F32)
                ds = pr * (dpr - jnp.tile(stack(dl_s, q0), (1, FT // LANES)))
                dsb = ds.astype(BF16)
                dq = jnp.dot(dsb, k, preferred_element_type=F32) * SCALE
                dk = dk + lax.dot_general(dsb, qa[:, 0:LANES], tn, preferred_element_type=F32)
                dv = dv + lax.dot_general(pr.astype(BF16), dob, tn, preferred_element_type=F32)
                rsum = jnp.sum(ds, axis=1, keepdims=True)
                dq_s[pl.ds(q0, FT), :] += jnp.where(masks[0], dq[:FT], dq[FT:])
                rs_ref[pl.ds(q0, FT), :] += jnp.where(lane == head0, rsum[:FT], 0.0) + jnp.where(lane == head0 + 1, rsum[FT:], 0.0)
                return (dk, dv, df0 - jnp.sum(ds[:FT], axis=0, keepdims=True), df1 - jnp.sum(ds[FT:], axis=0, keepdims=True))

            z = jnp.zeros((FT, LANES), F32)
            z1 = jnp.zeros((1, FT), F32)
            carry = step(kb, (z, z, z1, z1), True)
            dk, dv, df0, df1 = lax.fori_loop(kb + 1, nq, lambda qi, cr: step(qi, cr, False), carry)
            dp_ref[pl.ds(k0, FT), LANES:2 * LANES] = dk.astype(BF16)
            dp_ref[pl.ds(k0, FT), 2 * LANES:3 * LANES] = dv.astype(BF16)
            df_ref[0:1, pl.ds(k0, FT)] = df0
            df_ref[1:2, pl.ds(k0, FT)] = df1
            return 0

        lax.fori_loop(0, S // FT, kloop, 0)
        dp_ref[:, 0:LANES] = dq_s[...].astype(BF16)

    a_blk = LAY_A // 384
    pair = pl.BlockSpec((S, LANES), lambda p_: (0, p_))
    heads = pl.BlockSpec((2, S, LANES), lambda p_: (p_, 0, 0))
    return _hosted_call(
        body, comm, (p, do, o, gcol, fcol, dp), name="fox_bwd", grid=(N_FOX_PAIRS,),
        in_specs=[pl.BlockSpec((S, 384), lambda p_: (0, a_blk + p_)), pair, pair, heads, heads, pl.BlockSpec(memory_space=pl.ANY)],
        out_specs=[pl.BlockSpec((S, 384), lambda p_: (0, a_blk + p_)), pl.BlockSpec((None, 2, S), lambda p_: (p_, 0, 0)),
                   pl.BlockSpec((None, S, LANES), lambda p_: (p_, 0, 0))],
        out_shape=[_sds((S, LAY_N), BF16), _sds((4, 2, S), F32), _sds((4, S, LANES), F32)],
        scratch_shapes=[pltpu.VMEM((S, LANES), F32), pltpu.VMEM((2, S, 2 * LANES), BF16), pltpu.VMEM((S, 2 * LANES), BF16),
                        pltpu.VMEM((2, S, LANES), BF16), pltpu.VMEM((2, S, LANES), F32)],
        aliases={5: 0}, vmem_mb=VMEM_MB)


def fgate_bwd(dfrow, dfcol, fraw, dp):
    def body(df_ref, dc_ref, f_ref, dp_in, dpf_ref, db_ref):
        del dp_in
        lane = lax.broadcasted_iota(jnp.int32, (8, S), 1)
        rsum = (dc_ref[0] + dc_ref[1]) + (dc_ref[2] + dc_ref[3])
        acc, sh = df_ref[...] + rsum.T[0:8, :], 1
        while sh < S:
            acc = acc + jnp.where(lane < S - sh, pltpu.roll(acc, S - sh, 1), 0.0)
            sh *= 2
        df = acc * _sigmoid(-f_ref[...])
        db_ref[...] = jnp.broadcast_to(jnp.sum(df, axis=1, keepdims=True), (8, LANES))
        dfc = jnp.concatenate([df, jnp.zeros((LANES - 8, S), F32)], axis=0).T
        dpf_ref[:, 0:LANES] = dfc.astype(BF16)
        dpf_ref[:, LANES:2 * LANES] = jnp.zeros((S, LANES), BF16)

    return pl.pallas_call(
        body, name="fgate_bwd", grid=(1,),
        in_specs=[pl.BlockSpec((8, S), lambda i: (0, 0)), pl.BlockSpec((4, S, LANES), lambda i: (0, 0, 0)),
                  pl.BlockSpec((8, S), lambda i: (0, 0)), pl.BlockSpec(memory_space=pl.ANY)],
        out_specs=[pl.BlockSpec((S, 2 * LANES), lambda i: (0, LAY_F // (2 * LANES))), pl.BlockSpec((8, LANES), lambda i: (0, 0))],
        out_shape=[_sds((S, LAY_N), BF16), _sds((8, LANES), F32)],
        input_output_aliases={3: 0},
        compiler_params=_params(VMEM_MB),
    )(dfrow, dfcol, fraw, dp)


def dil_bwd(p, dyb, yb, lse, tabs, dp, comm=None):
    def body(*refs):
        qkv = [refs[3 * g:3 * g + 3] for g in range(3)]
        dy_ref, y_ref, lse_ref, c_ref, s1_ref, s2_ref = refs[9:15]
        dp_ref = refs[16]
        dq_s, dk_s, dv_s, dl_s = refs[17:21]
        masks = _head_masks(SPAN)
        m256 = _head_masks(256)
        nt = (((1,), (1,)), ((), ()))
        tn = (((0,), (0,)), ((), ()))
        dk_s[...] = jnp.zeros_like(dk_s)
        dv_s[...] = jnp.zeros_like(dv_s)

        def prep(i, _):
            rows = pl.ds(pl.multiple_of(i * 256, 256), 256)
            pr = dy_ref[rows, :] * y_ref[rows, :]
            d0 = jnp.sum(jnp.where(m256[0], pr, 0.0), axis=1, keepdims=True)
            d1 = jnp.sum(jnp.where(m256[1], pr, 0.0), axis=1, keepdims=True)
            dl_s[rows, :] = jnp.where(m256[0], d0, d1)
            return 0

        lax.fori_loop(0, S // 256, prep, 0)

        for g, (d, nb) in enumerate(DIL_GROUPS):
            q_ref, k_ref, v_ref = qkv[g]

            def blk(n, _):
                start, prev, has_prev = _dil_block(n, d, nb)
                q = _dil_rows(q_ref, start, d)
                kc = jnp.concatenate([_dil_rows(k_ref, prev, d), _dil_rows(k_ref, start, d)], axis=0).astype(BF16)
                vc = jnp.concatenate([_dil_rows(v_ref, prev, d), _dil_rows(v_ref, start, d)], axis=0).astype(BF16)
                dov = _dil_rows(dy_ref, start, d)
                lsev = _dil_rows(lse_ref, start, d)
                dlv = _dil_rows(dl_s, start, d)
                valid = _band_mask(has_prev)
                valid2 = jnp.concatenate([valid, valid], axis=0)

                def stack(t):
                    return jnp.concatenate([jnp.where(masks[0], t, 0.0), jnp.where(masks[1], t, 0.0)], axis=0)

                def column(t):
                    return jnp.concatenate([jnp.max(jnp.where(masks[hh], t, NEG), axis=1, keepdims=True) for hh in range(2)], axis=0)

                q2 = (stack(q) * SCALE).astype(BF16)
                dob = stack(dov).astype(BF16)
                s = jnp.where(valid2, lax.dot_general(q2, kc, nt, preferred_element_type=F32), NEG)
                pr = jnp.exp(s - column(lsev))
                dpr = lax.dot_general(dob, vc, nt, preferred_element_type=F32)
                dsb = (pr * (dpr - column(dlv))).astype(BF16)
                dq = jnp.dot(dsb, kc, preferred_element_type=F32) * SCALE
                dkc = lax.dot_general(dsb, q2, tn, preferred_element_type=F32)
                dvc = lax.dot_general(pr.astype(BF16), dob, tn, preferred_element_type=F32)
                _dil_store(dq_s.at[g], start, d, jnp.where(masks[0], dq[:SPAN], dq[SPAN:]))
                for ref, val in ((dk_s.at[g], dkc), (dv_s.at[g], dvc)):
                    _dil_store(ref, prev, d, _dil_rows(ref, prev, d) + jnp.where(has_prev, val[0:SPAN], 0.0))
                    _dil_store(ref, start, d, _dil_rows(ref, start, d) + val[SPAN:])
                return 0

            lax.fori_loop(0, 16, blk, 0)

        def fin(i, _):
            rows = pl.ds(pl.multiple_of(i * 256, 256), 256)
            c, s1, s2 = c_ref[rows, :], s1_ref[rows, :], s2_ref[rows, :]
            for g in range(3):
                base = g * 384
                dp_ref[rows, base:base + LANES] = _rope_bwd(dq_s[g, rows, :], c, s1, s2).astype(BF16)
                dp_ref[rows, base + LANES:base + 2 * LANES] = _rope_bwd(dk_s[g, rows, :], c, s1, s2).astype(BF16)
                dp_ref[rows, base + 2 * LANES:base + 3 * LANES] = dv_s[g, rows, :].astype(BF16)
            return 0

        lax.fori_loop(0, S // 256, fin, 0)

    def spec(g, t):
        return pl.BlockSpec((S, LANES), lambda p_: (0, (p_ * 3 + g) * 3 + t))

    pair = pl.BlockSpec((S, LANES), lambda p_: (0, p_))
    tab = pl.BlockSpec((S, LANES), lambda p_: (0, 0))
    return _hosted_call(
        body, comm, [p] * 9 + [dyb, yb, lse, *tabs, dp], name="dil_bwd", grid=(N_DIL_PAIRS,),
        in_specs=[spec(g, t) for g in range(3) for t in range(3)] + [pair, pair, pair, tab, tab, tab, pl.BlockSpec(memory_space=pl.ANY)],
        out_specs=pl.BlockSpec((S, 1152), lambda p_: (0, p_)),
        out_shape=_sds((S, LAY_N), BF16),
        scratch_shapes=[pltpu.VMEM((3, S, LANES), F32)] * 3 + [pltpu.VMEM((S, LANES), F32)],
        aliases={15: 0}, vmem_mb=VMEM_MB)


def in_bwd_tail(dh1, x, dx1, mod, g_mix, comm=None):
    tm = 256
    nsteps = S // tm

    def body(dh_ref, x_ref, dx1_ref, mod_ref, g_ref, dx_ref, cs_ref, s_cs):
        i = pl.program_id(0)

        @pl.when(i == 0)
        def _():
            s_cs[...] = jnp.zeros_like(s_cs)

        xv, g, dh = x_ref[...], g_ref[...], dh_ref[...]
        r = lax.rsqrt(jnp.mean(xv * xv, axis=1, keepdims=True) + EPS)
        xh = xv * r
        s_cs[0] += _colsum8(dh)
        s_cs[1] += _colsum8(dh * (xh * g))
        dn = dh * (1.0 + mod_ref[1:2, :])
        s_cs[2] += _colsum8(dn * xh)
        gd = dn * g
        dx_ref[...] = dx1_ref[...] + r * (gd - xh * jnp.mean(gd * xh, axis=1, keepdims=True))

        @pl.when(i == nsteps - 1)
        def _():
            for t in range(3):
                cs_ref[t:t + 1, :] = jnp.sum(s_cs[t], axis=0, keepdims=True)
            cs_ref[3:8, :] = jnp.zeros((5, D), F32)

    row = pl.BlockSpec((tm, D), lambda i: (i, 0))
    return _hosted_call(
        body, comm, (dh1, x, dx1, mod, g_mix), name="in_bwd_tail", grid=(nsteps,),
        in_specs=[row, row, row, pl.BlockSpec((8, D), lambda i: (0, 0)), pl.BlockSpec((1, D), lambda i: (0, 0))],
        out_specs=[row, pl.BlockSpec((8, D), lambda i: (0, 0))],
        out_shape=[_sds((S, D), F32), _sds((8, D), F32)],
        scratch_shapes=[pltpu.VMEM((3, 8, D), F32)])


def _lay_pieces():
    out = []
    qa, ka, va, fa, qb, kb, vb, ga = 0, 512, 1024, 1536, 1544, 2312, 3080, 3848
    for p in range(N_DIL_PAIRS):
        for g in range(3):
            base = LAY_B + (p * 3 + g) * 384
            hd0 = (4 * g + 2 * p) * HD
            out += [(base, qb + hd0, LANES), (base + LANES, kb + hd0, LANES), (base + 2 * LANES, vb + hd0, LANES)]
    for p in range(N_FOX_PAIRS):
        base = LAY_A + p * 384
        out += [(base, qa + p * LANES, LANES), (base + LANES, ka + p * LANES, LANES), (base + 2 * LANES, va + p * LANES, LANES)]
    out.append((LAY_F, fa, 8))
    out.append((LAY_G, ga, 2 * D))
    return out


def lay_from_nat(w_nat):
    parts, pos = [], 0
    for lay, nat, width in sorted(_lay_pieces()):
        if lay > pos:
            parts.append(jnp.zeros((w_nat.shape[0], lay - pos), w_nat.dtype))
        parts.append(w_nat[:, nat:nat + width])
        pos = lay + width
    if pos < LAY_N:
        parts.append(jnp.zeros((w_nat.shape[0], LAY_N - pos), w_nat.dtype))
    return jnp.concatenate(parts, axis=1)


def nat_from_lay(w_lay):
    parts = [w_lay[:, lay:lay + width] for lay, nat, width in sorted(_lay_pieces(), key=lambda t: t[1])]
    return jnp.concatenate(parts, axis=1)


def _shard_runs():
    runs = []
    for lay, nat, width in _lay_pieces():
        while width:
            k, loc = nat // IN_SHARD, nat % IN_SHARD
            w = min(width, IN_SHARD - loc)
            runs.append((lay, k, loc, w))
            lay, nat, width = lay + w, nat + w, width - w
    return runs


def lay_from_shards(g):
    tm = 256

    def body(g_ref, o_ref):
        o_ref[:, LAY_F:LAY_G] = jnp.zeros((tm, LAY_G - LAY_F), g.dtype)
        for lay, k, loc, w in _shard_runs():
            o_ref[:, lay:lay + w] = g_ref[k, :, loc:loc + w]

    return pl.pallas_call(
        body, name="lay_from_shards", grid=(D // tm,),
        in_specs=[pl.BlockSpec((4, tm, IN_SHARD_PAD), lambda i: (0, i, 0))],
        out_specs=pl.BlockSpec((tm, LAY_N), lambda i: (i, 0)),
        out_shape=_sds((D, LAY_N), g.dtype), compiler_params=_params(VMEM_MB),
    )(g)


def shards_from_lay(dw_lay):
    tm = 256

    def body(x_ref, o_ref):
        o_ref[:, :, IN_SHARD:] = jnp.zeros((4, tm, IN_SHARD_PAD - IN_SHARD), F32)
        for lay, k, loc, w in _shard_runs():
            o_ref[k, :, loc:loc + w] = x_ref[:, lay:lay + w]

    return pl.pallas_call(
        body, name="shards_from_lay", grid=(D // tm,),
        in_specs=[pl.BlockSpec((tm, LAY_N), lambda i: (i, 0))],
        out_specs=pl.BlockSpec((4, tm, IN_SHARD_PAD), lambda i: (0, i, 0)),
        out_shape=_sds((4, D, IN_SHARD_PAD), F32), compiler_params=_params(VMEM_MB),
    )(dw_lay)


def _pos():
    return lax.axis_index("x"), lax.axis_index("y"), lax.axis_index("c")


def _other_chips(x, y):
    return [(1 - x, y), (x, 1 - y), (1 - x, 1 - y)]


def _remote(src, dst, send_sem, recv_sem, dev):
    return pltpu.make_async_remote_copy(src_ref=src, dst_ref=dst, send_sem=send_sem, recv_sem=recv_sem,
                                        device_id=dev, device_id_type=MESH)


VMEM_SPEC = pl.BlockSpec(memory_space=pltpu.VMEM)
ANY_SPEC = pl.BlockSpec(memory_space=pl.ANY)


def gather_all(v, name, with_sum):
    r = v.shape[0]

    def body(v_ref, out_ref, *rest):
        send_s, recv_s = rest[-2:]
        x, y, c = _pos()
        me = 4 * x + 2 * y + c
        out_ref[me] = v_ref[...]
        peers = []
        for m in range(1, 8):
            px = 1 - x if m & 4 else x
            py = 1 - y if m & 2 else y
            pc = 1 - c if m & 1 else c
            peers.append((px, py, pc))
        copies = [_remote(v_ref, out_ref.at[me], send_s.at[i], recv_s.at[i], dev) for i, dev in enumerate(peers)]
        for cp in copies:
            cp.start()
        for i, (px, py, pc) in enumerate(peers):
            _remote(v_ref, out_ref.at[4 * px + 2 * py + pc], send_s.at[i], recv_s.at[i], (px, py, pc)).wait_recv()
        for cp in copies:
            cp.wait_send()
        if with_sum:
            acc = out_ref[0]
            for b in range(1, 8):
                acc = acc + out_ref[b]
            rest[0][...] = acc

    out_shape = [_sds((8, r, LANES), F32)] + ([_sds((r, LANES), F32)] if with_sum else [])
    return pl.pallas_call(
        body, name=name, in_specs=[VMEM_SPEC], out_specs=[VMEM_SPEC] * len(out_shape), out_shape=out_shape,
        scratch_shapes=[pltpu.SemaphoreType.DMA((7,)), pltpu.SemaphoreType.DMA((7,))],
    )(v)


def mod_exchange(c_all, w_ada_sh, b_sh):
    def body(c_ref, w_ref, b_ref, out_ref, sc_ref, modp, send_s, recv_s):
        cv = c_ref[...]
        sc = cv * _sigmoid(cv)
        sc_ref[...] = sc
        modp[...] = jnp.dot(sc, w_ref[...], precision=lax.Precision.HIGHEST, preferred_element_type=F32) + b_ref[...]
        x, y, c = _pos()
        k = 2 * x + y
        out_ref[k] = modp[...]
        chips = _other_chips(x, y)
        copies = [_remote(modp, out_ref.at[k], send_s.at[j], recv_s.at[j], (cx, cy, c)) for j, (cx, cy) in enumerate(chips)]
        for cp in copies:
            cp.start()
        for j, (cx, cy) in enumerate(chips):
            _remote(modp, out_ref.at[2 * cx + cy], send_s.at[j], recv_s.at[j], (cx, cy, c)).wait_recv()
        for cp in copies:
            cp.wait_send()

    n = w_ada_sh.shape[1]
    return pl.pallas_call(
        body, name="mod_exchange", in_specs=[VMEM_SPEC] * 3, out_specs=[VMEM_SPEC] * 2,
        out_shape=[_sds((4, 8, n), F32), _sds((8, D), F32)],
        scratch_shapes=[pltpu.VMEM((8, n), F32), pltpu.SemaphoreType.DMA((3,)), pltpu.SemaphoreType.DMA((3,))],
        compiler_params=_params(VMEM_MB),
    )(c_all, w_ada_sh, b_sh)


def gather_weights(bufs):
    n = len(bufs)

    def body(*refs):
        outs = refs[n:2 * n]
        send_s, recv_s, fsend_s, frecv_s = refs[2 * n:]
        x, y, c = _pos()
        k = 2 * x + y
        chips = _other_chips(x, y)
        sends, fwds = [], []
        for a in range(n):
            half = outs[a].shape[1] // 2
            rows = pl.ds(c * half, half)
            for j, (cx, cy) in enumerate(chips):
                cp = _remote(outs[a].at[k, rows], outs[a].at[k, rows], send_s.at[3 * a + j], recv_s.at[3 * a + j], (cx, cy, c))
                cp.start()
                sends.append(cp)
        for a in range(n):
            half = outs[a].shape[1] // 2
            rows = pl.ds(c * half, half)
            for j, (cx, cy) in enumerate(chips):
                kj = 2 * cx + cy
                _remote(outs[a].at[kj, rows], outs[a].at[kj, rows], send_s.at[3 * a + j], recv_s.at[3 * a + j], (cx, cy, c)).wait_recv()
                fw = _remote(outs[a].at[kj, rows], outs[a].at[kj, rows], fsend_s.at[3 * a + j], frecv_s.at[3 * a + j], (x, y, 1 - c))
                fw.start()
                fwds.append(fw)
        for a in range(n):
            half = outs[a].shape[1] // 2
            orows = pl.ds((1 - c) * half, half)
            for j, (cx, cy) in enumerate(chips):
                kj = 2 * cx + cy
                _remote(outs[a].at[kj, orows], outs[a].at[kj, orows], fsend_s.at[3 * a + j], frecv_s.at[3 * a + j], (x, y, 1 - c)).wait_recv()
        for cp in sends + fwds:
            cp.wait_send()

    return pl.pallas_call(
        body, name="gather_weights", in_specs=[ANY_SPEC] * n, out_specs=[ANY_SPEC] * n,
        out_shape=[_sds(b.shape, b.dtype) for b in bufs],
        scratch_shapes=[pltpu.SemaphoreType.DMA((3 * n,))] * 4,
        input_output_aliases={a: a for a in range(n)},
    )(*bufs)


def cast_into_slabs(ws, sizes, chip, comm):
    tr = 64
    n = len(ws)
    n_in = [w.shape[0] // tr for w in ws]
    n_out = [r // tr for r, _ in sizes]
    steps = max(n_out)
    nci, nco = len(comm.ins), len(comm.out_shapes)

    def body(chip_ref, *refs):
        del chip_ref
        w_refs, cin = refs[:n], refs[n:n + nci]
        o_refs, cout = refs[n + nci:2 * n + nci], refs[2 * n + nci:2 * n + nci + nco]
        sems = refs[2 * n + nci + nco:]
        i = pl.program_id(0)

        @pl.when(i == 0)
        def _():
            comm.start(cin, cout, sems)

        for a in range(n):
            c0, cols = ws[a].shape[1], sizes[a][1]

            @pl.when(i < n_in[a])
            def _(a=a, c0=c0, cols=cols):
                o_refs[a][:, 0:c0] = w_refs[a][...].astype(BF16)
                if cols > c0:
                    o_refs[a][:, c0:] = jnp.zeros((tr, cols - c0), BF16)

            if n_out[a] > n_in[a]:
                @pl.when((i >= n_in[a]) & (i < n_out[a]))
                def _(a=a, cols=cols):
                    o_refs[a][...] = jnp.zeros((tr, cols), BF16)

        @pl.when(i == steps - 1)
        def _():
            comm.wait(cin, cout, sems)

    any_spec = pl.BlockSpec(memory_space=pl.ANY)
    grid_spec = pltpu.PrefetchScalarGridSpec(
        num_scalar_prefetch=1, grid=(steps,),
        in_specs=[pl.BlockSpec((tr, w.shape[1]), functools.partial(lambda i, k, last: (jnp.minimum(i, last), 0), last=n_in[a] - 1))
                  for a, w in enumerate(ws)] + [any_spec] * nci,
        out_specs=[pl.BlockSpec((None, tr, sizes[a][1]), functools.partial(lambda i, k, last: (k[0], jnp.minimum(i, last), 0), last=n_out[a] - 1))
                   for a in range(n)] + [any_spec] * nco,
        scratch_shapes=[pltpu.SemaphoreType.DMA((s,)) for s in comm.sems])
    res = pl.pallas_call(
        body, name="cast_into_slabs", grid_spec=grid_spec,
        out_shape=[_sds((4,) + tuple(sz), BF16) for sz in sizes] + comm.out_shapes,
        input_output_aliases={1 + n + ci: n + co for ci, co in comm.aliases.items()},
    )(chip, *ws, *comm.ins)
    return list(res[:n]), list(res[n:])


def cast_into_slab(w, rows, cols, chip, tag):
    r0, c0 = w.shape
    tr = 256 if (r0 % 256 == 0 and rows % 256 == 0) else 64
    n_in, n_out = r0 // tr, rows // tr

    def body(chip_ref, w_ref, o_ref):
        del chip_ref
        i = pl.program_id(0)

        @pl.when(i < n_in)
        def _():
            o_ref[:, 0:c0] = w_ref[...].astype(BF16)
            if cols > c0:
                o_ref[:, c0:] = jnp.zeros((tr, cols - c0), BF16)

        @pl.when(i >= n_in)
        def _():
            o_ref[...] = jnp.zeros((tr, cols), BF16)

    grid_spec = pltpu.PrefetchScalarGridSpec(
        num_scalar_prefetch=1, grid=(n_out,),
        in_specs=[pl.BlockSpec((tr, c0), lambda i, k: (jnp.minimum(i, n_in - 1), 0))],
        out_specs=pl.BlockSpec((None, tr, cols), lambda i, k: (k[0], i, 0)))
    return pl.pallas_call(body, name="cast_" + tag, grid_spec=grid_spec, out_shape=_sds((4, rows, cols), BF16))(chip, w)


def _row_tile(rows, cap=256):
    t = cap
    while rows % t or t % 8:
        t -= 8
    return t


def _comm_wait(sends, recvs, local=()):
    for cp in recvs:
        cp.wait_recv()
    for cp in sends:
        cp.wait_send()
    for cp in local:
        cp.wait()


def ag_ici(bufs):
    n = len(bufs)

    def copies(ins, outs, sems):
        send_s, recv_s = sems
        x, y, c = _pos()
        k = 2 * x + y
        sends, recvs = [], []
        for a in range(n):
            half = outs[a].shape[1] // 2
            rows = pl.ds(c * half, half)
            for j, (cx, cy) in enumerate(_other_chips(x, y)):
                sem = (send_s.at[3 * a + j], recv_s.at[3 * a + j], (cx, cy, c))
                sends.append(_remote(outs[a].at[k, rows], outs[a].at[k, rows], *sem))
                recvs.append(_remote(outs[a].at[k, rows], outs[a].at[2 * cx + cy, rows], *sem))
        return sends, recvs

    def start(ins, outs, sems):
        for cp in copies(ins, outs, sems)[0]:
            cp.start()

    def wait(ins, outs, sems):
        _comm_wait(*copies(ins, outs, sems))

    return Comm(bufs, [_sds(b.shape, b.dtype) for b in bufs], [3 * n, 3 * n], start, wait, aliases={a: a for a in range(n)})


def ag_d2d(bufs):
    n = len(bufs)

    def copies(ins, outs, sems):
        send_s, recv_s = sems
        x, y, c = _pos()
        sends, recvs = [], []
        for a in range(n):
            half = outs[a].shape[1] // 2
            rows, orows = pl.ds(c * half, half), pl.ds((1 - c) * half, half)
            for j, (cx, cy) in enumerate(_other_chips(x, y)):
                kj = 2 * cx + cy
                sem = (send_s.at[3 * a + j], recv_s.at[3 * a + j], (x, y, 1 - c))
                sends.append(_remote(outs[a].at[kj, rows], outs[a].at[kj, rows], *sem))
                recvs.append(_remote(outs[a].at[kj, orows], outs[a].at[kj, orows], *sem))
        return sends, recvs

    def start(ins, outs, sems):
        for cp in copies(ins, outs, sems)[0]:
            cp.start()

    def wait(ins, outs, sems):
        _comm_wait(*copies(ins, outs, sems))

    return Comm(bufs, [_sds(b.shape, b.dtype) for b in bufs], [3 * n, 3 * n], start, wait, aliases={a: a for a in range(n)})


def rs_a(grads):
    n = len(grads)

    def copies(ins, outs, sems):
        send_s, recv_s = sems
        x, y, c = _pos()
        cps = []
        for a in range(n):
            half = ins[a].shape[1] // 2
            cps.append(_remote(ins[a].at[:, pl.ds((1 - c) * half, half), :], outs[a], send_s.at[a], recv_s.at[a], (x, y, 1 - c)))
        return cps

    def start(ins, outs, sems):
        for cp in copies(ins, outs, sems):
            cp.start()

    def wait(ins, outs, sems):
        cps = copies(ins, outs, sems)
        _comm_wait(cps, cps)

    return Comm(grads, [_sds((4, g.shape[1] // 2, g.shape[2]), g.dtype) for g in grads], [n, n], start, wait)


def rs_b(pres):
    n = len(pres)

    def copies(ins, outs, sems):
        send_s, recv_s = sems
        x, y, c = _pos()
        cps = []
        for a in range(n):
            for j, (cx, cy) in enumerate(_other_chips(x, y)):
                cps.append(_remote(ins[a].at[2 * cx + cy], outs[a].at[j], send_s.at[3 * a + j], recv_s.at[3 * a + j], (cx, cy, c)))
        return cps

    def start(ins, outs, sems):
        for cp in copies(ins, outs, sems):
            cp.start()

    def wait(ins, outs, sems):
        cps = copies(ins, outs, sems)
        _comm_wait(cps, cps)

    return Comm(pres, [_sds((3,) + p_.shape[1:], p_.dtype) for p_ in pres], [3 * n, 3 * n], start, wait)


def rs_b_rows(pre, buf, lo, n):
    def copies(ins, outs, sems):
        send_s, recv_s = sems
        x, y, c = _pos()
        rows = pl.ds(lo, n)
        return [_remote(ins[0].at[2 * cx + cy, rows], outs[0].at[j, rows], send_s.at[j], recv_s.at[j], (cx, cy, c))
                for j, (cx, cy) in enumerate(_other_chips(x, y))]

    def start(ins, outs, sems):
        for cp in copies(ins, outs, sems):
            cp.start()

    def wait(ins, outs, sems):
        cps = copies(ins, outs, sems)
        _comm_wait(cps, cps)

    ins = [pre] if buf is None else [pre, buf]
    return Comm(ins, [_sds((3,) + pre.shape[1:], pre.dtype)], [3, 3], start, wait, aliases={} if buf is None else {1: 0})


def rs_c(reds):
    n = len(reds)

    def copies(ins, outs, sems):
        send_s, recv_s = sems
        x, y, c = _pos()
        sends, recvs = [], []
        for a in range(n):
            half = outs[a].shape[0] // 2
            rows, orows = pl.ds(c * half, half), pl.ds((1 - c) * half, half)
            sem = (send_s.at[a], recv_s.at[a], (x, y, 1 - c))
            sends.append(_remote(outs[a].at[rows], outs[a].at[rows], *sem))
            recvs.append(_remote(outs[a].at[orows], outs[a].at[orows], *sem))
        return sends, recvs

    def start(ins, outs, sems):
        for cp in copies(ins, outs, sems)[0]:
            cp.start()

    def wait(ins, outs, sems):
        _comm_wait(*copies(ins, outs, sems))

    return Comm(reds, [_sds(r_.shape, r_.dtype) for r_ in reds], [n, n], start, wait, aliases={a: a for a in range(n)})


def comm_join(*comms):
    ni = np.cumsum([0] + [len(c.ins) for c in comms])
    no = np.cumsum([0] + [len(c.out_shapes) for c in comms])
    ns = np.cumsum([0] + [len(c.sems) for c in comms])

    def parts(ins, outs, sems):
        return [(c, ins[ni[i]:ni[i + 1]], outs[no[i]:no[i + 1]], sems[ns[i]:ns[i + 1]]) for i, c in enumerate(comms)]

    def start(ins, outs, sems):
        for c, a, b, s in parts(ins, outs, sems):
            c.start(a, b, s)

    def wait(ins, outs, sems):
        for c, a, b, s in parts(ins, outs, sems):
            c.wait(a, b, s)

    aliases = {int(ni[i]) + k: int(no[i]) + v for i, c in enumerate(comms) for k, v in c.aliases.items()}
    return Comm(sum((c.ins for c in comms), []), sum((c.out_shapes for c in comms), []), sum((c.sems for c in comms), []),
                start, wait, aliases)


def comm_only(comm, name):
    nci, nco = len(comm.ins), len(comm.out_shapes)

    def body(*refs):
        ins, outs, sems = refs[:nci], refs[nci:nci + nco], refs[nci + nco:]
        comm.start(ins, outs, sems)
        comm.wait(ins, outs, sems)

    return pl.pallas_call(
        body, name=name, in_specs=[ANY_SPEC] * nci, out_specs=[ANY_SPEC] * nco, out_shape=comm.out_shapes,
        scratch_shapes=[pltpu.SemaphoreType.DMA((s,)) for s in comm.sems],
        input_output_aliases=comm.aliases,
    )(*comm.ins)


def rs_add_halves(g, other, core, name):
    _, r, cdim = g.shape
    half = r // 2
    tr = _row_tile(half, 256)
    nb = half // tr

    def body(core_ref, g_ref, o_ref, out_ref):
        del core_ref
        out_ref[...] = (g_ref[...] + o_ref[...]).astype(BF16)

    grid_spec = pltpu.PrefetchScalarGridSpec(
        num_scalar_prefetch=1, grid=(4, nb),
        in_specs=[pl.BlockSpec((None, tr, cdim), lambda k, i, cr: (k, cr[0] * nb + i, 0)),
                  pl.BlockSpec((None, tr, cdim), lambda k, i, cr: (k, i, 0))],
        out_specs=pl.BlockSpec((None, tr, cdim), lambda k, i, cr: (k, i, 0)))
    return pl.pallas_call(body, name=name, grid_spec=grid_spec, out_shape=_sds((4, half, cdim), BF16))(core, g, other)


def rs_add_slabs(t, pre, place, name):
    _, half, cdim = t.shape
    tr = _row_tile(half, 256)
    nb = half // tr

    def body(place_ref, own_ref, t_ref, out_ref):
        del place_ref
        out_ref[...] = ((own_ref[...].astype(F32) + t_ref[0].astype(F32)) + t_ref[1].astype(F32)) + t_ref[2].astype(F32)

    grid_spec = pltpu.PrefetchScalarGridSpec(
        num_scalar_prefetch=1, grid=(nb,),
        in_specs=[pl.BlockSpec((None, tr, cdim), lambda i, pr: (pr[0], i, 0)), pl.BlockSpec((3, tr, cdim), lambda i, pr: (0, i, 0))],
        out_specs=pl.BlockSpec((tr, cdim), lambda i, pr: (pr[1] * nb + i, 0)))
    return pl.pallas_call(body, name=name, grid_spec=grid_spec, out_shape=_sds((2 * half, cdim), F32))(place, pre, t)


def _adam_math(w, g, m, v):
    m = ADAM_B1 * m + (1.0 - ADAM_B1) * g
    v = ADAM_B2 * v + (1.0 - ADAM_B2) * (g * g)
    m_hat = m / (1.0 - ADAM_B1 ** ADAM_STEP)
    v_hat = v / (1.0 - ADAM_B2 ** ADAM_STEP)
    delta = -ADAM_LR * (m_hat / (jnp.sqrt(v_hat) + ADAM_EPS) + ADAM_WD * w)
    return delta, m, v


def adam(w, g, m, v, name, comm=None):
    r, cdim = w.shape
    tr = _row_tile(r) if r >= 8 else r

    def body(w_ref, g_ref, m_ref, v_ref, g_out, d_ref, nm_ref, nv_ref):
        gv = g_ref[:, :cdim]
        g_out[...] = gv
        d_ref[...], nm_ref[...], nv_ref[...] = _adam_math(w_ref[...], gv, m_ref[...], v_ref[...])

    blk = pl.BlockSpec((tr, cdim), lambda i: (i, 0))
    return _hosted_call(
        body, comm, (w, g, m, v), name=name, grid=(r // tr,),
        in_specs=[blk, pl.BlockSpec((tr, g.shape[1]), lambda i: (i, 0)), blk, blk],
        out_specs=[blk] * 4, out_shape=[_sds((r, cdim), F32)] * 4)


def adam_w_ada(sc_t, dmod_sh, w, m, v, comm=None):
    r, cdim = w.shape
    tr = 256

    def body(s_ref, d_ref, w_ref, m_ref, v_ref, g_ref, dl_ref, nm_ref, nv_ref):
        g = jnp.dot(s_ref[...], d_ref[...], precision=lax.Precision.HIGHEST, preferred_element_type=F32)
        g_ref[...] = g
        dl_ref[...], nm_ref[...], nv_ref[...] = _adam_math(w_ref[...], g, m_ref[...], v_ref[...])

    blk = pl.BlockSpec((tr, cdim), lambda i: (i, 0))
    return _hosted_call(
        body, comm, (sc_t, dmod_sh, w, m, v), name="adam_w_ada", grid=(r // tr,),
        in_specs=[pl.BlockSpec((tr, LANES), lambda i: (i, 0)), pl.BlockSpec((LANES, cdim), lambda i: (0, 0)), blk, blk, blk],
        out_specs=[blk] * 4, out_shape=[_sds((r, cdim), F32)] * 4)


SMALL_ROWS = 80


def kernel(x, c, w_ada, b_ada, g_mix, w_in, b_fgate, w_br_a, w_br_b, w_out, g_ffn, w_ffn_gate, w_ffn_up, w_ffn_down, g_final, loss_target, m_w_ada, m_b_ada, m_g_mix, m_w_in, m_b_fgate, m_w_br_a, m_w_br_b, m_w_out, m_g_ffn, m_w_ffn_gate, m_w_ffn_up, m_w_ffn_down, m_g_final, v_w_ada, v_b_ada, v_g_mix, v_w_in, v_b_fgate, v_w_br_a, v_w_br_b, v_w_out, v_g_ffn, v_w_ffn_gate, v_w_ffn_up, v_w_ffn_down, v_g_final):
    xi, yi, ci = _pos()
    chip = 2 * xi + yi
    seq = 4 * xi + 2 * yi + ci
    n_ada = w_ada.shape[2]

    c_all = gather_all(c.reshape(8, LANES), "gather_c", False)[0].reshape(8, D)
    b_sh = lax.dynamic_slice(b_ada, (0, chip * n_ada), (1, n_ada))
    mod_all, sc = mod_exchange(c_all, w_ada[0], b_sh)
    mod = lax.dynamic_index_in_dim(mod_all, seq, axis=1, keepdims=False).reshape(6, D)
    mod8 = jnp.pad(mod, ((0, 2), (0, 0)))

    core = ci.astype(jnp.int32).reshape(1)
    chip1 = chip.astype(jnp.int32).reshape(1)
    place = jnp.stack([chip, ci]).astype(jnp.int32)
    s_in = cast_into_slab(w_in[0], D, IN_SHARD_PAD, chip1, "w_in")
    (s_bra, s_brb, s_out, s_gate, s_up, s_down), (g_in,) = cast_into_slabs(
        [w_br_a[0], w_br_b[0], w_out[0], w_ffn_gate[0], w_ffn_up[0], w_ffn_down[0]],
        [(512, 256), (256, 256), (256, D), (D, FF_PAD), (D, FF_PAD), (FF_PAD, D)], chip1, ag_ici([s_in]))
    xs, tgt, g_fin = x[0], loss_target[0], g_final.reshape(1, D)

    def halves(gs, others, tag):
        return [rs_add_halves(g, o, core, f"rs_{tag}_halves_{i}") for i, (g, o) in enumerate(zip(gs, others))]

    def slab_sums(ts, pres, tag):
        return [rs_add_slabs(t, pre, place, f"rs_{tag}_slabs_{i}") for i, (t, pre) in enumerate(zip(ts, pres))]

    tabs = rope_tables()
    h1, (g_in,) = norm_mod_fwd(xs, g_mix, mod8, 0, 1, comm=ag_d2d([g_in]))
    w_lay = lay_from_shards(g_in)
    p, mix_w = in_proj_fwd(h1, w_lay, tabs, comm=ag_ici([s_bra, s_brb, s_out]))
    frow, fraw, fcol = fgate_fwd(p, jnp.pad(b_fgate, ((0, 0), (0, LANES - 8))))
    (ya_att, gcol), res = fox_fwd(p, fcol, comm=comm_join(ag_d2d(mix_w), ag_ici([s_gate, s_up])))
    g_bra, g_brb, g_out = res[:3]
    (yb, lse_b), res = dil_fwd(p, comm=comm_join(ag_d2d(res[3:]), ag_ici([s_down])))
    w_gate, w_up = res[:2]
    w_bra = g_bra.transpose(1, 0, 2).reshape(512, D)
    w_brb = g_brb.transpose(1, 0, 2).reshape(256, D)
    w_o = g_out.reshape(D, D)
    (merged, ya, ybp), (g_down,) = merge_fwd(ya_att, yb, p, w_bra, w_brb, comm=ag_d2d(res[2:]))
    w_down = g_down.reshape(FFP, D)
    mix, x1, h2 = out_proj_fwd(merged, w_o, xs, mod8, g_ffn)
    a, u, z = ffn_up_fwd(h2, w_gate, w_up)
    dx2, dffn, dg_final, dga_f, loss_part = ffn_down_loss(z, w_down, x1, mod8, g_fin, tgt)

    da, du, dw_down = ffn_down_bwd(dffn, w_down, a, u, z)
    g_down = [dw_down.reshape(4, FF_PAD, D)]
    dh2a, oth = mm_nt(da, w_gate, "ffn_gate_dx", comm=rs_a(g_down))
    pre_down = halves(g_down, oth, "down")
    dh2b, _ = mm_nt(du, w_up, "ffn_up_dx")
    dw_gate, _ = mm_tn(h2, da, "ffn_gate_dw", shard_major=True)
    dw_up, _ = mm_tn(h2, du, "ffn_up_dw", shard_major=True)
    g_gu = [dw_gate, dw_up]
    (dx1, dp1, dya_att, dyb, cs_mid, dw_out, dw_bra, dw_brb), res = mid_bwd(
        dh2a, dh2b, x1, dx2, mix, mod8, g_ffn, p, ya, ybp, merged, ya_att, yb, w_o, w_bra, w_brb,
        comm=comm_join(rs_b(pre_down), rs_a(g_gu)))
    red_down = slab_sums(res[:1], pre_down, "down")
    pre_gu = halves(g_gu, res[1:], "gu")
    g_mix3 = [dw_bra.reshape(512, 4, 256).transpose(1, 0, 2), dw_brb.reshape(256, 4, 256).transpose(1, 0, 2), dw_out.reshape(4, 256, D)]
    (dp2, dfrow, dfcol), res = fox_bwd(p, dya_att, ya_att, gcol, fcol, dp1,
                                       comm=comm_join(rs_b(pre_gu), rs_c(red_down), rs_a(g_mix3)))
    red_gu = slab_sums(res[:2], pre_gu, "gu")
    r_down = res[2]
    pre_mix3 = halves(g_mix3, res[3:], "mix")
    dp3, db_fg = fgate_bwd(dfrow.reshape(8, S), dfcol, fraw, dp2)
    dp4, res = dil_bwd(p, dyb, yb, lse_b, tabs, dp3, comm=comm_join(rs_c(red_gu), rs_b(pre_mix3)))
    r_gate, r_up = res[:2]
    red_mix3 = slab_sums(res[2:], pre_mix3, "mix")
    dw_lay, (r_bra, r_brb, r_out) = mm_tn(h1, dp4, "in_proj_dw", comm=rs_c(red_mix3))
    g_in4 = [shards_from_lay(dw_lay)]
    dh1, oth = mm_nt(dp4, w_lay, "in_proj_dx", comm=rs_a(g_in4))
    (pre_in,) = halves(g_in4, oth, "in")
    qrows = pre_in.shape[1] // 4
    (dx, cs_in), (t_in,) = in_bwd_tail(dh1, xs, dx1, mod8, g_mix, comm=rs_b_rows(pre_in, None, 0, qrows))

    dmod = jnp.concatenate([cs_in[0:2], cs_mid[3:4], cs_mid[0:2], dga_f], axis=0)
    small = dict(dmod=dmod, dg_mix=cs_in[2:3], dg_ffn=cs_mid[2:3], dg_final=dg_final, db_fgate=db_fg[:, 0], loss=loss_part[0, 0])
    sv = jnp.concatenate([
        small["dmod"].reshape(48, LANES), small["dg_mix"].reshape(8, LANES), small["dg_ffn"].reshape(8, LANES),
        small["dg_final"].reshape(8, LANES), jnp.pad(small["db_fgate"], (0, LANES - 8)).reshape(1, LANES),
        jnp.broadcast_to(small["loss"], (1, LANES)), jnp.zeros((SMALL_ROWS - 74, LANES), F32)], axis=0)
    sv_all, sv_sum = gather_all(sv, "gather_small", True)
    loss = sv_sum[73, 0]
    g_small = dict(b_ada=sv_sum[0:48].reshape(1, 6 * D), g_mix=sv_sum[48:56].reshape(1, D), g_ffn=sv_sum[56:64].reshape(1, D),
                   g_final=sv_sum[64:72].reshape(D), b_fgate=sv_sum[72, 0:8].reshape(1, 8))

    dmod_all = lax.dynamic_slice(sv_all[:, 0:48, :].reshape(8, 6 * D), (0, chip * n_ada), (8, n_ada))
    (g_ada, d_ada, nm_ada, nv_ada), (t_in,) = adam_w_ada(
        jnp.pad(sc.T, ((0, 0), (0, LANES - 8))), jnp.pad(dmod_all, ((0, LANES - 8), (0, 0))), w_ada[0], m_w_ada[0], v_w_ada[0],
        comm=rs_b_rows(pre_in, t_in, qrows, qrows))

    big = dict(w_in=(w_in, m_w_in, v_w_in), w_br_a=(w_br_a, m_w_br_a, v_w_br_a), w_br_b=(w_br_b, m_w_br_b, v_w_br_b),
               w_out=(w_out, m_w_out, v_w_out), w_ffn_gate=(w_ffn_gate, m_w_ffn_gate, v_w_ffn_gate),
               w_ffn_up=(w_ffn_up, m_w_ffn_up, v_w_ffn_up), w_ffn_down=(w_ffn_down, m_w_ffn_down, v_w_ffn_down))
    gpad = dict(w_br_a=r_bra, w_br_b=r_brb, w_out=r_out, w_ffn_gate=r_gate, w_ffn_up=r_up, w_ffn_down=r_down)
    upd = {}
    for part, nm in ((2, "w_ffn_gate"), (3, "w_ffn_up")):
        w, m, v = big[nm]
        upd[nm], (t_in,) = adam(w[0], gpad[nm], m[0], v[0], "adam_" + nm, comm=rs_b_rows(pre_in, t_in, part * qrows, qrows))
    (gpad["w_in"],) = comm_only(rs_c(slab_sums([t_in], [pre_in], "in")), "rs_in_share")
    for nm, (w, m, v) in big.items():
        if nm not in upd:
            upd[nm] = adam(w[0], gpad[nm], m[0], v[0], "adam_" + nm)[0]

    def pack(gm, gf, gl, ba, bf):
        rows = [gm.reshape(1, D), gf.reshape(1, D), gl.reshape(1, D), ba.reshape(6, D), jnp.pad(bf.reshape(1, 8), ((0, 0), (0, D - 8)))]
        return jnp.concatenate(rows + [jnp.zeros((6, D), F32)], axis=0)

    packed = adam(pack(g_mix, g_ffn, g_final, b_ada, b_fgate),
                  pack(g_small["g_mix"], g_small["g_ffn"], g_small["g_final"], g_small["b_ada"], g_small["b_fgate"]),
                  pack(m_g_mix, m_g_ffn, m_g_final, m_b_ada, m_b_fgate), pack(v_g_mix, v_g_ffn, v_g_final, v_b_ada, v_b_fgate),
                  "adam_small")[0]

    def unpack(t):
        return dict(g_mix=t[0:1], g_ffn=t[1:2], g_final=t[2], b_ada=t[3:9].reshape(1, 6 * D), b_fgate=t[9:10, 0:8])

    small_upd = [unpack(t) for t in packed[1:]]
    order =["w_ada", "b_ada", "g_mix", "w_in", "b_fgate", "w_br_a", "w_br_b", "w_out", "g_ffn", "w_ffn_gate", "w_ffn_up", "w_ffn_down", "g_final"]

    def leaf(nm, which):
        if nm == "w_ada":
            return (g_ada, d_ada, nm_ada, nv_ada)[which][None]
        if nm in big:
            return upd[nm][which][None]
        return g_small[nm] if which == 0 else small_upd[which - 1][nm]

    outs = [loss, dx[None]]
    for which in range(4):
        outs += [leaf(nm, which) for nm in order]
    return tuple(outs)
```

```python
import functools

import numpy as np
import jax
import jax.numpy as jnp
from jax import lax
from jax.experimental import pallas as pl
from jax.experimental.pallas import tpu as pltpu

F32, BF16 = jnp.float32, jnp.bfloat16
S, D = 2048, 1024
HD = 64
LANES = 128
N_FOX_PAIRS, N_DIL_PAIRS = 4, 2
DIL_GROUPS = ((1, 16), (4, 4), (16, 1))
SPAN = 128
ROT_DIM, ROPE_THETA = 16, 500000.0
D_FF, FF_SHARD, FF_PAD = 2816, 704, 768
FFP = 4 * FF_PAD
IN_COLS, IN_SHARD, IN_SHARD_PAD = 5896, 1474, 1536
LAY_B, LAY_A, LAY_F, LAY_G, LAY_N = 0, 2304, 3840, 4096, 6144
EPS, NEG = 1e-6, -1e30
SCALE = HD ** -0.5
ADAM_LR, ADAM_B1, ADAM_B2, ADAM_EPS, ADAM_WD, ADAM_STEP = 0.001, 0.9, 0.999, 1e-08, 0.01, 10
VMEM_MB = 56
MESH = pl.DeviceIdType.MESH


def _params(vmem_mb=None, **kw):
    if vmem_mb is not None:
        kw["vmem_limit_bytes"] = vmem_mb * 1024 * 1024
    return pltpu.CompilerParams(**kw)


def _sds(shape, dtype):
    return jax.ShapeDtypeStruct(shape, dtype)


def _sigmoid(x):
    return 1.0 / (1.0 + jnp.exp(-x))


def _colsum8(x):
    tm, n = x.shape
    return jnp.sum(x.reshape(tm // 8, 8, n), axis=0)


class Comm:
    def __init__(self, ins, out_shapes, sems, start, wait, aliases=None):
        self.ins, self.out_shapes, self.sems = list(ins), list(out_shapes), list(sems)
        self.start, self.wait, self.aliases = start, wait, dict(aliases or {})


def _hosted_call(body, comm, args, *, name, grid, in_specs, out_specs, out_shape, scratch_shapes=(), aliases=None, vmem_mb=None):
    single = not isinstance(out_shape, (list, tuple))
    out_specs_l = [out_specs] if single else list(out_specs)
    out_shape_l = [out_shape] if single else list(out_shape)
    n_in, n_out, n_scr = len(in_specs), len(out_shape_l), len(scratch_shapes)
    aliases = dict(aliases or {})
    if comm is None:
        res = pl.pallas_call(body, name=name, grid=grid, in_specs=list(in_specs), out_specs=out_specs, out_shape=out_shape,
                             scratch_shapes=list(scratch_shapes), input_output_aliases=aliases,
                             compiler_params=_params(vmem_mb))(*args)
        return res, []
    nci, nco = len(comm.ins), len(comm.out_shapes)

    def wrapped(*refs):
        main_in, cin = refs[:n_in], refs[n_in:n_in + nci]
        o0 = n_in + nci
        main_out, cout = refs[o0:o0 + n_out], refs[o0 + n_out:o0 + n_out + nco]
        s0 = o0 + n_out + nco
        scr, sems = refs[s0:s0 + n_scr], refs[s0 + n_scr:]
        ids = [pl.program_id(i) for i in range(len(grid))]
        first = functools.reduce(jnp.logical_and, [i == 0 for i in ids])
        last = functools.reduce(jnp.logical_and, [i == g - 1 for i, g in zip(ids, grid)])

        @pl.when(first)
        def _():
            comm.start(cin, cout, sems)

        body(*main_in, *main_out, *scr)

        @pl.when(last)
        def _():
            comm.wait(cin, cout, sems)

    for ci, co in comm.aliases.items():
        aliases[n_in + ci] = n_out + co
    any_spec = pl.BlockSpec(memory_space=pl.ANY)
    res = pl.pallas_call(
        wrapped, name=name, grid=grid, in_specs=list(in_specs) + [any_spec] * nci, out_specs=out_specs_l + [any_spec] * nco,
        out_shape=out_shape_l + comm.out_shapes,
        scratch_shapes=list(scratch_shapes) + [pltpu.SemaphoreType.DMA((s,)) for s in comm.sems],
        input_output_aliases=aliases, compiler_params=_params(vmem_mb))(*args, *comm.ins)
    main = list(res[:n_out])
    return (main[0] if single else main), list(res[n_out:])


def norm_mod_fwd(x, g, mod, shift_row, scale_row, comm=None):
    tm = 256

    def body(x_ref, g_ref, mod_ref, h_ref):
        xv = x_ref[...]
        r = lax.rsqrt(jnp.mean(xv * xv, axis=1, keepdims=True) + EPS)
        n = xv * r * g_ref[...]
        h = n * (1.0 + mod_ref[scale_row:scale_row + 1, :]) + mod_ref[shift_row:shift_row + 1, :]
        h_ref[...] = h.astype(BF16)

    return _hosted_call(
        body, comm, (x, g, mod), name="norm_mod_fwd", grid=(S // tm,),
        in_specs=[pl.BlockSpec((tm, D), lambda i: (i, 0)), pl.BlockSpec((1, D), lambda i: (0, 0)),
                  pl.BlockSpec((8, D), lambda i: (0, 0))],
        out_specs=pl.BlockSpec((tm, D), lambda i: (i, 0)),
        out_shape=_sds((S, D), BF16))


def rope_tables():
    pos = jnp.arange(S, dtype=F32)
    inv_freq = ROPE_THETA ** (-jnp.arange(0, ROT_DIM, 2, dtype=F32) / ROT_DIM)
    ang = pos[:, None] * inv_freq[None, :]
    cos, sin = jnp.cos(ang), jnp.sin(ang)
    one, zero = jnp.ones((S, HD - ROT_DIM), F32), jnp.zeros((S, HD - ROT_DIM), F32)
    z8 = jnp.zeros((S, 8), F32)
    c = jnp.concatenate([cos, cos, one], axis=1)
    s1 = jnp.concatenate([-sin, z8, zero], axis=1)
    s2 = jnp.concatenate([z8, sin, zero], axis=1)
    return tuple(jnp.concatenate([t, t], axis=1) for t in (c, s1, s2))


def _rope(y, c, s1, s2):
    return y * c + pltpu.roll(y, LANES - 8, 1) * s1 + pltpu.roll(y, 8, 1) * s2


def _rope_bwd(dy, c, s1, s2):
    return dy * c + pltpu.roll(dy * s1, 8, 1) + pltpu.roll(dy * s2, LANES - 8, 1)


def in_proj_fwd(h, w_lay, tabs, comm=None):
    tm, tn = 2048, 768
    n_rope = N_DIL_PAIRS * 3 // 2

    def body(a_ref, w_ref, c_ref, s1_ref, s2_ref, o_ref):
        j = pl.program_id(0)
        y = jnp.dot(a_ref[...], w_ref[...], preferred_element_type=F32)

        @pl.when(j < n_rope)
        def _():
            c, s1, s2 = c_ref[...], s1_ref[...], s2_ref[...]
            for t in range(tn // LANES):
                chunk = y[:, LANES * t:LANES * (t + 1)]
                o_ref[:, LANES * t:LANES * (t + 1)] = chunk if t % 3 == 2 else _rope(chunk, c, s1, s2)

        @pl.when(j >= n_rope)
        def _():
            o_ref[...] = y

    tab = pl.BlockSpec((tm, LANES), lambda j, i: (i, 0))
    return _hosted_call(
        body, comm, (h, w_lay, *tabs), name="in_proj_fwd", grid=(LAY_N // tn, S // tm),
        in_specs=[pl.BlockSpec((tm, D), lambda j, i: (i, 0)), pl.BlockSpec((D, tn), lambda j, i: (0, j)), tab, tab, tab],
        out_specs=pl.BlockSpec((tm, tn), lambda j, i: (i, j)),
        out_shape=_sds((S, LAY_N), F32), vmem_mb=VMEM_MB)


def _log1p_small(t):
    return jnp.where(t < 1e-2, t * (1.0 - t * (0.5 - t * (1.0 / 3.0))), jnp.log(1.0 + t))


def fgate_fwd(p, b_pad):
    def body(fa_ref, b_ref, frow_ref, fraw_ref, fcol_ref):
        f = fa_ref[...] + b_ref[...]
        fr = f.T[0:8, :]
        ls = jnp.minimum(fr, 0.0) - _log1p_small(jnp.exp(-jnp.abs(fr)))
        lane = lax.broadcasted_iota(jnp.int32, (8, S), 1)
        acc, sh = ls, 1
        while sh < S:
            acc = acc + jnp.where(lane >= sh, pltpu.roll(acc, sh, 1), 0.0)
            sh *= 2
        frow_ref[...] = acc
        fraw_ref[...] = fr
        for hh in range(8):
            fcol_ref[hh] = jnp.broadcast_to(acc[hh:hh + 1, :], (LANES, S)).T

    return pl.pallas_call(
        body, name="fgate_fwd", grid=(1,),
        in_specs=[pl.BlockSpec((S, LANES), lambda i: (0, LAY_F // LANES)), pl.BlockSpec((1, LANES), lambda i: (0, 0))],
        out_specs=[pl.BlockSpec((8, S), lambda i: (0, 0)), pl.BlockSpec((8, S), lambda i: (0, 0)),
                   pl.BlockSpec((8, S, LANES), lambda i: (0, 0, 0))],
        out_shape=[_sds((8, S), F32), _sds((8, S), F32), _sds((8, S, LANES), F32)],
        compiler_params=_params(VMEM_MB),
    )(p, b_pad)


def _head_masks(rows):
    lane = lax.broadcasted_iota(jnp.int32, (rows, LANES), 1)
    return lane < HD, lane >= HD


FT = 256


def _split3(f):
    hi = f.astype(BF16).astype(F32)
    r = f - hi
    mid = r.astype(BF16).astype(F32)
    return hi, mid, r - mid


def _fox_operands(qkv_ref, tcol_ref, scol_ref, qa_s, ka_s):
    rows = 256
    lane = lax.broadcasted_iota(jnp.int32, (rows, LANES), 1)

    def chunk(i, _):
        r = pl.ds(pl.multiple_of(i * rows, rows), rows)
        q, k = qkv_ref[r, 0:LANES], qkv_ref[r, LANES:2 * LANES]
        s0, s1 = _split3(scol_ref[0, r, :]), _split3(scol_ref[1, r, :])
        ka = jnp.where(lane == 0, -s0[0], jnp.where(lane == 1, -s0[1], jnp.where(lane == 2, -s0[2], jnp.where(
            lane == 3, -s1[0], jnp.where(lane == 4, -s1[1], jnp.where(lane == 5, -s1[2], jnp.where(lane < 9, 1.0, 0.0)))))))
        ka_s[r, 0:LANES] = k.astype(BF16)
        ka_s[r, LANES:2 * LANES] = ka.astype(BF16)
        for hh in range(2):
            own = (lane < HD) if hh == 0 else (lane >= HD)
            t3 = _split3(tcol_ref[hh, r, :])
            ones = (lane >= 3 * hh) & (lane < 3 * hh + 3)
            qa = jnp.where(ones, 1.0, jnp.where(lane == 6, t3[0], jnp.where(lane == 7, t3[1], jnp.where(lane == 8, t3[2], 0.0))))
            qa_s[hh, r, 0:LANES] = jnp.where(own, q * SCALE, 0.0).astype(BF16)
            qa_s[hh, r, LANES:2 * LANES] = qa.astype(BF16)
        return 0

    lax.fori_loop(0, S // rows, chunk, 0)


def fox_fwd(p, fcol, comm=None):
    nt = (((1,), (1,)), ((), ()))

    def body(qkv_ref, fc_ref, o_ref, g_ref, qa_s, ka_s):
        _fox_operands(qkv_ref, fc_ref, fc_ref, qa_s, ka_s)
        masks = _head_masks(FT)
        causal = lax.broadcasted_iota(jnp.int32, (FT, FT), 1) <= lax.broadcasted_iota(jnp.int32, (FT, FT), 0)
        causal2 = jnp.concatenate([causal, causal], axis=0)

        def qloop(qi, _):
            q0 = pl.multiple_of(qi * FT, FT)
            qa = jnp.concatenate([qa_s[0, pl.ds(q0, FT), :], qa_s[1, pl.ds(q0, FT), :]], axis=0)

            def step(kb, carry, diagonal):
                m, l, acc = carry
                k0 = pl.multiple_of(kb * FT, FT)
                v = qkv_ref[pl.ds(k0, FT), 2 * LANES:3 * LANES].astype(BF16)
                s = lax.dot_general(qa, ka_s[pl.ds(k0, FT), :], nt, preferred_element_type=F32)
                if diagonal:
                    s = jnp.where(causal2, s, NEG)
                m_new = jnp.maximum(m, jnp.max(s, axis=1, keepdims=True))
                pr = jnp.exp(s - m_new)
                alpha = jnp.exp(m - m_new)
                return (m_new, l * alpha + jnp.sum(pr, axis=1, keepdims=True),
                        acc * alpha + jnp.dot(pr.astype(BF16), v, preferred_element_type=F32))

            init = (jnp.full((2 * FT, 1), NEG, F32), jnp.zeros((2 * FT, 1), F32), jnp.zeros((2 * FT, LANES), F32))
            carry = lax.fori_loop(0, qi, lambda kb, cr: step(kb, cr, False), init)
            m, l, acc = step(qi, carry, True)
            out = acc / l
            lse = m + jnp.log(l)
            o_ref[pl.ds(q0, FT), :] = jnp.where(masks[0], out[:FT], out[FT:]).astype(BF16)
            g_ref[0, pl.ds(q0, FT), :] = fc_ref[0, pl.ds(q0, FT), :] - lse[:FT]
            g_ref[1, pl.ds(q0, FT), :] = fc_ref[1, pl.ds(q0, FT), :] - lse[FT:]
            return 0

        lax.fori_loop(0, S // FT, qloop, 0)

    a_blk = LAY_A // 384
    return _hosted_call(
        body, comm, (p, fcol), name="fox_fwd", grid=(N_FOX_PAIRS,),
        in_specs=[pl.BlockSpec((S, 384), lambda p_: (0, a_blk + p_)), pl.BlockSpec((2, S, LANES), lambda p_: (p_, 0, 0))],
        out_specs=[pl.BlockSpec((S, LANES), lambda p_: (0, p_)), pl.BlockSpec((2, S, LANES), lambda p_: (p_, 0, 0))],
        out_shape=[_sds((S, 4 * LANES), BF16), _sds((8, S, LANES), F32)],
        scratch_shapes=[pltpu.VMEM((2, S, 2 * LANES), BF16), pltpu.VMEM((S, 2 * LANES), BF16)],
        vmem_mb=VMEM_MB)


def _dil_rows(ref, start, d):
    return ref[pl.ds(start, SPAN), :] if d == 1 else ref[pl.ds(start, SPAN, stride=d), :]


def _dil_store(ref, start, d, val):
    if d == 1:
        ref[pl.ds(start, SPAN), :] = val
    else:
        ref[pl.ds(start, SPAN, stride=d), :] = val


def _band_mask(has_prev):
    qi = lax.broadcasted_iota(jnp.int32, (SPAN, 2 * SPAN), 0) + SPAN
    kj = lax.broadcasted_iota(jnp.int32, (SPAN, 2 * SPAN), 1)
    dist = qi - kj
    return (dist >= 0) & (dist <= SPAN) & (has_prev | (kj >= SPAN))


def _dil_block(n, d, nb):
    r, j = n // nb, n % nb
    start = r + d * SPAN * j
    prev = jnp.maximum(start - d * SPAN, r)
    return start, prev, j > 0


def dil_fwd(p, comm=None):
    def body(*refs):
        qkv = [refs[3 * g:3 * g + 3] for g in range(3)]
        y_ref, lse_ref = refs[9], refs[10]
        acc_s, m_s, l_s = refs[11], refs[12], refs[13]
        masks = _head_masks(SPAN)
        for g, (d, nb) in enumerate(DIL_GROUPS):
            q_ref, k_ref, v_ref = qkv[g]

            def blk(n, _):
                start, prev, has_prev = _dil_block(n, d, nb)
                q = _dil_rows(q_ref, start, d)
                kc = jnp.concatenate([_dil_rows(k_ref, prev, d), _dil_rows(k_ref, start, d)], axis=0).astype(BF16)
                vc = jnp.concatenate([_dil_rows(v_ref, prev, d), _dil_rows(v_ref, start, d)], axis=0).astype(BF16)
                valid = _band_mask(has_prev)
                valid2 = jnp.concatenate([valid, valid], axis=0)
                q2 = (jnp.concatenate([jnp.where(masks[0], q, 0.0), jnp.where(masks[1], q, 0.0)], axis=0) * SCALE).astype(BF16)
                s = jnp.where(valid2, lax.dot_general(q2, kc, (((1,), (1,)), ((), ())), preferred_element_type=F32), NEG)
                m = jnp.max(s, axis=1, keepdims=True)
                pr = jnp.exp(s - m)
                l = jnp.sum(pr, axis=1, keepdims=True)
                acc = jnp.dot(pr.astype(BF16), vc, preferred_element_type=F32)
                _dil_store(acc_s.at[g], start, d, jnp.where(masks[0], acc[:SPAN], acc[SPAN:]))
                _dil_store(m_s.at[g], start, d, jnp.where(masks[0], m[:SPAN], m[SPAN:]))
                _dil_store(l_s.at[g], start, d, jnp.where(masks[0], l[:SPAN], l[SPAN:]))
                return 0

            lax.fori_loop(0, 16, blk, 0)

        def merge(i, _):
            rows = pl.ds(pl.multiple_of(i * 256, 256), 256)
            m = [m_s[g, rows, :] for g in range(3)]
            mx = jnp.maximum(jnp.maximum(m[0], m[1]), m[2])
            w = [jnp.exp(m[g] - mx) for g in range(3)]
            l = sum(l_s[g, rows, :] * w[g] for g in range(3))
            y_ref[rows, :] = sum(acc_s[g, rows, :] * w[g] for g in range(3)) / l
            lse_ref[rows, :] = mx + jnp.log(l)
            return 0

        lax.fori_loop(0, S // 256, merge, 0)

    def spec(g, t):
        return pl.BlockSpec((S, LANES), lambda p_: (0, (p_ * 3 + g) * 3 + t))

    return _hosted_call(
        body, comm, [p] * 9, name="dil_fwd", grid=(N_DIL_PAIRS,),
        in_specs=[spec(g, t) for g in range(3) for t in range(3)],
        out_specs=[pl.BlockSpec((S, LANES), lambda p_: (0, p_)), pl.BlockSpec((S, LANES), lambda p_: (0, p_))],
        out_shape=[_sds((S, 2 * LANES), F32), _sds((S, 2 * LANES), F32)],
        scratch_shapes=[pltpu.VMEM((3, S, LANES), F32)] * 3,
        vmem_mb=VMEM_MB)


def merge_fwd(ya_att, yb, p, w_bra, w_brb, comm=None):
    tm = 256
    gblk = LAY_G // D

    def body(a_ref, b_ref, ga_ref, gb_ref, wa_ref, wb_ref, mg_ref, ya_ref, yb_ref):
        ya = jnp.dot(a_ref[...], wa_ref[...], preferred_element_type=F32)
        ybp = jnp.dot(b_ref[...].astype(BF16), wb_ref[...], preferred_element_type=F32)
        mg_ref[...] = (_sigmoid(ga_ref[...]) * ya + _sigmoid(gb_ref[...]) * ybp).astype(BF16)
        ya_ref[...] = ya
        yb_ref[...] = ybp

    row = lambda w: pl.BlockSpec((tm, w), lambda i: (i, 0))
    return _hosted_call(
        body, comm, (ya_att, yb, p, p, w_bra, w_brb), name="merge_fwd", grid=(S // tm,),
        in_specs=[row(512), row(256), pl.BlockSpec((tm, D), lambda i: (i, gblk)), pl.BlockSpec((tm, D), lambda i: (i, gblk + 1)),
                  pl.BlockSpec((512, D), lambda i: (0, 0)), pl.BlockSpec((256, D), lambda i: (0, 0))],
        out_specs=[row(D), row(D), row(D)],
        out_shape=[_sds((S, D), BF16), _sds((S, D), F32), _sds((S, D), F32)])


def out_proj_fwd(merged, w_out, x, mod, g_ffn):
    tm = 256

    def body(a_ref, w_ref, x_ref, mod_ref, g_ref, mix_ref, x1_ref, h2_ref):
        mix = jnp.dot(a_ref[...], w_ref[...], preferred_element_type=F32)
        x1 = x_ref[...] + mod_ref[2:3, :] * mix
        r = lax.rsqrt(jnp.mean(x1 * x1, axis=1, keepdims=True) + EPS)
        h2 = (x1 * r * g_ref[...]) * (1.0 + mod_ref[4:5, :]) + mod_ref[3:4, :]
        mix_ref[...] = mix
        x1_ref[...] = x1
        h2_ref[...] = h2.astype(BF16)

    row = pl.BlockSpec((tm, D), lambda i: (i, 0))
    return pl.pallas_call(
        body, name="out_proj_fwd", grid=(S // tm,),
        in_specs=[row, pl.BlockSpec((D, D), lambda i: (0, 0)), row, pl.BlockSpec((8, D), lambda i: (0, 0)),
                  pl.BlockSpec((1, D), lambda i: (0, 0))],
        out_specs=[row, row, row],
        out_shape=[_sds((S, D), F32), _sds((S, D), F32), _sds((S, D), BF16)],
    )(merged, w_out, x, mod, g_ffn)


def ffn_up_fwd(h2, w_gate, w_up):
    tm = 1024
    nt = (((1,), (1,)), ((), ()))

    def body(h_ref, wg_ref, wu_ref, a_ref, u_ref, z_ref):
        h = h_ref[...]
        a = lax.dot_general(h, wg_ref[...], nt, preferred_element_type=F32)
        u = lax.dot_general(h, wu_ref[...], nt, preferred_element_type=F32)
        a_ref[...] = a
        u_ref[...] = u
        z_ref[...] = (a * _sigmoid(a) * u).astype(BF16)

    out = pl.BlockSpec((tm, FF_PAD), lambda k, i: (i, k))
    return pl.pallas_call(
        body, name="ffn_up_fwd", grid=(4, S // tm),
        in_specs=[pl.BlockSpec((tm, D), lambda k, i: (i, 0)), pl.BlockSpec((None, FF_PAD, D), lambda k, i: (k, 0, 0)),
                  pl.BlockSpec((None, FF_PAD, D), lambda k, i: (k, 0, 0))],
        out_specs=[out, out, out],
        out_shape=[_sds((S, FFP), F32), _sds((S, FFP), F32), _sds((S, FFP), BF16)], compiler_params=_params(VMEM_MB),
    )(h2, w_gate, w_up)


def ffn_down_loss(z, w_down, x1, mod, g_final, tgt):
    tm = 256

    def body(z_ref, w_ref, x1_ref, mod_ref, g_ref, t_ref, dx2_ref, dffn_ref, dg_ref, dga_ref, loss_ref, s_dg, s_dga, s_loss):
        i = pl.program_id(0)

        @pl.when(i == 0)
        def _():
            s_dg[...] = jnp.zeros_like(s_dg)
            s_dga[...] = jnp.zeros_like(s_dga)
            s_loss[...] = jnp.zeros_like(s_loss)

        ffn = jnp.dot(z_ref[...], w_ref[...], preferred_element_type=F32)
        gaf = mod_ref[5:6, :]
        x2 = x1_ref[...] + gaf * ffn
        r = lax.rsqrt(jnp.mean(x2 * x2, axis=1, keepdims=True) + EPS)
        xh = x2 * r
        g = g_ref[...]
        e = xh * g - t_ref[...]
        s_loss[...] += 0.5 * jnp.sum(jnp.mean(e * e, axis=1, keepdims=True), axis=0, keepdims=True)
        dy = e * (1.0 / D)
        gdy = dy * g
        dx2 = r * (gdy - xh * jnp.mean(gdy * xh, axis=1, keepdims=True))
        s_dg[...] += _colsum8(dy * xh)
        s_dga[...] += _colsum8(dx2 * ffn)
        dx2_ref[...] = dx2
        dffn_ref[...] = (dx2 * gaf).astype(BF16)

        @pl.when(i == pl.num_programs(0) - 1)
        def _():
            dg_ref[...] = jnp.sum(s_dg[...], axis=0, keepdims=True)
            dga_ref[...] = jnp.sum(s_dga[...], axis=0, keepdims=True)
            loss_ref[...] = jnp.broadcast_to(s_loss[...], (1, LANES))

    row = pl.BlockSpec((tm, D), lambda i: (i, 0))
    vec = pl.BlockSpec((1, D), lambda i: (0, 0))
    return pl.pallas_call(
        body, name="ffn_down_loss", grid=(S // tm,),
        in_specs=[pl.BlockSpec((tm, FFP), lambda i: (i, 0)), pl.BlockSpec((FFP, D), lambda i: (0, 0)), row,
                  pl.BlockSpec((8, D), lambda i: (0, 0)), vec, row],
        out_specs=[row, row, vec, vec, pl.BlockSpec((1, LANES), lambda i: (0, 0))],
        out_shape=[_sds((S, D), F32), _sds((S, D), BF16), _sds((1, D), F32), _sds((1, D), F32), _sds((1, LANES), F32)],
        scratch_shapes=[pltpu.VMEM((8, D), F32), pltpu.VMEM((8, D), F32), pltpu.VMEM((1, 1), F32)],
        compiler_params=_params(VMEM_MB),
    )(z, w_down, x1, mod, g_final, tgt)


def ffn_down_bwd(dffn, w_down, a, u, z):
    tm, tn = 1024, 768

    def body(d_ref, w_ref, a_ref, u_ref, z_ref, da_ref, du_ref, dw_ref):
        i = pl.program_id(1)
        dff = d_ref[...]
        dz = lax.dot_general(dff, w_ref[...], (((1,), (1,)), ((), ())), preferred_element_type=F32)
        av, uv = a_ref[...], u_ref[...]
        sg = _sigmoid(av)
        du_ref[...] = (dz * (av * sg)).astype(BF16)
        da_ref[...] = (dz * uv * (sg * (1.0 + av * (1.0 - sg)))).astype(BF16)
        dw = lax.dot_general(z_ref[...], dff, (((0,), (0,)), ((), ())), preferred_element_type=F32)

        @pl.when(i == 0)
        def _():
            dw_ref[...] = dw

        @pl.when(i > 0)
        def _():
            dw_ref[...] += dw

    tile = pl.BlockSpec((tm, tn), lambda j, i: (i, j))
    return pl.pallas_call(
        body, name="ffn_down_bwd", grid=(FFP // tn, S // tm),
        in_specs=[pl.BlockSpec((tm, D), lambda j, i: (i, 0)), pl.BlockSpec((tn, D), lambda j, i: (j, 0)), tile, tile, tile],
        out_specs=[tile, tile, pl.BlockSpec((tn, D), lambda j, i: (j, 0))],
        out_shape=[_sds((S, FFP), BF16), _sds((S, FFP), BF16), _sds((FFP, D), F32)], compiler_params=_params(VMEM_MB),
    )(dffn, w_down, a, u, z)


def mm_nt(dy, w, name, comm=None):
    tm = 1024
    n = dy.shape[1]
    if w.ndim == 2:
        k_in, tk = w.shape[0], 768
        w_spec = pl.BlockSpec((k_in, tk), lambda i, k: (0, k))
        dims = (((1,), (1,)), ((), ()))
    else:
        k_in, tk = w.shape[2], FF_PAD
        w_spec = pl.BlockSpec((None, tk, k_in), lambda i, k: (k, 0, 0))
        dims = (((1,), (0,)), ((), ()))
    nk = n // tk

    def body(d_ref, w_ref, o_ref, acc):
        k = pl.program_id(1)
        part = lax.dot_general(d_ref[...], w_ref[...], dims, preferred_element_type=F32)

        @pl.when(k == 0)
        def _():
            acc[...] = part

        @pl.when(k > 0)
        def _():
            acc[...] += part

        @pl.when(k == nk - 1)
        def _():
            o_ref[...] = acc[...]

    return _hosted_call(
        body, comm, (dy, w), name=name, grid=(S // tm, nk),
        in_specs=[pl.BlockSpec((tm, tk), lambda i, k: (i, k)), w_spec],
        out_specs=pl.BlockSpec((tm, k_in), lambda i, k: (i, 0)),
        out_shape=_sds((S, k_in), F32),
        scratch_shapes=[pltpu.VMEM((tm, k_in), F32)], vmem_mb=VMEM_MB)


def mm_tn(h, dy, name, shard_major=False, comm=None):
    tm, tn = 2048, 768
    k_in, n = h.shape[1], dy.shape[1]

    def body(h_ref, d_ref, o_ref):
        i = pl.program_id(1)
        ops = (d_ref[...], h_ref[...]) if shard_major else (h_ref[...], d_ref[...])
        dw = lax.dot_general(*ops, (((0,), (0,)), ((), ())), preferred_element_type=F32)

        @pl.when(i == 0)
        def _():
            o_ref[...] = dw

        @pl.when(i > 0)
        def _():
            o_ref[...] += dw

    if shard_major:
        out_spec, out_shape = pl.BlockSpec((None, tn, k_in), lambda j, i: (j, 0, 0)), _sds((n // tn, tn, k_in), F32)
    else:
        out_spec, out_shape = pl.BlockSpec((k_in, tn), lambda j, i: (0, j)), _sds((k_in, n), F32)
    return _hosted_call(
        body, comm, (h, dy), name=name, grid=(n // tn, S // tm),
        in_specs=[pl.BlockSpec((tm, k_in), lambda j, i: (i, 0)), pl.BlockSpec((tm, tn), lambda j, i: (i, j))],
        out_specs=out_spec, out_shape=out_shape, vmem_mb=VMEM_MB)


def mid_bwd(dh2a, dh2b, x1, dx2, mix, mod, g_ffn, p, ya, ybp, merged, ya_att, yb, w_out, w_bra, w_brb, comm=None):
    tm = 256
    gblk = LAY_G // D
    nsteps = S // tm

    def body(dha_ref, dhb_ref, x1_ref, dx2_ref, mix_ref, mod_ref, g_ref, ga_ref, gb_ref, ya_ref, yb_ref, mg_ref,
             att_ref, ybb_ref, wo_ref, wa_ref, wb_ref,
             dx1_ref, dpg_ref, datt_ref, dyb_ref, cs_ref, dwo_ref, dwa_ref, dwb_ref, s_cs):
        i = pl.program_id(0)

        @pl.when(i == 0)
        def _():
            s_cs[...] = jnp.zeros_like(s_cs)
            dwo_ref[...] = jnp.zeros_like(dwo_ref)
            dwa_ref[...] = jnp.zeros_like(dwa_ref)
            dwb_ref[...] = jnp.zeros_like(dwb_ref)

        x1 = x1_ref[...]
        g = g_ref[...]
        r = lax.rsqrt(jnp.mean(x1 * x1, axis=1, keepdims=True) + EPS)
        xh = x1 * r
        dh2 = dha_ref[...] + dhb_ref[...]
        s_cs[0] += _colsum8(dh2)
        s_cs[1] += _colsum8(dh2 * (xh * g))
        dn2 = dh2 * (1.0 + mod_ref[4:5, :])
        s_cs[2] += _colsum8(dn2 * xh)
        gd = dn2 * g
        dx1 = dx2_ref[...] + r * (gd - xh * jnp.mean(gd * xh, axis=1, keepdims=True))
        s_cs[3] += _colsum8(dx1 * mix_ref[...])
        dx1_ref[...] = dx1
        dmix = (dx1 * mod_ref[2:3, :]).astype(BF16)
        dmg = lax.dot_general(dmix, wo_ref[...], (((1,), (1,)), ((), ())), preferred_element_type=F32)
        sga, sgb = _sigmoid(ga_ref[...]), _sigmoid(gb_ref[...])
        dya = (dmg * sga).astype(BF16)
        dybp = (dmg * sgb).astype(BF16)
        dpg_ref[:, 0:D] = (dmg * ya_ref[...] * (sga * (1.0 - sga))).astype(BF16)
        dpg_ref[:, D:2 * D] = (dmg * yb_ref[...] * (sgb * (1.0 - sgb))).astype(BF16)
        datt_ref[...] = lax.dot_general(dya, wa_ref[...], (((1,), (1,)), ((), ())), preferred_element_type=F32).astype(BF16)
        dyb_ref[...] = lax.dot_general(dybp, wb_ref[...], (((1,), (1,)), ((), ())), preferred_element_type=F32)
        tn_dims = (((0,), (0,)), ((), ()))
        dwo_ref[...] += lax.dot_general(mg_ref[...], dmix, tn_dims, preferred_element_type=F32)
        dwa_ref[...] += lax.dot_general(att_ref[...], dya, tn_dims, preferred_element_type=F32)
        dwb_ref[...] += lax.dot_general(ybb_ref[...].astype(BF16), dybp, tn_dims, preferred_element_type=F32)

        @pl.when(i == nsteps - 1)
        def _():
            for t in range(4):
                cs_ref[t:t + 1, :] = jnp.sum(s_cs[t], axis=0, keepdims=True)
            cs_ref[4:8, :] = jnp.zeros((4, D), F32)

    row = lambda w: pl.BlockSpec((tm, w), lambda i: (i, 0))
    full = lambda a, b: pl.BlockSpec((a, b), lambda i: (0, 0))
    return _hosted_call(
        body, comm, (dh2a, dh2b, x1, dx2, mix, mod, g_ffn, p, p, ya, ybp, merged, ya_att, yb, w_out, w_bra, w_brb),
        name="mid_bwd", grid=(nsteps,),
        in_specs=[row(D), row(D), row(D), row(D), row(D), full(8, D), full(1, D),
                  pl.BlockSpec((tm, D), lambda i: (i, gblk)), pl.BlockSpec((tm, D), lambda i: (i, gblk + 1)),
                  row(D), row(D), row(D), row(512), row(256), full(D, D), full(512, D), full(256, D)],
        out_specs=[row(D), pl.BlockSpec((tm, 2 * D), lambda i: (i, LAY_G // (2 * D))), row(512), row(256), full(8, D),
                   full(D, D), full(512, D), full(256, D)],
        out_shape=[_sds((S, D), F32), _sds((S, LAY_N), BF16), _sds((S, 512), BF16), _sds((S, 256), F32), _sds((8, D), F32),
                   _sds((D, D), F32), _sds((512, D), F32), _sds((256, D), F32)],
        scratch_shapes=[pltpu.VMEM((4, 8, D), F32)],
        vmem_mb=VMEM_MB)


def fox_bwd(p, do, o, gcol, fcol, dp, comm=None):
    nq = S // FT
    nt = (((1,), (1,)), ((), ()))
    tn = (((0,), (0,)), ((), ()))

    def body(qkv_ref, do_ref, o_ref, g_ref, fc_ref, dp_in, dp_ref, df_ref, rs_ref, dq_s, qa_s, ka_s, dob_s, dl_s):
        del dp_in
        _fox_operands(qkv_ref, g_ref, fc_ref, qa_s, ka_s)
        masks = _head_masks(FT)
        lane = lax.broadcasted_iota(jnp.int32, (FT, LANES), 1)
        head0 = 2 * pl.program_id(0)
        causal = lax.broadcasted_iota(jnp.int32, (FT, FT), 1) <= lax.broadcasted_iota(jnp.int32, (FT, FT), 0)
        dq_s[...] = jnp.zeros_like(dq_s)
        rs_ref[...] = jnp.zeros_like(rs_ref)

        causal2 = jnp.concatenate([causal, causal], axis=0)

        def prep(i, _):
            r = pl.ds(pl.multiple_of(i * 256, 256), 256)
            m256 = _head_masks(256)
            dov, ov = do_ref[r, :].astype(F32), o_ref[r, :].astype(F32)
            for hh in range(2):
                dom = jnp.where(m256[hh], dov, 0.0)
                dob_s[hh, r, :] = dom.astype(BF16)
                dl_s[hh, r, :] = jnp.broadcast_to(jnp.sum(dom * ov, axis=1, keepdims=True), (256, LANES))
            return 0

        lax.fori_loop(0, S // 256, prep, 0)

        def stack(ref, q0, cols=slice(None)):
            return jnp.concatenate([ref[0, pl.ds(q0, FT), cols], ref[1, pl.ds(q0, FT), cols]], axis=0)

        def kloop(kb, _):
            k0 = pl.multiple_of(kb * FT, FT)
            k = qkv_ref[pl.ds(k0, FT), LANES:2 * LANES].astype(BF16)
            v = qkv_ref[pl.ds(k0, FT), 2 * LANES:3 * LANES].astype(BF16)
            ka = ka_s[pl.ds(k0, FT), :]

            def step(qi, carry, diagonal):
                dk, dv, df0, df1 = carry
                q0 = pl.multiple_of(qi * FT, FT)
                qa, dob = stack(qa_s, q0), stack(dob_s, q0)
                s = lax.dot_general(qa, ka, nt, preferred_element_type=F32)
                pr = jnp.exp(jnp.where(causal2, s, NEG)) if diagonal else jnp.exp(s)
                dpr = lax.dot_general(dob, v, nt, preferred_element_type=F32)
                ds = pr * (dpr - jnp.tile(stack(dl_s, q0), (1, FT // LANES)))
                dsb = ds.astype(BF16)
                dq = jnp.dot(dsb, k, preferred_element_type=F32) * SCALE
                dk = dk + lax.dot_general(dsb, qa[:, 0:LANES], tn, preferred_element_type=F32)
                dv = dv + lax.dot_general(pr.astype(BF16), dob, tn, preferred_element_type=F32)
                rsum = jnp.sum(ds, axis=1, keepdims=True)
                dq_s[pl.ds(q0, FT), :] += jnp.where(masks[0], dq[:FT], dq[FT:])
                rs_ref[pl.ds(q0, FT), :] += jnp.where(lane == head0, rsum[:FT], 0.0) + jnp.where(lane == head0 + 1, rsum[FT:], 0.0)
                return (dk, dv, df0 - jnp.sum(ds[:FT], axis=0, keepdims=True), df1 - jnp.sum(ds[FT:], axis=0, keepdims=True))

            z = jnp.zeros((FT, LANES), F32)
            z1 = jnp.zeros((1, FT), F32)
            carry = step(kb, (z, z, z1, z1), True)
            dk, dv, df0, df1 = lax.fori_loop(kb + 1, nq, lambda qi, cr: step(qi, cr, False), carry)
            dp_ref[pl.ds(k0, FT), LANES:2 * LANES] = dk.astype(BF16)
            dp_ref[pl.ds(k0, FT), 2 * LANES:3 * LANES] = dv.astype(BF16)
            df_ref[0:1, pl.ds(k0, FT)] = df0
            df_ref[1:2, pl.ds(k0, FT)] = df1
            return 0

        lax.fori_loop(0, S // FT, kloop, 0)
        dp_ref[:, 0:LANES] = dq_s[...].astype(BF16)

    a_blk = LAY_A // 384
    pair = pl.BlockSpec((S, LANES), lambda p_: (0, p_))
    heads = pl.BlockSpec((2, S, LANES), lambda p_: (p_, 0, 0))
    return _hosted_call(
        body, comm, (p, do, o, gcol, fcol, dp), name="fox_bwd", grid=(N_FOX_PAIRS,),
        in_specs=[pl.BlockSpec((S, 384), lambda p_: (0, a_blk + p_)), pair, pair, heads, heads, pl.BlockSpec(memory_space=pl.ANY)],
        out_specs=[pl.BlockSpec((S, 384), lambda p_: (0, a_blk + p_)), pl.BlockSpec((None, 2, S), lambda p_: (p_, 0, 0)),
                   pl.BlockSpec((None, S, LANES), lambda p_: (p_, 0, 0))],
        out_shape=[_sds((S, LAY_N), BF16), _sds((4, 2, S), F32), _sds((4, S, LANES), F32)],
        scratch_shapes=[pltpu.VMEM((S, LANES), F32), pltpu.VMEM((2, S, 2 * LANES), BF16), pltpu.VMEM((S, 2 * LANES), BF16),
                        pltpu.VMEM((2, S, LANES), BF16), pltpu.VMEM((2, S, LANES), F32)],
        aliases={5: 0}, vmem_mb=VMEM_MB)


def fgate_bwd(dfrow, dfcol, fraw, dp):
    def body(df_ref, dc_ref, f_ref, dp_in, dpf_ref, db_ref):
        del dp_in
        lane = lax.broadcasted_iota(jnp.int32, (8, S), 1)
        rsum = (dc_ref[0] + dc_ref[1]) + (dc_ref[2] + dc_ref[3])
        acc, sh = df_ref[...] + rsum.T[0:8, :], 1
        while sh < S:
            acc = acc + jnp.where(lane < S - sh, pltpu.roll(acc, S - sh, 1), 0.0)
            sh *= 2
        df = acc * _sigmoid(-f_ref[...])
        db_ref[...] = jnp.broadcast_to(jnp.sum(df, axis=1, keepdims=True), (8, LANES))
        dfc = jnp.concatenate([df, jnp.zeros((LANES - 8, S), F32)], axis=0).T
        dpf_ref[:, 0:LANES] = dfc.astype(BF16)
        dpf_ref[:, LANES:2 * LANES] = jnp.zeros((S, LANES), BF16)

    return pl.pallas_call(
        body, name="fgate_bwd", grid=(1,),
        in_specs=[pl.BlockSpec((8, S), lambda i: (0, 0)), pl.BlockSpec((4, S, LANES), lambda i: (0, 0, 0)),
                  pl.BlockSpec((8, S), lambda i: (0, 0)), pl.BlockSpec(memory_space=pl.ANY)],
        out_specs=[pl.BlockSpec((S, 2 * LANES), lambda i: (0, LAY_F // (2 * LANES))), pl.BlockSpec((8, LANES), lambda i: (0, 0))],
        out_shape=[_sds((S, LAY_N), BF16), _sds((8, LANES), F32)],
        input_output_aliases={3: 0},
        compiler_params=_params(VMEM_MB),
    )(dfrow, dfcol, fraw, dp)


def dil_bwd(p, dyb, yb, lse, tabs, dp, comm=None):
    def body(*refs):
        qkv = [refs[3 * g:3 * g + 3] for g in range(3)]
        dy_ref, y_ref, lse_ref, c_ref, s1_ref, s2_ref = refs[9:15]
        dp_ref = refs[16]
        dq_s, dk_s, dv_s, dl_s = refs[17:21]
        masks = _head_masks(SPAN)
        m256 = _head_masks(256)
        nt = (((1,), (1,)), ((), ()))
        tn = (((0,), (0,)), ((), ()))
        dk_s[...] = jnp.zeros_like(dk_s)
        dv_s[...] = jnp.zeros_like(dv_s)

        def prep(i, _):
            rows = pl.ds(pl.multiple_of(i * 256, 256), 256)
            pr = dy_ref[rows, :] * y_ref[rows, :]
            d0 = jnp.sum(jnp.where(m256[0], pr, 0.0), axis=1, keepdims=True)
            d1 = jnp.sum(jnp.where(m256[1], pr, 0.0), axis=1, keepdims=True)
            dl_s[rows, :] = jnp.where(m256[0], d0, d1)
            return 0

        lax.fori_loop(0, S // 256, prep, 0)

        for g, (d, nb) in enumerate(DIL_GROUPS):
            q_ref, k_ref, v_ref = qkv[g]

            def blk(n, _):
                start, prev, has_prev = _dil_block(n, d, nb)
                q = _dil_rows(q_ref, start, d)
                kc = jnp.concatenate([_dil_rows(k_ref, prev, d), _dil_rows(k_ref, start, d)], axis=0).astype(BF16)
                vc = jnp.concatenate([_dil_rows(v_ref, prev, d), _dil_rows(v_ref, start, d)], axis=0).astype(BF16)
                dov = _dil_rows(dy_ref, start, d)
                lsev = _dil_rows(lse_ref, start, d)
                dlv = _dil_rows(dl_s, start, d)
                valid = _band_mask(has_prev)
                valid2 = jnp.concatenate([valid, valid], axis=0)

                def stack(t):
                    return jnp.concatenate([jnp.where(masks[0], t, 0.0), jnp.where(masks[1], t, 0.0)], axis=0)

                def column(t):
                    return jnp.concatenate([jnp.max(jnp.where(masks[hh], t, NEG), axis=1, keepdims=True) for hh in range(2)], axis=0)

                q2 = (stack(q) * SCALE).astype(BF16)
                dob = stack(dov).astype(BF16)
                s = jnp.where(valid2, lax.dot_general(q2, kc, nt, preferred_element_type=F32), NEG)
                pr = jnp.exp(s - column(lsev))
                dpr = lax.dot_general(dob, vc, nt, preferred_element_type=F32)
                dsb = (pr * (dpr - column(dlv))).astype(BF16)
                dq = jnp.dot(dsb, kc, preferred_element_type=F32) * SCALE
                dkc = lax.dot_general(dsb, q2, tn, preferred_element_type=F32)
                dvc = lax.dot_general(pr.astype(BF16), dob, tn, preferred_element_type=F32)
                _dil_store(dq_s.at[g], start, d, jnp.where(masks[0], dq[:SPAN], dq[SPAN:]))
                for ref, val in ((dk_s.at[g], dkc), (dv_s.at[g], dvc)):
                    _dil_store(ref, prev, d, _dil_rows(ref, prev, d) + jnp.where(has_prev, val[0:SPAN], 0.0))
                    _dil_store(ref, start, d, _dil_rows(ref, start, d) + val[SPAN:])
                return 0

            lax.fori_loop(0, 16, blk, 0)

        def fin(i, _):
            rows = pl.ds(pl.multiple_of(i * 256, 256), 256)
            c, s1, s2 = c_ref[rows, :], s1_ref[rows, :], s2_ref[rows, :]
            for g in range(3):
                base = g * 384
                dp_ref[rows, base:base + LANES] = _rope_bwd(dq_s[g, rows, :], c, s1, s2).astype(BF16)
                dp_ref[rows, base + LANES:base + 2 * LANES] = _rope_bwd(dk_s[g, rows, :], c, s1, s2).astype(BF16)
                dp_ref[rows, base + 2 * LANES:base + 3 * LANES] = dv_s[g, rows, :].astype(BF16)
            return 0

        lax.fori_loop(0, S // 256, fin, 0)

    def spec(g, t):
        return pl.BlockSpec((S, LANES), lambda p_: (0, (p_ * 3 + g) * 3 + t))

    pair = pl.BlockSpec((S, LANES), lambda p_: (0, p_))
    tab = pl.BlockSpec((S, LANES), lambda p_: (0, 0))
    return _hosted_call(
        body, comm, [p] * 9 + [dyb, yb, lse, *tabs, dp], name="dil_bwd", grid=(N_DIL_PAIRS,),
        in_specs=[spec(g, t) for g in range(3) for t in range(3)] + [pair, pair, pair, tab, tab, tab, pl.BlockSpec(memory_space=pl.ANY)],
        out_specs=pl.BlockSpec((S, 1152), lambda p_: (0, p_)),
        out_shape=_sds((S, LAY_N), BF16),
        scratch_shapes=[pltpu.VMEM((3, S, LANES), F32)] * 3 + [pltpu.VMEM((S, LANES), F32)],
        aliases={15: 0}, vmem_mb=VMEM_MB)


def in_bwd_tail(dh1, x, dx1, mod, g_mix, comm=None):
    tm = 256
    nsteps = S // tm

    def body(dh_ref, x_ref, dx1_ref, mod_ref, g_ref, dx_ref, cs_ref, s_cs):
        i = pl.program_id(0)

        @pl.when(i == 0)
        def _():
            s_cs[...] = jnp.zeros_like(s_cs)

        xv, g, dh = x_ref[...], g_ref[...], dh_ref[...]
        r = lax.rsqrt(jnp.mean(xv * xv, axis=1, keepdims=True) + EPS)
        xh = xv * r
        s_cs[0] += _colsum8(dh)
        s_cs[1] += _colsum8(dh * (xh * g))
        dn = dh * (1.0 + mod_ref[1:2, :])
        s_cs[2] += _colsum8(dn * xh)
        gd = dn * g
        dx_ref[...] = dx1_ref[...] + r * (gd - xh * jnp.mean(gd * xh, axis=1, keepdims=True))

        @pl.when(i == nsteps - 1)
        def _():
            for t in range(3):
                cs_ref[t:t + 1, :] = jnp.sum(s_cs[t], axis=0, keepdims=True)
            cs_ref[3:8, :] = jnp.zeros((5, D), F32)

    row = pl.BlockSpec((tm, D), lambda i: (i, 0))
    return _hosted_call(
        body, comm, (dh1, x, dx1, mod, g_mix), name="in_bwd_tail", grid=(nsteps,),
        in_specs=[row, row, row, pl.BlockSpec((8, D), lambda i: (0, 0)), pl.BlockSpec((1, D), lambda i: (0, 0))],
        out_specs=[row, pl.BlockSpec((8, D), lambda i: (0, 0))],
        out_shape=[_sds((S, D), F32), _sds((8, D), F32)],
        scratch_shapes=[pltpu.VMEM((3, 8, D), F32)])


def _lay_pieces():
    out = []
    qa, ka, va, fa, qb, kb, vb, ga = 0, 512, 1024, 1536, 1544, 2312, 3080, 3848
    for p in range(N_DIL_PAIRS):
        for g in range(3):
            base = LAY_B + (p * 3 + g) * 384
            hd0 = (4 * g + 2 * p) * HD
            out += [(base, qb + hd0, LANES), (base + LANES, kb + hd0, LANES), (base + 2 * LANES, vb + hd0, LANES)]
    for p in range(N_FOX_PAIRS):
        base = LAY_A + p * 384
        out += [(base, qa + p * LANES, LANES), (base + LANES, ka + p * LANES, LANES), (base + 2 * LANES, va + p * LANES, LANES)]
    out.append((LAY_F, fa, 8))
    out.append((LAY_G, ga, 2 * D))
    return out


def lay_from_nat(w_nat):
    parts, pos = [], 0
    for lay, nat, width in sorted(_lay_pieces()):
        if lay > pos:
            parts.append(jnp.zeros((w_nat.shape[0], lay - pos), w_nat.dtype))
        parts.append(w_nat[:, nat:nat + width])
        pos = lay + width
    if pos < LAY_N:
        parts.append(jnp.zeros((w_nat.shape[0], LAY_N - pos), w_nat.dtype))
    return jnp.concatenate(parts, axis=1)


def nat_from_lay(w_lay):
    parts = [w_lay[:, lay:lay + width] for lay, nat, width in sorted(_lay_pieces(), key=lambda t: t[1])]
    return jnp.concatenate(parts, axis=1)


def _shard_runs():
    runs = []
    for lay, nat, width in _lay_pieces():
        while width:
            k, loc = nat // IN_SHARD, nat % IN_SHARD
            w = min(width, IN_SHARD - loc)
            runs.append((lay, k, loc, w))
            lay, nat, width = lay + w, nat + w, width - w
    return runs


def lay_from_shards(g):
    tm = 256

    def body(g_ref, o_ref):
        o_ref[:, LAY_F:LAY_G] = jnp.zeros((tm, LAY_G - LAY_F), g.dtype)
        for lay, k, loc, w in _shard_runs():
            o_ref[:, lay:lay + w] = g_ref[k, :, loc:loc + w]

    return pl.pallas_call(
        body, name="lay_from_shards", grid=(D // tm,),
        in_specs=[pl.BlockSpec((4, tm, IN_SHARD_PAD), lambda i: (0, i, 0))],
        out_specs=pl.BlockSpec((tm, LAY_N), lambda i: (i, 0)),
        out_shape=_sds((D, LAY_N), g.dtype), compiler_params=_params(VMEM_MB),
    )(g)


def shards_from_lay(dw_lay):
    tm = 256

    def body(x_ref, o_ref):
        o_ref[:, :, IN_SHARD:] = jnp.zeros((4, tm, IN_SHARD_PAD - IN_SHARD), F32)
        for lay, k, loc, w in _shard_runs():
            o_ref[k, :, loc:loc + w] = x_ref[:, lay:lay + w]

    return pl.pallas_call(
        body, name="shards_from_lay", grid=(D // tm,),
        in_specs=[pl.BlockSpec((tm, LAY_N), lambda i: (i, 0))],
        out_specs=pl.BlockSpec((4, tm, IN_SHARD_PAD), lambda i: (0, i, 0)),
        out_shape=_sds((4, D, IN_SHARD_PAD), F32), compiler_params=_params(VMEM_MB),
    )(dw_lay)


def _pos():
    return lax.axis_index("x"), lax.axis_index("y"), lax.axis_index("c")


def _other_chips(x, y):
    return [(1 - x, y), (x, 1 - y), (1 - x, 1 - y)]


def _remote(src, dst, send_sem, recv_sem, dev):
    return pltpu.make_async_remote_copy(src_ref=src, dst_ref=dst, send_sem=send_sem, recv_sem=recv_sem,
                                        device_id=dev, device_id_type=MESH)


VMEM_SPEC = pl.BlockSpec(memory_space=pltpu.VMEM)
ANY_SPEC = pl.BlockSpec(memory_space=pl.ANY)


def gather_all(v, name, with_sum):
    r = v.shape[0]

    def body(v_ref, out_ref, *rest):
        send_s, recv_s = rest[-2:]
        x, y, c = _pos()
        me = 4 * x + 2 * y + c
        out_ref[me] = v_ref[...]
        peers = []
        for m in range(1, 8):
            px = 1 - x if m & 4 else x
            py = 1 - y if m & 2 else y
            pc = 1 - c if m & 1 else c
            peers.append((px, py, pc))
        copies = [_remote(v_ref, out_ref.at[me], send_s.at[i], recv_s.at[i], dev) for i, dev in enumerate(peers)]
        for cp in copies:
            cp.start()
        for i, (px, py, pc) in enumerate(peers):
            _remote(v_ref, out_ref.at[4 * px + 2 * py + pc], send_s.at[i], recv_s.at[i], (px, py, pc)).wait_recv()
        for cp in copies:
            cp.wait_send()
        if with_sum:
            acc = out_ref[0]
            for b in range(1, 8):
                acc = acc + out_ref[b]
            rest[0][...] = acc

    out_shape = [_sds((8, r, LANES), F32)] + ([_sds((r, LANES), F32)] if with_sum else [])
    return pl.pallas_call(
        body, name=name, in_specs=[VMEM_SPEC], out_specs=[VMEM_SPEC] * len(out_shape), out_shape=out_shape,
        scratch_shapes=[pltpu.SemaphoreType.DMA((7,)), pltpu.SemaphoreType.DMA((7,))],
    )(v)


def mod_exchange(c_all, w_ada_sh, b_sh):
    def body(c_ref, w_ref, b_ref, out_ref, sc_ref, modp, send_s, recv_s):
        cv = c_ref[...]
        sc = cv * _sigmoid(cv)
        sc_ref[...] = sc
        modp[...] = jnp.dot(sc, w_ref[...], precision=lax.Precision.HIGHEST, preferred_element_type=F32) + b_ref[...]
        x, y, c = _pos()
        k = 2 * x + y
        out_ref[k] = modp[...]
        chips = _other_chips(x, y)
        copies = [_remote(modp, out_ref.at[k], send_s.at[j], recv_s.at[j], (cx, cy, c)) for j, (cx, cy) in enumerate(chips)]
        for cp in copies:
            cp.start()
        for j, (cx, cy) in enumerate(chips):
            _remote(modp, out_ref.at[2 * cx + cy], send_s.at[j], recv_s.at[j], (cx, cy, c)).wait_recv()
        for cp in copies:
            cp.wait_send()

    n = w_ada_sh.shape[1]
    return pl.pallas_call(
        body, name="mod_exchange", in_specs=[VMEM_SPEC] * 3, out_specs=[VMEM_SPEC] * 2,
        out_shape=[_sds((4, 8, n), F32), _sds((8, D), F32)],
        scratch_shapes=[pltpu.VMEM((8, n), F32), pltpu.SemaphoreType.DMA((3,)), pltpu.SemaphoreType.DMA((3,))],
        compiler_params=_params(VMEM_MB),
    )(c_all, w_ada_sh, b_sh)


def gather_weights(bufs):
    n = len(bufs)

    def body(*refs):
        outs = refs[n:2 * n]
        send_s, recv_s, fsend_s, frecv_s = refs[2 * n:]
        x, y, c = _pos()
        k = 2 * x + y
        chips = _other_chips(x, y)
        sends, fwds = [], []
        for a in range(n):
            half = outs[a].shape[1] // 2
            rows = pl.ds(c * half, half)
            for j, (cx, cy) in enumerate(chips):
                cp = _remote(outs[a].at[k, rows], outs[a].at[k, rows], send_s.at[3 * a + j], recv_s.at[3 * a + j], (cx, cy, c))
                cp.start()
                sends.append(cp)
        for a in range(n):
            half = outs[a].shape[1] // 2
            rows = pl.ds(c * half, half)
            for j, (cx, cy) in enumerate(chips):
                kj = 2 * cx + cy
                _remote(outs[a].at[kj, rows], outs[a].at[kj, rows], send_s.at[3 * a + j], recv_s.at[3 * a + j], (cx, cy, c)).wait_recv()
                fw = _remote(outs[a].at[kj, rows], outs[a].at[kj, rows], fsend_s.at[3 * a + j], frecv_s.at[3 * a + j], (x, y, 1 - c))
                fw.start()
                fwds.append(fw)
        for a in range(n):
            half = outs[a].shape[1] // 2
            orows = pl.ds((1 - c) * half, half)
            for j, (cx, cy) in enumerate(chips):
                kj = 2 * cx + cy
                _remote(outs[a].at[kj, orows], outs[a].at[kj, orows], fsend_s.at[3 * a + j], frecv_s.at[3 * a + j], (x, y, 1 - c)).wait_recv()
        for cp in sends + fwds:
            cp.wait_send()

    return pl.pallas_call(
        body, name="gather_weights", in_specs=[ANY_SPEC] * n, out_specs=[ANY_SPEC] * n,
        out_shape=[_sds(b.shape, b.dtype) for b in bufs],
        scratch_shapes=[pltpu.SemaphoreType.DMA((3 * n,))] * 4,
        input_output_aliases={a: a for a in range(n)},
    )(*bufs)


def cast_into_slabs(ws, sizes, chip, comm):
    tr = 64
    n = len(ws)
    n_in = [w.shape[0] // tr for w in ws]
    n_out = [r // tr for r, _ in sizes]
    steps = max(n_out)
    nci, nco = len(comm.ins), len(comm.out_shapes)

    def body(chip_ref, *refs):
        del chip_ref
        w_refs, cin = refs[:n], refs[n:n + nci]
        o_refs, cout = refs[n + nci:2 * n + nci], refs[2 * n + nci:2 * n + nci + nco]
        sems = refs[2 * n + nci + nco:]
        i = pl.program_id(0)

        @pl.when(i == 0)
        def _():
            comm.start(cin, cout, sems)

        for a in range(n):
            c0, cols = ws[a].shape[1], sizes[a][1]

            @pl.when(i < n_in[a])
            def _(a=a, c0=c0, cols=cols):
                o_refs[a][:, 0:c0] = w_refs[a][...].astype(BF16)
                if cols > c0:
                    o_refs[a][:, c0:] = jnp.zeros((tr, cols - c0), BF16)

            if n_out[a] > n_in[a]:
                @pl.when((i >= n_in[a]) & (i < n_out[a]))
                def _(a=a, cols=cols):
                    o_refs[a][...] = jnp.zeros((tr, cols), BF16)

        @pl.when(i == steps - 1)
        def _():
            comm.wait(cin, cout, sems)

    any_spec = pl.BlockSpec(memory_space=pl.ANY)
    grid_spec = pltpu.PrefetchScalarGridSpec(
        num_scalar_prefetch=1, grid=(steps,),
        in_specs=[pl.BlockSpec((tr, w.shape[1]), functools.partial(lambda i, k, last: (jnp.minimum(i, last), 0), last=n_in[a] - 1))
                  for a, w in enumerate(ws)] + [any_spec] * nci,
        out_specs=[pl.BlockSpec((None, tr, sizes[a][1]), functools.partial(lambda i, k, last: (k[0], jnp.minimum(i, last), 0), last=n_out[a] - 1))
                   for a in range(n)] + [any_spec] * nco,
        scratch_shapes=[pltpu.SemaphoreType.DMA((s,)) for s in comm.sems])
    res = pl.pallas_call(
        body, name="cast_into_slabs", grid_spec=grid_spec,
        out_shape=[_sds((4,) + tuple(sz), BF16) for sz in sizes] + comm.out_shapes,
        input_output_aliases={1 + n + ci: n + co for ci, co in comm.aliases.items()},
    )(chip, *ws, *comm.ins)
    return list(res[:n]), list(res[n:])


def cast_into_slab(w, rows, cols, chip, tag):
    r0, c0 = w.shape
    tr = 256 if (r0 % 256 == 0 and rows % 256 == 0) else 64
    n_in, n_out = r0 // tr, rows // tr

    def body(chip_ref, w_ref, o_ref):
        del chip_ref
        i = pl.program_id(0)

        @pl.when(i < n_in)
        def _():
            o_ref[:, 0:c0] = w_ref[...].astype(BF16)
            if cols > c0:
                o_ref[:, c0:] = jnp.zeros((tr, cols - c0), BF16)

        @pl.when(i >= n_in)
        def _():
            o_ref[...] = jnp.zeros((tr, cols), BF16)

    grid_spec = pltpu.PrefetchScalarGridSpec(
        num_scalar_prefetch=1, grid=(n_out,),
        in_specs=[pl.BlockSpec((tr, c0), lambda i, k: (jnp.minimum(i, n_in - 1), 0))],
        out_specs=pl.BlockSpec((None, tr, cols), lambda i, k: (k[0], i, 0)))
    return pl.pallas_call(body, name="cast_" + tag, grid_spec=grid_spec, out_shape=_sds((4, rows, cols), BF16))(chip, w)


def _row_tile(rows, cap=256):
    t = cap
    while rows % t or t % 8:
        t -= 8
    return t


def _comm_wait(sends, recvs, local=()):
    for cp in recvs:
        cp.wait_recv()
    for cp in sends:
        cp.wait_send()
    for cp in local:
        cp.wait()


def ag_ici(bufs):
    n = len(bufs)

    def copies(ins, outs, sems):
        send_s, recv_s = sems
        x, y, c = _pos()
        k = 2 * x + y
        sends, recvs = [], []
        for a in range(n):
            half = outs[a].shape[1] // 2
            rows = pl.ds(c * half, half)
            for j, (cx, cy) in enumerate(_other_chips(x, y)):
                sem = (send_s.at[3 * a + j], recv_s.at[3 * a + j], (cx, cy, c))
                sends.append(_remote(outs[a].at[k, rows], outs[a].at[k, rows], *sem))
                recvs.append(_remote(outs[a].at[k, rows], outs[a].at[2 * cx + cy, rows], *sem))
        return sends, recvs

    def start(ins, outs, sems):
        for cp in copies(ins, outs, sems)[0]:
            cp.start()

    def wait(ins, outs, sems):
        _comm_wait(*copies(ins, outs, sems))

    return Comm(bufs, [_sds(b.shape, b.dtype) for b in bufs], [3 * n, 3 * n], start, wait, aliases={a: a for a in range(n)})


def ag_d2d(bufs):
    n = len(bufs)

    def copies(ins, outs, sems):
        send_s, recv_s = sems
        x, y, c = _pos()
        sends, recvs = [], []
        for a in range(n):
            half = outs[a].shape[1] // 2
            rows, orows = pl.ds(c * half, half), pl.ds((1 - c) * half, half)
            for j, (cx, cy) in enumerate(_other_chips(x, y)):
                kj = 2 * cx + cy
                sem = (send_s.at[3 * a + j], recv_s.at[3 * a + j], (x, y, 1 - c))
                sends.append(_remote(outs[a].at[kj, rows], outs[a].at[kj, rows], *sem))
                recvs.append(_remote(outs[a].at[kj, orows], outs[a].at[kj, orows], *sem))
        return sends, recvs

    def start(ins, outs, sems):
        for cp in copies(ins, outs, sems)[0]:
            cp.start()

    def wait(ins, outs, sems):
        _comm_wait(*copies(ins, outs, sems))

    return Comm(bufs, [_sds(b.shape, b.dtype) for b in bufs], [3 * n, 3 * n], start, wait, aliases={a: a for a in range(n)})


def rs_a(grads):
    n = len(grads)

    def copies(ins, outs, sems):
        send_s, recv_s = sems
        x, y, c = _pos()
        cps = []
        for a in range(n):
            half = ins[a].shape[1] // 2
            cps.append(_remote(ins[a].at[:, pl.ds((1 - c) * half, half), :], outs[a], send_s.at[a], recv_s.at[a], (x, y, 1 - c)))
        return cps

    def start(ins, outs, sems):
        for cp in copies(ins, outs, sems):
            cp.start()

    def wait(ins, outs, sems):
        cps = copies(ins, outs, sems)
        _comm_wait(cps, cps)

    return Comm(grads, [_sds((4, g.shape[1] // 2, g.shape[2]), g.dtype) for g in grads], [n, n], start, wait)


def rs_b(pres):
    n = len(pres)

    def copies(ins, outs, sems):
        send_s, recv_s = sems
        x, y, c = _pos()
        cps = []
        for a in range(n):
            for j, (cx, cy) in enumerate(_other_chips(x, y)):
                cps.append(_remote(ins[a].at[2 * cx + cy], outs[a].at[j], send_s.at[3 * a + j], recv_s.at[3 * a + j], (cx, cy, c)))
        return cps

    def start(ins, outs, sems):
        for cp in copies(ins, outs, sems):
            cp.start()

    def wait(ins, outs, sems):
        cps = copies(ins, outs, sems)
        _comm_wait(cps, cps)

    return Comm(pres, [_sds((3,) + p_.shape[1:], p_.dtype) for p_ in pres], [3 * n, 3 * n], start, wait)


def rs_b_rows(pre, buf, lo, n):
    def copies(ins, outs, sems):
        send_s, recv_s = sems
        x, y, c = _pos()
        rows = pl.ds(lo, n)
        return [_remote(ins[0].at[2 * cx + cy, rows], outs[0].at[j, rows], send_s.at[j], recv_s.at[j], (cx, cy, c))
                for j, (cx, cy) in enumerate(_other_chips(x, y))]

    def start(ins, outs, sems):
        for cp in copies(ins, outs, sems):
            cp.start()

    def wait(ins, outs, sems):
        cps = copies(ins, outs, sems)
        _comm_wait(cps, cps)

    ins = [pre] if buf is None else [pre, buf]
    return Comm(ins, [_sds((3,) + pre.shape[1:], pre.dtype)], [3, 3], start, wait, aliases={} if buf is None else {1: 0})


def rs_c(reds):
    n = len(reds)

    def copies(ins, outs, sems):
        send_s, recv_s = sems
        x, y, c = _pos()
        sends, recvs = [], []
        for a in range(n):
            half = outs[a].shape[0] // 2
            rows, orows = pl.ds(c * half, half), pl.ds((1 - c) * half, half)
            sem = (send_s.at[a], recv_s.at[a], (x, y, 1 - c))
            sends.append(_remote(outs[a].at[rows], outs[a].at[rows], *sem))
            recvs.append(_remote(outs[a].at[orows], outs[a].at[orows], *sem))
        return sends, recvs

    def start(ins, outs, sems):
        for cp in copies(ins, outs, sems)[0]:
            cp.start()

    def wait(ins, outs, sems):
        _comm_wait(*copies(ins, outs, sems))

    return Comm(reds, [_sds(r_.shape, r_.dtype) for r_ in reds], [n, n], start, wait, aliases={a: a for a in range(n)})


def comm_join(*comms):
    ni = np.cumsum([0] + [len(c.ins) for c in comms])
    no = np.cumsum([0] + [len(c.out_shapes) for c in comms])
    ns = np.cumsum([0] + [len(c.sems) for c in comms])

    def parts(ins, outs, sems):
        return [(c, ins[ni[i]:ni[i + 1]], outs[no[i]:no[i + 1]], sems[ns[i]:ns[i + 1]]) for i, c in enumerate(comms)]

    def start(ins, outs, sems):
        for c, a, b, s in parts(ins, outs, sems):
            c.start(a, b, s)

    def wait(ins, outs, sems):
        for c, a, b, s in parts(ins, outs, sems):
            c.wait(a, b, s)

    aliases = {int(ni[i]) + k: int(no[i]) + v for i, c in enumerate(comms) for k, v in c.aliases.items()}
    return Comm(sum((c.ins for c in comms), []), sum((c.out_shapes for c in comms), []), sum((c.sems for c in comms), []),
                start, wait, aliases)


def comm_only(comm, name):
    nci, nco = len(comm.ins), len(comm.out_shapes)

    def body(*refs):
        ins, outs, sems = refs[:nci], refs[nci:nci + nco], refs[nci + nco:]
        comm.start(ins, outs, sems)
        comm.wait(ins, outs, sems)

    return pl.pallas_call(
        body, name=name, in_specs=[ANY_SPEC] * nci, out_specs=[ANY_SPEC] * nco, out_shape=comm.out_shapes,
        scratch_shapes=[pltpu.SemaphoreType.DMA((s,)) for s in comm.sems],
        input_output_aliases=comm.aliases,
    )(*comm.ins)


def rs_add_halves(g, other, core, name):
    _, r, cdim = g.shape
    half = r // 2
    tr = _row_tile(half, 256)
    nb = half // tr

    def body(core_ref, g_ref, o_ref, out_ref):
        del core_ref
        out_ref[...] = (g_ref[...] + o_ref[...]).astype(BF16)

    grid_spec = pltpu.PrefetchScalarGridSpec(
        num_scalar_prefetch=1, grid=(4, nb),
        in_specs=[pl.BlockSpec((None, tr, cdim), lambda k, i, cr: (k, cr[0] * nb + i, 0)),
                  pl.BlockSpec((None, tr, cdim), lambda k, i, cr: (k, i, 0))],
        out_specs=pl.BlockSpec((None, tr, cdim), lambda k, i, cr: (k, i, 0)))
    return pl.pallas_call(body, name=name, grid_spec=grid_spec, out_shape=_sds((4, half, cdim), BF16))(core, g, other)


def rs_add_slabs(t, pre, place, name):
    _, half, cdim = t.shape
    tr = _row_tile(half, 256)
    nb = half // tr

    def body(place_ref, own_ref, t_ref, out_ref):
        del place_ref
        out_ref[...] = ((own_ref[...].astype(F32) + t_ref[0].astype(F32)) + t_ref[1].astype(F32)) + t_ref[2].astype(F32)

    grid_spec = pltpu.PrefetchScalarGridSpec(
        num_scalar_prefetch=1, grid=(nb,),
        in_specs=[pl.BlockSpec((None, tr, cdim), lambda i, pr: (pr[0], i, 0)), pl.BlockSpec((3, tr, cdim), lambda i, pr: (0, i, 0))],
        out_specs=pl.BlockSpec((tr, cdim), lambda i, pr: (pr[1] * nb + i, 0)))
    return pl.pallas_call(body, name=name, grid_spec=grid_spec, out_shape=_sds((2 * half, cdim), F32))(place, pre, t)


def _adam_math(w, g, m, v):
    m = ADAM_B1 * m + (1.0 - ADAM_B1) * g
    v = ADAM_B2 * v + (1.0 - ADAM_B2) * (g * g)
    m_hat = m / (1.0 - ADAM_B1 ** ADAM_STEP)
    v_hat = v / (1.0 - ADAM_B2 ** ADAM_STEP)
    delta = -ADAM_LR * (m_hat / (jnp.sqrt(v_hat) + ADAM_EPS) + ADAM_WD * w)
    return delta, m, v


def adam(w, g, m, v, name, comm=None):
    r, cdim = w.shape
    tr = _row_tile(r) if r >= 8 else r

    def body(w_ref, g_ref, m_ref, v_ref, g_out, d_ref, nm_ref, nv_ref):
        gv = g_ref[:, :cdim]
        g_out[...] = gv
        d_ref[...], nm_ref[...], nv_ref[...] = _adam_math(w_ref[...], gv, m_ref[...], v_ref[...])

    blk = pl.BlockSpec((tr, cdim), lambda i: (i, 0))
    return _hosted_call(
        body, comm, (w, g, m, v), name=name, grid=(r // tr,),
        in_specs=[blk, pl.BlockSpec((tr, g.shape[1]), lambda i: (i, 0)), blk, blk],
        out_specs=[blk] * 4, out_shape=[_sds((r, cdim), F32)] * 4)


def adam_w_ada(sc_t, dmod_sh, w, m, v, comm=None):
    r, cdim = w.shape
    tr = 256

    def body(s_ref, d_ref, w_ref, m_ref, v_ref, g_ref, dl_ref, nm_ref, nv_ref):
        g = jnp.dot(s_ref[...], d_ref[...], precision=lax.Precision.HIGHEST, preferred_element_type=F32)
        g_ref[...] = g
        dl_ref[...], nm_ref[...], nv_ref[...] = _adam_math(w_ref[...], g, m_ref[...], v_ref[...])

    blk = pl.BlockSpec((tr, cdim), lambda i: (i, 0))
    return _hosted_call(
        body, comm, (sc_t, dmod_sh, w, m, v), name="adam_w_ada", grid=(r // tr,),
        in_specs=[pl.BlockSpec((tr, LANES), lambda i: (i, 0)), pl.BlockSpec((LANES, cdim), lambda i: (0, 0)), blk, blk, blk],
        out_specs=[blk] * 4, out_shape=[_sds((r, cdim), F32)] * 4)


SMALL_ROWS = 80


def kernel(x, c, w_ada, b_ada, g_mix, w_in, b_fgate, w_br_a, w_br_b, w_out, g_ffn, w_ffn_gate, w_ffn_up, w_ffn_down, g_final, loss_target, m_w_ada, m_b_ada, m_g_mix, m_w_in, m_b_fgate, m_w_br_a, m_w_br_b, m_w_out, m_g_ffn, m_w_ffn_gate, m_w_ffn_up, m_w_ffn_down, m_g_final, v_w_ada, v_b_ada, v_g_mix, v_w_in, v_b_fgate, v_w_br_a, v_w_br_b, v_w_out, v_g_ffn, v_w_ffn_gate, v_w_ffn_up, v_w_ffn_down, v_g_final):
    xi, yi, ci = _pos()
    chip = 2 * xi + yi
    seq = 4 * xi + 2 * yi + ci
    n_ada = w_ada.shape[2]

    c_all = gather_all(c.reshape(8, LANES), "gather_c", False)[0].reshape(8, D)
    b_sh = lax.dynamic_slice(b_ada, (0, chip * n_ada), (1, n_ada))
    mod_all, sc = mod_exchange(c_all, w_ada[0], b_sh)
    mod = lax.dynamic_index_in_dim(mod_all, seq, axis=1, keepdims=False).reshape(6, D)
    mod8 = jnp.pad(mod, ((0, 2), (0, 0)))

    core = ci.astype(jnp.int32).reshape(1)
    chip1 = chip.astype(jnp.int32).reshape(1)
    place = jnp.stack([chip, ci]).astype(jnp.int32)
    s_in = cast_into_slab(w_in[0], D, IN_SHARD_PAD, chip1, "w_in")
    (s_bra, s_brb, s_out, s_gate, s_up, s_down), (g_in,) = cast_into_slabs(
        [w_br_a[0], w_br_b[0], w_out[0], w_ffn_gate[0].T, w_ffn_up[0].T, w_ffn_down[0]],
        [(512, 256), (256, 256), (256, D), (FF_PAD, D), (FF_PAD, D), (FF_PAD, D)], chip1, ag_ici([s_in]))
    xs, tgt, g_fin = x[0], loss_target[0], g_final.reshape(1, D)

    def halves(gs, others, tag):
        return [rs_add_halves(g, o, core, f"rs_{tag}_halves_{i}") for i, (g, o) in enumerate(zip(gs, others))]

    def slab_sums(ts, pres, tag):
        return [rs_add_slabs(t, pre, place, f"rs_{tag}_slabs_{i}") for i, (t, pre) in enumerate(zip(ts, pres))]

    tabs = rope_tables()
    h1, (g_in,) = norm_mod_fwd(xs, g_mix, mod8, 0, 1, comm=ag_d2d([g_in]))
    w_lay = lay_from_shards(g_in)
    p, mix_w = in_proj_fwd(h1, w_lay, tabs, comm=ag_ici([s_bra, s_brb, s_out]))
    frow, fraw, fcol = fgate_fwd(p, jnp.pad(b_fgate, ((0, 0), (0, LANES - 8))))
    (ya_att, gcol), res = fox_fwd(p, fcol, comm=comm_join(ag_d2d(mix_w), ag_ici([s_gate, s_up])))
    g_bra, g_brb, g_out = res[:3]
    (yb, lse_b), res = dil_fwd(p, comm=comm_join(ag_d2d(res[3:]), ag_ici([s_down])))
    w_gate, w_up = res[:2]
    w_bra = g_bra.transpose(1, 0, 2).reshape(512, D)
    w_brb = g_brb.transpose(1, 0, 2).reshape(256, D)
    w_o = g_out.reshape(D, D)
    (merged, ya, ybp), (g_down,) = merge_fwd(ya_att, yb, p, w_bra, w_brb, comm=ag_d2d(res[2:]))
    w_down = g_down.reshape(FFP, D)
    mix, x1, h2 = out_proj_fwd(merged, w_o, xs, mod8, g_ffn)
    a, u, z = ffn_up_fwd(h2, w_gate, w_up)
    dx2, dffn, dg_final, dga_f, loss_part = ffn_down_loss(z, w_down, x1, mod8, g_fin, tgt)

    da, du, dw_down = ffn_down_bwd(dffn, w_down, a, u, z)
    g_down = [dw_down.reshape(4, FF_PAD, D)]
    dh2a, oth = mm_nt(da, w_gate, "ffn_gate_dx", comm=rs_a(g_down))
    pre_down = halves(g_down, oth, "down")
    dh2b, _ = mm_nt(du, w_up, "ffn_up_dx")
    dw_gate, _ = mm_tn(h2, da, "ffn_gate_dw", shard_major=True)
    dw_up, _ = mm_tn(h2, du, "ffn_up_dw", shard_major=True)
    g_gu = [dw_gate, dw_up]
    (dx1, dp1, dya_att, dyb, cs_mid, dw_out, dw_bra, dw_brb), res = mid_bwd(
        dh2a, dh2b, x1, dx2, mix, mod8, g_ffn, p, ya, ybp, merged, ya_att, yb, w_o, w_bra, w_brb,
        comm=comm_join(rs_b(pre_down), rs_a(g_gu)))
    red_down = slab_sums(res[:1], pre_down, "down")
    pre_gu = halves(g_gu, res[1:], "gu")
    g_mix3 = [dw_bra.reshape(512, 4, 256).transpose(1, 0, 2), dw_brb.reshape(256, 4, 256).transpose(1, 0, 2), dw_out.reshape(4, 256, D)]
    (dp2, dfrow, dfcol), res = fox_bwd(p, dya_att, ya_att, gcol, fcol, dp1,
                                       comm=comm_join(rs_b(pre_gu), rs_c(red_down), rs_a(g_mix3)))
    red_gu = slab_sums(res[:2], pre_gu, "gu")
    r_down = res[2]
    pre_mix3 = halves(g_mix3, res[3:], "mix")
    dp3, db_fg = fgate_bwd(dfrow.reshape(8, S), dfcol, fraw, dp2)
    dp4, res = dil_bwd(p, dyb, yb, lse_b, tabs, dp3, comm=comm_join(rs_c(red_gu), rs_b(pre_mix3)))
    r_gate, r_up = res[:2]
    red_mix3 = slab_sums(res[2:], pre_mix3, "mix")
    dw_lay, (r_bra, r_brb, r_out) = mm_tn(h1, dp4, "in_proj_dw", comm=rs_c(red_mix3))
    g_in4 = [shards_from_lay(dw_lay)]
    dh1, oth = mm_nt(dp4, w_lay, "in_proj_dx", comm=rs_a(g_in4))
    (pre_in,) = halves(g_in4, oth, "in")
    qrows = pre_in.shape[1] // 4
    (dx, cs_in), (t_in,) = in_bwd_tail(dh1, xs, dx1, mod8, g_mix, comm=rs_b_rows(pre_in, None, 0, qrows))

    dmod = jnp.concatenate([cs_in[0:2], cs_mid[3:4], cs_mid[0:2], dga_f], axis=0)
    small = dict(dmod=dmod, dg_mix=cs_in[2:3], dg_ffn=cs_mid[2:3], dg_final=dg_final, db_fgate=db_fg[:, 0], loss=loss_part[0, 0])
    sv = jnp.concatenate([
        small["dmod"].reshape(48, LANES), small["dg_mix"].reshape(8, LANES), small["dg_ffn"].reshape(8, LANES),
        small["dg_final"].reshape(8, LANES), jnp.pad(small["db_fgate"], (0, LANES - 8)).reshape(1, LANES),
        jnp.broadcast_to(small["loss"], (1, LANES)), jnp.zeros((SMALL_ROWS - 74, LANES), F32)], axis=0)
    sv_all, sv_sum = gather_all(sv, "gather_small", True)
    loss = sv_sum[73, 0]
    g_small = dict(b_ada=sv_sum[0:48].reshape(1, 6 * D), g_mix=sv_sum[48:56].reshape(1, D), g_ffn=sv_sum[56:64].reshape(1, D),
                   g_final=sv_sum[64:72].reshape(D), b_fgate=sv_sum[72, 0:8].reshape(1, 8))

    dmod_all = lax.dynamic_slice(sv_all[:, 0:48, :].reshape(8, 6 * D), (0, chip * n_ada), (8, n_ada))
    (g_ada, d_ada, nm_ada, nv_ada), (t_in,) = adam_w_ada(
        jnp.pad(sc.T, ((0, 0), (0, LANES - 8))), jnp.pad(dmod_all, ((0, LANES - 8), (0, 0))), w_ada[0], m_w_ada[0], v_w_ada[0],
        comm=rs_b_rows(pre_in, t_in, qrows, qrows))

    big = dict(w_in=(w_in, m_w_in, v_w_in), w_br_a=(w_br_a, m_w_br_a, v_w_br_a), w_br_b=(w_br_b, m_w_br_b, v_w_br_b),
               w_out=(w_out, m_w_out, v_w_out), w_ffn_gate=(w_ffn_gate, m_w_ffn_gate, v_w_ffn_gate),
               w_ffn_up=(w_ffn_up, m_w_ffn_up, v_w_ffn_up), w_ffn_down=(w_ffn_down, m_w_ffn_down, v_w_ffn_down))
    gpad = dict(w_br_a=r_bra, w_br_b=r_brb, w_out=r_out, w_ffn_gate=r_gate, w_ffn_up=r_up, w_ffn_down=r_down)
    upd = {}
    for part, nm in ((2, "w_ffn_gate"), (3, "w_ffn_up")):
        w, m, v = big[nm]
        res, (t_in,) = adam(w[0].T, gpad[nm], m[0].T, v[0].T, "adam_" + nm, comm=rs_b_rows(pre_in, t_in, part * qrows, qrows))
        upd[nm] = [t.T for t in res]
    (gpad["w_in"],) = comm_only(rs_c(slab_sums([t_in], [pre_in], "in")), "rs_in_share")
    for nm, (w, m, v) in big.items():
        if nm not in upd:
            upd[nm] = adam(w[0], gpad[nm], m[0], v[0], "adam_" + nm)[0]

    def pack(gm, gf, gl, ba, bf):
        rows = [gm.reshape(1, D), gf.reshape(1, D), gl.reshape(1, D), ba.reshape(6, D), jnp.pad(bf.reshape(1, 8), ((0, 0), (0, D - 8)))]
        return jnp.concatenate(rows + [jnp.zeros((6, D), F32)], axis=0)

    packed = adam(pack(g_mix, g_ffn, g_final, b_ada, b_fgate),
                  pack(g_small["g_mix"], g_small["g_ffn"], g_small["g_final"], g_small["b_ada"], g_small["b_fgate"]),
                  pack(m_g_mix, m_g_ffn, m_g_final, m_b_ada, m_b_fgate), pack(v_g_mix, v_g_ffn, v_g_final, v_b_ada, v_b_fgate),
                  "adam_small")[0]

    def unpack(t):
        return dict(g_mix=t[0:1], g_ffn=t[1:2], g_final=t[2], b_ada=t[3:9].reshape(1, 6 * D), b_fgate=t[9:10, 0:8])

    small_upd = [unpack(t) for t in packed[1:]]
    order =["w_ada", "b_ada", "g_mix", "w_in", "b_fgate", "w_br_a", "w_br_b", "w_out", "g_ffn", "w_ffn_gate", "w_ffn_up", "w_ffn_down", "g_final"]

    def leaf(nm, which):
        if nm == "w_ada":
            return (g_ada, d_ada, nm_ada, nv_ada)[which][None]
        if nm in big:
            return upd[nm][which][None]
        return g_small[nm] if which == 0 else small_upd[which - 1][nm]

    outs = [loss, dx[None]]
    for which in range(4):
        outs += [leaf(nm, which) for nm in order]
    return tuple(outs)
```

```python
import functools

import numpy as np
import jax
import jax.numpy as jnp
from jax import lax
from jax.experimental import pallas as pl
from jax.experimental.pallas import tpu as pltpu

F32, BF16 = jnp.float32, jnp.bfloat16
S, D = 2048, 1024
HD = 64
LANES = 128
N_FOX_PAIRS, N_DIL_PAIRS = 4, 2
DIL_GROUPS = ((1, 16), (4, 4), (16, 1))
SPAN = 128
ROT_DIM, ROPE_THETA = 16, 500000.0
D_FF, FF_SHARD, FF_PAD = 2816, 704, 768
FFP = 4 * FF_PAD
IN_COLS, IN_SHARD, IN_SHARD_PAD = 5896, 1474, 1536
LAY_B, LAY_A, LAY_F, LAY_G, LAY_N = 0, 2304, 3840, 4096, 6144
EPS, NEG = 1e-6, -1e30
SCALE = HD ** -0.5
ADAM_LR, ADAM_B1, ADAM_B2, ADAM_EPS, ADAM_WD, ADAM_STEP = 0.001, 0.9, 0.999, 1e-08, 0.01, 10
VMEM_MB = 56
MESH = pl.DeviceIdType.MESH


def _params(vmem_mb=None, **kw):
    if vmem_mb is not None:
        kw["vmem_limit_bytes"] = vmem_mb * 1024 * 1024
    return pltpu.CompilerParams(**kw)


def _sds(shape, dtype):
    return jax.ShapeDtypeStruct(shape, dtype)


def _sigmoid(x):
    return 1.0 / (1.0 + jnp.exp(-x))


def _colsum8(x):
    tm, n = x.shape
    return jnp.sum(x.reshape(tm // 8, 8, n), axis=0)


class Comm:
    def __init__(self, ins, out_shapes, sems, start, wait, aliases=None):
        self.ins, self.out_shapes, self.sems = list(ins), list(out_shapes), list(sems)
        self.start, self.wait, self.aliases = start, wait, dict(aliases or {})


def _hosted_call(body, comm, args, *, name, grid, in_specs, out_specs, out_shape, scratch_shapes=(), aliases=None, vmem_mb=None):
    single = not isinstance(out_shape, (list, tuple))
    out_specs_l = [out_specs] if single else list(out_specs)
    out_shape_l = [out_shape] if single else list(out_shape)
    n_in, n_out, n_scr = len(in_specs), len(out_shape_l), len(scratch_shapes)
    aliases = dict(aliases or {})
    if comm is None:
        res = pl.pallas_call(body, name=name, grid=grid, in_specs=list(in_specs), out_specs=out_specs, out_shape=out_shape,
                             scratch_shapes=list(scratch_shapes), input_output_aliases=aliases,
                             compiler_params=_params(vmem_mb))(*args)
        return res, []
    nci, nco = len(comm.ins), len(comm.out_shapes)

    def wrapped(*refs):
        main_in, cin = refs[:n_in], refs[n_in:n_in + nci]
        o0 = n_in + nci
        main_out, cout = refs[o0:o0 + n_out], refs[o0 + n_out:o0 + n_out + nco]
        s0 = o0 + n_out + nco
        scr, sems = refs[s0:s0 + n_scr], refs[s0 + n_scr:]
        ids = [pl.program_id(i) for i in range(len(grid))]
        first = functools.reduce(jnp.logical_and, [i == 0 for i in ids])
        last = functools.reduce(jnp.logical_and, [i == g - 1 for i, g in zip(ids, grid)])

        @pl.when(first)
        def _():
            comm.start(cin, cout, sems)

        body(*main_in, *main_out, *scr)

        @pl.when(last)
        def _():
            comm.wait(cin, cout, sems)

    for ci, co in comm.aliases.items():
        aliases[n_in + ci] = n_out + co
    any_spec = pl.BlockSpec(memory_space=pl.ANY)
    res = pl.pallas_call(
        wrapped, name=name, grid=grid, in_specs=list(in_specs) + [any_spec] * nci, out_specs=out_specs_l + [any_spec] * nco,
        out_shape=out_shape_l + comm.out_shapes,
        scratch_shapes=list(scratch_shapes) + [pltpu.SemaphoreType.DMA((s,)) for s in comm.sems],
        input_output_aliases=aliases, compiler_params=_params(vmem_mb))(*args, *comm.ins)
    main = list(res[:n_out])
    return (main[0] if single else main), list(res[n_out:])


def norm_mod_fwd(x, g, mod, shift_row, scale_row, comm=None):
    tm = 256

    def body(x_ref, g_ref, mod_ref, h_ref):
        xv = x_ref[...]
        r = lax.rsqrt(jnp.mean(xv * xv, axis=1, keepdims=True) + EPS)
        n = xv * r * g_ref[...]
        h = n * (1.0 + mod_ref[scale_row:scale_row + 1, :]) + mod_ref[shift_row:shift_row + 1, :]
        h_ref[...] = h.astype(BF16)

    return _hosted_call(
        body, comm, (x, g, mod), name="norm_mod_fwd", grid=(S // tm,),
        in_specs=[pl.BlockSpec((tm, D), lambda i: (i, 0)), pl.BlockSpec((1, D), lambda i: (0, 0)),
                  pl.BlockSpec((8, D), lambda i: (0, 0))],
        out_specs=pl.BlockSpec((tm, D), lambda i: (i, 0)),
        out_shape=_sds((S, D), BF16))


def rope_tables():
    pos = jnp.arange(S, dtype=F32)
    inv_freq = ROPE_THETA ** (-jnp.arange(0, ROT_DIM, 2, dtype=F32) / ROT_DIM)
    ang = pos[:, None] * inv_freq[None, :]
    cos, sin = jnp.cos(ang), jnp.sin(ang)
    one, zero = jnp.ones((S, HD - ROT_DIM), F32), jnp.zeros((S, HD - ROT_DIM), F32)
    z8 = jnp.zeros((S, 8), F32)
    c = jnp.concatenate([cos, cos, one], axis=1)
    s1 = jnp.concatenate([-sin, z8, zero], axis=1)
    s2 = jnp.concatenate([z8, sin, zero], axis=1)
    return tuple(jnp.concatenate([t, t], axis=1) for t in (c, s1, s2))


def _rope(y, c, s1, s2):
    return y * c + pltpu.roll(y, LANES - 8, 1) * s1 + pltpu.roll(y, 8, 1) * s2


def _rope_bwd(dy, c, s1, s2):
    return dy * c + pltpu.roll(dy * s1, 8, 1) + pltpu.roll(dy * s2, LANES - 8, 1)


def in_proj_fwd(h, w_lay, tabs, comm=None):
    tm, tn = 2048, 768
    n_rope = N_DIL_PAIRS * 3 // 2

    def body(a_ref, w_ref, c_ref, s1_ref, s2_ref, o_ref):
        j = pl.program_id(0)
        y = jnp.dot(a_ref[...], w_ref[...], preferred_element_type=F32)

        @pl.when(j < n_rope)
        def _():
            c, s1, s2 = c_ref[...], s1_ref[...], s2_ref[...]
            for t in range(tn // LANES):
                chunk = y[:, LANES * t:LANES * (t + 1)]
                o_ref[:, LANES * t:LANES * (t + 1)] = chunk if t % 3 == 2 else _rope(chunk, c, s1, s2)

        @pl.when(j >= n_rope)
        def _():
            o_ref[...] = y

    tab = pl.BlockSpec((tm, LANES), lambda j, i: (i, 0))
    return _hosted_call(
        body, comm, (h, w_lay, *tabs), name="in_proj_fwd", grid=(LAY_N // tn, S // tm),
        in_specs=[pl.BlockSpec((tm, D), lambda j, i: (i, 0)), pl.BlockSpec((D, tn), lambda j, i: (0, j)), tab, tab, tab],
        out_specs=pl.BlockSpec((tm, tn), lambda j, i: (i, j)),
        out_shape=_sds((S, LAY_N), F32), vmem_mb=VMEM_MB)


def _log1p_small(t):
    return jnp.where(t < 1e-2, t * (1.0 - t * (0.5 - t * (1.0 / 3.0))), jnp.log(1.0 + t))


def fgate_fwd(p, b_pad):
    def body(fa_ref, b_ref, frow_ref, fraw_ref, fcol_ref):
        f = fa_ref[...] + b_ref[...]
        fr = f.T[0:8, :]
        ls = jnp.minimum(fr, 0.0) - _log1p_small(jnp.exp(-jnp.abs(fr)))
        lane = lax.broadcasted_iota(jnp.int32, (8, S), 1)
        acc, sh = ls, 1
        while sh < S:
            acc = acc + jnp.where(lane >= sh, pltpu.roll(acc, sh, 1), 0.0)
            sh *= 2
        frow_ref[...] = acc
        fraw_ref[...] = fr
        for hh in range(8):
            fcol_ref[hh] = jnp.broadcast_to(acc[hh:hh + 1, :], (LANES, S)).T

    return pl.pallas_call(
        body, name="fgate_fwd", grid=(1,),
        in_specs=[pl.BlockSpec((S, LANES), lambda i: (0, LAY_F // LANES)), pl.BlockSpec((1, LANES), lambda i: (0, 0))],
        out_specs=[pl.BlockSpec((8, S), lambda i: (0, 0)), pl.BlockSpec((8, S), lambda i: (0, 0)),
                   pl.BlockSpec((8, S, LANES), lambda i: (0, 0, 0))],
        out_shape=[_sds((8, S), F32), _sds((8, S), F32), _sds((8, S, LANES), F32)],
        compiler_params=_params(VMEM_MB),
    )(p, b_pad)


def _head_masks(rows):
    lane = lax.broadcasted_iota(jnp.int32, (rows, LANES), 1)
    return lane < HD, lane >= HD


FT = 256


def _split3(f):
    hi = f.astype(BF16).astype(F32)
    r = f - hi
    mid = r.astype(BF16).astype(F32)
    return hi, mid, r - mid


def _fox_operands(qkv_ref, tcol_ref, scol_ref, qa_s, ka_s):
    rows = 256
    lane = lax.broadcasted_iota(jnp.int32, (rows, LANES), 1)

    def chunk(i, _):
        r = pl.ds(pl.multiple_of(i * rows, rows), rows)
        q, k = qkv_ref[r, 0:LANES], qkv_ref[r, LANES:2 * LANES]
        s0, s1 = _split3(scol_ref[0, r, :]), _split3(scol_ref[1, r, :])
        ka = jnp.where(lane == 0, -s0[0], jnp.where(lane == 1, -s0[1], jnp.where(lane == 2, -s0[2], jnp.where(
            lane == 3, -s1[0], jnp.where(lane == 4, -s1[1], jnp.where(lane == 5, -s1[2], jnp.where(lane < 9, 1.0, 0.0)))))))
        ka_s[r, 0:LANES] = k.astype(BF16)
        ka_s[r, LANES:2 * LANES] = ka.astype(BF16)
        for hh in range(2):
            own = (lane < HD) if hh == 0 else (lane >= HD)
            t3 = _split3(tcol_ref[hh, r, :])
            ones = (lane >= 3 * hh) & (lane < 3 * hh + 3)
            qa = jnp.where(ones, 1.0, jnp.where(lane == 6, t3[0], jnp.where(lane == 7, t3[1], jnp.where(lane == 8, t3[2], 0.0))))
            qa_s[hh, r, 0:LANES] = jnp.where(own, q * SCALE, 0.0).astype(BF16)
            qa_s[hh, r, LANES:2 * LANES] = qa.astype(BF16)
        return 0

    lax.fori_loop(0, S // rows, chunk, 0)


def fox_fwd(p, fcol, comm=None):
    nt = (((1,), (1,)), ((), ()))

    def body(qkv_ref, fc_ref, o_ref, g_ref, qa_s, ka_s):
        _fox_operands(qkv_ref, fc_ref, fc_ref, qa_s, ka_s)
        masks = _head_masks(FT)
        causal = lax.broadcasted_iota(jnp.int32, (FT, FT), 1) <= lax.broadcasted_iota(jnp.int32, (FT, FT), 0)
        causal2 = jnp.concatenate([causal, causal], axis=0)

        def qloop(qi, _):
            q0 = pl.multiple_of(qi * FT, FT)
            qa = jnp.concatenate([qa_s[0, pl.ds(q0, FT), :], qa_s[1, pl.ds(q0, FT), :]], axis=0)

            def step(kb, carry, diagonal):
                m, l, acc = carry
                k0 = pl.multiple_of(kb * FT, FT)
                v = qkv_ref[pl.ds(k0, FT), 2 * LANES:3 * LANES].astype(BF16)
                s = lax.dot_general(qa, ka_s[pl.ds(k0, FT), :], nt, preferred_element_type=F32)
                if diagonal:
                    s = jnp.where(causal2, s, NEG)
                m_new = jnp.maximum(m, jnp.max(s, axis=1, keepdims=True))
                pr = jnp.exp(s - m_new)
                alpha = jnp.exp(m - m_new)
                return (m_new, l * alpha + jnp.sum(pr, axis=1, keepdims=True),
                        acc * alpha + jnp.dot(pr.astype(BF16), v, preferred_element_type=F32))

            init = (jnp.full((2 * FT, 1), NEG, F32), jnp.zeros((2 * FT, 1), F32), jnp.zeros((2 * FT, LANES), F32))
            carry = lax.fori_loop(0, qi, lambda kb, cr: step(kb, cr, False), init)
            m, l, acc = step(qi, carry, True)
            out = acc / l
            lse = m + jnp.log(l)
            o_ref[pl.ds(q0, FT), :] = jnp.where(masks[0], out[:FT], out[FT:]).astype(BF16)
            g_ref[0, pl.ds(q0, FT), :] = fc_ref[0, pl.ds(q0, FT), :] - lse[:FT]
            g_ref[1, pl.ds(q0, FT), :] = fc_ref[1, pl.ds(q0, FT), :] - lse[FT:]
            return 0

        lax.fori_loop(0, S // FT, qloop, 0)

    a_blk = LAY_A // 384
    return _hosted_call(
        body, comm, (p, fcol), name="fox_fwd", grid=(N_FOX_PAIRS,),
        in_specs=[pl.BlockSpec((S, 384), lambda p_: (0, a_blk + p_)), pl.BlockSpec((2, S, LANES), lambda p_: (p_, 0, 0))],
        out_specs=[pl.BlockSpec((S, LANES), lambda p_: (0, p_)), pl.BlockSpec((2, S, LANES), lambda p_: (p_, 0, 0))],
        out_shape=[_sds((S, 4 * LANES), BF16), _sds((8, S, LANES), F32)],
        scratch_shapes=[pltpu.VMEM((2, S, 2 * LANES), BF16), pltpu.VMEM((S, 2 * LANES), BF16)],
        vmem_mb=VMEM_MB)


def _dil_rows(ref, start, d):
    return ref[pl.ds(start, SPAN), :] if d == 1 else ref[pl.ds(start, SPAN, stride=d), :]


def _dil_store(ref, start, d, val):
    if d == 1:
        ref[pl.ds(start, SPAN), :] = val
    else:
        ref[pl.ds(start, SPAN, stride=d), :] = val


def _band_mask(has_prev):
    qi = lax.broadcasted_iota(jnp.int32, (SPAN, 2 * SPAN), 0) + SPAN
    kj = lax.broadcasted_iota(jnp.int32, (SPAN, 2 * SPAN), 1)
    dist = qi - kj
    return (dist >= 0) & (dist <= SPAN) & (has_prev | (kj >= SPAN))


def _dil_block(n, d, nb):
    r, j = n // nb, n % nb
    start = r + d * SPAN * j
    prev = jnp.maximum(start - d * SPAN, r)
    return start, prev, j > 0


def dil_fwd(p, comm=None):
    def body(*refs):
        qkv = [refs[3 * g:3 * g + 3] for g in range(3)]
        y_ref, lse_ref = refs[9], refs[10]
        acc_s, m_s, l_s = refs[11], refs[12], refs[13]
        masks = _head_masks(SPAN)
        for g, (d, nb) in enumerate(DIL_GROUPS):
            q_ref, k_ref, v_ref = qkv[g]

            def blk(n, _):
                start, prev, has_prev = _dil_block(n, d, nb)
                q = _dil_rows(q_ref, start, d)
                kc = jnp.concatenate([_dil_rows(k_ref, prev, d), _dil_rows(k_ref, start, d)], axis=0).astype(BF16)
                vc = jnp.concatenate([_dil_rows(v_ref, prev, d), _dil_rows(v_ref, start, d)], axis=0).astype(BF16)
                valid = _band_mask(has_prev)
                valid2 = jnp.concatenate([valid, valid], axis=0)
                q2 = (jnp.concatenate([jnp.where(masks[0], q, 0.0), jnp.where(masks[1], q, 0.0)], axis=0) * SCALE).astype(BF16)
                s = jnp.where(valid2, lax.dot_general(q2, kc, (((1,), (1,)), ((), ())), preferred_element_type=F32), NEG)
                m = jnp.max(s, axis=1, keepdims=True)
                pr = jnp.exp(s - m)
                l = jnp.sum(pr, axis=1, keepdims=True)
                acc = jnp.dot(pr.astype(BF16), vc, preferred_element_type=F32)
                _dil_store(acc_s.at[g], start, d, jnp.where(masks[0], acc[:SPAN], acc[SPAN:]))
                _dil_store(m_s.at[g], start, d, jnp.where(masks[0], m[:SPAN], m[SPAN:]))
                _dil_store(l_s.at[g], start, d, jnp.where(masks[0], l[:SPAN], l[SPAN:]))
                return 0

            lax.fori_loop(0, 16, blk, 0)

        def merge(i, _):
            rows = pl.ds(pl.multiple_of(i * 256, 256), 256)
            m = [m_s[g, rows, :] for g in range(3)]
            mx = jnp.maximum(jnp.maximum(m[0], m[1]), m[2])
            w = [jnp.exp(m[g] - mx) for g in range(3)]
            l = sum(l_s[g, rows, :] * w[g] for g in range(3))
            y_ref[rows, :] = sum(acc_s[g, rows, :] * w[g] for g in range(3)) / l
            lse_ref[rows, :] = mx + jnp.log(l)
            return 0

        lax.fori_loop(0, S // 256, merge, 0)

    def spec(g, t):
        return pl.BlockSpec((S, LANES), lambda p_: (0, (p_ * 3 + g) * 3 + t))

    return _hosted_call(
        body, comm, [p] * 9, name="dil_fwd", grid=(N_DIL_PAIRS,),
        in_specs=[spec(g, t) for g in range(3) for t in range(3)],
        out_specs=[pl.BlockSpec((S, LANES), lambda p_: (0, p_)), pl.BlockSpec((S, LANES), lambda p_: (0, p_))],
        out_shape=[_sds((S, 2 * LANES), F32), _sds((S, 2 * LANES), F32)],
        scratch_shapes=[pltpu.VMEM((3, S, LANES), F32)] * 3,
        vmem_mb=VMEM_MB)


def merge_fwd(ya_att, yb, p, w_bra, w_brb, comm=None):
    tm = 256
    gblk = LAY_G // D

    def body(a_ref, b_ref, ga_ref, gb_ref, wa_ref, wb_ref, mg_ref, ya_ref, yb_ref):
        ya = jnp.dot(a_ref[...], wa_ref[...], preferred_element_type=F32)
        ybp = jnp.dot(b_ref[...].astype(BF16), wb_ref[...], preferred_element_type=F32)
        mg_ref[...] = (_sigmoid(ga_ref[...]) * ya + _sigmoid(gb_ref[...]) * ybp).astype(BF16)
        ya_ref[...] = ya
        yb_ref[...] = ybp

    row = lambda w: pl.BlockSpec((tm, w), lambda i: (i, 0))
    return _hosted_call(
        body, comm, (ya_att, yb, p, p, w_bra, w_brb), name="merge_fwd", grid=(S // tm,),
        in_specs=[row(512), row(256), pl.BlockSpec((tm, D), lambda i: (i, gblk)), pl.BlockSpec((tm, D), lambda i: (i, gblk + 1)),
                  pl.BlockSpec((512, D), lambda i: (0, 0)), pl.BlockSpec((256, D), lambda i: (0, 0))],
        out_specs=[row(D), row(D), row(D)],
        out_shape=[_sds((S, D), BF16), _sds((S, D), F32), _sds((S, D), F32)])


def out_proj_fwd(merged, w_out, x, mod, g_ffn):
    tm = 256

    def body(a_ref, w_ref, x_ref, mod_ref, g_ref, mix_ref, x1_ref, h2_ref):
        mix = jnp.dot(a_ref[...], w_ref[...], preferred_element_type=F32)
        x1 = x_ref[...] + mod_ref[2:3, :] * mix
        r = lax.rsqrt(jnp.mean(x1 * x1, axis=1, keepdims=True) + EPS)
        h2 = (x1 * r * g_ref[...]) * (1.0 + mod_ref[4:5, :]) + mod_ref[3:4, :]
        mix_ref[...] = mix
        x1_ref[...] = x1
        h2_ref[...] = h2.astype(BF16)

    row = pl.BlockSpec((tm, D), lambda i: (i, 0))
    return pl.pallas_call(
        body, name="out_proj_fwd", grid=(S // tm,),
        in_specs=[row, pl.BlockSpec((D, D), lambda i: (0, 0)), row, pl.BlockSpec((8, D), lambda i: (0, 0)),
                  pl.BlockSpec((1, D), lambda i: (0, 0))],
        out_specs=[row, row, row],
        out_shape=[_sds((S, D), F32), _sds((S, D), F32), _sds((S, D), BF16)],
    )(merged, w_out, x, mod, g_ffn)


def ffn_up_fwd(h2, w_gate, w_up):
    tm = 1024
    nt = (((1,), (1,)), ((), ()))

    def body(h_ref, wg_ref, wu_ref, a_ref, u_ref, z_ref):
        h = h_ref[...]
        a = lax.dot_general(h, wg_ref[...], nt, preferred_element_type=F32)
        u = lax.dot_general(h, wu_ref[...], nt, preferred_element_type=F32)
        a_ref[...] = a
        u_ref[...] = u
        z_ref[...] = (a * _sigmoid(a) * u).astype(BF16)

    out = pl.BlockSpec((tm, FF_PAD), lambda k, i: (i, k))
    return pl.pallas_call(
        body, name="ffn_up_fwd", grid=(4, S // tm),
        in_specs=[pl.BlockSpec((tm, D), lambda k, i: (i, 0)), pl.BlockSpec((None, FF_PAD, D), lambda k, i: (k, 0, 0)),
                  pl.BlockSpec((None, FF_PAD, D), lambda k, i: (k, 0, 0))],
        out_specs=[out, out, out],
        out_shape=[_sds((S, FFP), F32), _sds((S, FFP), F32), _sds((S, FFP), BF16)], compiler_params=_params(VMEM_MB),
    )(h2, w_gate, w_up)


def ffn_down_loss(z, w_down, x1, mod, g_final, tgt):
    tm = 256

    def body(z_ref, w_ref, x1_ref, mod_ref, g_ref, t_ref, dx2_ref, dffn_ref, dg_ref, dga_ref, loss_ref, s_dg, s_dga, s_loss):
        i = pl.program_id(0)

        @pl.when(i == 0)
        def _():
            s_dg[...] = jnp.zeros_like(s_dg)
            s_dga[...] = jnp.zeros_like(s_dga)
            s_loss[...] = jnp.zeros_like(s_loss)

        ffn = jnp.dot(z_ref[...], w_ref[...], preferred_element_type=F32)
        gaf = mod_ref[5:6, :]
        x2 = x1_ref[...] + gaf * ffn
        r = lax.rsqrt(jnp.mean(x2 * x2, axis=1, keepdims=True) + EPS)
        xh = x2 * r
        g = g_ref[...]
        e = xh * g - t_ref[...]
        s_loss[...] += 0.5 * jnp.sum(jnp.mean(e * e, axis=1, keepdims=True), axis=0, keepdims=True)
        dy = e * (1.0 / D)
        gdy = dy * g
        dx2 = r * (gdy - xh * jnp.mean(gdy * xh, axis=1, keepdims=True))
        s_dg[...] += _colsum8(dy * xh)
        s_dga[...] += _colsum8(dx2 * ffn)
        dx2_ref[...] = dx2
        dffn_ref[...] = (dx2 * gaf).astype(BF16)

        @pl.when(i == pl.num_programs(0) - 1)
        def _():
            dg_ref[...] = jnp.sum(s_dg[...], axis=0, keepdims=True)
            dga_ref[...] = jnp.sum(s_dga[...], axis=0, keepdims=True)
            loss_ref[...] = jnp.broadcast_to(s_loss[...], (1, LANES))

    row = pl.BlockSpec((tm, D), lambda i: (i, 0))
    vec = pl.BlockSpec((1, D), lambda i: (0, 0))
    return pl.pallas_call(
        body, name="ffn_down_loss", grid=(S // tm,),
        in_specs=[pl.BlockSpec((tm, FFP), lambda i: (i, 0)), pl.BlockSpec((FFP, D), lambda i: (0, 0)), row,
                  pl.BlockSpec((8, D), lambda i: (0, 0)), vec, row],
        out_specs=[row, row, vec, vec, pl.BlockSpec((1, LANES), lambda i: (0, 0))],
        out_shape=[_sds((S, D), F32), _sds((S, D), BF16), _sds((1, D), F32), _sds((1, D), F32), _sds((1, LANES), F32)],
        scratch_shapes=[pltpu.VMEM((8, D), F32), pltpu.VMEM((8, D), F32), pltpu.VMEM((1, 1), F32)],
        compiler_params=_params(VMEM_MB),
    )(z, w_down, x1, mod, g_final, tgt)


def ffn_down_bwd(dffn, w_down, a, u, z):
    tm, tn = 1024, 768

    def body(d_ref, w_ref, a_ref, u_ref, z_ref, da_ref, du_ref, dw_ref):
        i = pl.program_id(1)
        dff = d_ref[...]
        dz = lax.dot_general(dff, w_ref[...], (((1,), (1,)), ((), ())), preferred_element_type=F32)
        av, uv = a_ref[...], u_ref[...]
        sg = _sigmoid(av)
        du_ref[...] = (dz * (av * sg)).astype(BF16)
        da_ref[...] = (dz * uv * (sg * (1.0 + av * (1.0 - sg)))).astype(BF16)
        dw = lax.dot_general(z_ref[...], dff, (((0,), (0,)), ((), ())), preferred_element_type=F32)

        @pl.when(i == 0)
        def _():
            dw_ref[...] = dw

        @pl.when(i > 0)
        def _():
            dw_ref[...] += dw

    tile = pl.BlockSpec((tm, tn), lambda j, i: (i, j))
    return pl.pallas_call(
        body, name="ffn_down_bwd", grid=(FFP // tn, S // tm),
        in_specs=[pl.BlockSpec((tm, D), lambda j, i: (i, 0)), pl.BlockSpec((tn, D), lambda j, i: (j, 0)), tile, tile, tile],
        out_specs=[tile, tile, pl.BlockSpec((tn, D), lambda j, i: (j, 0))],
        out_shape=[_sds((S, FFP), BF16), _sds((S, FFP), BF16), _sds((FFP, D), F32)], compiler_params=_params(VMEM_MB),
    )(dffn, w_down, a, u, z)


def mm_nt(dy, w, name, comm=None):
    tm = 1024
    n = dy.shape[1]
    if w.ndim == 2:
        k_in, tk = w.shape[0], 768
        w_spec = pl.BlockSpec((k_in, tk), lambda i, k: (0, k))
        dims = (((1,), (1,)), ((), ()))
    else:
        k_in, tk = w.shape[2], FF_PAD
        w_spec = pl.BlockSpec((None, tk, k_in), lambda i, k: (k, 0, 0))
        dims = (((1,), (0,)), ((), ()))
    nk = n // tk

    def body(d_ref, w_ref, o_ref, acc):
        k = pl.program_id(1)
        part = lax.dot_general(d_ref[...], w_ref[...], dims, preferred_element_type=F32)

        @pl.when(k == 0)
        def _():
            acc[...] = part

        @pl.when(k > 0)
        def _():
            acc[...] += part

        @pl.when(k == nk - 1)
        def _():
            o_ref[...] = acc[...]

    return _hosted_call(
        body, comm, (dy, w), name=name, grid=(S // tm, nk),
        in_specs=[pl.BlockSpec((tm, tk), lambda i, k: (i, k)), w_spec],
        out_specs=pl.BlockSpec((tm, k_in), lambda i, k: (i, 0)),
        out_shape=_sds((S, k_in), F32),
        scratch_shapes=[pltpu.VMEM((tm, k_in), F32)], vmem_mb=VMEM_MB)


def mm_tn(h, dy, name, shard_major=False, comm=None):
    tm, tn = 2048, 768
    k_in, n = h.shape[1], dy.shape[1]

    def body(h_ref, d_ref, o_ref):
        i = pl.program_id(1)
        ops = (d_ref[...], h_ref[...]) if shard_major else (h_ref[...], d_ref[...])
        dw = lax.dot_general(*ops, (((0,), (0,)), ((), ())), preferred_element_type=F32)

        @pl.when(i == 0)
        def _():
            o_ref[...] = dw

        @pl.when(i > 0)
        def _():
            o_ref[...] += dw

    if shard_major:
        out_spec, out_shape = pl.BlockSpec((None, tn, k_in), lambda j, i: (j, 0, 0)), _sds((n // tn, tn, k_in), F32)
    else:
        out_spec, out_shape = pl.BlockSpec((k_in, tn), lambda j, i: (0, j)), _sds((k_in, n), F32)
    return _hosted_call(
        body, comm, (h, dy), name=name, grid=(n // tn, S // tm),
        in_specs=[pl.BlockSpec((tm, k_in), lambda j, i: (i, 0)), pl.BlockSpec((tm, tn), lambda j, i: (i, j))],
        out_specs=out_spec, out_shape=out_shape, vmem_mb=VMEM_MB)


def mid_bwd(dh2a, dh2b, x1, dx2, mix, mod, g_ffn, p, ya, ybp, merged, ya_att, yb, w_out, w_bra, w_brb, comm=None):
    tm = 256
    gblk = LAY_G // D
    nsteps = S // tm

    def body(dha_ref, dhb_ref, x1_ref, dx2_ref, mix_ref, mod_ref, g_ref, ga_ref, gb_ref, ya_ref, yb_ref, mg_ref,
             att_ref, ybb_ref, wo_ref, wa_ref, wb_ref,
             dx1_ref, dpg_ref, datt_ref, dyb_ref, cs_ref, dwo_ref, dwa_ref, dwb_ref, s_cs):
        i = pl.program_id(0)

        @pl.when(i == 0)
        def _():
            s_cs[...] = jnp.zeros_like(s_cs)
            dwo_ref[...] = jnp.zeros_like(dwo_ref)
            dwa_ref[...] = jnp.zeros_like(dwa_ref)
            dwb_ref[...] = jnp.zeros_like(dwb_ref)

        x1 = x1_ref[...]
        g = g_ref[...]
        r = lax.rsqrt(jnp.mean(x1 * x1, axis=1, keepdims=True) + EPS)
        xh = x1 * r
        dh2 = dha_ref[...] + dhb_ref[...]
        s_cs[0] += _colsum8(dh2)
        s_cs[1] += _colsum8(dh2 * (xh * g))
        dn2 = dh2 * (1.0 + mod_ref[4:5, :])
        s_cs[2] += _colsum8(dn2 * xh)
        gd = dn2 * g
        dx1 = dx2_ref[...] + r * (gd - xh * jnp.mean(gd * xh, axis=1, keepdims=True))
        s_cs[3] += _colsum8(dx1 * mix_ref[...])
        dx1_ref[...] = dx1
        dmix = (dx1 * mod_ref[2:3, :]).astype(BF16)
        dmg = lax.dot_general(dmix, wo_ref[...], (((1,), (1,)), ((), ())), preferred_element_type=F32)
        sga, sgb = _sigmoid(ga_ref[...]), _sigmoid(gb_ref[...])
        dya = (dmg * sga).astype(BF16)
        dybp = (dmg * sgb).astype(BF16)
        dpg_ref[:, 0:D] = (dmg * ya_ref[...] * (sga * (1.0 - sga))).astype(BF16)
        dpg_ref[:, D:2 * D] = (dmg * yb_ref[...] * (sgb * (1.0 - sgb))).astype(BF16)
        datt_ref[...] = lax.dot_general(dya, wa_ref[...], (((1,), (1,)), ((), ())), preferred_element_type=F32).astype(BF16)
        dyb_ref[...] = lax.dot_general(dybp, wb_ref[...], (((1,), (1,)), ((), ())), preferred_element_type=F32)
        tn_dims = (((0,), (0,)), ((), ()))
        dwo_ref[...] += lax.dot_general(mg_ref[...], dmix, tn_dims, preferred_element_type=F32)
        dwa_ref[...] += lax.dot_general(att_ref[...], dya, tn_dims, preferred_element_type=F32)
        dwb_ref[...] += lax.dot_general(ybb_ref[...].astype(BF16), dybp, tn_dims, preferred_element_type=F32)

        @pl.when(i == nsteps - 1)
        def _():
            for t in range(4):
                cs_ref[t:t + 1, :] = jnp.sum(s_cs[t], axis=0, keepdims=True)
            cs_ref[4:8, :] = jnp.zeros((4, D), F32)

    row = lambda w: pl.BlockSpec((tm, w), lambda i: (i, 0))
    full = lambda a, b: pl.BlockSpec((a, b), lambda i: (0, 0))
    return _hosted_call(
        body, comm, (dh2a, dh2b, x1, dx2, mix, mod, g_ffn, p, p, ya, ybp, merged, ya_att, yb, w_out, w_bra, w_brb),
        name="mid_bwd", grid=(nsteps,),
        in_specs=[row(D), row(D), row(D), row(D), row(D), full(8, D), full(1, D),
                  pl.BlockSpec((tm, D), lambda i: (i, gblk)), pl.BlockSpec((tm, D), lambda i: (i, gblk + 1)),
                  row(D), row(D), row(D), row(512), row(256), full(D, D), full(512, D), full(256, D)],
        out_specs=[row(D), pl.BlockSpec((tm, 2 * D), lambda i: (i, LAY_G // (2 * D))), row(512), row(256), full(8, D),
                   full(D, D), full(512, D), full(256, D)],
        out_shape=[_sds((S, D), F32), _sds((S, LAY_N), BF16), _sds((S, 512), BF16), _sds((S, 256), F32), _sds((8, D), F32),
                   _sds((D, D), F32), _sds((512, D), F32), _sds((256, D), F32)],
        scratch_shapes=[pltpu.VMEM((4, 8, D), F32)],
        vmem_mb=VMEM_MB)


def fox_bwd(p, do, o, gcol, fcol, dp, comm=None):
    nq = S // FT
    nt = (((1,), (1,)), ((), ()))
    tn = (((0,), (0,)), ((), ()))

    def body(qkv_ref, do_ref, o_ref, g_ref, fc_ref, dp_in, dp_ref, df_ref, rs_ref, dq_s, qa_s, ka_s, dob_s, dl_s):
        del dp_in
        _fox_operands(qkv_ref, g_ref, fc_ref, qa_s, ka_s)
        masks = _head_masks(FT)
        lane = lax.broadcasted_iota(jnp.int32, (FT, LANES), 1)
        head0 = 2 * pl.program_id(0)
        causal = lax.broadcasted_iota(jnp.int32, (FT, FT), 1) <= lax.broadcasted_iota(jnp.int32, (FT, FT), 0)
        dq_s[...] = jnp.zeros_like(dq_s)
        rs_ref[...] = jnp.zeros_like(rs_ref)

        causal2 = jnp.concatenate([causal, causal], axis=0)

        def prep(i, _):
            r = pl.ds(pl.multiple_of(i * 256, 256), 256)
            m256 = _head_masks(256)
            dov, ov = do_ref[r, :].astype(F32), o_ref[r, :].astype(F32)
            for hh in range(2):
                dom = jnp.where(m256[hh], dov, 0.0)
                dob_s[hh, r, :] = dom.astype(BF16)
                dl_s[hh, r, :] = jnp.broadcast_to(jnp.sum(dom * ov, axis=1, keepdims=True), (256, LANES))
            return 0

        lax.fori_loop(0, S // 256, prep, 0)

        def stack(ref, q0, cols=slice(None)):
            return jnp.concatenate([ref[0, pl.ds(q0, FT), cols], ref[1, pl.ds(q0, FT), cols]], axis=0)

        def kloop(kb, _):
            k0 = pl.multiple_of(kb * FT, FT)
            k = qkv_ref[pl.ds(k0, FT), LANES:2 * LANES].astype(BF16)
            v = qkv_ref[pl.ds(k0, FT), 2 * LANES:3 * LANES].astype(BF16)
            ka = ka_s[pl.ds(k0, FT), :]

            def step(qi, carry, diagonal):
                dk, dv, df0, df1 = carry
                q0 = pl.multiple_of(qi * FT, FT)
                qa, dob = stack(qa_s, q0), stack(dob_s, q0)
                s = lax.dot_general(qa, ka, nt, preferred_element_type=F32)
                pr = jnp.exp(jnp.where(causal2, s, NEG)) if diagonal else jnp.exp(s)
                dpr = lax.dot_general(dob, v, nt, preferred_element_type=F32)
                ds = pr * (dpr - jnp.tile(stack(dl_s, q0), (1, FT // LANES)))
                dsb = ds.astype(BF16)
                dq = jnp.dot(dsb, k, preferred_element_type=F32) * SCALE
                dk = dk + lax.dot_general(dsb, qa[:, 0:LANES], tn, preferred_element_type=F32)
                dv = dv + lax.dot_general(pr.astype(BF16), dob, tn, preferred_element_type=F32)
                rsum = jnp.sum(ds, axis=1, keepdims=True)
                dq_s[pl.ds(q0, FT), :] += jnp.where(masks[0], dq[:FT], dq[FT:])
                rs_ref[pl.ds(q0, FT), :] += jnp.where(lane == head0, rsum[:FT], 0.0) + jnp.where(lane == head0 + 1, rsum[FT:], 0.0)
                return (dk, dv, df0 - jnp.sum(ds[:FT], axis=0, keepdims=True), df1 - jnp.sum(ds[FT:], axis=0, keepdims=True))

            z = jnp.zeros((FT, LANES), F32)
            z1 = jnp.zeros((1, FT), F32)
            carry = step(kb, (z, z, z1, z1), True)
            dk, dv, df0, df1 = lax.fori_loop(kb + 1, nq, lambda qi, cr: step(qi, cr, False), carry)
            dp_ref[pl.ds(k0, FT), LANES:2 * LANES] = dk.astype(BF16)
            dp_ref[pl.ds(k0, FT), 2 * LANES:3 * LANES] = dv.astype(BF16)
            df_ref[0:1, pl.ds(k0, FT)] = df0
            df_ref[1:2, pl.ds(k0, FT)] = df1
            return 0

        lax.fori_loop(0, S // FT, kloop, 0)
        dp_ref[:, 0:LANES] = dq_s[...].astype(BF16)

    a_blk = LAY_A // 384
    pair = pl.BlockSpec((S, LANES), lambda p_: (0, p_))
    heads = pl.BlockSpec((2, S, LANES), lambda p_: (p_, 0, 0))
    return _hosted_call(
        body, comm, (p, do, o, gcol, fcol, dp), name="fox_bwd", grid=(N_FOX_PAIRS,),
        in_specs=[pl.BlockSpec((S, 384), lambda p_: (0, a_blk + p_)), pair, pair, heads, heads, pl.BlockSpec(memory_space=pl.ANY)],
        out_specs=[pl.BlockSpec((S, 384), lambda p_: (0, a_blk + p_)), pl.BlockSpec((None, 2, S), lambda p_: (p_, 0, 0)),
                   pl.BlockSpec((None, S, LANES), lambda p_: (p_, 0, 0))],
        out_shape=[_sds((S, LAY_N), BF16), _sds((4, 2, S), F32), _sds((4, S, LANES), F32)],
        scratch_shapes=[pltpu.VMEM((S, LANES), F32), pltpu.VMEM((2, S, 2 * LANES), BF16), pltpu.VMEM((S, 2 * LANES), BF16),
                        pltpu.VMEM((2, S, LANES), BF16), pltpu.VMEM((2, S, LANES), F32)],
        aliases={5: 0}, vmem_mb=VMEM_MB)


def fgate_bwd(dfrow, dfcol, fraw, dp):
    def body(df_ref, dc_ref, f_ref, dp_in, dpf_ref, db_ref):
        del dp_in
        lane = lax.broadcasted_iota(jnp.int32, (8, S), 1)
        rsum = (dc_ref[0] + dc_ref[1]) + (dc_ref[2] + dc_ref[3])
        acc, sh = df_ref[...] + rsum.T[0:8, :], 1
        while sh < S:
            acc = acc + jnp.where(lane < S - sh, pltpu.roll(acc, S - sh, 1), 0.0)
            sh *= 2
        df = acc * _sigmoid(-f_ref[...])
        db_ref[...] = jnp.broadcast_to(jnp.sum(df, axis=1, keepdims=True), (8, LANES))
        dfc = jnp.concatenate([df, jnp.zeros((LANES - 8, S), F32)], axis=0).T
        dpf_ref[:, 0:LANES] = dfc.astype(BF16)
        dpf_ref[:, LANES:2 * LANES] = jnp.zeros((S, LANES), BF16)

    return pl.pallas_call(
        body, name="fgate_bwd", grid=(1,),
        in_specs=[pl.BlockSpec((8, S), lambda i: (0, 0)), pl.BlockSpec((4, S, LANES), lambda i: (0, 0, 0)),
                  pl.BlockSpec((8, S), lambda i: (0, 0)), pl.BlockSpec(memory_space=pl.ANY)],
        out_specs=[pl.BlockSpec((S, 2 * LANES), lambda i: (0, LAY_F // (2 * LANES))), pl.BlockSpec((8, LANES), lambda i: (0, 0))],
        out_shape=[_sds((S, LAY_N), BF16), _sds((8, LANES), F32)],
        input_output_aliases={3: 0},
        compiler_params=_params(VMEM_MB),
    )(dfrow, dfcol, fraw, dp)


def dil_bwd(p, dyb, yb, lse, tabs, dp, comm=None):
    def body(*refs):
        qkv = [refs[3 * g:3 * g + 3] for g in range(3)]
        dy_ref, y_ref, lse_ref, c_ref, s1_ref, s2_ref = refs[9:15]
        dp_ref = refs[16]
        dq_s, dk_s, dv_s, dl_s = refs[17:21]
        masks = _head_masks(SPAN)
        m256 = _head_masks(256)
        nt = (((1,), (1,)), ((), ()))
        tn = (((0,), (0,)), ((), ()))
        dk_s[...] = jnp.zeros_like(dk_s)
        dv_s[...] = jnp.zeros_like(dv_s)

        def prep(i, _):
            rows = pl.ds(pl.multiple_of(i * 256, 256), 256)
            pr = dy_ref[rows, :] * y_ref[rows, :]
            d0 = jnp.sum(jnp.where(m256[0], pr, 0.0), axis=1, keepdims=True)
            d1 = jnp.sum(jnp.where(m256[1], pr, 0.0), axis=1, keepdims=True)
            dl_s[rows, :] = jnp.where(m256[0], d0, d1)
            return 0

        lax.fori_loop(0, S // 256, prep, 0)

        for g, (d, nb) in enumerate(DIL_GROUPS):
            q_ref, k_ref, v_ref = qkv[g]

            def blk(n, _):
                start, prev, has_prev = _dil_block(n, d, nb)
                q = _dil_rows(q_ref, start, d)
                kc = jnp.concatenate([_dil_rows(k_ref, prev, d), _dil_rows(k_ref, start, d)], axis=0).astype(BF16)
                vc = jnp.concatenate([_dil_rows(v_ref, prev, d), _dil_rows(v_ref, start, d)], axis=0).astype(BF16)
                dov = _dil_rows(dy_ref, start, d)
                lsev = _dil_rows(lse_ref, start, d)
                dlv = _dil_rows(dl_s, start, d)
                valid = _band_mask(has_prev)
                valid2 = jnp.concatenate([valid, valid], axis=0)

                def stack(t):
                    return jnp.concatenate([jnp.where(masks[0], t, 0.0), jnp.where(masks[1], t, 0.0)], axis=0)

                def column(t):
                    return jnp.concatenate([jnp.max(jnp.where(masks[hh], t, NEG), axis=1, keepdims=True) for hh in range(2)], axis=0)

                q2 = (stack(q) * SCALE).astype(BF16)
                dob = stack(dov).astype(BF16)
                s = jnp.where(valid2, lax.dot_general(q2, kc, nt, preferred_element_type=F32), NEG)
                pr = jnp.exp(s - column(lsev))
                dpr = lax.dot_general(dob, vc, nt, preferred_element_type=F32)
                dsb = (pr * (dpr - column(dlv))).astype(BF16)
                dq = jnp.dot(dsb, kc, preferred_element_type=F32) * SCALE
                dkc = lax.dot_general(dsb, q2, tn, preferred_element_type=F32)
                dvc = lax.dot_general(pr.astype(BF16), dob, tn, preferred_element_type=F32)
                _dil_store(dq_s.at[g], start, d, jnp.where(masks[0], dq[:SPAN], dq[SPAN:]))
                for ref, val in ((dk_s.at[g], dkc), (dv_s.at[g], dvc)):
                    _dil_store(ref, prev, d, _dil_rows(ref, prev, d) + jnp.where(has_prev, val[0:SPAN], 0.0))
                    _dil_store(ref, start, d, _dil_rows(ref, start, d) + val[SPAN:])
                return 0

            lax.fori_loop(0, 16, blk, 0)

        def fin(i, _):
            rows = pl.ds(pl.multiple_of(i * 256, 256), 256)
            c, s1, s2 = c_ref[rows, :], s1_ref[rows, :], s2_ref[rows, :]
            for g in range(3):
                base = g * 384
                dp_ref[rows, base:base + LANES] = _rope_bwd(dq_s[g, rows, :], c, s1, s2).astype(BF16)
                dp_ref[rows, base + LANES:base + 2 * LANES] = _rope_bwd(dk_s[g, rows, :], c, s1, s2).astype(BF16)
                dp_ref[rows, base + 2 * LANES:base + 3 * LANES] = dv_s[g, rows, :].astype(BF16)
            return 0

        lax.fori_loop(0, S // 256, fin, 0)

    def spec(g, t):
        return pl.BlockSpec((S, LANES), lambda p_: (0, (p_ * 3 + g) * 3 + t))

    pair = pl.BlockSpec((S, LANES), lambda p_: (0, p_))
    tab = pl.BlockSpec((S, LANES), lambda p_: (0, 0))
    return _hosted_call(
        body, comm, [p] * 9 + [dyb, yb, lse, *tabs, dp], name="dil_bwd", grid=(N_DIL_PAIRS,),
        in_specs=[spec(g, t) for g in range(3) for t in range(3)] + [pair, pair, pair, tab, tab, tab, pl.BlockSpec(memory_space=pl.ANY)],
        out_specs=pl.BlockSpec((S, 1152), lambda p_: (0, p_)),
        out_shape=_sds((S, LAY_N), BF16),
        scratch_shapes=[pltpu.VMEM((3, S, LANES), F32)] * 3 + [pltpu.VMEM((S, LANES), F32)],
        aliases={15: 0}, vmem_mb=VMEM_MB)


def in_bwd_tail(dh1, x, dx1, mod, g_mix, comm=None):
    tm = 256
    nsteps = S // tm

    def body(dh_ref, x_ref, dx1_ref, mod_ref, g_ref, dx_ref, cs_ref, s_cs):
        i = pl.program_id(0)

        @pl.when(i == 0)
        def _():
            s_cs[...] = jnp.zeros_like(s_cs)

        xv, g, dh = x_ref[...], g_ref[...], dh_ref[...]
        r = lax.rsqrt(jnp.mean(xv * xv, axis=1, keepdims=True) + EPS)
        xh = xv * r
        s_cs[0] += _colsum8(dh)
        s_cs[1] += _colsum8(dh * (xh * g))
        dn = dh * (1.0 + mod_ref[1:2, :])
        s_cs[2] += _colsum8(dn * xh)
        gd = dn * g
        dx_ref[...] = dx1_ref[...] + r * (gd - xh * jnp.mean(gd * xh, axis=1, keepdims=True))

        @pl.when(i == nsteps - 1)
        def _():
            for t in range(3):
                cs_ref[t:t + 1, :] = jnp.sum(s_cs[t], axis=0, keepdims=True)
            cs_ref[3:8, :] = jnp.zeros((5, D), F32)

    row = pl.BlockSpec((tm, D), lambda i: (i, 0))
    return _hosted_call(
        body, comm, (dh1, x, dx1, mod, g_mix), name="in_bwd_tail", grid=(nsteps,),
        in_specs=[row, row, row, pl.BlockSpec((8, D), lambda i: (0, 0)), pl.BlockSpec((1, D), lambda i: (0, 0))],
        out_specs=[row, pl.BlockSpec((8, D), lambda i: (0, 0))],
        out_shape=[_sds((S, D), F32), _sds((8, D), F32)],
        scratch_shapes=[pltpu.VMEM((3, 8, D), F32)])


def _lay_pieces():
    out = []
    qa, ka, va, fa, qb, kb, vb, ga = 0, 512, 1024, 1536, 1544, 2312, 3080, 3848
    for p in range(N_DIL_PAIRS):
        for g in range(3):
            base = LAY_B + (p * 3 + g) * 384
            hd0 = (4 * g + 2 * p) * HD
            out += [(base, qb + hd0, LANES), (base + LANES, kb + hd0, LANES), (base + 2 * LANES, vb + hd0, LANES)]
    for p in range(N_FOX_PAIRS):
        base = LAY_A + p * 384
        out += [(base, qa + p * LANES, LANES), (base + LANES, ka + p * LANES, LANES), (base + 2 * LANES, va + p * LANES, LANES)]
    out.append((LAY_F, fa, 8))
    out.append((LAY_G, ga, 2 * D))
    return out


def lay_from_nat(w_nat):
    parts, pos = [], 0
    for lay, nat, width in sorted(_lay_pieces()):
        if lay > pos:
            parts.append(jnp.zeros((w_nat.shape[0], lay - pos), w_nat.dtype))
        parts.append(w_nat[:, nat:nat + width])
        pos = lay + width
    if pos < LAY_N:
        parts.append(jnp.zeros((w_nat.shape[0], LAY_N - pos), w_nat.dtype))
    return jnp.concatenate(parts, axis=1)


def nat_from_lay(w_lay):
    parts = [w_lay[:, lay:lay + width] for lay, nat, width in sorted(_lay_pieces(), key=lambda t: t[1])]
    return jnp.concatenate(parts, axis=1)


def _shard_runs():
    runs = []
    for lay, nat, width in _lay_pieces():
        while width:
            k, loc = nat // IN_SHARD, nat % IN_SHARD
            w = min(width, IN_SHARD - loc)
            runs.append((lay, k, loc, w))
            lay, nat, width = lay + w, nat + w, width - w
    return runs


def lay_from_shards(g):
    tm = 256

    def body(g_ref, o_ref):
        o_ref[:, LAY_F:LAY_G] = jnp.zeros((tm, LAY_G - LAY_F), g.dtype)
        for lay, k, loc, w in _shard_runs():
            o_ref[:, lay:lay + w] = g_ref[k, :, loc:loc + w]

    return pl.pallas_call(
        body, name="lay_from_shards", grid=(D // tm,),
        in_specs=[pl.BlockSpec((4, tm, IN_SHARD_PAD), lambda i: (0, i, 0))],
        out_specs=pl.BlockSpec((tm, LAY_N), lambda i: (i, 0)),
        out_shape=_sds((D, LAY_N), g.dtype), compiler_params=_params(VMEM_MB),
    )(g)


def shards_from_lay(dw_lay):
    tm = 256

    def body(x_ref, o_ref):
        o_ref[:, :, IN_SHARD:] = jnp.zeros((4, tm, IN_SHARD_PAD - IN_SHARD), F32)
        for lay, k, loc, w in _shard_runs():
            o_ref[k, :, loc:loc + w] = x_ref[:, lay:lay + w]

    return pl.pallas_call(
        body, name="shards_from_lay", grid=(D // tm,),
        in_specs=[pl.BlockSpec((tm, LAY_N), lambda i: (i, 0))],
        out_specs=pl.BlockSpec((4, tm, IN_SHARD_PAD), lambda i: (0, i, 0)),
        out_shape=_sds((4, D, IN_SHARD_PAD), F32), compiler_params=_params(VMEM_MB),
    )(dw_lay)


def _pos():
    return lax.axis_index("x"), lax.axis_index("y"), lax.axis_index("c")


def _other_chips(x, y):
    return [(1 - x, y), (x, 1 - y), (1 - x, 1 - y)]


def _remote(src, dst, send_sem, recv_sem, dev):
    return pltpu.make_async_remote_copy(src_ref=src, dst_ref=dst, send_sem=send_sem, recv_sem=recv_sem,
                                        device_id=dev, device_id_type=MESH)


VMEM_SPEC = pl.BlockSpec(memory_space=pltpu.VMEM)
ANY_SPEC = pl.BlockSpec(memory_space=pl.ANY)


def gather_all(v, name, with_sum):
    r = v.shape[0]

    def body(v_ref, out_ref, *rest):
        send_s, recv_s = rest[-2:]
        x, y, c = _pos()
        me = 4 * x + 2 * y + c
        out_ref[me] = v_ref[...]
        peers = []
        for m in range(1, 8):
            px = 1 - x if m & 4 else x
            py = 1 - y if m & 2 else y
            pc = 1 - c if m & 1 else c
            peers.append((px, py, pc))
        copies = [_remote(v_ref, out_ref.at[me], send_s.at[i], recv_s.at[i], dev) for i, dev in enumerate(peers)]
        for cp in copies:
            cp.start()
        for i, (px, py, pc) in enumerate(peers):
            _remote(v_ref, out_ref.at[4 * px + 2 * py + pc], send_s.at[i], recv_s.at[i], (px, py, pc)).wait_recv()
        for cp in copies:
            cp.wait_send()
        if with_sum:
            acc = out_ref[0]
            for b in range(1, 8):
                acc = acc + out_ref[b]
            rest[0][...] = acc

    out_shape = [_sds((8, r, LANES), F32)] + ([_sds((r, LANES), F32)] if with_sum else [])
    return pl.pallas_call(
        body, name=name, in_specs=[VMEM_SPEC], out_specs=[VMEM_SPEC] * len(out_shape), out_shape=out_shape,
        scratch_shapes=[pltpu.SemaphoreType.DMA((7,)), pltpu.SemaphoreType.DMA((7,))],
    )(v)


def mod_exchange(c_all, w_ada_sh, b_sh):
    def body(c_ref, w_ref, b_ref, out_ref, sc_ref, modp, send_s, recv_s):
        cv = c_ref[...]
        sc = cv * _sigmoid(cv)
        sc_ref[...] = sc
        modp[...] = jnp.dot(sc, w_ref[...], precision=lax.Precision.HIGHEST, preferred_element_type=F32) + b_ref[...]
        x, y, c = _pos()
        k = 2 * x + y
        out_ref[k] = modp[...]
        chips = _other_chips(x, y)
        copies = [_remote(modp, out_ref.at[k], send_s.at[j], recv_s.at[j], (cx, cy, c)) for j, (cx, cy) in enumerate(chips)]
        for cp in copies:
            cp.start()
        for j, (cx, cy) in enumerate(chips):
            _remote(modp, out_ref.at[2 * cx + cy], send_s.at[j], recv_s.at[j], (cx, cy, c)).wait_recv()
        for cp in copies:
            cp.wait_send()

    n = w_ada_sh.shape[1]
    return pl.pallas_call(
        body, name="mod_exchange", in_specs=[VMEM_SPEC] * 3, out_specs=[VMEM_SPEC] * 2,
        out_shape=[_sds((4, 8, n), F32), _sds((8, D), F32)],
        scratch_shapes=[pltpu.VMEM((8, n), F32), pltpu.SemaphoreType.DMA((3,)), pltpu.SemaphoreType.DMA((3,))],
        compiler_params=_params(VMEM_MB),
    )(c_all, w_ada_sh, b_sh)


def gather_weights(bufs):
    n = len(bufs)

    def body(*refs):
        outs = refs[n:2 * n]
        send_s, recv_s, fsend_s, frecv_s = refs[2 * n:]
        x, y, c = _pos()
        k = 2 * x + y
        chips = _other_chips(x, y)
        sends, fwds = [], []
        for a in range(n):
            half = outs[a].shape[1] // 2
            rows = pl.ds(c * half, half)
            for j, (cx, cy) in enumerate(chips):
                cp = _remote(outs[a].at[k, rows], outs[a].at[k, rows], send_s.at[3 * a + j], recv_s.at[3 * a + j], (cx, cy, c))
                cp.start()
                sends.append(cp)
        for a in range(n):
            half = outs[a].shape[1] // 2
            rows = pl.ds(c * half, half)
            for j, (cx, cy) in enumerate(chips):
                kj = 2 * cx + cy
                _remote(outs[a].at[kj, rows], outs[a].at[kj, rows], send_s.at[3 * a + j], recv_s.at[3 * a + j], (cx, cy, c)).wait_recv()
                fw = _remote(outs[a].at[kj, rows], outs[a].at[kj, rows], fsend_s.at[3 * a + j], frecv_s.at[3 * a + j], (x, y, 1 - c))
                fw.start()
                fwds.append(fw)
        for a in range(n):
            half = outs[a].shape[1] // 2
            orows = pl.ds((1 - c) * half, half)
            for j, (cx, cy) in enumerate(chips):
                kj = 2 * cx + cy
                _remote(outs[a].at[kj, orows], outs[a].at[kj, orows], fsend_s.at[3 * a + j], frecv_s.at[3 * a + j], (x, y, 1 - c)).wait_recv()
        for cp in sends + fwds:
            cp.wait_send()

    return pl.pallas_call(
        body, name="gather_weights", in_specs=[ANY_SPEC] * n, out_specs=[ANY_SPEC] * n,
        out_shape=[_sds(b.shape, b.dtype) for b in bufs],
        scratch_shapes=[pltpu.SemaphoreType.DMA((3 * n,))] * 4,
        input_output_aliases={a: a for a in range(n)},
    )(*bufs)


def cast_into_slabs(ws, sizes, chip, comm):
    tr = 64
    n = len(ws)
    n_in = [w.shape[0] // tr for w in ws]
    n_out = [r // tr for r, _ in sizes]
    steps = max(n_out)
    nci, nco = len(comm.ins), len(comm.out_shapes)

    def body(chip_ref, *refs):
        del chip_ref
        w_refs, cin = refs[:n], refs[n:n + nci]
        o_refs, cout = refs[n + nci:2 * n + nci], refs[2 * n + nci:2 * n + nci + nco]
        sems = refs[2 * n + nci + nco:]
        i = pl.program_id(0)

        @pl.when(i == 0)
        def _():
            comm.start(cin, cout, sems)

        for a in range(n):
            c0, cols = ws[a].shape[1], sizes[a][1]

            @pl.when(i < n_in[a])
            def _(a=a, c0=c0, cols=cols):
                o_refs[a][:, 0:c0] = w_refs[a][...].astype(BF16)
                if cols > c0:
                    o_refs[a][:, c0:] = jnp.zeros((tr, cols - c0), BF16)

            if n_out[a] > n_in[a]:
                @pl.when((i >= n_in[a]) & (i < n_out[a]))
                def _(a=a, cols=cols):
                    o_refs[a][...] = jnp.zeros((tr, cols), BF16)

        @pl.when(i == steps - 1)
        def _():
            comm.wait(cin, cout, sems)

    any_spec = pl.BlockSpec(memory_space=pl.ANY)
    grid_spec = pltpu.PrefetchScalarGridSpec(
        num_scalar_prefetch=1, grid=(steps,),
        in_specs=[pl.BlockSpec((tr, w.shape[1]), functools.partial(lambda i, k, last: (jnp.minimum(i, last), 0), last=n_in[a] - 1))
                  for a, w in enumerate(ws)] + [any_spec] * nci,
        out_specs=[pl.BlockSpec((None, tr, sizes[a][1]), functools.partial(lambda i, k, last: (k[0], jnp.minimum(i, last), 0), last=n_out[a] - 1))
                   for a in range(n)] + [any_spec] * nco,
        scratch_shapes=[pltpu.SemaphoreType.DMA((s,)) for s in comm.sems])
    res = pl.pallas_call(
        body, name="cast_into_slabs", grid_spec=grid_spec,
        out_shape=[_sds((4,) + tuple(sz), BF16) for sz in sizes] + comm.out_shapes,
        input_output_aliases={1 + n + ci: n + co for ci, co in comm.aliases.items()},
    )(chip, *ws, *comm.ins)
    return list(res[:n]), list(res[n:])


def cast_into_slab(w, rows, cols, chip, tag):
    r0, c0 = w.shape
    tr = 256 if (r0 % 256 == 0 and rows % 256 == 0) else 64
    n_in, n_out = r0 // tr, rows // tr

    def body(chip_ref, w_ref, o_ref):
        del chip_ref
        i = pl.program_id(0)

        @pl.when(i < n_in)
        def _():
            o_ref[:, 0:c0] = w_ref[...].astype(BF16)
            if cols > c0:
                o_ref[:, c0:] = jnp.zeros((tr, cols - c0), BF16)

        @pl.when(i >= n_in)
        def _():
            o_ref[...] = jnp.zeros((tr, cols), BF16)

    grid_spec = pltpu.PrefetchScalarGridSpec(
        num_scalar_prefetch=1, grid=(n_out,),
        in_specs=[pl.BlockSpec((tr, c0), lambda i, k: (jnp.minimum(i, n_in - 1), 0))],
        out_specs=pl.BlockSpec((None, tr, cols), lambda i, k: (k[0], i, 0)))
    return pl.pallas_call(body, name="cast_" + tag, grid_spec=grid_spec, out_shape=_sds((4, rows, cols), BF16))(chip, w)


def _row_tile(rows, cap=256):
    t = cap
    while rows % t or t % 8:
        t -= 8
    return t


def _comm_wait(sends, recvs, local=()):
    for cp in recvs:
        cp.wait_recv()
    for cp in sends:
        cp.wait_send()
    for cp in local:
        cp.wait()


def ag_ici(bufs):
    n = len(bufs)

    def copies(ins, outs, sems):
        send_s, recv_s = sems
        x, y, c = _pos()
        k = 2 * x + y
        sends, recvs = [], []
        for a in range(n):
            half = outs[a].shape[1] // 2
            rows = pl.ds(c * half, half)
            for j, (cx, cy) in enumerate(_other_chips(x, y)):
                sem = (send_s.at[3 * a + j], recv_s.at[3 * a + j], (cx, cy, c))
                sends.append(_remote(outs[a].at[k, rows], outs[a].at[k, rows], *sem))
                recvs.append(_remote(outs[a].at[k, rows], outs[a].at[2 * cx + cy, rows], *sem))
        return sends, recvs

    def start(ins, outs, sems):
        for cp in copies(ins, outs, sems)[0]:
            cp.start()

    def wait(ins, outs, sems):
        _comm_wait(*copies(ins, outs, sems))

    return Comm(bufs, [_sds(b.shape, b.dtype) for b in bufs], [3 * n, 3 * n], start, wait, aliases={a: a for a in range(n)})


def ag_d2d(bufs):
    n = len(bufs)

    def copies(ins, outs, sems):
        send_s, recv_s = sems
        x, y, c = _pos()
        sends, recvs = [], []
        for a in range(n):
            half = outs[a].shape[1] // 2
            rows, orows = pl.ds(c * half, half), pl.ds((1 - c) * half, half)
            for j, (cx, cy) in enumerate(_other_chips(x, y)):
                kj = 2 * cx + cy
                sem = (send_s.at[3 * a + j], recv_s.at[3 * a + j], (x, y, 1 - c))
                sends.append(_remote(outs[a].at[kj, rows], outs[a].at[kj, rows], *sem))
                recvs.append(_remote(outs[a].at[kj, orows], outs[a].at[kj, orows], *sem))
        return sends, recvs

    def start(ins, outs, sems):
        for cp in copies(ins, outs, sems)[0]:
            cp.start()

    def wait(ins, outs, sems):
        _comm_wait(*copies(ins, outs, sems))

    return Comm(bufs, [_sds(b.shape, b.dtype) for b in bufs], [3 * n, 3 * n], start, wait, aliases={a: a for a in range(n)})


def rs_a(grads):
    n = len(grads)

    def copies(ins, outs, sems):
        send_s, recv_s = sems
        x, y, c = _pos()
        cps = []
        for a in range(n):
            half = ins[a].shape[1] // 2
            cps.append(_remote(ins[a].at[:, pl.ds((1 - c) * half, half), :], outs[a], send_s.at[a], recv_s.at[a], (x, y, 1 - c)))
        return cps

    def start(ins, outs, sems):
        for cp in copies(ins, outs, sems):
            cp.start()

    def wait(ins, outs, sems):
        cps = copies(ins, outs, sems)
        _comm_wait(cps, cps)

    return Comm(grads, [_sds((4, g.shape[1] // 2, g.shape[2]), g.dtype) for g in grads], [n, n], start, wait)


def rs_b(pres):
    n = len(pres)

    def copies(ins, outs, sems):
        send_s, recv_s = sems
        x, y, c = _pos()
        cps = []
        for a in range(n):
            for j, (cx, cy) in enumerate(_other_chips(x, y)):
                cps.append(_remote(ins[a].at[2 * cx + cy], outs[a].at[j], send_s.at[3 * a + j], recv_s.at[3 * a + j], (cx, cy, c)))
        return cps

    def start(ins, outs, sems):
        for cp in copies(ins, outs, sems):
            cp.start()

    def wait(ins, outs, sems):
        cps = copies(ins, outs, sems)
        _comm_wait(cps, cps)

    return Comm(pres, [_sds((3,) + p_.shape[1:], p_.dtype) for p_ in pres], [3 * n, 3 * n], start, wait)


def rs_b_rows(pre, buf, lo, n):
    def copies(ins, outs, sems):
        send_s, recv_s = sems
        x, y, c = _pos()
        rows = pl.ds(lo, n)
        return [_remote(ins[0].at[2 * cx + cy, rows], outs[0].at[j, rows], send_s.at[j], recv_s.at[j], (cx, cy, c))
                for j, (cx, cy) in enumerate(_other_chips(x, y))]

    def start(ins, outs, sems):
        for cp in copies(ins, outs, sems):
            cp.start()

    def wait(ins, outs, sems):
        cps = copies(ins, outs, sems)
        _comm_wait(cps, cps)

    ins = [pre] if buf is None else [pre, buf]
    return Comm(ins, [_sds((3,) + pre.shape[1:], pre.dtype)], [3, 3], start, wait, aliases={} if buf is None else {1: 0})


def rs_c(reds):
    n = len(reds)

    def copies(ins, outs, sems):
        send_s, recv_s = sems
        x, y, c = _pos()
        sends, recvs = [], []
        for a in range(n):
            half = outs[a].shape[0] // 2
            rows, orows = pl.ds(c * half, half), pl.ds((1 - c) * half, half)
            sem = (send_s.at[a], recv_s.at[a], (x, y, 1 - c))
            sends.append(_remote(outs[a].at[rows], outs[a].at[rows], *sem))
            recvs.append(_remote(outs[a].at[orows], outs[a].at[orows], *sem))
        return sends, recvs

    def start(ins, outs, sems):
        for cp in copies(ins, outs, sems)[0]:
            cp.start()

    def wait(ins, outs, sems):
        _comm_wait(*copies(ins, outs, sems))

    return Comm(reds, [_sds(r_.shape, r_.dtype) for r_ in reds], [n, n], start, wait, aliases={a: a for a in range(n)})


def comm_join(*comms):
    ni = np.cumsum([0] + [len(c.ins) for c in comms])
    no = np.cumsum([0] + [len(c.out_shapes) for c in comms])
    ns = np.cumsum([0] + [len(c.sems) for c in comms])

    def parts(ins, outs, sems):
        return [(c, ins[ni[i]:ni[i + 1]], outs[no[i]:no[i + 1]], sems[ns[i]:ns[i + 1]]) for i, c in enumerate(comms)]

    def start(ins, outs, sems):
        for c, a, b, s in parts(ins, outs, sems):
            c.start(a, b, s)

    def wait(ins, outs, sems):
        for c, a, b, s in parts(ins, outs, sems):
            c.wait(a, b, s)

    aliases = {int(ni[i]) + k: int(no[i]) + v for i, c in enumerate(comms) for k, v in c.aliases.items()}
    return Comm(sum((c.ins for c in comms), []), sum((c.out_shapes for c in comms), []), sum((c.sems for c in comms), []),
                start, wait, aliases)


def comm_only(comm, name):
    nci, nco = len(comm.ins), len(comm.out_shapes)

    def body(*refs):
        ins, outs, sems = refs[:nci], refs[nci:nci + nco], refs[nci + nco:]
        comm.start(ins, outs, sems)
        comm.wait(ins, outs, sems)

    return pl.pallas_call(
        body, name=name, in_specs=[ANY_SPEC] * nci, out_specs=[ANY_SPEC] * nco, out_shape=comm.out_shapes,
        scratch_shapes=[pltpu.SemaphoreType.DMA((s,)) for s in comm.sems],
        input_output_aliases=comm.aliases,
    )(*comm.ins)


def rs_add_halves(g, other, core, name):
    _, r, cdim = g.shape
    half = r // 2
    tr = _row_tile(half, 256)
    nb = half // tr

    def body(core_ref, g_ref, o_ref, out_ref):
        del core_ref
        out_ref[...] = (g_ref[...] + o_ref[...]).astype(BF16)

    grid_spec = pltpu.PrefetchScalarGridSpec(
        num_scalar_prefetch=1, grid=(4, nb),
        in_specs=[pl.BlockSpec((None, tr, cdim), lambda k, i, cr: (k, cr[0] * nb + i, 0)),
                  pl.BlockSpec((None, tr, cdim), lambda k, i, cr: (k, i, 0))],
        out_specs=pl.BlockSpec((None, tr, cdim), lambda k, i, cr: (k, i, 0)))
    return pl.pallas_call(body, name=name, grid_spec=grid_spec, out_shape=_sds((4, half, cdim), BF16))(core, g, other)


def rs_add_slabs(t, pre, place, name):
    _, half, cdim = t.shape
    tr = _row_tile(half, 256)
    nb = half // tr

    def body(place_ref, own_ref, t_ref, out_ref):
        del place_ref
        out_ref[...] = ((own_ref[...].astype(F32) + t_ref[0].astype(F32)) + t_ref[1].astype(F32)) + t_ref[2].astype(F32)

    grid_spec = pltpu.PrefetchScalarGridSpec(
        num_scalar_prefetch=1, grid=(nb,),
        in_specs=[pl.BlockSpec((None, tr, cdim), lambda i, pr: (pr[0], i, 0)), pl.BlockSpec((3, tr, cdim), lambda i, pr: (0, i, 0))],
        out_specs=pl.BlockSpec((tr, cdim), lambda i, pr: (pr[1] * nb + i, 0)))
    return pl.pallas_call(body, name=name, grid_spec=grid_spec, out_shape=_sds((2 * half, cdim), F32))(place, pre, t)


def _adam_math(w, g, m, v):
    m = ADAM_B1 * m + (1.0 - ADAM_B1) * g
    v = ADAM_B2 * v + (1.0 - ADAM_B2) * (g * g)
    m_hat = m / (1.0 - ADAM_B1 ** ADAM_STEP)
    v_hat = v / (1.0 - ADAM_B2 ** ADAM_STEP)
    delta = -ADAM_LR * (m_hat / (jnp.sqrt(v_hat) + ADAM_EPS) + ADAM_WD * w)
    return delta, m, v


def adam(w, g, m, v, name, comm=None):
    r, cdim = w.shape
    tr = _row_tile(r) if r >= 8 else r

    def body(w_ref, g_ref, m_ref, v_ref, g_out, d_ref, nm_ref, nv_ref):
        gv = g_ref[:, :cdim]
        g_out[...] = gv
        d_ref[...], nm_ref[...], nv_ref[...] = _adam_math(w_ref[...], gv, m_ref[...], v_ref[...])

    blk = pl.BlockSpec((tr, cdim), lambda i: (i, 0))
    return _hosted_call(
        body, comm, (w, g, m, v), name=name, grid=(r // tr,),
        in_specs=[blk, pl.BlockSpec((tr, g.shape[1]), lambda i: (i, 0)), blk, blk],
        out_specs=[blk] * 4, out_shape=[_sds((r, cdim), F32)] * 4)


def adam_small(groups):
    n = len(groups)

    def body(*refs):
        ins, outs = refs[:4 * n], refs[4 * n:]
        for i in range(n):
            w_ref, m_ref, v_ref, g_ref = ins[4 * i:4 * i + 4]
            d_ref, nm_ref, nv_ref = outs[3 * i:3 * i + 3]
            d_ref[...], nm_ref[...], nv_ref[...] = _adam_math(w_ref[...], g_ref[...], m_ref[...], v_ref[...])

    flat = [t for grp in groups for t in grp]
    out = pl.pallas_call(
        body, name="adam_small", in_specs=[VMEM_SPEC] * (4 * n), out_specs=[VMEM_SPEC] * (3 * n),
        out_shape=[_sds(grp[0].shape, F32) for grp in groups for _ in range(3)],
    )(*flat)
    return [out[3 * i:3 * i + 3] for i in range(n)]


def adam_w_ada(sc_t, dmod_sh, w, m, v, comm=None):
    r, cdim = w.shape
    tr = 256

    def body(s_ref, d_ref, w_ref, m_ref, v_ref, g_ref, dl_ref, nm_ref, nv_ref):
        g = jnp.dot(s_ref[...], d_ref[...], precision=lax.Precision.HIGHEST, preferred_element_type=F32)
        g_ref[...] = g
        dl_ref[...], nm_ref[...], nv_ref[...] = _adam_math(w_ref[...], g, m_ref[...], v_ref[...])

    blk = pl.BlockSpec((tr, cdim), lambda i: (i, 0))
    return _hosted_call(
        body, comm, (sc_t, dmod_sh, w, m, v), name="adam_w_ada", grid=(r // tr,),
        in_specs=[pl.BlockSpec((tr, LANES), lambda i: (i, 0)), pl.BlockSpec((LANES, cdim), lambda i: (0, 0)), blk, blk, blk],
        out_specs=[blk] * 4, out_shape=[_sds((r, cdim), F32)] * 4)


SMALL_ROWS = 80


def kernel(x, c, w_ada, b_ada, g_mix, w_in, b_fgate, w_br_a, w_br_b, w_out, g_ffn, w_ffn_gate, w_ffn_up, w_ffn_down, g_final, loss_target, m_w_ada, m_b_ada, m_g_mix, m_w_in, m_b_fgate, m_w_br_a, m_w_br_b, m_w_out, m_g_ffn, m_w_ffn_gate, m_w_ffn_up, m_w_ffn_down, m_g_final, v_w_ada, v_b_ada, v_g_mix, v_w_in, v_b_fgate, v_w_br_a, v_w_br_b, v_w_out, v_g_ffn, v_w_ffn_gate, v_w_ffn_up, v_w_ffn_down, v_g_final):
    xi, yi, ci = _pos()
    chip = 2 * xi + yi
    seq = 4 * xi + 2 * yi + ci
    n_ada = w_ada.shape[2]

    c_all = gather_all(c.reshape(8, LANES), "gather_c", False)[0].reshape(8, D)
    b_sh = lax.dynamic_slice(b_ada, (0, chip * n_ada), (1, n_ada))
    mod_all, sc = mod_exchange(c_all, w_ada[0], b_sh)
    mod = lax.dynamic_index_in_dim(mod_all, seq, axis=1, keepdims=False).reshape(6, D)
    mod8 = jnp.pad(mod, ((0, 2), (0, 0)))

    core = ci.astype(jnp.int32).reshape(1)
    chip1 = chip.astype(jnp.int32).reshape(1)
    place = jnp.stack([chip, ci]).astype(jnp.int32)
    s_in = cast_into_slab(w_in[0], D, IN_SHARD_PAD, chip1, "w_in")
    (s_bra, s_brb, s_out, s_gate, s_up, s_down), (g_in,) = cast_into_slabs(
        [w_br_a[0], w_br_b[0], w_out[0], w_ffn_gate[0].T, w_ffn_up[0].T, w_ffn_down[0]],
        [(512, 256), (256, 256), (256, D), (FF_PAD, D), (FF_PAD, D), (FF_PAD, D)], chip1, ag_ici([s_in]))
    xs, tgt, g_fin = x[0], loss_target[0], g_final.reshape(1, D)

    def halves(gs, others, tag):
        return [rs_add_halves(g, o, core, f"rs_{tag}_halves_{i}") for i, (g, o) in enumerate(zip(gs, others))]

    def slab_sums(ts, pres, tag):
        return [rs_add_slabs(t, pre, place, f"rs_{tag}_slabs_{i}") for i, (t, pre) in enumerate(zip(ts, pres))]

    tabs = rope_tables()
    h1, (g_in,) = norm_mod_fwd(xs, g_mix, mod8, 0, 1, comm=ag_d2d([g_in]))
    w_lay = lay_from_shards(g_in)
    p, mix_w = in_proj_fwd(h1, w_lay, tabs, comm=ag_ici([s_bra, s_brb, s_out]))
    frow, fraw, fcol = fgate_fwd(p, jnp.pad(b_fgate, ((0, 0), (0, LANES - 8))))
    (ya_att, gcol), res = fox_fwd(p, fcol, comm=comm_join(ag_d2d(mix_w), ag_ici([s_gate, s_up])))
    g_bra, g_brb, g_out = res[:3]
    (yb, lse_b), res = dil_fwd(p, comm=comm_join(ag_d2d(res[3:]), ag_ici([s_down])))
    w_gate, w_up = res[:2]
    w_bra = g_bra.transpose(1, 0, 2).reshape(512, D)
    w_brb = g_brb.transpose(1, 0, 2).reshape(256, D)
    w_o = g_out.reshape(D, D)
    (merged, ya, ybp), (g_down,) = merge_fwd(ya_att, yb, p, w_bra, w_brb, comm=ag_d2d(res[2:]))
    w_down = g_down.reshape(FFP, D)
    mix, x1, h2 = out_proj_fwd(merged, w_o, xs, mod8, g_ffn)
    a, u, z = ffn_up_fwd(h2, w_gate, w_up)
    dx2, dffn, dg_final, dga_f, loss_part = ffn_down_loss(z, w_down, x1, mod8, g_fin, tgt)

    da, du, dw_down = ffn_down_bwd(dffn, w_down, a, u, z)
    g_down = [dw_down.reshape(4, FF_PAD, D)]
    dh2a, oth = mm_nt(da, w_gate, "ffn_gate_dx", comm=rs_a(g_down))
    pre_down = halves(g_down, oth, "down")
    dh2b, _ = mm_nt(du, w_up, "ffn_up_dx")
    dw_gate, _ = mm_tn(h2, da, "ffn_gate_dw", shard_major=True)
    dw_up, _ = mm_tn(h2, du, "ffn_up_dw", shard_major=True)
    g_gu = [dw_gate, dw_up]
    (dx1, dp1, dya_att, dyb, cs_mid, dw_out, dw_bra, dw_brb), res = mid_bwd(
        dh2a, dh2b, x1, dx2, mix, mod8, g_ffn, p, ya, ybp, merged, ya_att, yb, w_o, w_bra, w_brb,
        comm=comm_join(rs_b(pre_down), rs_a(g_gu)))
    red_down = slab_sums(res[:1], pre_down, "down")
    pre_gu = halves(g_gu, res[1:], "gu")
    g_mix3 = [dw_bra.reshape(512, 4, 256).transpose(1, 0, 2), dw_brb.reshape(256, 4, 256).transpose(1, 0, 2), dw_out.reshape(4, 256, D)]
    (dp2, dfrow, dfcol), res = fox_bwd(p, dya_att, ya_att, gcol, fcol, dp1,
                                       comm=comm_join(rs_b(pre_gu), rs_c(red_down), rs_a(g_mix3)))
    red_gu = slab_sums(res[:2], pre_gu, "gu")
    r_down = res[2]
    pre_mix3 = halves(g_mix3, res[3:], "mix")
    dp3, db_fg = fgate_bwd(dfrow.reshape(8, S), dfcol, fraw, dp2)
    dp4, res = dil_bwd(p, dyb, yb, lse_b, tabs, dp3, comm=comm_join(rs_c(red_gu), rs_b(pre_mix3)))
    r_gate, r_up = res[:2]
    red_mix3 = slab_sums(res[2:], pre_mix3, "mix")
    dw_lay, (r_bra, r_brb, r_out) = mm_tn(h1, dp4, "in_proj_dw", comm=rs_c(red_mix3))
    g_in4 = [shards_from_lay(dw_lay)]
    dh1, oth = mm_nt(dp4, w_lay, "in_proj_dx", comm=rs_a(g_in4))
    (pre_in,) = halves(g_in4, oth, "in")
    qrows = pre_in.shape[1] // 4
    (dx, cs_in), (t_in,) = in_bwd_tail(dh1, xs, dx1, mod8, g_mix, comm=rs_b_rows(pre_in, None, 0, qrows))

    dmod = jnp.concatenate([cs_in[0:2], cs_mid[3:4], cs_mid[0:2], dga_f], axis=0)
    small = dict(dmod=dmod, dg_mix=cs_in[2:3], dg_ffn=cs_mid[2:3], dg_final=dg_final, db_fgate=db_fg[:, 0], loss=loss_part[0, 0])
    sv = jnp.concatenate([
        small["dmod"].reshape(48, LANES), small["dg_mix"].reshape(8, LANES), small["dg_ffn"].reshape(8, LANES),
        small["dg_final"].reshape(8, LANES), jnp.pad(small["db_fgate"], (0, LANES - 8)).reshape(1, LANES),
        jnp.broadcast_to(small["loss"], (1, LANES)), jnp.zeros((SMALL_ROWS - 74, LANES), F32)], axis=0)
    sv_all, sv_sum = gather_all(sv, "gather_small", True)
    loss = sv_sum[73, 0]
    g_small = dict(b_ada=sv_sum[0:48].reshape(1, 6 * D), g_mix=sv_sum[48:56].reshape(1, D), g_ffn=sv_sum[56:64].reshape(1, D),
                   g_final=sv_sum[64:72].reshape(D), b_fgate=sv_sum[72, 0:8].reshape(1, 8))

    dmod_all = lax.dynamic_slice(sv_all[:, 0:48, :].reshape(8, 6 * D), (0, chip * n_ada), (8, n_ada))
    (g_ada, d_ada, nm_ada, nv_ada), (t_in,) = adam_w_ada(
        jnp.pad(sc.T, ((0, 0), (0, LANES - 8))), jnp.pad(dmod_all, ((0, LANES - 8), (0, 0))), w_ada[0], m_w_ada[0], v_w_ada[0],
        comm=rs_b_rows(pre_in, t_in, qrows, 3 * qrows))

    big = dict(w_in=(w_in, m_w_in, v_w_in), w_br_a=(w_br_a, m_w_br_a, v_w_br_a), w_br_b=(w_br_b, m_w_br_b, v_w_br_b),
               w_out=(w_out, m_w_out, v_w_out), w_ffn_gate=(w_ffn_gate, m_w_ffn_gate, v_w_ffn_gate),
               w_ffn_up=(w_ffn_up, m_w_ffn_up, v_w_ffn_up), w_ffn_down=(w_ffn_down, m_w_ffn_down, v_w_ffn_down))
    gpad = dict(w_br_a=r_bra, w_br_b=r_brb, w_out=r_out, w_ffn_gate=r_gate, w_ffn_up=r_up, w_ffn_down=r_down)
    upd = {}
    for nm in ("w_ffn_gate", "w_ffn_up"):
        w, m, v = big[nm]
        upd[nm] = [t.T for t in adam(w[0].T, gpad[nm], m[0].T, v[0].T, "adam_" + nm)[0]]
    (gpad["w_in"],) = comm_only(rs_c(slab_sums([t_in], [pre_in], "in")), "rs_in_share")
    for nm, (w, m, v) in big.items():
        if nm not in upd:
            upd[nm] = adam(w[0], gpad[nm], m[0], v[0], "adam_" + nm)[0]

    small_names = ["g_mix", "g_ffn", "g_final", "b_ada", "b_fgate"]
    small_w = dict(g_mix=(g_mix, m_g_mix, v_g_mix), g_ffn=(g_ffn, m_g_ffn, v_g_ffn), g_final=(g_final, m_g_final, v_g_final),
                   b_ada=(b_ada, m_b_ada, v_b_ada), b_fgate=(b_fgate, m_b_fgate, v_b_fgate))
    row = lambda t: t.reshape(1, -1)
    res = adam_small([[row(t) for t in small_w[nm]] + [row(g_small[nm])] for nm in small_names])
    small_upd = [{nm: res[i][which].reshape(small_w[nm][0].shape) for i, nm in enumerate(small_names)} for which in range(3)]
    order =["w_ada", "b_ada", "g_mix", "w_in", "b_fgate", "w_br_a", "w_br_b", "w_out", "g_ffn", "w_ffn_gate", "w_ffn_up", "w_ffn_down", "g_final"]

    def leaf(nm, which):
        if nm == "w_ada":
            return (g_ada, d_ada, nm_ada, nv_ada)[which][None]
        if nm in big:
            return upd[nm][which][None]
        return g_small[nm] if which == 0 else small_upd[which - 1][nm]

    outs = [loss, dx[None]]
    for which in range(4):
        outs += [leaf(nm, which) for nm in order]
    return tuple(outs)
```

```python
import functools

import numpy as np
import jax
import jax.numpy as jnp
from jax import lax
from jax.experimental import pallas as pl
from jax.experimental.pallas import tpu as pltpu

F32, BF16 = jnp.float32, jnp.bfloat16
S, D = 2048, 1024
HD = 64
LANES = 128
N_FOX_PAIRS, N_DIL_PAIRS = 4, 2
DIL_GROUPS = ((1, 16), (4, 4), (16, 1))
SPAN = 128
ROT_DIM, ROPE_THETA = 16, 500000.0
D_FF, FF_SHARD, FF_PAD = 2816, 704, 768
FFP = 4 * FF_PAD
IN_COLS, IN_SHARD, IN_SHARD_PAD = 5896, 1474, 1536
LAY_B, LAY_A, LAY_F, LAY_G, LAY_N = 0, 2304, 3840, 4096, 6144
EPS, NEG = 1e-6, -1e30
SCALE = HD ** -0.5
ADAM_LR, ADAM_B1, ADAM_B2, ADAM_EPS, ADAM_WD, ADAM_STEP = 0.001, 0.9, 0.999, 1e-08, 0.01, 10
VMEM_MB = 56
MESH = pl.DeviceIdType.MESH


def _params(vmem_mb=None, **kw):
    if vmem_mb is not None:
        kw["vmem_limit_bytes"] = vmem_mb * 1024 * 1024
    return pltpu.CompilerParams(**kw)


def _sds(shape, dtype):
    return jax.ShapeDtypeStruct(shape, dtype)


def _sigmoid(x):
    return 1.0 / (1.0 + jnp.exp(-x))


def _colsum8(x):
    tm, n = x.shape
    return jnp.sum(x.reshape(tm // 8, 8, n), axis=0)


class Comm:
    def __init__(self, ins, out_shapes, sems, start, wait, aliases=None):
        self.ins, self.out_shapes, self.sems = list(ins), list(out_shapes), list(sems)
        self.start, self.wait, self.aliases = start, wait, dict(aliases or {})


def _hosted_call(body, comm, args, *, name, grid, in_specs, out_specs, out_shape, scratch_shapes=(), aliases=None, vmem_mb=None):
    single = not isinstance(out_shape, (list, tuple))
    out_specs_l = [out_specs] if single else list(out_specs)
    out_shape_l = [out_shape] if single else list(out_shape)
    n_in, n_out, n_scr = len(in_specs), len(out_shape_l), len(scratch_shapes)
    aliases = dict(aliases or {})
    if comm is None:
        res = pl.pallas_call(body, name=name, grid=grid, in_specs=list(in_specs), out_specs=out_specs, out_shape=out_shape,
                             scratch_shapes=list(scratch_shapes), input_output_aliases=aliases,
                             compiler_params=_params(vmem_mb))(*args)
        return res, []
    nci, nco = len(comm.ins), len(comm.out_shapes)

    def wrapped(*refs):
        main_in, cin = refs[:n_in], refs[n_in:n_in + nci]
        o0 = n_in + nci
        main_out, cout = refs[o0:o0 + n_out], refs[o0 + n_out:o0 + n_out + nco]
        s0 = o0 + n_out + nco
        scr, sems = refs[s0:s0 + n_scr], refs[s0 + n_scr:]
        ids = [pl.program_id(i) for i in range(len(grid))]
        first = functools.reduce(jnp.logical_and, [i == 0 for i in ids])
        last = functools.reduce(jnp.logical_and, [i == g - 1 for i, g in zip(ids, grid)])

        @pl.when(first)
        def _():
            comm.start(cin, cout, sems)

        body(*main_in, *main_out, *scr)

        @pl.when(last)
        def _():
            comm.wait(cin, cout, sems)

    for ci, co in comm.aliases.items():
        aliases[n_in + ci] = n_out + co
    any_spec = pl.BlockSpec(memory_space=pl.ANY)
    res = pl.pallas_call(
        wrapped, name=name, grid=grid, in_specs=list(in_specs) + [any_spec] * nci, out_specs=out_specs_l + [any_spec] * nco,
        out_shape=out_shape_l + comm.out_shapes,
        scratch_shapes=list(scratch_shapes) + [pltpu.SemaphoreType.DMA((s,)) for s in comm.sems],
        input_output_aliases=aliases, compiler_params=_params(vmem_mb))(*args, *comm.ins)
    main = list(res[:n_out])
    return (main[0] if single else main), list(res[n_out:])


def norm_mod_fwd(x, g, mod, shift_row, scale_row, comm=None):
    tm = 256

    def body(x_ref, g_ref, mod_ref, h_ref):
        xv = x_ref[...]
        r = lax.rsqrt(jnp.mean(xv * xv, axis=1, keepdims=True) + EPS)
        n = xv * r * g_ref[...]
        h = n * (1.0 + mod_ref[scale_row:scale_row + 1, :]) + mod_ref[shift_row:shift_row + 1, :]
        h_ref[...] = h.astype(BF16)

    return _hosted_call(
        body, comm, (x, g, mod), name="norm_mod_fwd", grid=(S // tm,),
        in_specs=[pl.BlockSpec((tm, D), lambda i: (i, 0)), pl.BlockSpec((1, D), lambda i: (0, 0)),
                  pl.BlockSpec((8, D), lambda i: (0, 0))],
        out_specs=pl.BlockSpec((tm, D), lambda i: (i, 0)),
        out_shape=_sds((S, D), BF16))


def rope_tables():
    pos = jnp.arange(S, dtype=F32)
    inv_freq = ROPE_THETA ** (-jnp.arange(0, ROT_DIM, 2, dtype=F32) / ROT_DIM)
    ang = pos[:, None] * inv_freq[None, :]
    cos, sin = jnp.cos(ang), jnp.sin(ang)
    one, zero = jnp.ones((S, HD - ROT_DIM), F32), jnp.zeros((S, HD - ROT_DIM), F32)
    z8 = jnp.zeros((S, 8), F32)
    c = jnp.concatenate([cos, cos, one], axis=1)
    s1 = jnp.concatenate([-sin, z8, zero], axis=1)
    s2 = jnp.concatenate([z8, sin, zero], axis=1)
    return tuple(jnp.concatenate([t, t], axis=1) for t in (c, s1, s2))


def _rope(y, c, s1, s2):
    return y * c + pltpu.roll(y, LANES - 8, 1) * s1 + pltpu.roll(y, 8, 1) * s2


def _rope_bwd(dy, c, s1, s2):
    return dy * c + pltpu.roll(dy * s1, 8, 1) + pltpu.roll(dy * s2, LANES - 8, 1)


def in_proj_fwd(h, w_lay, tabs, comm=None):
    tm, tn = 2048, 768
    n_rope = N_DIL_PAIRS * 3 // 2

    def body(a_ref, w_ref, c_ref, s1_ref, s2_ref, o_ref):
        j = pl.program_id(0)
        y = jnp.dot(a_ref[...], w_ref[...], preferred_element_type=F32)

        @pl.when(j < n_rope)
        def _():
            c, s1, s2 = c_ref[...], s1_ref[...], s2_ref[...]
            for t in range(tn // LANES):
                chunk = y[:, LANES * t:LANES * (t + 1)]
                o_ref[:, LANES * t:LANES * (t + 1)] = chunk if t % 3 == 2 else _rope(chunk, c, s1, s2)

        @pl.when(j >= n_rope)
        def _():
            o_ref[...] = y

    tab = pl.BlockSpec((tm, LANES), lambda j, i: (i, 0))
    return _hosted_call(
        body, comm, (h, w_lay, *tabs), name="in_proj_fwd", grid=(LAY_N // tn, S // tm),
        in_specs=[pl.BlockSpec((tm, D), lambda j, i: (i, 0)), pl.BlockSpec((D, tn), lambda j, i: (0, j)), tab, tab, tab],
        out_specs=pl.BlockSpec((tm, tn), lambda j, i: (i, j)),
        out_shape=_sds((S, LAY_N), F32), vmem_mb=VMEM_MB)


def _log1p_small(t):
    return jnp.where(t < 1e-2, t * (1.0 - t * (0.5 - t * (1.0 / 3.0))), jnp.log(1.0 + t))


def fgate_fwd(p, b_pad):
    def body(fa_ref, b_ref, fraw_ref, fcol_ref):
        f = fa_ref[...] + b_ref[...]
        fr = f.T[0:8, :]
        ls = jnp.minimum(fr, 0.0) - _log1p_small(jnp.exp(-jnp.abs(fr)))
        lane = lax.broadcasted_iota(jnp.int32, (8, S), 1)
        acc, sh = ls, 1
        while sh < S:
            acc = acc + jnp.where(lane >= sh, pltpu.roll(acc, sh, 1), 0.0)
            sh *= 2
        fraw_ref[...] = fr
        for hh in range(8):
            fcol_ref[hh] = jnp.broadcast_to(acc[hh:hh + 1, :], (LANES, S)).T

    return pl.pallas_call(
        body, name="fgate_fwd", grid=(1,),
        in_specs=[pl.BlockSpec((S, LANES), lambda i: (0, LAY_F // LANES)), pl.BlockSpec((1, LANES), lambda i: (0, 0))],
        out_specs=[pl.BlockSpec((8, S), lambda i: (0, 0)), pl.BlockSpec((8, S, LANES), lambda i: (0, 0, 0))],
        out_shape=[_sds((8, S), F32), _sds((8, S, LANES), F32)],
        compiler_params=_params(VMEM_MB),
    )(p, b_pad)


def _head_masks(rows):
    lane = lax.broadcasted_iota(jnp.int32, (rows, LANES), 1)
    return lane < HD, lane >= HD


FT = 256


def _split3(f):
    hi = f.astype(BF16).astype(F32)
    r = f - hi
    mid = r.astype(BF16).astype(F32)
    return hi, mid, r - mid


def _fox_operands(qkv_ref, tcol_ref, scol_ref, qa_s, ka_s):
    rows = 256
    lane = lax.broadcasted_iota(jnp.int32, (rows, LANES), 1)

    def chunk(i, _):
        r = pl.ds(pl.multiple_of(i * rows, rows), rows)
        q, k = qkv_ref[r, 0:LANES], qkv_ref[r, LANES:2 * LANES]
        s0, s1 = _split3(scol_ref[0, r, :]), _split3(scol_ref[1, r, :])
        ka = jnp.where(lane == 0, -s0[0], jnp.where(lane == 1, -s0[1], jnp.where(lane == 2, -s0[2], jnp.where(
            lane == 3, -s1[0], jnp.where(lane == 4, -s1[1], jnp.where(lane == 5, -s1[2], jnp.where(lane < 9, 1.0, 0.0)))))))
        ka_s[r, 0:LANES] = k.astype(BF16)
        ka_s[r, LANES:2 * LANES] = ka.astype(BF16)
        for hh in range(2):
            own = (lane < HD) if hh == 0 else (lane >= HD)
            t3 = _split3(tcol_ref[hh, r, :])
            ones = (lane >= 3 * hh) & (lane < 3 * hh + 3)
            qa = jnp.where(ones, 1.0, jnp.where(lane == 6, t3[0], jnp.where(lane == 7, t3[1], jnp.where(lane == 8, t3[2], 0.0))))
            qa_s[hh, r, 0:LANES] = jnp.where(own, q * SCALE, 0.0).astype(BF16)
            qa_s[hh, r, LANES:2 * LANES] = qa.astype(BF16)
        return 0

    lax.fori_loop(0, S // rows, chunk, 0)


def fox_fwd(p, fcol, comm=None):
    nt = (((1,), (1,)), ((), ()))

    def body(qkv_ref, fc_ref, o_ref, g_ref, qa_s, ka_s):
        _fox_operands(qkv_ref, fc_ref, fc_ref, qa_s, ka_s)
        masks = _head_masks(FT)
        causal = lax.broadcasted_iota(jnp.int32, (FT, FT), 1) <= lax.broadcasted_iota(jnp.int32, (FT, FT), 0)
        causal2 = jnp.concatenate([causal, causal], axis=0)

        def qloop(qi, _):
            q0 = pl.multiple_of(qi * FT, FT)
            qa = jnp.concatenate([qa_s[0, pl.ds(q0, FT), :], qa_s[1, pl.ds(q0, FT), :]], axis=0)

            def step(kb, carry, diagonal):
                m, l, acc = carry
                k0 = pl.multiple_of(kb * FT, FT)
                v = qkv_ref[pl.ds(k0, FT), 2 * LANES:3 * LANES].astype(BF16)
                s = lax.dot_general(qa, ka_s[pl.ds(k0, FT), :], nt, preferred_element_type=F32)
                if diagonal:
                    s = jnp.where(causal2, s, NEG)
                m_new = jnp.maximum(m, jnp.max(s, axis=1, keepdims=True))
                pr = jnp.exp(s - m_new)
                alpha = jnp.exp(m - m_new)
                return (m_new, l * alpha + jnp.sum(pr, axis=1, keepdims=True),
                        acc * alpha + jnp.dot(pr.astype(BF16), v, preferred_element_type=F32))

            init = (jnp.full((2 * FT, 1), NEG, F32), jnp.zeros((2 * FT, 1), F32), jnp.zeros((2 * FT, LANES), F32))
            carry = lax.fori_loop(0, qi, lambda kb, cr: step(kb, cr, False), init)
            m, l, acc = step(qi, carry, True)
            out = acc / l
            lse = m + jnp.log(l)
            o_ref[pl.ds(q0, FT), :] = jnp.where(masks[0], out[:FT], out[FT:]).astype(BF16)
            g_ref[0, pl.ds(q0, FT), :] = fc_ref[0, pl.ds(q0, FT), :] - lse[:FT]
            g_ref[1, pl.ds(q0, FT), :] = fc_ref[1, pl.ds(q0, FT), :] - lse[FT:]
            return 0

        lax.fori_loop(0, S // FT, qloop, 0)

    a_blk = LAY_A // 384
    return _hosted_call(
        body, comm, (p, fcol), name="fox_fwd", grid=(N_FOX_PAIRS,),
        in_specs=[pl.BlockSpec((S, 384), lambda p_: (0, a_blk + p_)), pl.BlockSpec((2, S, LANES), lambda p_: (p_, 0, 0))],
        out_specs=[pl.BlockSpec((S, LANES), lambda p_: (0, p_)), pl.BlockSpec((2, S, LANES), lambda p_: (p_, 0, 0))],
        out_shape=[_sds((S, 4 * LANES), BF16), _sds((8, S, LANES), F32)],
        scratch_shapes=[pltpu.VMEM((2, S, 2 * LANES), BF16), pltpu.VMEM((S, 2 * LANES), BF16)],
        vmem_mb=VMEM_MB)


def _dil_rows(ref, start, d):
    return ref[pl.ds(start, SPAN), :] if d == 1 else ref[pl.ds(start, SPAN, stride=d), :]


def _dil_store(ref, start, d, val):
    if d == 1:
        ref[pl.ds(start, SPAN), :] = val
    else:
        ref[pl.ds(start, SPAN, stride=d), :] = val


def _band_mask(has_prev):
    qi = lax.broadcasted_iota(jnp.int32, (SPAN, 2 * SPAN), 0) + SPAN
    kj = lax.broadcasted_iota(jnp.int32, (SPAN, 2 * SPAN), 1)
    dist = qi - kj
    return (dist >= 0) & (dist <= SPAN) & (has_prev | (kj >= SPAN))


def _dil_block(n, d, nb):
    r, j = n // nb, n % nb
    start = r + d * SPAN * j
    prev = jnp.maximum(start - d * SPAN, r)
    return start, prev, j > 0


def dil_fwd(p, comm=None):
    def body(*refs):
        qkv = [refs[3 * g:3 * g + 3] for g in range(3)]
        y_ref, lse_ref = refs[9], refs[10]
        acc_s, m_s, l_s = refs[11], refs[12], refs[13]
        masks = _head_masks(SPAN)
        for g, (d, nb) in enumerate(DIL_GROUPS):
            q_ref, k_ref, v_ref = qkv[g]

            def blk(n, _):
                start, prev, has_prev = _dil_block(n, d, nb)
                q = _dil_rows(q_ref, start, d)
                kc = jnp.concatenate([_dil_rows(k_ref, prev, d), _dil_rows(k_ref, start, d)], axis=0).astype(BF16)
                vc = jnp.concatenate([_dil_rows(v_ref, prev, d), _dil_rows(v_ref, start, d)], axis=0).astype(BF16)
                valid = _band_mask(has_prev)
                valid2 = jnp.concatenate([valid, valid], axis=0)
                q2 = (jnp.concatenate([jnp.where(masks[0], q, 0.0), jnp.where(masks[1], q, 0.0)], axis=0) * SCALE).astype(BF16)
                s = jnp.where(valid2, lax.dot_general(q2, kc, (((1,), (1,)), ((), ())), preferred_element_type=F32), NEG)
                m = jnp.max(s, axis=1, keepdims=True)
                pr = jnp.exp(s - m)
                l = jnp.sum(pr, axis=1, keepdims=True)
                acc = jnp.dot(pr.astype(BF16), vc, preferred_element_type=F32)
                _dil_store(acc_s.at[g], start, d, jnp.where(masks[0], acc[:SPAN], acc[SPAN:]))
                _dil_store(m_s.at[g], start, d, jnp.where(masks[0], m[:SPAN], m[SPAN:]))
                _dil_store(l_s.at[g], start, d, jnp.where(masks[0], l[:SPAN], l[SPAN:]))
                return 0

            lax.fori_loop(0, 16, blk, 0)

        def merge(i, _):
            rows = pl.ds(pl.multiple_of(i * 256, 256), 256)
            m = [m_s[g, rows, :] for g in range(3)]
            mx = jnp.maximum(jnp.maximum(m[0], m[1]), m[2])
            w = [jnp.exp(m[g] - mx) for g in range(3)]
            l = sum(l_s[g, rows, :] * w[g] for g in range(3))
            y_ref[rows, :] = sum(acc_s[g, rows, :] * w[g] for g in range(3)) / l
            lse_ref[rows, :] = mx + jnp.log(l)
            return 0

        lax.fori_loop(0, S // 256, merge, 0)

    def spec(g, t):
        return pl.BlockSpec((S, LANES), lambda p_: (0, (p_ * 3 + g) * 3 + t))

    return _hosted_call(
        body, comm, [p] * 9, name="dil_fwd", grid=(N_DIL_PAIRS,),
        in_specs=[spec(g, t) for g in range(3) for t in range(3)],
        out_specs=[pl.BlockSpec((S, LANES), lambda p_: (0, p_)), pl.BlockSpec((S, LANES), lambda p_: (0, p_))],
        out_shape=[_sds((S, 2 * LANES), F32), _sds((S, 2 * LANES), F32)],
        scratch_shapes=[pltpu.VMEM((3, S, LANES), F32)] * 3,
        vmem_mb=VMEM_MB)


def merge_fwd(ya_att, yb, p, w_bra, w_brb, comm=None):
    tm = 256
    gblk = LAY_G // D

    def body(a_ref, b_ref, ga_ref, gb_ref, wa_ref, wb_ref, mg_ref, ya_ref, yb_ref):
        ya = jnp.dot(a_ref[...], wa_ref[...], preferred_element_type=F32)
        ybp = jnp.dot(b_ref[...].astype(BF16), wb_ref[...], preferred_element_type=F32)
        mg_ref[...] = (_sigmoid(ga_ref[...]) * ya + _sigmoid(gb_ref[...]) * ybp).astype(BF16)
        ya_ref[...] = ya
        yb_ref[...] = ybp

    row = lambda w: pl.BlockSpec((tm, w), lambda i: (i, 0))
    return _hosted_call(
        body, comm, (ya_att, yb, p, p, w_bra, w_brb), name="merge_fwd", grid=(S // tm,),
        in_specs=[row(512), row(256), pl.BlockSpec((tm, D), lambda i: (i, gblk)), pl.BlockSpec((tm, D), lambda i: (i, gblk + 1)),
                  pl.BlockSpec((512, D), lambda i: (0, 0)), pl.BlockSpec((256, D), lambda i: (0, 0))],
        out_specs=[row(D), row(D), row(D)],
        out_shape=[_sds((S, D), BF16), _sds((S, D), F32), _sds((S, D), F32)])


def out_proj_fwd(merged, w_out, x, mod, g_ffn):
    tm = 256

    def body(a_ref, w_ref, x_ref, mod_ref, g_ref, mix_ref, x1_ref, h2_ref):
        mix = jnp.dot(a_ref[...], w_ref[...], preferred_element_type=F32)
        x1 = x_ref[...] + mod_ref[2:3, :] * mix
        r = lax.rsqrt(jnp.mean(x1 * x1, axis=1, keepdims=True) + EPS)
        h2 = (x1 * r * g_ref[...]) * (1.0 + mod_ref[4:5, :]) + mod_ref[3:4, :]
        mix_ref[...] = mix
        x1_ref[...] = x1
        h2_ref[...] = h2.astype(BF16)

    row = pl.BlockSpec((tm, D), lambda i: (i, 0))
    return pl.pallas_call(
        body, name="out_proj_fwd", grid=(S // tm,),
        in_specs=[row, pl.BlockSpec((D, D), lambda i: (0, 0)), row, pl.BlockSpec((8, D), lambda i: (0, 0)),
                  pl.BlockSpec((1, D), lambda i: (0, 0))],
        out_specs=[row, row, row],
        out_shape=[_sds((S, D), F32), _sds((S, D), F32), _sds((S, D), BF16)],
    )(merged, w_out, x, mod, g_ffn)


def ffn_up_fwd(h2, w_gate, w_up):
    tm = 1024
    nt = (((1,), (1,)), ((), ()))

    def body(h_ref, wg_ref, wu_ref, a_ref, u_ref, z_ref):
        h = h_ref[...]
        a = lax.dot_general(h, wg_ref[...], nt, preferred_element_type=F32)
        u = lax.dot_general(h, wu_ref[...], nt, preferred_element_type=F32)
        a_ref[...] = a
        u_ref[...] = u
        z_ref[...] = (a * _sigmoid(a) * u).astype(BF16)

    out = pl.BlockSpec((tm, FF_PAD), lambda k, i: (i, k))
    return pl.pallas_call(
        body, name="ffn_up_fwd", grid=(4, S // tm),
        in_specs=[pl.BlockSpec((tm, D), lambda k, i: (i, 0)), pl.BlockSpec((None, FF_PAD, D), lambda k, i: (k, 0, 0)),
                  pl.BlockSpec((None, FF_PAD, D), lambda k, i: (k, 0, 0))],
        out_specs=[out, out, out],
        out_shape=[_sds((S, FFP), F32), _sds((S, FFP), F32), _sds((S, FFP), BF16)], compiler_params=_params(VMEM_MB),
    )(h2, w_gate, w_up)


def ffn_down_loss(z, w_down, x1, mod, g_final, tgt):
    tm = 256

    def body(z_ref, w_ref, x1_ref, mod_ref, g_ref, t_ref, dx2_ref, dffn_ref, dg_ref, dga_ref, loss_ref, s_dg, s_dga, s_loss):
        i = pl.program_id(0)

        @pl.when(i == 0)
        def _():
            s_dg[...] = jnp.zeros_like(s_dg)
            s_dga[...] = jnp.zeros_like(s_dga)
            s_loss[...] = jnp.zeros_like(s_loss)

        ffn = jnp.dot(z_ref[...], w_ref[...], preferred_element_type=F32)
        gaf = mod_ref[5:6, :]
        x2 = x1_ref[...] + gaf * ffn
        r = lax.rsqrt(jnp.mean(x2 * x2, axis=1, keepdims=True) + EPS)
        xh = x2 * r
        g = g_ref[...]
        e = xh * g - t_ref[...]
        s_loss[...] += 0.5 * jnp.sum(jnp.mean(e * e, axis=1, keepdims=True), axis=0, keepdims=True)
        dy = e * (1.0 / D)
        gdy = dy * g
        dx2 = r * (gdy - xh * jnp.mean(gdy * xh, axis=1, keepdims=True))
        s_dg[...] += _colsum8(dy * xh)
        s_dga[...] += _colsum8(dx2 * ffn)
        dx2_ref[...] = dx2
        dffn_ref[...] = (dx2 * gaf).astype(BF16)

        @pl.when(i == pl.num_programs(0) - 1)
        def _():
            dg_ref[...] = jnp.sum(s_dg[...], axis=0, keepdims=True)
            dga_ref[...] = jnp.sum(s_dga[...], axis=0, keepdims=True)
            loss_ref[...] = jnp.broadcast_to(s_loss[...], (1, LANES))

    row = pl.BlockSpec((tm, D), lambda i: (i, 0))
    vec = pl.BlockSpec((1, D), lambda i: (0, 0))
    return pl.pallas_call(
        body, name="ffn_down_loss", grid=(S // tm,),
        in_specs=[pl.BlockSpec((tm, FFP), lambda i: (i, 0)), pl.BlockSpec((FFP, D), lambda i: (0, 0)), row,
                  pl.BlockSpec((8, D), lambda i: (0, 0)), vec, row],
        out_specs=[row, row, vec, vec, pl.BlockSpec((1, LANES), lambda i: (0, 0))],
        out_shape=[_sds((S, D), F32), _sds((S, D), BF16), _sds((1, D), F32), _sds((1, D), F32), _sds((1, LANES), F32)],
        scratch_shapes=[pltpu.VMEM((8, D), F32), pltpu.VMEM((8, D), F32), pltpu.VMEM((1, 1), F32)],
        compiler_params=_params(VMEM_MB),
    )(z, w_down, x1, mod, g_final, tgt)


def ffn_down_bwd(dffn, w_down, a, u, z):
    tm, tn = 1024, 768

    def body(d_ref, w_ref, a_ref, u_ref, z_ref, da_ref, du_ref, dw_ref):
        i = pl.program_id(1)
        dff = d_ref[...]
        dz = lax.dot_general(dff, w_ref[...], (((1,), (1,)), ((), ())), preferred_element_type=F32)
        av, uv = a_ref[...], u_ref[...]
        sg = _sigmoid(av)
        du_ref[...] = (dz * (av * sg)).astype(BF16)
        da_ref[...] = (dz * uv * (sg * (1.0 + av * (1.0 - sg)))).astype(BF16)
        dw = lax.dot_general(z_ref[...], dff, (((0,), (0,)), ((), ())), preferred_element_type=F32)

        @pl.when(i == 0)
        def _():
            dw_ref[...] = dw

        @pl.when(i > 0)
        def _():
            dw_ref[...] += dw

    tile = pl.BlockSpec((tm, tn), lambda j, i: (i, j))
    return pl.pallas_call(
        body, name="ffn_down_bwd", grid=(FFP // tn, S // tm),
        in_specs=[pl.BlockSpec((tm, D), lambda j, i: (i, 0)), pl.BlockSpec((tn, D), lambda j, i: (j, 0)), tile, tile, tile],
        out_specs=[tile, tile, pl.BlockSpec((tn, D), lambda j, i: (j, 0))],
        out_shape=[_sds((S, FFP), BF16), _sds((S, FFP), BF16), _sds((FFP, D), F32)], compiler_params=_params(VMEM_MB),
    )(dffn, w_down, a, u, z)


def mm_nt(dy, w, name, comm=None):
    tm = 1024
    n = dy.shape[1]
    if w.ndim == 2:
        k_in, tk = w.shape[0], 768
        w_spec = pl.BlockSpec((k_in, tk), lambda i, k: (0, k))
        dims = (((1,), (1,)), ((), ()))
    else:
        k_in, tk = w.shape[2], FF_PAD
        w_spec = pl.BlockSpec((None, tk, k_in), lambda i, k: (k, 0, 0))
        dims = (((1,), (0,)), ((), ()))
    nk = n // tk

    def body(d_ref, w_ref, o_ref, acc):
        k = pl.program_id(1)
        part = lax.dot_general(d_ref[...], w_ref[...], dims, preferred_element_type=F32)

        @pl.when(k == 0)
        def _():
            acc[...] = part

        @pl.when(k > 0)
        def _():
            acc[...] += part

        @pl.when(k == nk - 1)
        def _():
            o_ref[...] = acc[...]

    return _hosted_call(
        body, comm, (dy, w), name=name, grid=(S // tm, nk),
        in_specs=[pl.BlockSpec((tm, tk), lambda i, k: (i, k)), w_spec],
        out_specs=pl.BlockSpec((tm, k_in), lambda i, k: (i, 0)),
        out_shape=_sds((S, k_in), F32),
        scratch_shapes=[pltpu.VMEM((tm, k_in), F32)], vmem_mb=VMEM_MB)


def mm_tn(h, dy, name, shard_major=False, comm=None):
    tm, tn = 2048, 768
    k_in, n = h.shape[1], dy.shape[1]

    def body(h_ref, d_ref, o_ref):
        i = pl.program_id(1)
        ops = (d_ref[...], h_ref[...]) if shard_major else (h_ref[...], d_ref[...])
        dw = lax.dot_general(*ops, (((0,), (0,)), ((), ())), preferred_element_type=F32)

        @pl.when(i == 0)
        def _():
            o_ref[...] = dw

        @pl.when(i > 0)
        def _():
            o_ref[...] += dw

    if shard_major:
        out_spec, out_shape = pl.BlockSpec((None, tn, k_in), lambda j, i: (j, 0, 0)), _sds((n // tn, tn, k_in), F32)
    else:
        out_spec, out_shape = pl.BlockSpec((k_in, tn), lambda j, i: (0, j)), _sds((k_in, n), F32)
    return _hosted_call(
        body, comm, (h, dy), name=name, grid=(n // tn, S // tm),
        in_specs=[pl.BlockSpec((tm, k_in), lambda j, i: (i, 0)), pl.BlockSpec((tm, tn), lambda j, i: (i, j))],
        out_specs=out_spec, out_shape=out_shape, vmem_mb=VMEM_MB)


def mid_bwd(dh2a, dh2b, x1, dx2, mix, mod, g_ffn, p, ya, ybp, merged, ya_att, yb, w_out, w_bra, w_brb, comm=None):
    tm = 256
    gblk = LAY_G // D
    nsteps = S // tm

    def body(dha_ref, dhb_ref, x1_ref, dx2_ref, mix_ref, mod_ref, g_ref, ga_ref, gb_ref, ya_ref, yb_ref, mg_ref,
             att_ref, ybb_ref, wo_ref, wa_ref, wb_ref,
             dx1_ref, dpg_ref, datt_ref, dyb_ref, cs_ref, dwo_ref, dwa_ref, dwb_ref, s_cs):
        i = pl.program_id(0)

        @pl.when(i == 0)
        def _():
            s_cs[...] = jnp.zeros_like(s_cs)
            dwo_ref[...] = jnp.zeros_like(dwo_ref)
            dwa_ref[...] = jnp.zeros_like(dwa_ref)
            dwb_ref[...] = jnp.zeros_like(dwb_ref)

        x1 = x1_ref[...]
        g = g_ref[...]
        r = lax.rsqrt(jnp.mean(x1 * x1, axis=1, keepdims=True) + EPS)
        xh = x1 * r
        dh2 = dha_ref[...] + dhb_ref[...]
        s_cs[0] += _colsum8(dh2)
        s_cs[1] += _colsum8(dh2 * (xh * g))
        dn2 = dh2 * (1.0 + mod_ref[4:5, :])
        s_cs[2] += _colsum8(dn2 * xh)
        gd = dn2 * g
        dx1 = dx2_ref[...] + r * (gd - xh * jnp.mean(gd * xh, axis=1, keepdims=True))
        s_cs[3] += _colsum8(dx1 * mix_ref[...])
        dx1_ref[...] = dx1
        dmix = (dx1 * mod_ref[2:3, :]).astype(BF16)
        dmg = lax.dot_general(dmix, wo_ref[...], (((1,), (1,)), ((), ())), preferred_element_type=F32)
        sga, sgb = _sigmoid(ga_ref[...]), _sigmoid(gb_ref[...])
        dya = (dmg * sga).astype(BF16)
        dybp = (dmg * sgb).astype(BF16)
        dpg_ref[:, 0:D] = (dmg * ya_ref[...] * (sga * (1.0 - sga))).astype(BF16)
        dpg_ref[:, D:2 * D] = (dmg * yb_ref[...] * (sgb * (1.0 - sgb))).astype(BF16)
        datt_ref[...] = lax.dot_general(dya, wa_ref[...], (((1,), (1,)), ((), ())), preferred_element_type=F32).astype(BF16)
        dyb_ref[...] = lax.dot_general(dybp, wb_ref[...], (((1,), (1,)), ((), ())), preferred_element_type=F32)
        tn_dims = (((0,), (0,)), ((), ()))
        dwo_ref[...] += lax.dot_general(mg_ref[...], dmix, tn_dims, preferred_element_type=F32)
        dwa_ref[...] += lax.dot_general(att_ref[...], dya, tn_dims, preferred_element_type=F32)
        dwb_ref[...] += lax.dot_general(ybb_ref[...].astype(BF16), dybp, tn_dims, preferred_element_type=F32)

        @pl.when(i == nsteps - 1)
        def _():
            for t in range(4):
                cs_ref[t:t + 1, :] = jnp.sum(s_cs[t], axis=0, keepdims=True)
            cs_ref[4:8, :] = jnp.zeros((4, D), F32)

    row = lambda w: pl.BlockSpec((tm, w), lambda i: (i, 0))
    full = lambda a, b: pl.BlockSpec((a, b), lambda i: (0, 0))
    return _hosted_call(
        body, comm, (dh2a, dh2b, x1, dx2, mix, mod, g_ffn, p, p, ya, ybp, merged, ya_att, yb, w_out, w_bra, w_brb),
        name="mid_bwd", grid=(nsteps,),
        in_specs=[row(D), row(D), row(D), row(D), row(D), full(8, D), full(1, D),
                  pl.BlockSpec((tm, D), lambda i: (i, gblk)), pl.BlockSpec((tm, D), lambda i: (i, gblk + 1)),
                  row(D), row(D), row(D), row(512), row(256), full(D, D), full(512, D), full(256, D)],
        out_specs=[row(D), pl.BlockSpec((tm, 2 * D), lambda i: (i, LAY_G // (2 * D))), row(512), row(256), full(8, D),
                   full(D, D), full(512, D), full(256, D)],
        out_shape=[_sds((S, D), F32), _sds((S, LAY_N), BF16), _sds((S, 512), BF16), _sds((S, 256), F32), _sds((8, D), F32),
                   _sds((D, D), F32), _sds((512, D), F32), _sds((256, D), F32)],
        scratch_shapes=[pltpu.VMEM((4, 8, D), F32)],
        vmem_mb=VMEM_MB)


def fox_bwd(p, do, o, gcol, fcol, dp, comm=None):
    nq = S // FT
    nt = (((1,), (1,)), ((), ()))
    tn = (((0,), (0,)), ((), ()))

    def body(qkv_ref, do_ref, o_ref, g_ref, fc_ref, dp_in, dp_ref, df_ref, rs_ref, dq_s, qa_s, ka_s, dob_s, dl_s):
        del dp_in
        _fox_operands(qkv_ref, g_ref, fc_ref, qa_s, ka_s)
        masks = _head_masks(FT)
        lane = lax.broadcasted_iota(jnp.int32, (FT, LANES), 1)
        head0 = 2 * pl.program_id(0)
        causal = lax.broadcasted_iota(jnp.int32, (FT, FT), 1) <= lax.broadcasted_iota(jnp.int32, (FT, FT), 0)
        dq_s[...] = jnp.zeros_like(dq_s)
        rs_ref[...] = jnp.zeros_like(rs_ref)

        causal2 = jnp.concatenate([causal, causal], axis=0)

        def prep(i, _):
            r = pl.ds(pl.multiple_of(i * 256, 256), 256)
            m256 = _head_masks(256)
            dov, ov = do_ref[r, :].astype(F32), o_ref[r, :].astype(F32)
            for hh in range(2):
                dom = jnp.where(m256[hh], dov, 0.0)
                dob_s[hh, r, :] = dom.astype(BF16)
                dl_s[hh, r, :] = jnp.broadcast_to(jnp.sum(dom * ov, axis=1, keepdims=True), (256, LANES))
            return 0

        lax.fori_loop(0, S // 256, prep, 0)

        def stack(ref, q0, cols=slice(None)):
            return jnp.concatenate([ref[0, pl.ds(q0, FT), cols], ref[1, pl.ds(q0, FT), cols]], axis=0)

        def kloop(kb, _):
            k0 = pl.multiple_of(kb * FT, FT)
            k = qkv_ref[pl.ds(k0, FT), LANES:2 * LANES].astype(BF16)
            v = qkv_ref[pl.ds(k0, FT), 2 * LANES:3 * LANES].astype(BF16)
            ka = ka_s[pl.ds(k0, FT), :]

            def step(qi, carry, diagonal):
                dk, dv, df0, df1 = carry
                q0 = pl.multiple_of(qi * FT, FT)
                qa, dob = stack(qa_s, q0), stack(dob_s, q0)
                s = lax.dot_general(qa, ka, nt, preferred_element_type=F32)
                pr = jnp.exp(jnp.where(causal2, s, NEG)) if diagonal else jnp.exp(s)
                dpr = lax.dot_general(dob, v, nt, preferred_element_type=F32)
                ds = pr * (dpr - jnp.tile(stack(dl_s, q0), (1, FT // LANES)))
                dsb = ds.astype(BF16)
                dq = jnp.dot(dsb, k, preferred_element_type=F32) * SCALE
                dk = dk + lax.dot_general(dsb, qa[:, 0:LANES], tn, preferred_element_type=F32)
                dv = dv + lax.dot_general(pr.astype(BF16), dob, tn, preferred_element_type=F32)
                rsum = jnp.sum(ds, axis=1, keepdims=True)
                dq_s[pl.ds(q0, FT), :] += jnp.where(masks[0], dq[:FT], dq[FT:])
                rs_ref[pl.ds(q0, FT), :] += jnp.where(lane == head0, rsum[:FT], 0.0) + jnp.where(lane == head0 + 1, rsum[FT:], 0.0)
                return (dk, dv, df0 - jnp.sum(ds[:FT], axis=0, keepdims=True), df1 - jnp.sum(ds[FT:], axis=0, keepdims=True))

            z = jnp.zeros((FT, LANES), F32)
            z1 = jnp.zeros((1, FT), F32)
            carry = step(kb, (z, z, z1, z1), True)
            dk, dv, df0, df1 = lax.fori_loop(kb + 1, nq, lambda qi, cr: step(qi, cr, False), carry)
            dp_ref[pl.ds(k0, FT), LANES:2 * LANES] = dk.astype(BF16)
            dp_ref[pl.ds(k0, FT), 2 * LANES:3 * LANES] = dv.astype(BF16)
            df_ref[0:1, pl.ds(k0, FT)] = df0
            df_ref[1:2, pl.ds(k0, FT)] = df1
            return 0

        lax.fori_loop(0, S // FT, kloop, 0)
        dp_ref[:, 0:LANES] = dq_s[...].astype(BF16)

    a_blk = LAY_A // 384
    pair = pl.BlockSpec((S, LANES), lambda p_: (0, p_))
    heads = pl.BlockSpec((2, S, LANES), lambda p_: (p_, 0, 0))
    return _hosted_call(
        body, comm, (p, do, o, gcol, fcol, dp), name="fox_bwd", grid=(N_FOX_PAIRS,),
        in_specs=[pl.BlockSpec((S, 384), lambda p_: (0, a_blk + p_)), pair, pair, heads, heads, pl.BlockSpec(memory_space=pl.ANY)],
        out_specs=[pl.BlockSpec((S, 384), lambda p_: (0, a_blk + p_)), pl.BlockSpec((None, 2, S), lambda p_: (p_, 0, 0)),
                   pl.BlockSpec((None, S, LANES), lambda p_: (p_, 0, 0))],
        out_shape=[_sds((S, LAY_N), BF16), _sds((4, 2, S), F32), _sds((4, S, LANES), F32)],
        scratch_shapes=[pltpu.VMEM((S, LANES), F32), pltpu.VMEM((2, S, 2 * LANES), BF16), pltpu.VMEM((S, 2 * LANES), BF16),
                        pltpu.VMEM((2, S, LANES), BF16), pltpu.VMEM((2, S, LANES), F32)],
        aliases={5: 0}, vmem_mb=VMEM_MB)


def fgate_bwd(dfrow, dfcol, fraw, dp):
    def body(df_ref, dc_ref, f_ref, dp_in, dpf_ref, db_ref):
        del dp_in
        lane = lax.broadcasted_iota(jnp.int32, (8, S), 1)
        rsum = (dc_ref[0] + dc_ref[1]) + (dc_ref[2] + dc_ref[3])
        acc, sh = df_ref[...] + rsum.T[0:8, :], 1
        while sh < S:
            acc = acc + jnp.where(lane < S - sh, pltpu.roll(acc, S - sh, 1), 0.0)
            sh *= 2
        df = acc * _sigmoid(-f_ref[...])
        db_ref[...] = jnp.broadcast_to(jnp.sum(df, axis=1, keepdims=True), (8, LANES))
        dfc = jnp.concatenate([df, jnp.zeros((LANES - 8, S), F32)], axis=0).T
        dpf_ref[:, 0:LANES] = dfc.astype(BF16)
        dpf_ref[:, LANES:2 * LANES] = jnp.zeros((S, LANES), BF16)

    return pl.pallas_call(
        body, name="fgate_bwd", grid=(1,),
        in_specs=[pl.BlockSpec((8, S), lambda i: (0, 0)), pl.BlockSpec((4, S, LANES), lambda i: (0, 0, 0)),
                  pl.BlockSpec((8, S), lambda i: (0, 0)), pl.BlockSpec(memory_space=pl.ANY)],
        out_specs=[pl.BlockSpec((S, 2 * LANES), lambda i: (0, LAY_F // (2 * LANES))), pl.BlockSpec((8, LANES), lambda i: (0, 0))],
        out_shape=[_sds((S, LAY_N), BF16), _sds((8, LANES), F32)],
        input_output_aliases={3: 0},
        compiler_params=_params(VMEM_MB),
    )(dfrow, dfcol, fraw, dp)


def dil_bwd(p, dyb, yb, lse, tabs, dp, comm=None):
    def body(*refs):
        qkv = [refs[3 * g:3 * g + 3] for g in range(3)]
        dy_ref, y_ref, lse_ref, c_ref, s1_ref, s2_ref = refs[9:15]
        dp_ref = refs[16]
        dq_s, dk_s, dv_s, dl_s = refs[17:21]
        masks = _head_masks(SPAN)
        m256 = _head_masks(256)
        nt = (((1,), (1,)), ((), ()))
        tn = (((0,), (0,)), ((), ()))
        dk_s[...] = jnp.zeros_like(dk_s)
        dv_s[...] = jnp.zeros_like(dv_s)

        def prep(i, _):
            rows = pl.ds(pl.multiple_of(i * 256, 256), 256)
            pr = dy_ref[rows, :] * y_ref[rows, :]
            d0 = jnp.sum(jnp.where(m256[0], pr, 0.0), axis=1, keepdims=True)
            d1 = jnp.sum(jnp.where(m256[1], pr, 0.0), axis=1, keepdims=True)
            dl_s[rows, :] = jnp.where(m256[0], d0, d1)
            return 0

        lax.fori_loop(0, S // 256, prep, 0)

        for g, (d, nb) in enumerate(DIL_GROUPS):
            q_ref, k_ref, v_ref = qkv[g]

            def blk(n, _):
                start, prev, has_prev = _dil_block(n, d, nb)
                q = _dil_rows(q_ref, start, d)
                kc = jnp.concatenate([_dil_rows(k_ref, prev, d), _dil_rows(k_ref, start, d)], axis=0).astype(BF16)
                vc = jnp.concatenate([_dil_rows(v_ref, prev, d), _dil_rows(v_ref, start, d)], axis=0).astype(BF16)
                dov = _dil_rows(dy_ref, start, d)
                lsev = _dil_rows(lse_ref, start, d)
                dlv = _dil_rows(dl_s, start, d)
                valid = _band_mask(has_prev)
                valid2 = jnp.concatenate([valid, valid], axis=0)

                def stack(t):
                    return jnp.concatenate([jnp.where(masks[0], t, 0.0), jnp.where(masks[1], t, 0.0)], axis=0)

                def column(t):
                    return jnp.concatenate([jnp.max(jnp.where(masks[hh], t, NEG), axis=1, keepdims=True) for hh in range(2)], axis=0)

                q2 = (stack(q) * SCALE).astype(BF16)
                dob = stack(dov).astype(BF16)
                s = jnp.where(valid2, lax.dot_general(q2, kc, nt, preferred_element_type=F32), NEG)
                pr = jnp.exp(s - column(lsev))
                dpr = lax.dot_general(dob, vc, nt, preferred_element_type=F32)
                dsb = (pr * (dpr - column(dlv))).astype(BF16)
                dq = jnp.dot(dsb, kc, preferred_element_type=F32) * SCALE
                dkc = lax.dot_general(dsb, q2, tn, preferred_element_type=F32)
                dvc = lax.dot_general(pr.astype(BF16), dob, tn, preferred_element_type=F32)
                _dil_store(dq_s.at[g], start, d, jnp.where(masks[0], dq[:SPAN], dq[SPAN:]))
                for ref, val in ((dk_s.at[g], dkc), (dv_s.at[g], dvc)):
                    _dil_store(ref, prev, d, _dil_rows(ref, prev, d) + jnp.where(has_prev, val[0:SPAN], 0.0))
                    _dil_store(ref, start, d, _dil_rows(ref, start, d) + val[SPAN:])
                return 0

            lax.fori_loop(0, 16, blk, 0)

        def fin(i, _):
            rows = pl.ds(pl.multiple_of(i * 256, 256), 256)
            c, s1, s2 = c_ref[rows, :], s1_ref[rows, :], s2_ref[rows, :]
            for g in range(3):
                base = g * 384
                dp_ref[rows, base:base + LANES] = _rope_bwd(dq_s[g, rows, :], c, s1, s2).astype(BF16)
                dp_ref[rows, base + LANES:base + 2 * LANES] = _rope_bwd(dk_s[g, rows, :], c, s1, s2).astype(BF16)
                dp_ref[rows, base + 2 * LANES:base + 3 * LANES] = dv_s[g, rows, :].astype(BF16)
            return 0

        lax.fori_loop(0, S // 256, fin, 0)

    def spec(g, t):
        return pl.BlockSpec((S, LANES), lambda p_: (0, (p_ * 3 + g) * 3 + t))

    pair = pl.BlockSpec((S, LANES), lambda p_: (0, p_))
    tab = pl.BlockSpec((S, LANES), lambda p_: (0, 0))
    return _hosted_call(
        body, comm, [p] * 9 + [dyb, yb, lse, *tabs, dp], name="dil_bwd", grid=(N_DIL_PAIRS,),
        in_specs=[spec(g, t) for g in range(3) for t in range(3)] + [pair, pair, pair, tab, tab, tab, pl.BlockSpec(memory_space=pl.ANY)],
        out_specs=pl.BlockSpec((S, 1152), lambda p_: (0, p_)),
        out_shape=_sds((S, LAY_N), BF16),
        scratch_shapes=[pltpu.VMEM((3, S, LANES), F32)] * 3 + [pltpu.VMEM((S, LANES), F32)],
        aliases={15: 0}, vmem_mb=VMEM_MB)


def in_bwd_tail(dh1, x, dx1, mod, g_mix, comm=None):
    tm = 256
    nsteps = S // tm

    def body(dh_ref, x_ref, dx1_ref, mod_ref, g_ref, dx_ref, cs_ref, s_cs):
        i = pl.program_id(0)

        @pl.when(i == 0)
        def _():
            s_cs[...] = jnp.zeros_like(s_cs)

        xv, g, dh = x_ref[...], g_ref[...], dh_ref[...]
        r = lax.rsqrt(jnp.mean(xv * xv, axis=1, keepdims=True) + EPS)
        xh = xv * r
        s_cs[0] += _colsum8(dh)
        s_cs[1] += _colsum8(dh * (xh * g))
        dn = dh * (1.0 + mod_ref[1:2, :])
        s_cs[2] += _colsum8(dn * xh)
        gd = dn * g
        dx_ref[...] = dx1_ref[...] + r * (gd - xh * jnp.mean(gd * xh, axis=1, keepdims=True))

        @pl.when(i == nsteps - 1)
        def _():
            for t in range(3):
                cs_ref[t:t + 1, :] = jnp.sum(s_cs[t], axis=0, keepdims=True)
            cs_ref[3:8, :] = jnp.zeros((5, D), F32)

    row = pl.BlockSpec((tm, D), lambda i: (i, 0))
    return _hosted_call(
        body, comm, (dh1, x, dx1, mod, g_mix), name="in_bwd_tail", grid=(nsteps,),
        in_specs=[row, row, row, pl.BlockSpec((8, D), lambda i: (0, 0)), pl.BlockSpec((1, D), lambda i: (0, 0))],
        out_specs=[row, pl.BlockSpec((8, D), lambda i: (0, 0))],
        out_shape=[_sds((S, D), F32), _sds((8, D), F32)],
        scratch_shapes=[pltpu.VMEM((3, 8, D), F32)])


def _lay_pieces():
    out = []
    qa, ka, va, fa, qb, kb, vb, ga = 0, 512, 1024, 1536, 1544, 2312, 3080, 3848
    for p in range(N_DIL_PAIRS):
        for g in range(3):
            base = LAY_B + (p * 3 + g) * 384
            hd0 = (4 * g + 2 * p) * HD
            out += [(base, qb + hd0, LANES), (base + LANES, kb + hd0, LANES), (base + 2 * LANES, vb + hd0, LANES)]
    for p in range(N_FOX_PAIRS):
        base = LAY_A + p * 384
        out += [(base, qa + p * LANES, LANES), (base + LANES, ka + p * LANES, LANES), (base + 2 * LANES, va + p * LANES, LANES)]
    out.append((LAY_F, fa, 8))
    out.append((LAY_G, ga, 2 * D))
    return out


def _shard_runs():
    runs = []
    for lay, nat, width in _lay_pieces():
        while width:
            k, loc = nat // IN_SHARD, nat % IN_SHARD
            w = min(width, IN_SHARD - loc)
            runs.append((lay, k, loc, w))
            lay, nat, width = lay + w, nat + w, width - w
    return runs


def lay_from_shards(g):
    tm = 256

    def body(g_ref, o_ref):
        o_ref[:, LAY_F:LAY_G] = jnp.zeros((tm, LAY_G - LAY_F), g.dtype)
        for lay, k, loc, w in _shard_runs():
            o_ref[:, lay:lay + w] = g_ref[k, :, loc:loc + w]

    return pl.pallas_call(
        body, name="lay_from_shards", grid=(D // tm,),
        in_specs=[pl.BlockSpec((4, tm, IN_SHARD_PAD), lambda i: (0, i, 0))],
        out_specs=pl.BlockSpec((tm, LAY_N), lambda i: (i, 0)),
        out_shape=_sds((D, LAY_N), g.dtype), compiler_params=_params(VMEM_MB),
    )(g)


def shards_from_lay(dw_lay):
    tm = 256

    def body(x_ref, o_ref):
        o_ref[:, :, IN_SHARD:] = jnp.zeros((4, tm, IN_SHARD_PAD - IN_SHARD), F32)
        for lay, k, loc, w in _shard_runs():
            o_ref[k, :, loc:loc + w] = x_ref[:, lay:lay + w]

    return pl.pallas_call(
        body, name="shards_from_lay", grid=(D // tm,),
        in_specs=[pl.BlockSpec((tm, LAY_N), lambda i: (i, 0))],
        out_specs=pl.BlockSpec((4, tm, IN_SHARD_PAD), lambda i: (0, i, 0)),
        out_shape=_sds((4, D, IN_SHARD_PAD), F32), compiler_params=_params(VMEM_MB),
    )(dw_lay)


def _pos():
    return lax.axis_index("x"), lax.axis_index("y"), lax.axis_index("c")


def _other_chips(x, y):
    return [(1 - x, y), (x, 1 - y), (1 - x, 1 - y)]


def _remote(src, dst, send_sem, recv_sem, dev):
    return pltpu.make_async_remote_copy(src_ref=src, dst_ref=dst, send_sem=send_sem, recv_sem=recv_sem,
                                        device_id=dev, device_id_type=MESH)


VMEM_SPEC = pl.BlockSpec(memory_space=pltpu.VMEM)
ANY_SPEC = pl.BlockSpec(memory_space=pl.ANY)


def gather_all(v, name, with_sum):
    r = v.shape[0]

    def body(v_ref, out_ref, *rest):
        send_s, recv_s = rest[-2:]
        x, y, c = _pos()
        me = 4 * x + 2 * y + c
        out_ref[me] = v_ref[...]
        peers = []
        for m in range(1, 8):
            px = 1 - x if m & 4 else x
            py = 1 - y if m & 2 else y
            pc = 1 - c if m & 1 else c
            peers.append((px, py, pc))
        copies = [_remote(v_ref, out_ref.at[me], send_s.at[i], recv_s.at[i], dev) for i, dev in enumerate(peers)]
        for cp in copies:
            cp.start()
        for i, (px, py, pc) in enumerate(peers):
            _remote(v_ref, out_ref.at[4 * px + 2 * py + pc], send_s.at[i], recv_s.at[i], (px, py, pc)).wait_recv()
        for cp in copies:
            cp.wait_send()
        if with_sum:
            acc = out_ref[0]
            for b in range(1, 8):
                acc = acc + out_ref[b]
            rest[0][...] = acc

    out_shape = [_sds((8, r, LANES), F32)] + ([_sds((r, LANES), F32)] if with_sum else [])
    return pl.pallas_call(
        body, name=name, in_specs=[VMEM_SPEC], out_specs=[VMEM_SPEC] * len(out_shape), out_shape=out_shape,
        scratch_shapes=[pltpu.SemaphoreType.DMA((7,)), pltpu.SemaphoreType.DMA((7,))],
    )(v)


def mod_exchange(c_all, w_ada_sh, b_sh):
    def body(c_ref, w_ref, b_ref, out_ref, sc_ref, modp, send_s, recv_s):
        cv = c_ref[...]
        sc = cv * _sigmoid(cv)
        sc_ref[...] = sc
        modp[...] = jnp.dot(sc, w_ref[...], precision=lax.Precision.HIGHEST, preferred_element_type=F32) + b_ref[...]
        x, y, c = _pos()
        k = 2 * x + y
        out_ref[k] = modp[...]
        chips = _other_chips(x, y)
        copies = [_remote(modp, out_ref.at[k], send_s.at[j], recv_s.at[j], (cx, cy, c)) for j, (cx, cy) in enumerate(chips)]
        for cp in copies:
            cp.start()
        for j, (cx, cy) in enumerate(chips):
            _remote(modp, out_ref.at[2 * cx + cy], send_s.at[j], recv_s.at[j], (cx, cy, c)).wait_recv()
        for cp in copies:
            cp.wait_send()

    n = w_ada_sh.shape[1]
    return pl.pallas_call(
        body, name="mod_exchange", in_specs=[VMEM_SPEC] * 3, out_specs=[VMEM_SPEC] * 2,
        out_shape=[_sds((4, 8, n), F32), _sds((8, D), F32)],
        scratch_shapes=[pltpu.VMEM((8, n), F32), pltpu.SemaphoreType.DMA((3,)), pltpu.SemaphoreType.DMA((3,))],
        compiler_params=_params(VMEM_MB),
    )(c_all, w_ada_sh, b_sh)


def cast_into_slabs(ws, sizes, chip, comm):
    tr = 64
    n = len(ws)
    n_in = [w.shape[0] // tr for w in ws]
    n_out = [r // tr for r, _ in sizes]
    steps = max(n_out)
    nci, nco = len(comm.ins), len(comm.out_shapes)

    def body(chip_ref, *refs):
        del chip_ref
        w_refs, cin = refs[:n], refs[n:n + nci]
        o_refs, cout = refs[n + nci:2 * n + nci], refs[2 * n + nci:2 * n + nci + nco]
        sems = refs[2 * n + nci + nco:]
        i = pl.program_id(0)

        @pl.when(i == 0)
        def _():
            comm.start(cin, cout, sems)

        for a in range(n):
            c0, cols = ws[a].shape[1], sizes[a][1]

            @pl.when(i < n_in[a])
            def _(a=a, c0=c0, cols=cols):
                o_refs[a][:, 0:c0] = w_refs[a][...].astype(BF16)
                if cols > c0:
                    o_refs[a][:, c0:] = jnp.zeros((tr, cols - c0), BF16)

            if n_out[a] > n_in[a]:
                @pl.when((i >= n_in[a]) & (i < n_out[a]))
                def _(a=a, cols=cols):
                    o_refs[a][...] = jnp.zeros((tr, cols), BF16)

        @pl.when(i == steps - 1)
        def _():
            comm.wait(cin, cout, sems)

    any_spec = pl.BlockSpec(memory_space=pl.ANY)
    grid_spec = pltpu.PrefetchScalarGridSpec(
        num_scalar_prefetch=1, grid=(steps,),
        in_specs=[pl.BlockSpec((tr, w.shape[1]), functools.partial(lambda i, k, last: (jnp.minimum(i, last), 0), last=n_in[a] - 1))
                  for a, w in enumerate(ws)] + [any_spec] * nci,
        out_specs=[pl.BlockSpec((None, tr, sizes[a][1]), functools.partial(lambda i, k, last: (k[0], jnp.minimum(i, last), 0), last=n_out[a] - 1))
                   for a in range(n)] + [any_spec] * nco,
        scratch_shapes=[pltpu.SemaphoreType.DMA((s,)) for s in comm.sems])
    res = pl.pallas_call(
        body, name="cast_into_slabs", grid_spec=grid_spec,
        out_shape=[_sds((4,) + tuple(sz), BF16) for sz in sizes] + comm.out_shapes,
        input_output_aliases={1 + n + ci: n + co for ci, co in comm.aliases.items()},
    )(chip, *ws, *comm.ins)
    return list(res[:n]), list(res[n:])


def cast_into_slab(w, rows, cols, chip, tag):
    r0, c0 = w.shape
    tr = 256 if (r0 % 256 == 0 and rows % 256 == 0) else 64
    n_in, n_out = r0 // tr, rows // tr

    def body(chip_ref, w_ref, o_ref):
        del chip_ref
        i = pl.program_id(0)

        @pl.when(i < n_in)
        def _():
            o_ref[:, 0:c0] = w_ref[...].astype(BF16)
            if cols > c0:
                o_ref[:, c0:] = jnp.zeros((tr, cols - c0), BF16)

        @pl.when(i >= n_in)
        def _():
            o_ref[...] = jnp.zeros((tr, cols), BF16)

    grid_spec = pltpu.PrefetchScalarGridSpec(
        num_scalar_prefetch=1, grid=(n_out,),
        in_specs=[pl.BlockSpec((tr, c0), lambda i, k: (jnp.minimum(i, n_in - 1), 0))],
        out_specs=pl.BlockSpec((None, tr, cols), lambda i, k: (k[0], i, 0)))
    return pl.pallas_call(body, name="cast_" + tag, grid_spec=grid_spec, out_shape=_sds((4, rows, cols), BF16))(chip, w)


def _row_tile(rows, cap=256):
    t = cap
    while rows % t or t % 8:
        t -= 8
    return t


def _comm_wait(sends, recvs, local=()):
    for cp in recvs:
        cp.wait_recv()
    for cp in sends:
        cp.wait_send()
    for cp in local:
        cp.wait()


def ag_ici(bufs):
    n = len(bufs)

    def copies(ins, outs, sems):
        send_s, recv_s = sems
        x, y, c = _pos()
        k = 2 * x + y
        sends, recvs = [], []
        for a in range(n):
            half = outs[a].shape[1] // 2
            rows = pl.ds(c * half, half)
            for j, (cx, cy) in enumerate(_other_chips(x, y)):
                sem = (send_s.at[3 * a + j], recv_s.at[3 * a + j], (cx, cy, c))
                sends.append(_remote(outs[a].at[k, rows], outs[a].at[k, rows], *sem))
                recvs.append(_remote(outs[a].at[k, rows], outs[a].at[2 * cx + cy, rows], *sem))
        return sends, recvs

    def start(ins, outs, sems):
        for cp in copies(ins, outs, sems)[0]:
            cp.start()

    def wait(ins, outs, sems):
        _comm_wait(*copies(ins, outs, sems))

    return Comm(bufs, [_sds(b.shape, b.dtype) for b in bufs], [3 * n, 3 * n], start, wait, aliases={a: a for a in range(n)})


def ag_d2d(bufs):
    n = len(bufs)

    def copies(ins, outs, sems):
        send_s, recv_s = sems
        x, y, c = _pos()
        sends, recvs = [], []
        for a in range(n):
            half = outs[a].shape[1] // 2
            rows, orows = pl.ds(c * half, half), pl.ds((1 - c) * half, half)
            for j, (cx, cy) in enumerate(_other_chips(x, y)):
                kj = 2 * cx + cy
                sem = (send_s.at[3 * a + j], recv_s.at[3 * a + j], (x, y, 1 - c))
                sends.append(_remote(outs[a].at[kj, rows], outs[a].at[kj, rows], *sem))
                recvs.append(_remote(outs[a].at[kj, orows], outs[a].at[kj, orows], *sem))
        return sends, recvs

    def start(ins, outs, sems):
        for cp in copies(ins, outs, sems)[0]:
            cp.start()

    def wait(ins, outs, sems):
        _comm_wait(*copies(ins, outs, sems))

    return Comm(bufs, [_sds(b.shape, b.dtype) for b in bufs], [3 * n, 3 * n], start, wait, aliases={a: a for a in range(n)})


def rs_a(grads):
    n = len(grads)

    def copies(ins, outs, sems):
        send_s, recv_s = sems
        x, y, c = _pos()
        cps = []
        for a in range(n):
            half = ins[a].shape[1] // 2
            cps.append(_remote(ins[a].at[:, pl.ds((1 - c) * half, half), :], outs[a], send_s.at[a], recv_s.at[a], (x, y, 1 - c)))
        return cps

    def start(ins, outs, sems):
        for cp in copies(ins, outs, sems):
            cp.start()

    def wait(ins, outs, sems):
        cps = copies(ins, outs, sems)
        _comm_wait(cps, cps)

    return Comm(grads, [_sds((4, g.shape[1] // 2, g.shape[2]), g.dtype) for g in grads], [n, n], start, wait)


def rs_b(pres):
    n = len(pres)

    def copies(ins, outs, sems):
        send_s, recv_s = sems
        x, y, c = _pos()
        cps = []
        for a in range(n):
            for j, (cx, cy) in enumerate(_other_chips(x, y)):
                cps.append(_remote(ins[a].at[2 * cx + cy], outs[a].at[j], send_s.at[3 * a + j], recv_s.at[3 * a + j], (cx, cy, c)))
        return cps

    def start(ins, outs, sems):
        for cp in copies(ins, outs, sems):
            cp.start()

    def wait(ins, outs, sems):
        cps = copies(ins, outs, sems)
        _comm_wait(cps, cps)

    return Comm(pres, [_sds((3,) + p_.shape[1:], p_.dtype) for p_ in pres], [3 * n, 3 * n], start, wait)


def rs_b_rows(pre, buf, lo, n):
    def copies(ins, outs, sems):
        send_s, recv_s = sems
        x, y, c = _pos()
        rows = pl.ds(lo, n)
        return [_remote(ins[0].at[2 * cx + cy, rows], outs[0].at[j, rows], send_s.at[j], recv_s.at[j], (cx, cy, c))
                for j, (cx, cy) in enumerate(_other_chips(x, y))]

    def start(ins, outs, sems):
        for cp in copies(ins, outs, sems):
            cp.start()

    def wait(ins, outs, sems):
        cps = copies(ins, outs, sems)
        _comm_wait(cps, cps)

    ins = [pre] if buf is None else [pre, buf]
    return Comm(ins, [_sds((3,) + pre.shape[1:], pre.dtype)], [3, 3], start, wait, aliases={} if buf is None else {1: 0})


def rs_c(reds):
    n = len(reds)

    def copies(ins, outs, sems):
        send_s, recv_s = sems
        x, y, c = _pos()
        sends, recvs = [], []
        for a in range(n):
            half = outs[a].shape[0] // 2
            rows, orows = pl.ds(c * half, half), pl.ds((1 - c) * half, half)
            sem = (send_s.at[a], recv_s.at[a], (x, y, 1 - c))
            sends.append(_remote(outs[a].at[rows], outs[a].at[rows], *sem))
            recvs.append(_remote(outs[a].at[orows], outs[a].at[orows], *sem))
        return sends, recvs

    def start(ins, outs, sems):
        for cp in copies(ins, outs, sems)[0]:
            cp.start()

    def wait(ins, outs, sems):
        _comm_wait(*copies(ins, outs, sems))

    return Comm(reds, [_sds(r_.shape, r_.dtype) for r_ in reds], [n, n], start, wait, aliases={a: a for a in range(n)})


def comm_join(*comms):
    ni = np.cumsum([0] + [len(c.ins) for c in comms])
    no = np.cumsum([0] + [len(c.out_shapes) for c in comms])
    ns = np.cumsum([0] + [len(c.sems) for c in comms])

    def parts(ins, outs, sems):
        return [(c, ins[ni[i]:ni[i + 1]], outs[no[i]:no[i + 1]], sems[ns[i]:ns[i + 1]]) for i, c in enumerate(comms)]

    def start(ins, outs, sems):
        for c, a, b, s in parts(ins, outs, sems):
            c.start(a, b, s)

    def wait(ins, outs, sems):
        for c, a, b, s in parts(ins, outs, sems):
            c.wait(a, b, s)

    aliases = {int(ni[i]) + k: int(no[i]) + v for i, c in enumerate(comms) for k, v in c.aliases.items()}
    return Comm(sum((c.ins for c in comms), []), sum((c.out_shapes for c in comms), []), sum((c.sems for c in comms), []),
                start, wait, aliases)


def comm_only(comm, name):
    nci, nco = len(comm.ins), len(comm.out_shapes)

    def body(*refs):
        ins, outs, sems = refs[:nci], refs[nci:nci + nco], refs[nci + nco:]
        comm.start(ins, outs, sems)
        comm.wait(ins, outs, sems)

    return pl.pallas_call(
        body, name=name, in_specs=[ANY_SPEC] * nci, out_specs=[ANY_SPEC] * nco, out_shape=comm.out_shapes,
        scratch_shapes=[pltpu.SemaphoreType.DMA((s,)) for s in comm.sems],
        input_output_aliases=comm.aliases,
    )(*comm.ins)


def rs_add_halves(g, other, core, name):
    _, r, cdim = g.shape
    half = r // 2
    tr = _row_tile(half, 256)
    nb = half // tr

    def body(core_ref, g_ref, o_ref, out_ref):
        del core_ref
        out_ref[...] = (g_ref[...] + o_ref[...]).astype(BF16)

    grid_spec = pltpu.PrefetchScalarGridSpec(
        num_scalar_prefetch=1, grid=(4, nb),
        in_specs=[pl.BlockSpec((None, tr, cdim), lambda k, i, cr: (k, cr[0] * nb + i, 0)),
                  pl.BlockSpec((None, tr, cdim), lambda k, i, cr: (k, i, 0))],
        out_specs=pl.BlockSpec((None, tr, cdim), lambda k, i, cr: (k, i, 0)))
    return pl.pallas_call(body, name=name, grid_spec=grid_spec, out_shape=_sds((4, half, cdim), BF16))(core, g, other)


def rs_add_slabs(t, pre, place, name):
    _, half, cdim = t.shape
    tr = _row_tile(half, 256)
    nb = half // tr

    def body(place_ref, own_ref, t_ref, out_ref):
        del place_ref
        out_ref[...] = ((own_ref[...].astype(F32) + t_ref[0].astype(F32)) + t_ref[1].astype(F32)) + t_ref[2].astype(F32)

    grid_spec = pltpu.PrefetchScalarGridSpec(
        num_scalar_prefetch=1, grid=(nb,),
        in_specs=[pl.BlockSpec((None, tr, cdim), lambda i, pr: (pr[0], i, 0)), pl.BlockSpec((3, tr, cdim), lambda i, pr: (0, i, 0))],
        out_specs=pl.BlockSpec((tr, cdim), lambda i, pr: (pr[1] * nb + i, 0)))
    return pl.pallas_call(body, name=name, grid_spec=grid_spec, out_shape=_sds((2 * half, cdim), F32))(place, pre, t)


def _adam_math(w, g, m, v):
    m = ADAM_B1 * m + (1.0 - ADAM_B1) * g
    v = ADAM_B2 * v + (1.0 - ADAM_B2) * (g * g)
    m_hat = m / (1.0 - ADAM_B1 ** ADAM_STEP)
    v_hat = v / (1.0 - ADAM_B2 ** ADAM_STEP)
    delta = -ADAM_LR * (m_hat / (jnp.sqrt(v_hat) + ADAM_EPS) + ADAM_WD * w)
    return delta, m, v


def adam(w, g, m, v, name, comm=None):
    r, cdim = w.shape
    tr = _row_tile(r) if r >= 8 else r

    def body(w_ref, g_ref, m_ref, v_ref, g_out, d_ref, nm_ref, nv_ref):
        gv = g_ref[:, :cdim]
        g_out[...] = gv
        d_ref[...], nm_ref[...], nv_ref[...] = _adam_math(w_ref[...], gv, m_ref[...], v_ref[...])

    blk = pl.BlockSpec((tr, cdim), lambda i: (i, 0))
    return _hosted_call(
        body, comm, (w, g, m, v), name=name, grid=(r // tr,),
        in_specs=[blk, pl.BlockSpec((tr, g.shape[1]), lambda i: (i, 0)), blk, blk],
        out_specs=[blk] * 4, out_shape=[_sds((r, cdim), F32)] * 4)


def adam_small(groups):
    n = len(groups)

    def body(*refs):
        ins, outs = refs[:4 * n], refs[4 * n:]
        for i in range(n):
            w_ref, m_ref, v_ref, g_ref = ins[4 * i:4 * i + 4]
            d_ref, nm_ref, nv_ref = outs[3 * i:3 * i + 3]
            d_ref[...], nm_ref[...], nv_ref[...] = _adam_math(w_ref[...], g_ref[...], m_ref[...], v_ref[...])

    flat = [t for grp in groups for t in grp]
    out = pl.pallas_call(
        body, name="adam_small", in_specs=[VMEM_SPEC] * (4 * n), out_specs=[VMEM_SPEC] * (3 * n),
        out_shape=[_sds(grp[0].shape, F32) for grp in groups for _ in range(3)],
    )(*flat)
    return [out[3 * i:3 * i + 3] for i in range(n)]


def adam_w_ada(sc_t, dmod_sh, w, m, v, comm=None):
    r, cdim = w.shape
    tr = 256

    def body(s_ref, d_ref, w_ref, m_ref, v_ref, g_ref, dl_ref, nm_ref, nv_ref):
        g = jnp.dot(s_ref[...], d_ref[...], precision=lax.Precision.HIGHEST, preferred_element_type=F32)
        g_ref[...] = g
        dl_ref[...], nm_ref[...], nv_ref[...] = _adam_math(w_ref[...], g, m_ref[...], v_ref[...])

    blk = pl.BlockSpec((tr, cdim), lambda i: (i, 0))
    return _hosted_call(
        body, comm, (sc_t, dmod_sh, w, m, v), name="adam_w_ada", grid=(r // tr,),
        in_specs=[pl.BlockSpec((tr, LANES), lambda i: (i, 0)), pl.BlockSpec((LANES, cdim), lambda i: (0, 0)), blk, blk, blk],
        out_specs=[blk] * 4, out_shape=[_sds((r, cdim), F32)] * 4)


SMALL_ROWS = 80


def kernel(x, c, w_ada, b_ada, g_mix, w_in, b_fgate, w_br_a, w_br_b, w_out, g_ffn, w_ffn_gate, w_ffn_up, w_ffn_down, g_final, loss_target, m_w_ada, m_b_ada, m_g_mix, m_w_in, m_b_fgate, m_w_br_a, m_w_br_b, m_w_out, m_g_ffn, m_w_ffn_gate, m_w_ffn_up, m_w_ffn_down, m_g_final, v_w_ada, v_b_ada, v_g_mix, v_w_in, v_b_fgate, v_w_br_a, v_w_br_b, v_w_out, v_g_ffn, v_w_ffn_gate, v_w_ffn_up, v_w_ffn_down, v_g_final):
    xi, yi, ci = _pos()
    chip = 2 * xi + yi
    seq = 4 * xi + 2 * yi + ci
    n_ada = w_ada.shape[2]

    c_all = gather_all(c.reshape(8, LANES), "gather_c", False)[0].reshape(8, D)
    b_sh = lax.dynamic_slice(b_ada, (0, chip * n_ada), (1, n_ada))
    mod_all, sc = mod_exchange(c_all, w_ada[0], b_sh)
    mod = lax.dynamic_index_in_dim(mod_all, seq, axis=1, keepdims=False).reshape(6, D)
    mod8 = jnp.pad(mod, ((0, 2), (0, 0)))

    core = ci.astype(jnp.int32).reshape(1)
    chip1 = chip.astype(jnp.int32).reshape(1)
    place = jnp.stack([chip, ci]).astype(jnp.int32)
    s_in = cast_into_slab(w_in[0], D, IN_SHARD_PAD, chip1, "w_in")
    (s_bra, s_brb, s_out, s_gate, s_up, s_down), (g_in,) = cast_into_slabs(
        [w_br_a[0], w_br_b[0], w_out[0], w_ffn_gate[0].T, w_ffn_up[0].T, w_ffn_down[0]],
        [(512, 256), (256, 256), (256, D), (FF_PAD, D), (FF_PAD, D), (FF_PAD, D)], chip1, ag_ici([s_in]))
    xs, tgt, g_fin = x[0], loss_target[0], g_final.reshape(1, D)

    def halves(gs, others, tag):
        return [rs_add_halves(g, o, core, f"rs_{tag}_halves_{i}") for i, (g, o) in enumerate(zip(gs, others))]

    def slab_sums(ts, pres, tag):
        return [rs_add_slabs(t, pre, place, f"rs_{tag}_slabs_{i}") for i, (t, pre) in enumerate(zip(ts, pres))]

    tabs = rope_tables()
    h1, (g_in,) = norm_mod_fwd(xs, g_mix, mod8, 0, 1, comm=ag_d2d([g_in]))
    w_lay = lay_from_shards(g_in)
    p, mix_w = in_proj_fwd(h1, w_lay, tabs, comm=ag_ici([s_bra, s_brb, s_out]))
    fraw, fcol = fgate_fwd(p, jnp.pad(b_fgate, ((0, 0), (0, LANES - 8))))
    (ya_att, gcol), res = fox_fwd(p, fcol, comm=comm_join(ag_d2d(mix_w), ag_ici([s_gate, s_up])))
    g_bra, g_brb, g_out = res[:3]
    (yb, lse_b), res = dil_fwd(p, comm=comm_join(ag_d2d(res[3:]), ag_ici([s_down])))
    w_gate, w_up = res[:2]
    w_bra = g_bra.transpose(1, 0, 2).reshape(512, D)
    w_brb = g_brb.transpose(1, 0, 2).reshape(256, D)
    w_o = g_out.reshape(D, D)
    (merged, ya, ybp), (g_down,) = merge_fwd(ya_att, yb, p, w_bra, w_brb, comm=ag_d2d(res[2:]))
    w_down = g_down.reshape(FFP, D)
    mix, x1, h2 = out_proj_fwd(merged, w_o, xs, mod8, g_ffn)
    a, u, z = ffn_up_fwd(h2, w_gate, w_up)
    dx2, dffn, dg_final, dga_f, loss_part = ffn_down_loss(z, w_down, x1, mod8, g_fin, tgt)

    da, du, dw_down = ffn_down_bwd(dffn, w_down, a, u, z)
    g_down = [dw_down.reshape(4, FF_PAD, D)]
    dh2a, oth = mm_nt(da, w_gate, "ffn_gate_dx", comm=rs_a(g_down))
    pre_down = halves(g_down, oth, "down")
    dh2b, _ = mm_nt(du, w_up, "ffn_up_dx")
    dw_gate, _ = mm_tn(h2, da, "ffn_gate_dw", shard_major=True)
    dw_up, _ = mm_tn(h2, du, "ffn_up_dw", shard_major=True)
    g_gu = [dw_gate, dw_up]
    (dx1, dp1, dya_att, dyb, cs_mid, dw_out, dw_bra, dw_brb), res = mid_bwd(
        dh2a, dh2b, x1, dx2, mix, mod8, g_ffn, p, ya, ybp, merged, ya_att, yb, w_o, w_bra, w_brb,
        comm=comm_join(rs_b(pre_down), rs_a(g_gu)))
    red_down = slab_sums(res[:1], pre_down, "down")
    pre_gu = halves(g_gu, res[1:], "gu")
    g_mix3 = [dw_bra.reshape(512, 4, 256).transpose(1, 0, 2), dw_brb.reshape(256, 4, 256).transpose(1, 0, 2), dw_out.reshape(4, 256, D)]
    (dp2, dfrow, dfcol), res = fox_bwd(p, dya_att, ya_att, gcol, fcol, dp1,
                                       comm=comm_join(rs_b(pre_gu), rs_c(red_down), rs_a(g_mix3)))
    red_gu = slab_sums(res[:2], pre_gu, "gu")
    r_down = res[2]
    pre_mix3 = halves(g_mix3, res[3:], "mix")
    dp3, db_fg = fgate_bwd(dfrow.reshape(8, S), dfcol, fraw, dp2)
    dp4, res = dil_bwd(p, dyb, yb, lse_b, tabs, dp3, comm=comm_join(rs_c(red_gu), rs_b(pre_mix3)))
    r_gate, r_up = res[:2]
    red_mix3 = slab_sums(res[2:], pre_mix3, "mix")
    dw_lay, (r_bra, r_brb, r_out) = mm_tn(h1, dp4, "in_proj_dw", comm=rs_c(red_mix3))
    g_in4 = [shards_from_lay(dw_lay)]
    dh1, oth = mm_nt(dp4, w_lay, "in_proj_dx", comm=rs_a(g_in4))
    (pre_in,) = halves(g_in4, oth, "in")
    qrows = pre_in.shape[1] // 4
    (dx, cs_in), (t_in,) = in_bwd_tail(dh1, xs, dx1, mod8, g_mix, comm=rs_b_rows(pre_in, None, 0, qrows))

    dmod = jnp.concatenate([cs_in[0:2], cs_mid[3:4], cs_mid[0:2], dga_f], axis=0)
    small = dict(dmod=dmod, dg_mix=cs_in[2:3], dg_ffn=cs_mid[2:3], dg_final=dg_final, db_fgate=db_fg[:, 0], loss=loss_part[0, 0])
    sv = jnp.concatenate([
        small["dmod"].reshape(48, LANES), small["dg_mix"].reshape(8, LANES), small["dg_ffn"].reshape(8, LANES),
        small["dg_final"].reshape(8, LANES), jnp.pad(small["db_fgate"], (0, LANES - 8)).reshape(1, LANES),
        jnp.broadcast_to(small["loss"], (1, LANES)), jnp.zeros((SMALL_ROWS - 74, LANES), F32)], axis=0)
    sv_all, sv_sum = gather_all(sv, "gather_small", True)
    loss = sv_sum[73, 0]
    g_small = dict(b_ada=sv_sum[0:48].reshape(1, 6 * D), g_mix=sv_sum[48:56].reshape(1, D), g_ffn=sv_sum[56:64].reshape(1, D),
                   g_final=sv_sum[64:72].reshape(D), b_fgate=sv_sum[72, 0:8].reshape(1, 8))

    dmod_all = lax.dynamic_slice(sv_all[:, 0:48, :].reshape(8, 6 * D), (0, chip * n_ada), (8, n_ada))
    (g_ada, d_ada, nm_ada, nv_ada), (t_in,) = adam_w_ada(
        jnp.pad(sc.T, ((0, 0), (0, LANES - 8))), jnp.pad(dmod_all, ((0, LANES - 8), (0, 0))), w_ada[0], m_w_ada[0], v_w_ada[0],
        comm=rs_b_rows(pre_in, t_in, qrows, 3 * qrows))

    big = dict(w_in=(w_in, m_w_in, v_w_in), w_br_a=(w_br_a, m_w_br_a, v_w_br_a), w_br_b=(w_br_b, m_w_br_b, v_w_br_b),
               w_out=(w_out, m_w_out, v_w_out), w_ffn_gate=(w_ffn_gate, m_w_ffn_gate, v_w_ffn_gate),
               w_ffn_up=(w_ffn_up, m_w_ffn_up, v_w_ffn_up), w_ffn_down=(w_ffn_down, m_w_ffn_down, v_w_ffn_down))
    gpad = dict(w_br_a=r_bra, w_br_b=r_brb, w_out=r_out, w_ffn_gate=r_gate, w_ffn_up=r_up, w_ffn_down=r_down)
    upd = {}
    for nm in ("w_ffn_gate", "w_ffn_up"):
        w, m, v = big[nm]
        upd[nm] = [t.T for t in adam(w[0].T, gpad[nm], m[0].T, v[0].T, "adam_" + nm)[0]]
    (gpad["w_in"],) = comm_only(rs_c(slab_sums([t_in], [pre_in], "in")), "rs_in_share")
    for nm, (w, m, v) in big.items():
        if nm not in upd:
            upd[nm] = adam(w[0], gpad[nm], m[0], v[0], "adam_" + nm)[0]

    small_names = ["g_mix", "g_ffn", "g_final", "b_ada", "b_fgate"]
    small_w = dict(g_mix=(g_mix, m_g_mix, v_g_mix), g_ffn=(g_ffn, m_g_ffn, v_g_ffn), g_final=(g_final, m_g_final, v_g_final),
                   b_ada=(b_ada, m_b_ada, v_b_ada), b_fgate=(b_fgate, m_b_fgate, v_b_fgate))
    row = lambda t: t.reshape(1, -1)
    res = adam_small([[row(t) for t in small_w[nm]] + [row(g_small[nm])] for nm in small_names])
    small_upd = [{nm: res[i][which].reshape(small_w[nm][0].shape) for i, nm in enumerate(small_names)} for which in range(3)]
    order =["w_ada", "b_ada", "g_mix", "w_in", "b_fgate", "w_br_a", "w_br_b", "w_out", "g_ffn", "w_ffn_gate", "w_ffn_up", "w_ffn_down", "g_final"]

    def leaf(nm, which):
        if nm == "w_ada":
            return (g_ada, d_ada, nm_ada, nv_ada)[which][None]
        if nm in big:
            return upd[nm][which][None]
        return g_small[nm] if which == 0 else small_upd[which - 1][nm]

    outs = [loss, dx[None]]
    for which in range(4):
        outs += [leaf(nm, which) for nm in order]
    return tuple(outs)
```

```python
import functools

import numpy as np
import jax
import jax.numpy as jnp
from jax import lax
from jax.experimental import pallas as pl
from jax.experimental.pallas import tpu as pltpu

F32, BF16 = jnp.float32, jnp.bfloat16
S, D = 2048, 1024
HD = 64
LANES = 128
N_FOX_PAIRS, N_DIL_PAIRS = 4, 2
DIL_GROUPS = ((1, 16), (4, 4), (16, 1))
SPAN = 128
ROT_DIM, ROPE_THETA = 16, 500000.0
D_FF, FF_SHARD, FF_PAD = 2816, 704, 768
FFP = 4 * FF_PAD
IN_COLS, IN_SHARD, IN_SHARD_PAD = 5896, 1474, 1536
LAY_B, LAY_A, LAY_F, LAY_G, LAY_N = 0, 2304, 3840, 4096, 6144
EPS, NEG = 1e-6, -1e30
SCALE = HD ** -0.5
ADAM_LR, ADAM_B1, ADAM_B2, ADAM_EPS, ADAM_WD, ADAM_STEP = 0.001, 0.9, 0.999, 1e-08, 0.01, 10
VMEM_MB = 56
MESH = pl.DeviceIdType.MESH


def _params(vmem_mb=None, **kw):
    if vmem_mb is not None:
        kw["vmem_limit_bytes"] = vmem_mb * 1024 * 1024
    return pltpu.CompilerParams(**kw)


def _sds(shape, dtype):
    return jax.ShapeDtypeStruct(shape, dtype)


def _sigmoid(x):
    return 1.0 / (1.0 + jnp.exp(-x))


def _colsum8(x):
    tm, n = x.shape
    return jnp.sum(x.reshape(tm // 8, 8, n), axis=0)


class Comm:
    def __init__(self, ins, out_shapes, sems, start, wait, aliases=None):
        self.ins, self.out_shapes, self.sems = list(ins), list(out_shapes), list(sems)
        self.start, self.wait, self.aliases = start, wait, dict(aliases or {})


def _hosted_call(body, comm, args, *, name, grid, in_specs, out_specs, out_shape, scratch_shapes=(), aliases=None, vmem_mb=None):
    single = not isinstance(out_shape, (list, tuple))
    out_specs_l = [out_specs] if single else list(out_specs)
    out_shape_l = [out_shape] if single else list(out_shape)
    n_in, n_out, n_scr = len(in_specs), len(out_shape_l), len(scratch_shapes)
    aliases = dict(aliases or {})
    if comm is None:
        res = pl.pallas_call(body, name=name, grid=grid, in_specs=list(in_specs), out_specs=out_specs, out_shape=out_shape,
                             scratch_shapes=list(scratch_shapes), input_output_aliases=aliases,
                             compiler_params=_params(vmem_mb))(*args)
        return res, []
    nci, nco = len(comm.ins), len(comm.out_shapes)

    def wrapped(*refs):
        main_in, cin = refs[:n_in], refs[n_in:n_in + nci]
        o0 = n_in + nci
        main_out, cout = refs[o0:o0 + n_out], refs[o0 + n_out:o0 + n_out + nco]
        s0 = o0 + n_out + nco
        scr, sems = refs[s0:s0 + n_scr], refs[s0 + n_scr:]
        ids = [pl.program_id(i) for i in range(len(grid))]
        first = functools.reduce(jnp.logical_and, [i == 0 for i in ids])
        last = functools.reduce(jnp.logical_and, [i == g - 1 for i, g in zip(ids, grid)])

        @pl.when(first)
        def _():
            comm.start(cin, cout, sems)

        body(*main_in, *main_out, *scr)

        @pl.when(last)
        def _():
            comm.wait(cin, cout, sems)

    for ci, co in comm.aliases.items():
        aliases[n_in + ci] = n_out + co
    any_spec = pl.BlockSpec(memory_space=pl.ANY)
    res = pl.pallas_call(
        wrapped, name=name, grid=grid, in_specs=list(in_specs) + [any_spec] * nci, out_specs=out_specs_l + [any_spec] * nco,
        out_shape=out_shape_l + comm.out_shapes,
        scratch_shapes=list(scratch_shapes) + [pltpu.SemaphoreType.DMA((s,)) for s in comm.sems],
        input_output_aliases=aliases, compiler_params=_params(vmem_mb))(*args, *comm.ins)
    main = list(res[:n_out])
    return (main[0] if single else main), list(res[n_out:])


def norm_mod_fwd(x, g, mod, shift_row, scale_row, comm=None):
    tm = 256

    def body(x_ref, g_ref, mod_ref, h_ref):
        xv = x_ref[...]
        r = lax.rsqrt(jnp.mean(xv * xv, axis=1, keepdims=True) + EPS)
        n = xv * r * g_ref[...]
        h = n * (1.0 + mod_ref[scale_row:scale_row + 1, :]) + mod_ref[shift_row:shift_row + 1, :]
        h_ref[...] = h.astype(BF16)

    return _hosted_call(
        body, comm, (x, g, mod), name="norm_mod_fwd", grid=(S // tm,),
        in_specs=[pl.BlockSpec((tm, D), lambda i: (i, 0)), pl.BlockSpec((1, D), lambda i: (0, 0)),
                  pl.BlockSpec((8, D), lambda i: (0, 0))],
        out_specs=pl.BlockSpec((tm, D), lambda i: (i, 0)),
        out_shape=_sds((S, D), BF16))


def rope_tables():
    pos = jnp.arange(S, dtype=F32)
    inv_freq = ROPE_THETA ** (-jnp.arange(0, ROT_DIM, 2, dtype=F32) / ROT_DIM)
    ang = pos[:, None] * inv_freq[None, :]
    cos, sin = jnp.cos(ang), jnp.sin(ang)
    one, zero = jnp.ones((S, HD - ROT_DIM), F32), jnp.zeros((S, HD - ROT_DIM), F32)
    z8 = jnp.zeros((S, 8), F32)
    c = jnp.concatenate([cos, cos, one], axis=1)
    s1 = jnp.concatenate([-sin, z8, zero], axis=1)
    s2 = jnp.concatenate([z8, sin, zero], axis=1)
    return tuple(jnp.concatenate([t, t], axis=1) for t in (c, s1, s2))


def _rope(y, c, s1, s2):
    return y * c + pltpu.roll(y, LANES - 8, 1) * s1 + pltpu.roll(y, 8, 1) * s2


def _rope_bwd(dy, c, s1, s2):
    return dy * c + pltpu.roll(dy * s1, 8, 1) + pltpu.roll(dy * s2, LANES - 8, 1)


def in_proj_fwd(h, w_lay, tabs, comm=None):
    tm, tn = 2048, 768
    n_rope = N_DIL_PAIRS * 3 // 2

    def body(a_ref, w_ref, c_ref, s1_ref, s2_ref, o_ref):
        j = pl.program_id(0)
        y = jnp.dot(a_ref[...], w_ref[...], preferred_element_type=F32)

        @pl.when(j < n_rope)
        def _():
            c, s1, s2 = c_ref[...], s1_ref[...], s2_ref[...]
            for t in range(tn // LANES):
                chunk = y[:, LANES * t:LANES * (t + 1)]
                o_ref[:, LANES * t:LANES * (t + 1)] = chunk if t % 3 == 2 else _rope(chunk, c, s1, s2)

        @pl.when(j >= n_rope)
        def _():
            o_ref[...] = y

    tab = pl.BlockSpec((tm, LANES), lambda j, i: (i, 0))
    return _hosted_call(
        body, comm, (h, w_lay, *tabs), name="in_proj_fwd", grid=(LAY_N // tn, S // tm),
        in_specs=[pl.BlockSpec((tm, D), lambda j, i: (i, 0)), pl.BlockSpec((D, tn), lambda j, i: (0, j)), tab, tab, tab],
        out_specs=pl.BlockSpec((tm, tn), lambda j, i: (i, j)),
        out_shape=_sds((S, LAY_N), F32), vmem_mb=VMEM_MB)


def _log1p_small(t):
    return jnp.where(t < 1e-2, t * (1.0 - t * (0.5 - t * (1.0 / 3.0))), jnp.log(1.0 + t))


def fgate_fwd(p, b_pad):
    def body(fa_ref, b_ref, fraw_ref, fcol_ref):
        f = fa_ref[...] + b_ref[...]
        fr = f.T[0:8, :]
        ls = jnp.minimum(fr, 0.0) - _log1p_small(jnp.exp(-jnp.abs(fr)))
        lane = lax.broadcasted_iota(jnp.int32, (8, S), 1)
        acc, sh = ls, 1
        while sh < S:
            acc = acc + jnp.where(lane >= sh, pltpu.roll(acc, sh, 1), 0.0)
            sh *= 2
        fraw_ref[...] = fr
        for hh in range(8):
            fcol_ref[hh] = jnp.broadcast_to(acc[hh:hh + 1, :], (LANES, S)).T

    return pl.pallas_call(
        body, name="fgate_fwd", grid=(1,),
        in_specs=[pl.BlockSpec((S, LANES), lambda i: (0, LAY_F // LANES)), pl.BlockSpec((1, LANES), lambda i: (0, 0))],
        out_specs=[pl.BlockSpec((8, S), lambda i: (0, 0)), pl.BlockSpec((8, S, LANES), lambda i: (0, 0, 0))],
        out_shape=[_sds((8, S), F32), _sds((8, S, LANES), F32)],
        compiler_params=_params(VMEM_MB),
    )(p, b_pad)


def _head_masks(rows):
    lane = lax.broadcasted_iota(jnp.int32, (rows, LANES), 1)
    return lane < HD, lane >= HD


FT = 256


def _split3(f):
    hi = f.astype(BF16).astype(F32)
    r = f - hi
    mid = r.astype(BF16).astype(F32)
    return hi, mid, r - mid


def _fox_operands(qkv_ref, tcol_ref, scol_ref, qa_s, ka_s):
    rows = 256
    lane = lax.broadcasted_iota(jnp.int32, (rows, LANES), 1)

    def chunk(i, _):
        r = pl.ds(pl.multiple_of(i * rows, rows), rows)
        q, k = qkv_ref[r, 0:LANES], qkv_ref[r, LANES:2 * LANES]
        s0, s1 = _split3(scol_ref[0, r, :]), _split3(scol_ref[1, r, :])
        ka = jnp.where(lane == 0, -s0[0], jnp.where(lane == 1, -s0[1], jnp.where(lane == 2, -s0[2], jnp.where(
            lane == 3, -s1[0], jnp.where(lane == 4, -s1[1], jnp.where(lane == 5, -s1[2], jnp.where(lane < 9, 1.0, 0.0)))))))
        ka_s[r, 0:LANES] = k.astype(BF16)
        ka_s[r, LANES:2 * LANES] = ka.astype(BF16)
        for hh in range(2):
            own = (lane < HD) if hh == 0 else (lane >= HD)
            t3 = _split3(tcol_ref[hh, r, :])
            ones = (lane >= 3 * hh) & (lane < 3 * hh + 3)
            qa = jnp.where(ones, 1.0, jnp.where(lane == 6, t3[0], jnp.where(lane == 7, t3[1], jnp.where(lane == 8, t3[2], 0.0))))
            qa_s[hh, r, 0:LANES] = jnp.where(own, q * SCALE, 0.0).astype(BF16)
            qa_s[hh, r, LANES:2 * LANES] = qa.astype(BF16)
        return 0

    lax.fori_loop(0, S // rows, chunk, 0)


def fox_fwd(p, fcol, comm=None):
    nt = (((1,), (1,)), ((), ()))

    def body(qkv_ref, fc_ref, o_ref, g_ref, qa_s, ka_s):
        _fox_operands(qkv_ref, fc_ref, fc_ref, qa_s, ka_s)
        masks = _head_masks(FT)
        causal = lax.broadcasted_iota(jnp.int32, (FT, FT), 1) <= lax.broadcasted_iota(jnp.int32, (FT, FT), 0)
        causal2 = jnp.concatenate([causal, causal], axis=0)

        def qloop(qi, _):
            q0 = pl.multiple_of(qi * FT, FT)
            qa = jnp.concatenate([qa_s[0, pl.ds(q0, FT), :], qa_s[1, pl.ds(q0, FT), :]], axis=0)

            def step(kb, carry, diagonal, tiles=1):
                m, l, acc = carry
                k0 = pl.multiple_of(kb * FT, FT)
                keys = pl.ds(k0, tiles * FT)
                v = qkv_ref[keys, 2 * LANES:3 * LANES].astype(BF16)
                s = lax.dot_general(qa, ka_s[keys, :], nt, preferred_element_type=F32)
                if diagonal:
                    s = jnp.where(causal2, s, NEG)
                m_new = jnp.maximum(m, jnp.max(s, axis=1, keepdims=True))
                pr = jnp.exp(s - m_new)
                alpha = jnp.exp(m - m_new)
                return (m_new, l * alpha + jnp.sum(pr, axis=1, keepdims=True),
                        acc * alpha + jnp.dot(pr.astype(BF16), v, preferred_element_type=F32))

            init = (jnp.full((2 * FT, 1), NEG, F32), jnp.zeros((2 * FT, 1), F32), jnp.zeros((2 * FT, LANES), F32))
            pairs = qi // 2
            carry = lax.fori_loop(0, pairs, lambda j, cr: step(2 * j, cr, False, 2), init)
            carry = lax.fori_loop(2 * pairs, qi, lambda kb, cr: step(kb, cr, False), carry)
            m, l, acc = step(qi, carry, True)
            out = acc / l
            lse = m + jnp.log(l)
            o_ref[pl.ds(q0, FT), :] = jnp.where(masks[0], out[:FT], out[FT:]).astype(BF16)
            g_ref[0, pl.ds(q0, FT), :] = fc_ref[0, pl.ds(q0, FT), :] - lse[:FT]
            g_ref[1, pl.ds(q0, FT), :] = fc_ref[1, pl.ds(q0, FT), :] - lse[FT:]
            return 0

        lax.fori_loop(0, S // FT, qloop, 0)

    a_blk = LAY_A // 384
    return _hosted_call(
        body, comm, (p, fcol), name="fox_fwd", grid=(N_FOX_PAIRS,),
        in_specs=[pl.BlockSpec((S, 384), lambda p_: (0, a_blk + p_)), pl.BlockSpec((2, S, LANES), lambda p_: (p_, 0, 0))],
        out_specs=[pl.BlockSpec((S, LANES), lambda p_: (0, p_)), pl.BlockSpec((2, S, LANES), lambda p_: (p_, 0, 0))],
        out_shape=[_sds((S, 4 * LANES), BF16), _sds((8, S, LANES), F32)],
        scratch_shapes=[pltpu.VMEM((2, S, 2 * LANES), BF16), pltpu.VMEM((S, 2 * LANES), BF16)],
        vmem_mb=VMEM_MB)


def _dil_rows(ref, start, d):
    return ref[pl.ds(start, SPAN), :] if d == 1 else ref[pl.ds(start, SPAN, stride=d), :]


def _dil_store(ref, start, d, val):
    if d == 1:
        ref[pl.ds(start, SPAN), :] = val
    else:
        ref[pl.ds(start, SPAN, stride=d), :] = val


def _band_mask(has_prev):
    qi = lax.broadcasted_iota(jnp.int32, (SPAN, 2 * SPAN), 0) + SPAN
    kj = lax.broadcasted_iota(jnp.int32, (SPAN, 2 * SPAN), 1)
    dist = qi - kj
    return (dist >= 0) & (dist <= SPAN) & (has_prev | (kj >= SPAN))


def _dil_block(n, d, nb):
    r, j = n // nb, n % nb
    start = r + d * SPAN * j
    prev = jnp.maximum(start - d * SPAN, r)
    return start, prev, j > 0


def dil_fwd(p, comm=None):
    def body(*refs):
        qkv = [refs[3 * g:3 * g + 3] for g in range(3)]
        y_ref, lse_ref = refs[9], refs[10]
        acc_s, m_s, l_s = refs[11], refs[12], refs[13]
        masks = _head_masks(SPAN)
        for g, (d, nb) in enumerate(DIL_GROUPS):
            q_ref, k_ref, v_ref = qkv[g]

            def blk(n, _):
                start, prev, has_prev = _dil_block(n, d, nb)
                q = _dil_rows(q_ref, start, d)
                kc = jnp.concatenate([_dil_rows(k_ref, prev, d), _dil_rows(k_ref, start, d)], axis=0).astype(BF16)
                vc = jnp.concatenate([_dil_rows(v_ref, prev, d), _dil_rows(v_ref, start, d)], axis=0).astype(BF16)
                valid = _band_mask(has_prev)
                valid2 = jnp.concatenate([valid, valid], axis=0)
                q2 = (jnp.concatenate([jnp.where(masks[0], q, 0.0), jnp.where(masks[1], q, 0.0)], axis=0) * SCALE).astype(BF16)
                s = jnp.where(valid2, lax.dot_general(q2, kc, (((1,), (1,)), ((), ())), preferred_element_type=F32), NEG)
                m = jnp.max(s, axis=1, keepdims=True)
                pr = jnp.exp(s - m)
                l = jnp.sum(pr, axis=1, keepdims=True)
                acc = jnp.dot(pr.astype(BF16), vc, preferred_element_type=F32)
                _dil_store(acc_s.at[g], start, d, jnp.where(masks[0], acc[:SPAN], acc[SPAN:]))
                _dil_store(m_s.at[g], start, d, jnp.where(masks[0], m[:SPAN], m[SPAN:]))
                _dil_store(l_s.at[g], start, d, jnp.where(masks[0], l[:SPAN], l[SPAN:]))
                return 0

            lax.fori_loop(0, 16, blk, 0)

        def merge(i, _):
            rows = pl.ds(pl.multiple_of(i * 256, 256), 256)
            m = [m_s[g, rows, :] for g in range(3)]
            mx = jnp.maximum(jnp.maximum(m[0], m[1]), m[2])
            w = [jnp.exp(m[g] - mx) for g in range(3)]
            l = sum(l_s[g, rows, :] * w[g] for g in range(3))
            y_ref[rows, :] = sum(acc_s[g, rows, :] * w[g] for g in range(3)) / l
            lse_ref[rows, :] = mx + jnp.log(l)
            return 0

        lax.fori_loop(0, S // 256, merge, 0)

    def spec(g, t):
        return pl.BlockSpec((S, LANES), lambda p_: (0, (p_ * 3 + g) * 3 + t))

    return _hosted_call(
        body, comm, [p] * 9, name="dil_fwd", grid=(N_DIL_PAIRS,),
        in_specs=[spec(g, t) for g in range(3) for t in range(3)],
        out_specs=[pl.BlockSpec((S, LANES), lambda p_: (0, p_)), pl.BlockSpec((S, LANES), lambda p_: (0, p_))],
        out_shape=[_sds((S, 2 * LANES), F32), _sds((S, 2 * LANES), F32)],
        scratch_shapes=[pltpu.VMEM((3, S, LANES), F32)] * 3,
        vmem_mb=VMEM_MB)


def merge_fwd(ya_att, yb, p, w_bra, w_brb, comm=None):
    tm = 256
    gblk = LAY_G // D

    def body(a_ref, b_ref, ga_ref, gb_ref, wa_ref, wb_ref, mg_ref, ya_ref, yb_ref):
        ya = jnp.dot(a_ref[...], wa_ref[...], preferred_element_type=F32)
        ybp = jnp.dot(b_ref[...].astype(BF16), wb_ref[...], preferred_element_type=F32)
        mg_ref[...] = (_sigmoid(ga_ref[...]) * ya + _sigmoid(gb_ref[...]) * ybp).astype(BF16)
        ya_ref[...] = ya
        yb_ref[...] = ybp

    row = lambda w: pl.BlockSpec((tm, w), lambda i: (i, 0))
    return _hosted_call(
        body, comm, (ya_att, yb, p, p, w_bra, w_brb), name="merge_fwd", grid=(S // tm,),
        in_specs=[row(512), row(256), pl.BlockSpec((tm, D), lambda i: (i, gblk)), pl.BlockSpec((tm, D), lambda i: (i, gblk + 1)),
                  pl.BlockSpec((512, D), lambda i: (0, 0)), pl.BlockSpec((256, D), lambda i: (0, 0))],
        out_specs=[row(D), row(D), row(D)],
        out_shape=[_sds((S, D), BF16), _sds((S, D), F32), _sds((S, D), F32)])


def out_proj_fwd(merged, w_out, x, mod, g_ffn):
    tm = 256

    def body(a_ref, w_ref, x_ref, mod_ref, g_ref, mix_ref, x1_ref, h2_ref):
        mix = jnp.dot(a_ref[...], w_ref[...], preferred_element_type=F32)
        x1 = x_ref[...] + mod_ref[2:3, :] * mix
        r = lax.rsqrt(jnp.mean(x1 * x1, axis=1, keepdims=True) + EPS)
        h2 = (x1 * r * g_ref[...]) * (1.0 + mod_ref[4:5, :]) + mod_ref[3:4, :]
        mix_ref[...] = mix
        x1_ref[...] = x1
        h2_ref[...] = h2.astype(BF16)

    row = pl.BlockSpec((tm, D), lambda i: (i, 0))
    return pl.pallas_call(
        body, name="out_proj_fwd", grid=(S // tm,),
        in_specs=[row, pl.BlockSpec((D, D), lambda i: (0, 0)), row, pl.BlockSpec((8, D), lambda i: (0, 0)),
                  pl.BlockSpec((1, D), lambda i: (0, 0))],
        out_specs=[row, row, row],
        out_shape=[_sds((S, D), F32), _sds((S, D), F32), _sds((S, D), BF16)],
    )(merged, w_out, x, mod, g_ffn)


def ffn_up_fwd(h2, w_gate, w_up):
    tm = 1024
    nt = (((1,), (1,)), ((), ()))

    def body(h_ref, wg_ref, wu_ref, a_ref, u_ref, z_ref):
        h = h_ref[...]
        a = lax.dot_general(h, wg_ref[...], nt, preferred_element_type=F32)
        u = lax.dot_general(h, wu_ref[...], nt, preferred_element_type=F32)
        a_ref[...] = a
        u_ref[...] = u
        z_ref[...] = (a * _sigmoid(a) * u).astype(BF16)

    out = pl.BlockSpec((tm, FF_PAD), lambda k, i: (i, k))
    return pl.pallas_call(
        body, name="ffn_up_fwd", grid=(4, S // tm),
        in_specs=[pl.BlockSpec((tm, D), lambda k, i: (i, 0)), pl.BlockSpec((None, FF_PAD, D), lambda k, i: (k, 0, 0)),
                  pl.BlockSpec((None, FF_PAD, D), lambda k, i: (k, 0, 0))],
        out_specs=[out, out, out],
        out_shape=[_sds((S, FFP), F32), _sds((S, FFP), F32), _sds((S, FFP), BF16)], compiler_params=_params(VMEM_MB),
    )(h2, w_gate, w_up)


def ffn_down_loss(z, w_down, x1, mod, g_final, tgt):
    tm = 256

    def body(z_ref, w_ref, x1_ref, mod_ref, g_ref, t_ref, dx2_ref, dffn_ref, dg_ref, dga_ref, loss_ref, s_dg, s_dga, s_loss):
        i = pl.program_id(0)

        @pl.when(i == 0)
        def _():
            s_dg[...] = jnp.zeros_like(s_dg)
            s_dga[...] = jnp.zeros_like(s_dga)
            s_loss[...] = jnp.zeros_like(s_loss)

        ffn = jnp.dot(z_ref[...], w_ref[...], preferred_element_type=F32)
        gaf = mod_ref[5:6, :]
        x2 = x1_ref[...] + gaf * ffn
        r = lax.rsqrt(jnp.mean(x2 * x2, axis=1, keepdims=True) + EPS)
        xh = x2 * r
        g = g_ref[...]
        e = xh * g - t_ref[...]
        s_loss[...] += 0.5 * jnp.sum(jnp.mean(e * e, axis=1, keepdims=True), axis=0, keepdims=True)
        dy = e * (1.0 / D)
        gdy = dy * g
        dx2 = r * (gdy - xh * jnp.mean(gdy * xh, axis=1, keepdims=True))
        s_dg[...] += _colsum8(dy * xh)
        s_dga[...] += _colsum8(dx2 * ffn)
        dx2_ref[...] = dx2
        dffn_ref[...] = (dx2 * gaf).astype(BF16)

        @pl.when(i == pl.num_programs(0) - 1)
        def _():
            dg_ref[...] = jnp.sum(s_dg[...], axis=0, keepdims=True)
            dga_ref[...] = jnp.sum(s_dga[...], axis=0, keepdims=True)
            loss_ref[...] = jnp.broadcast_to(s_loss[...], (1, LANES))

    row = pl.BlockSpec((tm, D), lambda i: (i, 0))
    vec = pl.BlockSpec((1, D), lambda i: (0, 0))
    return pl.pallas_call(
        body, name="ffn_down_loss", grid=(S // tm,),
        in_specs=[pl.BlockSpec((tm, FFP), lambda i: (i, 0)), pl.BlockSpec((FFP, D), lambda i: (0, 0)), row,
                  pl.BlockSpec((8, D), lambda i: (0, 0)), vec, row],
        out_specs=[row, row, vec, vec, pl.BlockSpec((1, LANES), lambda i: (0, 0))],
        out_shape=[_sds((S, D), F32), _sds((S, D), BF16), _sds((1, D), F32), _sds((1, D), F32), _sds((1, LANES), F32)],
        scratch_shapes=[pltpu.VMEM((8, D), F32), pltpu.VMEM((8, D), F32), pltpu.VMEM((1, 1), F32)],
        compiler_params=_params(VMEM_MB),
    )(z, w_down, x1, mod, g_final, tgt)


def ffn_down_bwd(dffn, w_down, a, u, z):
    tm, tn = 1024, 768

    def body(d_ref, w_ref, a_ref, u_ref, z_ref, da_ref, du_ref, dw_ref):
        i = pl.program_id(1)
        dff = d_ref[...]
        dz = lax.dot_general(dff, w_ref[...], (((1,), (1,)), ((), ())), preferred_element_type=F32)
        av, uv = a_ref[...], u_ref[...]
        sg = _sigmoid(av)
        du_ref[...] = (dz * (av * sg)).astype(BF16)
        da_ref[...] = (dz * uv * (sg * (1.0 + av * (1.0 - sg)))).astype(BF16)
        dw = lax.dot_general(z_ref[...], dff, (((0,), (0,)), ((), ())), preferred_element_type=F32)

        @pl.when(i == 0)
        def _():
            dw_ref[...] = dw

        @pl.when(i > 0)
        def _():
            dw_ref[...] += dw

    tile = pl.BlockSpec((tm, tn), lambda j, i: (i, j))
    return pl.pallas_call(
        body, name="ffn_down_bwd", grid=(FFP // tn, S // tm),
        in_specs=[pl.BlockSpec((tm, D), lambda j, i: (i, 0)), pl.BlockSpec((tn, D), lambda j, i: (j, 0)), tile, tile, tile],
        out_specs=[tile, tile, pl.BlockSpec((tn, D), lambda j, i: (j, 0))],
        out_shape=[_sds((S, FFP), BF16), _sds((S, FFP), BF16), _sds((FFP, D), F32)], compiler_params=_params(VMEM_MB),
    )(dffn, w_down, a, u, z)


def mm_nt(dy, w, name, comm=None):
    tm = 1024
    n = dy.shape[1]
    if w.ndim == 2:
        k_in, tk = w.shape[0], 768
        w_spec = pl.BlockSpec((k_in, tk), lambda i, k: (0, k))
        dims = (((1,), (1,)), ((), ()))
    else:
        k_in, tk = w.shape[2], FF_PAD
        w_spec = pl.BlockSpec((None, tk, k_in), lambda i, k: (k, 0, 0))
        dims = (((1,), (0,)), ((), ()))
    nk = n // tk

    def body(d_ref, w_ref, o_ref, acc):
        k = pl.program_id(1)
        part = lax.dot_general(d_ref[...], w_ref[...], dims, preferred_element_type=F32)

        @pl.when(k == 0)
        def _():
            acc[...] = part

        @pl.when(k > 0)
        def _():
            acc[...] += part

        @pl.when(k == nk - 1)
        def _():
            o_ref[...] = acc[...]

    return _hosted_call(
        body, comm, (dy, w), name=name, grid=(S // tm, nk),
        in_specs=[pl.BlockSpec((tm, tk), lambda i, k: (i, k)), w_spec],
        out_specs=pl.BlockSpec((tm, k_in), lambda i, k: (i, 0)),
        out_shape=_sds((S, k_in), F32),
        scratch_shapes=[pltpu.VMEM((tm, k_in), F32)], vmem_mb=VMEM_MB)


def mm_tn(h, dy, name, shard_major=False, comm=None):
    tm, tn = 2048, 768
    k_in, n = h.shape[1], dy.shape[1]

    def body(h_ref, d_ref, o_ref):
        i = pl.program_id(1)
        ops = (d_ref[...], h_ref[...]) if shard_major else (h_ref[...], d_ref[...])
        dw = lax.dot_general(*ops, (((0,), (0,)), ((), ())), preferred_element_type=F32)

        @pl.when(i == 0)
        def _():
            o_ref[...] = dw

        @pl.when(i > 0)
        def _():
            o_ref[...] += dw

    if shard_major:
        out_spec, out_shape = pl.BlockSpec((None, tn, k_in), lambda j, i: (j, 0, 0)), _sds((n // tn, tn, k_in), F32)
    else:
        out_spec, out_shape = pl.BlockSpec((k_in, tn), lambda j, i: (0, j)), _sds((k_in, n), F32)
    return _hosted_call(
        body, comm, (h, dy), name=name, grid=(n // tn, S // tm),
        in_specs=[pl.BlockSpec((tm, k_in), lambda j, i: (i, 0)), pl.BlockSpec((tm, tn), lambda j, i: (i, j))],
        out_specs=out_spec, out_shape=out_shape, vmem_mb=VMEM_MB)


def mid_bwd(dh2a, dh2b, x1, dx2, mix, mod, g_ffn, p, ya, ybp, merged, ya_att, yb, w_out, w_bra, w_brb, comm=None):
    tm = 256
    gblk = LAY_G // D
    nsteps = S // tm

    def body(dha_ref, dhb_ref, x1_ref, dx2_ref, mix_ref, mod_ref, g_ref, ga_ref, gb_ref, ya_ref, yb_ref, mg_ref,
             att_ref, ybb_ref, wo_ref, wa_ref, wb_ref,
             dx1_ref, dpg_ref, datt_ref, dyb_ref, cs_ref, dwo_ref, dwa_ref, dwb_ref, s_cs):
        i = pl.program_id(0)

        @pl.when(i == 0)
        def _():
            s_cs[...] = jnp.zeros_like(s_cs)
            dwo_ref[...] = jnp.zeros_like(dwo_ref)
            dwa_ref[...] = jnp.zeros_like(dwa_ref)
            dwb_ref[...] = jnp.zeros_like(dwb_ref)

        x1 = x1_ref[...]
        g = g_ref[...]
        r = lax.rsqrt(jnp.mean(x1 * x1, axis=1, keepdims=True) + EPS)
        xh = x1 * r
        dh2 = dha_ref[...] + dhb_ref[...]
        s_cs[0] += _colsum8(dh2)
        s_cs[1] += _colsum8(dh2 * (xh * g))
        dn2 = dh2 * (1.0 + mod_ref[4:5, :])
        s_cs[2] += _colsum8(dn2 * xh)
        gd = dn2 * g
        dx1 = dx2_ref[...] + r * (gd - xh * jnp.mean(gd * xh, axis=1, keepdims=True))
        s_cs[3] += _colsum8(dx1 * mix_ref[...])
        dx1_ref[...] = dx1
        dmix = (dx1 * mod_ref[2:3, :]).astype(BF16)
        dmg = lax.dot_general(dmix, wo_ref[...], (((1,), (1,)), ((), ())), preferred_element_type=F32)
        sga, sgb = _sigmoid(ga_ref[...]), _sigmoid(gb_ref[...])
        dya = (dmg * sga).astype(BF16)
        dybp = (dmg * sgb).astype(BF16)
        dpg_ref[:, 0:D] = (dmg * ya_ref[...] * (sga * (1.0 - sga))).astype(BF16)
        dpg_ref[:, D:2 * D] = (dmg * yb_ref[...] * (sgb * (1.0 - sgb))).astype(BF16)
        datt_ref[...] = lax.dot_general(dya, wa_ref[...], (((1,), (1,)), ((), ())), preferred_element_type=F32).astype(BF16)
        dyb_ref[...] = lax.dot_general(dybp, wb_ref[...], (((1,), (1,)), ((), ())), preferred_element_type=F32)
        tn_dims = (((0,), (0,)), ((), ()))
        dwo_ref[...] += lax.dot_general(mg_ref[...], dmix, tn_dims, preferred_element_type=F32)
        dwa_ref[...] += lax.dot_general(att_ref[...], dya, tn_dims, preferred_element_type=F32)
        dwb_ref[...] += lax.dot_general(ybb_ref[...].astype(BF16), dybp, tn_dims, preferred_element_type=F32)

        @pl.when(i == nsteps - 1)
        def _():
            for t in range(4):
                cs_ref[t:t + 1, :] = jnp.sum(s_cs[t], axis=0, keepdims=True)
            cs_ref[4:8, :] = jnp.zeros((4, D), F32)

    row = lambda w: pl.BlockSpec((tm, w), lambda i: (i, 0))
    full = lambda a, b: pl.BlockSpec((a, b), lambda i: (0, 0))
    return _hosted_call(
        body, comm, (dh2a, dh2b, x1, dx2, mix, mod, g_ffn, p, p, ya, ybp, merged, ya_att, yb, w_out, w_bra, w_brb),
        name="mid_bwd", grid=(nsteps,),
        in_specs=[row(D), row(D), row(D), row(D), row(D), full(8, D), full(1, D),
                  pl.BlockSpec((tm, D), lambda i: (i, gblk)), pl.BlockSpec((tm, D), lambda i: (i, gblk + 1)),
                  row(D), row(D), row(D), row(512), row(256), full(D, D), full(512, D), full(256, D)],
        out_specs=[row(D), pl.BlockSpec((tm, 2 * D), lambda i: (i, LAY_G // (2 * D))), row(512), row(256), full(8, D),
                   full(D, D), full(512, D), full(256, D)],
        out_shape=[_sds((S, D), F32), _sds((S, LAY_N), BF16), _sds((S, 512), BF16), _sds((S, 256), F32), _sds((8, D), F32),
                   _sds((D, D), F32), _sds((512, D), F32), _sds((256, D), F32)],
        scratch_shapes=[pltpu.VMEM((4, 8, D), F32)],
        vmem_mb=VMEM_MB)


def fox_bwd(p, do, o, gcol, fcol, dp, comm=None):
    nq = S // FT
    nt = (((1,), (1,)), ((), ()))
    tn = (((0,), (0,)), ((), ()))

    def body(qkv_ref, do_ref, o_ref, g_ref, fc_ref, dp_in, dp_ref, df_ref, rs_ref, dq_s, qa_s, ka_s, dob_s, dl_s):
        del dp_in
        _fox_operands(qkv_ref, g_ref, fc_ref, qa_s, ka_s)
        masks = _head_masks(FT)
        lane = lax.broadcasted_iota(jnp.int32, (FT, LANES), 1)
        head0 = 2 * pl.program_id(0)
        causal = lax.broadcasted_iota(jnp.int32, (FT, FT), 1) <= lax.broadcasted_iota(jnp.int32, (FT, FT), 0)
        dq_s[...] = jnp.zeros_like(dq_s)
        rs_ref[...] = jnp.zeros_like(rs_ref)

        causal2 = jnp.concatenate([causal, causal], axis=0)

        def prep(i, _):
            r = pl.ds(pl.multiple_of(i * 256, 256), 256)
            m256 = _head_masks(256)
            dov, ov = do_ref[r, :].astype(F32), o_ref[r, :].astype(F32)
            for hh in range(2):
                dom = jnp.where(m256[hh], dov, 0.0)
                dob_s[hh, r, :] = dom.astype(BF16)
                dl_s[hh, r, :] = jnp.broadcast_to(jnp.sum(dom * ov, axis=1, keepdims=True), (256, LANES))
            return 0

        lax.fori_loop(0, S // 256, prep, 0)

        def stack(ref, q0, n, cols=slice(None)):
            return jnp.concatenate([ref[0, pl.ds(q0, n), cols], ref[1, pl.ds(q0, n), cols]], axis=0)

        def kloop(kb, _):
            k0 = pl.multiple_of(kb * FT, FT)
            k = qkv_ref[pl.ds(k0, FT), LANES:2 * LANES].astype(BF16)
            v = qkv_ref[pl.ds(k0, FT), 2 * LANES:3 * LANES].astype(BF16)
            ka = ka_s[pl.ds(k0, FT), :]

            def step(qi, carry, diagonal, tiles=1):
                dk, dv, df0, df1 = carry
                n = tiles * FT
                q0 = pl.multiple_of(qi * FT, FT)
                qa, dob = stack(qa_s, q0, n), stack(dob_s, q0, n)
                s = lax.dot_general(qa, ka, nt, preferred_element_type=F32)
                pr = jnp.exp(jnp.where(causal2, s, NEG)) if diagonal else jnp.exp(s)
                dpr = lax.dot_general(dob, v, nt, preferred_element_type=F32)
                ds = pr * (dpr - jnp.tile(stack(dl_s, q0, n), (1, FT // LANES)))
                dsb = ds.astype(BF16)
                dq = jnp.dot(dsb, k, preferred_element_type=F32) * SCALE
                dk = dk + lax.dot_general(dsb, qa[:, 0:LANES], tn, preferred_element_type=F32)
                dv = dv + lax.dot_general(pr.astype(BF16), dob, tn, preferred_element_type=F32)
                rsum = jnp.sum(ds, axis=1, keepdims=True)
                lane_n = lax.broadcasted_iota(jnp.int32, (n, LANES), 1)
                dq_s[pl.ds(q0, n), :] += jnp.where(lane_n < HD, dq[:n], dq[n:])
                rs_ref[pl.ds(q0, n), :] += jnp.where(lane_n == head0, rsum[:n], 0.0) + jnp.where(lane_n == head0 + 1, rsum[n:], 0.0)
                return (dk, dv, df0 - jnp.sum(ds[:n], axis=0, keepdims=True), df1 - jnp.sum(ds[n:], axis=0, keepdims=True))

            z = jnp.zeros((FT, LANES), F32)
            z1 = jnp.zeros((1, FT), F32)
            carry = step(kb, (z, z, z1, z1), True)
            pairs = (nq - 1 - kb) // 2
            carry = lax.fori_loop(0, pairs, lambda j, cr: step(kb + 1 + 2 * j, cr, False, 2), carry)
            dk, dv, df0, df1 = lax.fori_loop(kb + 1 + 2 * pairs, nq, lambda qi, cr: step(qi, cr, False), carry)
            dp_ref[pl.ds(k0, FT), LANES:2 * LANES] = dk.astype(BF16)
            dp_ref[pl.ds(k0, FT), 2 * LANES:3 * LANES] = dv.astype(BF16)
            df_ref[0:1, pl.ds(k0, FT)] = df0
            df_ref[1:2, pl.ds(k0, FT)] = df1
            return 0

        lax.fori_loop(0, S // FT, kloop, 0)
        dp_ref[:, 0:LANES] = dq_s[...].astype(BF16)

    a_blk = LAY_A // 384
    pair = pl.BlockSpec((S, LANES), lambda p_: (0, p_))
    heads = pl.BlockSpec((2, S, LANES), lambda p_: (p_, 0, 0))
    return _hosted_call(
        body, comm, (p, do, o, gcol, fcol, dp), name="fox_bwd", grid=(N_FOX_PAIRS,),
        in_specs=[pl.BlockSpec((S, 384), lambda p_: (0, a_blk + p_)), pair, pair, heads, heads, pl.BlockSpec(memory_space=pl.ANY)],
        out_specs=[pl.BlockSpec((S, 384), lambda p_: (0, a_blk + p_)), pl.BlockSpec((None, 2, S), lambda p_: (p_, 0, 0)),
                   pl.BlockSpec((None, S, LANES), lambda p_: (p_, 0, 0))],
        out_shape=[_sds((S, LAY_N), BF16), _sds((4, 2, S), F32), _sds((4, S, LANES), F32)],
        scratch_shapes=[pltpu.VMEM((S, LANES), F32), pltpu.VMEM((2, S, 2 * LANES), BF16), pltpu.VMEM((S, 2 * LANES), BF16),
                        pltpu.VMEM((2, S, LANES), BF16), pltpu.VMEM((2, S, LANES), F32)],
        aliases={5: 0}, vmem_mb=VMEM_MB)


def fgate_bwd(dfrow, dfcol, fraw, dp):
    def body(df_ref, dc_ref, f_ref, dp_in, dpf_ref, db_ref):
        del dp_in
        lane = lax.broadcasted_iota(jnp.int32, (8, S), 1)
        rsum = (dc_ref[0] + dc_ref[1]) + (dc_ref[2] + dc_ref[3])
        acc, sh = df_ref[...] + rsum.T[0:8, :], 1
        while sh < S:
            acc = acc + jnp.where(lane < S - sh, pltpu.roll(acc, S - sh, 1), 0.0)
            sh *= 2
        df = acc * _sigmoid(-f_ref[...])
        db_ref[...] = jnp.broadcast_to(jnp.sum(df, axis=1, keepdims=True), (8, LANES))
        dfc = jnp.concatenate([df, jnp.zeros((LANES - 8, S), F32)], axis=0).T
        dpf_ref[:, 0:LANES] = dfc.astype(BF16)
        dpf_ref[:, LANES:2 * LANES] = jnp.zeros((S, LANES), BF16)

    return pl.pallas_call(
        body, name="fgate_bwd", grid=(1,),
        in_specs=[pl.BlockSpec((8, S), lambda i: (0, 0)), pl.BlockSpec((4, S, LANES), lambda i: (0, 0, 0)),
                  pl.BlockSpec((8, S), lambda i: (0, 0)), pl.BlockSpec(memory_space=pl.ANY)],
        out_specs=[pl.BlockSpec((S, 2 * LANES), lambda i: (0, LAY_F // (2 * LANES))), pl.BlockSpec((8, LANES), lambda i: (0, 0))],
        out_shape=[_sds((S, LAY_N), BF16), _sds((8, LANES), F32)],
        input_output_aliases={3: 0},
        compiler_params=_params(VMEM_MB),
    )(dfrow, dfcol, fraw, dp)


def dil_bwd(p, dyb, yb, lse, tabs, dp, comm=None):
    def body(*refs):
        qkv = [refs[3 * g:3 * g + 3] for g in range(3)]
        dy_ref, y_ref, lse_ref, c_ref, s1_ref, s2_ref = refs[9:15]
        dp_ref = refs[16]
        dq_s, dk_s, dv_s, dl_s = refs[17:21]
        masks = _head_masks(SPAN)
        m256 = _head_masks(256)
        nt = (((1,), (1,)), ((), ()))
        tn = (((0,), (0,)), ((), ()))
        dk_s[...] = jnp.zeros_like(dk_s)
        dv_s[...] = jnp.zeros_like(dv_s)

        def prep(i, _):
            rows = pl.ds(pl.multiple_of(i * 256, 256), 256)
            pr = dy_ref[rows, :] * y_ref[rows, :]
            d0 = jnp.sum(jnp.where(m256[0], pr, 0.0), axis=1, keepdims=True)
            d1 = jnp.sum(jnp.where(m256[1], pr, 0.0), axis=1, keepdims=True)
            dl_s[rows, :] = jnp.where(m256[0], d0, d1)
            return 0

        lax.fori_loop(0, S // 256, prep, 0)

        for g, (d, nb) in enumerate(DIL_GROUPS):
            q_ref, k_ref, v_ref = qkv[g]

            def blk(n, _):
                start, prev, has_prev = _dil_block(n, d, nb)
                q = _dil_rows(q_ref, start, d)
                kc = jnp.concatenate([_dil_rows(k_ref, prev, d), _dil_rows(k_ref, start, d)], axis=0).astype(BF16)
                vc = jnp.concatenate([_dil_rows(v_ref, prev, d), _dil_rows(v_ref, start, d)], axis=0).astype(BF16)
                dov = _dil_rows(dy_ref, start, d)
                lsev = _dil_rows(lse_ref, start, d)
                dlv = _dil_rows(dl_s, start, d)
                valid = _band_mask(has_prev)
                valid2 = jnp.concatenate([valid, valid], axis=0)

                def stack(t):
                    return jnp.concatenate([jnp.where(masks[0], t, 0.0), jnp.where(masks[1], t, 0.0)], axis=0)

                def column(t):
                    return jnp.concatenate([jnp.max(jnp.where(masks[hh], t, NEG), axis=1, keepdims=True) for hh in range(2)], axis=0)

                q2 = (stack(q) * SCALE).astype(BF16)
                dob = stack(dov).astype(BF16)
                s = jnp.where(valid2, lax.dot_general(q2, kc, nt, preferred_element_type=F32), NEG)
                pr = jnp.exp(s - column(lsev))
                dpr = lax.dot_general(dob, vc, nt, preferred_element_type=F32)
                dsb = (pr * (dpr - column(dlv))).astype(BF16)
                dq = jnp.dot(dsb, kc, preferred_element_type=F32) * SCALE
                dkc = lax.dot_general(dsb, q2, tn, preferred_element_type=F32)
                dvc = lax.dot_general(pr.astype(BF16), dob, tn, preferred_element_type=F32)
                _dil_store(dq_s.at[g], start, d, jnp.where(masks[0], dq[:SPAN], dq[SPAN:]))
                for ref, val in ((dk_s.at[g], dkc), (dv_s.at[g], dvc)):
                    _dil_store(ref, prev, d, _dil_rows(ref, prev, d) + jnp.where(has_prev, val[0:SPAN], 0.0))
                    _dil_store(ref, start, d, _dil_rows(ref, start, d) + val[SPAN:])
                return 0

            lax.fori_loop(0, 16, blk, 0)

        def fin(i, _):
            rows = pl.ds(pl.multiple_of(i * 256, 256), 256)
            c, s1, s2 = c_ref[rows, :], s1_ref[rows, :], s2_ref[rows, :]
            for g in range(3):
                base = g * 384
                dp_ref[rows, base:base + LANES] = _rope_bwd(dq_s[g, rows, :], c, s1, s2).astype(BF16)
                dp_ref[rows, base + LANES:base + 2 * LANES] = _rope_bwd(dk_s[g, rows, :], c, s1, s2).astype(BF16)
                dp_ref[rows, base + 2 * LANES:base + 3 * LANES] = dv_s[g, rows, :].astype(BF16)
            return 0

        lax.fori_loop(0, S // 256, fin, 0)

    def spec(g, t):
        return pl.BlockSpec((S, LANES), lambda p_: (0, (p_ * 3 + g) * 3 + t))

    pair = pl.BlockSpec((S, LANES), lambda p_: (0, p_))
    tab = pl.BlockSpec((S, LANES), lambda p_: (0, 0))
    return _hosted_call(
        body, comm, [p] * 9 + [dyb, yb, lse, *tabs, dp], name="dil_bwd", grid=(N_DIL_PAIRS,),
        in_specs=[spec(g, t) for g in range(3) for t in range(3)] + [pair, pair, pair, tab, tab, tab, pl.BlockSpec(memory_space=pl.ANY)],
        out_specs=pl.BlockSpec((S, 1152), lambda p_: (0, p_)),
        out_shape=_sds((S, LAY_N), BF16),
        scratch_shapes=[pltpu.VMEM((3, S, LANES), F32)] * 3 + [pltpu.VMEM((S, LANES), F32)],
        aliases={15: 0}, vmem_mb=VMEM_MB)


def in_bwd_tail(dh1, x, dx1, mod, g_mix, comm=None):
    tm = 256
    nsteps = S // tm

    def body(dh_ref, x_ref, dx1_ref, mod_ref, g_ref, dx_ref, cs_ref, s_cs):
        i = pl.program_id(0)

        @pl.when(i == 0)
        def _():
            s_cs[...] = jnp.zeros_like(s_cs)

        xv, g, dh = x_ref[...], g_ref[...], dh_ref[...]
        r = lax.rsqrt(jnp.mean(xv * xv, axis=1, keepdims=True) + EPS)
        xh = xv * r
        s_cs[0] += _colsum8(dh)
        s_cs[1] += _colsum8(dh * (xh * g))
        dn = dh * (1.0 + mod_ref[1:2, :])
        s_cs[2] += _colsum8(dn * xh)
        gd = dn * g
        dx_ref[...] = dx1_ref[...] + r * (gd - xh * jnp.mean(gd * xh, axis=1, keepdims=True))

        @pl.when(i == nsteps - 1)
        def _():
            for t in range(3):
                cs_ref[t:t + 1, :] = jnp.sum(s_cs[t], axis=0, keepdims=True)
            cs_ref[3:8, :] = jnp.zeros((5, D), F32)

    row = pl.BlockSpec((tm, D), lambda i: (i, 0))
    return _hosted_call(
        body, comm, (dh1, x, dx1, mod, g_mix), name="in_bwd_tail", grid=(nsteps,),
        in_specs=[row, row, row, pl.BlockSpec((8, D), lambda i: (0, 0)), pl.BlockSpec((1, D), lambda i: (0, 0))],
        out_specs=[row, pl.BlockSpec((8, D), lambda i: (0, 0))],
        out_shape=[_sds((S, D), F32), _sds((8, D), F32)],
        scratch_shapes=[pltpu.VMEM((3, 8, D), F32)])


def _lay_pieces():
    out = []
    qa, ka, va, fa, qb, kb, vb, ga = 0, 512, 1024, 1536, 1544, 2312, 3080, 3848
    for p in range(N_DIL_PAIRS):
        for g in range(3):
            base = LAY_B + (p * 3 + g) * 384
            hd0 = (4 * g + 2 * p) * HD
            out += [(base, qb + hd0, LANES), (base + LANES, kb + hd0, LANES), (base + 2 * LANES, vb + hd0, LANES)]
    for p in range(N_FOX_PAIRS):
        base = LAY_A + p * 384
        out += [(base, qa + p * LANES, LANES), (base + LANES, ka + p * LANES, LANES), (base + 2 * LANES, va + p * LANES, LANES)]
    out.append((LAY_F, fa, 8))
    out.append((LAY_G, ga, 2 * D))
    return out


def _shard_runs():
    runs = []
    for lay, nat, width in _lay_pieces():
        while width:
            k, loc = nat // IN_SHARD, nat % IN_SHARD
            w = min(width, IN_SHARD - loc)
            runs.append((lay, k, loc, w))
            lay, nat, width = lay + w, nat + w, width - w
    return runs


def lay_from_shards(g):
    tm = 256

    def body(g_ref, o_ref):
        o_ref[:, LAY_F:LAY_G] = jnp.zeros((tm, LAY_G - LAY_F), g.dtype)
        for lay, k, loc, w in _shard_runs():
            o_ref[:, lay:lay + w] = g_ref[k, :, loc:loc + w]

    return pl.pallas_call(
        body, name="lay_from_shards", grid=(D // tm,),
        in_specs=[pl.BlockSpec((4, tm, IN_SHARD_PAD), lambda i: (0, i, 0))],
        out_specs=pl.BlockSpec((tm, LAY_N), lambda i: (i, 0)),
        out_shape=_sds((D, LAY_N), g.dtype), compiler_params=_params(VMEM_MB),
    )(g)


def shards_from_lay(dw_lay):
    tm = 256

    def body(x_ref, o_ref):
        o_ref[:, :, IN_SHARD:] = jnp.zeros((4, tm, IN_SHARD_PAD - IN_SHARD), F32)
        for lay, k, loc, w in _shard_runs():
            o_ref[k, :, loc:loc + w] = x_ref[:, lay:lay + w]

    return pl.pallas_call(
        body, name="shards_from_lay", grid=(D // tm,),
        in_specs=[pl.BlockSpec((tm, LAY_N), lambda i: (i, 0))],
        out_specs=pl.BlockSpec((4, tm, IN_SHARD_PAD), lambda i: (0, i, 0)),
        out_shape=_sds((4, D, IN_SHARD_PAD), F32), compiler_params=_params(VMEM_MB),
    )(dw_lay)


def _pos():
    return lax.axis_index("x"), lax.axis_index("y"), lax.axis_index("c")


def _other_chips(x, y):
    return [(1 - x, y), (x, 1 - y), (1 - x, 1 - y)]


def _remote(src, dst, send_sem, recv_sem, dev):
    return pltpu.make_async_remote_copy(src_ref=src, dst_ref=dst, send_sem=send_sem, recv_sem=recv_sem,
                                        device_id=dev, device_id_type=MESH)


VMEM_SPEC = pl.BlockSpec(memory_space=pltpu.VMEM)
ANY_SPEC = pl.BlockSpec(memory_space=pl.ANY)


def gather_all(v, name, with_sum):
    r = v.shape[0]

    def body(v_ref, out_ref, *rest):
        send_s, recv_s = rest[-2:]
        x, y, c = _pos()
        me = 4 * x + 2 * y + c
        out_ref[me] = v_ref[...]
        peers = []
        for m in range(1, 8):
            px = 1 - x if m & 4 else x
            py = 1 - y if m & 2 else y
            pc = 1 - c if m & 1 else c
            peers.append((px, py, pc))
        copies = [_remote(v_ref, out_ref.at[me], send_s.at[i], recv_s.at[i], dev) for i, dev in enumerate(peers)]
        for cp in copies:
            cp.start()
        for i, (px, py, pc) in enumerate(peers):
            _remote(v_ref, out_ref.at[4 * px + 2 * py + pc], send_s.at[i], recv_s.at[i], (px, py, pc)).wait_recv()
        for cp in copies:
            cp.wait_send()
        if with_sum:
            acc = out_ref[0]
            for b in range(1, 8):
                acc = acc + out_ref[b]
            rest[0][...] = acc

    out_shape = [_sds((8, r, LANES), F32)] + ([_sds((r, LANES), F32)] if with_sum else [])
    return pl.pallas_call(
        body, name=name, in_specs=[VMEM_SPEC], out_specs=[VMEM_SPEC] * len(out_shape), out_shape=out_shape,
        scratch_shapes=[pltpu.SemaphoreType.DMA((7,)), pltpu.SemaphoreType.DMA((7,))],
    )(v)


def mod_exchange(c_all, w_ada_sh, b_sh):
    def body(c_ref, w_ref, b_ref, out_ref, sc_ref, modp, send_s, recv_s):
        cv = c_ref[...]
        sc = cv * _sigmoid(cv)
        sc_ref[...] = sc
        modp[...] = jnp.dot(sc, w_ref[...], precision=lax.Precision.HIGHEST, preferred_element_type=F32) + b_ref[...]
        x, y, c = _pos()
        k = 2 * x + y
        out_ref[k] = modp[...]
        chips = _other_chips(x, y)
        copies = [_remote(modp, out_ref.at[k], send_s.at[j], recv_s.at[j], (cx, cy, c)) for j, (cx, cy) in enumerate(chips)]
        for cp in copies:
            cp.start()
        for j, (cx, cy) in enumerate(chips):
            _remote(modp, out_ref.at[2 * cx + cy], send_s.at[j], recv_s.at[j], (cx, cy, c)).wait_recv()
        for cp in copies:
            cp.wait_send()

    n = w_ada_sh.shape[1]
    return pl.pallas_call(
        body, name="mod_exchange", in_specs=[VMEM_SPEC] * 3, out_specs=[VMEM_SPEC] * 2,
        out_shape=[_sds((4, 8, n), F32), _sds((8, D), F32)],
        scratch_shapes=[pltpu.VMEM((8, n), F32), pltpu.SemaphoreType.DMA((3,)), pltpu.SemaphoreType.DMA((3,))],
        compiler_params=_params(VMEM_MB),
    )(c_all, w_ada_sh, b_sh)


def cast_into_slabs(ws, sizes, chip, comm):
    tr = 64
    n = len(ws)
    n_in = [w.shape[0] // tr for w in ws]
    n_out = [r // tr for r, _ in sizes]
    steps = max(n_out)
    nci, nco = len(comm.ins), len(comm.out_shapes)

    def body(chip_ref, *refs):
        del chip_ref
        w_refs, cin = refs[:n], refs[n:n + nci]
        o_refs, cout = refs[n + nci:2 * n + nci], refs[2 * n + nci:2 * n + nci + nco]
        sems = refs[2 * n + nci + nco:]
        i = pl.program_id(0)

        @pl.when(i == 0)
        def _():
            comm.start(cin, cout, sems)

        for a in range(n):
            c0, cols = ws[a].shape[1], sizes[a][1]

            @pl.when(i < n_in[a])
            def _(a=a, c0=c0, cols=cols):
                o_refs[a][:, 0:c0] = w_refs[a][...].astype(BF16)
                if cols > c0:
                    o_refs[a][:, c0:] = jnp.zeros((tr, cols - c0), BF16)

            if n_out[a] > n_in[a]:
                @pl.when((i >= n_in[a]) & (i < n_out[a]))
                def _(a=a, cols=cols):
                    o_refs[a][...] = jnp.zeros((tr, cols), BF16)

        @pl.when(i == steps - 1)
        def _():
            comm.wait(cin, cout, sems)

    any_spec = pl.BlockSpec(memory_space=pl.ANY)
    grid_spec = pltpu.PrefetchScalarGridSpec(
        num_scalar_prefetch=1, grid=(steps,),
        in_specs=[pl.BlockSpec((tr, w.shape[1]), functools.partial(lambda i, k, last: (jnp.minimum(i, last), 0), last=n_in[a] - 1))
                  for a, w in enumerate(ws)] + [any_spec] * nci,
        out_specs=[pl.BlockSpec((None, tr, sizes[a][1]), functools.partial(lambda i, k, last: (k[0], jnp.minimum(i, last), 0), last=n_out[a] - 1))
                   for a in range(n)] + [any_spec] * nco,
        scratch_shapes=[pltpu.SemaphoreType.DMA((s,)) for s in comm.sems])
    res = pl.pallas_call(
        body, name="cast_into_slabs", grid_spec=grid_spec,
        out_shape=[_sds((4,) + tuple(sz), BF16) for sz in sizes] + comm.out_shapes,
        input_output_aliases={1 + n + ci: n + co for ci, co in comm.aliases.items()},
    )(chip, *ws, *comm.ins)
    return list(res[:n]), list(res[n:])


def cast_into_slab(w, rows, cols, chip, tag):
    r0, c0 = w.shape
    tr = 256 if (r0 % 256 == 0 and rows % 256 == 0) else 64
    n_in, n_out = r0 // tr, rows // tr

    def body(chip_ref, w_ref, o_ref):
        del chip_ref
        i = pl.program_id(0)

        @pl.when(i < n_in)
        def _():
            o_ref[:, 0:c0] = w_ref[...].astype(BF16)
            if cols > c0:
                o_ref[:, c0:] = jnp.zeros((tr, cols - c0), BF16)

        @pl.when(i >= n_in)
        def _():
            o_ref[...] = jnp.zeros((tr, cols), BF16)

    grid_spec = pltpu.PrefetchScalarGridSpec(
        num_scalar_prefetch=1, grid=(n_out,),
        in_specs=[pl.BlockSpec((tr, c0), lambda i, k: (jnp.minimum(i, n_in - 1), 0))],
        out_specs=pl.BlockSpec((None, tr, cols), lambda i, k: (k[0], i, 0)))
    return pl.pallas_call(body, name="cast_" + tag, grid_spec=grid_spec, out_shape=_sds((4, rows, cols), BF16))(chip, w)


def _row_tile(rows, cap=256):
    t = cap
    while rows % t or t % 8:
        t -= 8
    return t


def _comm_wait(sends, recvs, local=()):
    for cp in recvs:
        cp.wait_recv()
    for cp in sends:
        cp.wait_send()
    for cp in local:
        cp.wait()


def ag_ici(bufs):
    n = len(bufs)

    def copies(ins, outs, sems):
        send_s, recv_s = sems
        x, y, c = _pos()
        k = 2 * x + y
        sends, recvs = [], []
        for a in range(n):
            half = outs[a].shape[1] // 2
            rows = pl.ds(c * half, half)
            for j, (cx, cy) in enumerate(_other_chips(x, y)):
                sem = (send_s.at[3 * a + j], recv_s.at[3 * a + j], (cx, cy, c))
                sends.append(_remote(outs[a].at[k, rows], outs[a].at[k, rows], *sem))
                recvs.append(_remote(outs[a].at[k, rows], outs[a].at[2 * cx + cy, rows], *sem))
        return sends, recvs

    def start(ins, outs, sems):
        for cp in copies(ins, outs, sems)[0]:
            cp.start()

    def wait(ins, outs, sems):
        _comm_wait(*copies(ins, outs, sems))

    return Comm(bufs, [_sds(b.shape, b.dtype) for b in bufs], [3 * n, 3 * n], start, wait, aliases={a: a for a in range(n)})


def ag_d2d(bufs):
    n = len(bufs)

    def copies(ins, outs, sems):
        send_s, recv_s = sems
        x, y, c = _pos()
        sends, recvs = [], []
        for a in range(n):
            half = outs[a].shape[1] // 2
            rows, orows = pl.ds(c * half, half), pl.ds((1 - c) * half, half)
            for j, (cx, cy) in enumerate(_other_chips(x, y)):
                kj = 2 * cx + cy
                sem = (send_s.at[3 * a + j], recv_s.at[3 * a + j], (x, y, 1 - c))
                sends.append(_remote(outs[a].at[kj, rows], outs[a].at[kj, rows], *sem))
                recvs.append(_remote(outs[a].at[kj, orows], outs[a].at[kj, orows], *sem))
        return sends, recvs

    def start(ins, outs, sems):
        for cp in copies(ins, outs, sems)[0]:
            cp.start()

    def wait(ins, outs, sems):
        _comm_wait(*copies(ins, outs, sems))

    return Comm(bufs, [_sds(b.shape, b.dtype) for b in bufs], [3 * n, 3 * n], start, wait, aliases={a: a for a in range(n)})


def rs_a(grads):
    n = len(grads)

    def copies(ins, outs, sems):
        send_s, recv_s = sems
        x, y, c = _pos()
        cps = []
        for a in range(n):
            half = ins[a].shape[1] // 2
            cps.append(_remote(ins[a].at[:, pl.ds((1 - c) * half, half), :], outs[a], send_s.at[a], recv_s.at[a], (x, y, 1 - c)))
        return cps

    def start(ins, outs, sems):
        for cp in copies(ins, outs, sems):
            cp.start()

    def wait(ins, outs, sems):
        cps = copies(ins, outs, sems)
        _comm_wait(cps, cps)

    return Comm(grads, [_sds((4, g.shape[1] // 2, g.shape[2]), g.dtype) for g in grads], [n, n], start, wait)


def rs_b(pres):
    n = len(pres)

    def copies(ins, outs, sems):
        send_s, recv_s = sems
        x, y, c = _pos()
        cps = []
        for a in range(n):
            for j, (cx, cy) in enumerate(_other_chips(x, y)):
                cps.append(_remote(ins[a].at[2 * cx + cy], outs[a].at[j], send_s.at[3 * a + j], recv_s.at[3 * a + j], (cx, cy, c)))
        return cps

    def start(ins, outs, sems):
        for cp in copies(ins, outs, sems):
            cp.start()

    def wait(ins, outs, sems):
        cps = copies(ins, outs, sems)
        _comm_wait(cps, cps)

    return Comm(pres, [_sds((3,) + p_.shape[1:], p_.dtype) for p_ in pres], [3 * n, 3 * n], start, wait)


def rs_b_rows(pre, buf, lo, n):
    def copies(ins, outs, sems):
        send_s, recv_s = sems
        x, y, c = _pos()
        rows = pl.ds(lo, n)
        return [_remote(ins[0].at[2 * cx + cy, rows], outs[0].at[j, rows], send_s.at[j], recv_s.at[j], (cx, cy, c))
                for j, (cx, cy) in enumerate(_other_chips(x, y))]

    def start(ins, outs, sems):
        for cp in copies(ins, outs, sems):
            cp.start()

    def wait(ins, outs, sems):
        cps = copies(ins, outs, sems)
        _comm_wait(cps, cps)

    ins = [pre] if buf is None else [pre, buf]
    return Comm(ins, [_sds((3,) + pre.shape[1:], pre.dtype)], [3, 3], start, wait, aliases={} if buf is None else {1: 0})


def rs_c(reds):
    n = len(reds)

    def copies(ins, outs, sems):
        send_s, recv_s = sems
        x, y, c = _pos()
        sends, recvs = [], []
        for a in range(n):
            half = outs[a].shape[0] // 2
            rows, orows = pl.ds(c * half, half), pl.ds((1 - c) * half, half)
            sem = (send_s.at[a], recv_s.at[a], (x, y, 1 - c))
            sends.append(_remote(outs[a].at[rows], outs[a].at[rows], *sem))
            recvs.append(_remote(outs[a].at[orows], outs[a].at[orows], *sem))
        return sends, recvs

    def start(ins, outs, sems):
        for cp in copies(ins, outs, sems)[0]:
            cp.start()

    def wait(ins, outs, sems):
        _comm_wait(*copies(ins, outs, sems))

    return Comm(reds, [_sds(r_.shape, r_.dtype) for r_ in reds], [n, n], start, wait, aliases={a: a for a in range(n)})


def comm_join(*comms):
    ni = np.cumsum([0] + [len(c.ins) for c in comms])
    no = np.cumsum([0] + [len(c.out_shapes) for c in comms])
    ns = np.cumsum([0] + [len(c.sems) for c in comms])

    def parts(ins, outs, sems):
        return [(c, ins[ni[i]:ni[i + 1]], outs[no[i]:no[i + 1]], sems[ns[i]:ns[i + 1]]) for i, c in enumerate(comms)]

    def start(ins, outs, sems):
        for c, a, b, s in parts(ins, outs, sems):
            c.start(a, b, s)

    def wait(ins, outs, sems):
        for c, a, b, s in parts(ins, outs, sems):
            c.wait(a, b, s)

    aliases = {int(ni[i]) + k: int(no[i]) + v for i, c in enumerate(comms) for k, v in c.aliases.items()}
    return Comm(sum((c.ins for c in comms), []), sum((c.out_shapes for c in comms), []), sum((c.sems for c in comms), []),
                start, wait, aliases)


def comm_only(comm, name):
    nci, nco = len(comm.ins), len(comm.out_shapes)

    def body(*refs):
        ins, outs, sems = refs[:nci], refs[nci:nci + nco], refs[nci + nco:]
        comm.start(ins, outs, sems)
        comm.wait(ins, outs, sems)

    return pl.pallas_call(
        body, name=name, in_specs=[ANY_SPEC] * nci, out_specs=[ANY_SPEC] * nco, out_shape=comm.out_shapes,
        scratch_shapes=[pltpu.SemaphoreType.DMA((s,)) for s in comm.sems],
        input_output_aliases=comm.aliases,
    )(*comm.ins)


def rs_add_halves(g, other, core, name):
    _, r, cdim = g.shape
    half = r // 2
    tr = _row_tile(half, 256)
    nb = half // tr

    def body(core_ref, g_ref, o_ref, out_ref):
        del core_ref
        out_ref[...] = (g_ref[...] + o_ref[...]).astype(BF16)

    grid_spec = pltpu.PrefetchScalarGridSpec(
        num_scalar_prefetch=1, grid=(4, nb),
        in_specs=[pl.BlockSpec((None, tr, cdim), lambda k, i, cr: (k, cr[0] * nb + i, 0)),
                  pl.BlockSpec((None, tr, cdim), lambda k, i, cr: (k, i, 0))],
        out_specs=pl.BlockSpec((None, tr, cdim), lambda k, i, cr: (k, i, 0)))
    return pl.pallas_call(body, name=name, grid_spec=grid_spec, out_shape=_sds((4, half, cdim), BF16))(core, g, other)


def rs_add_slabs(t, pre, place, name):
    _, half, cdim = t.shape
    tr = _row_tile(half, 256)
    nb = half // tr

    def body(place_ref, own_ref, t_ref, out_ref):
        del place_ref
        out_ref[...] = ((own_ref[...].astype(F32) + t_ref[0].astype(F32)) + t_ref[1].astype(F32)) + t_ref[2].astype(F32)

    grid_spec = pltpu.PrefetchScalarGridSpec(
        num_scalar_prefetch=1, grid=(nb,),
        in_specs=[pl.BlockSpec((None, tr, cdim), lambda i, pr: (pr[0], i, 0)), pl.BlockSpec((3, tr, cdim), lambda i, pr: (0, i, 0))],
        out_specs=pl.BlockSpec((tr, cdim), lambda i, pr: (pr[1] * nb + i, 0)))
    return pl.pallas_call(body, name=name, grid_spec=grid_spec, out_shape=_sds((2 * half, cdim), F32))(place, pre, t)


def _adam_math(w, g, m, v):
    m = ADAM_B1 * m + (1.0 - ADAM_B1) * g
    v = ADAM_B2 * v + (1.0 - ADAM_B2) * (g * g)
    m_hat = m / (1.0 - ADAM_B1 ** ADAM_STEP)
    v_hat = v / (1.0 - ADAM_B2 ** ADAM_STEP)
    delta = -ADAM_LR * (m_hat / (jnp.sqrt(v_hat) + ADAM_EPS) + ADAM_WD * w)
    return delta, m, v


def adam(w, g, m, v, name, comm=None):
    r, cdim = w.shape
    tr = _row_tile(r) if r >= 8 else r

    def body(w_ref, g_ref, m_ref, v_ref, g_out, d_ref, nm_ref, nv_ref):
        gv = g_ref[:, :cdim]
        g_out[...] = gv
        d_ref[...], nm_ref[...], nv_ref[...] = _adam_math(w_ref[...], gv, m_ref[...], v_ref[...])

    blk = pl.BlockSpec((tr, cdim), lambda i: (i, 0))
    return _hosted_call(
        body, comm, (w, g, m, v), name=name, grid=(r // tr,),
        in_specs=[blk, pl.BlockSpec((tr, g.shape[1]), lambda i: (i, 0)), blk, blk],
        out_specs=[blk] * 4, out_shape=[_sds((r, cdim), F32)] * 4)


def adam_small(groups):
    n = len(groups)

    def body(*refs):
        ins, outs = refs[:4 * n], refs[4 * n:]
        for i in range(n):
            w_ref, m_ref, v_ref, g_ref = ins[4 * i:4 * i + 4]
            d_ref, nm_ref, nv_ref = outs[3 * i:3 * i + 3]
            d_ref[...], nm_ref[...], nv_ref[...] = _adam_math(w_ref[...], g_ref[...], m_ref[...], v_ref[...])

    flat = [t for grp in groups for t in grp]
    out = pl.pallas_call(
        body, name="adam_small", in_specs=[VMEM_SPEC] * (4 * n), out_specs=[VMEM_SPEC] * (3 * n),
        out_shape=[_sds(grp[0].shape, F32) for grp in groups for _ in range(3)],
    )(*flat)
    return [out[3 * i:3 * i + 3] for i in range(n)]


def adam_w_ada(sc_t, dmod_sh, w, m, v, comm=None):
    r, cdim = w.shape
    tr = 256

    def body(s_ref, d_ref, w_ref, m_ref, v_ref, g_ref, dl_ref, nm_ref, nv_ref):
        g = jnp.dot(s_ref[...], d_ref[...], precision=lax.Precision.HIGHEST, preferred_element_type=F32)
        g_ref[...] = g
        dl_ref[...], nm_ref[...], nv_ref[...] = _adam_math(w_ref[...], g, m_ref[...], v_ref[...])

    blk = pl.BlockSpec((tr, cdim), lambda i: (i, 0))
    return _hosted_call(
        body, comm, (sc_t, dmod_sh, w, m, v), name="adam_w_ada", grid=(r // tr,),
        in_specs=[pl.BlockSpec((tr, LANES), lambda i: (i, 0)), pl.BlockSpec((LANES, cdim), lambda i: (0, 0)), blk, blk, blk],
        out_specs=[blk] * 4, out_shape=[_sds((r, cdim), F32)] * 4)


SMALL_ROWS = 80


def kernel(x, c, w_ada, b_ada, g_mix, w_in, b_fgate, w_br_a, w_br_b, w_out, g_ffn, w_ffn_gate, w_ffn_up, w_ffn_down, g_final, loss_target, m_w_ada, m_b_ada, m_g_mix, m_w_in, m_b_fgate, m_w_br_a, m_w_br_b, m_w_out, m_g_ffn, m_w_ffn_gate, m_w_ffn_up, m_w_ffn_down, m_g_final, v_w_ada, v_b_ada, v_g_mix, v_w_in, v_b_fgate, v_w_br_a, v_w_br_b, v_w_out, v_g_ffn, v_w_ffn_gate, v_w_ffn_up, v_w_ffn_down, v_g_final):
    xi, yi, ci = _pos()
    chip = 2 * xi + yi
    seq = 4 * xi + 2 * yi + ci
    n_ada = w_ada.shape[2]

    c_all = gather_all(c.reshape(8, LANES), "gather_c", False)[0].reshape(8, D)
    b_sh = lax.dynamic_slice(b_ada, (0, chip * n_ada), (1, n_ada))
    mod_all, sc = mod_exchange(c_all, w_ada[0], b_sh)
    mod = lax.dynamic_index_in_dim(mod_all, seq, axis=1, keepdims=False).reshape(6, D)
    mod8 = jnp.pad(mod, ((0, 2), (0, 0)))

    core = ci.astype(jnp.int32).reshape(1)
    chip1 = chip.astype(jnp.int32).reshape(1)
    place = jnp.stack([chip, ci]).astype(jnp.int32)
    s_in = cast_into_slab(w_in[0], D, IN_SHARD_PAD, chip1, "w_in")
    (s_bra, s_brb, s_out, s_gate, s_up, s_down), (g_in,) = cast_into_slabs(
        [w_br_a[0], w_br_b[0], w_out[0], w_ffn_gate[0].T, w_ffn_up[0].T, w_ffn_down[0]],
        [(512, 256), (256, 256), (256, D), (FF_PAD, D), (FF_PAD, D), (FF_PAD, D)], chip1, ag_ici([s_in]))
    xs, tgt, g_fin = x[0], loss_target[0], g_final.reshape(1, D)

    def halves(gs, others, tag):
        return [rs_add_halves(g, o, core, f"rs_{tag}_halves_{i}") for i, (g, o) in enumerate(zip(gs, others))]

    def slab_sums(ts, pres, tag):
        return [rs_add_slabs(t, pre, place, f"rs_{tag}_slabs_{i}") for i, (t, pre) in enumerate(zip(ts, pres))]

    tabs = rope_tables()
    h1, (g_in,) = norm_mod_fwd(xs, g_mix, mod8, 0, 1, comm=ag_d2d([g_in]))
    w_lay = lay_from_shards(g_in)
    p, mix_w = in_proj_fwd(h1, w_lay, tabs, comm=ag_ici([s_bra, s_brb, s_out]))
    fraw, fcol = fgate_fwd(p, jnp.pad(b_fgate, ((0, 0), (0, LANES - 8))))
    (ya_att, gcol), res = fox_fwd(p, fcol, comm=comm_join(ag_d2d(mix_w), ag_ici([s_gate, s_up])))
    g_bra, g_brb, g_out = res[:3]
    (yb, lse_b), res = dil_fwd(p, comm=comm_join(ag_d2d(res[3:]), ag_ici([s_down])))
    w_gate, w_up = res[:2]
    w_bra = g_bra.transpose(1, 0, 2).reshape(512, D)
    w_brb = g_brb.transpose(1, 0, 2).reshape(256, D)
    w_o = g_out.reshape(D, D)
    (merged, ya, ybp), (g_down,) = merge_fwd(ya_att, yb, p, w_bra, w_brb, comm=ag_d2d(res[2:]))
    w_down = g_down.reshape(FFP, D)
    mix, x1, h2 = out_proj_fwd(merged, w_o, xs, mod8, g_ffn)
    a, u, z = ffn_up_fwd(h2, w_gate, w_up)
    dx2, dffn, dg_final, dga_f, loss_part = ffn_down_loss(z, w_down, x1, mod8, g_fin, tgt)

    da, du, dw_down = ffn_down_bwd(dffn, w_down, a, u, z)
    g_down = [dw_down.reshape(4, FF_PAD, D)]
    dh2a, oth = mm_nt(da, w_gate, "ffn_gate_dx", comm=rs_a(g_down))
    pre_down = halves(g_down, oth, "down")
    dh2b, _ = mm_nt(du, w_up, "ffn_up_dx")
    dw_gate, _ = mm_tn(h2, da, "ffn_gate_dw", shard_major=True)
    dw_up, _ = mm_tn(h2, du, "ffn_up_dw", shard_major=True)
    g_gu = [dw_gate, dw_up]
    (dx1, dp1, dya_att, dyb, cs_mid, dw_out, dw_bra, dw_brb), res = mid_bwd(
        dh2a, dh2b, x1, dx2, mix, mod8, g_ffn, p, ya, ybp, merged, ya_att, yb, w_o, w_bra, w_brb,
        comm=comm_join(rs_b(pre_down), rs_a(g_gu)))
    red_down = slab_sums(res[:1], pre_down, "down")
    pre_gu = halves(g_gu, res[1:], "gu")
    g_mix3 = [dw_bra.reshape(512, 4, 256).transpose(1, 0, 2), dw_brb.reshape(256, 4, 256).transpose(1, 0, 2), dw_out.reshape(4, 256, D)]
    (dp2, dfrow, dfcol), res = fox_bwd(p, dya_att, ya_att, gcol, fcol, dp1,
                                       comm=comm_join(rs_b(pre_gu), rs_c(red_down), rs_a(g_mix3)))
    red_gu = slab_sums(res[:2], pre_gu, "gu")
    r_down = res[2]
    pre_mix3 = halves(g_mix3, res[3:], "mix")
    dp3, db_fg = fgate_bwd(dfrow.reshape(8, S), dfcol, fraw, dp2)
    dp4, res = dil_bwd(p, dyb, yb, lse_b, tabs, dp3, comm=comm_join(rs_c(red_gu), rs_b(pre_mix3)))
    r_gate, r_up = res[:2]
    red_mix3 = slab_sums(res[2:], pre_mix3, "mix")
    dw_lay, (r_bra, r_brb, r_out) = mm_tn(h1, dp4, "in_proj_dw", comm=rs_c(red_mix3))
    g_in4 = [shards_from_lay(dw_lay)]
    dh1, oth = mm_nt(dp4, w_lay, "in_proj_dx", comm=rs_a(g_in4))
    (pre_in,) = halves(g_in4, oth, "in")
    qrows = pre_in.shape[1] // 4
    (dx, cs_in), (t_in,) = in_bwd_tail(dh1, xs, dx1, mod8, g_mix, comm=rs_b_rows(pre_in, None, 0, qrows))

    dmod = jnp.concatenate([cs_in[0:2], cs_mid[3:4], cs_mid[0:2], dga_f], axis=0)
    small = dict(dmod=dmod, dg_mix=cs_in[2:3], dg_ffn=cs_mid[2:3], dg_final=dg_final, db_fgate=db_fg[:, 0], loss=loss_part[0, 0])
    sv = jnp.concatenate([
        small["dmod"].reshape(48, LANES), small["dg_mix"].reshape(8, LANES), small["dg_ffn"].reshape(8, LANES),
        small["dg_final"].reshape(8, LANES), jnp.pad(small["db_fgate"], (0, LANES - 8)).reshape(1, LANES),
        jnp.broadcast_to(small["loss"], (1, LANES)), jnp.zeros((SMALL_ROWS - 74, LANES), F32)], axis=0)
    sv_all, sv_sum = gather_all(sv, "gather_small", True)
    loss = sv_sum[73, 0]
    g_small = dict(b_ada=sv_sum[0:48].reshape(1, 6 * D), g_mix=sv_sum[48:56].reshape(1, D), g_ffn=sv_sum[56:64].reshape(1, D),
                   g_final=sv_sum[64:72].reshape(D), b_fgate=sv_sum[72, 0:8].reshape(1, 8))

    dmod_all = lax.dynamic_slice(sv_all[:, 0:48, :].reshape(8, 6 * D), (0, chip * n_ada), (8, n_ada))
    (g_ada, d_ada, nm_ada, nv_ada), (t_in,) = adam_w_ada(
        jnp.pad(sc.T, ((0, 0), (0, LANES - 8))), jnp.pad(dmod_all, ((0, LANES - 8), (0, 0))), w_ada[0], m_w_ada[0], v_w_ada[0],
        comm=rs_b_rows(pre_in, t_in, qrows, 3 * qrows))

    big = dict(w_in=(w_in, m_w_in, v_w_in), w_br_a=(w_br_a, m_w_br_a, v_w_br_a), w_br_b=(w_br_b, m_w_br_b, v_w_br_b),
               w_out=(w_out, m_w_out, v_w_out), w_ffn_gate=(w_ffn_gate, m_w_ffn_gate, v_w_ffn_gate),
               w_ffn_up=(w_ffn_up, m_w_ffn_up, v_w_ffn_up), w_ffn_down=(w_ffn_down, m_w_ffn_down, v_w_ffn_down))
    gpad = dict(w_br_a=r_bra, w_br_b=r_brb, w_out=r_out, w_ffn_gate=r_gate, w_ffn_up=r_up, w_ffn_down=r_down)
    upd = {}
    for nm in ("w_ffn_gate", "w_ffn_up"):
        w, m, v = big[nm]
        upd[nm] = [t.T for t in adam(w[0].T, gpad[nm], m[0].T, v[0].T, "adam_" + nm)[0]]
    (gpad["w_in"],) = comm_only(rs_c(slab_sums([t_in], [pre_in], "in")), "rs_in_share")
    for nm, (w, m, v) in big.items():
        if nm not in upd:
            upd[nm] = adam(w[0], gpad[nm], m[0], v[0], "adam_" + nm)[0]

    small_names = ["g_mix", "g_ffn", "g_final", "b_ada", "b_fgate"]
    small_w = dict(g_mix=(g_mix, m_g_mix, v_g_mix), g_ffn=(g_ffn, m_g_ffn, v_g_ffn), g_final=(g_final, m_g_final, v_g_final),
                   b_ada=(b_ada, m_b_ada, v_b_ada), b_fgate=(b_fgate, m_b_fgate, v_b_fgate))
    row = lambda t: t.reshape(1, -1)
    res = adam_small([[row(t) for t in small_w[nm]] + [row(g_small[nm])] for nm in small_names])
    small_upd = [{nm: res[i][which].reshape(small_w[nm][0].shape) for i, nm in enumerate(small_names)} for which in range(3)]
    order =["w_ada", "b_ada", "g_mix", "w_in", "b_fgate", "w_br_a", "w_br_b", "w_out", "g_ffn", "w_ffn_gate", "w_ffn_up", "w_ffn_down", "g_final"]

    def leaf(nm, which):
        if nm == "w_ada":
            return (g_ada, d_ada, nm_ada, nv_ada)[which][None]
        if nm in big:
            return upd[nm][which][None]
        return g_small[nm] if which == 0 else small_upd[which - 1][nm]

    outs = [loss, dx[None]]
    for which in range(4):
        outs += [leaf(nm, which) for nm in order]
    return tuple(outs)
```

```python
import functools

import numpy as np
import jax
import jax.numpy as jnp
from jax import lax
from jax.experimental import pallas as pl
from jax.experimental.pallas import tpu as pltpu

F32, BF16 = jnp.float32, jnp.bfloat16
S, D = 2048, 1024
HD = 64
LANES = 128
N_FOX_PAIRS, N_DIL_PAIRS = 4, 2
DIL_GROUPS = ((1, 16), (4, 4), (16, 1))
SPAN = 128
ROT_DIM, ROPE_THETA = 16, 500000.0
D_FF, FF_SHARD, FF_PAD = 2816, 704, 768
FFP = 4 * FF_PAD
IN_COLS, IN_SHARD, IN_SHARD_PAD = 5896, 1474, 1536
LAY_B, LAY_A, LAY_F, LAY_G, LAY_N = 0, 2304, 3840, 4096, 6144
EPS, NEG = 1e-6, -1e30
SCALE = HD ** -0.5
ADAM_LR, ADAM_B1, ADAM_B2, ADAM_EPS, ADAM_WD, ADAM_STEP = 0.001, 0.9, 0.999, 1e-08, 0.01, 10
VMEM_MB = 56
MESH = pl.DeviceIdType.MESH


def _params(vmem_mb=None, **kw):
    if vmem_mb is not None:
        kw["vmem_limit_bytes"] = vmem_mb * 1024 * 1024
    return pltpu.CompilerParams(**kw)


def _sds(shape, dtype):
    return jax.ShapeDtypeStruct(shape, dtype)


def _sigmoid(x):
    return 1.0 / (1.0 + jnp.exp(-x))


def _colsum8(x):
    tm, n = x.shape
    return jnp.sum(x.reshape(tm // 8, 8, n), axis=0)


class Comm:
    def __init__(self, ins, out_shapes, sems, start, wait, aliases=None):
        self.ins, self.out_shapes, self.sems = list(ins), list(out_shapes), list(sems)
        self.start, self.wait, self.aliases = start, wait, dict(aliases or {})


def _hosted_call(body, comm, args, *, name, grid, in_specs, out_specs, out_shape, scratch_shapes=(), aliases=None, vmem_mb=None):
    single = not isinstance(out_shape, (list, tuple))
    out_specs_l = [out_specs] if single else list(out_specs)
    out_shape_l = [out_shape] if single else list(out_shape)
    n_in, n_out, n_scr = len(in_specs), len(out_shape_l), len(scratch_shapes)
    aliases = dict(aliases or {})
    if comm is None:
        res = pl.pallas_call(body, name=name, grid=grid, in_specs=list(in_specs), out_specs=out_specs, out_shape=out_shape,
                             scratch_shapes=list(scratch_shapes), input_output_aliases=aliases,
                             compiler_params=_params(vmem_mb))(*args)
        return res, []
    nci, nco = len(comm.ins), len(comm.out_shapes)

    def wrapped(*refs):
        main_in, cin = refs[:n_in], refs[n_in:n_in + nci]
        o0 = n_in + nci
        main_out, cout = refs[o0:o0 + n_out], refs[o0 + n_out:o0 + n_out + nco]
        s0 = o0 + n_out + nco
        scr, sems = refs[s0:s0 + n_scr], refs[s0 + n_scr:]
        ids = [pl.program_id(i) for i in range(len(grid))]
        first = functools.reduce(jnp.logical_and, [i == 0 for i in ids])
        last = functools.reduce(jnp.logical_and, [i == g - 1 for i, g in zip(ids, grid)])

        @pl.when(first)
        def _():
            comm.start(cin, cout, sems)

        body(*main_in, *main_out, *scr)

        @pl.when(last)
        def _():
            comm.wait(cin, cout, sems)

    for ci, co in comm.aliases.items():
        aliases[n_in + ci] = n_out + co
    any_spec = pl.BlockSpec(memory_space=pl.ANY)
    res = pl.pallas_call(
        wrapped, name=name, grid=grid, in_specs=list(in_specs) + [any_spec] * nci, out_specs=out_specs_l + [any_spec] * nco,
        out_shape=out_shape_l + comm.out_shapes,
        scratch_shapes=list(scratch_shapes) + [pltpu.SemaphoreType.DMA((s,)) for s in comm.sems],
        input_output_aliases=aliases, compiler_params=_params(vmem_mb))(*args, *comm.ins)
    main = list(res[:n_out])
    return (main[0] if single else main), list(res[n_out:])


def norm_mod_fwd(x, g, mod, shift_row, scale_row, comm=None):
    tm = 256

    def body(x_ref, g_ref, mod_ref, h_ref):
        xv = x_ref[...]
        r = lax.rsqrt(jnp.mean(xv * xv, axis=1, keepdims=True) + EPS)
        n = xv * r * g_ref[...]
        h = n * (1.0 + mod_ref[scale_row:scale_row + 1, :]) + mod_ref[shift_row:shift_row + 1, :]
        h_ref[...] = h.astype(BF16)

    return _hosted_call(
        body, comm, (x, g, mod), name="norm_mod_fwd", grid=(S // tm,),
        in_specs=[pl.BlockSpec((tm, D), lambda i: (i, 0)), pl.BlockSpec((1, D), lambda i: (0, 0)),
                  pl.BlockSpec((8, D), lambda i: (0, 0))],
        out_specs=pl.BlockSpec((tm, D), lambda i: (i, 0)),
        out_shape=_sds((S, D), BF16))


def rope_tables():
    pos = jnp.arange(S, dtype=F32)
    inv_freq = ROPE_THETA ** (-jnp.arange(0, ROT_DIM, 2, dtype=F32) / ROT_DIM)
    ang = pos[:, None] * inv_freq[None, :]
    cos, sin = jnp.cos(ang), jnp.sin(ang)
    one, zero = jnp.ones((S, HD - ROT_DIM), F32), jnp.zeros((S, HD - ROT_DIM), F32)
    z8 = jnp.zeros((S, 8), F32)
    c = jnp.concatenate([cos, cos, one], axis=1)
    s1 = jnp.concatenate([-sin, z8, zero], axis=1)
    s2 = jnp.concatenate([z8, sin, zero], axis=1)
    return tuple(jnp.concatenate([t, t], axis=1) for t in (c, s1, s2))


def _rope(y, c, s1, s2):
    return y * c + pltpu.roll(y, LANES - 8, 1) * s1 + pltpu.roll(y, 8, 1) * s2


def _rope_bwd(dy, c, s1, s2):
    return dy * c + pltpu.roll(dy * s1, 8, 1) + pltpu.roll(dy * s2, LANES - 8, 1)


def in_proj_fwd(h, w_lay, tabs, comm=None):
    tm, tn = 2048, 768
    n_rope = N_DIL_PAIRS * 3 // 2

    def body(a_ref, w_ref, c_ref, s1_ref, s2_ref, o_ref):
        j = pl.program_id(0)
        y = jnp.dot(a_ref[...], w_ref[...], preferred_element_type=F32)

        @pl.when(j < n_rope)
        def _():
            c, s1, s2 = c_ref[...], s1_ref[...], s2_ref[...]
            for t in range(tn // LANES):
                chunk = y[:, LANES * t:LANES * (t + 1)]
                o_ref[:, LANES * t:LANES * (t + 1)] = chunk if t % 3 == 2 else _rope(chunk, c, s1, s2)

        @pl.when(j >= n_rope)
        def _():
            o_ref[...] = y

    tab = pl.BlockSpec((tm, LANES), lambda j, i: (i, 0))
    return _hosted_call(
        body, comm, (h, w_lay, *tabs), name="in_proj_fwd", grid=(LAY_N // tn, S // tm),
        in_specs=[pl.BlockSpec((tm, D), lambda j, i: (i, 0)), pl.BlockSpec((D, tn), lambda j, i: (0, j)), tab, tab, tab],
        out_specs=pl.BlockSpec((tm, tn), lambda j, i: (i, j)),
        out_shape=_sds((S, LAY_N), F32), vmem_mb=VMEM_MB)


def _log1p_small(t):
    return jnp.where(t < 1e-2, t * (1.0 - t * (0.5 - t * (1.0 / 3.0))), jnp.log(1.0 + t))


def fgate_fwd(p, b_pad):
    def body(fa_ref, b_ref, fraw_ref, fcol_ref):
        f = fa_ref[...] + b_ref[...]
        fr = f.T[0:8, :]
        ls = jnp.minimum(fr, 0.0) - _log1p_small(jnp.exp(-jnp.abs(fr)))
        lane = lax.broadcasted_iota(jnp.int32, (8, S), 1)
        acc, sh = ls, 1
        while sh < S:
            acc = acc + jnp.where(lane >= sh, pltpu.roll(acc, sh, 1), 0.0)
            sh *= 2
        fraw_ref[...] = fr
        for hh in range(8):
            fcol_ref[hh] = jnp.broadcast_to(acc[hh:hh + 1, :], (LANES, S)).T

    return pl.pallas_call(
        body, name="fgate_fwd", grid=(1,),
        in_specs=[pl.BlockSpec((S, LANES), lambda i: (0, LAY_F // LANES)), pl.BlockSpec((1, LANES), lambda i: (0, 0))],
        out_specs=[pl.BlockSpec((8, S), lambda i: (0, 0)), pl.BlockSpec((8, S, LANES), lambda i: (0, 0, 0))],
        out_shape=[_sds((8, S), F32), _sds((8, S, LANES), F32)],
        compiler_params=_params(VMEM_MB),
    )(p, b_pad)


def _head_masks(rows):
    lane = lax.broadcasted_iota(jnp.int32, (rows, LANES), 1)
    return lane < HD, lane >= HD


FT = 256


def _split3(f):
    hi = f.astype(BF16).astype(F32)
    r = f - hi
    mid = r.astype(BF16).astype(F32)
    return hi, mid, r - mid


def _fox_operands(qkv_ref, tcol_ref, scol_ref, qa_s, ka_s):
    rows = 256
    lane = lax.broadcasted_iota(jnp.int32, (rows, LANES), 1)

    def chunk(i, _):
        r = pl.ds(pl.multiple_of(i * rows, rows), rows)
        q, k = qkv_ref[r, 0:LANES], qkv_ref[r, LANES:2 * LANES]
        s0, s1 = _split3(scol_ref[0, r, :]), _split3(scol_ref[1, r, :])
        ka = jnp.where(lane == 0, -s0[0], jnp.where(lane == 1, -s0[1], jnp.where(lane == 2, -s0[2], jnp.where(
            lane == 3, -s1[0], jnp.where(lane == 4, -s1[1], jnp.where(lane == 5, -s1[2], jnp.where(lane < 9, 1.0, 0.0)))))))
        ka_s[r, 0:LANES] = k.astype(BF16)
        ka_s[r, LANES:2 * LANES] = ka.astype(BF16)
        for hh in range(2):
            own = (lane < HD) if hh == 0 else (lane >= HD)
            t3 = _split3(tcol_ref[hh, r, :])
            ones = (lane >= 3 * hh) & (lane < 3 * hh + 3)
            qa = jnp.where(ones, 1.0, jnp.where(lane == 6, t3[0], jnp.where(lane == 7, t3[1], jnp.where(lane == 8, t3[2], 0.0))))
            qa_s[hh, r, 0:LANES] = jnp.where(own, q * SCALE, 0.0).astype(BF16)
            qa_s[hh, r, LANES:2 * LANES] = qa.astype(BF16)
        return 0

    lax.fori_loop(0, S // rows, chunk, 0)


def fox_fwd(p, fcol, comm=None):
    nt = (((1,), (1,)), ((), ()))

    def body(qkv_ref, fc_ref, o_ref, g_ref, qa_s, ka_s):
        _fox_operands(qkv_ref, fc_ref, fc_ref, qa_s, ka_s)
        masks = _head_masks(FT)
        causal = lax.broadcasted_iota(jnp.int32, (FT, FT), 1) <= lax.broadcasted_iota(jnp.int32, (FT, FT), 0)
        causal2 = jnp.concatenate([causal, causal], axis=0)

        def qloop(qi, _):
            q0 = pl.multiple_of(qi * FT, FT)
            qa = jnp.concatenate([qa_s[0, pl.ds(q0, FT), :], qa_s[1, pl.ds(q0, FT), :]], axis=0)

            def step(kb, carry, diagonal, tiles=1):
                m, l, acc = carry
                k0 = pl.multiple_of(kb * FT, FT)
                keys = pl.ds(k0, tiles * FT)
                v = qkv_ref[keys, 2 * LANES:3 * LANES].astype(BF16)
                s = lax.dot_general(qa, ka_s[keys, :], nt, preferred_element_type=F32)
                if diagonal:
                    s = jnp.where(causal2, s, NEG)
                m_new = jnp.maximum(m, jnp.max(s, axis=1, keepdims=True))
                pr = jnp.exp(s - m_new)
                alpha = jnp.exp(m - m_new)
                return (m_new, l * alpha + jnp.sum(pr, axis=1, keepdims=True),
                        acc * alpha + jnp.dot(pr.astype(BF16), v, preferred_element_type=F32))

            init = (jnp.full((2 * FT, 1), NEG, F32), jnp.zeros((2 * FT, 1), F32), jnp.zeros((2 * FT, LANES), F32))
            pairs = qi // 2
            carry = lax.fori_loop(0, pairs, lambda j, cr: step(2 * j, cr, False, 2), init)
            carry = lax.fori_loop(2 * pairs, qi, lambda kb, cr: step(kb, cr, False), carry)
            m, l, acc = step(qi, carry, True)
            out = acc / l
            lse = m + jnp.log(l)
            o_ref[pl.ds(q0, FT), :] = jnp.where(masks[0], out[:FT], out[FT:]).astype(BF16)
            g_ref[0, pl.ds(q0, FT), :] = fc_ref[0, pl.ds(q0, FT), :] - lse[:FT]
            g_ref[1, pl.ds(q0, FT), :] = fc_ref[1, pl.ds(q0, FT), :] - lse[FT:]
            return 0

        lax.fori_loop(0, S // FT, qloop, 0)

    a_blk = LAY_A // 384
    return _hosted_call(
        body, comm, (p, fcol), name="fox_fwd", grid=(N_FOX_PAIRS,),
        in_specs=[pl.BlockSpec((S, 384), lambda p_: (0, a_blk + p_)), pl.BlockSpec((2, S, LANES), lambda p_: (p_, 0, 0))],
        out_specs=[pl.BlockSpec((S, LANES), lambda p_: (0, p_)), pl.BlockSpec((2, S, LANES), lambda p_: (p_, 0, 0))],
        out_shape=[_sds((S, 4 * LANES), BF16), _sds((8, S, LANES), F32)],
        scratch_shapes=[pltpu.VMEM((2, S, 2 * LANES), BF16), pltpu.VMEM((S, 2 * LANES), BF16)],
        vmem_mb=VMEM_MB)


def _dil_rows(ref, start, d):
    return ref[pl.ds(start, SPAN), :] if d == 1 else ref[pl.ds(start, SPAN, stride=d), :]


def _dil_store(ref, start, d, val):
    if d == 1:
        ref[pl.ds(start, SPAN), :] = val
    else:
        ref[pl.ds(start, SPAN, stride=d), :] = val


def _band_mask(has_prev):
    qi = lax.broadcasted_iota(jnp.int32, (SPAN, 2 * SPAN), 0) + SPAN
    kj = lax.broadcasted_iota(jnp.int32, (SPAN, 2 * SPAN), 1)
    dist = qi - kj
    return (dist >= 0) & (dist <= SPAN) & (has_prev | (kj >= SPAN))


def _dil_block(n, d, nb):
    r, j = n // nb, n % nb
    start = r + d * SPAN * j
    prev = jnp.maximum(start - d * SPAN, r)
    return start, prev, j > 0


def dil_fwd(p, comm=None):
    def body(*refs):
        qkv = [refs[3 * g:3 * g + 3] for g in range(3)]
        y_ref, lse_ref = refs[9], refs[10]
        acc_s, m_s, l_s = refs[11], refs[12], refs[13]
        masks = _head_masks(SPAN)
        for g, (d, nb) in enumerate(DIL_GROUPS):
            q_ref, k_ref, v_ref = qkv[g]

            def blk(n, _):
                start, prev, has_prev = _dil_block(n, d, nb)
                q = _dil_rows(q_ref, start, d)
                kc = jnp.concatenate([_dil_rows(k_ref, prev, d), _dil_rows(k_ref, start, d)], axis=0).astype(BF16)
                vc = jnp.concatenate([_dil_rows(v_ref, prev, d), _dil_rows(v_ref, start, d)], axis=0).astype(BF16)
                valid = _band_mask(has_prev)
                valid2 = jnp.concatenate([valid, valid], axis=0)
                q2 = (jnp.concatenate([jnp.where(masks[0], q, 0.0), jnp.where(masks[1], q, 0.0)], axis=0) * SCALE).astype(BF16)
                s = jnp.where(valid2, lax.dot_general(q2, kc, (((1,), (1,)), ((), ())), preferred_element_type=F32), NEG)
                m = jnp.max(s, axis=1, keepdims=True)
                pr = jnp.exp(s - m)
                l = jnp.sum(pr, axis=1, keepdims=True)
                acc = jnp.dot(pr.astype(BF16), vc, preferred_element_type=F32)
                _dil_store(acc_s.at[g], start, d, jnp.where(masks[0], acc[:SPAN], acc[SPAN:]))
                _dil_store(m_s.at[g], start, d, jnp.where(masks[0], m[:SPAN], m[SPAN:]))
                _dil_store(l_s.at[g], start, d, jnp.where(masks[0], l[:SPAN], l[SPAN:]))
                return 0

            lax.fori_loop(0, 16, blk, 0)

        def merge(i, _):
            rows = pl.ds(pl.multiple_of(i * 256, 256), 256)
            m = [m_s[g, rows, :] for g in range(3)]
            mx = jnp.maximum(jnp.maximum(m[0], m[1]), m[2])
            w = [jnp.exp(m[g] - mx) for g in range(3)]
            l = sum(l_s[g, rows, :] * w[g] for g in range(3))
            y_ref[rows, :] = sum(acc_s[g, rows, :] * w[g] for g in range(3)) / l
            lse_ref[rows, :] = mx + jnp.log(l)
            return 0

        lax.fori_loop(0, S // 256, merge, 0)

    def spec(g, t):
        return pl.BlockSpec((S, LANES), lambda p_: (0, (p_ * 3 + g) * 3 + t))

    return _hosted_call(
        body, comm, [p] * 9, name="dil_fwd", grid=(N_DIL_PAIRS,),
        in_specs=[spec(g, t) for g in range(3) for t in range(3)],
        out_specs=[pl.BlockSpec((S, LANES), lambda p_: (0, p_)), pl.BlockSpec((S, LANES), lambda p_: (0, p_))],
        out_shape=[_sds((S, 2 * LANES), F32), _sds((S, 2 * LANES), F32)],
        scratch_shapes=[pltpu.VMEM((3, S, LANES), F32)] * 3,
        vmem_mb=VMEM_MB)


def merge_fwd(ya_att, yb, p, w_bra, w_brb, comm=None):
    tm = 256
    gblk = LAY_G // D

    def body(a_ref, b_ref, ga_ref, gb_ref, wa_ref, wb_ref, mg_ref, ya_ref, yb_ref):
        ya = jnp.dot(a_ref[...], wa_ref[...], preferred_element_type=F32)
        ybp = jnp.dot(b_ref[...].astype(BF16), wb_ref[...], preferred_element_type=F32)
        mg_ref[...] = (_sigmoid(ga_ref[...]) * ya + _sigmoid(gb_ref[...]) * ybp).astype(BF16)
        ya_ref[...] = ya
        yb_ref[...] = ybp

    row = lambda w: pl.BlockSpec((tm, w), lambda i: (i, 0))
    return _hosted_call(
        body, comm, (ya_att, yb, p, p, w_bra, w_brb), name="merge_fwd", grid=(S // tm,),
        in_specs=[row(512), row(256), pl.BlockSpec((tm, D), lambda i: (i, gblk)), pl.BlockSpec((tm, D), lambda i: (i, gblk + 1)),
                  pl.BlockSpec((512, D), lambda i: (0, 0)), pl.BlockSpec((256, D), lambda i: (0, 0))],
        out_specs=[row(D), row(D), row(D)],
        out_shape=[_sds((S, D), BF16), _sds((S, D), F32), _sds((S, D), F32)])


def out_proj_fwd(merged, w_out, x, mod, g_ffn):
    tm = 256

    def body(a_ref, w_ref, x_ref, mod_ref, g_ref, mix_ref, x1_ref, h2_ref):
        mix = jnp.dot(a_ref[...], w_ref[...], preferred_element_type=F32)
        x1 = x_ref[...] + mod_ref[2:3, :] * mix
        r = lax.rsqrt(jnp.mean(x1 * x1, axis=1, keepdims=True) + EPS)
        h2 = (x1 * r * g_ref[...]) * (1.0 + mod_ref[4:5, :]) + mod_ref[3:4, :]
        mix_ref[...] = mix
        x1_ref[...] = x1
        h2_ref[...] = h2.astype(BF16)

    row = pl.BlockSpec((tm, D), lambda i: (i, 0))
    return pl.pallas_call(
        body, name="out_proj_fwd", grid=(S // tm,),
        in_specs=[row, pl.BlockSpec((D, D), lambda i: (0, 0)), row, pl.BlockSpec((8, D), lambda i: (0, 0)),
                  pl.BlockSpec((1, D), lambda i: (0, 0))],
        out_specs=[row, row, row],
        out_shape=[_sds((S, D), F32), _sds((S, D), F32), _sds((S, D), BF16)],
    )(merged, w_out, x, mod, g_ffn)


def ffn_up_fwd(h2, w_gate, w_up):
    tm = 1024
    nt = (((1,), (1,)), ((), ()))

    def body(h_ref, wg_ref, wu_ref, a_ref, u_ref, z_ref):
        h = h_ref[...]
        a = lax.dot_general(h, wg_ref[...], nt, preferred_element_type=F32)
        u = lax.dot_general(h, wu_ref[...], nt, preferred_element_type=F32)
        a_ref[...] = a
        u_ref[...] = u
        z_ref[...] = (a * _sigmoid(a) * u).astype(BF16)

    out = pl.BlockSpec((tm, FF_PAD), lambda k, i: (i, k))
    return pl.pallas_call(
        body, name="ffn_up_fwd", grid=(4, S // tm),
        in_specs=[pl.BlockSpec((tm, D), lambda k, i: (i, 0)), pl.BlockSpec((None, FF_PAD, D), lambda k, i: (k, 0, 0)),
                  pl.BlockSpec((None, FF_PAD, D), lambda k, i: (k, 0, 0))],
        out_specs=[out, out, out],
        out_shape=[_sds((S, FFP), F32), _sds((S, FFP), F32), _sds((S, FFP), BF16)], compiler_params=_params(VMEM_MB),
    )(h2, w_gate, w_up)


def ffn_down_loss(z, w_down, x1, mod, g_final, tgt):
    tm = 256

    def body(z_ref, w_ref, x1_ref, mod_ref, g_ref, t_ref, dx2_ref, dffn_ref, dg_ref, dga_ref, loss_ref, s_dg, s_dga, s_loss):
        i = pl.program_id(0)

        @pl.when(i == 0)
        def _():
            s_dg[...] = jnp.zeros_like(s_dg)
            s_dga[...] = jnp.zeros_like(s_dga)
            s_loss[...] = jnp.zeros_like(s_loss)

        ffn = jnp.dot(z_ref[...], w_ref[...], preferred_element_type=F32)
        gaf = mod_ref[5:6, :]
        x2 = x1_ref[...] + gaf * ffn
        r = lax.rsqrt(jnp.mean(x2 * x2, axis=1, keepdims=True) + EPS)
        xh = x2 * r
        g = g_ref[...]
        e = xh * g - t_ref[...]
        s_loss[...] += 0.5 * jnp.sum(jnp.mean(e * e, axis=1, keepdims=True), axis=0, keepdims=True)
        dy = e * (1.0 / D)
        gdy = dy * g
        dx2 = r * (gdy - xh * jnp.mean(gdy * xh, axis=1, keepdims=True))
        s_dg[...] += _colsum8(dy * xh)
        s_dga[...] += _colsum8(dx2 * ffn)
        dx2_ref[...] = dx2
        dffn_ref[...] = (dx2 * gaf).astype(BF16)

        @pl.when(i == pl.num_programs(0) - 1)
        def _():
            dg_ref[...] = jnp.sum(s_dg[...], axis=0, keepdims=True)
            dga_ref[...] = jnp.sum(s_dga[...], axis=0, keepdims=True)
            loss_ref[...] = jnp.broadcast_to(s_loss[...], (1, LANES))

    row = pl.BlockSpec((tm, D), lambda i: (i, 0))
    vec = pl.BlockSpec((1, D), lambda i: (0, 0))
    return pl.pallas_call(
        body, name="ffn_down_loss", grid=(S // tm,),
        in_specs=[pl.BlockSpec((tm, FFP), lambda i: (i, 0)), pl.BlockSpec((FFP, D), lambda i: (0, 0)), row,
                  pl.BlockSpec((8, D), lambda i: (0, 0)), vec, row],
        out_specs=[row, row, vec, vec, pl.BlockSpec((1, LANES), lambda i: (0, 0))],
        out_shape=[_sds((S, D), F32), _sds((S, D), BF16), _sds((1, D), F32), _sds((1, D), F32), _sds((1, LANES), F32)],
        scratch_shapes=[pltpu.VMEM((8, D), F32), pltpu.VMEM((8, D), F32), pltpu.VMEM((1, 1), F32)],
        compiler_params=_params(VMEM_MB),
    )(z, w_down, x1, mod, g_final, tgt)


def ffn_down_bwd(dffn, w_down, a, u, z):
    tm, tn = 1024, 768

    def body(d_ref, w_ref, a_ref, u_ref, z_ref, da_ref, du_ref, dw_ref):
        i = pl.program_id(1)
        dff = d_ref[...]
        dz = lax.dot_general(dff, w_ref[...], (((1,), (1,)), ((), ())), preferred_element_type=F32)
        av, uv = a_ref[...], u_ref[...]
        sg = _sigmoid(av)
        du_ref[...] = (dz * (av * sg)).astype(BF16)
        da_ref[...] = (dz * uv * (sg * (1.0 + av * (1.0 - sg)))).astype(BF16)
        dw = lax.dot_general(z_ref[...], dff, (((0,), (0,)), ((), ())), preferred_element_type=F32)

        @pl.when(i == 0)
        def _():
            dw_ref[...] = dw

        @pl.when(i > 0)
        def _():
            dw_ref[...] += dw

    tile = pl.BlockSpec((tm, tn), lambda j, i: (i, j))
    return pl.pallas_call(
        body, name="ffn_down_bwd", grid=(FFP // tn, S // tm),
        in_specs=[pl.BlockSpec((tm, D), lambda j, i: (i, 0)), pl.BlockSpec((tn, D), lambda j, i: (j, 0)), tile, tile, tile],
        out_specs=[tile, tile, pl.BlockSpec((tn, D), lambda j, i: (j, 0))],
        out_shape=[_sds((S, FFP), BF16), _sds((S, FFP), BF16), _sds((FFP, D), F32)], compiler_params=_params(VMEM_MB),
    )(dffn, w_down, a, u, z)


def mm_nt(dy, w, name, comm=None):
    tm = 1024
    n = dy.shape[1]
    if w.ndim == 2:
        k_in, tk = w.shape[0], 1536
        w_spec = pl.BlockSpec((k_in, tk), lambda i, k: (0, k))
        dims = (((1,), (1,)), ((), ()))
    else:
        k_in, tk = w.shape[2], FF_PAD
        w_spec = pl.BlockSpec((None, tk, k_in), lambda i, k: (k, 0, 0))
        dims = (((1,), (0,)), ((), ()))
    nk = n // tk

    def body(d_ref, w_ref, o_ref, acc):
        k = pl.program_id(1)
        part = lax.dot_general(d_ref[...], w_ref[...], dims, preferred_element_type=F32)

        @pl.when(k == 0)
        def _():
            acc[...] = part

        @pl.when(k > 0)
        def _():
            acc[...] += part

        @pl.when(k == nk - 1)
        def _():
            o_ref[...] = acc[...]

    return _hosted_call(
        body, comm, (dy, w), name=name, grid=(S // tm, nk),
        in_specs=[pl.BlockSpec((tm, tk), lambda i, k: (i, k)), w_spec],
        out_specs=pl.BlockSpec((tm, k_in), lambda i, k: (i, 0)),
        out_shape=_sds((S, k_in), F32),
        scratch_shapes=[pltpu.VMEM((tm, k_in), F32)], vmem_mb=VMEM_MB)


def mm_tn(h, dy, name, shard_major=False, comm=None):
    tm, tn = 2048, 768
    k_in, n = h.shape[1], dy.shape[1]

    def body(h_ref, d_ref, o_ref):
        i = pl.program_id(1)
        ops = (d_ref[...], h_ref[...]) if shard_major else (h_ref[...], d_ref[...])
        dw = lax.dot_general(*ops, (((0,), (0,)), ((), ())), preferred_element_type=F32)

        @pl.when(i == 0)
        def _():
            o_ref[...] = dw

        @pl.when(i > 0)
        def _():
            o_ref[...] += dw

    if shard_major:
        out_spec, out_shape = pl.BlockSpec((None, tn, k_in), lambda j, i: (j, 0, 0)), _sds((n // tn, tn, k_in), F32)
    else:
        out_spec, out_shape = pl.BlockSpec((k_in, tn), lambda j, i: (0, j)), _sds((k_in, n), F32)
    return _hosted_call(
        body, comm, (h, dy), name=name, grid=(n // tn, S // tm),
        in_specs=[pl.BlockSpec((tm, k_in), lambda j, i: (i, 0)), pl.BlockSpec((tm, tn), lambda j, i: (i, j))],
        out_specs=out_spec, out_shape=out_shape, vmem_mb=VMEM_MB)


def mid_bwd(dh2a, dh2b, x1, dx2, mix, mod, g_ffn, p, ya, ybp, merged, ya_att, yb, w_out, w_bra, w_brb, comm=None):
    tm = 256
    gblk = LAY_G // D
    nsteps = S // tm

    def body(dha_ref, dhb_ref, x1_ref, dx2_ref, mix_ref, mod_ref, g_ref, ga_ref, gb_ref, ya_ref, yb_ref, mg_ref,
             att_ref, ybb_ref, wo_ref, wa_ref, wb_ref,
             dx1_ref, dpg_ref, datt_ref, dyb_ref, cs_ref, dwo_ref, dwa_ref, dwb_ref, s_cs):
        i = pl.program_id(0)

        @pl.when(i == 0)
        def _():
            s_cs[...] = jnp.zeros_like(s_cs)
            dwo_ref[...] = jnp.zeros_like(dwo_ref)
            dwa_ref[...] = jnp.zeros_like(dwa_ref)
            dwb_ref[...] = jnp.zeros_like(dwb_ref)

        x1 = x1_ref[...]
        g = g_ref[...]
        r = lax.rsqrt(jnp.mean(x1 * x1, axis=1, keepdims=True) + EPS)
        xh = x1 * r
        dh2 = dha_ref[...] + dhb_ref[...]
        s_cs[0] += _colsum8(dh2)
        s_cs[1] += _colsum8(dh2 * (xh * g))
        dn2 = dh2 * (1.0 + mod_ref[4:5, :])
        s_cs[2] += _colsum8(dn2 * xh)
        gd = dn2 * g
        dx1 = dx2_ref[...] + r * (gd - xh * jnp.mean(gd * xh, axis=1, keepdims=True))
        s_cs[3] += _colsum8(dx1 * mix_ref[...])
        dx1_ref[...] = dx1
        dmix = (dx1 * mod_ref[2:3, :]).astype(BF16)
        dmg = lax.dot_general(dmix, wo_ref[...], (((1,), (1,)), ((), ())), preferred_element_type=F32)
        sga, sgb = _sigmoid(ga_ref[...]), _sigmoid(gb_ref[...])
        dya = (dmg * sga).astype(BF16)
        dybp = (dmg * sgb).astype(BF16)
        dpg_ref[:, 0:D] = (dmg * ya_ref[...] * (sga * (1.0 - sga))).astype(BF16)
        dpg_ref[:, D:2 * D] = (dmg * yb_ref[...] * (sgb * (1.0 - sgb))).astype(BF16)
        datt_ref[...] = lax.dot_general(dya, wa_ref[...], (((1,), (1,)), ((), ())), preferred_element_type=F32).astype(BF16)
        dyb_ref[...] = lax.dot_general(dybp, wb_ref[...], (((1,), (1,)), ((), ())), preferred_element_type=F32)
        tn_dims = (((0,), (0,)), ((), ()))
        dwo_ref[...] += lax.dot_general(mg_ref[...], dmix, tn_dims, preferred_element_type=F32)
        dwa_ref[...] += lax.dot_general(att_ref[...], dya, tn_dims, preferred_element_type=F32)
        dwb_ref[...] += lax.dot_general(ybb_ref[...].astype(BF16), dybp, tn_dims, preferred_element_type=F32)

        @pl.when(i == nsteps - 1)
        def _():
            for t in range(4):
                cs_ref[t:t + 1, :] = jnp.sum(s_cs[t], axis=0, keepdims=True)
            cs_ref[4:8, :] = jnp.zeros((4, D), F32)

    row = lambda w: pl.BlockSpec((tm, w), lambda i: (i, 0))
    full = lambda a, b: pl.BlockSpec((a, b), lambda i: (0, 0))
    return _hosted_call(
        body, comm, (dh2a, dh2b, x1, dx2, mix, mod, g_ffn, p, p, ya, ybp, merged, ya_att, yb, w_out, w_bra, w_brb),
        name="mid_bwd", grid=(nsteps,),
        in_specs=[row(D), row(D), row(D), row(D), row(D), full(8, D), full(1, D),
                  pl.BlockSpec((tm, D), lambda i: (i, gblk)), pl.BlockSpec((tm, D), lambda i: (i, gblk + 1)),
                  row(D), row(D), row(D), row(512), row(256), full(D, D), full(512, D), full(256, D)],
        out_specs=[row(D), pl.BlockSpec((tm, 2 * D), lambda i: (i, LAY_G // (2 * D))), row(512), row(256), full(8, D),
                   full(D, D), full(512, D), full(256, D)],
        out_shape=[_sds((S, D), F32), _sds((S, LAY_N), BF16), _sds((S, 512), BF16), _sds((S, 256), F32), _sds((8, D), F32),
                   _sds((D, D), F32), _sds((512, D), F32), _sds((256, D), F32)],
        scratch_shapes=[pltpu.VMEM((4, 8, D), F32)],
        vmem_mb=VMEM_MB)


def fox_bwd(p, do, o, gcol, fcol, dp, comm=None):
    nq = S // FT
    nt = (((1,), (1,)), ((), ()))
    tn = (((0,), (0,)), ((), ()))

    def body(qkv_ref, do_ref, o_ref, g_ref, fc_ref, dp_in, dp_ref, df_ref, rs_ref, dq_s, qa_s, ka_s, dob_s, dl_s):
        del dp_in
        _fox_operands(qkv_ref, g_ref, fc_ref, qa_s, ka_s)
        masks = _head_masks(FT)
        lane = lax.broadcasted_iota(jnp.int32, (FT, LANES), 1)
        head0 = 2 * pl.program_id(0)
        causal = lax.broadcasted_iota(jnp.int32, (FT, FT), 1) <= lax.broadcasted_iota(jnp.int32, (FT, FT), 0)
        dq_s[...] = jnp.zeros_like(dq_s)
        rs_ref[...] = jnp.zeros_like(rs_ref)

        causal2 = jnp.concatenate([causal, causal], axis=0)

        def prep(i, _):
            r = pl.ds(pl.multiple_of(i * 256, 256), 256)
            m256 = _head_masks(256)
            dov, ov = do_ref[r, :].astype(F32), o_ref[r, :].astype(F32)
            for hh in range(2):
                dom = jnp.where(m256[hh], dov, 0.0)
                dob_s[hh, r, :] = dom.astype(BF16)
                dl_s[hh, r, :] = jnp.broadcast_to(jnp.sum(dom * ov, axis=1, keepdims=True), (256, LANES))
            return 0

        lax.fori_loop(0, S // 256, prep, 0)

        def stack(ref, q0, n, cols=slice(None)):
            return jnp.concatenate([ref[0, pl.ds(q0, n), cols], ref[1, pl.ds(q0, n), cols]], axis=0)

        def kloop(kb, _):
            k0 = pl.multiple_of(kb * FT, FT)
            k = qkv_ref[pl.ds(k0, FT), LANES:2 * LANES].astype(BF16)
            v = qkv_ref[pl.ds(k0, FT), 2 * LANES:3 * LANES].astype(BF16)
            ka = ka_s[pl.ds(k0, FT), :]

            def step(qi, carry, diagonal, tiles=1):
                dk, dv, df0, df1 = carry
                n = tiles * FT
                q0 = pl.multiple_of(qi * FT, FT)
                qa, dob = stack(qa_s, q0, n), stack(dob_s, q0, n)
                s = lax.dot_general(qa, ka, nt, preferred_element_type=F32)
                pr = jnp.exp(jnp.where(causal2, s, NEG)) if diagonal else jnp.exp(s)
                dpr = lax.dot_general(dob, v, nt, preferred_element_type=F32)
                ds = pr * (dpr - jnp.tile(stack(dl_s, q0, n), (1, FT // LANES)))
                dsb = ds.astype(BF16)
                dq = jnp.dot(dsb, k, preferred_element_type=F32) * SCALE
                dk = dk + lax.dot_general(dsb, qa[:, 0:LANES], tn, preferred_element_type=F32)
                dv = dv + lax.dot_general(pr.astype(BF16), dob, tn, preferred_element_type=F32)
                rsum = jnp.sum(ds, axis=1, keepdims=True)
                lane_n = lax.broadcasted_iota(jnp.int32, (n, LANES), 1)
                dq_s[pl.ds(q0, n), :] += jnp.where(lane_n < HD, dq[:n], dq[n:])
                rs_ref[pl.ds(q0, n), :] += jnp.where(lane_n == head0, rsum[:n], 0.0) + jnp.where(lane_n == head0 + 1, rsum[n:], 0.0)
                return (dk, dv, df0 - jnp.sum(ds[:n], axis=0, keepdims=True), df1 - jnp.sum(ds[n:], axis=0, keepdims=True))

            z = jnp.zeros((FT, LANES), F32)
            z1 = jnp.zeros((1, FT), F32)
            carry = step(kb, (z, z, z1, z1), True)
            pairs = (nq - 1 - kb) // 2
            carry = lax.fori_loop(0, pairs, lambda j, cr: step(kb + 1 + 2 * j, cr, False, 2), carry)
            dk, dv, df0, df1 = lax.fori_loop(kb + 1 + 2 * pairs, nq, lambda qi, cr: step(qi, cr, False), carry)
            dp_ref[pl.ds(k0, FT), LANES:2 * LANES] = dk.astype(BF16)
            dp_ref[pl.ds(k0, FT), 2 * LANES:3 * LANES] = dv.astype(BF16)
            df_ref[0:1, pl.ds(k0, FT)] = df0
            df_ref[1:2, pl.ds(k0, FT)] = df1
            return 0

        lax.fori_loop(0, S // FT, kloop, 0)
        dp_ref[:, 0:LANES] = dq_s[...].astype(BF16)

    a_blk = LAY_A // 384
    pair = pl.BlockSpec((S, LANES), lambda p_: (0, p_))
    heads = pl.BlockSpec((2, S, LANES), lambda p_: (p_, 0, 0))
    return _hosted_call(
        body, comm, (p, do, o, gcol, fcol, dp), name="fox_bwd", grid=(N_FOX_PAIRS,),
        in_specs=[pl.BlockSpec((S, 384), lambda p_: (0, a_blk + p_)), pair, pair, heads, heads, pl.BlockSpec(memory_space=pl.ANY)],
        out_specs=[pl.BlockSpec((S, 384), lambda p_: (0, a_blk + p_)), pl.BlockSpec((None, 2, S), lambda p_: (p_, 0, 0)),
                   pl.BlockSpec((None, S, LANES), lambda p_: (p_, 0, 0))],
        out_shape=[_sds((S, LAY_N), BF16), _sds((4, 2, S), F32), _sds((4, S, LANES), F32)],
        scratch_shapes=[pltpu.VMEM((S, LANES), F32), pltpu.VMEM((2, S, 2 * LANES), BF16), pltpu.VMEM((S, 2 * LANES), BF16),
                        pltpu.VMEM((2, S, LANES), BF16), pltpu.VMEM((2, S, LANES), F32)],
        aliases={5: 0}, vmem_mb=VMEM_MB)


def fgate_bwd(dfrow, dfcol, fraw, dp):
    def body(df_ref, dc_ref, f_ref, dp_in, dpf_ref, db_ref):
        del dp_in
        lane = lax.broadcasted_iota(jnp.int32, (8, S), 1)
        rsum = (dc_ref[0] + dc_ref[1]) + (dc_ref[2] + dc_ref[3])
        acc, sh = df_ref[...] + rsum.T[0:8, :], 1
        while sh < S:
            acc = acc + jnp.where(lane < S - sh, pltpu.roll(acc, S - sh, 1), 0.0)
            sh *= 2
        df = acc * _sigmoid(-f_ref[...])
        db_ref[...] = jnp.broadcast_to(jnp.sum(df, axis=1, keepdims=True), (8, LANES))
        dfc = jnp.concatenate([df, jnp.zeros((LANES - 8, S), F32)], axis=0).T
        dpf_ref[:, 0:LANES] = dfc.astype(BF16)
        dpf_ref[:, LANES:2 * LANES] = jnp.zeros((S, LANES), BF16)

    return pl.pallas_call(
        body, name="fgate_bwd", grid=(1,),
        in_specs=[pl.BlockSpec((8, S), lambda i: (0, 0)), pl.BlockSpec((4, S, LANES), lambda i: (0, 0, 0)),
                  pl.BlockSpec((8, S), lambda i: (0, 0)), pl.BlockSpec(memory_space=pl.ANY)],
        out_specs=[pl.BlockSpec((S, 2 * LANES), lambda i: (0, LAY_F // (2 * LANES))), pl.BlockSpec((8, LANES), lambda i: (0, 0))],
        out_shape=[_sds((S, LAY_N), BF16), _sds((8, LANES), F32)],
        input_output_aliases={3: 0},
        compiler_params=_params(VMEM_MB),
    )(dfrow, dfcol, fraw, dp)


def dil_bwd(p, dyb, yb, lse, tabs, dp, comm=None):
    def body(*refs):
        qkv = [refs[3 * g:3 * g + 3] for g in range(3)]
        dy_ref, y_ref, lse_ref, c_ref, s1_ref, s2_ref = refs[9:15]
        dp_ref = refs[16]
        dq_s, dk_s, dv_s, dl_s = refs[17:21]
        masks = _head_masks(SPAN)
        m256 = _head_masks(256)
        nt = (((1,), (1,)), ((), ()))
        tn = (((0,), (0,)), ((), ()))
        dk_s[...] = jnp.zeros_like(dk_s)
        dv_s[...] = jnp.zeros_like(dv_s)

        def prep(i, _):
            rows = pl.ds(pl.multiple_of(i * 256, 256), 256)
            pr = dy_ref[rows, :] * y_ref[rows, :]
            d0 = jnp.sum(jnp.where(m256[0], pr, 0.0), axis=1, keepdims=True)
            d1 = jnp.sum(jnp.where(m256[1], pr, 0.0), axis=1, keepdims=True)
            dl_s[rows, :] = jnp.where(m256[0], d0, d1)
            return 0

        lax.fori_loop(0, S // 256, prep, 0)

        for g, (d, nb) in enumerate(DIL_GROUPS):
            q_ref, k_ref, v_ref = qkv[g]

            def blk(n, _):
                start, prev, has_prev = _dil_block(n, d, nb)
                q = _dil_rows(q_ref, start, d)
                kc = jnp.concatenate([_dil_rows(k_ref, prev, d), _dil_rows(k_ref, start, d)], axis=0).astype(BF16)
                vc = jnp.concatenate([_dil_rows(v_ref, prev, d), _dil_rows(v_ref, start, d)], axis=0).astype(BF16)
                dov = _dil_rows(dy_ref, start, d)
                lsev = _dil_rows(lse_ref, start, d)
                dlv = _dil_rows(dl_s, start, d)
                valid = _band_mask(has_prev)
                valid2 = jnp.concatenate([valid, valid], axis=0)

                def stack(t):
                    return jnp.concatenate([jnp.where(masks[0], t, 0.0), jnp.where(masks[1], t, 0.0)], axis=0)

                def column(t):
                    return jnp.concatenate([jnp.max(jnp.where(masks[hh], t, NEG), axis=1, keepdims=True) for hh in range(2)], axis=0)

                q2 = (stack(q) * SCALE).astype(BF16)
                dob = stack(dov).astype(BF16)
                s = jnp.where(valid2, lax.dot_general(q2, kc, nt, preferred_element_type=F32), NEG)
                pr = jnp.exp(s - column(lsev))
                dpr = lax.dot_general(dob, vc, nt, preferred_element_type=F32)
                dsb = (pr * (dpr - column(dlv))).astype(BF16)
                dq = jnp.dot(dsb, kc, preferred_element_type=F32) * SCALE
                dkc = lax.dot_general(dsb, q2, tn, preferred_element_type=F32)
                dvc = lax.dot_general(pr.astype(BF16), dob, tn, preferred_element_type=F32)
                _dil_store(dq_s.at[g], start, d, jnp.where(masks[0], dq[:SPAN], dq[SPAN:]))
                for ref, val in ((dk_s.at[g], dkc), (dv_s.at[g], dvc)):
                    _dil_store(ref, prev, d, _dil_rows(ref, prev, d) + jnp.where(has_prev, val[0:SPAN], 0.0))
                    _dil_store(ref, start, d, _dil_rows(ref, start, d) + val[SPAN:])
                return 0

            lax.fori_loop(0, 16, blk, 0)

        def fin(i, _):
            rows = pl.ds(pl.multiple_of(i * 256, 256), 256)
            c, s1, s2 = c_ref[rows, :], s1_ref[rows, :], s2_ref[rows, :]
            for g in range(3):
                base = g * 384
                dp_ref[rows, base:base + LANES] = _rope_bwd(dq_s[g, rows, :], c, s1, s2).astype(BF16)
                dp_ref[rows, base + LANES:base + 2 * LANES] = _rope_bwd(dk_s[g, rows, :], c, s1, s2).astype(BF16)
                dp_ref[rows, base + 2 * LANES:base + 3 * LANES] = dv_s[g, rows, :].astype(BF16)
            return 0

        lax.fori_loop(0, S // 256, fin, 0)

    def spec(g, t):
        return pl.BlockSpec((S, LANES), lambda p_: (0, (p_ * 3 + g) * 3 + t))

    pair = pl.BlockSpec((S, LANES), lambda p_: (0, p_))
    tab = pl.BlockSpec((S, LANES), lambda p_: (0, 0))
    return _hosted_call(
        body, comm, [p] * 9 + [dyb, yb, lse, *tabs, dp], name="dil_bwd", grid=(N_DIL_PAIRS,),
        in_specs=[spec(g, t) for g in range(3) for t in range(3)] + [pair, pair, pair, tab, tab, tab, pl.BlockSpec(memory_space=pl.ANY)],
        out_specs=pl.BlockSpec((S, 1152), lambda p_: (0, p_)),
        out_shape=_sds((S, LAY_N), BF16),
        scratch_shapes=[pltpu.VMEM((3, S, LANES), F32)] * 3 + [pltpu.VMEM((S, LANES), F32)],
        aliases={15: 0}, vmem_mb=VMEM_MB)


def in_bwd_tail(dh1, x, dx1, mod, g_mix, comm=None):
    tm = 256
    nsteps = S // tm

    def body(dh_ref, x_ref, dx1_ref, mod_ref, g_ref, dx_ref, cs_ref, s_cs):
        i = pl.program_id(0)

        @pl.when(i == 0)
        def _():
            s_cs[...] = jnp.zeros_like(s_cs)

        xv, g, dh = x_ref[...], g_ref[...], dh_ref[...]
        r = lax.rsqrt(jnp.mean(xv * xv, axis=1, keepdims=True) + EPS)
        xh = xv * r
        s_cs[0] += _colsum8(dh)
        s_cs[1] += _colsum8(dh * (xh * g))
        dn = dh * (1.0 + mod_ref[1:2, :])
        s_cs[2] += _colsum8(dn * xh)
        gd = dn * g
        dx_ref[...] = dx1_ref[...] + r * (gd - xh * jnp.mean(gd * xh, axis=1, keepdims=True))

        @pl.when(i == nsteps - 1)
        def _():
            for t in range(3):
                cs_ref[t:t + 1, :] = jnp.sum(s_cs[t], axis=0, keepdims=True)
            cs_ref[3:8, :] = jnp.zeros((5, D), F32)

    row = pl.BlockSpec((tm, D), lambda i: (i, 0))
    return _hosted_call(
        body, comm, (dh1, x, dx1, mod, g_mix), name="in_bwd_tail", grid=(nsteps,),
        in_specs=[row, row, row, pl.BlockSpec((8, D), lambda i: (0, 0)), pl.BlockSpec((1, D), lambda i: (0, 0))],
        out_specs=[row, pl.BlockSpec((8, D), lambda i: (0, 0))],
        out_shape=[_sds((S, D), F32), _sds((8, D), F32)],
        scratch_shapes=[pltpu.VMEM((3, 8, D), F32)])


def _lay_pieces():
    out = []
    qa, ka, va, fa, qb, kb, vb, ga = 0, 512, 1024, 1536, 1544, 2312, 3080, 3848
    for p in range(N_DIL_PAIRS):
        for g in range(3):
            base = LAY_B + (p * 3 + g) * 384
            hd0 = (4 * g + 2 * p) * HD
            out += [(base, qb + hd0, LANES), (base + LANES, kb + hd0, LANES), (base + 2 * LANES, vb + hd0, LANES)]
    for p in range(N_FOX_PAIRS):
        base = LAY_A + p * 384
        out += [(base, qa + p * LANES, LANES), (base + LANES, ka + p * LANES, LANES), (base + 2 * LANES, va + p * LANES, LANES)]
    out.append((LAY_F, fa, 8))
    out.append((LAY_G, ga, 2 * D))
    return out


def _shard_runs():
    runs = []
    for lay, nat, width in _lay_pieces():
        while width:
            k, loc = nat // IN_SHARD, nat % IN_SHARD
            w = min(width, IN_SHARD - loc)
            runs.append((lay, k, loc, w))
            lay, nat, width = lay + w, nat + w, width - w
    return runs


def lay_from_shards(g):
    tm = 256

    def body(g_ref, o_ref):
        o_ref[:, LAY_F:LAY_G] = jnp.zeros((tm, LAY_G - LAY_F), g.dtype)
        for lay, k, loc, w in _shard_runs():
            o_ref[:, lay:lay + w] = g_ref[k, :, loc:loc + w]

    return pl.pallas_call(
        body, name="lay_from_shards", grid=(D // tm,),
        in_specs=[pl.BlockSpec((4, tm, IN_SHARD_PAD), lambda i: (0, i, 0))],
        out_specs=pl.BlockSpec((tm, LAY_N), lambda i: (i, 0)),
        out_shape=_sds((D, LAY_N), g.dtype), compiler_params=_params(VMEM_MB),
    )(g)


def shards_from_lay(dw_lay):
    tm = 256

    def body(x_ref, o_ref):
        o_ref[:, :, IN_SHARD:] = jnp.zeros((4, tm, IN_SHARD_PAD - IN_SHARD), F32)
        for lay, k, loc, w in _shard_runs():
            o_ref[k, :, loc:loc + w] = x_ref[:, lay:lay + w]

    return pl.pallas_call(
        body, name="shards_from_lay", grid=(D // tm,),
        in_specs=[pl.BlockSpec((tm, LAY_N), lambda i: (i, 0))],
        out_specs=pl.BlockSpec((4, tm, IN_SHARD_PAD), lambda i: (0, i, 0)),
        out_shape=_sds((4, D, IN_SHARD_PAD), F32), compiler_params=_params(VMEM_MB),
    )(dw_lay)


def _pos():
    return lax.axis_index("x"), lax.axis_index("y"), lax.axis_index("c")


def _other_chips(x, y):
    return [(1 - x, y), (x, 1 - y), (1 - x, 1 - y)]


def _remote(src, dst, send_sem, recv_sem, dev):
    return pltpu.make_async_remote_copy(src_ref=src, dst_ref=dst, send_sem=send_sem, recv_sem=recv_sem,
                                        device_id=dev, device_id_type=MESH)


VMEM_SPEC = pl.BlockSpec(memory_space=pltpu.VMEM)
ANY_SPEC = pl.BlockSpec(memory_space=pl.ANY)


def gather_all(v, name, with_sum):
    r = v.shape[0]

    def body(v_ref, out_ref, *rest):
        send_s, recv_s = rest[-2:]
        x, y, c = _pos()
        me = 4 * x + 2 * y + c
        out_ref[me] = v_ref[...]
        peers = []
        for m in range(1, 8):
            px = 1 - x if m & 4 else x
            py = 1 - y if m & 2 else y
            pc = 1 - c if m & 1 else c
            peers.append((px, py, pc))
        copies = [_remote(v_ref, out_ref.at[me], send_s.at[i], recv_s.at[i], dev) for i, dev in enumerate(peers)]
        for cp in copies:
            cp.start()
        for i, (px, py, pc) in enumerate(peers):
            _remote(v_ref, out_ref.at[4 * px + 2 * py + pc], send_s.at[i], recv_s.at[i], (px, py, pc)).wait_recv()
        for cp in copies:
            cp.wait_send()
        if with_sum:
            acc = out_ref[0]
            for b in range(1, 8):
                acc = acc + out_ref[b]
            rest[0][...] = acc

    out_shape = [_sds((8, r, LANES), F32)] + ([_sds((r, LANES), F32)] if with_sum else [])
    return pl.pallas_call(
        body, name=name, in_specs=[VMEM_SPEC], out_specs=[VMEM_SPEC] * len(out_shape), out_shape=out_shape,
        scratch_shapes=[pltpu.SemaphoreType.DMA((7,)), pltpu.SemaphoreType.DMA((7,))],
    )(v)


def mod_exchange(c_all, w_ada_sh, b_sh):
    def body(c_ref, w_ref, b_ref, out_ref, sc_ref, modp, send_s, recv_s):
        cv = c_ref[...]
        sc = cv * _sigmoid(cv)
        sc_ref[...] = sc
        modp[...] = jnp.dot(sc, w_ref[...], precision=lax.Precision.HIGHEST, preferred_element_type=F32) + b_ref[...]
        x, y, c = _pos()
        k = 2 * x + y
        out_ref[k] = modp[...]
        chips = _other_chips(x, y)
        copies = [_remote(modp, out_ref.at[k], send_s.at[j], recv_s.at[j], (cx, cy, c)) for j, (cx, cy) in enumerate(chips)]
        for cp in copies:
            cp.start()
        for j, (cx, cy) in enumerate(chips):
            _remote(modp, out_ref.at[2 * cx + cy], send_s.at[j], recv_s.at[j], (cx, cy, c)).wait_recv()
        for cp in copies:
            cp.wait_send()

    n = w_ada_sh.shape[1]
    return pl.pallas_call(
        body, name="mod_exchange", in_specs=[VMEM_SPEC] * 3, out_specs=[VMEM_SPEC] * 2,
        out_shape=[_sds((4, 8, n), F32), _sds((8, D), F32)],
        scratch_shapes=[pltpu.VMEM((8, n), F32), pltpu.SemaphoreType.DMA((3,)), pltpu.SemaphoreType.DMA((3,))],
        compiler_params=_params(VMEM_MB),
    )(c_all, w_ada_sh, b_sh)


def cast_into_slabs(ws, sizes, chip, comm):
    tr = 64
    n = len(ws)
    n_in = [w.shape[0] // tr for w in ws]
    n_out = [r // tr for r, _ in sizes]
    steps = max(n_out)
    nci, nco = len(comm.ins), len(comm.out_shapes)

    def body(chip_ref, *refs):
        del chip_ref
        w_refs, cin = refs[:n], refs[n:n + nci]
        o_refs, cout = refs[n + nci:2 * n + nci], refs[2 * n + nci:2 * n + nci + nco]
        sems = refs[2 * n + nci + nco:]
        i = pl.program_id(0)

        @pl.when(i == 0)
        def _():
            comm.start(cin, cout, sems)

        for a in range(n):
            c0, cols = ws[a].shape[1], sizes[a][1]

            @pl.when(i < n_in[a])
            def _(a=a, c0=c0, cols=cols):
                o_refs[a][:, 0:c0] = w_refs[a][...].astype(BF16)
                if cols > c0:
                    o_refs[a][:, c0:] = jnp.zeros((tr, cols - c0), BF16)

            if n_out[a] > n_in[a]:
                @pl.when((i >= n_in[a]) & (i < n_out[a]))
                def _(a=a, cols=cols):
                    o_refs[a][...] = jnp.zeros((tr, cols), BF16)

        @pl.when(i == steps - 1)
        def _():
            comm.wait(cin, cout, sems)

    any_spec = pl.BlockSpec(memory_space=pl.ANY)
    grid_spec = pltpu.PrefetchScalarGridSpec(
        num_scalar_prefetch=1, grid=(steps,),
        in_specs=[pl.BlockSpec((tr, w.shape[1]), functools.partial(lambda i, k, last: (jnp.minimum(i, last), 0), last=n_in[a] - 1))
                  for a, w in enumerate(ws)] + [any_spec] * nci,
        out_specs=[pl.BlockSpec((None, tr, sizes[a][1]), functools.partial(lambda i, k, last: (k[0], jnp.minimum(i, last), 0), last=n_out[a] - 1))
                   for a in range(n)] + [any_spec] * nco,
        scratch_shapes=[pltpu.SemaphoreType.DMA((s,)) for s in comm.sems])
    res = pl.pallas_call(
        body, name="cast_into_slabs", grid_spec=grid_spec,
        out_shape=[_sds((4,) + tuple(sz), BF16) for sz in sizes] + comm.out_shapes,
        input_output_aliases={1 + n + ci: n + co for ci, co in comm.aliases.items()},
    )(chip, *ws, *comm.ins)
    return list(res[:n]), list(res[n:])


def cast_into_slab(w, rows, cols, chip, tag):
    r0, c0 = w.shape
    tr = 256 if (r0 % 256 == 0 and rows % 256 == 0) else 64
    n_in, n_out = r0 // tr, rows // tr

    def body(chip_ref, w_ref, o_ref):
        del chip_ref
        i = pl.program_id(0)

        @pl.when(i < n_in)
        def _():
            o_ref[:, 0:c0] = w_ref[...].astype(BF16)
            if cols > c0:
                o_ref[:, c0:] = jnp.zeros((tr, cols - c0), BF16)

        @pl.when(i >= n_in)
        def _():
            o_ref[...] = jnp.zeros((tr, cols), BF16)

    grid_spec = pltpu.PrefetchScalarGridSpec(
        num_scalar_prefetch=1, grid=(n_out,),
        in_specs=[pl.BlockSpec((tr, c0), lambda i, k: (jnp.minimum(i, n_in - 1), 0))],
        out_specs=pl.BlockSpec((None, tr, cols), lambda i, k: (k[0], i, 0)))
    return pl.pallas_call(body, name="cast_" + tag, grid_spec=grid_spec, out_shape=_sds((4, rows, cols), BF16))(chip, w)


def _row_tile(rows, cap=256):
    t = cap
    while rows % t or t % 8:
        t -= 8
    return t


def _comm_wait(sends, recvs, local=()):
    for cp in recvs:
        cp.wait_recv()
    for cp in sends:
        cp.wait_send()
    for cp in local:
        cp.wait()


def ag_ici(bufs):
    n = len(bufs)

    def copies(ins, outs, sems):
        send_s, recv_s = sems
        x, y, c = _pos()
        k = 2 * x + y
        sends, recvs = [], []
        for a in range(n):
            half = outs[a].shape[1] // 2
            rows = pl.ds(c * half, half)
            for j, (cx, cy) in enumerate(_other_chips(x, y)):
                sem = (send_s.at[3 * a + j], recv_s.at[3 * a + j], (cx, cy, c))
                sends.append(_remote(outs[a].at[k, rows], outs[a].at[k, rows], *sem))
                recvs.append(_remote(outs[a].at[k, rows], outs[a].at[2 * cx + cy, rows], *sem))
        return sends, recvs

    def start(ins, outs, sems):
        for cp in copies(ins, outs, sems)[0]:
            cp.start()

    def wait(ins, outs, sems):
        _comm_wait(*copies(ins, outs, sems))

    return Comm(bufs, [_sds(b.shape, b.dtype) for b in bufs], [3 * n, 3 * n], start, wait, aliases={a: a for a in range(n)})


def ag_d2d(bufs):
    n = len(bufs)

    def copies(ins, outs, sems):
        send_s, recv_s = sems
        x, y, c = _pos()
        sends, recvs = [], []
        for a in range(n):
            half = outs[a].shape[1] // 2
            rows, orows = pl.ds(c * half, half), pl.ds((1 - c) * half, half)
            for j, (cx, cy) in enumerate(_other_chips(x, y)):
                kj = 2 * cx + cy
                sem = (send_s.at[3 * a + j], recv_s.at[3 * a + j], (x, y, 1 - c))
                sends.append(_remote(outs[a].at[kj, rows], outs[a].at[kj, rows], *sem))
                recvs.append(_remote(outs[a].at[kj, orows], outs[a].at[kj, orows], *sem))
        return sends, recvs

    def start(ins, outs, sems):
        for cp in copies(ins, outs, sems)[0]:
            cp.start()

    def wait(ins, outs, sems):
        _comm_wait(*copies(ins, outs, sems))

    return Comm(bufs, [_sds(b.shape, b.dtype) for b in bufs], [3 * n, 3 * n], start, wait, aliases={a: a for a in range(n)})


def rs_a(grads):
    n = len(grads)

    def copies(ins, outs, sems):
        send_s, recv_s = sems
        x, y, c = _pos()
        cps = []
        for a in range(n):
            half = ins[a].shape[1] // 2
            cps.append(_remote(ins[a].at[:, pl.ds((1 - c) * half, half), :], outs[a], send_s.at[a], recv_s.at[a], (x, y, 1 - c)))
        return cps

    def start(ins, outs, sems):
        for cp in copies(ins, outs, sems):
            cp.start()

    def wait(ins, outs, sems):
        cps = copies(ins, outs, sems)
        _comm_wait(cps, cps)

    return Comm(grads, [_sds((4, g.shape[1] // 2, g.shape[2]), g.dtype) for g in grads], [n, n], start, wait)


def rs_b(pres):
    n = len(pres)

    def copies(ins, outs, sems):
        send_s, recv_s = sems
        x, y, c = _pos()
        cps = []
        for a in range(n):
            for j, (cx, cy) in enumerate(_other_chips(x, y)):
                cps.append(_remote(ins[a].at[2 * cx + cy], outs[a].at[j], send_s.at[3 * a + j], recv_s.at[3 * a + j], (cx, cy, c)))
        return cps

    def start(ins, outs, sems):
        for cp in copies(ins, outs, sems):
            cp.start()

    def wait(ins, outs, sems):
        cps = copies(ins, outs, sems)
        _comm_wait(cps, cps)

    return Comm(pres, [_sds((3,) + p_.shape[1:], p_.dtype) for p_ in pres], [3 * n, 3 * n], start, wait)


def rs_b_rows(pre, buf, lo, n):
    def copies(ins, outs, sems):
        send_s, recv_s = sems
        x, y, c = _pos()
        rows = pl.ds(lo, n)
        return [_remote(ins[0].at[2 * cx + cy, rows], outs[0].at[j, rows], send_s.at[j], recv_s.at[j], (cx, cy, c))
                for j, (cx, cy) in enumerate(_other_chips(x, y))]

    def start(ins, outs, sems):
        for cp in copies(ins, outs, sems):
            cp.start()

    def wait(ins, outs, sems):
        cps = copies(ins, outs, sems)
        _comm_wait(cps, cps)

    ins = [pre] if buf is None else [pre, buf]
    return Comm(ins, [_sds((3,) + pre.shape[1:], pre.dtype)], [3, 3], start, wait, aliases={} if buf is None else {1: 0})


def rs_c(reds):
    n = len(reds)

    def copies(ins, outs, sems):
        send_s, recv_s = sems
        x, y, c = _pos()
        sends, recvs = [], []
        for a in range(n):
            half = outs[a].shape[0] // 2
            rows, orows = pl.ds(c * half, half), pl.ds((1 - c) * half, half)
            sem = (send_s.at[a], recv_s.at[a], (x, y, 1 - c))
            sends.append(_remote(outs[a].at[rows], outs[a].at[rows], *sem))
            recvs.append(_remote(outs[a].at[orows], outs[a].at[orows], *sem))
        return sends, recvs

    def start(ins, outs, sems):
        for cp in copies(ins, outs, sems)[0]:
            cp.start()

    def wait(ins, outs, sems):
        _comm_wait(*copies(ins, outs, sems))

    return Comm(reds, [_sds(r_.shape, r_.dtype) for r_ in reds], [n, n], start, wait, aliases={a: a for a in range(n)})


def comm_join(*comms):
    ni = np.cumsum([0] + [len(c.ins) for c in comms])
    no = np.cumsum([0] + [len(c.out_shapes) for c in comms])
    ns = np.cumsum([0] + [len(c.sems) for c in comms])

    def parts(ins, outs, sems):
        return [(c, ins[ni[i]:ni[i + 1]], outs[no[i]:no[i + 1]], sems[ns[i]:ns[i + 1]]) for i, c in enumerate(comms)]

    def start(ins, outs, sems):
        for c, a, b, s in parts(ins, outs, sems):
            c.start(a, b, s)

    def wait(ins, outs, sems):
        for c, a, b, s in parts(ins, outs, sems):
            c.wait(a, b, s)

    aliases = {int(ni[i]) + k: int(no[i]) + v for i, c in enumerate(comms) for k, v in c.aliases.items()}
    return Comm(sum((c.ins for c in comms), []), sum((c.out_shapes for c in comms), []), sum((c.sems for c in comms), []),
                start, wait, aliases)


def comm_only(comm, name):
    nci, nco = len(comm.ins), len(comm.out_shapes)

    def body(*refs):
        ins, outs, sems = refs[:nci], refs[nci:nci + nco], refs[nci + nco:]
        comm.start(ins, outs, sems)
        comm.wait(ins, outs, sems)

    return pl.pallas_call(
        body, name=name, in_specs=[ANY_SPEC] * nci, out_specs=[ANY_SPEC] * nco, out_shape=comm.out_shapes,
        scratch_shapes=[pltpu.SemaphoreType.DMA((s,)) for s in comm.sems],
        input_output_aliases=comm.aliases,
    )(*comm.ins)


def rs_add_halves(g, other, core, name):
    _, r, cdim = g.shape
    half = r // 2
    tr = _row_tile(half, 256)
    nb = half // tr

    def body(core_ref, g_ref, o_ref, out_ref):
        del core_ref
        out_ref[...] = (g_ref[...] + o_ref[...]).astype(BF16)

    grid_spec = pltpu.PrefetchScalarGridSpec(
        num_scalar_prefetch=1, grid=(4, nb),
        in_specs=[pl.BlockSpec((None, tr, cdim), lambda k, i, cr: (k, cr[0] * nb + i, 0)),
                  pl.BlockSpec((None, tr, cdim), lambda k, i, cr: (k, i, 0))],
        out_specs=pl.BlockSpec((None, tr, cdim), lambda k, i, cr: (k, i, 0)))
    return pl.pallas_call(body, name=name, grid_spec=grid_spec, out_shape=_sds((4, half, cdim), BF16))(core, g, other)


def rs_add_slabs(t, pre, place, name):
    _, half, cdim = t.shape
    tr = _row_tile(half, 256)
    nb = half // tr

    def body(place_ref, own_ref, t_ref, out_ref):
        del place_ref
        out_ref[...] = ((own_ref[...].astype(F32) + t_ref[0].astype(F32)) + t_ref[1].astype(F32)) + t_ref[2].astype(F32)

    grid_spec = pltpu.PrefetchScalarGridSpec(
        num_scalar_prefetch=1, grid=(nb,),
        in_specs=[pl.BlockSpec((None, tr, cdim), lambda i, pr: (pr[0], i, 0)), pl.BlockSpec((3, tr, cdim), lambda i, pr: (0, i, 0))],
        out_specs=pl.BlockSpec((tr, cdim), lambda i, pr: (pr[1] * nb + i, 0)))
    return pl.pallas_call(body, name=name, grid_spec=grid_spec, out_shape=_sds((2 * half, cdim), F32))(place, pre, t)


def _adam_math(w, g, m, v):
    m = ADAM_B1 * m + (1.0 - ADAM_B1) * g
    v = ADAM_B2 * v + (1.0 - ADAM_B2) * (g * g)
    m_hat = m / (1.0 - ADAM_B1 ** ADAM_STEP)
    v_hat = v / (1.0 - ADAM_B2 ** ADAM_STEP)
    delta = -ADAM_LR * (m_hat / (jnp.sqrt(v_hat) + ADAM_EPS) + ADAM_WD * w)
    return delta, m, v


def adam(w, g, m, v, name, comm=None):
    r, cdim = w.shape
    tr = _row_tile(r) if r >= 8 else r

    def body(w_ref, g_ref, m_ref, v_ref, g_out, d_ref, nm_ref, nv_ref):
        gv = g_ref[:, :cdim]
        g_out[...] = gv
        d_ref[...], nm_ref[...], nv_ref[...] = _adam_math(w_ref[...], gv, m_ref[...], v_ref[...])

    blk = pl.BlockSpec((tr, cdim), lambda i: (i, 0))
    return _hosted_call(
        body, comm, (w, g, m, v), name=name, grid=(r // tr,),
        in_specs=[blk, pl.BlockSpec((tr, g.shape[1]), lambda i: (i, 0)), blk, blk],
        out_specs=[blk] * 4, out_shape=[_sds((r, cdim), F32)] * 4)


def adam_small(groups):
    n = len(groups)

    def body(*refs):
        ins, outs = refs[:4 * n], refs[4 * n:]
        for i in range(n):
            w_ref, m_ref, v_ref, g_ref = ins[4 * i:4 * i + 4]
            d_ref, nm_ref, nv_ref = outs[3 * i:3 * i + 3]
            d_ref[...], nm_ref[...], nv_ref[...] = _adam_math(w_ref[...], g_ref[...], m_ref[...], v_ref[...])

    flat = [t for grp in groups for t in grp]
    out = pl.pallas_call(
        body, name="adam_small", in_specs=[VMEM_SPEC] * (4 * n), out_specs=[VMEM_SPEC] * (3 * n),
        out_shape=[_sds(grp[0].shape, F32) for grp in groups for _ in range(3)],
    )(*flat)
    return [out[3 * i:3 * i + 3] for i in range(n)]


def adam_w_ada(sc_t, dmod_sh, w, m, v, comm=None):
    r, cdim = w.shape
    tr = 256

    def body(s_ref, d_ref, w_ref, m_ref, v_ref, g_ref, dl_ref, nm_ref, nv_ref):
        g = jnp.dot(s_ref[...], d_ref[...], precision=lax.Precision.HIGHEST, preferred_element_type=F32)
        g_ref[...] = g
        dl_ref[...], nm_ref[...], nv_ref[...] = _adam_math(w_ref[...], g, m_ref[...], v_ref[...])

    blk = pl.BlockSpec((tr, cdim), lambda i: (i, 0))
    return _hosted_call(
        body, comm, (sc_t, dmod_sh, w, m, v), name="adam_w_ada", grid=(r // tr,),
        in_specs=[pl.BlockSpec((tr, LANES), lambda i: (i, 0)), pl.BlockSpec((LANES, cdim), lambda i: (0, 0)), blk, blk, blk],
        out_specs=[blk] * 4, out_shape=[_sds((r, cdim), F32)] * 4)


SMALL_ROWS = 80


def kernel(x, c, w_ada, b_ada, g_mix, w_in, b_fgate, w_br_a, w_br_b, w_out, g_ffn, w_ffn_gate, w_ffn_up, w_ffn_down, g_final, loss_target, m_w_ada, m_b_ada, m_g_mix, m_w_in, m_b_fgate, m_w_br_a, m_w_br_b, m_w_out, m_g_ffn, m_w_ffn_gate, m_w_ffn_up, m_w_ffn_down, m_g_final, v_w_ada, v_b_ada, v_g_mix, v_w_in, v_b_fgate, v_w_br_a, v_w_br_b, v_w_out, v_g_ffn, v_w_ffn_gate, v_w_ffn_up, v_w_ffn_down, v_g_final):
    xi, yi, ci = _pos()
    chip = 2 * xi + yi
    seq = 4 * xi + 2 * yi + ci
    n_ada = w_ada.shape[2]

    c_all = gather_all(c.reshape(8, LANES), "gather_c", False)[0].reshape(8, D)
    b_sh = lax.dynamic_slice(b_ada, (0, chip * n_ada), (1, n_ada))
    mod_all, sc = mod_exchange(c_all, w_ada[0], b_sh)
    mod = lax.dynamic_index_in_dim(mod_all, seq, axis=1, keepdims=False).reshape(6, D)
    mod8 = jnp.pad(mod, ((0, 2), (0, 0)))

    core = ci.astype(jnp.int32).reshape(1)
    chip1 = chip.astype(jnp.int32).reshape(1)
    place = jnp.stack([chip, ci]).astype(jnp.int32)
    s_in = cast_into_slab(w_in[0], D, IN_SHARD_PAD, chip1, "w_in")
    (s_bra, s_brb, s_out, s_gate, s_up, s_down), (g_in,) = cast_into_slabs(
        [w_br_a[0], w_br_b[0], w_out[0], w_ffn_gate[0].T, w_ffn_up[0].T, w_ffn_down[0]],
        [(512, 256), (256, 256), (256, D), (FF_PAD, D), (FF_PAD, D), (FF_PAD, D)], chip1, ag_ici([s_in]))
    xs, tgt, g_fin = x[0], loss_target[0], g_final.reshape(1, D)

    def halves(gs, others, tag):
        return [rs_add_halves(g, o, core, f"rs_{tag}_halves_{i}") for i, (g, o) in enumerate(zip(gs, others))]

    def slab_sums(ts, pres, tag):
        return [rs_add_slabs(t, pre, place, f"rs_{tag}_slabs_{i}") for i, (t, pre) in enumerate(zip(ts, pres))]

    tabs = rope_tables()
    h1, (g_in,) = norm_mod_fwd(xs, g_mix, mod8, 0, 1, comm=ag_d2d([g_in]))
    w_lay = lay_from_shards(g_in)
    p, mix_w = in_proj_fwd(h1, w_lay, tabs, comm=ag_ici([s_bra, s_brb, s_out]))
    fraw, fcol = fgate_fwd(p, jnp.pad(b_fgate, ((0, 0), (0, LANES - 8))))
    (ya_att, gcol), res = fox_fwd(p, fcol, comm=comm_join(ag_d2d(mix_w), ag_ici([s_gate, s_up])))
    g_bra, g_brb, g_out = res[:3]
    (yb, lse_b), res = dil_fwd(p, comm=comm_join(ag_d2d(res[3:]), ag_ici([s_down])))
    w_gate, w_up = res[:2]
    w_bra = g_bra.transpose(1, 0, 2).reshape(512, D)
    w_brb = g_brb.transpose(1, 0, 2).reshape(256, D)
    w_o = g_out.reshape(D, D)
    (merged, ya, ybp), (g_down,) = merge_fwd(ya_att, yb, p, w_bra, w_brb, comm=ag_d2d(res[2:]))
    w_down = g_down.reshape(FFP, D)
    mix, x1, h2 = out_proj_fwd(merged, w_o, xs, mod8, g_ffn)
    a, u, z = ffn_up_fwd(h2, w_gate, w_up)
    dx2, dffn, dg_final, dga_f, loss_part = ffn_down_loss(z, w_down, x1, mod8, g_fin, tgt)

    da, du, dw_down = ffn_down_bwd(dffn, w_down, a, u, z)
    g_down = [dw_down.reshape(4, FF_PAD, D)]
    dh2a, oth = mm_nt(da, w_gate, "ffn_gate_dx", comm=rs_a(g_down))
    pre_down = halves(g_down, oth, "down")
    dh2b, _ = mm_nt(du, w_up, "ffn_up_dx")
    dw_gate, _ = mm_tn(h2, da, "ffn_gate_dw", shard_major=True)
    dw_up, _ = mm_tn(h2, du, "ffn_up_dw", shard_major=True)
    g_gu = [dw_gate, dw_up]
    (dx1, dp1, dya_att, dyb, cs_mid, dw_out, dw_bra, dw_brb), res = mid_bwd(
        dh2a, dh2b, x1, dx2, mix, mod8, g_ffn, p, ya, ybp, merged, ya_att, yb, w_o, w_bra, w_brb,
        comm=comm_join(rs_b(pre_down), rs_a(g_gu)))
    red_down = slab_sums(res[:1], pre_down, "down")
    pre_gu = halves(g_gu, res[1:], "gu")
    g_mix3 = [dw_bra.reshape(512, 4, 256).transpose(1, 0, 2), dw_brb.reshape(256, 4, 256).transpose(1, 0, 2), dw_out.reshape(4, 256, D)]
    (dp2, dfrow, dfcol), res = fox_bwd(p, dya_att, ya_att, gcol, fcol, dp1,
                                       comm=comm_join(rs_b(pre_gu), rs_c(red_down), rs_a(g_mix3)))
    red_gu = slab_sums(res[:2], pre_gu, "gu")
    r_down = res[2]
    pre_mix3 = halves(g_mix3, res[3:], "mix")
    dp3, db_fg = fgate_bwd(dfrow.reshape(8, S), dfcol, fraw, dp2)
    dp4, res = dil_bwd(p, dyb, yb, lse_b, tabs, dp3, comm=comm_join(rs_c(red_gu), rs_b(pre_mix3)))
    r_gate, r_up = res[:2]
    red_mix3 = slab_sums(res[2:], pre_mix3, "mix")
    dw_lay, (r_bra, r_brb, r_out) = mm_tn(h1, dp4, "in_proj_dw", comm=rs_c(red_mix3))
    g_in4 = [shards_from_lay(dw_lay)]
    dh1, oth = mm_nt(dp4, w_lay, "in_proj_dx", comm=rs_a(g_in4))
    (pre_in,) = halves(g_in4, oth, "in")
    qrows = pre_in.shape[1] // 4
    (dx, cs_in), (t_in,) = in_bwd_tail(dh1, xs, dx1, mod8, g_mix, comm=rs_b_rows(pre_in, None, 0, qrows))

    dmod = jnp.concatenate([cs_in[0:2], cs_mid[3:4], cs_mid[0:2], dga_f], axis=0)
    small = dict(dmod=dmod, dg_mix=cs_in[2:3], dg_ffn=cs_mid[2:3], dg_final=dg_final, db_fgate=db_fg[:, 0], loss=loss_part[0, 0])
    sv = jnp.concatenate([
        small["dmod"].reshape(48, LANES), small["dg_mix"].reshape(8, LANES), small["dg_ffn"].reshape(8, LANES),
        small["dg_final"].reshape(8, LANES), jnp.pad(small["db_fgate"], (0, LANES - 8)).reshape(1, LANES),
        jnp.broadcast_to(small["loss"], (1, LANES)), jnp.zeros((SMALL_ROWS - 74, LANES), F32)], axis=0)
    sv_all, sv_sum = gather_all(sv, "gather_small", True)
    loss = sv_sum[73, 0]
    g_small = dict(b_ada=sv_sum[0:48].reshape(1, 6 * D), g_mix=sv_sum[48:56].reshape(1, D), g_ffn=sv_sum[56:64].reshape(1, D),
                   g_final=sv_sum[64:72].reshape(D), b_fgate=sv_sum[72, 0:8].reshape(1, 8))

    dmod_all = lax.dynamic_slice(sv_all[:, 0:48, :].reshape(8, 6 * D), (0, chip * n_ada), (8, n_ada))
    (g_ada, d_ada, nm_ada, nv_ada), (t_in,) = adam_w_ada(
        jnp.pad(sc.T, ((0, 0), (0, LANES - 8))), jnp.pad(dmod_all, ((0, LANES - 8), (0, 0))), w_ada[0], m_w_ada[0], v_w_ada[0],
        comm=rs_b_rows(pre_in, t_in, qrows, 3 * qrows))

    big = dict(w_in=(w_in, m_w_in, v_w_in), w_br_a=(w_br_a, m_w_br_a, v_w_br_a), w_br_b=(w_br_b, m_w_br_b, v_w_br_b),
               w_out=(w_out, m_w_out, v_w_out), w_ffn_gate=(w_ffn_gate, m_w_ffn_gate, v_w_ffn_gate),
               w_ffn_up=(w_ffn_up, m_w_ffn_up, v_w_ffn_up), w_ffn_down=(w_ffn_down, m_w_ffn_down, v_w_ffn_down))
    gpad = dict(w_br_a=r_bra, w_br_b=r_brb, w_out=r_out, w_ffn_gate=r_gate, w_ffn_up=r_up, w_ffn_down=r_down)
    upd = {}
    for nm in ("w_ffn_gate", "w_ffn_up"):
        w, m, v = big[nm]
        upd[nm] = [t.T for t in adam(w[0].T, gpad[nm], m[0].T, v[0].T, "adam_" + nm)[0]]
    (gpad["w_in"],) = comm_only(rs_c(slab_sums([t_in], [pre_in], "in")), "rs_in_share")
    for nm, (w, m, v) in big.items():
        if nm not in upd:
            upd[nm] = adam(w[0], gpad[nm], m[0], v[0], "adam_" + nm)[0]

    small_names = ["g_mix", "g_ffn", "g_final", "b_ada", "b_fgate"]
    small_w = dict(g_mix=(g_mix, m_g_mix, v_g_mix), g_ffn=(g_ffn, m_g_ffn, v_g_ffn), g_final=(g_final, m_g_final, v_g_final),
                   b_ada=(b_ada, m_b_ada, v_b_ada), b_fgate=(b_fgate, m_b_fgate, v_b_fgate))
    row = lambda t: t.reshape(1, -1)
    res = adam_small([[row(t) for t in small_w[nm]] + [row(g_small[nm])] for nm in small_names])
    small_upd = [{nm: res[i][which].reshape(small_w[nm][0].shape) for i, nm in enumerate(small_names)} for which in range(3)]
    order =["w_ada", "b_ada", "g_mix", "w_in", "b_fgate", "w_br_a", "w_br_b", "w_out", "g_ffn", "w_ffn_gate", "w_ffn_up", "w_ffn_down", "g_final"]

    def leaf(nm, which):
        if nm == "w_ada":
            return (g_ada, d_ada, nm_ada, nv_ada)[which][None]
        if nm in big:
            return upd[nm][which][None]
        return g_small[nm] if which == 0 else small_upd[which - 1][nm]

    outs = [loss, dx[None]]
    for which in range(4):
        outs += [leaf(nm, which) for nm in order]
    return tuple(outs)
```

```python
import functools

import numpy as np
import jax
import jax.numpy as jnp
from jax import lax
from jax.experimental import pallas as pl
from jax.experimental.pallas import tpu as pltpu

F32, BF16 = jnp.float32, jnp.bfloat16
S, D = 2048, 1024
HD = 64
LANES = 128
N_FOX_PAIRS, N_DIL_PAIRS = 4, 2
DIL_GROUPS = ((1, 16), (4, 4), (16, 1))
SPAN = 128
ROT_DIM, ROPE_THETA = 16, 500000.0
D_FF, FF_SHARD, FF_PAD = 2816, 704, 768
FFP = 4 * FF_PAD
IN_COLS, IN_SHARD, IN_SHARD_PAD = 5896, 1474, 1536
LAY_B, LAY_A, LAY_F, LAY_G, LAY_N = 0, 2304, 3840, 4096, 6144
EPS, NEG = 1e-6, -1e30
SCALE = HD ** -0.5
ADAM_LR, ADAM_B1, ADAM_B2, ADAM_EPS, ADAM_WD, ADAM_STEP = 0.001, 0.9, 0.999, 1e-08, 0.01, 10
VMEM_MB = 56
MESH = pl.DeviceIdType.MESH


def _params(vmem_mb=None, **kw):
    if vmem_mb is not None:
        kw["vmem_limit_bytes"] = vmem_mb * 1024 * 1024
    return pltpu.CompilerParams(**kw)


def _sds(shape, dtype):
    return jax.ShapeDtypeStruct(shape, dtype)


def _sigmoid(x):
    return 1.0 / (1.0 + jnp.exp(-x))


def _colsum8(x):
    tm, n = x.shape
    return jnp.sum(x.reshape(tm // 8, 8, n), axis=0)


class Comm:
    def __init__(self, ins, out_shapes, sems, start, wait, aliases=None):
        self.ins, self.out_shapes, self.sems = list(ins), list(out_shapes), list(sems)
        self.start, self.wait, self.aliases = start, wait, dict(aliases or {})


def _hosted_call(body, comm, args, *, name, grid, in_specs, out_specs, out_shape, scratch_shapes=(), aliases=None, vmem_mb=None):
    single = not isinstance(out_shape, (list, tuple))
    out_specs_l = [out_specs] if single else list(out_specs)
    out_shape_l = [out_shape] if single else list(out_shape)
    n_in, n_out, n_scr = len(in_specs), len(out_shape_l), len(scratch_shapes)
    aliases = dict(aliases or {})
    if comm is None:
        res = pl.pallas_call(body, name=name, grid=grid, in_specs=list(in_specs), out_specs=out_specs, out_shape=out_shape,
                             scratch_shapes=list(scratch_shapes), input_output_aliases=aliases,
                             compiler_params=_params(vmem_mb))(*args)
        return res, []
    nci, nco = len(comm.ins), len(comm.out_shapes)

    def wrapped(*refs):
        main_in, cin = refs[:n_in], refs[n_in:n_in + nci]
        o0 = n_in + nci
        main_out, cout = refs[o0:o0 + n_out], refs[o0 + n_out:o0 + n_out + nco]
        s0 = o0 + n_out + nco
        scr, sems = refs[s0:s0 + n_scr], refs[s0 + n_scr:]
        ids = [pl.program_id(i) for i in range(len(grid))]
        first = functools.reduce(jnp.logical_and, [i == 0 for i in ids])
        last = functools.reduce(jnp.logical_and, [i == g - 1 for i, g in zip(ids, grid)])

        @pl.when(first)
        def _():
            comm.start(cin, cout, sems)

        body(*main_in, *main_out, *scr)

        @pl.when(last)
        def _():
            comm.wait(cin, cout, sems)

    for ci, co in comm.aliases.items():
        aliases[n_in + ci] = n_out + co
    any_spec = pl.BlockSpec(memory_space=pl.ANY)
    res = pl.pallas_call(
        wrapped, name=name, grid=grid, in_specs=list(in_specs) + [any_spec] * nci, out_specs=out_specs_l + [any_spec] * nco,
        out_shape=out_shape_l + comm.out_shapes,
        scratch_shapes=list(scratch_shapes) + [pltpu.SemaphoreType.DMA((s,)) for s in comm.sems],
        input_output_aliases=aliases, compiler_params=_params(vmem_mb))(*args, *comm.ins)
    main = list(res[:n_out])
    return (main[0] if single else main), list(res[n_out:])


def norm_mod_fwd(x, g, mod, shift_row, scale_row, comm=None):
    tm = 256

    def body(x_ref, g_ref, mod_ref, h_ref):
        xv = x_ref[...]
        r = lax.rsqrt(jnp.mean(xv * xv, axis=1, keepdims=True) + EPS)
        n = xv * r * g_ref[...]
        h = n * (1.0 + mod_ref[scale_row:scale_row + 1, :]) + mod_ref[shift_row:shift_row + 1, :]
        h_ref[...] = h.astype(BF16)

    return _hosted_call(
        body, comm, (x, g, mod), name="norm_mod_fwd", grid=(S // tm,),
        in_specs=[pl.BlockSpec((tm, D), lambda i: (i, 0)), pl.BlockSpec((1, D), lambda i: (0, 0)),
                  pl.BlockSpec((8, D), lambda i: (0, 0))],
        out_specs=pl.BlockSpec((tm, D), lambda i: (i, 0)),
        out_shape=_sds((S, D), BF16))


def rope_tables():
    pos = jnp.arange(S, dtype=F32)
    inv_freq = ROPE_THETA ** (-jnp.arange(0, ROT_DIM, 2, dtype=F32) / ROT_DIM)
    ang = pos[:, None] * inv_freq[None, :]
    cos, sin = jnp.cos(ang), jnp.sin(ang)
    one, zero = jnp.ones((S, HD - ROT_DIM), F32), jnp.zeros((S, HD - ROT_DIM), F32)
    z8 = jnp.zeros((S, 8), F32)
    c = jnp.concatenate([cos, cos, one], axis=1)
    s1 = jnp.concatenate([-sin, z8, zero], axis=1)
    s2 = jnp.concatenate([z8, sin, zero], axis=1)
    return tuple(jnp.concatenate([t, t], axis=1) for t in (c, s1, s2))


def _rope(y, c, s1, s2):
    return y * c + pltpu.roll(y, LANES - 8, 1) * s1 + pltpu.roll(y, 8, 1) * s2


def _rope_bwd(dy, c, s1, s2):
    return dy * c + pltpu.roll(dy * s1, 8, 1) + pltpu.roll(dy * s2, LANES - 8, 1)


def in_proj_fwd(h, w_lay, tabs, comm=None):
    tm, tn = 2048, 768
    n_rope = N_DIL_PAIRS * 3 // 2

    def body(a_ref, w_ref, c_ref, s1_ref, s2_ref, o_ref):
        j = pl.program_id(0)
        y = jnp.dot(a_ref[...], w_ref[...], preferred_element_type=F32)

        @pl.when(j < n_rope)
        def _():
            c, s1, s2 = c_ref[...], s1_ref[...], s2_ref[...]
            for t in range(tn // LANES):
                chunk = y[:, LANES * t:LANES * (t + 1)]
                o_ref[:, LANES * t:LANES * (t + 1)] = chunk if t % 3 == 2 else _rope(chunk, c, s1, s2)

        @pl.when(j >= n_rope)
        def _():
            o_ref[...] = y

    tab = pl.BlockSpec((tm, LANES), lambda j, i: (i, 0))
    return _hosted_call(
        body, comm, (h, w_lay, *tabs), name="in_proj_fwd", grid=(LAY_N // tn, S // tm),
        in_specs=[pl.BlockSpec((tm, D), lambda j, i: (i, 0)), pl.BlockSpec((D, tn), lambda j, i: (0, j)), tab, tab, tab],
        out_specs=pl.BlockSpec((tm, tn), lambda j, i: (i, j)),
        out_shape=_sds((S, LAY_N), F32), vmem_mb=VMEM_MB)


def _log1p_small(t):
    return jnp.where(t < 1e-2, t * (1.0 - t * (0.5 - t * (1.0 / 3.0))), jnp.log(1.0 + t))


def fgate_fwd(p, b_pad):
    def body(fa_ref, b_ref, fraw_ref, fcol_ref):
        f = fa_ref[...] + b_ref[...]
        fr = f.T[0:8, :]
        ls = jnp.minimum(fr, 0.0) - _log1p_small(jnp.exp(-jnp.abs(fr)))
        lane = lax.broadcasted_iota(jnp.int32, (8, S), 1)
        acc, sh = ls, 1
        while sh < S:
            acc = acc + jnp.where(lane >= sh, pltpu.roll(acc, sh, 1), 0.0)
            sh *= 2
        fraw_ref[...] = fr
        for hh in range(8):
            fcol_ref[hh] = jnp.broadcast_to(acc[hh:hh + 1, :], (LANES, S)).T

    return pl.pallas_call(
        body, name="fgate_fwd", grid=(1,),
        in_specs=[pl.BlockSpec((S, LANES), lambda i: (0, LAY_F // LANES)), pl.BlockSpec((1, LANES), lambda i: (0, 0))],
        out_specs=[pl.BlockSpec((8, S), lambda i: (0, 0)), pl.BlockSpec((8, S, LANES), lambda i: (0, 0, 0))],
        out_shape=[_sds((8, S), F32), _sds((8, S, LANES), F32)],
        compiler_params=_params(VMEM_MB),
    )(p, b_pad)


def _head_masks(rows):
    lane = lax.broadcasted_iota(jnp.int32, (rows, LANES), 1)
    return lane < HD, lane >= HD


FT = 256


def _split3(f):
    hi = f.astype(BF16).astype(F32)
    r = f - hi
    mid = r.astype(BF16).astype(F32)
    return hi, mid, r - mid


def _fox_operands(qkv_ref, tcol_ref, scol_ref, qa_s, ka_s):
    rows = 256
    lane = lax.broadcasted_iota(jnp.int32, (rows, LANES), 1)

    def chunk(i, _):
        r = pl.ds(pl.multiple_of(i * rows, rows), rows)
        q, k = qkv_ref[r, 0:LANES], qkv_ref[r, LANES:2 * LANES]
        s0, s1 = _split3(scol_ref[0, r, :]), _split3(scol_ref[1, r, :])
        ka = jnp.where(lane == 0, -s0[0], jnp.where(lane == 1, -s0[1], jnp.where(lane == 2, -s0[2], jnp.where(
            lane == 3, -s1[0], jnp.where(lane == 4, -s1[1], jnp.where(lane == 5, -s1[2], jnp.where(lane < 9, 1.0, 0.0)))))))
        ka_s[r, 0:LANES] = k.astype(BF16)
        ka_s[r, LANES:2 * LANES] = ka.astype(BF16)
        for hh in range(2):
            own = (lane < HD) if hh == 0 else (lane >= HD)
            t3 = _split3(tcol_ref[hh, r, :])
            ones = (lane >= 3 * hh) & (lane < 3 * hh + 3)
            qa = jnp.where(ones, 1.0, jnp.where(lane == 6, t3[0], jnp.where(lane == 7, t3[1], jnp.where(lane == 8, t3[2], 0.0))))
            qa_s[hh, r, 0:LANES] = jnp.where(own, q * SCALE, 0.0).astype(BF16)
            qa_s[hh, r, LANES:2 * LANES] = qa.astype(BF16)
        return 0

    lax.fori_loop(0, S // rows, chunk, 0)


def fox_fwd(p, fcol, comm=None):
    nt = (((1,), (1,)), ((), ()))

    def body(qkv_ref, fc_ref, o_ref, g_ref, qa_s, ka_s):
        _fox_operands(qkv_ref, fc_ref, fc_ref, qa_s, ka_s)
        masks = _head_masks(FT)
        causal = lax.broadcasted_iota(jnp.int32, (FT, FT), 1) <= lax.broadcasted_iota(jnp.int32, (FT, FT), 0)
        causal2 = jnp.concatenate([causal, causal], axis=0)

        def qloop(qi, _):
            q0 = pl.multiple_of(qi * FT, FT)
            qa = jnp.concatenate([qa_s[0, pl.ds(q0, FT), :], qa_s[1, pl.ds(q0, FT), :]], axis=0)

            def step(kb, carry, diagonal, tiles=1):
                m, l, acc = carry
                k0 = pl.multiple_of(kb * FT, FT)
                keys = pl.ds(k0, tiles * FT)
                v = qkv_ref[keys, 2 * LANES:3 * LANES].astype(BF16)
                s = lax.dot_general(qa, ka_s[keys, :], nt, preferred_element_type=F32)
                if diagonal:
                    s = jnp.where(causal2, s, NEG)
                m_new = jnp.maximum(m, jnp.max(s, axis=1, keepdims=True))
                pr = jnp.exp(s - m_new)
                alpha = jnp.exp(m - m_new)
                return (m_new, l * alpha + jnp.sum(pr, axis=1, keepdims=True),
                        acc * alpha + jnp.dot(pr.astype(BF16), v, preferred_element_type=F32))

            init = (jnp.full((2 * FT, 1), NEG, F32), jnp.zeros((2 * FT, 1), F32), jnp.zeros((2 * FT, LANES), F32))
            pairs = qi // 2
            carry = lax.fori_loop(0, pairs, lambda j, cr: step(2 * j, cr, False, 2), init)
            carry = lax.fori_loop(2 * pairs, qi, lambda kb, cr: step(kb, cr, False), carry)
            m, l, acc = step(qi, carry, True)
            out = acc / l
            lse = m + jnp.log(l)
            o_ref[pl.ds(q0, FT), :] = jnp.where(masks[0], out[:FT], out[FT:]).astype(BF16)
            g_ref[0, pl.ds(q0, FT), :] = fc_ref[0, pl.ds(q0, FT), :] - lse[:FT]
            g_ref[1, pl.ds(q0, FT), :] = fc_ref[1, pl.ds(q0, FT), :] - lse[FT:]
            return 0

        lax.fori_loop(0, S // FT, qloop, 0)

    a_blk = LAY_A // 384
    return _hosted_call(
        body, comm, (p, fcol), name="fox_fwd", grid=(N_FOX_PAIRS,),
        in_specs=[pl.BlockSpec((S, 384), lambda p_: (0, a_blk + p_)), pl.BlockSpec((2, S, LANES), lambda p_: (p_, 0, 0))],
        out_specs=[pl.BlockSpec((S, LANES), lambda p_: (0, p_)), pl.BlockSpec((2, S, LANES), lambda p_: (p_, 0, 0))],
        out_shape=[_sds((S, 4 * LANES), BF16), _sds((8, S, LANES), F32)],
        scratch_shapes=[pltpu.VMEM((2, S, 2 * LANES), BF16), pltpu.VMEM((S, 2 * LANES), BF16)],
        vmem_mb=VMEM_MB)


def _dil_rows(ref, start, d):
    return ref[pl.ds(start, SPAN), :] if d == 1 else ref[pl.ds(start, SPAN, stride=d), :]


def _dil_store(ref, start, d, val):
    if d == 1:
        ref[pl.ds(start, SPAN), :] = val
    else:
        ref[pl.ds(start, SPAN, stride=d), :] = val


def _band_mask(has_prev):
    qi = lax.broadcasted_iota(jnp.int32, (SPAN, 2 * SPAN), 0) + SPAN
    kj = lax.broadcasted_iota(jnp.int32, (SPAN, 2 * SPAN), 1)
    dist = qi - kj
    return (dist >= 0) & (dist <= SPAN) & (has_prev | (kj >= SPAN))


def _dil_block(n, d, nb):
    r, j = n // nb, n % nb
    start = r + d * SPAN * j
    prev = jnp.maximum(start - d * SPAN, r)
    return start, prev, j > 0


def dil_fwd(p, comm=None):
    def body(*refs):
        qkv = [refs[3 * g:3 * g + 3] for g in range(3)]
        y_ref, lse_ref = refs[9], refs[10]
        acc_s, m_s, l_s = refs[11], refs[12], refs[13]
        masks = _head_masks(SPAN)
        for g, (d, nb) in enumerate(DIL_GROUPS):
            q_ref, k_ref, v_ref = qkv[g]

            def blk(n, _):
                start, prev, has_prev = _dil_block(n, d, nb)
                q = _dil_rows(q_ref, start, d)
                kc = jnp.concatenate([_dil_rows(k_ref, prev, d), _dil_rows(k_ref, start, d)], axis=0).astype(BF16)
                vc = jnp.concatenate([_dil_rows(v_ref, prev, d), _dil_rows(v_ref, start, d)], axis=0).astype(BF16)
                valid = _band_mask(has_prev)
                valid2 = jnp.concatenate([valid, valid], axis=0)
                q2 = (jnp.concatenate([jnp.where(masks[0], q, 0.0), jnp.where(masks[1], q, 0.0)], axis=0) * SCALE).astype(BF16)
                s = jnp.where(valid2, lax.dot_general(q2, kc, (((1,), (1,)), ((), ())), preferred_element_type=F32), NEG)
                m = jnp.max(s, axis=1, keepdims=True)
                pr = jnp.exp(s - m)
                l = jnp.sum(pr, axis=1, keepdims=True)
                acc = jnp.dot(pr.astype(BF16), vc, preferred_element_type=F32)
                _dil_store(acc_s.at[g], start, d, jnp.where(masks[0], acc[:SPAN], acc[SPAN:]))
                _dil_store(m_s.at[g], start, d, jnp.where(masks[0], m[:SPAN], m[SPAN:]))
                _dil_store(l_s.at[g], start, d, jnp.where(masks[0], l[:SPAN], l[SPAN:]))
                return 0

            lax.fori_loop(0, 16, blk, 0)

        def merge(i, _):
            rows = pl.ds(pl.multiple_of(i * 256, 256), 256)
            m = [m_s[g, rows, :] for g in range(3)]
            mx = jnp.maximum(jnp.maximum(m[0], m[1]), m[2])
            w = [jnp.exp(m[g] - mx) for g in range(3)]
            l = sum(l_s[g, rows, :] * w[g] for g in range(3))
            y_ref[rows, :] = sum(acc_s[g, rows, :] * w[g] for g in range(3)) / l
            lse_ref[rows, :] = mx + jnp.log(l)
            return 0

        lax.fori_loop(0, S // 256, merge, 0)

    def spec(g, t):
        return pl.BlockSpec((S, LANES), lambda p_: (0, (p_ * 3 + g) * 3 + t))

    return _hosted_call(
        body, comm, [p] * 9, name="dil_fwd", grid=(N_DIL_PAIRS,),
        in_specs=[spec(g, t) for g in range(3) for t in range(3)],
        out_specs=[pl.BlockSpec((S, LANES), lambda p_: (0, p_)), pl.BlockSpec((S, LANES), lambda p_: (0, p_))],
        out_shape=[_sds((S, 2 * LANES), F32), _sds((S, 2 * LANES), F32)],
        scratch_shapes=[pltpu.VMEM((3, S, LANES), F32)] * 3,
        vmem_mb=VMEM_MB)


def merge_fwd(ya_att, yb, p, w_bra, w_brb, comm=None):
    tm = 256
    gblk = LAY_G // D

    def body(a_ref, b_ref, ga_ref, gb_ref, wa_ref, wb_ref, mg_ref, ya_ref, yb_ref):
        ya = jnp.dot(a_ref[...], wa_ref[...], preferred_element_type=F32)
        ybp = jnp.dot(b_ref[...].astype(BF16), wb_ref[...], preferred_element_type=F32)
        mg_ref[...] = (_sigmoid(ga_ref[...]) * ya + _sigmoid(gb_ref[...]) * ybp).astype(BF16)
        ya_ref[...] = ya
        yb_ref[...] = ybp

    row = lambda w: pl.BlockSpec((tm, w), lambda i: (i, 0))
    return _hosted_call(
        body, comm, (ya_att, yb, p, p, w_bra, w_brb), name="merge_fwd", grid=(S // tm,),
        in_specs=[row(512), row(256), pl.BlockSpec((tm, D), lambda i: (i, gblk)), pl.BlockSpec((tm, D), lambda i: (i, gblk + 1)),
                  pl.BlockSpec((512, D), lambda i: (0, 0)), pl.BlockSpec((256, D), lambda i: (0, 0))],
        out_specs=[row(D), row(D), row(D)],
        out_shape=[_sds((S, D), BF16), _sds((S, D), F32), _sds((S, D), F32)])


def out_proj_fwd(merged, w_out, x, mod, g_ffn):
    tm = 256

    def body(a_ref, w_ref, x_ref, mod_ref, g_ref, mix_ref, x1_ref, h2_ref):
        mix = jnp.dot(a_ref[...], w_ref[...], preferred_element_type=F32)
        x1 = x_ref[...] + mod_ref[2:3, :] * mix
        r = lax.rsqrt(jnp.mean(x1 * x1, axis=1, keepdims=True) + EPS)
        h2 = (x1 * r * g_ref[...]) * (1.0 + mod_ref[4:5, :]) + mod_ref[3:4, :]
        mix_ref[...] = mix
        x1_ref[...] = x1
        h2_ref[...] = h2.astype(BF16)

    row = pl.BlockSpec((tm, D), lambda i: (i, 0))
    return pl.pallas_call(
        body, name="out_proj_fwd", grid=(S // tm,),
        in_specs=[row, pl.BlockSpec((D, D), lambda i: (0, 0)), row, pl.BlockSpec((8, D), lambda i: (0, 0)),
                  pl.BlockSpec((1, D), lambda i: (0, 0))],
        out_specs=[row, row, row],
        out_shape=[_sds((S, D), F32), _sds((S, D), F32), _sds((S, D), BF16)],
    )(merged, w_out, x, mod, g_ffn)


def ffn_up_fwd(h2, w_gate, w_up):
    tm = 1024
    nt = (((1,), (1,)), ((), ()))

    def body(h_ref, wg_ref, wu_ref, a_ref, u_ref, z_ref):
        h = h_ref[...]
        a = lax.dot_general(h, wg_ref[...], nt, preferred_element_type=F32)
        u = lax.dot_general(h, wu_ref[...], nt, preferred_element_type=F32)
        a_ref[...] = a
        u_ref[...] = u
        z_ref[...] = (a * _sigmoid(a) * u).astype(BF16)

    out = pl.BlockSpec((tm, FF_PAD), lambda k, i: (i, k))
    return pl.pallas_call(
        body, name="ffn_up_fwd", grid=(4, S // tm),
        in_specs=[pl.BlockSpec((tm, D), lambda k, i: (i, 0)), pl.BlockSpec((None, FF_PAD, D), lambda k, i: (k, 0, 0)),
                  pl.BlockSpec((None, FF_PAD, D), lambda k, i: (k, 0, 0))],
        out_specs=[out, out, out],
        out_shape=[_sds((S, FFP), F32), _sds((S, FFP), F32), _sds((S, FFP), BF16)], compiler_params=_params(VMEM_MB),
    )(h2, w_gate, w_up)


def ffn_down_loss(z, w_down, x1, mod, g_final, tgt):
    tm = 256

    def body(z_ref, w_ref, x1_ref, mod_ref, g_ref, t_ref, dx2_ref, dffn_ref, dg_ref, dga_ref, loss_ref, s_dg, s_dga, s_loss):
        i = pl.program_id(0)

        @pl.when(i == 0)
        def _():
            s_dg[...] = jnp.zeros_like(s_dg)
            s_dga[...] = jnp.zeros_like(s_dga)
            s_loss[...] = jnp.zeros_like(s_loss)

        ffn = jnp.dot(z_ref[...], w_ref[...], preferred_element_type=F32)
        gaf = mod_ref[5:6, :]
        x2 = x1_ref[...] + gaf * ffn
        r = lax.rsqrt(jnp.mean(x2 * x2, axis=1, keepdims=True) + EPS)
        xh = x2 * r
        g = g_ref[...]
        e = xh * g - t_ref[...]
        s_loss[...] += 0.5 * jnp.sum(jnp.mean(e * e, axis=1, keepdims=True), axis=0, keepdims=True)
        dy = e * (1.0 / D)
        gdy = dy * g
        dx2 = r * (gdy - xh * jnp.mean(gdy * xh, axis=1, keepdims=True))
        s_dg[...] += _colsum8(dy * xh)
        s_dga[...] += _colsum8(dx2 * ffn)
        dx2_ref[...] = dx2
        dffn_ref[...] = (dx2 * gaf).astype(BF16)

        @pl.when(i == pl.num_programs(0) - 1)
        def _():
            dg_ref[...] = jnp.sum(s_dg[...], axis=0, keepdims=True)
            dga_ref[...] = jnp.sum(s_dga[...], axis=0, keepdims=True)
            loss_ref[...] = jnp.broadcast_to(s_loss[...], (1, LANES))

    row = pl.BlockSpec((tm, D), lambda i: (i, 0))
    vec = pl.BlockSpec((1, D), lambda i: (0, 0))
    return pl.pallas_call(
        body, name="ffn_down_loss", grid=(S // tm,),
        in_specs=[pl.BlockSpec((tm, FFP), lambda i: (i, 0)), pl.BlockSpec((FFP, D), lambda i: (0, 0)), row,
                  pl.BlockSpec((8, D), lambda i: (0, 0)), vec, row],
        out_specs=[row, row, vec, vec, pl.BlockSpec((1, LANES), lambda i: (0, 0))],
        out_shape=[_sds((S, D), F32), _sds((S, D), BF16), _sds((1, D), F32), _sds((1, D), F32), _sds((1, LANES), F32)],
        scratch_shapes=[pltpu.VMEM((8, D), F32), pltpu.VMEM((8, D), F32), pltpu.VMEM((1, 1), F32)],
        compiler_params=_params(VMEM_MB),
    )(z, w_down, x1, mod, g_final, tgt)


def ffn_down_bwd(dffn, w_down, a, u, z):
    tm, tn = 1024, 768

    def body(d_ref, w_ref, a_ref, u_ref, z_ref, da_ref, du_ref, dw_ref):
        i = pl.program_id(1)
        dff = d_ref[...]
        dz = lax.dot_general(dff, w_ref[...], (((1,), (1,)), ((), ())), preferred_element_type=F32)
        av, uv = a_ref[...], u_ref[...]
        sg = _sigmoid(av)
        du_ref[...] = (dz * (av * sg)).astype(BF16)
        da_ref[...] = (dz * uv * (sg * (1.0 + av * (1.0 - sg)))).astype(BF16)
        dw = lax.dot_general(z_ref[...], dff, (((0,), (0,)), ((), ())), preferred_element_type=F32)

        @pl.when(i == 0)
        def _():
            dw_ref[...] = dw

        @pl.when(i > 0)
        def _():
            dw_ref[...] += dw

    tile = pl.BlockSpec((tm, tn), lambda j, i: (i, j))
    return pl.pallas_call(
        body, name="ffn_down_bwd", grid=(FFP // tn, S // tm),
        in_specs=[pl.BlockSpec((tm, D), lambda j, i: (i, 0)), pl.BlockSpec((tn, D), lambda j, i: (j, 0)), tile, tile, tile],
        out_specs=[tile, tile, pl.BlockSpec((tn, D), lambda j, i: (j, 0))],
        out_shape=[_sds((S, FFP), BF16), _sds((S, FFP), BF16), _sds((FFP, D), F32)], compiler_params=_params(VMEM_MB),
    )(dffn, w_down, a, u, z)


def mm_nt(dy, w, name, comm=None):
    tm = 1024
    n = dy.shape[1]
    if w.ndim == 2:
        k_in, tk = w.shape[0], 1536
        w_spec = pl.BlockSpec((k_in, tk), lambda i, k: (0, k))
        dims = (((1,), (1,)), ((), ()))
    else:
        k_in, tk = w.shape[2], FF_PAD
        w_spec = pl.BlockSpec((None, tk, k_in), lambda i, k: (k, 0, 0))
        dims = (((1,), (0,)), ((), ()))
    nk = n // tk

    def body(d_ref, w_ref, o_ref, acc):
        k = pl.program_id(1)
        part = lax.dot_general(d_ref[...], w_ref[...], dims, preferred_element_type=F32)

        @pl.when(k == 0)
        def _():
            acc[...] = part

        @pl.when(k > 0)
        def _():
            acc[...] += part

        @pl.when(k == nk - 1)
        def _():
            o_ref[...] = acc[...]

    return _hosted_call(
        body, comm, (dy, w), name=name, grid=(S // tm, nk),
        in_specs=[pl.BlockSpec((tm, tk), lambda i, k: (i, k)), w_spec],
        out_specs=pl.BlockSpec((tm, k_in), lambda i, k: (i, 0)),
        out_shape=_sds((S, k_in), F32),
        scratch_shapes=[pltpu.VMEM((tm, k_in), F32)], vmem_mb=VMEM_MB)


def mm_tn(h, dy, name, shard_major=False, comm=None):
    tm, tn = 2048, (768 if shard_major else 1536)
    k_in, n = h.shape[1], dy.shape[1]

    def body(h_ref, d_ref, o_ref):
        i = pl.program_id(1)
        ops = (d_ref[...], h_ref[...]) if shard_major else (h_ref[...], d_ref[...])
        dw = lax.dot_general(*ops, (((0,), (0,)), ((), ())), preferred_element_type=F32)

        @pl.when(i == 0)
        def _():
            o_ref[...] = dw

        @pl.when(i > 0)
        def _():
            o_ref[...] += dw

    if shard_major:
        out_spec, out_shape = pl.BlockSpec((None, tn, k_in), lambda j, i: (j, 0, 0)), _sds((n // tn, tn, k_in), F32)
    else:
        out_spec, out_shape = pl.BlockSpec((k_in, tn), lambda j, i: (0, j)), _sds((k_in, n), F32)
    return _hosted_call(
        body, comm, (h, dy), name=name, grid=(n // tn, S // tm),
        in_specs=[pl.BlockSpec((tm, k_in), lambda j, i: (i, 0)), pl.BlockSpec((tm, tn), lambda j, i: (i, j))],
        out_specs=out_spec, out_shape=out_shape, vmem_mb=VMEM_MB)


def mid_bwd(dh2a, dh2b, x1, dx2, mix, mod, g_ffn, p, ya, ybp, merged, ya_att, yb, w_out, w_bra, w_brb, comm=None):
    tm = 256
    gblk = LAY_G // D
    nsteps = S // tm

    def body(dha_ref, dhb_ref, x1_ref, dx2_ref, mix_ref, mod_ref, g_ref, ga_ref, gb_ref, ya_ref, yb_ref, mg_ref,
             att_ref, ybb_ref, wo_ref, wa_ref, wb_ref,
             dx1_ref, dpg_ref, datt_ref, dyb_ref, cs_ref, dwo_ref, dwa_ref, dwb_ref, s_cs):
        i = pl.program_id(0)

        @pl.when(i == 0)
        def _():
            s_cs[...] = jnp.zeros_like(s_cs)
            dwo_ref[...] = jnp.zeros_like(dwo_ref)
            dwa_ref[...] = jnp.zeros_like(dwa_ref)
            dwb_ref[...] = jnp.zeros_like(dwb_ref)

        x1 = x1_ref[...]
        g = g_ref[...]
        r = lax.rsqrt(jnp.mean(x1 * x1, axis=1, keepdims=True) + EPS)
        xh = x1 * r
        dh2 = dha_ref[...] + dhb_ref[...]
        s_cs[0] += _colsum8(dh2)
        s_cs[1] += _colsum8(dh2 * (xh * g))
        dn2 = dh2 * (1.0 + mod_ref[4:5, :])
        s_cs[2] += _colsum8(dn2 * xh)
        gd = dn2 * g
        dx1 = dx2_ref[...] + r * (gd - xh * jnp.mean(gd * xh, axis=1, keepdims=True))
        s_cs[3] += _colsum8(dx1 * mix_ref[...])
        dx1_ref[...] = dx1
        dmix = (dx1 * mod_ref[2:3, :]).astype(BF16)
        dmg = lax.dot_general(dmix, wo_ref[...], (((1,), (1,)), ((), ())), preferred_element_type=F32)
        sga, sgb = _sigmoid(ga_ref[...]), _sigmoid(gb_ref[...])
        dya = (dmg * sga).astype(BF16)
        dybp = (dmg * sgb).astype(BF16)
        dpg_ref[:, 0:D] = (dmg * ya_ref[...] * (sga * (1.0 - sga))).astype(BF16)
        dpg_ref[:, D:2 * D] = (dmg * yb_ref[...] * (sgb * (1.0 - sgb))).astype(BF16)
        datt_ref[...] = lax.dot_general(dya, wa_ref[...], (((1,), (1,)), ((), ())), preferred_element_type=F32).astype(BF16)
        dyb_ref[...] = lax.dot_general(dybp, wb_ref[...], (((1,), (1,)), ((), ())), preferred_element_type=F32)
        tn_dims = (((0,), (0,)), ((), ()))
        dwo_ref[...] += lax.dot_general(mg_ref[...], dmix, tn_dims, preferred_element_type=F32)
        dwa_ref[...] += lax.dot_general(att_ref[...], dya, tn_dims, preferred_element_type=F32)
        dwb_ref[...] += lax.dot_general(ybb_ref[...].astype(BF16), dybp, tn_dims, preferred_element_type=F32)

        @pl.when(i == nsteps - 1)
        def _():
            for t in range(4):
                cs_ref[t:t + 1, :] = jnp.sum(s_cs[t], axis=0, keepdims=True)
            cs_ref[4:8, :] = jnp.zeros((4, D), F32)

    row = lambda w: pl.BlockSpec((tm, w), lambda i: (i, 0))
    full = lambda a, b: pl.BlockSpec((a, b), lambda i: (0, 0))
    return _hosted_call(
        body, comm, (dh2a, dh2b, x1, dx2, mix, mod, g_ffn, p, p, ya, ybp, merged, ya_att, yb, w_out, w_bra, w_brb),
        name="mid_bwd", grid=(nsteps,),
        in_specs=[row(D), row(D), row(D), row(D), row(D), full(8, D), full(1, D),
                  pl.BlockSpec((tm, D), lambda i: (i, gblk)), pl.BlockSpec((tm, D), lambda i: (i, gblk + 1)),
                  row(D), row(D), row(D), row(512), row(256), full(D, D), full(512, D), full(256, D)],
        out_specs=[row(D), pl.BlockSpec((tm, 2 * D), lambda i: (i, LAY_G // (2 * D))), row(512), row(256), full(8, D),
                   full(D, D), full(512, D), full(256, D)],
        out_shape=[_sds((S, D), F32), _sds((S, LAY_N), BF16), _sds((S, 512), BF16), _sds((S, 256), F32), _sds((8, D), F32),
                   _sds((D, D), F32), _sds((512, D), F32), _sds((256, D), F32)],
        scratch_shapes=[pltpu.VMEM((4, 8, D), F32)],
        vmem_mb=VMEM_MB)


def fox_bwd(p, do, o, gcol, fcol, dp, comm=None):
    nq = S // FT
    nt = (((1,), (1,)), ((), ()))
    tn = (((0,), (0,)), ((), ()))

    def body(qkv_ref, do_ref, o_ref, g_ref, fc_ref, dp_in, dp_ref, df_ref, rs_ref, dq_s, qa_s, ka_s, dob_s, dl_s):
        del dp_in
        _fox_operands(qkv_ref, g_ref, fc_ref, qa_s, ka_s)
        masks = _head_masks(FT)
        lane = lax.broadcasted_iota(jnp.int32, (FT, LANES), 1)
        head0 = 2 * pl.program_id(0)
        causal = lax.broadcasted_iota(jnp.int32, (FT, FT), 1) <= lax.broadcasted_iota(jnp.int32, (FT, FT), 0)
        dq_s[...] = jnp.zeros_like(dq_s)
        rs_ref[...] = jnp.zeros_like(rs_ref)

        causal2 = jnp.concatenate([causal, causal], axis=0)

        def prep(i, _):
            r = pl.ds(pl.multiple_of(i * 256, 256), 256)
            m256 = _head_masks(256)
            dov, ov = do_ref[r, :].astype(F32), o_ref[r, :].astype(F32)
            for hh in range(2):
                dom = jnp.where(m256[hh], dov, 0.0)
                dob_s[hh, r, :] = dom.astype(BF16)
                dl_s[hh, r, :] = jnp.broadcast_to(jnp.sum(dom * ov, axis=1, keepdims=True), (256, LANES))
            return 0

        lax.fori_loop(0, S // 256, prep, 0)

        def stack(ref, q0, n, cols=slice(None)):
            return jnp.concatenate([ref[0, pl.ds(q0, n), cols], ref[1, pl.ds(q0, n), cols]], axis=0)

        def kloop(kb, _):
            k0 = pl.multiple_of(kb * FT, FT)
            k = qkv_ref[pl.ds(k0, FT), LANES:2 * LANES].astype(BF16)
            v = qkv_ref[pl.ds(k0, FT), 2 * LANES:3 * LANES].astype(BF16)
            ka = ka_s[pl.ds(k0, FT), :]

            def step(qi, carry, diagonal, tiles=1):
                dk, dv, df0, df1 = carry
                n = tiles * FT
                q0 = pl.multiple_of(qi * FT, FT)
                qa, dob = stack(qa_s, q0, n), stack(dob_s, q0, n)
                s = lax.dot_general(qa, ka, nt, preferred_element_type=F32)
                pr = jnp.exp(jnp.where(causal2, s, NEG)) if diagonal else jnp.exp(s)
                dpr = lax.dot_general(dob, v, nt, preferred_element_type=F32)
                ds = pr * (dpr - jnp.tile(stack(dl_s, q0, n), (1, FT // LANES)))
                dsb = ds.astype(BF16)
                dq = jnp.dot(dsb, k, preferred_element_type=F32) * SCALE
                dk = dk + lax.dot_general(dsb, qa[:, 0:LANES], tn, preferred_element_type=F32)
                dv = dv + lax.dot_general(pr.astype(BF16), dob, tn, preferred_element_type=F32)
                rsum = jnp.sum(ds, axis=1, keepdims=True)
                lane_n = lax.broadcasted_iota(jnp.int32, (n, LANES), 1)
                dq_s[pl.ds(q0, n), :] += jnp.where(lane_n < HD, dq[:n], dq[n:])
                rs_ref[pl.ds(q0, n), :] += jnp.where(lane_n == head0, rsum[:n], 0.0) + jnp.where(lane_n == head0 + 1, rsum[n:], 0.0)
                return (dk, dv, df0 - jnp.sum(ds[:n], axis=0, keepdims=True), df1 - jnp.sum(ds[n:], axis=0, keepdims=True))

            z = jnp.zeros((FT, LANES), F32)
            z1 = jnp.zeros((1, FT), F32)
            carry = step(kb, (z, z, z1, z1), True)
            pairs = (nq - 1 - kb) // 2
            carry = lax.fori_loop(0, pairs, lambda j, cr: step(kb + 1 + 2 * j, cr, False, 2), carry)
            dk, dv, df0, df1 = lax.fori_loop(kb + 1 + 2 * pairs, nq, lambda qi, cr: step(qi, cr, False), carry)
            dp_ref[pl.ds(k0, FT), LANES:2 * LANES] = dk.astype(BF16)
            dp_ref[pl.ds(k0, FT), 2 * LANES:3 * LANES] = dv.astype(BF16)
            df_ref[0:1, pl.ds(k0, FT)] = df0
            df_ref[1:2, pl.ds(k0, FT)] = df1
            return 0

        lax.fori_loop(0, S // FT, kloop, 0)
        dp_ref[:, 0:LANES] = dq_s[...].astype(BF16)

    a_blk = LAY_A // 384
    pair = pl.BlockSpec((S, LANES), lambda p_: (0, p_))
    heads = pl.BlockSpec((2, S, LANES), lambda p_: (p_, 0, 0))
    return _hosted_call(
        body, comm, (p, do, o, gcol, fcol, dp), name="fox_bwd", grid=(N_FOX_PAIRS,),
        in_specs=[pl.BlockSpec((S, 384), lambda p_: (0, a_blk + p_)), pair, pair, heads, heads, pl.BlockSpec(memory_space=pl.ANY)],
        out_specs=[pl.BlockSpec((S, 384), lambda p_: (0, a_blk + p_)), pl.BlockSpec((None, 2, S), lambda p_: (p_, 0, 0)),
                   pl.BlockSpec((None, S, LANES), lambda p_: (p_, 0, 0))],
        out_shape=[_sds((S, LAY_N), BF16), _sds((4, 2, S), F32), _sds((4, S, LANES), F32)],
        scratch_shapes=[pltpu.VMEM((S, LANES), F32), pltpu.VMEM((2, S, 2 * LANES), BF16), pltpu.VMEM((S, 2 * LANES), BF16),
                        pltpu.VMEM((2, S, LANES), BF16), pltpu.VMEM((2, S, LANES), F32)],
        aliases={5: 0}, vmem_mb=VMEM_MB)


def fgate_bwd(dfrow, dfcol, fraw, dp):
    def body(df_ref, dc_ref, f_ref, dp_in, dpf_ref, db_ref):
        del dp_in
        lane = lax.broadcasted_iota(jnp.int32, (8, S), 1)
        rsum = (dc_ref[0] + dc_ref[1]) + (dc_ref[2] + dc_ref[3])
        acc, sh = df_ref[...] + rsum.T[0:8, :], 1
        while sh < S:
            acc = acc + jnp.where(lane < S - sh, pltpu.roll(acc, S - sh, 1), 0.0)
            sh *= 2
        df = acc * _sigmoid(-f_ref[...])
        db_ref[...] = jnp.broadcast_to(jnp.sum(df, axis=1, keepdims=True), (8, LANES))
        dfc = jnp.concatenate([df, jnp.zeros((LANES - 8, S), F32)], axis=0).T
        dpf_ref[:, 0:LANES] = dfc.astype(BF16)
        dpf_ref[:, LANES:2 * LANES] = jnp.zeros((S, LANES), BF16)

    return pl.pallas_call(
        body, name="fgate_bwd", grid=(1,),
        in_specs=[pl.BlockSpec((8, S), lambda i: (0, 0)), pl.BlockSpec((4, S, LANES), lambda i: (0, 0, 0)),
                  pl.BlockSpec((8, S), lambda i: (0, 0)), pl.BlockSpec(memory_space=pl.ANY)],
        out_specs=[pl.BlockSpec((S, 2 * LANES), lambda i: (0, LAY_F // (2 * LANES))), pl.BlockSpec((8, LANES), lambda i: (0, 0))],
        out_shape=[_sds((S, LAY_N), BF16), _sds((8, LANES), F32)],
        input_output_aliases={3: 0},
        compiler_params=_params(VMEM_MB),
    )(dfrow, dfcol, fraw, dp)


def dil_bwd(p, dyb, yb, lse, tabs, dp, comm=None):
    def body(*refs):
        qkv = [refs[3 * g:3 * g + 3] for g in range(3)]
        dy_ref, y_ref, lse_ref, c_ref, s1_ref, s2_ref = refs[9:15]
        dp_ref = refs[16]
        dq_s, dk_s, dv_s, dl_s = refs[17:21]
        masks = _head_masks(SPAN)
        m256 = _head_masks(256)
        nt = (((1,), (1,)), ((), ()))
        tn = (((0,), (0,)), ((), ()))
        dk_s[...] = jnp.zeros_like(dk_s)
        dv_s[...] = jnp.zeros_like(dv_s)

        def prep(i, _):
            rows = pl.ds(pl.multiple_of(i * 256, 256), 256)
            pr = dy_ref[rows, :] * y_ref[rows, :]
            d0 = jnp.sum(jnp.where(m256[0], pr, 0.0), axis=1, keepdims=True)
            d1 = jnp.sum(jnp.where(m256[1], pr, 0.0), axis=1, keepdims=True)
            dl_s[rows, :] = jnp.where(m256[0], d0, d1)
            return 0

        lax.fori_loop(0, S // 256, prep, 0)

        for g, (d, nb) in enumerate(DIL_GROUPS):
            q_ref, k_ref, v_ref = qkv[g]

            def blk(n, _):
                start, prev, has_prev = _dil_block(n, d, nb)
                q = _dil_rows(q_ref, start, d)
                kc = jnp.concatenate([_dil_rows(k_ref, prev, d), _dil_rows(k_ref, start, d)], axis=0).astype(BF16)
                vc = jnp.concatenate([_dil_rows(v_ref, prev, d), _dil_rows(v_ref, start, d)], axis=0).astype(BF16)
                dov = _dil_rows(dy_ref, start, d)
                lsev = _dil_rows(lse_ref, start, d)
                dlv = _dil_rows(dl_s, start, d)
                valid = _band_mask(has_prev)
                valid2 = jnp.concatenate([valid, valid], axis=0)

                def stack(t):
                    return jnp.concatenate([jnp.where(masks[0], t, 0.0), jnp.where(masks[1], t, 0.0)], axis=0)

                def column(t):
                    return jnp.concatenate([jnp.max(jnp.where(masks[hh], t, NEG), axis=1, keepdims=True) for hh in range(2)], axis=0)

                q2 = (stack(q) * SCALE).astype(BF16)
                dob = stack(dov).astype(BF16)
                s = jnp.where(valid2, lax.dot_general(q2, kc, nt, preferred_element_type=F32), NEG)
                pr = jnp.exp(s - column(lsev))
                dpr = lax.dot_general(dob, vc, nt, preferred_element_type=F32)
                dsb = (pr * (dpr - column(dlv))).astype(BF16)
                dq = jnp.dot(dsb, kc, preferred_element_type=F32) * SCALE
                dkc = lax.dot_general(dsb, q2, tn, preferred_element_type=F32)
                dvc = lax.dot_general(pr.astype(BF16), dob, tn, preferred_element_type=F32)
                _dil_store(dq_s.at[g], start, d, jnp.where(masks[0], dq[:SPAN], dq[SPAN:]))
                for ref, val in ((dk_s.at[g], dkc), (dv_s.at[g], dvc)):
                    _dil_store(ref, prev, d, _dil_rows(ref, prev, d) + jnp.where(has_prev, val[0:SPAN], 0.0))
                    _dil_store(ref, start, d, _dil_rows(ref, start, d) + val[SPAN:])
                return 0

            lax.fori_loop(0, 16, blk, 0)

        def fin(i, _):
            rows = pl.ds(pl.multiple_of(i * 256, 256), 256)
            c, s1, s2 = c_ref[rows, :], s1_ref[rows, :], s2_ref[rows, :]
            for g in range(3):
                base = g * 384
                dp_ref[rows, base:base + LANES] = _rope_bwd(dq_s[g, rows, :], c, s1, s2).astype(BF16)
                dp_ref[rows, base + LANES:base + 2 * LANES] = _rope_bwd(dk_s[g, rows, :], c, s1, s2).astype(BF16)
                dp_ref[rows, base + 2 * LANES:base + 3 * LANES] = dv_s[g, rows, :].astype(BF16)
            return 0

        lax.fori_loop(0, S // 256, fin, 0)

    def spec(g, t):
        return pl.BlockSpec((S, LANES), lambda p_: (0, (p_ * 3 + g) * 3 + t))

    pair = pl.BlockSpec((S, LANES), lambda p_: (0, p_))
    tab = pl.BlockSpec((S, LANES), lambda p_: (0, 0))
    return _hosted_call(
        body, comm, [p] * 9 + [dyb, yb, lse, *tabs, dp], name="dil_bwd", grid=(N_DIL_PAIRS,),
        in_specs=[spec(g, t) for g in range(3) for t in range(3)] + [pair, pair, pair, tab, tab, tab, pl.BlockSpec(memory_space=pl.ANY)],
        out_specs=pl.BlockSpec((S, 1152), lambda p_: (0, p_)),
        out_shape=_sds((S, LAY_N), BF16),
        scratch_shapes=[pltpu.VMEM((3, S, LANES), F32)] * 3 + [pltpu.VMEM((S, LANES), F32)],
        aliases={15: 0}, vmem_mb=VMEM_MB)


def in_bwd_tail(dh1, x, dx1, mod, g_mix, comm=None):
    tm = 256
    nsteps = S // tm

    def body(dh_ref, x_ref, dx1_ref, mod_ref, g_ref, dx_ref, cs_ref, s_cs):
        i = pl.program_id(0)

        @pl.when(i == 0)
        def _():
            s_cs[...] = jnp.zeros_like(s_cs)

        xv, g, dh = x_ref[...], g_ref[...], dh_ref[...]
        r = lax.rsqrt(jnp.mean(xv * xv, axis=1, keepdims=True) + EPS)
        xh = xv * r
        s_cs[0] += _colsum8(dh)
        s_cs[1] += _colsum8(dh * (xh * g))
        dn = dh * (1.0 + mod_ref[1:2, :])
        s_cs[2] += _colsum8(dn * xh)
        gd = dn * g
        dx_ref[...] = dx1_ref[...] + r * (gd - xh * jnp.mean(gd * xh, axis=1, keepdims=True))

        @pl.when(i == nsteps - 1)
        def _():
            for t in range(3):
                cs_ref[t:t + 1, :] = jnp.sum(s_cs[t], axis=0, keepdims=True)
            cs_ref[3:8, :] = jnp.zeros((5, D), F32)

    row = pl.BlockSpec((tm, D), lambda i: (i, 0))
    return _hosted_call(
        body, comm, (dh1, x, dx1, mod, g_mix), name="in_bwd_tail", grid=(nsteps,),
        in_specs=[row, row, row, pl.BlockSpec((8, D), lambda i: (0, 0)), pl.BlockSpec((1, D), lambda i: (0, 0))],
        out_specs=[row, pl.BlockSpec((8, D), lambda i: (0, 0))],
        out_shape=[_sds((S, D), F32), _sds((8, D), F32)],
        scratch_shapes=[pltpu.VMEM((3, 8, D), F32)])


def _lay_pieces():
    out = []
    qa, ka, va, fa, qb, kb, vb, ga = 0, 512, 1024, 1536, 1544, 2312, 3080, 3848
    for p in range(N_DIL_PAIRS):
        for g in range(3):
            base = LAY_B + (p * 3 + g) * 384
            hd0 = (4 * g + 2 * p) * HD
            out += [(base, qb + hd0, LANES), (base + LANES, kb + hd0, LANES), (base + 2 * LANES, vb + hd0, LANES)]
    for p in range(N_FOX_PAIRS):
        base = LAY_A + p * 384
        out += [(base, qa + p * LANES, LANES), (base + LANES, ka + p * LANES, LANES), (base + 2 * LANES, va + p * LANES, LANES)]
    out.append((LAY_F, fa, 8))
    out.append((LAY_G, ga, 2 * D))
    return out


def _shard_runs():
    runs = []
    for lay, nat, width in _lay_pieces():
        while width:
            k, loc = nat // IN_SHARD, nat % IN_SHARD
            w = min(width, IN_SHARD - loc)
            runs.append((lay, k, loc, w))
            lay, nat, width = lay + w, nat + w, width - w
    return runs


def lay_from_shards(g):
    tm = 256

    def body(g_ref, o_ref):
        o_ref[:, LAY_F:LAY_G] = jnp.zeros((tm, LAY_G - LAY_F), g.dtype)
        for lay, k, loc, w in _shard_runs():
            o_ref[:, lay:lay + w] = g_ref[k, :, loc:loc + w]

    return pl.pallas_call(
        body, name="lay_from_shards", grid=(D // tm,),
        in_specs=[pl.BlockSpec((4, tm, IN_SHARD_PAD), lambda i: (0, i, 0))],
        out_specs=pl.BlockSpec((tm, LAY_N), lambda i: (i, 0)),
        out_shape=_sds((D, LAY_N), g.dtype), compiler_params=_params(VMEM_MB),
    )(g)


def shards_from_lay(dw_lay):
    tm = 256

    def body(x_ref, o_ref):
        o_ref[:, :, IN_SHARD:] = jnp.zeros((4, tm, IN_SHARD_PAD - IN_SHARD), F32)
        for lay, k, loc, w in _shard_runs():
            o_ref[k, :, loc:loc + w] = x_ref[:, lay:lay + w]

    return pl.pallas_call(
        body, name="shards_from_lay", grid=(D // tm,),
        in_specs=[pl.BlockSpec((tm, LAY_N), lambda i: (i, 0))],
        out_specs=pl.BlockSpec((4, tm, IN_SHARD_PAD), lambda i: (0, i, 0)),
        out_shape=_sds((4, D, IN_SHARD_PAD), F32), compiler_params=_params(VMEM_MB),
    )(dw_lay)


def _pos():
    return lax.axis_index("x"), lax.axis_index("y"), lax.axis_index("c")


def _other_chips(x, y):
    return [(1 - x, y), (x, 1 - y), (1 - x, 1 - y)]


def _remote(src, dst, send_sem, recv_sem, dev):
    return pltpu.make_async_remote_copy(src_ref=src, dst_ref=dst, send_sem=send_sem, recv_sem=recv_sem,
                                        device_id=dev, device_id_type=MESH)


VMEM_SPEC = pl.BlockSpec(memory_space=pltpu.VMEM)
ANY_SPEC = pl.BlockSpec(memory_space=pl.ANY)


def gather_all(v, name, with_sum):
    r = v.shape[0]

    def body(v_ref, out_ref, *rest):
        send_s, recv_s = rest[-2:]
        x, y, c = _pos()
        me = 4 * x + 2 * y + c
        out_ref[me] = v_ref[...]
        peers = []
        for m in range(1, 8):
            px = 1 - x if m & 4 else x
            py = 1 - y if m & 2 else y
            pc = 1 - c if m & 1 else c
            peers.append((px, py, pc))
        copies = [_remote(v_ref, out_ref.at[me], send_s.at[i], recv_s.at[i], dev) for i, dev in enumerate(peers)]
        for cp in copies:
            cp.start()
        for i, (px, py, pc) in enumerate(peers):
            _remote(v_ref, out_ref.at[4 * px + 2 * py + pc], send_s.at[i], recv_s.at[i], (px, py, pc)).wait_recv()
        for cp in copies:
            cp.wait_send()
        if with_sum:
            acc = out_ref[0]
            for b in range(1, 8):
                acc = acc + out_ref[b]
            rest[0][...] = acc

    out_shape = [_sds((8, r, LANES), F32)] + ([_sds((r, LANES), F32)] if with_sum else [])
    return pl.pallas_call(
        body, name=name, in_specs=[VMEM_SPEC], out_specs=[VMEM_SPEC] * len(out_shape), out_shape=out_shape,
        scratch_shapes=[pltpu.SemaphoreType.DMA((7,)), pltpu.SemaphoreType.DMA((7,))],
    )(v)


def mod_exchange(c_all, w_ada_sh, b_sh):
    def body(c_ref, w_ref, b_ref, out_ref, sc_ref, modp, send_s, recv_s):
        cv = c_ref[...]
        sc = cv * _sigmoid(cv)
        sc_ref[...] = sc
        modp[...] = jnp.dot(sc, w_ref[...], precision=lax.Precision.HIGHEST, preferred_element_type=F32) + b_ref[...]
        x, y, c = _pos()
        k = 2 * x + y
        out_ref[k] = modp[...]
        chips = _other_chips(x, y)
        copies = [_remote(modp, out_ref.at[k], send_s.at[j], recv_s.at[j], (cx, cy, c)) for j, (cx, cy) in enumerate(chips)]
        for cp in copies:
            cp.start()
        for j, (cx, cy) in enumerate(chips):
            _remote(modp, out_ref.at[2 * cx + cy], send_s.at[j], recv_s.at[j], (cx, cy, c)).wait_recv()
        for cp in copies:
            cp.wait_send()

    n = w_ada_sh.shape[1]
    return pl.pallas_call(
        body, name="mod_exchange", in_specs=[VMEM_SPEC] * 3, out_specs=[VMEM_SPEC] * 2,
        out_shape=[_sds((4, 8, n), F32), _sds((8, D), F32)],
        scratch_shapes=[pltpu.VMEM((8, n), F32), pltpu.SemaphoreType.DMA((3,)), pltpu.SemaphoreType.DMA((3,))],
        compiler_params=_params(VMEM_MB),
    )(c_all, w_ada_sh, b_sh)


def cast_into_slabs(ws, sizes, chip, comm):
    tr = 64
    n = len(ws)
    n_in = [w.shape[0] // tr for w in ws]
    n_out = [r // tr for r, _ in sizes]
    steps = max(n_out)
    nci, nco = len(comm.ins), len(comm.out_shapes)

    def body(chip_ref, *refs):
        del chip_ref
        w_refs, cin = refs[:n], refs[n:n + nci]
        o_refs, cout = refs[n + nci:2 * n + nci], refs[2 * n + nci:2 * n + nci + nco]
        sems = refs[2 * n + nci + nco:]
        i = pl.program_id(0)

        @pl.when(i == 0)
        def _():
            comm.start(cin, cout, sems)

        for a in range(n):
            c0, cols = ws[a].shape[1], sizes[a][1]

            @pl.when(i < n_in[a])
            def _(a=a, c0=c0, cols=cols):
                o_refs[a][:, 0:c0] = w_refs[a][...].astype(BF16)
                if cols > c0:
                    o_refs[a][:, c0:] = jnp.zeros((tr, cols - c0), BF16)

            if n_out[a] > n_in[a]:
                @pl.when((i >= n_in[a]) & (i < n_out[a]))
                def _(a=a, cols=cols):
                    o_refs[a][...] = jnp.zeros((tr, cols), BF16)

        @pl.when(i == steps - 1)
        def _():
            comm.wait(cin, cout, sems)

    any_spec = pl.BlockSpec(memory_space=pl.ANY)
    grid_spec = pltpu.PrefetchScalarGridSpec(
        num_scalar_prefetch=1, grid=(steps,),
        in_specs=[pl.BlockSpec((tr, w.shape[1]), functools.partial(lambda i, k, last: (jnp.minimum(i, last), 0), last=n_in[a] - 1))
                  for a, w in enumerate(ws)] + [any_spec] * nci,
        out_specs=[pl.BlockSpec((None, tr, sizes[a][1]), functools.partial(lambda i, k, last: (k[0], jnp.minimum(i, last), 0), last=n_out[a] - 1))
                   for a in range(n)] + [any_spec] * nco,
        scratch_shapes=[pltpu.SemaphoreType.DMA((s,)) for s in comm.sems])
    res = pl.pallas_call(
        body, name="cast_into_slabs", grid_spec=grid_spec,
        out_shape=[_sds((4,) + tuple(sz), BF16) for sz in sizes] + comm.out_shapes,
        input_output_aliases={1 + n + ci: n + co for ci, co in comm.aliases.items()},
    )(chip, *ws, *comm.ins)
    return list(res[:n]), list(res[n:])


def cast_into_slab(w, rows, cols, chip, tag):
    r0, c0 = w.shape
    tr = 256 if (r0 % 256 == 0 and rows % 256 == 0) else 64
    n_in, n_out = r0 // tr, rows // tr

    def body(chip_ref, w_ref, o_ref):
        del chip_ref
        i = pl.program_id(0)

        @pl.when(i < n_in)
        def _():
            o_ref[:, 0:c0] = w_ref[...].astype(BF16)
            if cols > c0:
                o_ref[:, c0:] = jnp.zeros((tr, cols - c0), BF16)

        @pl.when(i >= n_in)
        def _():
            o_ref[...] = jnp.zeros((tr, cols), BF16)

    grid_spec = pltpu.PrefetchScalarGridSpec(
        num_scalar_prefetch=1, grid=(n_out,),
        in_specs=[pl.BlockSpec((tr, c0), lambda i, k: (jnp.minimum(i, n_in - 1), 0))],
        out_specs=pl.BlockSpec((None, tr, cols), lambda i, k: (k[0], i, 0)))
    return pl.pallas_call(body, name="cast_" + tag, grid_spec=grid_spec, out_shape=_sds((4, rows, cols), BF16))(chip, w)


def _row_tile(rows, cap=256):
    t = cap
    while rows % t or t % 8:
        t -= 8
    return t


def _comm_wait(sends, recvs, local=()):
    for cp in recvs:
        cp.wait_recv()
    for cp in sends:
        cp.wait_send()
    for cp in local:
        cp.wait()


def ag_ici(bufs):
    n = len(bufs)

    def copies(ins, outs, sems):
        send_s, recv_s = sems
        x, y, c = _pos()
        k = 2 * x + y
        sends, recvs = [], []
        for a in range(n):
            half = outs[a].shape[1] // 2
            rows = pl.ds(c * half, half)
            for j, (cx, cy) in enumerate(_other_chips(x, y)):
                sem = (send_s.at[3 * a + j], recv_s.at[3 * a + j], (cx, cy, c))
                sends.append(_remote(outs[a].at[k, rows], outs[a].at[k, rows], *sem))
                recvs.append(_remote(outs[a].at[k, rows], outs[a].at[2 * cx + cy, rows], *sem))
        return sends, recvs

    def start(ins, outs, sems):
        for cp in copies(ins, outs, sems)[0]:
            cp.start()

    def wait(ins, outs, sems):
        _comm_wait(*copies(ins, outs, sems))

    return Comm(bufs, [_sds(b.shape, b.dtype) for b in bufs], [3 * n, 3 * n], start, wait, aliases={a: a for a in range(n)})


def ag_d2d(bufs):
    n = len(bufs)

    def copies(ins, outs, sems):
        send_s, recv_s = sems
        x, y, c = _pos()
        sends, recvs = [], []
        for a in range(n):
            half = outs[a].shape[1] // 2
            rows, orows = pl.ds(c * half, half), pl.ds((1 - c) * half, half)
            for j, (cx, cy) in enumerate(_other_chips(x, y)):
                kj = 2 * cx + cy
                sem = (send_s.at[3 * a + j], recv_s.at[3 * a + j], (x, y, 1 - c))
                sends.append(_remote(outs[a].at[kj, rows], outs[a].at[kj, rows], *sem))
                recvs.append(_remote(outs[a].at[kj, orows], outs[a].at[kj, orows], *sem))
        return sends, recvs

    def start(ins, outs, sems):
        for cp in copies(ins, outs, sems)[0]:
            cp.start()

    def wait(ins, outs, sems):
        _comm_wait(*copies(ins, outs, sems))

    return Comm(bufs, [_sds(b.shape, b.dtype) for b in bufs], [3 * n, 3 * n], start, wait, aliases={a: a for a in range(n)})


def rs_a(grads):
    n = len(grads)

    def copies(ins, outs, sems):
        send_s, recv_s = sems
        x, y, c = _pos()
        cps = []
        for a in range(n):
            half = ins[a].shape[1] // 2
            cps.append(_remote(ins[a].at[:, pl.ds((1 - c) * half, half), :], outs[a], send_s.at[a], recv_s.at[a], (x, y, 1 - c)))
        return cps

    def start(ins, outs, sems):
        for cp in copies(ins, outs, sems):
            cp.start()

    def wait(ins, outs, sems):
        cps = copies(ins, outs, sems)
        _comm_wait(cps, cps)

    return Comm(grads, [_sds((4, g.shape[1] // 2, g.shape[2]), g.dtype) for g in grads], [n, n], start, wait)


def rs_b(pres):
    n = len(pres)

    def copies(ins, outs, sems):
        send_s, recv_s = sems
        x, y, c = _pos()
        cps = []
        for a in range(n):
            for j, (cx, cy) in enumerate(_other_chips(x, y)):
                cps.append(_remote(ins[a].at[2 * cx + cy], outs[a].at[j], send_s.at[3 * a + j], recv_s.at[3 * a + j], (cx, cy, c)))
        return cps

    def start(ins, outs, sems):
        for cp in copies(ins, outs, sems):
            cp.start()

    def wait(ins, outs, sems):
        cps = copies(ins, outs, sems)
        _comm_wait(cps, cps)

    return Comm(pres, [_sds((3,) + p_.shape[1:], p_.dtype) for p_ in pres], [3 * n, 3 * n], start, wait)


def rs_b_rows(pre, buf, lo, n):
    def copies(ins, outs, sems):
        send_s, recv_s = sems
        x, y, c = _pos()
        rows = pl.ds(lo, n)
        return [_remote(ins[0].at[2 * cx + cy, rows], outs[0].at[j, rows], send_s.at[j], recv_s.at[j], (cx, cy, c))
                for j, (cx, cy) in enumerate(_other_chips(x, y))]

    def start(ins, outs, sems):
        for cp in copies(ins, outs, sems):
            cp.start()

    def wait(ins, outs, sems):
        cps = copies(ins, outs, sems)
        _comm_wait(cps, cps)

    ins = [pre] if buf is None else [pre, buf]
    return Comm(ins, [_sds((3,) + pre.shape[1:], pre.dtype)], [3, 3], start, wait, aliases={} if buf is None else {1: 0})


def rs_c(reds):
    n = len(reds)

    def copies(ins, outs, sems):
        send_s, recv_s = sems
        x, y, c = _pos()
        sends, recvs = [], []
        for a in range(n):
            half = outs[a].shape[0] // 2
            rows, orows = pl.ds(c * half, half), pl.ds((1 - c) * half, half)
            sem = (send_s.at[a], recv_s.at[a], (x, y, 1 - c))
            sends.append(_remote(outs[a].at[rows], outs[a].at[rows], *sem))
            recvs.append(_remote(outs[a].at[orows], outs[a].at[orows], *sem))
        return sends, recvs

    def start(ins, outs, sems):
        for cp in copies(ins, outs, sems)[0]:
            cp.start()

    def wait(ins, outs, sems):
        _comm_wait(*copies(ins, outs, sems))

    return Comm(reds, [_sds(r_.shape, r_.dtype) for r_ in reds], [n, n], start, wait, aliases={a: a for a in range(n)})


def comm_join(*comms):
    ni = np.cumsum([0] + [len(c.ins) for c in comms])
    no = np.cumsum([0] + [len(c.out_shapes) for c in comms])
    ns = np.cumsum([0] + [len(c.sems) for c in comms])

    def parts(ins, outs, sems):
        return [(c, ins[ni[i]:ni[i + 1]], outs[no[i]:no[i + 1]], sems[ns[i]:ns[i + 1]]) for i, c in enumerate(comms)]

    def start(ins, outs, sems):
        for c, a, b, s in parts(ins, outs, sems):
            c.start(a, b, s)

    def wait(ins, outs, sems):
        for c, a, b, s in parts(ins, outs, sems):
            c.wait(a, b, s)

    aliases = {int(ni[i]) + k: int(no[i]) + v for i, c in enumerate(comms) for k, v in c.aliases.items()}
    return Comm(sum((c.ins for c in comms), []), sum((c.out_shapes for c in comms), []), sum((c.sems for c in comms), []),
                start, wait, aliases)


def comm_only(comm, name):
    nci, nco = len(comm.ins), len(comm.out_shapes)

    def body(*refs):
        ins, outs, sems = refs[:nci], refs[nci:nci + nco], refs[nci + nco:]
        comm.start(ins, outs, sems)
        comm.wait(ins, outs, sems)

    return pl.pallas_call(
        body, name=name, in_specs=[ANY_SPEC] * nci, out_specs=[ANY_SPEC] * nco, out_shape=comm.out_shapes,
        scratch_shapes=[pltpu.SemaphoreType.DMA((s,)) for s in comm.sems],
        input_output_aliases=comm.aliases,
    )(*comm.ins)


def rs_add_halves(g, other, core, name):
    _, r, cdim = g.shape
    half = r // 2
    tr = _row_tile(half, 256)
    nb = half // tr

    def body(core_ref, g_ref, o_ref, out_ref):
        del core_ref
        out_ref[...] = (g_ref[...] + o_ref[...]).astype(BF16)

    grid_spec = pltpu.PrefetchScalarGridSpec(
        num_scalar_prefetch=1, grid=(4, nb),
        in_specs=[pl.BlockSpec((None, tr, cdim), lambda k, i, cr: (k, cr[0] * nb + i, 0)),
                  pl.BlockSpec((None, tr, cdim), lambda k, i, cr: (k, i, 0))],
        out_specs=pl.BlockSpec((None, tr, cdim), lambda k, i, cr: (k, i, 0)))
    return pl.pallas_call(body, name=name, grid_spec=grid_spec, out_shape=_sds((4, half, cdim), BF16))(core, g, other)


def rs_add_slabs(t, pre, place, name):
    _, half, cdim = t.shape
    tr = _row_tile(half, 256)
    nb = half // tr

    def body(place_ref, own_ref, t_ref, out_ref):
        del place_ref
        out_ref[...] = ((own_ref[...].astype(F32) + t_ref[0].astype(F32)) + t_ref[1].astype(F32)) + t_ref[2].astype(F32)

    grid_spec = pltpu.PrefetchScalarGridSpec(
        num_scalar_prefetch=1, grid=(nb,),
        in_specs=[pl.BlockSpec((None, tr, cdim), lambda i, pr: (pr[0], i, 0)), pl.BlockSpec((3, tr, cdim), lambda i, pr: (0, i, 0))],
        out_specs=pl.BlockSpec((tr, cdim), lambda i, pr: (pr[1] * nb + i, 0)))
    return pl.pallas_call(body, name=name, grid_spec=grid_spec, out_shape=_sds((2 * half, cdim), F32))(place, pre, t)


def _adam_math(w, g, m, v):
    m = ADAM_B1 * m + (1.0 - ADAM_B1) * g
    v = ADAM_B2 * v + (1.0 - ADAM_B2) * (g * g)
    m_hat = m / (1.0 - ADAM_B1 ** ADAM_STEP)
    v_hat = v / (1.0 - ADAM_B2 ** ADAM_STEP)
    delta = -ADAM_LR * (m_hat / (jnp.sqrt(v_hat) + ADAM_EPS) + ADAM_WD * w)
    return delta, m, v


def adam(w, g, m, v, name, comm=None):
    r, cdim = w.shape
    tr = _row_tile(r) if r >= 8 else r

    def body(w_ref, g_ref, m_ref, v_ref, g_out, d_ref, nm_ref, nv_ref):
        gv = g_ref[:, :cdim]
        g_out[...] = gv
        d_ref[...], nm_ref[...], nv_ref[...] = _adam_math(w_ref[...], gv, m_ref[...], v_ref[...])

    blk = pl.BlockSpec((tr, cdim), lambda i: (i, 0))
    return _hosted_call(
        body, comm, (w, g, m, v), name=name, grid=(r // tr,),
        in_specs=[blk, pl.BlockSpec((tr, g.shape[1]), lambda i: (i, 0)), blk, blk],
        out_specs=[blk] * 4, out_shape=[_sds((r, cdim), F32)] * 4)


def adam_small(groups):
    n = len(groups)

    def body(*refs):
        ins, outs = refs[:4 * n], refs[4 * n:]
        for i in range(n):
            w_ref, m_ref, v_ref, g_ref = ins[4 * i:4 * i + 4]
            d_ref, nm_ref, nv_ref = outs[3 * i:3 * i + 3]
            d_ref[...], nm_ref[...], nv_ref[...] = _adam_math(w_ref[...], g_ref[...], m_ref[...], v_ref[...])

    flat = [t for grp in groups for t in grp]
    out = pl.pallas_call(
        body, name="adam_small", in_specs=[VMEM_SPEC] * (4 * n), out_specs=[VMEM_SPEC] * (3 * n),
        out_shape=[_sds(grp[0].shape, F32) for grp in groups for _ in range(3)],
    )(*flat)
    return [out[3 * i:3 * i + 3] for i in range(n)]


def adam_w_ada(sc_t, dmod_sh, w, m, v, comm=None):
    r, cdim = w.shape
    tr = 256

    def body(s_ref, d_ref, w_ref, m_ref, v_ref, g_ref, dl_ref, nm_ref, nv_ref):
        g = jnp.dot(s_ref[...], d_ref[...], precision=lax.Precision.HIGHEST, preferred_element_type=F32)
        g_ref[...] = g
        dl_ref[...], nm_ref[...], nv_ref[...] = _adam_math(w_ref[...], g, m_ref[...], v_ref[...])

    blk = pl.BlockSpec((tr, cdim), lambda i: (i, 0))
    return _hosted_call(
        body, comm, (sc_t, dmod_sh, w, m, v), name="adam_w_ada", grid=(r // tr,),
        in_specs=[pl.BlockSpec((tr, LANES), lambda i: (i, 0)), pl.BlockSpec((LANES, cdim), lambda i: (0, 0)), blk, blk, blk],
        out_specs=[blk] * 4, out_shape=[_sds((r, cdim), F32)] * 4)


SMALL_ROWS = 80


def kernel(x, c, w_ada, b_ada, g_mix, w_in, b_fgate, w_br_a, w_br_b, w_out, g_ffn, w_ffn_gate, w_ffn_up, w_ffn_down, g_final, loss_target, m_w_ada, m_b_ada, m_g_mix, m_w_in, m_b_fgate, m_w_br_a, m_w_br_b, m_w_out, m_g_ffn, m_w_ffn_gate, m_w_ffn_up, m_w_ffn_down, m_g_final, v_w_ada, v_b_ada, v_g_mix, v_w_in, v_b_fgate, v_w_br_a, v_w_br_b, v_w_out, v_g_ffn, v_w_ffn_gate, v_w_ffn_up, v_w_ffn_down, v_g_final):
    xi, yi, ci = _pos()
    chip = 2 * xi + yi
    seq = 4 * xi + 2 * yi + ci
    n_ada = w_ada.shape[2]

    c_all = gather_all(c.reshape(8, LANES), "gather_c", False)[0].reshape(8, D)
    b_sh = lax.dynamic_slice(b_ada, (0, chip * n_ada), (1, n_ada))
    mod_all, sc = mod_exchange(c_all, w_ada[0], b_sh)
    mod = lax.dynamic_index_in_dim(mod_all, seq, axis=1, keepdims=False).reshape(6, D)
    mod8 = jnp.pad(mod, ((0, 2), (0, 0)))

    core = ci.astype(jnp.int32).reshape(1)
    chip1 = chip.astype(jnp.int32).reshape(1)
    place = jnp.stack([chip, ci]).astype(jnp.int32)
    s_in = cast_into_slab(w_in[0], D, IN_SHARD_PAD, chip1, "w_in")
    (s_bra, s_brb, s_out, s_gate, s_up, s_down), (g_in,) = cast_into_slabs(
        [w_br_a[0], w_br_b[0], w_out[0], w_ffn_gate[0].T, w_ffn_up[0].T, w_ffn_down[0]],
        [(512, 256), (256, 256), (256, D), (FF_PAD, D), (FF_PAD, D), (FF_PAD, D)], chip1, ag_ici([s_in]))
    xs, tgt, g_fin = x[0], loss_target[0], g_final.reshape(1, D)

    def halves(gs, others, tag):
        return [rs_add_halves(g, o, core, f"rs_{tag}_halves_{i}") for i, (g, o) in enumerate(zip(gs, others))]

    def slab_sums(ts, pres, tag):
        return [rs_add_slabs(t, pre, place, f"rs_{tag}_slabs_{i}") for i, (t, pre) in enumerate(zip(ts, pres))]

    tabs = rope_tables()
    h1, (g_in,) = norm_mod_fwd(xs, g_mix, mod8, 0, 1, comm=ag_d2d([g_in]))
    w_lay = lay_from_shards(g_in)
    p, mix_w = in_proj_fwd(h1, w_lay, tabs, comm=ag_ici([s_bra, s_brb, s_out]))
    fraw, fcol = fgate_fwd(p, jnp.pad(b_fgate, ((0, 0), (0, LANES - 8))))
    (ya_att, gcol), res = fox_fwd(p, fcol, comm=comm_join(ag_d2d(mix_w), ag_ici([s_gate, s_up])))
    g_bra, g_brb, g_out = res[:3]
    (yb, lse_b), res = dil_fwd(p, comm=comm_join(ag_d2d(res[3:]), ag_ici([s_down])))
    w_gate, w_up = res[:2]
    w_bra = g_bra.transpose(1, 0, 2).reshape(512, D)
    w_brb = g_brb.transpose(1, 0, 2).reshape(256, D)
    w_o = g_out.reshape(D, D)
    (merged, ya, ybp), (g_down,) = merge_fwd(ya_att, yb, p, w_bra, w_brb, comm=ag_d2d(res[2:]))
    w_down = g_down.reshape(FFP, D)
    mix, x1, h2 = out_proj_fwd(merged, w_o, xs, mod8, g_ffn)
    a, u, z = ffn_up_fwd(h2, w_gate, w_up)
    dx2, dffn, dg_final, dga_f, loss_part = ffn_down_loss(z, w_down, x1, mod8, g_fin, tgt)

    da, du, dw_down = ffn_down_bwd(dffn, w_down, a, u, z)
    g_down = [dw_down.reshape(4, FF_PAD, D)]
    dh2a, oth = mm_nt(da, w_gate, "ffn_gate_dx", comm=rs_a(g_down))
    pre_down = halves(g_down, oth, "down")
    dh2b, _ = mm_nt(du, w_up, "ffn_up_dx")
    dw_gate, _ = mm_tn(h2, da, "ffn_gate_dw", shard_major=True)
    dw_up, _ = mm_tn(h2, du, "ffn_up_dw", shard_major=True)
    g_gu = [dw_gate, dw_up]
    (dx1, dp1, dya_att, dyb, cs_mid, dw_out, dw_bra, dw_brb), res = mid_bwd(
        dh2a, dh2b, x1, dx2, mix, mod8, g_ffn, p, ya, ybp, merged, ya_att, yb, w_o, w_bra, w_brb,
        comm=comm_join(rs_b(pre_down), rs_a(g_gu)))
    red_down = slab_sums(res[:1], pre_down, "down")
    pre_gu = halves(g_gu, res[1:], "gu")
    g_mix3 = [dw_bra.reshape(512, 4, 256).transpose(1, 0, 2), dw_brb.reshape(256, 4, 256).transpose(1, 0, 2), dw_out.reshape(4, 256, D)]
    (dp2, dfrow, dfcol), res = fox_bwd(p, dya_att, ya_att, gcol, fcol, dp1,
                                       comm=comm_join(rs_b(pre_gu), rs_c(red_down), rs_a(g_mix3)))
    red_gu = slab_sums(res[:2], pre_gu, "gu")
    r_down = res[2]
    pre_mix3 = halves(g_mix3, res[3:], "mix")
    dp3, db_fg = fgate_bwd(dfrow.reshape(8, S), dfcol, fraw, dp2)
    dp4, res = dil_bwd(p, dyb, yb, lse_b, tabs, dp3, comm=comm_join(rs_c(red_gu), rs_b(pre_mix3)))
    r_gate, r_up = res[:2]
    red_mix3 = slab_sums(res[2:], pre_mix3, "mix")
    dw_lay, (r_bra, r_brb, r_out) = mm_tn(h1, dp4, "in_proj_dw", comm=rs_c(red_mix3))
    g_in4 = [shards_from_lay(dw_lay)]
    dh1, oth = mm_nt(dp4, w_lay, "in_proj_dx", comm=rs_a(g_in4))
    (pre_in,) = halves(g_in4, oth, "in")
    qrows = pre_in.shape[1] // 4
    (dx, cs_in), (t_in,) = in_bwd_tail(dh1, xs, dx1, mod8, g_mix, comm=rs_b_rows(pre_in, None, 0, qrows))

    dmod = jnp.concatenate([cs_in[0:2], cs_mid[3:4], cs_mid[0:2], dga_f], axis=0)
    small = dict(dmod=dmod, dg_mix=cs_in[2:3], dg_ffn=cs_mid[2:3], dg_final=dg_final, db_fgate=db_fg[:, 0], loss=loss_part[0, 0])
    sv = jnp.concatenate([
        small["dmod"].reshape(48, LANES), small["dg_mix"].reshape(8, LANES), small["dg_ffn"].reshape(8, LANES),
        small["dg_final"].reshape(8, LANES), jnp.pad(small["db_fgate"], (0, LANES - 8)).reshape(1, LANES),
        jnp.broadcast_to(small["loss"], (1, LANES)), jnp.zeros((SMALL_ROWS - 74, LANES), F32)], axis=0)
    sv_all, sv_sum = gather_all(sv, "gather_small", True)
    loss = sv_sum[73, 0]
    g_small = dict(b_ada=sv_sum[0:48].reshape(1, 6 * D), g_mix=sv_sum[48:56].reshape(1, D), g_ffn=sv_sum[56:64].reshape(1, D),
                   g_final=sv_sum[64:72].reshape(D), b_fgate=sv_sum[72, 0:8].reshape(1, 8))

    dmod_all = lax.dynamic_slice(sv_all[:, 0:48, :].reshape(8, 6 * D), (0, chip * n_ada), (8, n_ada))
    (g_ada, d_ada, nm_ada, nv_ada), (t_in,) = adam_w_ada(
        jnp.pad(sc.T, ((0, 0), (0, LANES - 8))), jnp.pad(dmod_all, ((0, LANES - 8), (0, 0))), w_ada[0], m_w_ada[0], v_w_ada[0],
        comm=rs_b_rows(pre_in, t_in, qrows, 3 * qrows))

    big = dict(w_in=(w_in, m_w_in, v_w_in), w_br_a=(w_br_a, m_w_br_a, v_w_br_a), w_br_b=(w_br_b, m_w_br_b, v_w_br_b),
               w_out=(w_out, m_w_out, v_w_out), w_ffn_gate=(w_ffn_gate, m_w_ffn_gate, v_w_ffn_gate),
               w_ffn_up=(w_ffn_up, m_w_ffn_up, v_w_ffn_up), w_ffn_down=(w_ffn_down, m_w_ffn_down, v_w_ffn_down))
    gpad = dict(w_br_a=r_bra, w_br_b=r_brb, w_out=r_out, w_ffn_gate=r_gate, w_ffn_up=r_up, w_ffn_down=r_down)
    upd = {}
    for nm in ("w_ffn_gate", "w_ffn_up"):
        w, m, v = big[nm]
        upd[nm] = [t.T for t in adam(w[0].T, gpad[nm], m[0].T, v[0].T, "adam_" + nm)[0]]
    (gpad["w_in"],) = comm_only(rs_c(slab_sums([t_in], [pre_in], "in")), "rs_in_share")
    for nm, (w, m, v) in big.items():
        if nm not in upd:
            upd[nm] = adam(w[0], gpad[nm], m[0], v[0], "adam_" + nm)[0]

    small_names = ["g_mix", "g_ffn", "g_final", "b_ada", "b_fgate"]
    small_w = dict(g_mix=(g_mix, m_g_mix, v_g_mix), g_ffn=(g_ffn, m_g_ffn, v_g_ffn), g_final=(g_final, m_g_final, v_g_final),
                   b_ada=(b_ada, m_b_ada, v_b_ada), b_fgate=(b_fgate, m_b_fgate, v_b_fgate))
    row = lambda t: t.reshape(1, -1)
    res = adam_small([[row(t) for t in small_w[nm]] + [row(g_small[nm])] for nm in small_names])
    small_upd = [{nm: res[i][which].reshape(small_w[nm][0].shape) for i, nm in enumerate(small_names)} for which in range(3)]
    order =["w_ada", "b_ada", "g_mix", "w_in", "b_fgate", "w_br_a", "w_br_b", "w_out", "g_ffn", "w_ffn_gate", "w_ffn_up", "w_ffn_down", "g_final"]

    def leaf(nm, which):
        if nm == "w_ada":
            return (g_ada, d_ada, nm_ada, nv_ada)[which][None]
        if nm in big:
            return upd[nm][which][None]
        return g_small[nm] if which == 0 else small_upd[which - 1][nm]

    outs = [loss, dx[None]]
    for which in range(4):
        outs += [leaf(nm, which) for nm in order]
    return tuple(outs)
```
